```python
import math
import jax, jax.numpy as jnp
from jax import lax
import numpy as np

D_MODEL = 1024
BATCH = 32
SEQ = 2048
DEPTH = 1

CTX_LEN = 256
GRID_W = 64
S5_WIDTH = D_MODEL // 4
S5_CH_PER_GROUP = 16
S5_GROUPS = S5_WIDTH // S5_CH_PER_GROUP
S5_STATE = 64
CONV_HEAD_DIM = 64
CONV_WIDTH = D_MODEL - S5_WIDTH
CONV_HEADS = CONV_WIDTH // CONV_HEAD_DIM
CONV_ROW_WIDTH = CONV_WIDTH // 2
CONV_TAPS = 3
D_MIX = S5_WIDTH + CONV_WIDTH
D_IN_PROJ = S5_WIDTH + 3 * CONV_WIDTH
N_EXPERT_GROUPS = 4
EXPERTS_PER_GROUP = 4
N_EXPERTS = N_EXPERT_GROUPS * EXPERTS_PER_GROUP
TOP_K_IN_GROUP = 2
D_EXPERT = D_MODEL // 2
N_MOD = 6
RMS_EPS = 1e-6
DT_MIN = 1e-3
DT_MAX = 1e-1

kernel_name = 'hybrid_s5_shortconv_hmoe_prefix_dit'


def rmsnorm(x, g):
    xf = x.astype(jnp.float32)
    y = xf * lax.rsqrt(jnp.mean(xf * xf, axis=-1, keepdims=True) + RMS_EPS)
    return (y * g.astype(jnp.float32)).astype(x.dtype)


def ada_mod(cond, w_mod, b_mod):
    m = jax.nn.silu(cond) @ w_mod + b_mod
    return jnp.split(m[:, None, :], N_MOD, axis=-1)


def modulate(x, g, shift, scale):
    return rmsnorm(x, g) * (1.0 + scale) + shift


def s5_discretize(lam_re, lam_im, log_dt, b_re, b_im):
    lam = lax.complex(lam_re.astype(jnp.float32), lam_im.astype(jnp.float32))
    dt = jnp.exp(log_dt.astype(jnp.float32))[:, None]
    a_bar = jnp.exp(lam * dt)
    b = lax.complex(b_re.astype(jnp.float32), b_im.astype(jnp.float32))
    b_bar = ((a_bar - 1.0) / lam)[..., None] * b
    return a_bar, b_bar


def _linear_recurrence(e1, e2):
    a1, b1 = e1
    a2, b2 = e2
    return a1 * a2, a2 * b1 + b2


def s5_states(u, a_bar, b_bar, h0, reverse):
    bu = jnp.einsum('blgh,gph->blgp', u.astype(jnp.complex64), b_bar)
    if h0 is not None:
        first = -1 if reverse else 0
        bu = bu.at[:, first].add(a_bar * h0)
    a = jnp.broadcast_to(a_bar, (1, bu.shape[1]) + a_bar.shape)
    _, states = lax.associative_scan(_linear_recurrence, (a, bu), reverse=reverse, axis=1)
    return states


def s5_readout(u, st_f, st_b, c_f, c_b, d, w_glu, b_glu, dtype):
    bsz, n = u.shape[0], u.shape[1]
    y = (jnp.einsum('blgp,ghp->blgh', st_f, c_f).real
         + jnp.einsum('blgp,ghp->blgh', st_b, c_b).real
         + d.astype(jnp.float32).reshape(S5_GROUPS, S5_CH_PER_GROUP) * u)
    g = jax.nn.gelu(y.reshape(bsz, n, S5_WIDTH).astype(dtype))
    return g * jax.nn.sigmoid(g @ w_glu + b_glu)


def dwconv3(x, w, axis):
    n = x.shape[axis]
    pad = [(0, 0)] * x.ndim
    pad[axis] = (1, 1)
    xp = jnp.pad(x, pad)
    return (lax.slice_in_dim(xp, 0, n, axis=axis) * w[0]
            + lax.slice_in_dim(xp, 1, n + 1, axis=axis) * w[1]
            + lax.slice_in_dim(xp, 2, n + 2, axis=axis) * w[2])


def short_conv_latent(z, conv_w):
    b_g, c_g, v = jnp.split(z, 3, axis=-1)
    cv = c_g * v
    bsz, n, _ = cv.shape
    rows = n // GRID_W
    grid = cv.reshape(bsz, rows, GRID_W, CONV_WIDTH)
    row_part = dwconv3(grid[..., :CONV_ROW_WIDTH], conv_w[:, :CONV_ROW_WIDTH], axis=2)
    col_part = dwconv3(grid[..., CONV_ROW_WIDTH:], conv_w[:, CONV_ROW_WIDTH:], axis=1)
    conv = jnp.concatenate([row_part, col_part], axis=-1).reshape(bsz, n, CONV_WIDTH)
    return b_g * conv


def short_conv_context(z, conv_w):
    b_g, c_g, v = jnp.split(z, 3, axis=-1)
    return b_g * dwconv3(c_g * v, conv_w, axis=1)


def hier_moe(h, wg, bg, we, be, w1, w3, w2):
    shp = h.shape
    t = h.reshape(-1, shp[-1])
    g_prob = jax.nn.softmax((t @ wg + bg).astype(jnp.float32), axis=-1)
    g_p, g_idx = lax.top_k(g_prob, 1)
    e_logits = (t @ we + be).astype(jnp.float32).reshape(-1, N_EXPERT_GROUPS, EXPERTS_PER_GROUP)
    e_in = jnp.take_along_axis(e_logits, g_idx[:, :, None], axis=1)[:, 0]
    top_v, top_i = lax.top_k(e_in, TOP_K_IN_GROUP)
    weights = jax.nn.softmax(top_v, axis=-1) * g_p
    e_idx = g_idx * EXPERTS_PER_GROUP + top_i
    gates = jnp.einsum('tk,tke->te', weights,
                       jax.nn.one_hot(e_idx, N_EXPERTS, dtype=jnp.float32)).astype(h.dtype)
    out = jnp.zeros_like(t)
    for e in range(N_EXPERTS):
        he = jax.nn.silu(t @ w1[e]) * (t @ w3[e])
        out = out + gates[:, e:e + 1] * (he @ w2[e])
    return out.reshape(shp)


def setup_inputs(seed: int = 0) -> dict:
    key = jax.random.key(seed)
    ks = jax.random.split(key, 32)
    f32 = jnp.float32
    nrm = lambda k, shape, s: jax.random.normal(k, shape, f32) * s
    G, P, H = S5_GROUPS, S5_STATE, S5_CH_PER_GROUP
    lam_im = jnp.broadcast_to(jnp.pi * jnp.arange(P, dtype=f32), (DEPTH, 2, G, P))
    return {
        'x': nrm(ks[0], (BATCH, SEQ, D_MODEL), 1.0),
        'c': nrm(ks[1], (BATCH, D_MODEL), 1.0),
        'ctx': nrm(ks[2], (BATCH, CTX_LEN, D_MODEL), 1.0),
        'c_ctx': nrm(ks[3], (D_MODEL,), 1.0),
        'w_mod': nrm(ks[4], (DEPTH, D_MODEL, N_MOD * D_MODEL), 0.5 * D_MODEL ** -0.5),
        'b_mod': nrm(ks[5], (DEPTH, N_MOD * D_MODEL), 0.02),
        'norm1_g': 1.0 + nrm(ks[6], (DEPTH, D_MODEL), 0.05),
        'norm2_g': 1.0 + nrm(ks[7], (DEPTH, D_MODEL), 0.05),
        'w_in': nrm(ks[8], (DEPTH, D_MODEL, D_IN_PROJ), D_MODEL ** -0.5),
        's5_lambda_re': -0.5 + nrm(ks[9], (DEPTH, 2, G, P), 0.01),
        's5_lambda_im': lam_im + nrm(ks[10], (DEPTH, 2, G, P), 0.01),
        's5_log_dt': jax.random.uniform(ks[11], (DEPTH, 2, G), f32, math.log(DT_MIN), math.log(DT_MAX)),
        's5_b_re': nrm(ks[12], (DEPTH, 2, G, P, H), (2 * H) ** -0.5),
        's5_b_im': nrm(ks[13], (DEPTH, 2, G, P, H), (2 * H) ** -0.5),
        's5_c_re': nrm(ks[14], (DEPTH, 2, G, H, P), P ** -0.5),
        's5_c_im': nrm(ks[15], (DEPTH, 2, G, H, P), P ** -0.5),
        's5_d': nrm(ks[16], (DEPTH, S5_WIDTH), 1.0),
        'w_glu': nrm(ks[17], (DEPTH, S5_WIDTH, S5_WIDTH), S5_WIDTH ** -0.5),
        'b_glu': nrm(ks[18], (DEPTH, S5_WIDTH), 0.02),
        'conv_w': nrm(ks[19], (DEPTH, CONV_TAPS, CONV_WIDTH), CONV_TAPS ** -0.5),
        'w_out': nrm(ks[20], (DEPTH, D_MIX, D_MODEL), D_MIX ** -0.5),
        'router_group_w': nrm(ks[21], (DEPTH, D_MODEL, N_EXPERT_GROUPS), D_MODEL ** -0.5),
        'router_group_b': nrm(ks[22], (DEPTH, N_EXPERT_GROUPS), 0.01),
        'router_expert_w': nrm(ks[23], (DEPTH, D_MODEL, N_EXPERTS), D_MODEL ** -0.5),
        'router_expert_b': nrm(ks[24], (DEPTH, N_EXPERTS), 0.01),
        'expert_w1': nrm(ks[25], (DEPTH, N_EXPERTS, D_MODEL, D_EXPERT), D_MODEL ** -0.5),
        'expert_w3': nrm(ks[26], (DEPTH, N_EXPERTS, D_MODEL, D_EXPERT), D_MODEL ** -0.5),
        'expert_w2': nrm(ks[27], (DEPTH, N_EXPERTS, D_EXPERT, D_MODEL), D_EXPERT ** -0.5),
        'final_g': 1.0 + nrm(ks[28], (D_MODEL,), 0.05),
    }


def reference(x, c, ctx, c_ctx, w_mod, b_mod, norm1_g, norm2_g, w_in, s5_lambda_re, s5_lambda_im,
              s5_log_dt, s5_b_re, s5_b_im, s5_c_re, s5_c_im, s5_d, w_glu, b_glu, conv_w, w_out,
              router_group_w, router_group_b, router_expert_w, router_expert_b,
              expert_w1, expert_w3, expert_w2, final_g):
    f32 = jnp.float32
    bsz, n_tok, _ = x.shape
    n_ctx = ctx.shape[1]
    for l in range(DEPTH):
        last = l == DEPTH - 1
        mx = ada_mod(c, w_mod[l], b_mod[l])
        mc = ada_mod(c_ctx[None, :], w_mod[l], b_mod[l])
        a_f, bb_f = s5_discretize(s5_lambda_re[l, 0], s5_lambda_im[l, 0], s5_log_dt[l, 0],
                                  s5_b_re[l, 0], s5_b_im[l, 0])
        a_b, bb_b = s5_discretize(s5_lambda_re[l, 1], s5_lambda_im[l, 1], s5_log_dt[l, 1],
                                  s5_b_re[l, 1], s5_b_im[l, 1])
        cm_f = lax.complex(s5_c_re[l, 0].astype(f32), s5_c_im[l, 0].astype(f32))
        cm_b = lax.complex(s5_c_re[l, 1].astype(f32), s5_c_im[l, 1].astype(f32))

        hc = modulate(ctx, norm1_g[l], mc[0], mc[1])
        uc = (hc @ w_in[l][:, :S5_WIDTH]).astype(f32).reshape(bsz, n_ctx, S5_GROUPS, S5_CH_PER_GROUP)
        sc_f = s5_states(uc, a_f, bb_f, None, False)
        sc_b = s5_states(uc, a_b, bb_b, None, True)

        hx = modulate(x, norm1_g[l], mx[0], mx[1])
        zx = hx @ w_in[l]
        ux = zx[..., :S5_WIDTH].astype(f32).reshape(bsz, n_tok, S5_GROUPS, S5_CH_PER_GROUP)
        sx_f = s5_states(ux, a_f, bb_f, sc_f[:, -1], False)
        sx_b = s5_states(ux, a_b, bb_b, sc_b[:, 0], True)
        y_s5 = s5_readout(ux, sx_f, sx_b, cm_f, cm_b, s5_d[l], w_glu[l], b_glu[l], x.dtype)
        y_conv = short_conv_latent(zx[..., S5_WIDTH:], conv_w[l])
        yx = jnp.concatenate([y_s5, y_conv], axis=-1) @ w_out[l]
        x = x + mx[2] * yx
        x = x + mx[5] * hier_moe(modulate(x, norm2_g[l], mx[3], mx[4]), router_group_w[l],
                                 router_group_b[l], router_expert_w[l], router_expert_b[l],
                                 expert_w1[l], expert_w3[l], expert_w2[l])

        if not last:
            yc_s5 = s5_readout(uc, sc_f, sc_b, cm_f, cm_b, s5_d[l], w_glu[l], b_glu[l], ctx.dtype)
            yc_conv = short_conv_context(hc @ w_in[l][:, S5_WIDTH:], conv_w[l])
            ctx = ctx + mc[2] * (jnp.concatenate([yc_s5, yc_conv], axis=-1) @ w_out[l])
            ctx = ctx + mc[5] * hier_moe(modulate(ctx, norm2_g[l], mc[3], mc[4]), router_group_w[l],
                                         router_group_b[l], router_expert_w[l], router_expert_b[l],
                                         expert_w1[l], expert_w3[l], expert_w2[l])
    return rmsnorm(x, final_g)
```

```python
import functools

import jax
import jax.numpy as jnp
from jax import lax
from jax.experimental import pallas as pl
from jax.experimental.pallas import tpu as pltpu

F32 = jnp.float32
BF16 = jnp.bfloat16

RMS_EPS = 1e-6
N_MOD = 6
GRID_W = 64
S5_WIDTH = 256
S5_H = 16
S5_P = 64
S5_GROUPS = S5_WIDTH // S5_H
S5_PAIRS = S5_GROUPS // 2
S5_CHUNK = 16
CONV_WIDTH = 768
CONV_ROW_WIDTH = CONV_WIDTH // 2
N_GROUPS = 4
EXPERTS_PER_GROUP = 4
N_EXPERTS = N_GROUPS * EXPERTS_PER_GROUP
ROUTER_LANES = 128
EXPERT_LANE0 = N_GROUPS
TOKEN_TILE = 512
MOE_TILE = 1024
VMEM_LIMIT = 52 * 1024 * 1024


def _params(sem, vmem=VMEM_LIMIT):
    return pltpu.CompilerParams(dimension_semantics=sem, vmem_limit_bytes=vmem)


def _mod_kernel(c_ref, w_ref, b_ref, o_ref):
    c = c_ref[...]
    o_ref[...] = jnp.dot(c * jax.nn.sigmoid(c), w_ref[...], preferred_element_type=F32) + b_ref[...]


def _mod_rows(cond, w_mod, b_mod):
    n, d = cond.shape
    nout = w_mod.shape[1]
    bn = d
    return pl.pallas_call(
        _mod_kernel,
        grid=(nout // bn,),
        in_specs=[pl.BlockSpec((n, d), lambda j: (0, 0)),
                  pl.BlockSpec((d, bn), lambda j: (0, j)),
                  pl.BlockSpec((1, bn), lambda j: (0, j))],
        out_specs=pl.BlockSpec((n, bn), lambda j: (0, j)),
        out_shape=jax.ShapeDtypeStruct((n, nout), F32),
        compiler_params=_params(("arbitrary",)),
        name="mod",
    )(cond, w_mod, b_mod.reshape(1, nout))


def _modulated_norm(x, g, shift, scale):
    ms = jnp.mean(x * x, axis=-1, keepdims=True)
    return (x * lax.rsqrt(ms + RMS_EPS) * g) * (1.0 + scale) + shift


def _inproj_kernel(x_ref, mod_ref, g_ref, w_ref, u_ref, *conv_refs):
    h = _modulated_norm(x_ref[0], g_ref[...], mod_ref[0, 0:1, :], mod_ref[0, 1:2, :]).astype(BF16)
    u_ref[0] = jnp.dot(h, w_ref[:, 0:S5_WIDTH], preferred_element_type=F32).astype(BF16)
    if conv_refs:
        bg_ref, cv_ref = conv_refs
        o = S5_WIDTH
        bg_ref[0] = jnp.dot(h, w_ref[:, o:o + CONV_WIDTH], preferred_element_type=F32).astype(BF16)
        c_g = jnp.dot(h, w_ref[:, o + CONV_WIDTH:o + 2 * CONV_WIDTH], preferred_element_type=F32)
        v = jnp.dot(h, w_ref[:, o + 2 * CONV_WIDTH:o + 3 * CONV_WIDTH], preferred_element_type=F32)
        cv_ref[0] = (c_g * v).astype(BF16)


def _inproj(x, mods, per_batch_mod, norm_g, w_bf16, tm, with_conv):
    bsz, n, d = x.shape
    nw = w_bf16.shape[1]
    mod_map = (lambda b, i: (b, 0, 0)) if per_batch_mod else (lambda b, i: (0, 0, 0))
    out_shape = [jax.ShapeDtypeStruct((bsz, n, S5_WIDTH), BF16)]
    out_specs = [pl.BlockSpec((1, tm, S5_WIDTH), lambda b, i: (b, i, 0))]
    if with_conv:
        out_shape += [jax.ShapeDtypeStruct((bsz, n, CONV_WIDTH), BF16)] * 2
        out_specs += [pl.BlockSpec((1, tm, CONV_WIDTH), lambda b, i: (b, i, 0))] * 2
    return pl.pallas_call(
        _inproj_kernel,
        grid=(bsz, n // tm),
        in_specs=[pl.BlockSpec((1, tm, d), lambda b, i: (b, i, 0)),
                  pl.BlockSpec((1, N_MOD, d), mod_map),
                  pl.BlockSpec((1, d), lambda b, i: (0, 0)),
                  pl.BlockSpec((d, nw), lambda b, i: (0, 0))],
        out_specs=out_specs,
        out_shape=out_shape,
        compiler_params=_params(("parallel", "parallel")),
        name="inproj_conv" if with_conv else "inproj_ctx",
    )(x, mods, norm_g.reshape(1, d), w_bf16)


def _s5_matrices(lam_re, lam_im, log_dt, b_re, b_im, c_re, c_im, d_skip):
    lc, g_n, p_n, h_n = S5_CHUNK, S5_GROUPS, S5_P, S5_H
    lam = lax.complex(lam_re.astype(F32), lam_im.astype(F32))
    dt = jnp.exp(log_dt.astype(F32))[..., None]
    a_bar = jnp.exp(lam * dt)
    b_bar = ((a_bar - 1.0) / lam)[..., None] * lax.complex(b_re.astype(F32), b_im.astype(F32))
    cm = lax.complex(c_re.astype(F32), c_im.astype(F32))
    steps = jnp.arange(lc + 1, dtype=F32)
    apow = jnp.exp((lam * dt)[:, :, None, :] * steps[None, None, :, None])
    kern = jnp.einsum('dgop,dgjp,dgpi->dgjoi', cm, apow[:, :, :lc], b_bar).real
    s_idx = jnp.arange(lc)[:, None]
    t_idx = jnp.arange(lc)[None, :]
    lag_f = jnp.clip(t_idx - s_idx, 0, lc - 1)
    lag_b = jnp.clip(s_idx - t_idx, 0, lc - 1)
    t_f = kern[0][:, lag_f] * (t_idx >= s_idx)[None, :, :, None, None]
    t_b = kern[1][:, lag_b] * (s_idx >= t_idx)[None, :, :, None, None]
    skip = (jnp.eye(lc, dtype=F32)[None, :, :, None, None]
            * (jnp.eye(h_n, dtype=F32) * d_skip.astype(F32).reshape(g_n, h_n, 1))[:, None, None, :, :])
    t_mat = (t_f + t_b + skip).transpose(0, 1, 4, 2, 3).reshape(g_n, lc * h_n, lc * h_n)

    def in_mat(pw, bb):
        return (pw[:, :, None, :] * bb.transpose(0, 2, 1)[:, None, :, :]).reshape(g_n, lc * h_n, p_n)

    mb_f = in_mat(apow[0, :, lc - 1::-1][:, :lc], b_bar[0])
    mb_b = in_mat(apow[1, :, :lc], b_bar[1])

    def out_mat(pw, cc):
        return (pw.transpose(0, 2, 1)[:, :, :, None] * cc.transpose(0, 2, 1)[:, :, None, :]).reshape(
            g_n, p_n, lc * h_n)

    mc_f = out_mat(apow[0, :, 1:lc + 1], cm[0])
    mc_b = out_mat(apow[1, :, lc:0:-1], cm[1])
    a_chunk = apow[:, :, lc]

    q_n = S5_PAIRS
    zeros_in = jnp.zeros((g_n, lc * h_n, p_n), F32)

    def pair_cols(m):
        m = m.reshape(q_n, 2, lc * h_n, p_n)
        z = zeros_in.reshape(q_n, 2, lc * h_n, p_n)[:, 0]
        top = jnp.concatenate([m[:, 0], z], axis=-1)
        bot = jnp.concatenate([z, m[:, 1]], axis=-1)
        return jnp.concatenate([top, bot], axis=1)

    mb_pair = jnp.concatenate([pair_cols(mb_f.real), pair_cols(mb_f.imag),
                               pair_cols(mb_b.real), pair_cols(mb_b.imag)], axis=-1)

    def pair_rows(m):
        m = m.reshape(q_n, 2, p_n, lc * h_n)
        z = jnp.zeros_like(m[:, 0])
        top = jnp.concatenate([m[:, 0], z], axis=-1)
        bot = jnp.concatenate([z, m[:, 1]], axis=-1)
        return jnp.concatenate([top, bot], axis=1)

    mc_pair = jnp.concatenate([pair_rows(mc_f.real), pair_rows(-mc_f.imag),
                               pair_rows(mc_b.real), pair_rows(-mc_b.imag)], axis=1)
    a_rows = jnp.stack([a_chunk[0].real, a_chunk[0].imag, a_chunk[1].real, a_chunk[1].imag], axis=0)
    a_rows = a_rows.reshape(4, q_n, 2 * p_n).transpose(1, 0, 2)
    a_rows = jnp.concatenate([a_rows, jnp.zeros_like(a_rows)], axis=1)
    return t_mat.astype(BF16), mb_pair.astype(BF16), mc_pair.astype(BF16), a_rows


def _s5_kernel(u_ref, uc_ref, t_ref, mb_ref, mc_ref, a_ref, y_ref, s_lat, s_ctx, h_scr, *, bsz, rb):
    n_lat = u_ref.shape[1] // bsz
    n_ctx = uc_ref.shape[1] // bsz
    sw = 2 * S5_P

    mb = mb_ref[0]

    def in_lat(i, carry):
        r = pl.multiple_of(i * rb, rb)
        s_lat[pl.ds(r, rb), :] = jnp.dot(u_ref[0, pl.ds(r, rb), :], mb, preferred_element_type=F32)
        return carry

    lax.fori_loop(0, u_ref.shape[1] // rb, in_lat, 0)
    s_ctx[...] = jnp.dot(uc_ref[0], mb, preferred_element_type=F32)

    a_fr, a_fi, a_br, a_bi = (a_ref[0, k:k + 1, :] for k in range(4))

    def step(h, a_r, a_i, s_r, s_i):
        h_r, h_i = h
        return a_r * h_r - a_i * h_i + s_r, a_r * h_i + a_i * h_r + s_i

    def ctx_step(k, carry):
        hf, hb = carry
        rf = pl.multiple_of(k * bsz, bsz)
        rbk = pl.multiple_of((n_ctx - 1 - k) * bsz, bsz)
        hf = step(hf, a_fr, a_fi, s_ctx[pl.ds(rf, bsz), 0:sw], s_ctx[pl.ds(rf, bsz), sw:2 * sw])
        hb = step(hb, a_br, a_bi, s_ctx[pl.ds(rbk, bsz), 2 * sw:3 * sw], s_ctx[pl.ds(rbk, bsz), 3 * sw:4 * sw])
        return hf, hb

    zero = jnp.zeros((bsz, sw), F32)
    carry = lax.fori_loop(0, n_ctx, ctx_step, ((zero, zero), (zero, zero)))

    def lat_step(k, carry):
        hf, hb = carry
        rf = pl.multiple_of(k * bsz, bsz)
        rbk = pl.multiple_of((n_lat - 1 - k) * bsz, bsz)
        h_scr[pl.ds(rf, bsz), 0:sw] = hf[0].astype(BF16)
        h_scr[pl.ds(rf, bsz), sw:2 * sw] = hf[1].astype(BF16)
        h_scr[pl.ds(rbk, bsz), 2 * sw:3 * sw] = hb[0].astype(BF16)
        h_scr[pl.ds(rbk, bsz), 3 * sw:4 * sw] = hb[1].astype(BF16)
        hf = step(hf, a_fr, a_fi, s_lat[pl.ds(rf, bsz), 0:sw], s_lat[pl.ds(rf, bsz), sw:2 * sw])
        hb = step(hb, a_br, a_bi, s_lat[pl.ds(rbk, bsz), 2 * sw:3 * sw], s_lat[pl.ds(rbk, bsz), 3 * sw:4 * sw])
        return hf, hb

    lax.fori_loop(0, n_lat, lat_step, carry)

    t0 = t_ref[0]
    t1 = t_ref[1]
    mc = mc_ref[0]
    half = S5_CHUNK * S5_H

    def out_lat(i, carry):
        r = pl.multiple_of(i * rb, rb)
        u = u_ref[0, pl.ds(r, rb), :]
        inter = jnp.dot(h_scr[pl.ds(r, rb), :], mc, preferred_element_type=F32)
        y0 = jnp.dot(u[:, :half], t0, preferred_element_type=F32) + inter[:, :half]
        y1 = jnp.dot(u[:, half:], t1, preferred_element_type=F32) + inter[:, half:]
        y_ref[0, pl.ds(r, rb), 0:half] = y0.astype(BF16)
        y_ref[0, pl.ds(r, rb), half:2 * half] = y1.astype(BF16)
        return carry

    lax.fori_loop(0, u_ref.shape[1] // rb, out_lat, 0)


def _s5_scan(u_t, uc_t, t_mat, mb_pair, mc_pair, a_rows, bsz):
    q_n, r_lat, w = u_t.shape
    r_ctx = uc_t.shape[1]
    rb = min(512, r_lat)
    return pl.pallas_call(
        functools.partial(_s5_kernel, bsz=bsz, rb=rb),
        grid=(q_n,),
        in_specs=[pl.BlockSpec((1, r_lat, w), lambda q: (q, 0, 0)),
                  pl.BlockSpec((1, r_ctx, w), lambda q: (q, 0, 0)),
                  pl.BlockSpec((2, w // 2, w // 2), lambda q: (q, 0, 0)),
                  pl.BlockSpec((1, w, w), lambda q: (q, 0, 0)),
                  pl.BlockSpec((1, w, w), lambda q: (q, 0, 0)),
                  pl.BlockSpec((1, 8, 2 * S5_P), lambda q: (q, 0, 0))],
        out_specs=pl.BlockSpec((1, r_lat, w), lambda q: (q, 0, 0)),
        out_shape=jax.ShapeDtypeStruct((q_n, r_lat, w), BF16),
        scratch_shapes=[pltpu.VMEM((r_lat, w), F32), pltpu.VMEM((r_ctx, w), F32), pltpu.VMEM((r_lat, w), BF16)],
        compiler_params=_params(("parallel",)),
        name="s5_scan",
    )(u_t, uc_t, t_mat, mb_pair, mc_pair, a_rows)


def _to_chunk_major(u, bsz):
    n = u.shape[1]
    nc = n // S5_CHUNK
    u = u.reshape(bsz, nc, S5_CHUNK, S5_PAIRS, 2, S5_H).transpose(3, 1, 0, 4, 2, 5)
    return u.reshape(S5_PAIRS, nc * bsz, 2 * S5_CHUNK * S5_H)


def _from_chunk_major(y, bsz):
    nc = y.shape[1] // bsz
    y = y.reshape(S5_PAIRS, nc, bsz, 2, S5_CHUNK, S5_H).transpose(2, 1, 4, 0, 3, 5)
    return y.reshape(bsz, nc * S5_CHUNK, S5_WIDTH)


def _route(logits):
    lane = lax.broadcasted_iota(jnp.int32, logits.shape, 1).astype(F32)
    neg = jnp.float32(-jnp.inf)
    big = jnp.float32(ROUTER_LANES)
    in_groups = lane < N_GROUPS
    lg = jnp.where(in_groups, logits, neg)
    g_max = jnp.max(lg, axis=-1, keepdims=True)
    g_sum = jnp.sum(jnp.where(in_groups, jnp.exp(logits - g_max), 0.0), axis=-1, keepdims=True)
    g_p = 1.0 / g_sum
    g_idx = jnp.min(jnp.where(lg == g_max, lane, big), axis=-1, keepdims=True)
    lo = EXPERT_LANE0 + EXPERTS_PER_GROUP * g_idx
    in_sel = jnp.logical_and(lane >= lo, lane < lo + EXPERTS_PER_GROUP)
    le = jnp.where(in_sel, logits, neg)
    v1 = jnp.max(le, axis=-1, keepdims=True)
    i1 = jnp.min(jnp.where(le == v1, lane, big), axis=-1, keepdims=True)
    le2 = jnp.where(lane == i1, neg, le)
    v2 = jnp.max(le2, axis=-1, keepdims=True)
    i2 = jnp.min(jnp.where(le2 == v2, lane, big), axis=-1, keepdims=True)
    e21 = jnp.exp(v2 - v1)
    w1 = g_p / (1.0 + e21)
    w2 = w1 * e21
    return jnp.where(lane == i1, w1, 0.0) + jnp.where(lane == i2, w2, 0.0)


def _mix_kernel(x_ref, y_ref, bg_ref, cv_ref, cvp_ref, cvn_ref, mod_ref, g2_ref, cw_ref, wglu_ref, bglu_ref,
                wout_ref, wr_ref, br_ref, x1_ref, h2_ref, gates_ref):
    i = pl.program_id(1)
    tm = x_ref.shape[1]
    g = jax.nn.gelu(y_ref[0].astype(F32))
    glu = g * jax.nn.sigmoid(jnp.dot(g.astype(BF16), wglu_ref[...], preferred_element_type=F32) + bglu_ref[...])
    cv = cv_ref[0].astype(F32)
    row = lax.broadcasted_iota(jnp.int32, (tm, 1), 0)
    col_in_row = row % GRID_W
    cr = cv[:, :CONV_ROW_WIDTH]
    left = jnp.where(col_in_row == 0, 0.0, pltpu.roll(cr, 1, axis=0))
    right = jnp.where(col_in_row == GRID_W - 1, 0.0, pltpu.roll(cr, tm - 1, axis=0))
    w_r = cw_ref[:, :CONV_ROW_WIDTH]
    row_part = left * w_r[0:1] + cr * w_r[1:2] + right * w_r[2:3]
    cc = cv[:, CONV_ROW_WIDTH:]
    up_halo = jnp.where(i == 0, 0.0, cvp_ref[0].astype(F32))
    dn_halo = jnp.where(i == pl.num_programs(1) - 1, 0.0, cvn_ref[0].astype(F32))
    up = jnp.concatenate([up_halo, cc[:tm - GRID_W]], axis=0)
    dn = jnp.concatenate([cc[GRID_W:], dn_halo], axis=0)
    w_c = cw_ref[:, CONV_ROW_WIDTH:]
    col_part = up * w_c[0:1] + cc * w_c[1:2] + dn * w_c[2:3]
    bg = bg_ref[0].astype(F32)
    y_row = (bg[:, :CONV_ROW_WIDTH] * row_part).astype(BF16)
    y_col = (bg[:, CONV_ROW_WIDTH:] * col_part).astype(BF16)
    o1 = S5_WIDTH
    o2 = S5_WIDTH + CONV_ROW_WIDTH
    yx = (jnp.dot(glu.astype(BF16), wout_ref[0:o1, :], preferred_element_type=F32)
          + jnp.dot(y_row, wout_ref[o1:o2, :], preferred_element_type=F32)
          + jnp.dot(y_col, wout_ref[o2:, :], preferred_element_type=F32))
    x1 = x_ref[0] + mod_ref[0, 2:3, :] * yx
    x1_ref[0] = x1
    h2 = _modulated_norm(x1, g2_ref[...], mod_ref[0, 3:4, :], mod_ref[0, 4:5, :]).astype(BF16)
    h2_ref[0] = h2
    lg2 = jnp.dot(h2, wr_ref[...], preferred_element_type=F32)
    logits = lg2[:, :ROUTER_LANES] + lg2[:, ROUTER_LANES:] + br_ref[...]
    gates_ref[0] = _route(logits)


def _mix(x, y_s5, bg, cv, mods, norm2_g, conv_w, w_glu, b_glu, w_out, w_router, b_router, tm):
    bsz, n, d = x.shape
    halo_blocks = n // GRID_W
    per_tile = tm // GRID_W
    tok = lambda w: pl.BlockSpec((1, tm, w), lambda b, i: (b, i, 0))
    full = lambda a: pl.BlockSpec(a.shape, lambda b, i: (0,) * a.ndim)
    args = (x, y_s5, bg, cv, cv, cv, mods, norm2_g.reshape(1, d), conv_w, w_glu, b_glu.reshape(1, -1),
            w_out, w_router, b_router)
    in_specs = [tok(d), tok(S5_WIDTH), tok(CONV_WIDTH), tok(CONV_WIDTH),
                pl.BlockSpec((1, GRID_W, CONV_ROW_WIDTH),
                             lambda b, i: (b, jnp.maximum(i * per_tile - 1, 0), 1)),
                pl.BlockSpec((1, GRID_W, CONV_ROW_WIDTH),
                             lambda b, i: (b, jnp.minimum((i + 1) * per_tile, halo_blocks - 1), 1)),
                pl.BlockSpec((1, N_MOD, d), lambda b, i: (b, 0, 0))] + [full(a) for a in args[7:]]
    return pl.pallas_call(
        _mix_kernel,
        grid=(bsz, n // tm),
        in_specs=in_specs,
        out_specs=[tok(d), tok(d), tok(ROUTER_LANES)],
        out_shape=[jax.ShapeDtypeStruct((bsz, n, d), F32), jax.ShapeDtypeStruct((bsz, n, d), BF16),
                   jax.ShapeDtypeStruct((bsz, n, ROUTER_LANES), F32)],
        compiler_params=_params(("parallel", "parallel")),
        name="mix",
    )(*args)


def _moe_kernel(h_ref, gates_ref, x1_ref, mod_ref, fg_ref, w13_ref, w2_ref, o_ref, acc_ref):
    e = pl.program_id(2)
    de = w2_ref.shape[1]

    @pl.when(e == 0)
    def _():
        acc_ref[...] = jnp.zeros_like(acc_ref)

    gates = gates_ref[0]
    lane = lax.broadcasted_iota(jnp.int32, gates.shape, 1)
    gate = jnp.sum(jnp.where(lane == e + EXPERT_LANE0, gates, 0.0), axis=-1, keepdims=True)
    a = jnp.dot(h_ref[0], w13_ref[0], preferred_element_type=F32)
    a1 = a[:, :de]
    he = (a1 * jax.nn.sigmoid(a1)) * a[:, de:]
    acc_ref[...] += gate * jnp.dot(he.astype(BF16), w2_ref[0], preferred_element_type=F32)

    @pl.when(e == pl.num_programs(2) - 1)
    def _():
        x2 = x1_ref[0] + mod_ref[0, 5:6, :] * acc_ref[...]
        ms = jnp.mean(x2 * x2, axis=-1, keepdims=True)
        o_ref[0] = x2 * lax.rsqrt(ms + RMS_EPS) * fg_ref[...]


def _moe_dense(h2, gates, x1, mods, final_g, w13, w2, tm):
    bsz, n, d = x1.shape
    n_e, de, _ = w2.shape
    tok = lambda w: pl.BlockSpec((1, tm, w), lambda b, i, e: (b, i, 0))
    return pl.pallas_call(
        _moe_kernel,
        grid=(bsz, n // tm, n_e),
        in_specs=[tok(d), tok(ROUTER_LANES), tok(d),
                  pl.BlockSpec((1, N_MOD, d), lambda b, i, e: (b, 0, 0)),
                  pl.BlockSpec((1, d), lambda b, i, e: (0, 0)),
                  pl.BlockSpec((1, d, 2 * de), lambda b, i, e: (e, 0, 0)),
                  pl.BlockSpec((1, de, d), lambda b, i, e: (e, 0, 0))],
        out_specs=tok(d),
        out_shape=jax.ShapeDtypeStruct((bsz, n, d), F32),
        scratch_shapes=[pltpu.VMEM((tm, d), F32)],
        compiler_params=_params(("parallel", "parallel", "arbitrary")),
        name="moe",
    )(h2, gates, x1, mods, final_g.reshape(1, d), w13, w2)


def kernel(x, c, ctx, c_ctx, w_mod, b_mod, norm1_g, norm2_g, w_in, s5_lambda_re, s5_lambda_im, s5_log_dt,
           s5_b_re, s5_b_im, s5_c_re, s5_c_im, s5_d, w_glu, b_glu, conv_w, w_out, router_group_w,
           router_group_b, router_expert_w, router_expert_b, expert_w1, expert_w3, expert_w2, final_g):
    assert w_mod.shape[0] == 1, "single-layer kernel"
    bsz, n_tok, d = x.shape
    n_ctx = ctx.shape[1]
    l = 0

    n_cond = bsz + 1
    pad = (-n_cond) % 8
    cond = jnp.concatenate([c, c_ctx[None, :], jnp.zeros((pad, d), F32)], axis=0)
    m = _mod_rows(cond, w_mod[l], b_mod[l])
    mx = m[:bsz].reshape(bsz, N_MOD, d)
    mc = m[bsz:bsz + 1].reshape(1, N_MOD, d)

    w_in_b = w_in[l].astype(BF16)
    u, bg, cv = _inproj(x, mx, True, norm1_g[l], w_in_b, min(TOKEN_TILE, n_tok), True)
    (uc,) = _inproj(ctx, mc, False, norm1_g[l], w_in_b[:, :S5_WIDTH], min(TOKEN_TILE, n_ctx), False)

    t_mat, mb_pair, mc_pair, a_rows = _s5_matrices(
        s5_lambda_re[l], s5_lambda_im[l], s5_log_dt[l], s5_b_re[l], s5_b_im[l], s5_c_re[l], s5_c_im[l], s5_d[l])
    y_t = _s5_scan(_to_chunk_major(u, bsz), _to_chunk_major(uc, bsz), t_mat, mb_pair, mc_pair, a_rows, bsz)
    y_s5 = _from_chunk_major(y_t, bsz)

    w_router = jnp.concatenate(
        [router_group_w[l], router_expert_w[l], jnp.zeros((d, ROUTER_LANES - N_GROUPS - N_EXPERTS), F32)], axis=1)
    w_router_hi = w_router.astype(BF16)
    w_router_lo = (w_router - w_router_hi.astype(F32)).astype(BF16)
    w_router2 = jnp.concatenate([w_router_hi, w_router_lo], axis=1)
    b_router = jnp.concatenate(
        [router_group_b[l], router_expert_b[l], jnp.zeros((ROUTER_LANES - N_GROUPS - N_EXPERTS,), F32)])[None, :]

    x1, h2, gates = _mix(x, y_s5, bg, cv, mx, norm2_g[l], conv_w[l], w_glu[l].astype(BF16), b_glu[l],
                         w_out[l].astype(BF16), w_router2, b_router, min(TOKEN_TILE, n_tok))

    w13 = jnp.concatenate([expert_w1[l], expert_w3[l]], axis=-1).astype(BF16)
    w2 = expert_w2[l].astype(BF16)
    return _moe_dense(h2, gates, x1, mx, final_g, w13, w2, min(MOE_TILE, n_tok))
```

```python
import functools

import jax
import jax.numpy as jnp
from jax import lax
from jax.experimental import pallas as pl
from jax.experimental.pallas import tpu as pltpu
from jax.experimental.pallas import tpu_sc as plsc

F32 = jnp.float32
BF16 = jnp.bfloat16
U32 = jnp.uint32

RMS_EPS = 1e-6
N_MOD = 6
GRID_W = 64
S5_WIDTH = 256
S5_H = 16
S5_P = 64
S5_GROUPS = S5_WIDTH // S5_H
S5_PAIRS = S5_GROUPS // 2
S5_CHUNK = 16
CONV_WIDTH = 768
CONV_ROW_WIDTH = CONV_WIDTH // 2
N_GROUPS = 4
EXPERTS_PER_GROUP = 4
N_EXPERTS = N_GROUPS * EXPERTS_PER_GROUP
PAIRS_PER_GROUP = 6
ROUTER_ROWS = 32
BUCKET_ROWS = 32
ROW_EXTRA = 128
TOKEN_TILE = 512
MOE_TILE = 512
RANK_TILE = 2048
SC_ROWS = 32
SC_INDEX_TILE = 128
VMEM_LIMIT = 52 * 1024 * 1024


def _params(sem, vmem=VMEM_LIMIT):
    return pltpu.CompilerParams(dimension_semantics=sem, vmem_limit_bytes=vmem)


def _mod_kernel(c_ref, w_ref, b_ref, o_ref):
    c = c_ref[...]
    o_ref[...] = jnp.dot(c * jax.nn.sigmoid(c), w_ref[...], preferred_element_type=F32) + b_ref[...]


def _mod_rows(cond, w_mod, b_mod):
    n, d = cond.shape
    nout = w_mod.shape[1]
    bn = d
    return pl.pallas_call(
        _mod_kernel,
        grid=(nout // bn,),
        in_specs=[pl.BlockSpec((n, d), lambda j: (0, 0)),
                  pl.BlockSpec((d, bn), lambda j: (0, j)),
                  pl.BlockSpec((1, bn), lambda j: (0, j))],
        out_specs=pl.BlockSpec((n, bn), lambda j: (0, j)),
        out_shape=jax.ShapeDtypeStruct((n, nout), F32),
        compiler_params=_params(("arbitrary",)),
        name="mod",
    )(cond, w_mod, b_mod.reshape(1, nout))


def _modulated_norm(x, g, shift, scale):
    ms = jnp.mean(x * x, axis=-1, keepdims=True)
    return (x * lax.rsqrt(ms + RMS_EPS) * g) * (1.0 + scale) + shift


def _inproj_kernel(x_ref, mod_ref, g_ref, w_ref, u_ref, *conv_refs):
    h = _modulated_norm(x_ref[0], g_ref[...], mod_ref[0, 0:1, :], mod_ref[0, 1:2, :]).astype(BF16)
    u_ref[0] = jnp.dot(h, w_ref[:, 0:S5_WIDTH], preferred_element_type=F32).astype(BF16)
    if conv_refs:
        bg_ref, cv_ref = conv_refs
        o = S5_WIDTH
        bg_ref[0] = jnp.dot(h, w_ref[:, o:o + CONV_WIDTH], preferred_element_type=F32).astype(BF16)
        c_g = jnp.dot(h, w_ref[:, o + CONV_WIDTH:o + 2 * CONV_WIDTH], preferred_element_type=F32)
        v = jnp.dot(h, w_ref[:, o + 2 * CONV_WIDTH:o + 3 * CONV_WIDTH], preferred_element_type=F32)
        cv_ref[0] = (c_g * v).astype(BF16)


def _inproj(x, mods, per_batch_mod, norm_g, w_bf16, tm, with_conv):
    bsz, n, d = x.shape
    nw = w_bf16.shape[1]
    mod_map = (lambda b, i: (b, 0, 0)) if per_batch_mod else (lambda b, i: (0, 0, 0))
    out_shape = [jax.ShapeDtypeStruct((bsz, n, S5_WIDTH), BF16)]
    out_specs = [pl.BlockSpec((1, tm, S5_WIDTH), lambda b, i: (b, i, 0))]
    if with_conv:
        out_shape += [jax.ShapeDtypeStruct((bsz, n, CONV_WIDTH), BF16)] * 2
        out_specs += [pl.BlockSpec((1, tm, CONV_WIDTH), lambda b, i: (b, i, 0))] * 2
    return pl.pallas_call(
        _inproj_kernel,
        grid=(bsz, n // tm),
        in_specs=[pl.BlockSpec((1, tm, d), lambda b, i: (b, i, 0)),
                  pl.BlockSpec((1, N_MOD, d), mod_map),
                  pl.BlockSpec((1, d), lambda b, i: (0, 0)),
                  pl.BlockSpec((d, nw), lambda b, i: (0, 0))],
        out_specs=out_specs,
        out_shape=out_shape,
        compiler_params=_params(("parallel", "parallel")),
        name="inproj_conv" if with_conv else "inproj_ctx",
    )(x, mods, norm_g.reshape(1, d), w_bf16)


def _s5_matrices(lam_re, lam_im, log_dt, b_re, b_im, c_re, c_im, d_skip):
    lc, g_n, p_n, h_n = S5_CHUNK, S5_GROUPS, S5_P, S5_H
    lam = lax.complex(lam_re.astype(F32), lam_im.astype(F32))
    dt = jnp.exp(log_dt.astype(F32))[..., None]
    a_bar = jnp.exp(lam * dt)
    b_bar = ((a_bar - 1.0) / lam)[..., None] * lax.complex(b_re.astype(F32), b_im.astype(F32))
    cm = lax.complex(c_re.astype(F32), c_im.astype(F32))
    steps = jnp.arange(lc + 1, dtype=F32)
    apow = jnp.exp((lam * dt)[:, :, None, :] * steps[None, None, :, None])
    kern = jnp.einsum('dgop,dgjp,dgpi->dgjoi', cm, apow[:, :, :lc], b_bar).real
    s_idx = jnp.arange(lc)[:, None]
    t_idx = jnp.arange(lc)[None, :]
    lag_f = jnp.clip(t_idx - s_idx, 0, lc - 1)
    lag_b = jnp.clip(s_idx - t_idx, 0, lc - 1)
    t_f = kern[0][:, lag_f] * (t_idx >= s_idx)[None, :, :, None, None]
    t_b = kern[1][:, lag_b] * (s_idx >= t_idx)[None, :, :, None, None]
    skip = (jnp.eye(lc, dtype=F32)[None, :, :, None, None]
            * (jnp.eye(h_n, dtype=F32) * d_skip.astype(F32).reshape(g_n, h_n, 1))[:, None, None, :, :])
    t_mat = (t_f + t_b + skip).transpose(0, 1, 4, 2, 3).reshape(g_n, lc * h_n, lc * h_n)

    def in_mat(pw, bb):
        return (pw[:, :, None, :] * bb.transpose(0, 2, 1)[:, None, :, :]).reshape(g_n, lc * h_n, p_n)

    mb_f = in_mat(apow[0, :, lc - 1::-1][:, :lc], b_bar[0])
    mb_b = in_mat(apow[1, :, :lc], b_bar[1])

    def out_mat(pw, cc):
        return (pw.transpose(0, 2, 1)[:, :, :, None] * cc.transpose(0, 2, 1)[:, :, None, :]).reshape(
            g_n, p_n, lc * h_n)

    mc_f = out_mat(apow[0, :, 1:lc + 1], cm[0])
    mc_b = out_mat(apow[1, :, lc:0:-1], cm[1])
    a_chunk = apow[:, :, lc]

    q_n = S5_PAIRS
    zeros_in = jnp.zeros((g_n, lc * h_n, p_n), F32)

    def pair_cols(m):
        m = m.reshape(q_n, 2, lc * h_n, p_n)
        z = zeros_in.reshape(q_n, 2, lc * h_n, p_n)[:, 0]
        top = jnp.concatenate([m[:, 0], z], axis=-1)
        bot = jnp.concatenate([z, m[:, 1]], axis=-1)
        return jnp.concatenate([top, bot], axis=1)

    mb_pair = jnp.concatenate([pair_cols(mb_f.real), pair_cols(mb_f.imag),
                               pair_cols(mb_b.real), pair_cols(mb_b.imag)], axis=-1)

    def pair_rows(m):
        m = m.reshape(q_n, 2, p_n, lc * h_n)
        z = jnp.zeros_like(m[:, 0])
        top = jnp.concatenate([m[:, 0], z], axis=-1)
        bot = jnp.concatenate([z, m[:, 1]], axis=-1)
        return jnp.concatenate([top, bot], axis=1)

    mc_pair = jnp.concatenate([pair_rows(mc_f.real), pair_rows(-mc_f.imag),
                               pair_rows(mc_b.real), pair_rows(-mc_b.imag)], axis=1)
    a_rows = jnp.stack([a_chunk[0].real, a_chunk[0].imag, a_chunk[1].real, a_chunk[1].imag], axis=0)
    a_rows = a_rows.reshape(4, q_n, 2 * p_n).transpose(1, 0, 2)
    a_rows = jnp.concatenate([a_rows, jnp.zeros_like(a_rows)], axis=1)
    return t_mat.astype(BF16), mb_pair.astype(BF16), mc_pair.astype(BF16), a_rows


def _s5_kernel(u_ref, uc_ref, t_ref, mb_ref, mc_ref, a_ref, y_ref, s_lat, s_ctx, h_scr, *, bsz, rb):
    n_lat = u_ref.shape[1] // bsz
    n_ctx = uc_ref.shape[1] // bsz
    sw = 2 * S5_P

    mb = mb_ref[0]

    def in_lat(i, carry):
        r = pl.multiple_of(i * rb, rb)
        s_lat[pl.ds(r, rb), :] = jnp.dot(u_ref[0, pl.ds(r, rb), :], mb, preferred_element_type=F32)
        return carry

    lax.fori_loop(0, u_ref.shape[1] // rb, in_lat, 0)
    s_ctx[...] = jnp.dot(uc_ref[0], mb, preferred_element_type=F32)

    a_fr, a_fi, a_br, a_bi = (a_ref[0, k:k + 1, :] for k in range(4))

    def step(h, a_r, a_i, s_r, s_i):
        h_r, h_i = h
        return a_r * h_r - a_i * h_i + s_r, a_r * h_i + a_i * h_r + s_i

    def ctx_step(k, carry):
        hf, hb = carry
        rf = pl.multiple_of(k * bsz, bsz)
        rbk = pl.multiple_of((n_ctx - 1 - k) * bsz, bsz)
        hf = step(hf, a_fr, a_fi, s_ctx[pl.ds(rf, bsz), 0:sw], s_ctx[pl.ds(rf, bsz), sw:2 * sw])
        hb = step(hb, a_br, a_bi, s_ctx[pl.ds(rbk, bsz), 2 * sw:3 * sw], s_ctx[pl.ds(rbk, bsz), 3 * sw:4 * sw])
        return hf, hb

    zero = jnp.zeros((bsz, sw), F32)
    carry = lax.fori_loop(0, n_ctx, ctx_step, ((zero, zero), (zero, zero)))

    def lat_step(k, carry):
        hf, hb = carry
        rf = pl.multiple_of(k * bsz, bsz)
        rbk = pl.multiple_of((n_lat - 1 - k) * bsz, bsz)
        h_scr[pl.ds(rf, bsz), 0:sw] = hf[0].astype(BF16)
        h_scr[pl.ds(rf, bsz), sw:2 * sw] = hf[1].astype(BF16)
        h_scr[pl.ds(rbk, bsz), 2 * sw:3 * sw] = hb[0].astype(BF16)
        h_scr[pl.ds(rbk, bsz), 3 * sw:4 * sw] = hb[1].astype(BF16)
        hf = step(hf, a_fr, a_fi, s_lat[pl.ds(rf, bsz), 0:sw], s_lat[pl.ds(rf, bsz), sw:2 * sw])
        hb = step(hb, a_br, a_bi, s_lat[pl.ds(rbk, bsz), 2 * sw:3 * sw], s_lat[pl.ds(rbk, bsz), 3 * sw:4 * sw])
        return hf, hb

    lax.fori_loop(0, n_lat, lat_step, carry)

    t0 = t_ref[0]
    t1 = t_ref[1]
    mc = mc_ref[0]
    half = S5_CHUNK * S5_H

    def out_lat(i, carry):
        r = pl.multiple_of(i * rb, rb)
        u = u_ref[0, pl.ds(r, rb), :]
        inter = jnp.dot(h_scr[pl.ds(r, rb), :], mc, preferred_element_type=F32)
        y0 = jnp.dot(u[:, :half], t0, preferred_element_type=F32) + inter[:, :half]
        y1 = jnp.dot(u[:, half:], t1, preferred_element_type=F32) + inter[:, half:]
        y_ref[0, pl.ds(r, rb), 0:half] = y0.astype(BF16)
        y_ref[0, pl.ds(r, rb), half:2 * half] = y1.astype(BF16)
        return carry

    lax.fori_loop(0, u_ref.shape[1] // rb, out_lat, 0)


def _s5_scan(u_t, uc_t, t_mat, mb_pair, mc_pair, a_rows, bsz):
    q_n, r_lat, w = u_t.shape
    r_ctx = uc_t.shape[1]
    rb = min(512, r_lat)
    return pl.pallas_call(
        functools.partial(_s5_kernel, bsz=bsz, rb=rb),
        grid=(q_n,),
        in_specs=[pl.BlockSpec((1, r_lat, w), lambda q: (q, 0, 0)),
                  pl.BlockSpec((1, r_ctx, w), lambda q: (q, 0, 0)),
                  pl.BlockSpec((2, w // 2, w // 2), lambda q: (q, 0, 0)),
                  pl.BlockSpec((1, w, w), lambda q: (q, 0, 0)),
                  pl.BlockSpec((1, w, w), lambda q: (q, 0, 0)),
                  pl.BlockSpec((1, 8, 2 * S5_P), lambda q: (q, 0, 0))],
        out_specs=pl.BlockSpec((1, r_lat, w), lambda q: (q, 0, 0)),
        out_shape=jax.ShapeDtypeStruct((q_n, r_lat, w), BF16),
        scratch_shapes=[pltpu.VMEM((r_lat, w), F32), pltpu.VMEM((r_ctx, w), F32), pltpu.VMEM((r_lat, w), BF16)],
        compiler_params=_params(("parallel",)),
        name="s5_scan",
    )(u_t, uc_t, t_mat, mb_pair, mc_pair, a_rows)


def _to_chunk_major(u, bsz):
    n = u.shape[1]
    nc = n // S5_CHUNK
    u = u.reshape(bsz, nc, S5_CHUNK, S5_PAIRS, 2, S5_H).transpose(3, 1, 0, 4, 2, 5)
    return u.reshape(S5_PAIRS, nc * bsz, 2 * S5_CHUNK * S5_H)


def _from_chunk_major(y, bsz):
    nc = y.shape[1] // bsz
    y = y.reshape(S5_PAIRS, nc, bsz, 2, S5_CHUNK, S5_H).transpose(2, 1, 4, 0, 3, 5)
    return y.reshape(bsz, nc * S5_CHUNK, S5_WIDTH)


def _pack_bf16_pair(a, b):
    ua = lax.bitcast_convert_type(a.astype(BF16).astype(F32), U32)
    ub = lax.bitcast_convert_type(b.astype(BF16).astype(F32), U32)
    return ua | (ub >> 16)


def _unpack_bf16_pair(w):
    a = lax.bitcast_convert_type(w & jnp.uint32(0xFFFF0000), F32)
    b = lax.bitcast_convert_type(w << 16, F32)
    return a, b


def _first_max(rows):
    best = rows[0]
    for r in rows[1:]:
        best = jnp.maximum(best, r)
    idx = jnp.full(best.shape, float(len(rows) - 1), F32)
    for k in range(len(rows) - 2, -1, -1):
        idx = jnp.where(rows[k] == best, float(k), idx)
    return best, idx


def _route_rows(lg):
    g_rows = [lg[k:k + 1] for k in range(N_GROUPS)]
    g_max, g_idx = _first_max(g_rows)
    g_sum = sum(jnp.exp(r - g_max) for r in g_rows)
    g_p = 1.0 / g_sum
    e_rows = []
    for j in range(EXPERTS_PER_GROUP):
        r = lg[N_GROUPS + (N_GROUPS - 1) * EXPERTS_PER_GROUP + j:][:1]
        for g in range(N_GROUPS - 2, -1, -1):
            k = N_GROUPS + g * EXPERTS_PER_GROUP + j
            r = jnp.where(g_idx == float(g), lg[k:k + 1], r)
        e_rows.append(r)
    v1, i1 = _first_max(e_rows)
    rest = [jnp.where(i1 == float(j), -jnp.inf, e_rows[j]) for j in range(EXPERTS_PER_GROUP)]
    v2, i2 = _first_max(rest)
    e21 = jnp.exp(v2 - v1)
    w1 = g_p / (1.0 + e21)
    w2 = w1 * e21
    lo = jnp.minimum(i1, i2)
    hi = jnp.maximum(i1, i2)
    base = jnp.where(lo == 0.0, 0.0, jnp.where(lo == 1.0, 3.0, 5.0))
    bucket = g_idx * float(PAIRS_PER_GROUP) + base + hi - lo - 1.0
    first_is_lo = i1 < i2
    return bucket, jnp.where(first_is_lo, w1, w2), jnp.where(first_is_lo, w2, w1)


def _mix_kernel(x_ref, y_ref, bg_ref, cv_ref, cvp_ref, cvn_ref, mod_ref, g2_ref, cw_ref, wglu_ref, bglu_ref,
                wout_ref, wr_ref, br_ref, x1_ref, h2p_ref, route_ref, counts_ref):
    i = pl.program_id(1)
    tm = x_ref.shape[1]
    d = x_ref.shape[2]
    g = jax.nn.gelu(y_ref[0].astype(F32))
    glu = g * jax.nn.sigmoid(jnp.dot(g.astype(BF16), wglu_ref[...], preferred_element_type=F32) + bglu_ref[...])
    cv = cv_ref[0].astype(F32)
    row = lax.broadcasted_iota(jnp.int32, (tm, 1), 0)
    col_in_row = row % GRID_W
    cr = cv[:, :CONV_ROW_WIDTH]
    left = jnp.where(col_in_row == 0, 0.0, pltpu.roll(cr, 1, axis=0))
    right = jnp.where(col_in_row == GRID_W - 1, 0.0, pltpu.roll(cr, tm - 1, axis=0))
    w_r = cw_ref[:, :CONV_ROW_WIDTH]
    row_part = left * w_r[0:1] + cr * w_r[1:2] + right * w_r[2:3]
    cc = cv[:, CONV_ROW_WIDTH:]
    up_halo = jnp.where(i == 0, 0.0, cvp_ref[0].astype(F32))
    dn_halo = jnp.where(i == pl.num_programs(1) - 1, 0.0, cvn_ref[0].astype(F32))
    up = jnp.concatenate([up_halo, cc[:tm - GRID_W]], axis=0)
    dn = jnp.concatenate([cc[GRID_W:], dn_halo], axis=0)
    w_c = cw_ref[:, CONV_ROW_WIDTH:]
    col_part = up * w_c[0:1] + cc * w_c[1:2] + dn * w_c[2:3]
    bg = bg_ref[0].astype(F32)
    y_row = (bg[:, :CONV_ROW_WIDTH] * row_part).astype(BF16)
    y_col = (bg[:, CONV_ROW_WIDTH:] * col_part).astype(BF16)
    o1 = S5_WIDTH
    o2 = S5_WIDTH + CONV_ROW_WIDTH
    yx = (jnp.dot(glu.astype(BF16), wout_ref[0:o1, :], preferred_element_type=F32)
          + jnp.dot(y_row, wout_ref[o1:o2, :], preferred_element_type=F32)
          + jnp.dot(y_col, wout_ref[o2:, :], preferred_element_type=F32))
    x1 = x_ref[0] + mod_ref[0, 2:3, :] * yx
    x1_ref[0] = x1
    h2 = _modulated_norm(x1, g2_ref[...], mod_ref[0, 3:4, :], mod_ref[0, 4:5, :])
    h2b = h2.astype(BF16)
    lg2 = lax.dot_general(wr_ref[...], h2b, (((1,), (1,)), ((), ())), preferred_element_type=F32)
    lg = lg2[:ROUTER_ROWS] + lg2[ROUTER_ROWS:] + br_ref[...]
    bucket, w_a, w_b = _route_rows(lg)
    r8 = lax.broadcasted_iota(jnp.int32, (8, tm), 0)
    route_ref[...] = jnp.where(r8 == 0, bucket, jnp.where(r8 == 1, w_a, jnp.where(r8 == 2, w_b, 0.0)))
    rl = lax.broadcasted_iota(jnp.int32, (ROW_EXTRA, tm), 0)
    gates_t = jnp.where(rl == 0, w_a, jnp.where(rl == 1, w_b, 0.0))
    h2p_ref[0, :, 0:d // 2] = _pack_bf16_pair(h2[:, :d // 2], h2[:, d // 2:])
    h2p_ref[0, :, d // 2:] = lax.bitcast_convert_type(gates_t.T, U32)
    rb = lax.broadcasted_iota(jnp.int32, (BUCKET_ROWS, tm), 0).astype(F32)
    cnt = jnp.sum(jnp.where(rb == bucket, 1.0, 0.0), axis=-1, keepdims=True)

    @pl.when(jnp.logical_and(pl.program_id(0) == 0, i == 0))
    def _():
        counts_ref[...] = jnp.zeros_like(counts_ref)

    counts_ref[...] += cnt


def _mix(x, y_s5, bg, cv, mods, norm2_g, conv_w, w_glu, b_glu, w_out, w_router_t, b_router_t, tm):
    bsz, n, d = x.shape
    nt = n // tm
    halo_blocks = n // GRID_W
    per_tile = tm // GRID_W
    tok = lambda w: pl.BlockSpec((1, tm, w), lambda b, i: (b, i, 0))
    full = lambda a: pl.BlockSpec(a.shape, lambda b, i: (0,) * a.ndim)
    args = (x, y_s5, bg, cv, cv, cv, mods, norm2_g.reshape(1, d), conv_w, w_glu, b_glu.reshape(1, -1),
            w_out, w_router_t, b_router_t)
    in_specs = [tok(d), tok(S5_WIDTH), tok(CONV_WIDTH), tok(CONV_WIDTH),
                pl.BlockSpec((1, GRID_W, CONV_ROW_WIDTH),
                             lambda b, i: (b, jnp.maximum(i * per_tile - 1, 0), 1)),
                pl.BlockSpec((1, GRID_W, CONV_ROW_WIDTH),
                             lambda b, i: (b, jnp.minimum((i + 1) * per_tile, halo_blocks - 1), 1)),
                pl.BlockSpec((1, N_MOD, d), lambda b, i: (b, 0, 0))] + [full(a) for a in args[7:]]
    return pl.pallas_call(
        _mix_kernel,
        grid=(bsz, nt),
        in_specs=in_specs,
        out_specs=[tok(d), tok(d // 2 + ROW_EXTRA),
                   pl.BlockSpec((8, tm), lambda b, i: (0, b * nt + i)),
                   pl.BlockSpec((BUCKET_ROWS, 128), lambda b, i: (0, 0))],
        out_shape=[jax.ShapeDtypeStruct((bsz, n, d), F32),
                   jax.ShapeDtypeStruct((bsz, n, d // 2 + ROW_EXTRA), U32),
                   jax.ShapeDtypeStruct((8, bsz * n), F32),
                   jax.ShapeDtypeStruct((BUCKET_ROWS, 128), F32)],
        compiler_params=_params(("arbitrary", "arbitrary")),
        name="mix",
    )(*args)


def _rank_kernel(route_ref, offs_ref, dest_ref, run_ref):
    tr = route_ref.shape[1]

    @pl.when(pl.program_id(0) == 0)
    def _():
        run_ref[...] = jnp.zeros_like(run_ref)

    bucket = route_ref[0:1, :]
    rb = lax.broadcasted_iota(jnp.int32, (BUCKET_ROWS, tr), 0).astype(F32)
    onehot = jnp.where(rb == bucket, 1.0, 0.0)
    s_idx = lax.broadcasted_iota(jnp.int32, (tr, tr), 0)
    t_idx = lax.broadcasted_iota(jnp.int32, (tr, tr), 1)
    tri = jnp.where(s_idx <= t_idx, 1.0, 0.0).astype(BF16)
    prefix = jnp.dot(onehot.astype(BF16), tri, preferred_element_type=F32)
    before = run_ref[:, 0:1] + offs_ref[:, 0:1]
    dest = jnp.sum(onehot * (prefix - 1.0 + before), axis=0, keepdims=True)
    dest_ref[...] = dest.astype(jnp.int32)
    run_ref[...] += prefix[:, tr - 1:tr]


def _rank(route_t, offs_rows, tr):
    n = route_t.shape[1]
    return pl.pallas_call(
        _rank_kernel,
        grid=(n // tr,),
        in_specs=[pl.BlockSpec((8, tr), lambda i: (0, i)),
                  pl.BlockSpec((BUCKET_ROWS, 128), lambda i: (0, 0))],
        out_specs=pl.BlockSpec((1, tr), lambda i: (0, i)),
        out_shape=jax.ShapeDtypeStruct((1, n), jnp.int32),
        scratch_shapes=[pltpu.VMEM((BUCKET_ROWS, 128), F32)],
        compiler_params=_params(("arbitrary",)),
        name="rank",
    )(route_t, offs_rows)


def _sc_mesh():
    return plsc.VectorSubcoreMesh(core_axis_name="core", subcore_axis_name="subcore")


def _scatter_rows(src, dest, n_out):
    n, w = src.shape
    sub = SC_INDEX_TILE // SC_ROWS

    @functools.partial(pl.kernel, out_type=jax.ShapeDtypeStruct((n_out, w), src.dtype), mesh=_sc_mesh(),
                       scratch_types=[], name="scatter_rows")
    def scatter(x_hbm, i_hbm, o_hbm):
        def body(x_vmem, i_vmem):
            j = pl.program_id(1)
            pltpu.sync_copy(x_vmem, o_hbm.at[i_vmem.at[0, pl.ds(j * SC_ROWS, SC_ROWS)]])

        pltpu.emit_pipeline(
            body,
            grid=(n // SC_INDEX_TILE, sub),
            in_specs=[pl.BlockSpec((SC_ROWS, w), lambda i, j: (i * sub + j, 0)),
                      pl.BlockSpec((1, SC_INDEX_TILE), lambda i, j: (0, i))],
            out_specs=[],
            core_axis_name=("core", "subcore"),
            dimension_semantics=(pltpu.PARALLEL, pltpu.ARBITRARY),
        )(x_hbm, i_hbm)

    return scatter(src, dest)


def _gather_rows(src, idx):
    n = idx.shape[1]
    w = src.shape[1]
    sub = SC_INDEX_TILE // SC_ROWS

    @functools.partial(pl.kernel, out_type=jax.ShapeDtypeStruct((n, w), src.dtype), mesh=_sc_mesh(),
                       scratch_types=[], name="gather_rows")
    def gather(x_hbm, i_hbm, o_hbm):
        def body(i_vmem, o_vmem):
            j = pl.program_id(1)
            pltpu.sync_copy(x_hbm.at[i_vmem.at[0, pl.ds(j * SC_ROWS, SC_ROWS)]], o_vmem)

        pltpu.emit_pipeline(
            body,
            grid=(n // SC_INDEX_TILE, sub),
            in_specs=[pl.BlockSpec((1, SC_INDEX_TILE), lambda i, j: (0, i))],
            out_specs=[pl.BlockSpec((SC_ROWS, w), lambda i, j: (i * sub + j, 0))],
            core_axis_name=("core", "subcore"),
            dimension_semantics=(pltpu.PARALLEL, pltpu.ARBITRARY),
        )(i_hbm, o_hbm)

    return gather(src, idx)


def _moe_kernel(ea_ref, eb_ref, valid_ref, rows_ref, w13a_ref, w13b_ref, w2a_ref, w2b_ref, o_ref):
    j = pl.program_id(0)
    half = rows_ref.shape[1] - ROW_EXTRA
    de = w2a_ref.shape[1]

    @pl.when(valid_ref[j] != 0)
    def _():
        ha, hb = _unpack_bf16_pair(rows_ref[:, 0:half])
        ha = ha.astype(BF16)
        hb = hb.astype(BF16)
        gates = lax.bitcast_convert_type(rows_ref[:, half:], F32)

        def expert(w13_ref, w2_ref, gate):
            a = (jnp.dot(ha, w13_ref[0, 0:half, :], preferred_element_type=F32)
                 + jnp.dot(hb, w13_ref[0, half:, :], preferred_element_type=F32))
            a1 = a[:, :de]
            he = (a1 * jax.nn.sigmoid(a1)) * a[:, de:] * gate
            return jnp.dot(he.astype(BF16), w2_ref[0], preferred_element_type=F32)

        y = expert(w13a_ref, w2a_ref, gates[:, 0:1]) + expert(w13b_ref, w2b_ref, gates[:, 1:2])
        o_ref[...] = _pack_bf16_pair(y[:, :half], y[:, half:])


def _moe_grouped(rows, tile_ea, tile_eb, tile_valid, w13, w2, tmm):
    r, w = rows.shape
    n_e, de, d = w2.shape
    grid_spec = pltpu.PrefetchScalarGridSpec(
        num_scalar_prefetch=3,
        grid=(r // tmm,),
        in_specs=[pl.BlockSpec((tmm, w), lambda j, ea, eb, va: (j, 0)),
                  pl.BlockSpec((1, d, 2 * de), lambda j, ea, eb, va: (ea[j], 0, 0)),
                  pl.BlockSpec((1, d, 2 * de), lambda j, ea, eb, va: (eb[j], 0, 0)),
                  pl.BlockSpec((1, de, d), lambda j, ea, eb, va: (ea[j], 0, 0)),
                  pl.BlockSpec((1, de, d), lambda j, ea, eb, va: (eb[j], 0, 0))],
        out_specs=pl.BlockSpec((tmm, d // 2), lambda j, ea, eb, va: (j, 0)),
    )
    return pl.pallas_call(
        _moe_kernel,
        grid_spec=grid_spec,
        out_shape=jax.ShapeDtypeStruct((r, d // 2), U32),
        compiler_params=_params(("arbitrary",)),
        name="moe",
    )(tile_ea, tile_eb, tile_valid, rows, w13, w13, w2, w2)


def _final_kernel(x1_ref, moe_ref, mod_ref, fg_ref, o_ref):
    ya, yb = _unpack_bf16_pair(moe_ref[0])
    half = ya.shape[1]
    gate = mod_ref[0, 5:6, :]
    xa = x1_ref[0, :, 0:half] + gate[:, 0:half] * ya
    xb = x1_ref[0, :, half:] + gate[:, half:] * yb
    ms = (jnp.sum(xa * xa, axis=-1, keepdims=True) + jnp.sum(xb * xb, axis=-1, keepdims=True)) / (2 * half)
    inv = lax.rsqrt(ms + RMS_EPS)
    o_ref[0, :, 0:half] = xa * inv * fg_ref[:, 0:half]
    o_ref[0, :, half:] = xb * inv * fg_ref[:, half:]


def _final(x1, moe_tok, mods, final_g, tm):
    bsz, n, d = x1.shape
    tok = lambda w: pl.BlockSpec((1, tm, w), lambda b, i: (b, i, 0))
    return pl.pallas_call(
        _final_kernel,
        grid=(bsz, n // tm),
        in_specs=[tok(d), tok(d // 2), pl.BlockSpec((1, N_MOD, d), lambda b, i: (b, 0, 0)),
                  pl.BlockSpec((1, d), lambda b, i: (0, 0))],
        out_specs=tok(d),
        out_shape=jax.ShapeDtypeStruct((bsz, n, d), F32),
        compiler_params=_params(("parallel", "parallel")),
        name="final",
    )(x1, moe_tok, mods, final_g.reshape(1, d))


def _tile_plan(counts, tmm, n_tiles):
    tiles = (counts + (tmm - 1)) // tmm
    tile_end = jnp.cumsum(tiles)
    offs = (tile_end - tiles) * tmm
    n_valid = tile_end[-1]
    j = jnp.arange(n_tiles, dtype=jnp.int32)
    bucket = jnp.sum((tile_end[None, :] <= jnp.minimum(j, n_valid - 1)[:, None]).astype(jnp.int32), axis=1)
    pair_lo = jnp.array([0, 0, 0, 1, 1, 2], jnp.int32)
    pair_hi = jnp.array([1, 2, 3, 2, 3, 3], jnp.int32)
    group = bucket // PAIRS_PER_GROUP
    pair = bucket % PAIRS_PER_GROUP
    tile_ea = group * EXPERTS_PER_GROUP + pair_lo[pair]
    tile_eb = group * EXPERTS_PER_GROUP + pair_hi[pair]
    return offs, tile_ea, tile_eb, (j < n_valid).astype(jnp.int32)


def kernel(x, c, ctx, c_ctx, w_mod, b_mod, norm1_g, norm2_g, w_in, s5_lambda_re, s5_lambda_im, s5_log_dt,
           s5_b_re, s5_b_im, s5_c_re, s5_c_im, s5_d, w_glu, b_glu, conv_w, w_out, router_group_w,
           router_group_b, router_expert_w, router_expert_b, expert_w1, expert_w3, expert_w2, final_g):
    assert w_mod.shape[0] == 1, "single-layer kernel"
    bsz, n_tok, d = x.shape
    n_ctx = ctx.shape[1]
    n_all = bsz * n_tok
    l = 0
    tm = min(TOKEN_TILE, n_tok)

    n_cond = bsz + 1
    pad = (-n_cond) % 8
    cond = jnp.concatenate([c, c_ctx[None, :], jnp.zeros((pad, d), F32)], axis=0)
    m = _mod_rows(cond, w_mod[l], b_mod[l])
    mx = m[:bsz].reshape(bsz, N_MOD, d)
    mc = m[bsz:bsz + 1].reshape(1, N_MOD, d)

    w_in_b = w_in[l].astype(BF16)
    u, bg, cv = _inproj(x, mx, True, norm1_g[l], w_in_b, tm, True)
    (uc,) = _inproj(ctx, mc, False, norm1_g[l], w_in_b[:, :S5_WIDTH], min(TOKEN_TILE, n_ctx), False)

    t_mat, mb_pair, mc_pair, a_rows = _s5_matrices(
        s5_lambda_re[l], s5_lambda_im[l], s5_log_dt[l], s5_b_re[l], s5_b_im[l], s5_c_re[l], s5_c_im[l], s5_d[l])
    y_t = _s5_scan(_to_chunk_major(u, bsz), _to_chunk_major(uc, bsz), t_mat, mb_pair, mc_pair, a_rows, bsz)
    y_s5 = _from_chunk_major(y_t, bsz)

    n_logits = N_GROUPS + N_EXPERTS
    w_router = jnp.concatenate(
        [router_group_w[l], router_expert_w[l], jnp.zeros((d, ROUTER_ROWS - n_logits), F32)], axis=1).T
    w_router_hi = w_router.astype(BF16)
    w_router_lo = (w_router - w_router_hi.astype(F32)).astype(BF16)
    w_router_t = jnp.concatenate([w_router_hi, w_router_lo], axis=0)
    b_router = jnp.concatenate([router_group_b[l], router_expert_b[l], jnp.zeros((ROUTER_ROWS - n_logits,), F32)])
    b_router_t = jnp.broadcast_to(b_router[:, None], (ROUTER_ROWS, tm))

    x1, h2p, route_t, counts = _mix(x, y_s5, bg, cv, mx, norm2_g[l], conv_w[l], w_glu[l].astype(BF16), b_glu[l],
                                    w_out[l].astype(BF16), w_router_t, b_router_t, tm)

    n_buckets = N_GROUPS * PAIRS_PER_GROUP
    n_rows = n_all + n_buckets * MOE_TILE
    offs, tile_ea, tile_eb, tile_valid = _tile_plan(counts[:n_buckets, 0].astype(jnp.int32), MOE_TILE,
                                                    n_rows // MOE_TILE)
    offs_rows = jnp.zeros((BUCKET_ROWS,), F32).at[:n_buckets].set(offs.astype(F32))
    dest = _rank(route_t, jnp.broadcast_to(offs_rows[:, None], (BUCKET_ROWS, 128)), min(RANK_TILE, n_all))

    rows = _scatter_rows(h2p.reshape(n_all, d // 2 + ROW_EXTRA), dest, n_rows)
    w13 = jnp.concatenate([expert_w1[l], expert_w3[l]], axis=-1).astype(BF16)
    w2 = expert_w2[l].astype(BF16)
    y_rows = _moe_grouped(rows, tile_ea, tile_eb, tile_valid, w13, w2, MOE_TILE)
    moe_tok = _gather_rows(y_rows, dest).reshape(bsz, n_tok, d // 2)
    return _final(x1, moe_tok, mx, final_g, tm)
```

```python
import functools

import jax
import jax.numpy as jnp
from jax import lax
from jax.experimental import pallas as pl
from jax.experimental.pallas import tpu as pltpu
from jax.experimental.pallas import tpu_sc as plsc

F32 = jnp.float32
BF16 = jnp.bfloat16
U32 = jnp.uint32

RMS_EPS = 1e-6
N_MOD = 6
GRID_W = 64
S5_WIDTH = 256
S5_H = 16
S5_P = 64
S5_GROUPS = S5_WIDTH // S5_H
S5_PAIRS = S5_GROUPS // 2
S5_CHUNK = 16
CONV_WIDTH = 768
CONV_ROW_WIDTH = CONV_WIDTH // 2
N_GROUPS = 4
EXPERTS_PER_GROUP = 4
N_EXPERTS = N_GROUPS * EXPERTS_PER_GROUP
PAIRS_PER_GROUP = 6
ROUTER_ROWS = 32
BUCKET_ROWS = 32
ROW_EXTRA = 128
TOKEN_TILE = 512
MOE_TILE = 512
RANK_TILE = 2048
SC_ROWS = 32
SC_INDEX_TILE = 128
VMEM_LIMIT = 52 * 1024 * 1024


def _params(sem, vmem=VMEM_LIMIT):
    return pltpu.CompilerParams(dimension_semantics=sem, vmem_limit_bytes=vmem)


def _mod_kernel(c_ref, w_ref, b_ref, o_ref):
    c = c_ref[...]
    o_ref[...] = jnp.dot(c * jax.nn.sigmoid(c), w_ref[...], preferred_element_type=F32) + b_ref[...]


def _mod_rows(cond, w_mod, b_mod):
    n, d = cond.shape
    nout = w_mod.shape[1]
    bn = d
    return pl.pallas_call(
        _mod_kernel,
        grid=(nout // bn,),
        in_specs=[pl.BlockSpec((n, d), lambda j: (0, 0)),
                  pl.BlockSpec((d, bn), lambda j: (0, j)),
                  pl.BlockSpec((1, bn), lambda j: (0, j))],
        out_specs=pl.BlockSpec((n, bn), lambda j: (0, j)),
        out_shape=jax.ShapeDtypeStruct((n, nout), F32),
        compiler_params=_params(("arbitrary",)),
        name="mod",
    )(cond, w_mod, b_mod.reshape(1, nout))


def _modulated_norm(x, g, shift, scale):
    ms = jnp.mean(x * x, axis=-1, keepdims=True)
    return (x * lax.rsqrt(ms + RMS_EPS) * g) * (1.0 + scale) + shift


def _inproj_kernel(x_ref, mod_ref, g_ref, w_ref, u_ref, *conv_refs):
    h = _modulated_norm(x_ref[0], g_ref[...], mod_ref[0, 0:1, :], mod_ref[0, 1:2, :]).astype(BF16)
    u_ref[0] = jnp.dot(h, w_ref[:, 0:S5_WIDTH], preferred_element_type=F32).astype(BF16)
    if conv_refs:
        bg_ref, cv_ref = conv_refs
        o = S5_WIDTH
        bg_ref[0] = jnp.dot(h, w_ref[:, o:o + CONV_WIDTH], preferred_element_type=F32).astype(BF16)
        c_g = jnp.dot(h, w_ref[:, o + CONV_WIDTH:o + 2 * CONV_WIDTH], preferred_element_type=F32)
        v = jnp.dot(h, w_ref[:, o + 2 * CONV_WIDTH:o + 3 * CONV_WIDTH], preferred_element_type=F32)
        cv_ref[0] = (c_g * v).astype(BF16)


def _inproj(x, mods, per_batch_mod, norm_g, w_bf16, tm, with_conv):
    bsz, n, d = x.shape
    nw = w_bf16.shape[1]
    mod_map = (lambda b, i: (b, 0, 0)) if per_batch_mod else (lambda b, i: (0, 0, 0))
    out_shape = [jax.ShapeDtypeStruct((bsz, n, S5_WIDTH), BF16)]
    out_specs = [pl.BlockSpec((1, tm, S5_WIDTH), lambda b, i: (b, i, 0))]
    if with_conv:
        out_shape += [jax.ShapeDtypeStruct((bsz, n, CONV_WIDTH), BF16)] * 2
        out_specs += [pl.BlockSpec((1, tm, CONV_WIDTH), lambda b, i: (b, i, 0))] * 2
    return pl.pallas_call(
        _inproj_kernel,
        grid=(bsz, n // tm),
        in_specs=[pl.BlockSpec((1, tm, d), lambda b, i: (b, i, 0)),
                  pl.BlockSpec((1, N_MOD, d), mod_map),
                  pl.BlockSpec((1, d), lambda b, i: (0, 0)),
                  pl.BlockSpec((d, nw), lambda b, i: (0, 0))],
        out_specs=out_specs,
        out_shape=out_shape,
        compiler_params=_params(("parallel", "parallel")),
        name="inproj_conv" if with_conv else "inproj_ctx",
    )(x, mods, norm_g.reshape(1, d), w_bf16)


def _s5_matrices(lam_re, lam_im, log_dt, b_re, b_im, c_re, c_im, d_skip):
    lc, g_n, p_n, h_n = S5_CHUNK, S5_GROUPS, S5_P, S5_H
    lam = lax.complex(lam_re.astype(F32), lam_im.astype(F32))
    dt = jnp.exp(log_dt.astype(F32))[..., None]
    a_bar = jnp.exp(lam * dt)
    b_bar = ((a_bar - 1.0) / lam)[..., None] * lax.complex(b_re.astype(F32), b_im.astype(F32))
    cm = lax.complex(c_re.astype(F32), c_im.astype(F32))
    steps = jnp.arange(lc + 1, dtype=F32)
    apow = jnp.exp((lam * dt)[:, :, None, :] * steps[None, None, :, None])
    s_idx = jnp.arange(lc, dtype=F32)[:, None]
    t_idx = jnp.arange(lc, dtype=F32)[None, :]
    lag = jnp.stack([t_idx - s_idx, s_idx - t_idx])
    causal = (lag >= 0).astype(F32)
    lag_pow = (jnp.exp((lam * dt)[:, :, None, None, :] * jnp.maximum(lag, 0.0)[:, None, :, :, None])
               * causal[:, None, :, :, None])
    resp = jnp.einsum('dgop,dgstp,dgpi->gsito', cm, lag_pow, b_bar).real
    skip = (jnp.eye(lc, dtype=F32)[None, :, None, :, None]
            * (jnp.eye(h_n, dtype=F32) * d_skip.astype(F32).reshape(g_n, 1, h_n))[:, None, :, None, :])
    t_mat = (resp + skip).reshape(g_n, lc * h_n, lc * h_n)

    def in_mat(pw, bb):
        return (pw[:, :, None, :] * bb.transpose(0, 2, 1)[:, None, :, :]).reshape(g_n, lc * h_n, p_n)

    mb_f = in_mat(apow[0, :, lc - 1::-1][:, :lc], b_bar[0])
    mb_b = in_mat(apow[1, :, :lc], b_bar[1])

    def out_mat(pw, cc):
        return (pw.transpose(0, 2, 1)[:, :, :, None] * cc.transpose(0, 2, 1)[:, :, None, :]).reshape(
            g_n, p_n, lc * h_n)

    mc_f = out_mat(apow[0, :, 1:lc + 1], cm[0])
    mc_b = out_mat(apow[1, :, lc:0:-1], cm[1])
    a_chunk = apow[:, :, lc]

    q_n = S5_PAIRS
    zeros_in = jnp.zeros((g_n, lc * h_n, p_n), F32)

    def pair_cols(m):
        m = m.reshape(q_n, 2, lc * h_n, p_n)
        z = zeros_in.reshape(q_n, 2, lc * h_n, p_n)[:, 0]
        top = jnp.concatenate([m[:, 0], z], axis=-1)
        bot = jnp.concatenate([z, m[:, 1]], axis=-1)
        return jnp.concatenate([top, bot], axis=1)

    mb_pair = jnp.concatenate([pair_cols(mb_f.real), pair_cols(mb_f.imag),
                               pair_cols(mb_b.real), pair_cols(mb_b.imag)], axis=-1)

    def pair_rows(m):
        m = m.reshape(q_n, 2, p_n, lc * h_n)
        z = jnp.zeros_like(m[:, 0])
        top = jnp.concatenate([m[:, 0], z], axis=-1)
        bot = jnp.concatenate([z, m[:, 1]], axis=-1)
        return jnp.concatenate([top, bot], axis=1)

    mc_pair = jnp.concatenate([pair_rows(mc_f.real), pair_rows(-mc_f.imag),
                               pair_rows(mc_b.real), pair_rows(-mc_b.imag)], axis=1)
    a_rows = jnp.stack([a_chunk[0].real, a_chunk[0].imag, a_chunk[1].real, a_chunk[1].imag], axis=0)
    a_rows = a_rows.reshape(4, q_n, 2 * p_n).transpose(1, 0, 2)
    a_rows = jnp.concatenate([a_rows, jnp.zeros_like(a_rows)], axis=1)
    return t_mat.astype(BF16), mb_pair.astype(BF16), mc_pair.astype(BF16), a_rows


def _s5_kernel(u_ref, uc_ref, t_ref, mb_ref, mc_ref, a_ref, y_ref, s_lat, s_ctx, h_scr, *, bsz, rb):
    n_lat = u_ref.shape[1] // bsz
    n_ctx = uc_ref.shape[1] // bsz
    sw = 2 * S5_P

    mb = mb_ref[0]

    def in_lat(i, carry):
        r = pl.multiple_of(i * rb, rb)
        s_lat[pl.ds(r, rb), :] = jnp.dot(u_ref[0, pl.ds(r, rb), :], mb, preferred_element_type=F32)
        return carry

    lax.fori_loop(0, u_ref.shape[1] // rb, in_lat, 0)
    s_ctx[...] = jnp.dot(uc_ref[0], mb, preferred_element_type=F32)

    a_fr, a_fi, a_br, a_bi = (a_ref[0, k:k + 1, :] for k in range(4))

    def step(h, a_r, a_i, s_r, s_i):
        h_r, h_i = h
        return a_r * h_r - a_i * h_i + s_r, a_r * h_i + a_i * h_r + s_i

    def ctx_step(k, carry):
        hf, hb = carry
        rf = pl.multiple_of(k * bsz, bsz)
        rbk = pl.multiple_of((n_ctx - 1 - k) * bsz, bsz)
        hf = step(hf, a_fr, a_fi, s_ctx[pl.ds(rf, bsz), 0:sw], s_ctx[pl.ds(rf, bsz), sw:2 * sw])
        hb = step(hb, a_br, a_bi, s_ctx[pl.ds(rbk, bsz), 2 * sw:3 * sw], s_ctx[pl.ds(rbk, bsz), 3 * sw:4 * sw])
        return hf, hb

    zero = jnp.zeros((bsz, sw), F32)
    carry = lax.fori_loop(0, n_ctx, ctx_step, ((zero, zero), (zero, zero)))

    def lat_step(k, carry):
        hf, hb = carry
        rf = pl.multiple_of(k * bsz, bsz)
        rbk = pl.multiple_of((n_lat - 1 - k) * bsz, bsz)
        h_scr[pl.ds(rf, bsz), 0:sw] = hf[0].astype(BF16)
        h_scr[pl.ds(rf, bsz), sw:2 * sw] = hf[1].astype(BF16)
        h_scr[pl.ds(rbk, bsz), 2 * sw:3 * sw] = hb[0].astype(BF16)
        h_scr[pl.ds(rbk, bsz), 3 * sw:4 * sw] = hb[1].astype(BF16)
        hf = step(hf, a_fr, a_fi, s_lat[pl.ds(rf, bsz), 0:sw], s_lat[pl.ds(rf, bsz), sw:2 * sw])
        hb = step(hb, a_br, a_bi, s_lat[pl.ds(rbk, bsz), 2 * sw:3 * sw], s_lat[pl.ds(rbk, bsz), 3 * sw:4 * sw])
        return hf, hb

    lax.fori_loop(0, n_lat, lat_step, carry)

    t0 = t_ref[0]
    t1 = t_ref[1]
    mc = mc_ref[0]
    half = S5_CHUNK * S5_H

    def out_lat(i, carry):
        r = pl.multiple_of(i * rb, rb)
        u = u_ref[0, pl.ds(r, rb), :]
        inter = jnp.dot(h_scr[pl.ds(r, rb), :], mc, preferred_element_type=F32)
        y0 = jnp.dot(u[:, :half], t0, preferred_element_type=F32) + inter[:, :half]
        y1 = jnp.dot(u[:, half:], t1, preferred_element_type=F32) + inter[:, half:]
        y_ref[0, pl.ds(r, rb), 0:half] = y0.astype(BF16)
        y_ref[0, pl.ds(r, rb), half:2 * half] = y1.astype(BF16)
        return carry

    lax.fori_loop(0, u_ref.shape[1] // rb, out_lat, 0)


def _s5_scan(u_t, uc_t, t_mat, mb_pair, mc_pair, a_rows, bsz):
    q_n, r_lat, w = u_t.shape
    r_ctx = uc_t.shape[1]
    rb = min(512, r_lat)
    return pl.pallas_call(
        functools.partial(_s5_kernel, bsz=bsz, rb=rb),
        grid=(q_n,),
        in_specs=[pl.BlockSpec((1, r_lat, w), lambda q: (q, 0, 0)),
                  pl.BlockSpec((1, r_ctx, w), lambda q: (q, 0, 0)),
                  pl.BlockSpec((2, w // 2, w // 2), lambda q: (q, 0, 0)),
                  pl.BlockSpec((1, w, w), lambda q: (q, 0, 0)),
                  pl.BlockSpec((1, w, w), lambda q: (q, 0, 0)),
                  pl.BlockSpec((1, 8, 2 * S5_P), lambda q: (q, 0, 0))],
        out_specs=pl.BlockSpec((1, r_lat, w), lambda q: (q, 0, 0)),
        out_shape=jax.ShapeDtypeStruct((q_n, r_lat, w), BF16),
        scratch_shapes=[pltpu.VMEM((r_lat, w), F32), pltpu.VMEM((r_ctx, w), F32), pltpu.VMEM((r_lat, w), BF16)],
        compiler_params=_params(("parallel",)),
        name="s5_scan",
    )(u_t, uc_t, t_mat, mb_pair, mc_pair, a_rows)


def _to_chunk_major(u, bsz):
    n = u.shape[1]
    nc = n // S5_CHUNK
    u = u.reshape(bsz, nc, S5_CHUNK, S5_PAIRS, 2, S5_H).transpose(3, 1, 0, 4, 2, 5)
    return u.reshape(S5_PAIRS, nc * bsz, 2 * S5_CHUNK * S5_H)


def _from_chunk_major(y, bsz):
    nc = y.shape[1] // bsz
    y = y.reshape(S5_PAIRS, nc, bsz, 2, S5_CHUNK, S5_H).transpose(2, 1, 4, 0, 3, 5)
    return y.reshape(bsz, nc * S5_CHUNK, S5_WIDTH)


def _pack_bf16_pair(a, b):
    ua = lax.bitcast_convert_type(a.astype(BF16).astype(F32), U32)
    ub = lax.bitcast_convert_type(b.astype(BF16).astype(F32), U32)
    return ua | (ub >> 16)


def _unpack_bf16_pair(w):
    a = lax.bitcast_convert_type(w & jnp.uint32(0xFFFF0000), F32)
    b = lax.bitcast_convert_type(w << 16, F32)
    return a, b


def _first_max(rows):
    best = rows[0]
    for r in rows[1:]:
        best = jnp.maximum(best, r)
    idx = jnp.full(best.shape, float(len(rows) - 1), F32)
    for k in range(len(rows) - 2, -1, -1):
        idx = jnp.where(rows[k] == best, float(k), idx)
    return best, idx


def _route_rows(lg):
    g_rows = [lg[k:k + 1] for k in range(N_GROUPS)]
    g_max, g_idx = _first_max(g_rows)
    g_sum = sum(jnp.exp(r - g_max) for r in g_rows)
    g_p = 1.0 / g_sum
    e_rows = []
    for j in range(EXPERTS_PER_GROUP):
        r = lg[N_GROUPS + (N_GROUPS - 1) * EXPERTS_PER_GROUP + j:][:1]
        for g in range(N_GROUPS - 2, -1, -1):
            k = N_GROUPS + g * EXPERTS_PER_GROUP + j
            r = jnp.where(g_idx == float(g), lg[k:k + 1], r)
        e_rows.append(r)
    v1, i1 = _first_max(e_rows)
    rest = [jnp.where(i1 == float(j), -jnp.inf, e_rows[j]) for j in range(EXPERTS_PER_GROUP)]
    v2, i2 = _first_max(rest)
    e21 = jnp.exp(v2 - v1)
    w1 = g_p / (1.0 + e21)
    w2 = w1 * e21
    lo = jnp.minimum(i1, i2)
    hi = jnp.maximum(i1, i2)
    base = jnp.where(lo == 0.0, 0.0, jnp.where(lo == 1.0, 3.0, 5.0))
    bucket = g_idx * float(PAIRS_PER_GROUP) + base + hi - lo - 1.0
    first_is_lo = i1 < i2
    return bucket, jnp.where(first_is_lo, w1, w2), jnp.where(first_is_lo, w2, w1)


def _mix_kernel(x_ref, y_ref, bg_ref, cv_ref, cvp_ref, cvn_ref, mod_ref, g2_ref, cw_ref, wglu_ref, bglu_ref,
                wout_ref, wr_ref, br_ref, x1_ref, h2p_ref, route_ref, counts_ref):
    i = pl.program_id(1)
    tm = x_ref.shape[1]
    d = x_ref.shape[2]
    g = jax.nn.gelu(y_ref[0].astype(F32))
    glu = g * jax.nn.sigmoid(jnp.dot(g.astype(BF16), wglu_ref[...], preferred_element_type=F32) + bglu_ref[...])
    cv = cv_ref[0].astype(F32)
    row = lax.broadcasted_iota(jnp.int32, (tm, 1), 0)
    col_in_row = row % GRID_W
    cr = cv[:, :CONV_ROW_WIDTH]
    left = jnp.where(col_in_row == 0, 0.0, pltpu.roll(cr, 1, axis=0))
    right = jnp.where(col_in_row == GRID_W - 1, 0.0, pltpu.roll(cr, tm - 1, axis=0))
    w_r = cw_ref[:, :CONV_ROW_WIDTH]
    row_part = left * w_r[0:1] + cr * w_r[1:2] + right * w_r[2:3]
    cc = cv[:, CONV_ROW_WIDTH:]
    up_halo = jnp.where(i == 0, 0.0, cvp_ref[0].astype(F32))
    dn_halo = jnp.where(i == pl.num_programs(1) - 1, 0.0, cvn_ref[0].astype(F32))
    up = jnp.concatenate([up_halo, cc[:tm - GRID_W]], axis=0)
    dn = jnp.concatenate([cc[GRID_W:], dn_halo], axis=0)
    w_c = cw_ref[:, CONV_ROW_WIDTH:]
    col_part = up * w_c[0:1] + cc * w_c[1:2] + dn * w_c[2:3]
    bg = bg_ref[0].astype(F32)
    y_row = (bg[:, :CONV_ROW_WIDTH] * row_part).astype(BF16)
    y_col = (bg[:, CONV_ROW_WIDTH:] * col_part).astype(BF16)
    o1 = S5_WIDTH
    o2 = S5_WIDTH + CONV_ROW_WIDTH
    yx = (jnp.dot(glu.astype(BF16), wout_ref[0:o1, :], preferred_element_type=F32)
          + jnp.dot(y_row, wout_ref[o1:o2, :], preferred_element_type=F32)
          + jnp.dot(y_col, wout_ref[o2:, :], preferred_element_type=F32))
    x1 = x_ref[0] + mod_ref[0, 2:3, :] * yx
    x1_ref[0] = x1
    h2 = _modulated_norm(x1, g2_ref[...], mod_ref[0, 3:4, :], mod_ref[0, 4:5, :])
    h2b = h2.astype(BF16)
    lg2 = lax.dot_general(wr_ref[...], h2b, (((1,), (1,)), ((), ())), preferred_element_type=F32)
    lg = lg2[:ROUTER_ROWS] + lg2[ROUTER_ROWS:] + br_ref[...]
    bucket, w_a, w_b = _route_rows(lg)
    r8 = lax.broadcasted_iota(jnp.int32, (8, tm), 0)
    route_ref[...] = jnp.where(r8 == 0, bucket, jnp.where(r8 == 1, w_a, jnp.where(r8 == 2, w_b, 0.0)))
    rl = lax.broadcasted_iota(jnp.int32, (ROW_EXTRA, tm), 0)
    gates_t = jnp.where(rl == 0, w_a, jnp.where(rl == 1, w_b, 0.0))
    h2p_ref[0, :, 0:d // 2] = _pack_bf16_pair(h2[:, :d // 2], h2[:, d // 2:])
    h2p_ref[0, :, d // 2:] = lax.bitcast_convert_type(gates_t.T, U32)
    rb = lax.broadcasted_iota(jnp.int32, (BUCKET_ROWS, tm), 0).astype(F32)
    cnt = jnp.sum(jnp.where(rb == bucket, 1.0, 0.0), axis=-1, keepdims=True)

    @pl.when(jnp.logical_and(pl.program_id(0) == 0, i == 0))
    def _():
        counts_ref[...] = jnp.zeros_like(counts_ref)

    counts_ref[...] += cnt


def _mix(x, y_s5, bg, cv, mods, norm2_g, conv_w, w_glu, b_glu, w_out, w_router_t, b_router_t, tm):
    bsz, n, d = x.shape
    nt = n // tm
    halo_blocks = n // GRID_W
    per_tile = tm // GRID_W
    tok = lambda w: pl.BlockSpec((1, tm, w), lambda b, i: (b, i, 0))
    full = lambda a: pl.BlockSpec(a.shape, lambda b, i: (0,) * a.ndim)
    args = (x, y_s5, bg, cv, cv, cv, mods, norm2_g.reshape(1, d), conv_w, w_glu, b_glu.reshape(1, -1),
            w_out, w_router_t, b_router_t)
    in_specs = [tok(d), tok(S5_WIDTH), tok(CONV_WIDTH), tok(CONV_WIDTH),
                pl.BlockSpec((1, GRID_W, CONV_ROW_WIDTH),
                             lambda b, i: (b, jnp.maximum(i * per_tile - 1, 0), 1)),
                pl.BlockSpec((1, GRID_W, CONV_ROW_WIDTH),
                             lambda b, i: (b, jnp.minimum((i + 1) * per_tile, halo_blocks - 1), 1)),
                pl.BlockSpec((1, N_MOD, d), lambda b, i: (b, 0, 0))] + [full(a) for a in args[7:]]
    return pl.pallas_call(
        _mix_kernel,
        grid=(bsz, nt),
        in_specs=in_specs,
        out_specs=[tok(d), tok(d // 2 + ROW_EXTRA),
                   pl.BlockSpec((8, tm), lambda b, i: (0, b * nt + i)),
                   pl.BlockSpec((BUCKET_ROWS, 128), lambda b, i: (0, 0))],
        out_shape=[jax.ShapeDtypeStruct((bsz, n, d), F32),
                   jax.ShapeDtypeStruct((bsz, n, d // 2 + ROW_EXTRA), U32),
                   jax.ShapeDtypeStruct((8, bsz * n), F32),
                   jax.ShapeDtypeStruct((BUCKET_ROWS, 128), F32)],
        compiler_params=_params(("arbitrary", "arbitrary")),
        name="mix",
    )(*args)


def _rank_kernel(route_ref, offs_ref, dest_ref, run_ref):
    tr = route_ref.shape[1]

    @pl.when(pl.program_id(0) == 0)
    def _():
        run_ref[...] = jnp.zeros_like(run_ref)

    bucket = route_ref[0:1, :]
    rb = lax.broadcasted_iota(jnp.int32, (BUCKET_ROWS, tr), 0).astype(F32)
    onehot = jnp.where(rb == bucket, 1.0, 0.0)
    s_idx = lax.broadcasted_iota(jnp.int32, (tr, tr), 0)
    t_idx = lax.broadcasted_iota(jnp.int32, (tr, tr), 1)
    tri = jnp.where(s_idx <= t_idx, 1.0, 0.0).astype(BF16)
    prefix = jnp.dot(onehot.astype(BF16), tri, preferred_element_type=F32)
    before = run_ref[:, 0:1] + offs_ref[:, 0:1]
    dest = jnp.sum(onehot * (prefix - 1.0 + before), axis=0, keepdims=True)
    dest_ref[...] = dest.astype(jnp.int32)
    run_ref[...] += prefix[:, tr - 1:tr]


def _rank(route_t, offs_rows, tr):
    n = route_t.shape[1]
    return pl.pallas_call(
        _rank_kernel,
        grid=(n // tr,),
        in_specs=[pl.BlockSpec((8, tr), lambda i: (0, i)),
                  pl.BlockSpec((BUCKET_ROWS, 128), lambda i: (0, 0))],
        out_specs=pl.BlockSpec((1, tr), lambda i: (0, i)),
        out_shape=jax.ShapeDtypeStruct((1, n), jnp.int32),
        scratch_shapes=[pltpu.VMEM((BUCKET_ROWS, 128), F32)],
        compiler_params=_params(("arbitrary",)),
        name="rank",
    )(route_t, offs_rows)


def _sc_mesh():
    return plsc.VectorSubcoreMesh(core_axis_name="core", subcore_axis_name="subcore")


def _scatter_rows(src, dest, n_out):
    n, w = src.shape
    sub = SC_INDEX_TILE // SC_ROWS

    @functools.partial(pl.kernel, out_type=jax.ShapeDtypeStruct((n_out, w), src.dtype), mesh=_sc_mesh(),
                       scratch_types=[], name="scatter_rows")
    def scatter(x_hbm, i_hbm, o_hbm):
        def body(x_vmem, i_vmem):
            j = pl.program_id(1)
            pltpu.sync_copy(x_vmem, o_hbm.at[i_vmem.at[0, pl.ds(j * SC_ROWS, SC_ROWS)]])

        pltpu.emit_pipeline(
            body,
            grid=(n // SC_INDEX_TILE, sub),
            in_specs=[pl.BlockSpec((SC_ROWS, w), lambda i, j: (i * sub + j, 0)),
                      pl.BlockSpec((1, SC_INDEX_TILE), lambda i, j: (0, i))],
            out_specs=[],
            core_axis_name=("core", "subcore"),
            dimension_semantics=(pltpu.PARALLEL, pltpu.ARBITRARY),
        )(x_hbm, i_hbm)

    return scatter(src, dest)


def _gather_rows(src, idx):
    n = idx.shape[1]
    w = src.shape[1]
    sub = SC_INDEX_TILE // SC_ROWS

    @functools.partial(pl.kernel, out_type=jax.ShapeDtypeStruct((n, w), src.dtype), mesh=_sc_mesh(),
                       scratch_types=[], name="gather_rows")
    def gather(x_hbm, i_hbm, o_hbm):
        def body(i_vmem, o_vmem):
            j = pl.program_id(1)
            pltpu.sync_copy(x_hbm.at[i_vmem.at[0, pl.ds(j * SC_ROWS, SC_ROWS)]], o_vmem)

        pltpu.emit_pipeline(
            body,
            grid=(n // SC_INDEX_TILE, sub),
            in_specs=[pl.BlockSpec((1, SC_INDEX_TILE), lambda i, j: (0, i))],
            out_specs=[pl.BlockSpec((SC_ROWS, w), lambda i, j: (i * sub + j, 0))],
            core_axis_name=("core", "subcore"),
            dimension_semantics=(pltpu.PARALLEL, pltpu.ARBITRARY),
        )(i_hbm, o_hbm)

    return gather(src, idx)


def _moe_kernel(ea_ref, eb_ref, valid_ref, new_ref, rows_ref, w1a_ref, w3a_ref, w2a_ref, w1b_ref, w3b_ref, w2b_ref,
                o_ref, *w_scr):
    j = pl.program_id(0)
    half = rows_ref.shape[1] - ROW_EXTRA

    @pl.when(new_ref[j] != 0)
    def _():
        for src, dst in zip((w1a_ref, w3a_ref, w2a_ref, w1b_ref, w3b_ref, w2b_ref), w_scr):
            dst[...] = src[0].astype(BF16)

    @pl.when(valid_ref[j] != 0)
    def _():
        ha, hb = _unpack_bf16_pair(rows_ref[:, 0:half])
        ha = ha.astype(BF16)
        hb = hb.astype(BF16)
        gates = lax.bitcast_convert_type(rows_ref[:, half:], F32)

        def expert(w1_s, w3_s, w2_s, gate):
            def up(w_s):
                return (jnp.dot(ha, w_s[0:half, :], preferred_element_type=F32)
                        + jnp.dot(hb, w_s[half:, :], preferred_element_type=F32))

            a1 = up(w1_s)
            he = (a1 * jax.nn.sigmoid(a1)) * up(w3_s) * gate
            return jnp.dot(he.astype(BF16), w2_s[...], preferred_element_type=F32)

        y = expert(*w_scr[0:3], gates[:, 0:1]) + expert(*w_scr[3:6], gates[:, 1:2])
        o_ref[...] = _pack_bf16_pair(y[:, :half], y[:, half:])


def _moe_grouped(rows, tile_ea, tile_eb, tile_valid, tile_new, w1, w3, w2, tmm):
    r, w = rows.shape
    n_e, de, d = w2.shape
    up_a = pl.BlockSpec((1, d, de), lambda j, ea, eb, va, nw: (ea[j], 0, 0))
    up_b = pl.BlockSpec((1, d, de), lambda j, ea, eb, va, nw: (eb[j], 0, 0))
    grid_spec = pltpu.PrefetchScalarGridSpec(
        num_scalar_prefetch=4,
        grid=(r // tmm,),
        in_specs=[pl.BlockSpec((tmm, w), lambda j, ea, eb, va, nw: (j, 0)),
                  up_a, up_a, pl.BlockSpec((1, de, d), lambda j, ea, eb, va, nw: (ea[j], 0, 0)),
                  up_b, up_b, pl.BlockSpec((1, de, d), lambda j, ea, eb, va, nw: (eb[j], 0, 0))],
        out_specs=pl.BlockSpec((tmm, d // 2), lambda j, ea, eb, va, nw: (j, 0)),
        scratch_shapes=[pltpu.VMEM((d, de), BF16), pltpu.VMEM((d, de), BF16), pltpu.VMEM((de, d), BF16)] * 2,
    )
    return pl.pallas_call(
        _moe_kernel,
        grid_spec=grid_spec,
        out_shape=jax.ShapeDtypeStruct((r, d // 2), U32),
        compiler_params=_params(("arbitrary",)),
        name="moe",
    )(tile_ea, tile_eb, tile_valid, tile_new, rows, w1, w3, w2, w1, w3, w2)


def _final_kernel(x1_ref, moe_ref, mod_ref, fg_ref, o_ref):
    ya, yb = _unpack_bf16_pair(moe_ref[0])
    half = ya.shape[1]
    gate = mod_ref[0, 5:6, :]
    xa = x1_ref[0, :, 0:half] + gate[:, 0:half] * ya
    xb = x1_ref[0, :, half:] + gate[:, half:] * yb
    ms = (jnp.sum(xa * xa, axis=-1, keepdims=True) + jnp.sum(xb * xb, axis=-1, keepdims=True)) / (2 * half)
    inv = lax.rsqrt(ms + RMS_EPS)
    o_ref[0, :, 0:half] = xa * inv * fg_ref[:, 0:half]
    o_ref[0, :, half:] = xb * inv * fg_ref[:, half:]


def _final(x1, moe_tok, mods, final_g, tm):
    bsz, n, d = x1.shape
    tok = lambda w: pl.BlockSpec((1, tm, w), lambda b, i: (b, i, 0))
    return pl.pallas_call(
        _final_kernel,
        grid=(bsz, n // tm),
        in_specs=[tok(d), tok(d // 2), pl.BlockSpec((1, N_MOD, d), lambda b, i: (b, 0, 0)),
                  pl.BlockSpec((1, d), lambda b, i: (0, 0))],
        out_specs=tok(d),
        out_shape=jax.ShapeDtypeStruct((bsz, n, d), F32),
        compiler_params=_params(("parallel", "parallel")),
        name="final",
    )(x1, moe_tok, mods, final_g.reshape(1, d))


def _tile_plan(counts, tmm, n_tiles):
    tiles = (counts + (tmm - 1)) // tmm
    tile_end = jnp.cumsum(tiles)
    offs = (tile_end - tiles) * tmm
    n_valid = tile_end[-1]
    j = jnp.arange(n_tiles, dtype=jnp.int32)
    bucket = jnp.sum((tile_end[None, :] <= jnp.minimum(j, n_valid - 1)[:, None]).astype(jnp.int32), axis=1)
    pair_lo = jnp.array([0, 0, 0, 1, 1, 2], jnp.int32)
    pair_hi = jnp.array([1, 2, 3, 2, 3, 3], jnp.int32)
    group = bucket // PAIRS_PER_GROUP
    pair = bucket % PAIRS_PER_GROUP
    tile_ea = group * EXPERTS_PER_GROUP + pair_lo[pair]
    tile_eb = group * EXPERTS_PER_GROUP + pair_hi[pair]
    tile_new = jnp.concatenate([jnp.ones((1,), jnp.int32), (bucket[1:] != bucket[:-1]).astype(jnp.int32)])
    return offs, tile_ea, tile_eb, (j < n_valid).astype(jnp.int32), tile_new


def kernel(x, c, ctx, c_ctx, w_mod, b_mod, norm1_g, norm2_g, w_in, s5_lambda_re, s5_lambda_im, s5_log_dt,
           s5_b_re, s5_b_im, s5_c_re, s5_c_im, s5_d, w_glu, b_glu, conv_w, w_out, router_group_w,
           router_group_b, router_expert_w, router_expert_b, expert_w1, expert_w3, expert_w2, final_g):
    assert w_mod.shape[0] == 1, "single-layer kernel"
    bsz, n_tok, d = x.shape
    n_ctx = ctx.shape[1]
    n_all = bsz * n_tok
    l = 0
    tm = min(TOKEN_TILE, n_tok)

    n_cond = bsz + 1
    pad = (-n_cond) % 8
    cond = jnp.concatenate([c, c_ctx[None, :], jnp.zeros((pad, d), F32)], axis=0)
    m = _mod_rows(cond, w_mod[l], b_mod[l])
    mx = m[:bsz].reshape(bsz, N_MOD, d)
    mc = m[bsz:bsz + 1].reshape(1, N_MOD, d)

    w_in_b = w_in[l].astype(BF16)
    u, bg, cv = _inproj(x, mx, True, norm1_g[l], w_in_b, tm, True)
    (uc,) = _inproj(ctx, mc, False, norm1_g[l], w_in_b[:, :S5_WIDTH], min(TOKEN_TILE, n_ctx), False)

    t_mat, mb_pair, mc_pair, a_rows = _s5_matrices(
        s5_lambda_re[l], s5_lambda_im[l], s5_log_dt[l], s5_b_re[l], s5_b_im[l], s5_c_re[l], s5_c_im[l], s5_d[l])
    y_t = _s5_scan(_to_chunk_major(u, bsz), _to_chunk_major(uc, bsz), t_mat, mb_pair, mc_pair, a_rows, bsz)
    y_s5 = _from_chunk_major(y_t, bsz)

    n_logits = N_GROUPS + N_EXPERTS
    w_router = jnp.concatenate(
        [router_group_w[l], router_expert_w[l], jnp.zeros((d, ROUTER_ROWS - n_logits), F32)], axis=1).T
    w_router_hi = w_router.astype(BF16)
    w_router_lo = (w_router - w_router_hi.astype(F32)).astype(BF16)
    w_router_t = jnp.concatenate([w_router_hi, w_router_lo], axis=0)
    b_router = jnp.concatenate([router_group_b[l], router_expert_b[l], jnp.zeros((ROUTER_ROWS - n_logits,), F32)])
    b_router_t = jnp.broadcast_to(b_router[:, None], (ROUTER_ROWS, tm))

    x1, h2p, route_t, counts = _mix(x, y_s5, bg, cv, mx, norm2_g[l], conv_w[l], w_glu[l].astype(BF16), b_glu[l],
                                    w_out[l].astype(BF16), w_router_t, b_router_t, tm)

    n_buckets = N_GROUPS * PAIRS_PER_GROUP
    n_rows = n_all + n_buckets * MOE_TILE
    offs, tile_ea, tile_eb, tile_valid, tile_new = _tile_plan(counts[:n_buckets, 0].astype(jnp.int32), MOE_TILE,
                                                              n_rows // MOE_TILE)
    offs_rows = jnp.zeros((BUCKET_ROWS,), F32).at[:n_buckets].set(offs.astype(F32))
    dest = _rank(route_t, jnp.broadcast_to(offs_rows[:, None], (BUCKET_ROWS, 128)), min(RANK_TILE, n_all))

    rows = _scatter_rows(h2p.reshape(n_all, d // 2 + ROW_EXTRA), dest, n_rows)
    y_rows = _moe_grouped(rows, tile_ea, tile_eb, tile_valid, tile_new, expert_w1[l], expert_w3[l], expert_w2[l],
                          MOE_TILE)
    moe_tok = _gather_rows(y_rows, dest).reshape(bsz, n_tok, d // 2)
    return _final(x1, moe_tok, mx, final_g, tm)
```

```python
import functools

import jax
import jax.numpy as jnp
from jax import lax
from jax.experimental import pallas as pl
from jax.experimental.pallas import tpu as pltpu
from jax.experimental.pallas import tpu_sc as plsc

F32 = jnp.float32
BF16 = jnp.bfloat16
U32 = jnp.uint32

RMS_EPS = 1e-6
N_MOD = 6
GRID_W = 64
S5_WIDTH = 256
S5_H = 16
S5_P = 64
S5_GROUPS = S5_WIDTH // S5_H
S5_PAIRS = S5_GROUPS // 2
S5_CHUNK = 16
LANES = 128
STATE_PITCH = 136
CONV_WIDTH = 768
CONV_ROW_WIDTH = CONV_WIDTH // 2
N_GROUPS = 4
EXPERTS_PER_GROUP = 4
N_EXPERTS = N_GROUPS * EXPERTS_PER_GROUP
PAIRS_PER_GROUP = 6
ROUTER_ROWS = 32
BUCKET_ROWS = 32
ROW_EXTRA = 128
TOKEN_TILE = 512
MOE_TILE = 512
RANK_TILE = 2048
SC_ROWS = 32
SC_INDEX_TILE = 128
VMEM_LIMIT = 52 * 1024 * 1024


def _params(sem, vmem=VMEM_LIMIT):
    return pltpu.CompilerParams(dimension_semantics=sem, vmem_limit_bytes=vmem)


def _mod_kernel(c_ref, w_ref, b_ref, o_ref):
    c = c_ref[...]
    o_ref[...] = jnp.dot(c * jax.nn.sigmoid(c), w_ref[...], preferred_element_type=F32) + b_ref[...]


def _mod_rows(cond, w_mod, b_mod):
    n, d = cond.shape
    nout = w_mod.shape[1]
    bn = d
    return pl.pallas_call(
        _mod_kernel,
        grid=(nout // bn,),
        in_specs=[pl.BlockSpec((n, d), lambda j: (0, 0)),
                  pl.BlockSpec((d, bn), lambda j: (0, j)),
                  pl.BlockSpec((1, bn), lambda j: (0, j))],
        out_specs=pl.BlockSpec((n, bn), lambda j: (0, j)),
        out_shape=jax.ShapeDtypeStruct((n, nout), F32),
        compiler_params=_params(("arbitrary",)),
        name="mod",
    )(cond, w_mod, b_mod.reshape(1, nout))


def _modulated_norm(x, g, shift, scale):
    ms = jnp.mean(x * x, axis=-1, keepdims=True)
    return (x * lax.rsqrt(ms + RMS_EPS) * g) * (1.0 + scale) + shift


def _inproj_kernel(x_ref, mod_ref, g_ref, w_ref, u_ref, *conv_refs):
    h = _modulated_norm(x_ref[0], g_ref[...], mod_ref[0, 0:1, :], mod_ref[0, 1:2, :]).astype(BF16)
    u = jnp.dot(h, w_ref[:, 0:S5_WIDTH], preferred_element_type=F32)
    for j in range(S5_WIDTH // LANES):
        u_ref[0, j] = u[:, j * LANES:(j + 1) * LANES]
    if conv_refs:
        bg_ref, cv_ref = conv_refs
        o = S5_WIDTH
        bg_ref[0] = jnp.dot(h, w_ref[:, o:o + CONV_WIDTH], preferred_element_type=F32).astype(BF16)
        c_g = jnp.dot(h, w_ref[:, o + CONV_WIDTH:o + 2 * CONV_WIDTH], preferred_element_type=F32)
        v = jnp.dot(h, w_ref[:, o + 2 * CONV_WIDTH:o + 3 * CONV_WIDTH], preferred_element_type=F32)
        cv_ref[0] = (c_g * v).astype(BF16)


def _inproj(x, mods, per_batch_mod, norm_g, w_bf16, tm, with_conv):
    bsz, n, d = x.shape
    nw = w_bf16.shape[1]
    mod_map = (lambda b, i: (b, 0, 0)) if per_batch_mod else (lambda b, i: (0, 0, 0))
    n_slab = S5_WIDTH // LANES
    out_shape = [jax.ShapeDtypeStruct((bsz, n_slab, n, LANES), F32)]
    out_specs = [pl.BlockSpec((1, n_slab, tm, LANES), lambda b, i: (b, 0, i, 0))]
    if with_conv:
        out_shape += [jax.ShapeDtypeStruct((bsz, n, CONV_WIDTH), BF16)] * 2
        out_specs += [pl.BlockSpec((1, tm, CONV_WIDTH), lambda b, i: (b, i, 0))] * 2
    return pl.pallas_call(
        _inproj_kernel,
        grid=(bsz, n // tm),
        in_specs=[pl.BlockSpec((1, tm, d), lambda b, i: (b, i, 0)),
                  pl.BlockSpec((1, N_MOD, d), mod_map),
                  pl.BlockSpec((1, d), lambda b, i: (0, 0)),
                  pl.BlockSpec((d, nw), lambda b, i: (0, 0))],
        out_specs=out_specs,
        out_shape=out_shape,
        compiler_params=_params(("parallel", "parallel")),
        name="inproj_conv" if with_conv else "inproj_ctx",
    )(x, mods, norm_g.reshape(1, d), w_bf16)


def _s5_matrices(lam_re, lam_im, log_dt, b_re, b_im, c_re, c_im, d_skip):
    lc, g_n, p_n, h_n = S5_CHUNK, S5_GROUPS, S5_P, S5_H
    lam = lax.complex(lam_re.astype(F32), lam_im.astype(F32))
    dt = jnp.exp(log_dt.astype(F32))[..., None]
    a_bar = jnp.exp(lam * dt)
    b_bar = ((a_bar - 1.0) / lam)[..., None] * lax.complex(b_re.astype(F32), b_im.astype(F32))
    cm = lax.complex(c_re.astype(F32), c_im.astype(F32))
    steps = jnp.arange(lc + 1, dtype=F32)
    apow = jnp.exp((lam * dt)[:, :, None, :] * steps[None, None, :, None])
    s_idx = jnp.arange(lc, dtype=F32)[:, None]
    t_idx = jnp.arange(lc, dtype=F32)[None, :]
    lag = jnp.stack([t_idx - s_idx, s_idx - t_idx])
    causal = (lag >= 0).astype(F32)
    lag_pow = (jnp.exp((lam * dt)[:, :, None, None, :] * jnp.maximum(lag, 0.0)[:, None, :, :, None])
               * causal[:, None, :, :, None])
    resp = jnp.einsum('dgop,dgstp,dgpi->gsito', cm, lag_pow, b_bar).real
    skip = (jnp.eye(lc, dtype=F32)[None, :, None, :, None]
            * (jnp.eye(h_n, dtype=F32) * d_skip.astype(F32).reshape(g_n, 1, h_n))[:, None, :, None, :])
    t_mat = (resp + skip).reshape(g_n, lc * h_n, lc * h_n)

    def in_mat(pw, bb):
        return (pw[:, :, None, :] * bb.transpose(0, 2, 1)[:, None, :, :]).reshape(g_n, lc * h_n, p_n)

    mb_f = in_mat(apow[0, :, lc - 1::-1][:, :lc], b_bar[0])
    mb_b = in_mat(apow[1, :, :lc], b_bar[1])

    def out_mat(pw, cc):
        return (pw.transpose(0, 2, 1)[:, :, :, None] * cc.transpose(0, 2, 1)[:, :, None, :]).reshape(
            g_n, p_n, lc * h_n)

    mc_f = out_mat(apow[0, :, 1:lc + 1], cm[0])
    mc_b = out_mat(apow[1, :, lc:0:-1], cm[1])
    a_chunk = apow[:, :, lc]

    q_n = S5_PAIRS
    zeros_in = jnp.zeros((g_n, lc * h_n, p_n), F32)

    def pair_cols(m):
        m = m.reshape(q_n, 2, lc * h_n, p_n)
        z = zeros_in.reshape(q_n, 2, lc * h_n, p_n)[:, 0]
        top = jnp.concatenate([m[:, 0], z], axis=-1)
        bot = jnp.concatenate([z, m[:, 1]], axis=-1)
        return jnp.concatenate([top, bot], axis=1)

    mb_pair = jnp.concatenate([pair_cols(mb_f.real), pair_cols(mb_f.imag),
                               pair_cols(mb_b.real), pair_cols(mb_b.imag)], axis=-1)

    def pair_rows(m):
        m = m.reshape(q_n, 2, p_n, lc * h_n)
        z = jnp.zeros_like(m[:, 0])
        top = jnp.concatenate([m[:, 0], z], axis=-1)
        bot = jnp.concatenate([z, m[:, 1]], axis=-1)
        return jnp.concatenate([top, bot], axis=1)

    mc_pair = jnp.concatenate([pair_rows(mc_f.real), pair_rows(-mc_f.imag),
                               pair_rows(mc_b.real), pair_rows(-mc_b.imag)], axis=1)
    a_rows = jnp.stack([a_chunk[0].real, a_chunk[0].imag, a_chunk[1].real, a_chunk[1].imag], axis=0)
    a_rows = a_rows.reshape(4, q_n, 2 * p_n).transpose(1, 0, 2)
    a_rows = jnp.concatenate([a_rows, jnp.zeros_like(a_rows)], axis=1)
    return t_mat.astype(BF16), mb_pair.astype(BF16), mc_pair.astype(BF16), a_rows


def _chunkify_kernel(u_ref, o_ref):
    nc = o_ref.shape[2]
    half = S5_CHUNK * S5_H
    per_slab = LANES // S5_H
    for j in range(u_ref.shape[1]):
        cols = [u_ref[0, j, pl.ds(t, nc, stride=S5_CHUNK), :].T for t in range(S5_CHUNK)]
        for gl in range(per_slab):
            g = j * per_slab + gl
            m = jnp.concatenate([c[gl * S5_H:(gl + 1) * S5_H, :] for c in cols], axis=0)
            o_ref[g // 2, 0, :, (g % 2) * half:(g % 2 + 1) * half] = m.T.astype(BF16)


def _chunkify(u_slab):
    bsz, n_slab, n, _ = u_slab.shape
    nc = n // S5_CHUNK
    w = 2 * S5_CHUNK * S5_H
    return pl.pallas_call(
        _chunkify_kernel,
        grid=(bsz,),
        in_specs=[pl.BlockSpec((1, n_slab, n, LANES), lambda b: (b, 0, 0, 0))],
        out_specs=pl.BlockSpec((S5_PAIRS, 1, nc, w), lambda b: (0, b, 0, 0)),
        out_shape=jax.ShapeDtypeStruct((S5_PAIRS, bsz, nc, w), BF16),
        compiler_params=_params(("parallel",)),
        name="chunkify",
    )(u_slab)


def _unchunkify_kernel(y_ref, o_ref):
    nc = y_ref.shape[2]
    half = S5_CHUNK * S5_H
    per_slab = LANES // S5_H
    for j in range(o_ref.shape[1]):
        rows = []
        for gl in range(per_slab):
            g = j * per_slab + gl
            rows.append(y_ref[g // 2, 0, :, (g % 2) * half:(g % 2 + 1) * half].astype(F32).T)
        for t in range(S5_CHUNK):
            bt = jnp.concatenate([r[t * S5_H:(t + 1) * S5_H, :] for r in rows], axis=0)
            o_ref[0, j, pl.ds(t, nc, stride=S5_CHUNK), :] = bt.T


def _unchunkify(y_c, n_slab):
    q_n, bsz, nc, w = y_c.shape
    n = nc * S5_CHUNK
    return pl.pallas_call(
        _unchunkify_kernel,
        grid=(bsz,),
        in_specs=[pl.BlockSpec((q_n, 1, nc, w), lambda b: (0, b, 0, 0))],
        out_specs=pl.BlockSpec((1, n_slab, n, LANES), lambda b: (b, 0, 0, 0)),
        out_shape=jax.ShapeDtypeStruct((bsz, n_slab, n, LANES), F32),
        compiler_params=_params(("parallel",)),
        name="unchunkify",
    )(y_c)


def _chunkify_context(uc_slab):
    bsz, n_slab, n, _ = uc_slab.shape
    nc = n // S5_CHUNK
    per_slab = LANES // S5_H
    u = uc_slab.astype(BF16).reshape(bsz, n_slab, nc, S5_CHUNK, per_slab // 2, 2, S5_H)
    return u.transpose(1, 4, 0, 2, 5, 3, 6).reshape(S5_PAIRS, bsz, nc, 2 * S5_CHUNK * S5_H)


def _s5_kernel(u_ref, uc_ref, t_ref, mb_ref, mc_ref, a_ref, y_ref, s_lat, s_ctx, h_scr, *, bb):
    _, bsz, n_lat, w = u_ref.shape
    n_ctx = uc_ref.shape[2]
    n_blk = w // LANES
    rb = bb * n_lat

    mb = mb_ref[0]

    def in_lat(i, carry):
        s = jnp.dot(u_ref[0, pl.ds(i * bb, bb)].reshape(rb, w), mb, preferred_element_type=F32)
        for k in range(bb):
            r = pl.multiple_of((i * bb + k) * STATE_PITCH, 8)
            for blk in range(n_blk):
                s_lat[blk, pl.ds(r, n_lat), :] = s[k * n_lat:(k + 1) * n_lat, blk * LANES:(blk + 1) * LANES]
        return carry

    lax.fori_loop(0, bsz // bb, in_lat, 0)
    sc = jnp.dot(uc_ref[0].reshape(bsz * n_ctx, w), mb, preferred_element_type=F32)
    for blk in range(n_blk):
        s_ctx[blk] = sc[:, blk * LANES:(blk + 1) * LANES]

    a_fr, a_fi, a_br, a_bi = (a_ref[0, k:k + 1, :] for k in range(4))

    def step(h, a_r, a_i, s_r, s_i):
        h_r, h_i = h
        return a_r * h_r - a_i * h_i + s_r, a_r * h_i + a_i * h_r + s_i

    def ctx_rows(blk, c):
        return s_ctx[blk, pl.ds(c, bsz, stride=n_ctx), :]

    def lat_rows(ref, blk, c):
        return ref.at[blk, pl.ds(c, bsz, stride=STATE_PITCH), :]

    def ctx_step(k, carry):
        hf, hb = carry
        kb = n_ctx - 1 - k
        hf = step(hf, a_fr, a_fi, ctx_rows(0, k), ctx_rows(1, k))
        hb = step(hb, a_br, a_bi, ctx_rows(2, kb), ctx_rows(3, kb))
        return hf, hb

    zero = jnp.zeros((bsz, LANES), F32)
    carry = lax.fori_loop(0, n_ctx, ctx_step, ((zero, zero), (zero, zero)))

    def lat_step(k, carry):
        hf, hb = carry
        kb = n_lat - 1 - k
        lat_rows(h_scr, 0, k)[...] = hf[0]
        lat_rows(h_scr, 1, k)[...] = hf[1]
        lat_rows(h_scr, 2, kb)[...] = hb[0]
        lat_rows(h_scr, 3, kb)[...] = hb[1]
        hf = step(hf, a_fr, a_fi, lat_rows(s_lat, 0, k)[...], lat_rows(s_lat, 1, k)[...])
        hb = step(hb, a_br, a_bi, lat_rows(s_lat, 2, kb)[...], lat_rows(s_lat, 3, kb)[...])
        return hf, hb

    lax.fori_loop(0, n_lat, lat_step, carry)

    t0 = t_ref[0]
    t1 = t_ref[1]
    mc = mc_ref[0]
    half = S5_CHUNK * S5_H

    def out_lat(i, carry):
        u = u_ref[0, pl.ds(i * bb, bb)].reshape(rb, w)
        h_rows = []
        for k in range(bb):
            r = pl.multiple_of((i * bb + k) * STATE_PITCH, 8)
            h_rows.append(jnp.concatenate([h_scr[blk, pl.ds(r, n_lat), :] for blk in range(n_blk)], axis=1))
        h = jnp.concatenate(h_rows, axis=0).astype(BF16)
        inter = jnp.dot(h, mc, preferred_element_type=F32)
        y0 = jnp.dot(u[:, :half], t0, preferred_element_type=F32) + inter[:, :half]
        y1 = jnp.dot(u[:, half:], t1, preferred_element_type=F32) + inter[:, half:]
        y = jnp.concatenate([y0, y1], axis=1).astype(BF16)
        y_ref[0, pl.ds(i * bb, bb)] = y.reshape(bb, n_lat, w)
        return carry

    lax.fori_loop(0, bsz // bb, out_lat, 0)


def _s5_scan(u_c, uc_c, t_mat, mb_pair, mc_pair, a_rows):
    q_n, bsz, n_lat, w = u_c.shape
    n_ctx = uc_c.shape[2]
    assert n_lat + 8 == STATE_PITCH
    bb = min(4, bsz)
    n_blk = w // LANES
    return pl.pallas_call(
        functools.partial(_s5_kernel, bb=bb),
        grid=(q_n,),
        in_specs=[pl.BlockSpec((1, bsz, n_lat, w), lambda q: (q, 0, 0, 0)),
                  pl.BlockSpec((1, bsz, n_ctx, w), lambda q: (q, 0, 0, 0)),
                  pl.BlockSpec((2, w // 2, w // 2), lambda q: (q, 0, 0)),
                  pl.BlockSpec((1, w, w), lambda q: (q, 0, 0)),
                  pl.BlockSpec((1, w, w), lambda q: (q, 0, 0)),
                  pl.BlockSpec((1, 8, LANES), lambda q: (q, 0, 0))],
        out_specs=pl.BlockSpec((1, bsz, n_lat, w), lambda q: (q, 0, 0, 0)),
        out_shape=jax.ShapeDtypeStruct((q_n, bsz, n_lat, w), BF16),
        scratch_shapes=[pltpu.VMEM((n_blk, bsz * STATE_PITCH, LANES), F32),
                        pltpu.VMEM((n_blk, bsz * n_ctx, LANES), F32),
                        pltpu.VMEM((n_blk, bsz * STATE_PITCH, LANES), F32)],
        compiler_params=_params(("parallel",)),
        name="s5_scan",
    )(u_c, uc_c, t_mat, mb_pair, mc_pair, a_rows)


def _pack_bf16_pair(a, b):
    ua = lax.bitcast_convert_type(a.astype(BF16).astype(F32), U32)
    ub = lax.bitcast_convert_type(b.astype(BF16).astype(F32), U32)
    return ua | (ub >> 16)


def _unpack_bf16_pair(w):
    a = lax.bitcast_convert_type(w & jnp.uint32(0xFFFF0000), F32)
    b = lax.bitcast_convert_type(w << 16, F32)
    return a, b


def _first_max(rows):
    best = rows[0]
    for r in rows[1:]:
        best = jnp.maximum(best, r)
    idx = jnp.full(best.shape, float(len(rows) - 1), F32)
    for k in range(len(rows) - 2, -1, -1):
        idx = jnp.where(rows[k] == best, float(k), idx)
    return best, idx


def _route_rows(lg):
    g_rows = [lg[k:k + 1] for k in range(N_GROUPS)]
    g_max, g_idx = _first_max(g_rows)
    g_sum = sum(jnp.exp(r - g_max) for r in g_rows)
    g_p = 1.0 / g_sum
    e_rows = []
    for j in range(EXPERTS_PER_GROUP):
        r = lg[N_GROUPS + (N_GROUPS - 1) * EXPERTS_PER_GROUP + j:][:1]
        for g in range(N_GROUPS - 2, -1, -1):
            k = N_GROUPS + g * EXPERTS_PER_GROUP + j
            r = jnp.where(g_idx == float(g), lg[k:k + 1], r)
        e_rows.append(r)
    v1, i1 = _first_max(e_rows)
    rest = [jnp.where(i1 == float(j), -jnp.inf, e_rows[j]) for j in range(EXPERTS_PER_GROUP)]
    v2, i2 = _first_max(rest)
    e21 = jnp.exp(v2 - v1)
    w1 = g_p / (1.0 + e21)
    w2 = w1 * e21
    lo = jnp.minimum(i1, i2)
    hi = jnp.maximum(i1, i2)
    base = jnp.where(lo == 0.0, 0.0, jnp.where(lo == 1.0, 3.0, 5.0))
    bucket = g_idx * float(PAIRS_PER_GROUP) + base + hi - lo - 1.0
    first_is_lo = i1 < i2
    return bucket, jnp.where(first_is_lo, w1, w2), jnp.where(first_is_lo, w2, w1)


def _mix_kernel(x_ref, y_ref, bg_ref, cv_ref, cvp_ref, cvn_ref, mod_ref, g2_ref, cw_ref, wglu_ref, bglu_ref,
                wout_ref, wr_ref, br_ref, x1_ref, h2p_ref, route_ref, counts_ref):
    i = pl.program_id(1)
    tm = x_ref.shape[1]
    d = x_ref.shape[2]
    g = jax.nn.gelu(jnp.concatenate([y_ref[0, j] for j in range(y_ref.shape[1])], axis=1))
    glu = g * jax.nn.sigmoid(jnp.dot(g.astype(BF16), wglu_ref[...], preferred_element_type=F32) + bglu_ref[...])
    cv = cv_ref[0].astype(F32)
    row = lax.broadcasted_iota(jnp.int32, (tm, 1), 0)
    col_in_row = row % GRID_W
    cr = cv[:, :CONV_ROW_WIDTH]
    left = jnp.where(col_in_row == 0, 0.0, pltpu.roll(cr, 1, axis=0))
    right = jnp.where(col_in_row == GRID_W - 1, 0.0, pltpu.roll(cr, tm - 1, axis=0))
    w_r = cw_ref[:, :CONV_ROW_WIDTH]
    row_part = left * w_r[0:1] + cr * w_r[1:2] + right * w_r[2:3]
    cc = cv[:, CONV_ROW_WIDTH:]
    up_halo = jnp.where(i == 0, 0.0, cvp_ref[0].astype(F32))
    dn_halo = jnp.where(i == pl.num_programs(1) - 1, 0.0, cvn_ref[0].astype(F32))
    up = jnp.concatenate([up_halo, cc[:tm - GRID_W]], axis=0)
    dn = jnp.concatenate([cc[GRID_W:], dn_halo], axis=0)
    w_c = cw_ref[:, CONV_ROW_WIDTH:]
    col_part = up * w_c[0:1] + cc * w_c[1:2] + dn * w_c[2:3]
    bg = bg_ref[0].astype(F32)
    y_row = (bg[:, :CONV_ROW_WIDTH] * row_part).astype(BF16)
    y_col = (bg[:, CONV_ROW_WIDTH:] * col_part).astype(BF16)
    o1 = S5_WIDTH
    o2 = S5_WIDTH + CONV_ROW_WIDTH
    yx = (jnp.dot(glu.astype(BF16), wout_ref[0:o1, :], preferred_element_type=F32)
          + jnp.dot(y_row, wout_ref[o1:o2, :], preferred_element_type=F32)
          + jnp.dot(y_col, wout_ref[o2:, :], preferred_element_type=F32))
    x1 = x_ref[0] + mod_ref[0, 2:3, :] * yx
    x1_ref[0] = x1
    h2 = _modulated_norm(x1, g2_ref[...], mod_ref[0, 3:4, :], mod_ref[0, 4:5, :])
    h2b = h2.astype(BF16)
    lg2 = lax.dot_general(wr_ref[...], h2b, (((1,), (1,)), ((), ())), preferred_element_type=F32)
    lg = lg2[:ROUTER_ROWS] + lg2[ROUTER_ROWS:] + br_ref[...]
    bucket, w_a, w_b = _route_rows(lg)
    r8 = lax.broadcasted_iota(jnp.int32, (8, tm), 0)
    route_ref[...] = jnp.where(r8 == 0, bucket, jnp.where(r8 == 1, w_a, jnp.where(r8 == 2, w_b, 0.0)))
    rl = lax.broadcasted_iota(jnp.int32, (ROW_EXTRA, tm), 0)
    gates_t = jnp.where(rl == 0, w_a, jnp.where(rl == 1, w_b, 0.0))
    h2p_ref[0, :, 0:d // 2] = _pack_bf16_pair(h2[:, :d // 2], h2[:, d // 2:])
    h2p_ref[0, :, d // 2:] = lax.bitcast_convert_type(gates_t.T, U32)
    rb = lax.broadcasted_iota(jnp.int32, (BUCKET_ROWS, tm), 0).astype(F32)
    cnt = jnp.sum(jnp.where(rb == bucket, 1.0, 0.0), axis=-1, keepdims=True)

    @pl.when(jnp.logical_and(pl.program_id(0) == 0, i == 0))
    def _():
        counts_ref[...] = jnp.zeros_like(counts_ref)

    counts_ref[...] += cnt


def _mix(x, y_s5, bg, cv, mods, norm2_g, conv_w, w_glu, b_glu, w_out, w_router_t, b_router_t, tm):
    bsz, n, d = x.shape
    nt = n // tm
    halo_blocks = n // GRID_W
    per_tile = tm // GRID_W
    tok = lambda w: pl.BlockSpec((1, tm, w), lambda b, i: (b, i, 0))
    full = lambda a: pl.BlockSpec(a.shape, lambda b, i: (0,) * a.ndim)
    args = (x, y_s5, bg, cv, cv, cv, mods, norm2_g.reshape(1, d), conv_w, w_glu, b_glu.reshape(1, -1),
            w_out, w_router_t, b_router_t)
    in_specs = [tok(d), pl.BlockSpec((1, y_s5.shape[1], tm, LANES), lambda b, i: (b, 0, i, 0)),
                tok(CONV_WIDTH), tok(CONV_WIDTH),
                pl.BlockSpec((1, GRID_W, CONV_ROW_WIDTH),
                             lambda b, i: (b, jnp.maximum(i * per_tile - 1, 0), 1)),
                pl.BlockSpec((1, GRID_W, CONV_ROW_WIDTH),
                             lambda b, i: (b, jnp.minimum((i + 1) * per_tile, halo_blocks - 1), 1)),
                pl.BlockSpec((1, N_MOD, d), lambda b, i: (b, 0, 0))] + [full(a) for a in args[7:]]
    return pl.pallas_call(
        _mix_kernel,
        grid=(bsz, nt),
        in_specs=in_specs,
        out_specs=[tok(d), tok(d // 2 + ROW_EXTRA),
                   pl.BlockSpec((8, tm), lambda b, i: (0, b * nt + i)),
                   pl.BlockSpec((BUCKET_ROWS, 128), lambda b, i: (0, 0))],
        out_shape=[jax.ShapeDtypeStruct((bsz, n, d), F32),
                   jax.ShapeDtypeStruct((bsz, n, d // 2 + ROW_EXTRA), U32),
                   jax.ShapeDtypeStruct((8, bsz * n), F32),
                   jax.ShapeDtypeStruct((BUCKET_ROWS, 128), F32)],
        compiler_params=_params(("arbitrary", "arbitrary")),
        name="mix",
    )(*args)


def _rank_kernel(route_ref, offs_ref, dest_ref, run_ref):
    tr = route_ref.shape[1]

    @pl.when(pl.program_id(0) == 0)
    def _():
        run_ref[...] = jnp.zeros_like(run_ref)

    bucket = route_ref[0:1, :]
    rb = lax.broadcasted_iota(jnp.int32, (BUCKET_ROWS, tr), 0).astype(F32)
    onehot = jnp.where(rb == bucket, 1.0, 0.0)
    s_idx = lax.broadcasted_iota(jnp.int32, (tr, tr), 0)
    t_idx = lax.broadcasted_iota(jnp.int32, (tr, tr), 1)
    tri = jnp.where(s_idx <= t_idx, 1.0, 0.0).astype(BF16)
    prefix = jnp.dot(onehot.astype(BF16), tri, preferred_element_type=F32)
    before = run_ref[:, 0:1] + offs_ref[:, 0:1]
    dest = jnp.sum(onehot * (prefix - 1.0 + before), axis=0, keepdims=True)
    dest_ref[...] = dest.astype(jnp.int32)
    run_ref[...] += prefix[:, tr - 1:tr]


def _rank(route_t, offs_rows, tr):
    n = route_t.shape[1]
    return pl.pallas_call(
        _rank_kernel,
        grid=(n // tr,),
        in_specs=[pl.BlockSpec((8, tr), lambda i: (0, i)),
                  pl.BlockSpec((BUCKET_ROWS, 128), lambda i: (0, 0))],
        out_specs=pl.BlockSpec((1, tr), lambda i: (0, i)),
        out_shape=jax.ShapeDtypeStruct((1, n), jnp.int32),
        scratch_shapes=[pltpu.VMEM((BUCKET_ROWS, 128), F32)],
        compiler_params=_params(("arbitrary",)),
        name="rank",
    )(route_t, offs_rows)


def _sc_mesh():
    return plsc.VectorSubcoreMesh(core_axis_name="core", subcore_axis_name="subcore")


def _scatter_rows(src, dest, n_out):
    n, w = src.shape
    sub = SC_INDEX_TILE // SC_ROWS

    @functools.partial(pl.kernel, out_type=jax.ShapeDtypeStruct((n_out, w), src.dtype), mesh=_sc_mesh(),
                       scratch_types=[], name="scatter_rows")
    def scatter(x_hbm, i_hbm, o_hbm):
        def body(x_vmem, i_vmem):
            j = pl.program_id(1)
            pltpu.sync_copy(x_vmem, o_hbm.at[i_vmem.at[0, pl.ds(j * SC_ROWS, SC_ROWS)]])

        pltpu.emit_pipeline(
            body,
            grid=(n // SC_INDEX_TILE, sub),
            in_specs=[pl.BlockSpec((SC_ROWS, w), lambda i, j: (i * sub + j, 0)),
                      pl.BlockSpec((1, SC_INDEX_TILE), lambda i, j: (0, i))],
            out_specs=[],
            core_axis_name=("core", "subcore"),
            dimension_semantics=(pltpu.PARALLEL, pltpu.ARBITRARY),
        )(x_hbm, i_hbm)

    return scatter(src, dest)


def _gather_rows(src, idx):
    n = idx.shape[1]
    w = src.shape[1]
    sub = SC_INDEX_TILE // SC_ROWS

    @functools.partial(pl.kernel, out_type=jax.ShapeDtypeStruct((n, w), src.dtype), mesh=_sc_mesh(),
                       scratch_types=[], name="gather_rows")
    def gather(x_hbm, i_hbm, o_hbm):
        def body(i_vmem, o_vmem):
            j = pl.program_id(1)
            pltpu.sync_copy(x_hbm.at[i_vmem.at[0, pl.ds(j * SC_ROWS, SC_ROWS)]], o_vmem)

        pltpu.emit_pipeline(
            body,
            grid=(n // SC_INDEX_TILE, sub),
            in_specs=[pl.BlockSpec((1, SC_INDEX_TILE), lambda i, j: (0, i))],
            out_specs=[pl.BlockSpec((SC_ROWS, w), lambda i, j: (i * sub + j, 0))],
            core_axis_name=("core", "subcore"),
            dimension_semantics=(pltpu.PARALLEL, pltpu.ARBITRARY),
        )(i_hbm, o_hbm)

    return gather(src, idx)


def _moe_kernel(ea_ref, eb_ref, valid_ref, new_ref, rows_ref, w1a_ref, w3a_ref, w2a_ref, w1b_ref, w3b_ref, w2b_ref,
                o_ref, *w_scr):
    j = pl.program_id(0)
    half = rows_ref.shape[1] - ROW_EXTRA

    @pl.when(new_ref[j] != 0)
    def _():
        for src, dst in zip((w1a_ref, w3a_ref, w2a_ref, w1b_ref, w3b_ref, w2b_ref), w_scr):
            dst[...] = src[0].astype(BF16)

    @pl.when(valid_ref[j] != 0)
    def _():
        ha, hb = _unpack_bf16_pair(rows_ref[:, 0:half])
        ha = ha.astype(BF16)
        hb = hb.astype(BF16)
        gates = lax.bitcast_convert_type(rows_ref[:, half:], F32)

        def expert(w1_s, w3_s, w2_s, gate):
            def up(w_s):
                return (jnp.dot(ha, w_s[0:half, :], preferred_element_type=F32)
                        + jnp.dot(hb, w_s[half:, :], preferred_element_type=F32))

            a1 = up(w1_s)
            he = (a1 * jax.nn.sigmoid(a1)) * up(w3_s) * gate
            return jnp.dot(he.astype(BF16), w2_s[...], preferred_element_type=F32)

        y = expert(*w_scr[0:3], gates[:, 0:1]) + expert(*w_scr[3:6], gates[:, 1:2])
        o_ref[...] = _pack_bf16_pair(y[:, :half], y[:, half:])


def _moe_grouped(rows, tile_ea, tile_eb, tile_valid, tile_new, w1, w3, w2, tmm):
    r, w = rows.shape
    n_e, de, d = w2.shape
    up_a = pl.BlockSpec((1, d, de), lambda j, ea, eb, va, nw: (ea[j], 0, 0))
    up_b = pl.BlockSpec((1, d, de), lambda j, ea, eb, va, nw: (eb[j], 0, 0))
    grid_spec = pltpu.PrefetchScalarGridSpec(
        num_scalar_prefetch=4,
        grid=(r // tmm,),
        in_specs=[pl.BlockSpec((tmm, w), lambda j, ea, eb, va, nw: (j, 0)),
                  up_a, up_a, pl.BlockSpec((1, de, d), lambda j, ea, eb, va, nw: (ea[j], 0, 0)),
                  up_b, up_b, pl.BlockSpec((1, de, d), lambda j, ea, eb, va, nw: (eb[j], 0, 0))],
        out_specs=pl.BlockSpec((tmm, d // 2), lambda j, ea, eb, va, nw: (j, 0)),
        scratch_shapes=[pltpu.VMEM((d, de), BF16), pltpu.VMEM((d, de), BF16), pltpu.VMEM((de, d), BF16)] * 2,
    )
    return pl.pallas_call(
        _moe_kernel,
        grid_spec=grid_spec,
        out_shape=jax.ShapeDtypeStruct((r, d // 2), U32),
        compiler_params=_params(("arbitrary",)),
        name="moe",
    )(tile_ea, tile_eb, tile_valid, tile_new, rows, w1, w3, w2, w1, w3, w2)


def _final_kernel(x1_ref, moe_ref, mod_ref, fg_ref, o_ref):
    ya, yb = _unpack_bf16_pair(moe_ref[0])
    half = ya.shape[1]
    gate = mod_ref[0, 5:6, :]
    xa = x1_ref[0, :, 0:half] + gate[:, 0:half] * ya
    xb = x1_ref[0, :, half:] + gate[:, half:] * yb
    ms = (jnp.sum(xa * xa, axis=-1, keepdims=True) + jnp.sum(xb * xb, axis=-1, keepdims=True)) / (2 * half)
    inv = lax.rsqrt(ms + RMS_EPS)
    o_ref[0, :, 0:half] = xa * inv * fg_ref[:, 0:half]
    o_ref[0, :, half:] = xb * inv * fg_ref[:, half:]


def _final(x1, moe_tok, mods, final_g, tm):
    bsz, n, d = x1.shape
    tok = lambda w: pl.BlockSpec((1, tm, w), lambda b, i: (b, i, 0))
    return pl.pallas_call(
        _final_kernel,
        grid=(bsz, n // tm),
        in_specs=[tok(d), tok(d // 2), pl.BlockSpec((1, N_MOD, d), lambda b, i: (b, 0, 0)),
                  pl.BlockSpec((1, d), lambda b, i: (0, 0))],
        out_specs=tok(d),
        out_shape=jax.ShapeDtypeStruct((bsz, n, d), F32),
        compiler_params=_params(("parallel", "parallel")),
        name="final",
    )(x1, moe_tok, mods, final_g.reshape(1, d))


def _tile_plan(counts, tmm, n_tiles):
    tiles = (counts + (tmm - 1)) // tmm
    tile_end = jnp.cumsum(tiles)
    offs = (tile_end - tiles) * tmm
    n_valid = tile_end[-1]
    j = jnp.arange(n_tiles, dtype=jnp.int32)
    bucket = jnp.sum((tile_end[None, :] <= jnp.minimum(j, n_valid - 1)[:, None]).astype(jnp.int32), axis=1)
    pair_lo = jnp.array([0, 0, 0, 1, 1, 2], jnp.int32)
    pair_hi = jnp.array([1, 2, 3, 2, 3, 3], jnp.int32)
    group = bucket // PAIRS_PER_GROUP
    pair = bucket % PAIRS_PER_GROUP
    tile_ea = group * EXPERTS_PER_GROUP + pair_lo[pair]
    tile_eb = group * EXPERTS_PER_GROUP + pair_hi[pair]
    tile_new = jnp.concatenate([jnp.ones((1,), jnp.int32), (bucket[1:] != bucket[:-1]).astype(jnp.int32)])
    return offs, tile_ea, tile_eb, (j < n_valid).astype(jnp.int32), tile_new


def kernel(x, c, ctx, c_ctx, w_mod, b_mod, norm1_g, norm2_g, w_in, s5_lambda_re, s5_lambda_im, s5_log_dt,
           s5_b_re, s5_b_im, s5_c_re, s5_c_im, s5_d, w_glu, b_glu, conv_w, w_out, router_group_w,
           router_group_b, router_expert_w, router_expert_b, expert_w1, expert_w3, expert_w2, final_g):
    assert w_mod.shape[0] == 1, "single-layer kernel"
    bsz, n_tok, d = x.shape
    n_ctx = ctx.shape[1]
    n_all = bsz * n_tok
    l = 0
    tm = min(TOKEN_TILE, n_tok)

    n_cond = bsz + 1
    pad = (-n_cond) % 8
    cond = jnp.concatenate([c, c_ctx[None, :], jnp.zeros((pad, d), F32)], axis=0)
    m = _mod_rows(cond, w_mod[l], b_mod[l])
    mx = m[:bsz].reshape(bsz, N_MOD, d)
    mc = m[bsz:bsz + 1].reshape(1, N_MOD, d)

    w_in_b = w_in[l].astype(BF16)
    u, bg, cv = _inproj(x, mx, True, norm1_g[l], w_in_b, tm, True)
    (uc,) = _inproj(ctx, mc, False, norm1_g[l], w_in_b[:, :S5_WIDTH], min(TOKEN_TILE, n_ctx), False)

    t_mat, mb_pair, mc_pair, a_rows = _s5_matrices(
        s5_lambda_re[l], s5_lambda_im[l], s5_log_dt[l], s5_b_re[l], s5_b_im[l], s5_c_re[l], s5_c_im[l], s5_d[l])
    y_c = _s5_scan(_chunkify(u), _chunkify_context(uc), t_mat, mb_pair, mc_pair, a_rows)
    y_s5 = _unchunkify(y_c, u.shape[1])

    n_logits = N_GROUPS + N_EXPERTS
    w_router = jnp.concatenate(
        [router_group_w[l], router_expert_w[l], jnp.zeros((d, ROUTER_ROWS - n_logits), F32)], axis=1).T
    w_router_hi = w_router.astype(BF16)
    w_router_lo = (w_router - w_router_hi.astype(F32)).astype(BF16)
    w_router_t = jnp.concatenate([w_router_hi, w_router_lo], axis=0)
    b_router = jnp.concatenate([router_group_b[l], router_expert_b[l], jnp.zeros((ROUTER_ROWS - n_logits,), F32)])
    b_router_t = jnp.broadcast_to(b_router[:, None], (ROUTER_ROWS, tm))

    x1, h2p, route_t, counts = _mix(x, y_s5, bg, cv, mx, norm2_g[l], conv_w[l], w_glu[l].astype(BF16), b_glu[l],
                                    w_out[l].astype(BF16), w_router_t, b_router_t, tm)

    n_buckets = N_GROUPS * PAIRS_PER_GROUP
    n_rows = n_all + n_buckets * MOE_TILE
    offs, tile_ea, tile_eb, tile_valid, tile_new = _tile_plan(counts[:n_buckets, 0].astype(jnp.int32), MOE_TILE,
                                                              n_rows // MOE_TILE)
    offs_rows = jnp.zeros((BUCKET_ROWS,), F32).at[:n_buckets].set(offs.astype(F32))
    dest = _rank(route_t, jnp.broadcast_to(offs_rows[:, None], (BUCKET_ROWS, 128)), min(RANK_TILE, n_all))

    rows = _scatter_rows(h2p.reshape(n_all, d // 2 + ROW_EXTRA), dest, n_rows)
    y_rows = _moe_grouped(rows, tile_ea, tile_eb, tile_valid, tile_new, expert_w1[l], expert_w3[l], expert_w2[l],
                          MOE_TILE)
    moe_tok = _gather_rows(y_rows, dest).reshape(bsz, n_tok, d // 2)
    return _final(x1, moe_tok, mx, final_g, tm)
```

```python
import functools

import jax
import jax.numpy as jnp
from jax import lax
from jax.experimental import pallas as pl
from jax.experimental.pallas import tpu as pltpu
from jax.experimental.pallas import tpu_sc as plsc

F32 = jnp.float32
BF16 = jnp.bfloat16
U32 = jnp.uint32

RMS_EPS = 1e-6
N_MOD = 6
GRID_W = 64
S5_WIDTH = 256
S5_H = 16
S5_P = 64
S5_GROUPS = S5_WIDTH // S5_H
S5_PAIRS = S5_GROUPS // 2
S5_CHUNK = 16
LANES = 128
STATE_PITCH = 136
CONV_WIDTH = 768
CONV_ROW_WIDTH = CONV_WIDTH // 2
N_GROUPS = 4
EXPERTS_PER_GROUP = 4
N_EXPERTS = N_GROUPS * EXPERTS_PER_GROUP
PAIRS_PER_GROUP = 6
ROUTER_ROWS = 32
BUCKET_ROWS = 32
ROW_EXTRA = 128
TOKEN_TILE = 512
MOE_TILE = 512
RANK_TILE = 2048
MOE_PARTS = 2
SC_ROWS = 32
SC_INDEX_TILE = 128
VMEM_LIMIT = 52 * 1024 * 1024


def _params(sem, vmem=VMEM_LIMIT):
    return pltpu.CompilerParams(dimension_semantics=sem, vmem_limit_bytes=vmem)


def _mod_kernel(c_ref, w_ref, b_ref, o_ref):
    c = c_ref[...]
    o_ref[...] = jnp.dot(c * jax.nn.sigmoid(c), w_ref[...], preferred_element_type=F32) + b_ref[...]


def _mod_rows(cond, w_mod, b_mod):
    n, d = cond.shape
    nout = w_mod.shape[1]
    bn = d
    return pl.pallas_call(
        _mod_kernel,
        grid=(nout // bn,),
        in_specs=[pl.BlockSpec((n, d), lambda j: (0, 0)),
                  pl.BlockSpec((d, bn), lambda j: (0, j)),
                  pl.BlockSpec((1, bn), lambda j: (0, j))],
        out_specs=pl.BlockSpec((n, bn), lambda j: (0, j)),
        out_shape=jax.ShapeDtypeStruct((n, nout), F32),
        compiler_params=_params(("arbitrary",)),
        name="mod",
    )(cond, w_mod, b_mod.reshape(1, nout))


def _modulated_norm(x, g, shift, scale):
    ms = jnp.mean(x * x, axis=-1, keepdims=True)
    return (x * lax.rsqrt(ms + RMS_EPS) * g) * (1.0 + scale) + shift


def _inproj_kernel(x_ref, mod_ref, g_ref, w_ref, u_ref, *conv_refs):
    h = _modulated_norm(x_ref[0], g_ref[...], mod_ref[0, 0:1, :], mod_ref[0, 1:2, :]).astype(BF16)
    u = jnp.dot(h, w_ref[:, 0:S5_WIDTH], preferred_element_type=F32)
    for j in range(S5_WIDTH // LANES):
        u_ref[0, j] = u[:, j * LANES:(j + 1) * LANES]
    if conv_refs:
        bg_ref, cv_ref = conv_refs
        o = S5_WIDTH
        bg_ref[0] = jnp.dot(h, w_ref[:, o:o + CONV_WIDTH], preferred_element_type=F32).astype(BF16)
        c_g = jnp.dot(h, w_ref[:, o + CONV_WIDTH:o + 2 * CONV_WIDTH], preferred_element_type=F32)
        v = jnp.dot(h, w_ref[:, o + 2 * CONV_WIDTH:o + 3 * CONV_WIDTH], preferred_element_type=F32)
        cv_ref[0] = (c_g * v).astype(BF16)


def _inproj(x, mods, per_batch_mod, norm_g, w_bf16, tm, with_conv):
    bsz, n, d = x.shape
    nw = w_bf16.shape[1]
    mod_map = (lambda b, i: (b, 0, 0)) if per_batch_mod else (lambda b, i: (0, 0, 0))
    n_slab = S5_WIDTH // LANES
    out_shape = [jax.ShapeDtypeStruct((bsz, n_slab, n, LANES), F32)]
    out_specs = [pl.BlockSpec((1, n_slab, tm, LANES), lambda b, i: (b, 0, i, 0))]
    if with_conv:
        out_shape += [jax.ShapeDtypeStruct((bsz, n, CONV_WIDTH), BF16)] * 2
        out_specs += [pl.BlockSpec((1, tm, CONV_WIDTH), lambda b, i: (b, i, 0))] * 2
    return pl.pallas_call(
        _inproj_kernel,
        grid=(bsz, n // tm),
        in_specs=[pl.BlockSpec((1, tm, d), lambda b, i: (b, i, 0)),
                  pl.BlockSpec((1, N_MOD, d), mod_map),
                  pl.BlockSpec((1, d), lambda b, i: (0, 0)),
                  pl.BlockSpec((d, nw), lambda b, i: (0, 0))],
        out_specs=out_specs,
        out_shape=out_shape,
        compiler_params=_params(("parallel", "parallel")),
        name="inproj_conv" if with_conv else "inproj_ctx",
    )(x, mods, norm_g.reshape(1, d), w_bf16)


def _toeplitz_kernel(strip_ref, t_ref):
    lc = t_ref.shape[1] // S5_H
    for s in range(lc):
        off = (lc - 1 - s) * S5_H
        t_ref[0, s * S5_H:(s + 1) * S5_H, :] = strip_ref[0, :, off:off + lc * S5_H].astype(BF16)


def _toeplitz(strip):
    g_n, h_n, w = strip.shape
    n = S5_CHUNK * h_n
    return pl.pallas_call(
        _toeplitz_kernel,
        grid=(g_n,),
        in_specs=[pl.BlockSpec((1, h_n, w), lambda g: (g, 0, 0))],
        out_specs=pl.BlockSpec((1, n, n), lambda g: (g, 0, 0)),
        out_shape=jax.ShapeDtypeStruct((g_n, n, n), BF16),
        compiler_params=_params(("parallel",)),
        name="toeplitz",
    )(strip)


def _s5_matrices(lam_re, lam_im, log_dt, b_re, b_im, c_re, c_im, d_skip):
    lc, g_n, p_n, h_n = S5_CHUNK, S5_GROUPS, S5_P, S5_H
    lam = lax.complex(lam_re.astype(F32), lam_im.astype(F32))
    dt = jnp.exp(log_dt.astype(F32))[..., None]
    a_bar = jnp.exp(lam * dt)
    b_bar = ((a_bar - 1.0) / lam)[..., None] * lax.complex(b_re.astype(F32), b_im.astype(F32))
    cm = lax.complex(c_re.astype(F32), c_im.astype(F32))
    steps = jnp.arange(lc + 1, dtype=F32)
    apow = jnp.exp((lam * dt)[:, :, None, :] * steps[None, None, :, None])
    kern = jnp.einsum('dgop,dgjp,dgpi->dgjio', cm, apow[:, :, :lc], b_bar).real
    skip = jnp.eye(h_n, dtype=F32) * d_skip.astype(F32).reshape(g_n, 1, h_n)
    centre = kern[0, :, 0] + kern[1, :, 0] + skip
    lags = jnp.concatenate([kern[1, :, :0:-1], centre[:, None], kern[0, :, 1:]], axis=1)
    strip = lags.transpose(0, 2, 1, 3).reshape(g_n, h_n, (2 * lc - 1) * h_n)
    strip = jnp.pad(strip, ((0, 0), (0, 0), (0, h_n)))
    t_mat = _toeplitz(strip)

    def in_mat(pw, bb):
        return (pw[:, :, None, :] * bb.transpose(0, 2, 1)[:, None, :, :]).reshape(g_n, lc * h_n, p_n)

    mb_f = in_mat(apow[0, :, lc - 1::-1][:, :lc], b_bar[0])
    mb_b = in_mat(apow[1, :, :lc], b_bar[1])

    def out_mat(pw, cc):
        return (pw.transpose(0, 2, 1)[:, :, :, None] * cc.transpose(0, 2, 1)[:, :, None, :]).reshape(
            g_n, p_n, lc * h_n)

    mc_f = out_mat(apow[0, :, 1:lc + 1], cm[0])
    mc_b = out_mat(apow[1, :, lc:0:-1], cm[1])
    a_chunk = apow[:, :, lc]

    q_n = S5_PAIRS
    zeros_in = jnp.zeros((g_n, lc * h_n, p_n), F32)

    def pair_cols(m):
        m = m.reshape(q_n, 2, lc * h_n, p_n)
        z = zeros_in.reshape(q_n, 2, lc * h_n, p_n)[:, 0]
        top = jnp.concatenate([m[:, 0], z], axis=-1)
        bot = jnp.concatenate([z, m[:, 1]], axis=-1)
        return jnp.concatenate([top, bot], axis=1)

    mb_pair = jnp.concatenate([pair_cols(mb_f.real), pair_cols(mb_f.imag),
                               pair_cols(mb_b.real), pair_cols(mb_b.imag)], axis=-1)

    def pair_rows(m):
        m = m.reshape(q_n, 2, p_n, lc * h_n)
        z = jnp.zeros_like(m[:, 0])
        top = jnp.concatenate([m[:, 0], z], axis=-1)
        bot = jnp.concatenate([z, m[:, 1]], axis=-1)
        return jnp.concatenate([top, bot], axis=1)

    mc_pair = jnp.concatenate([pair_rows(mc_f.real), pair_rows(-mc_f.imag),
                               pair_rows(mc_b.real), pair_rows(-mc_b.imag)], axis=1)
    a_rows = jnp.stack([a_chunk[0].real, a_chunk[0].imag, a_chunk[1].real, a_chunk[1].imag], axis=0)
    a_rows = a_rows.reshape(4, q_n, 2 * p_n).transpose(1, 0, 2)
    a_rows = jnp.concatenate([a_rows, jnp.zeros_like(a_rows)], axis=1)
    return t_mat, mb_pair.astype(BF16), mc_pair.astype(BF16), a_rows


def _chunkify_kernel(u_ref, o_ref):
    _, nb, nc, _ = o_ref.shape
    half = S5_CHUNK * S5_H
    per_slab = LANES // S5_H
    for j in range(u_ref.shape[1]):
        cols = []
        for t in range(S5_CHUNK):
            rows = [u_ref[b, j, pl.ds(t, nc, stride=S5_CHUNK), :] for b in range(nb)]
            cols.append((rows[0] if nb == 1 else jnp.concatenate(rows, axis=0)).T)
        for gl in range(per_slab):
            g = j * per_slab + gl
            m = jnp.concatenate([c[gl * S5_H:(gl + 1) * S5_H, :] for c in cols], axis=0)
            o_ref[g // 2, :, :, (g % 2) * half:(g % 2 + 1) * half] = m.T.astype(BF16).reshape(nb, nc, half)


def _chunkify(u_slab, name):
    bsz, n_slab, n, _ = u_slab.shape
    nc = n // S5_CHUNK
    nb = min(bsz, max(1, LANES // nc))
    w = 2 * S5_CHUNK * S5_H
    return pl.pallas_call(
        _chunkify_kernel,
        grid=(bsz // nb,),
        in_specs=[pl.BlockSpec((nb, n_slab, n, LANES), lambda b: (b, 0, 0, 0))],
        out_specs=pl.BlockSpec((S5_PAIRS, nb, nc, w), lambda b: (0, b, 0, 0)),
        out_shape=jax.ShapeDtypeStruct((S5_PAIRS, bsz, nc, w), BF16),
        compiler_params=_params(("parallel",)),
        name=name,
    )(u_slab)


def _unchunkify_kernel(y_ref, o_ref):
    nc = y_ref.shape[2]
    half = S5_CHUNK * S5_H
    per_slab = LANES // S5_H
    for j in range(o_ref.shape[1]):
        rows = []
        for gl in range(per_slab):
            g = j * per_slab + gl
            rows.append(y_ref[g // 2, 0, :, (g % 2) * half:(g % 2 + 1) * half].astype(F32).T)
        for t in range(S5_CHUNK):
            bt = jnp.concatenate([r[t * S5_H:(t + 1) * S5_H, :] for r in rows], axis=0)
            o_ref[0, j, pl.ds(t, nc, stride=S5_CHUNK), :] = bt.T


def _unchunkify(y_c, n_slab):
    q_n, bsz, nc, w = y_c.shape
    n = nc * S5_CHUNK
    return pl.pallas_call(
        _unchunkify_kernel,
        grid=(bsz,),
        in_specs=[pl.BlockSpec((q_n, 1, nc, w), lambda b: (0, b, 0, 0))],
        out_specs=pl.BlockSpec((1, n_slab, n, LANES), lambda b: (b, 0, 0, 0)),
        out_shape=jax.ShapeDtypeStruct((bsz, n_slab, n, LANES), F32),
        compiler_params=_params(("parallel",)),
        name="unchunkify",
    )(y_c)


def _s5_kernel(u_ref, uc_ref, t_ref, mb_ref, mc_ref, a_ref, y_ref, s_lat, s_ctx, h_scr, *, bb):
    _, bsz, n_lat, w = u_ref.shape
    n_ctx = uc_ref.shape[2]
    n_blk = w // LANES
    rb = bb * n_lat

    mb = mb_ref[0]

    def in_lat(i, carry):
        s = jnp.dot(u_ref[0, pl.ds(i * bb, bb)].reshape(rb, w), mb, preferred_element_type=F32)
        for k in range(bb):
            r = pl.multiple_of((i * bb + k) * STATE_PITCH, 8)
            for blk in range(n_blk):
                s_lat[blk, pl.ds(r, n_lat), :] = s[k * n_lat:(k + 1) * n_lat, blk * LANES:(blk + 1) * LANES]
        return carry

    lax.fori_loop(0, bsz // bb, in_lat, 0)
    sc = jnp.dot(uc_ref[0].reshape(bsz * n_ctx, w), mb, preferred_element_type=F32)
    for blk in range(n_blk):
        s_ctx[blk] = sc[:, blk * LANES:(blk + 1) * LANES]

    a_fr, a_fi, a_br, a_bi = (a_ref[0, k:k + 1, :] for k in range(4))

    def step(h, a_r, a_i, s_r, s_i):
        h_r, h_i = h
        return a_r * h_r - a_i * h_i + s_r, a_r * h_i + a_i * h_r + s_i

    def ctx_rows(blk, c):
        return s_ctx[blk, pl.ds(c, bsz, stride=n_ctx), :]

    def lat_rows(ref, blk, c):
        return ref.at[blk, pl.ds(c, bsz, stride=STATE_PITCH), :]

    def ctx_step(k, carry):
        hf, hb = carry
        kb = n_ctx - 1 - k
        hf = step(hf, a_fr, a_fi, ctx_rows(0, k), ctx_rows(1, k))
        hb = step(hb, a_br, a_bi, ctx_rows(2, kb), ctx_rows(3, kb))
        return hf, hb

    zero = jnp.zeros((bsz, LANES), F32)
    carry = lax.fori_loop(0, n_ctx, ctx_step, ((zero, zero), (zero, zero)))

    def lat_step(k, carry):
        hf, hb = carry
        kb = n_lat - 1 - k
        lat_rows(h_scr, 0, k)[...] = hf[0]
        lat_rows(h_scr, 1, k)[...] = hf[1]
        lat_rows(h_scr, 2, kb)[...] = hb[0]
        lat_rows(h_scr, 3, kb)[...] = hb[1]
        hf = step(hf, a_fr, a_fi, lat_rows(s_lat, 0, k)[...], lat_rows(s_lat, 1, k)[...])
        hb = step(hb, a_br, a_bi, lat_rows(s_lat, 2, kb)[...], lat_rows(s_lat, 3, kb)[...])
        return hf, hb

    lax.fori_loop(0, n_lat, lat_step, carry)

    t0 = t_ref[0]
    t1 = t_ref[1]
    mc = mc_ref[0]
    half = S5_CHUNK * S5_H

    def out_lat(i, carry):
        u = u_ref[0, pl.ds(i * bb, bb)].reshape(rb, w)
        h_rows = []
        for k in range(bb):
            r = pl.multiple_of((i * bb + k) * STATE_PITCH, 8)
            h_rows.append(jnp.concatenate([h_scr[blk, pl.ds(r, n_lat), :] for blk in range(n_blk)], axis=1))
        h = jnp.concatenate(h_rows, axis=0).astype(BF16)
        inter = jnp.dot(h, mc, preferred_element_type=F32)
        y0 = jnp.dot(u[:, :half], t0, preferred_element_type=F32) + inter[:, :half]
        y1 = jnp.dot(u[:, half:], t1, preferred_element_type=F32) + inter[:, half:]
        y = jnp.concatenate([y0, y1], axis=1).astype(BF16)
        y_ref[0, pl.ds(i * bb, bb)] = y.reshape(bb, n_lat, w)
        return carry

    lax.fori_loop(0, bsz // bb, out_lat, 0)


def _s5_scan(u_c, uc_c, t_mat, mb_pair, mc_pair, a_rows):
    q_n, bsz, n_lat, w = u_c.shape
    n_ctx = uc_c.shape[2]
    assert n_lat + 8 == STATE_PITCH
    bb = min(4, bsz)
    n_blk = w // LANES
    return pl.pallas_call(
        functools.partial(_s5_kernel, bb=bb),
        grid=(q_n,),
        in_specs=[pl.BlockSpec((1, bsz, n_lat, w), lambda q: (q, 0, 0, 0)),
                  pl.BlockSpec((1, bsz, n_ctx, w), lambda q: (q, 0, 0, 0)),
                  pl.BlockSpec((2, w // 2, w // 2), lambda q: (q, 0, 0)),
                  pl.BlockSpec((1, w, w), lambda q: (q, 0, 0)),
                  pl.BlockSpec((1, w, w), lambda q: (q, 0, 0)),
                  pl.BlockSpec((1, 8, LANES), lambda q: (q, 0, 0))],
        out_specs=pl.BlockSpec((1, bsz, n_lat, w), lambda q: (q, 0, 0, 0)),
        out_shape=jax.ShapeDtypeStruct((q_n, bsz, n_lat, w), BF16),
        scratch_shapes=[pltpu.VMEM((n_blk, bsz * STATE_PITCH, LANES), F32),
                        pltpu.VMEM((n_blk, bsz * n_ctx, LANES), F32),
                        pltpu.VMEM((n_blk, bsz * STATE_PITCH, LANES), F32)],
        compiler_params=_params(("parallel",)),
        name="s5_scan",
    )(u_c, uc_c, t_mat, mb_pair, mc_pair, a_rows)


def _pack_bf16_pair(a, b):
    ua = lax.bitcast_convert_type(a.astype(BF16).astype(F32), U32)
    ub = lax.bitcast_convert_type(b.astype(BF16).astype(F32), U32)
    return ua | (ub >> 16)


def _unpack_bf16_pair(w):
    a = lax.bitcast_convert_type(w & jnp.uint32(0xFFFF0000), F32)
    b = lax.bitcast_convert_type(w << 16, F32)
    return a, b


def _first_max(rows):
    best = rows[0]
    for r in rows[1:]:
        best = jnp.maximum(best, r)
    idx = jnp.full(best.shape, float(len(rows) - 1), F32)
    for k in range(len(rows) - 2, -1, -1):
        idx = jnp.where(rows[k] == best, float(k), idx)
    return best, idx


def _route_rows(lg):
    g_rows = [lg[k:k + 1] for k in range(N_GROUPS)]
    g_max, g_idx = _first_max(g_rows)
    g_sum = sum(jnp.exp(r - g_max) for r in g_rows)
    g_p = 1.0 / g_sum
    e_rows = []
    for j in range(EXPERTS_PER_GROUP):
        r = lg[N_GROUPS + (N_GROUPS - 1) * EXPERTS_PER_GROUP + j:][:1]
        for g in range(N_GROUPS - 2, -1, -1):
            k = N_GROUPS + g * EXPERTS_PER_GROUP + j
            r = jnp.where(g_idx == float(g), lg[k:k + 1], r)
        e_rows.append(r)
    v1, i1 = _first_max(e_rows)
    rest = [jnp.where(i1 == float(j), -jnp.inf, e_rows[j]) for j in range(EXPERTS_PER_GROUP)]
    v2, i2 = _first_max(rest)
    e21 = jnp.exp(v2 - v1)
    w1 = g_p / (1.0 + e21)
    w2 = w1 * e21
    lo = jnp.minimum(i1, i2)
    hi = jnp.maximum(i1, i2)
    base = jnp.where(lo == 0.0, 0.0, jnp.where(lo == 1.0, 3.0, 5.0))
    bucket = g_idx * float(PAIRS_PER_GROUP) + base + hi - lo - 1.0
    first_is_lo = i1 < i2
    return bucket, jnp.where(first_is_lo, w1, w2), jnp.where(first_is_lo, w2, w1)


def _mix_kernel(x_ref, y_ref, bg_ref, cv_ref, cvp_ref, cvn_ref, mod_ref, g2_ref, cw_ref, wglu_ref, bglu_ref,
                wout_ref, wr_ref, br_ref, x1_ref, h2p_ref, route_ref, counts_ref):
    i = pl.program_id(1)
    tm = x_ref.shape[1]
    d = x_ref.shape[2]
    g = jax.nn.gelu(jnp.concatenate([y_ref[0, j] for j in range(y_ref.shape[1])], axis=1))
    glu = g * jax.nn.sigmoid(jnp.dot(g.astype(BF16), wglu_ref[...], preferred_element_type=F32) + bglu_ref[...])
    cv = cv_ref[0].astype(F32)
    row = lax.broadcasted_iota(jnp.int32, (tm, 1), 0)
    col_in_row = row % GRID_W
    cr = cv[:, :CONV_ROW_WIDTH]
    left = jnp.where(col_in_row == 0, 0.0, pltpu.roll(cr, 1, axis=0))
    right = jnp.where(col_in_row == GRID_W - 1, 0.0, pltpu.roll(cr, tm - 1, axis=0))
    w_r = cw_ref[:, :CONV_ROW_WIDTH]
    row_part = left * w_r[0:1] + cr * w_r[1:2] + right * w_r[2:3]
    cc = cv[:, CONV_ROW_WIDTH:]
    up_halo = jnp.where(i == 0, 0.0, cvp_ref[0].astype(F32))
    dn_halo = jnp.where(i == pl.num_programs(1) - 1, 0.0, cvn_ref[0].astype(F32))
    up = jnp.concatenate([up_halo, cc[:tm - GRID_W]], axis=0)
    dn = jnp.concatenate([cc[GRID_W:], dn_halo], axis=0)
    w_c = cw_ref[:, CONV_ROW_WIDTH:]
    col_part = up * w_c[0:1] + cc * w_c[1:2] + dn * w_c[2:3]
    bg = bg_ref[0].astype(F32)
    y_row = (bg[:, :CONV_ROW_WIDTH] * row_part).astype(BF16)
    y_col = (bg[:, CONV_ROW_WIDTH:] * col_part).astype(BF16)
    o1 = S5_WIDTH
    o2 = S5_WIDTH + CONV_ROW_WIDTH
    yx = (jnp.dot(glu.astype(BF16), wout_ref[0:o1, :], preferred_element_type=F32)
          + jnp.dot(y_row, wout_ref[o1:o2, :], preferred_element_type=F32)
          + jnp.dot(y_col, wout_ref[o2:, :], preferred_element_type=F32))
    x1 = x_ref[0] + mod_ref[0, 2:3, :] * yx
    x1_ref[0] = x1
    h2 = _modulated_norm(x1, g2_ref[...], mod_ref[0, 3:4, :], mod_ref[0, 4:5, :])
    h2b = h2.astype(BF16)
    lg2 = lax.dot_general(wr_ref[...], h2b, (((1,), (1,)), ((), ())), preferred_element_type=F32)
    lg = lg2[:ROUTER_ROWS] + lg2[ROUTER_ROWS:] + br_ref[...]
    bucket, w_a, w_b = _route_rows(lg)
    r8 = lax.broadcasted_iota(jnp.int32, (8, tm), 0)
    route_ref[...] = jnp.where(r8 == 0, bucket, jnp.where(r8 == 1, w_a, jnp.where(r8 == 2, w_b, 0.0)))
    rl = lax.broadcasted_iota(jnp.int32, (ROW_EXTRA, tm), 0)
    gates_t = jnp.where(rl == 0, w_a, jnp.where(rl == 1, w_b, 0.0))
    h2p_ref[0, :, 0:d // 2] = _pack_bf16_pair(h2[:, :d // 2], h2[:, d // 2:])
    h2p_ref[0, :, d // 2:] = lax.bitcast_convert_type(gates_t.T, U32)
    rb = lax.broadcasted_iota(jnp.int32, (BUCKET_ROWS, tm), 0).astype(F32)
    cnt = jnp.sum(jnp.where(rb == bucket, 1.0, 0.0), axis=-1, keepdims=True)

    @pl.when(jnp.logical_and(pl.program_id(0) == 0, i == 0))
    def _():
        counts_ref[...] = jnp.zeros_like(counts_ref)

    counts_ref[...] += cnt


def _mix(x, y_s5, bg, cv, mods, norm2_g, conv_w, w_glu, b_glu, w_out, w_router_t, b_router_t, tm, b0, nb):
    _, n, d = x.shape
    nt = n // tm
    halo_blocks = n // GRID_W
    per_tile = tm // GRID_W
    tok_in = lambda w: pl.BlockSpec((1, tm, w), lambda b, i: (b + b0, i, 0))
    tok_out = lambda w: pl.BlockSpec((1, tm, w), lambda b, i: (b, i, 0))
    full = lambda a: pl.BlockSpec(a.shape, lambda b, i: (0,) * a.ndim)
    args = (x, y_s5, bg, cv, cv, cv, mods, norm2_g.reshape(1, d), conv_w, w_glu, b_glu.reshape(1, -1),
            w_out, w_router_t, b_router_t)
    in_specs = [tok_in(d), pl.BlockSpec((1, y_s5.shape[1], tm, LANES), lambda b, i: (b + b0, 0, i, 0)),
                tok_in(CONV_WIDTH), tok_in(CONV_WIDTH),
                pl.BlockSpec((1, GRID_W, CONV_ROW_WIDTH),
                             lambda b, i: (b + b0, jnp.maximum(i * per_tile - 1, 0), 1)),
                pl.BlockSpec((1, GRID_W, CONV_ROW_WIDTH),
                             lambda b, i: (b + b0, jnp.minimum((i + 1) * per_tile, halo_blocks - 1), 1)),
                pl.BlockSpec((1, N_MOD, d), lambda b, i: (b + b0, 0, 0))] + [full(a) for a in args[7:]]
    return pl.pallas_call(
        _mix_kernel,
        grid=(nb, nt),
        in_specs=in_specs,
        out_specs=[tok_out(d), tok_out(d // 2 + ROW_EXTRA),
                   pl.BlockSpec((8, tm), lambda b, i: (0, b * nt + i)),
                   pl.BlockSpec((BUCKET_ROWS, 128), lambda b, i: (0, 0))],
        out_shape=[jax.ShapeDtypeStruct((nb, n, d), F32),
                   jax.ShapeDtypeStruct((nb, n, d // 2 + ROW_EXTRA), U32),
                   jax.ShapeDtypeStruct((8, nb * n), F32),
                   jax.ShapeDtypeStruct((BUCKET_ROWS, 128), F32)],
        compiler_params=_params(("arbitrary", "arbitrary")),
        name="mix",
    )(*args)


def _rank_kernel(route_ref, offs_ref, dest_ref, run_ref):
    tr = route_ref.shape[1]

    @pl.when(pl.program_id(0) == 0)
    def _():
        run_ref[...] = jnp.zeros_like(run_ref)

    bucket = route_ref[0:1, :]
    rb = lax.broadcasted_iota(jnp.int32, (BUCKET_ROWS, tr), 0).astype(F32)
    onehot = jnp.where(rb == bucket, 1.0, 0.0)
    s_idx = lax.broadcasted_iota(jnp.int32, (tr, tr), 0)
    t_idx = lax.broadcasted_iota(jnp.int32, (tr, tr), 1)
    tri = jnp.where(s_idx <= t_idx, 1.0, 0.0).astype(BF16)
    prefix = jnp.dot(onehot.astype(BF16), tri, preferred_element_type=F32)
    before = run_ref[:, 0:1] + offs_ref[:, 0:1]
    dest = jnp.sum(onehot * (prefix - 1.0 + before), axis=0, keepdims=True)
    dest_ref[...] = dest.astype(jnp.int32)
    run_ref[...] += prefix[:, tr - 1:tr]


def _rank(route_t, offs_rows, tr):
    n = route_t.shape[1]
    return pl.pallas_call(
        _rank_kernel,
        grid=(n // tr,),
        in_specs=[pl.BlockSpec((8, tr), lambda i: (0, i)),
                  pl.BlockSpec((BUCKET_ROWS, 128), lambda i: (0, 0))],
        out_specs=pl.BlockSpec((1, tr), lambda i: (0, i)),
        out_shape=jax.ShapeDtypeStruct((1, n), jnp.int32),
        scratch_shapes=[pltpu.VMEM((BUCKET_ROWS, 128), F32)],
        compiler_params=_params(("arbitrary",)),
        name="rank",
    )(route_t, offs_rows)


def _sc_mesh():
    return plsc.VectorSubcoreMesh(core_axis_name="core", subcore_axis_name="subcore")


def _scatter_rows(src, dest, n_out):
    n, w = src.shape
    sub = SC_INDEX_TILE // SC_ROWS

    @functools.partial(pl.kernel, out_type=jax.ShapeDtypeStruct((n_out, w), src.dtype), mesh=_sc_mesh(),
                       scratch_types=[], name="scatter_rows")
    def scatter(x_hbm, i_hbm, o_hbm):
        def body(x_vmem, i_vmem):
            j = pl.program_id(1)
            pltpu.sync_copy(x_vmem, o_hbm.at[i_vmem.at[0, pl.ds(j * SC_ROWS, SC_ROWS)]])

        pltpu.emit_pipeline(
            body,
            grid=(n // SC_INDEX_TILE, sub),
            in_specs=[pl.BlockSpec((SC_ROWS, w), lambda i, j: (i * sub + j, 0)),
                      pl.BlockSpec((1, SC_INDEX_TILE), lambda i, j: (0, i))],
            out_specs=[],
            core_axis_name=("core", "subcore"),
            dimension_semantics=(pltpu.PARALLEL, pltpu.ARBITRARY),
        )(x_hbm, i_hbm)

    return scatter(src, dest)


def _gather_rows(src, idx):
    n = idx.shape[1]
    w = src.shape[1]
    sub = SC_INDEX_TILE // SC_ROWS

    @functools.partial(pl.kernel, out_type=jax.ShapeDtypeStruct((n, w), src.dtype), mesh=_sc_mesh(),
                       scratch_types=[], name="gather_rows")
    def gather(x_hbm, i_hbm, o_hbm):
        def body(i_vmem, o_vmem):
            j = pl.program_id(1)
            pltpu.sync_copy(x_hbm.at[i_vmem.at[0, pl.ds(j * SC_ROWS, SC_ROWS)]], o_vmem)

        pltpu.emit_pipeline(
            body,
            grid=(n // SC_INDEX_TILE, sub),
            in_specs=[pl.BlockSpec((1, SC_INDEX_TILE), lambda i, j: (0, i))],
            out_specs=[pl.BlockSpec((SC_ROWS, w), lambda i, j: (i * sub + j, 0))],
            core_axis_name=("core", "subcore"),
            dimension_semantics=(pltpu.PARALLEL, pltpu.ARBITRARY),
        )(i_hbm, o_hbm)

    return gather(src, idx)


def _moe_kernel(ea_ref, eb_ref, valid_ref, new_ref, rows_ref, w1a_ref, w3a_ref, w2a_ref, w1b_ref, w3b_ref, w2b_ref,
                o_ref, *w_scr):
    j = pl.program_id(0)
    half = rows_ref.shape[1] - ROW_EXTRA

    @pl.when(new_ref[j] != 0)
    def _():
        for src, dst in zip((w1a_ref, w3a_ref, w2a_ref, w1b_ref, w3b_ref, w2b_ref), w_scr):
            dst[...] = src[0].astype(BF16)

    @pl.when(valid_ref[j] != 0)
    def _():
        ha, hb = _unpack_bf16_pair(rows_ref[:, 0:half])
        ha = ha.astype(BF16)
        hb = hb.astype(BF16)
        gates = lax.bitcast_convert_type(rows_ref[:, half:], F32)

        def expert(w1_s, w3_s, w2_s, gate):
            def up(w_s):
                return (jnp.dot(ha, w_s[0:half, :], preferred_element_type=F32)
                        + jnp.dot(hb, w_s[half:, :], preferred_element_type=F32))

            a1 = up(w1_s)
            he = (a1 * jax.nn.sigmoid(a1)) * up(w3_s) * gate
            return jnp.dot(he.astype(BF16), w2_s[...], preferred_element_type=F32)

        y = expert(*w_scr[0:3], gates[:, 0:1]) + expert(*w_scr[3:6], gates[:, 1:2])
        o_ref[...] = _pack_bf16_pair(y[:, :half], y[:, half:])


def _moe_grouped(rows, tile_ea, tile_eb, tile_valid, tile_new, w1, w3, w2, tmm):
    r, w = rows.shape
    n_e, de, d = w2.shape
    up_a = pl.BlockSpec((1, d, de), lambda j, ea, eb, va, nw: (ea[j], 0, 0))
    up_b = pl.BlockSpec((1, d, de), lambda j, ea, eb, va, nw: (eb[j], 0, 0))
    grid_spec = pltpu.PrefetchScalarGridSpec(
        num_scalar_prefetch=4,
        grid=(r // tmm,),
        in_specs=[pl.BlockSpec((tmm, w), lambda j, ea, eb, va, nw: (j, 0)),
                  up_a, up_a, pl.BlockSpec((1, de, d), lambda j, ea, eb, va, nw: (ea[j], 0, 0)),
                  up_b, up_b, pl.BlockSpec((1, de, d), lambda j, ea, eb, va, nw: (eb[j], 0, 0))],
        out_specs=pl.BlockSpec((tmm, d // 2), lambda j, ea, eb, va, nw: (j, 0)),
        scratch_shapes=[pltpu.VMEM((d, de), BF16), pltpu.VMEM((d, de), BF16), pltpu.VMEM((de, d), BF16)] * 2,
    )
    return pl.pallas_call(
        _moe_kernel,
        grid_spec=grid_spec,
        out_shape=jax.ShapeDtypeStruct((r, d // 2), U32),
        compiler_params=_params(("arbitrary",)),
        name="moe",
    )(tile_ea, tile_eb, tile_valid, tile_new, rows, w1, w3, w2, w1, w3, w2)


def _final_kernel(x1_ref, moe_ref, mod_ref, fg_ref, *rest):
    o_ref = rest[-1]
    ya, yb = _unpack_bf16_pair(moe_ref[0])
    half = ya.shape[1]
    gate = mod_ref[0, 5:6, :]
    xa = x1_ref[0, :, 0:half] + gate[:, 0:half] * ya
    xb = x1_ref[0, :, half:] + gate[:, half:] * yb
    ms = (jnp.sum(xa * xa, axis=-1, keepdims=True) + jnp.sum(xb * xb, axis=-1, keepdims=True)) / (2 * half)
    inv = lax.rsqrt(ms + RMS_EPS)
    o_ref[0, :, 0:half] = xa * inv * fg_ref[:, 0:half]
    o_ref[0, :, half:] = xb * inv * fg_ref[:, half:]


def _final(x1, moe_tok, mods, final_g, tm, b0, bsz, out_prev):
    nb, n, d = x1.shape
    tok = lambda w: pl.BlockSpec((1, tm, w), lambda b, i: (b, i, 0))
    args = [x1, moe_tok, mods, final_g.reshape(1, d)]
    in_specs = [tok(d), tok(d // 2), pl.BlockSpec((1, N_MOD, d), lambda b, i: (b + b0, 0, 0)),
                pl.BlockSpec((1, d), lambda b, i: (0, 0))]
    aliases = {}
    if out_prev is not None:
        args.append(out_prev)
        in_specs.append(pl.BlockSpec(memory_space=pl.ANY))
        aliases = {len(args) - 1: 0}
    return pl.pallas_call(
        _final_kernel,
        grid=(nb, n // tm),
        in_specs=in_specs,
        out_specs=pl.BlockSpec((1, tm, d), lambda b, i: (b + b0, i, 0)),
        out_shape=jax.ShapeDtypeStruct((bsz, n, d), F32),
        input_output_aliases=aliases,
        compiler_params=_params(("parallel", "parallel")),
        name="final",
    )(*args)


def _tile_plan(counts, tmm, n_tiles):
    tiles = (counts + (tmm - 1)) // tmm
    tile_end = jnp.cumsum(tiles)
    offs = (tile_end - tiles) * tmm
    n_valid = tile_end[-1]
    j = jnp.arange(n_tiles, dtype=jnp.int32)
    bucket = jnp.sum((tile_end[None, :] <= jnp.minimum(j, n_valid - 1)[:, None]).astype(jnp.int32), axis=1)
    pair_lo = jnp.array([0, 0, 0, 1, 1, 2], jnp.int32)
    pair_hi = jnp.array([1, 2, 3, 2, 3, 3], jnp.int32)
    group = bucket // PAIRS_PER_GROUP
    pair = bucket % PAIRS_PER_GROUP
    tile_ea = group * EXPERTS_PER_GROUP + pair_lo[pair]
    tile_eb = group * EXPERTS_PER_GROUP + pair_hi[pair]
    tile_new = jnp.concatenate([jnp.ones((1,), jnp.int32), (bucket[1:] != bucket[:-1]).astype(jnp.int32)])
    return offs, tile_ea, tile_eb, (j < n_valid).astype(jnp.int32), tile_new


def kernel(x, c, ctx, c_ctx, w_mod, b_mod, norm1_g, norm2_g, w_in, s5_lambda_re, s5_lambda_im, s5_log_dt,
           s5_b_re, s5_b_im, s5_c_re, s5_c_im, s5_d, w_glu, b_glu, conv_w, w_out, router_group_w,
           router_group_b, router_expert_w, router_expert_b, expert_w1, expert_w3, expert_w2, final_g):
    assert w_mod.shape[0] == 1, "single-layer kernel"
    bsz, n_tok, d = x.shape
    n_ctx = ctx.shape[1]
    n_all = bsz * n_tok
    l = 0
    tm = min(TOKEN_TILE, n_tok)

    n_cond = bsz + 1
    pad = (-n_cond) % 8
    cond = jnp.concatenate([c, c_ctx[None, :], jnp.zeros((pad, d), F32)], axis=0)
    m = _mod_rows(cond, w_mod[l], b_mod[l])
    mx = m[:bsz].reshape(bsz, N_MOD, d)
    mc = m[bsz:bsz + 1].reshape(1, N_MOD, d)

    w_in_b = w_in[l].astype(BF16)
    u, bg, cv = _inproj(x, mx, True, norm1_g[l], w_in_b, tm, True)
    (uc,) = _inproj(ctx, mc, False, norm1_g[l], w_in_b[:, :S5_WIDTH], min(TOKEN_TILE, n_ctx), False)

    t_mat, mb_pair, mc_pair, a_rows = _s5_matrices(
        s5_lambda_re[l], s5_lambda_im[l], s5_log_dt[l], s5_b_re[l], s5_b_im[l], s5_c_re[l], s5_c_im[l], s5_d[l])
    y_c = _s5_scan(_chunkify(u, "chunkify"), _chunkify(uc, "chunkify_ctx"), t_mat, mb_pair, mc_pair, a_rows)
    y_s5 = _unchunkify(y_c, u.shape[1])

    n_logits = N_GROUPS + N_EXPERTS
    w_router = jnp.concatenate(
        [router_group_w[l], router_expert_w[l], jnp.zeros((d, ROUTER_ROWS - n_logits), F32)], axis=1).T
    w_router_hi = w_router.astype(BF16)
    w_router_lo = (w_router - w_router_hi.astype(F32)).astype(BF16)
    w_router_t = jnp.concatenate([w_router_hi, w_router_lo], axis=0)
    b_router = jnp.concatenate([router_group_b[l], router_expert_b[l], jnp.zeros((ROUTER_ROWS - n_logits,), F32)])
    b_router_t = jnp.broadcast_to(b_router[:, None], (ROUTER_ROWS, tm))

    n_parts = MOE_PARTS if bsz % MOE_PARTS == 0 else 1
    nb = bsz // n_parts
    n_part = nb * n_tok
    n_buckets = N_GROUPS * PAIRS_PER_GROUP
    n_rows = n_part + n_buckets * MOE_TILE
    w_glu_b = w_glu[l].astype(BF16)
    w_out_b = w_out[l].astype(BF16)
    staged = []
    for p in range(n_parts):
        x1, h2p, route_t, counts = _mix(x, y_s5, bg, cv, mx, norm2_g[l], conv_w[l], w_glu_b, b_glu[l], w_out_b,
                                        w_router_t, b_router_t, tm, p * nb, nb)
        offs, *tiles = _tile_plan(counts[:n_buckets, 0].astype(jnp.int32), MOE_TILE, n_rows // MOE_TILE)
        offs_rows = jnp.zeros((BUCKET_ROWS,), F32).at[:n_buckets].set(offs.astype(F32))
        dest = _rank(route_t, jnp.broadcast_to(offs_rows[:, None], (BUCKET_ROWS, 128)), min(RANK_TILE, n_part))
        rows = _scatter_rows(h2p.reshape(n_part, d // 2 + ROW_EXTRA), dest, n_rows)
        staged.append((x1, rows, dest, tiles))
    out = None
    for p, (x1, rows, dest, tiles) in enumerate(staged):
        y_rows = _moe_grouped(rows, *tiles, expert_w1[l], expert_w3[l], expert_w2[l], MOE_TILE)
        moe_tok = _gather_rows(y_rows, dest).reshape(nb, n_tok, d // 2)
        out = _final(x1, moe_tok, mx, final_g, tm, p * nb, bsz, out)
    return out
```

```python
import functools

import jax
import jax.numpy as jnp
from jax import lax
from jax.experimental import pallas as pl
from jax.experimental.pallas import tpu as pltpu
from jax.experimental.pallas import tpu_sc as plsc

F32 = jnp.float32
BF16 = jnp.bfloat16
U32 = jnp.uint32

RMS_EPS = 1e-6
N_MOD = 6
GRID_W = 64
S5_WIDTH = 256
S5_H = 16
S5_P = 64
S5_GROUPS = S5_WIDTH // S5_H
S5_PAIRS = S5_GROUPS // 2
S5_CHUNK = 16
LANES = 128
STATE_PITCH = 136
CONV_WIDTH = 768
CONV_ROW_WIDTH = CONV_WIDTH // 2
N_GROUPS = 4
EXPERTS_PER_GROUP = 4
N_EXPERTS = N_GROUPS * EXPERTS_PER_GROUP
PAIRS_PER_GROUP = 6
ROUTER_ROWS = 32
BUCKET_ROWS = 32
ROW_EXTRA = 128
TOKEN_TILE = 512
MOE_TILE = 512
RANK_TILE = 2048
MIX_TILE = 1024
MIX_SUB = 512
MOE_PARTS = 2
SC_ROWS = 32
SC_INDEX_TILE = 128
VMEM_LIMIT = 52 * 1024 * 1024


def _params(sem, vmem=VMEM_LIMIT):
    return pltpu.CompilerParams(dimension_semantics=sem, vmem_limit_bytes=vmem)


def _mod_kernel(c_ref, w_ref, b_ref, o_ref):
    c = c_ref[...]
    o_ref[...] = jnp.dot(c * jax.nn.sigmoid(c), w_ref[...], preferred_element_type=F32) + b_ref[...]


def _mod_rows(cond, w_mod, b_mod):
    n, d = cond.shape
    nout = w_mod.shape[1]
    bn = d
    return pl.pallas_call(
        _mod_kernel,
        grid=(nout // bn,),
        in_specs=[pl.BlockSpec((n, d), lambda j: (0, 0)),
                  pl.BlockSpec((d, bn), lambda j: (0, j)),
                  pl.BlockSpec((1, bn), lambda j: (0, j))],
        out_specs=pl.BlockSpec((n, bn), lambda j: (0, j)),
        out_shape=jax.ShapeDtypeStruct((n, nout), F32),
        compiler_params=_params(("arbitrary",)),
        name="mod",
    )(cond, w_mod, b_mod.reshape(1, nout))


def _modulated_norm(x, g, shift, scale):
    ms = jnp.mean(x * x, axis=-1, keepdims=True)
    return (x * lax.rsqrt(ms + RMS_EPS) * g) * (1.0 + scale) + shift


def _inproj_kernel(x_ref, mod_ref, g_ref, w_ref, u_ref):
    h = _modulated_norm(x_ref[0], g_ref[...], mod_ref[0, 0:1, :], mod_ref[0, 1:2, :]).astype(BF16)
    u = jnp.dot(h, w_ref[...], preferred_element_type=F32)
    for j in range(S5_WIDTH // LANES):
        u_ref[0, j] = u[:, j * LANES:(j + 1) * LANES]


def _inproj(x, mods, per_batch_mod, norm_g, w_s5, tm, name):
    bsz, n, d = x.shape
    mod_map = (lambda b, i: (b, 0, 0)) if per_batch_mod else (lambda b, i: (0, 0, 0))
    n_slab = S5_WIDTH // LANES
    return pl.pallas_call(
        _inproj_kernel,
        grid=(bsz, n // tm),
        in_specs=[pl.BlockSpec((1, tm, d), lambda b, i: (b, i, 0)),
                  pl.BlockSpec((1, N_MOD, d), mod_map),
                  pl.BlockSpec((1, d), lambda b, i: (0, 0)),
                  pl.BlockSpec((d, S5_WIDTH), lambda b, i: (0, 0))],
        out_specs=pl.BlockSpec((1, n_slab, tm, LANES), lambda b, i: (b, 0, i, 0)),
        out_shape=jax.ShapeDtypeStruct((bsz, n_slab, n, LANES), F32),
        compiler_params=_params(("parallel", "parallel")),
        name=name,
    )(x, mods, norm_g.reshape(1, d), w_s5)


def _toeplitz_kernel(strip_ref, t_ref):
    lc = t_ref.shape[1] // S5_H
    for s in range(lc):
        off = (lc - 1 - s) * S5_H
        t_ref[0, s * S5_H:(s + 1) * S5_H, :] = strip_ref[0, :, off:off + lc * S5_H].astype(BF16)


def _toeplitz(strip):
    g_n, h_n, w = strip.shape
    n = S5_CHUNK * h_n
    return pl.pallas_call(
        _toeplitz_kernel,
        grid=(g_n,),
        in_specs=[pl.BlockSpec((1, h_n, w), lambda g: (g, 0, 0))],
        out_specs=pl.BlockSpec((1, n, n), lambda g: (g, 0, 0)),
        out_shape=jax.ShapeDtypeStruct((g_n, n, n), BF16),
        compiler_params=_params(("parallel",)),
        name="toeplitz",
    )(strip)


def _s5_matrices(lam_re, lam_im, log_dt, b_re, b_im, c_re, c_im, d_skip):
    lc, g_n, p_n, h_n = S5_CHUNK, S5_GROUPS, S5_P, S5_H
    lam = lax.complex(lam_re.astype(F32), lam_im.astype(F32))
    dt = jnp.exp(log_dt.astype(F32))[..., None]
    a_bar = jnp.exp(lam * dt)
    b_bar = ((a_bar - 1.0) / lam)[..., None] * lax.complex(b_re.astype(F32), b_im.astype(F32))
    cm = lax.complex(c_re.astype(F32), c_im.astype(F32))
    steps = jnp.arange(lc + 1, dtype=F32)
    apow = jnp.exp((lam * dt)[:, :, None, :] * steps[None, None, :, None])
    kern = jnp.einsum('dgop,dgjp,dgpi->dgjio', cm, apow[:, :, :lc], b_bar).real
    skip = jnp.eye(h_n, dtype=F32) * d_skip.astype(F32).reshape(g_n, 1, h_n)
    centre = kern[0, :, 0] + kern[1, :, 0] + skip
    lags = jnp.concatenate([kern[1, :, :0:-1], centre[:, None], kern[0, :, 1:]], axis=1)
    strip = lags.transpose(0, 2, 1, 3).reshape(g_n, h_n, (2 * lc - 1) * h_n)
    strip = jnp.pad(strip, ((0, 0), (0, 0), (0, h_n)))
    t_mat = _toeplitz(strip)

    def in_mat(pw, bb):
        return (pw[:, :, None, :] * bb.transpose(0, 2, 1)[:, None, :, :]).reshape(g_n, lc * h_n, p_n)

    mb_f = in_mat(apow[0, :, lc - 1::-1][:, :lc], b_bar[0])
    mb_b = in_mat(apow[1, :, :lc], b_bar[1])

    def out_mat(pw, cc):
        return (pw.transpose(0, 2, 1)[:, :, :, None] * cc.transpose(0, 2, 1)[:, :, None, :]).reshape(
            g_n, p_n, lc * h_n)

    mc_f = out_mat(apow[0, :, 1:lc + 1], cm[0])
    mc_b = out_mat(apow[1, :, lc:0:-1], cm[1])
    a_chunk = apow[:, :, lc]

    q_n = S5_PAIRS
    zeros_in = jnp.zeros((g_n, lc * h_n, p_n), F32)

    def pair_cols(m):
        m = m.reshape(q_n, 2, lc * h_n, p_n)
        z = zeros_in.reshape(q_n, 2, lc * h_n, p_n)[:, 0]
        top = jnp.concatenate([m[:, 0], z], axis=-1)
        bot = jnp.concatenate([z, m[:, 1]], axis=-1)
        return jnp.concatenate([top, bot], axis=1)

    mb_pair = jnp.concatenate([pair_cols(mb_f.real), pair_cols(mb_f.imag),
                               pair_cols(mb_b.real), pair_cols(mb_b.imag)], axis=-1)

    def pair_rows(m):
        m = m.reshape(q_n, 2, p_n, lc * h_n)
        z = jnp.zeros_like(m[:, 0])
        top = jnp.concatenate([m[:, 0], z], axis=-1)
        bot = jnp.concatenate([z, m[:, 1]], axis=-1)
        return jnp.concatenate([top, bot], axis=1)

    mc_pair = jnp.concatenate([pair_rows(mc_f.real), pair_rows(-mc_f.imag),
                               pair_rows(mc_b.real), pair_rows(-mc_b.imag)], axis=1)
    a_rows = jnp.stack([a_chunk[0].real, a_chunk[0].imag, a_chunk[1].real, a_chunk[1].imag], axis=0)
    a_rows = a_rows.reshape(4, q_n, 2 * p_n).transpose(1, 0, 2)
    a_rows = jnp.concatenate([a_rows, jnp.zeros_like(a_rows)], axis=1)
    return t_mat, mb_pair.astype(BF16), mc_pair.astype(BF16), a_rows


def _chunkify_kernel(u_ref, o_ref):
    _, nb, nc, _ = o_ref.shape
    half = S5_CHUNK * S5_H
    per_slab = LANES // S5_H
    for j in range(u_ref.shape[1]):
        cols = []
        for t in range(S5_CHUNK):
            rows = [u_ref[b, j, pl.ds(t, nc, stride=S5_CHUNK), :] for b in range(nb)]
            cols.append((rows[0] if nb == 1 else jnp.concatenate(rows, axis=0)).T)
        for gl in range(per_slab):
            g = j * per_slab + gl
            m = jnp.concatenate([c[gl * S5_H:(gl + 1) * S5_H, :] for c in cols], axis=0)
            o_ref[g // 2, :, :, (g % 2) * half:(g % 2 + 1) * half] = m.T.astype(BF16).reshape(nb, nc, half)


def _chunkify(u_slab, name):
    bsz, n_slab, n, _ = u_slab.shape
    nc = n // S5_CHUNK
    nb = min(bsz, max(1, LANES // nc))
    w = 2 * S5_CHUNK * S5_H
    return pl.pallas_call(
        _chunkify_kernel,
        grid=(bsz // nb,),
        in_specs=[pl.BlockSpec((nb, n_slab, n, LANES), lambda b: (b, 0, 0, 0))],
        out_specs=pl.BlockSpec((S5_PAIRS, nb, nc, w), lambda b: (0, b, 0, 0)),
        out_shape=jax.ShapeDtypeStruct((S5_PAIRS, bsz, nc, w), BF16),
        compiler_params=_params(("parallel",)),
        name=name,
    )(u_slab)


def _unchunkify_kernel(y_ref, o_ref):
    nc = y_ref.shape[2]
    half = S5_CHUNK * S5_H
    per_slab = LANES // S5_H
    for j in range(o_ref.shape[1]):
        rows = []
        for gl in range(per_slab):
            g = j * per_slab + gl
            rows.append(y_ref[g // 2, 0, :, (g % 2) * half:(g % 2 + 1) * half].astype(F32).T)
        for t in range(S5_CHUNK):
            bt = jnp.concatenate([r[t * S5_H:(t + 1) * S5_H, :] for r in rows], axis=0)
            o_ref[0, j, pl.ds(t, nc, stride=S5_CHUNK), :] = bt.T


def _unchunkify(y_c, n_slab):
    q_n, bsz, nc, w = y_c.shape
    n = nc * S5_CHUNK
    return pl.pallas_call(
        _unchunkify_kernel,
        grid=(bsz,),
        in_specs=[pl.BlockSpec((q_n, 1, nc, w), lambda b: (0, b, 0, 0))],
        out_specs=pl.BlockSpec((1, n_slab, n, LANES), lambda b: (b, 0, 0, 0)),
        out_shape=jax.ShapeDtypeStruct((bsz, n_slab, n, LANES), F32),
        compiler_params=_params(("parallel",)),
        name="unchunkify",
    )(y_c)


def _s5_kernel(u_ref, uc_ref, t_ref, mb_ref, mc_ref, a_ref, y_ref, s_lat, s_ctx, h_scr, *, bb):
    _, bsz, n_lat, w = u_ref.shape
    n_ctx = uc_ref.shape[2]
    n_blk = w // LANES
    rb = bb * n_lat

    mb = mb_ref[0]

    def in_lat(i, carry):
        s = jnp.dot(u_ref[0, pl.ds(i * bb, bb)].reshape(rb, w), mb, preferred_element_type=F32)
        for k in range(bb):
            r = pl.multiple_of((i * bb + k) * STATE_PITCH, 8)
            for blk in range(n_blk):
                s_lat[blk, pl.ds(r, n_lat), :] = s[k * n_lat:(k + 1) * n_lat, blk * LANES:(blk + 1) * LANES]
        return carry

    lax.fori_loop(0, bsz // bb, in_lat, 0)
    sc = jnp.dot(uc_ref[0].reshape(bsz * n_ctx, w), mb, preferred_element_type=F32)
    for blk in range(n_blk):
        s_ctx[blk] = sc[:, blk * LANES:(blk + 1) * LANES]

    a_fr, a_fi, a_br, a_bi = (a_ref[0, k:k + 1, :] for k in range(4))

    def step(h, a_r, a_i, s_r, s_i):
        h_r, h_i = h
        return a_r * h_r - a_i * h_i + s_r, a_r * h_i + a_i * h_r + s_i

    def ctx_rows(blk, c):
        return s_ctx[blk, pl.ds(c, bsz, stride=n_ctx), :]

    def lat_rows(ref, blk, c):
        return ref.at[blk, pl.ds(c, bsz, stride=STATE_PITCH), :]

    def ctx_step(k, carry):
        hf, hb = carry
        kb = n_ctx - 1 - k
        hf = step(hf, a_fr, a_fi, ctx_rows(0, k), ctx_rows(1, k))
        hb = step(hb, a_br, a_bi, ctx_rows(2, kb), ctx_rows(3, kb))
        return hf, hb

    zero = jnp.zeros((bsz, LANES), F32)
    carry = lax.fori_loop(0, n_ctx, ctx_step, ((zero, zero), (zero, zero)))

    def lat_step(k, carry):
        hf, hb = carry
        kb = n_lat - 1 - k
        lat_rows(h_scr, 0, k)[...] = hf[0]
        lat_rows(h_scr, 1, k)[...] = hf[1]
        lat_rows(h_scr, 2, kb)[...] = hb[0]
        lat_rows(h_scr, 3, kb)[...] = hb[1]
        hf = step(hf, a_fr, a_fi, lat_rows(s_lat, 0, k)[...], lat_rows(s_lat, 1, k)[...])
        hb = step(hb, a_br, a_bi, lat_rows(s_lat, 2, kb)[...], lat_rows(s_lat, 3, kb)[...])
        return hf, hb

    lax.fori_loop(0, n_lat, lat_step, carry)

    t0 = t_ref[0]
    t1 = t_ref[1]
    mc = mc_ref[0]
    half = S5_CHUNK * S5_H

    def out_lat(i, carry):
        u = u_ref[0, pl.ds(i * bb, bb)].reshape(rb, w)
        h_rows = []
        for k in range(bb):
            r = pl.multiple_of((i * bb + k) * STATE_PITCH, 8)
            h_rows.append(jnp.concatenate([h_scr[blk, pl.ds(r, n_lat), :] for blk in range(n_blk)], axis=1))
        h = jnp.concatenate(h_rows, axis=0).astype(BF16)
        inter = jnp.dot(h, mc, preferred_element_type=F32)
        y0 = jnp.dot(u[:, :half], t0, preferred_element_type=F32) + inter[:, :half]
        y1 = jnp.dot(u[:, half:], t1, preferred_element_type=F32) + inter[:, half:]
        y = jnp.concatenate([y0, y1], axis=1).astype(BF16)
        y_ref[0, pl.ds(i * bb, bb)] = y.reshape(bb, n_lat, w)
        return carry

    lax.fori_loop(0, bsz // bb, out_lat, 0)


def _s5_scan(u_c, uc_c, t_mat, mb_pair, mc_pair, a_rows):
    q_n, bsz, n_lat, w = u_c.shape
    n_ctx = uc_c.shape[2]
    assert n_lat + 8 == STATE_PITCH
    bb = min(4, bsz)
    n_blk = w // LANES
    return pl.pallas_call(
        functools.partial(_s5_kernel, bb=bb),
        grid=(q_n,),
        in_specs=[pl.BlockSpec((1, bsz, n_lat, w), lambda q: (q, 0, 0, 0)),
                  pl.BlockSpec((1, bsz, n_ctx, w), lambda q: (q, 0, 0, 0)),
                  pl.BlockSpec((2, w // 2, w // 2), lambda q: (q, 0, 0)),
                  pl.BlockSpec((1, w, w), lambda q: (q, 0, 0)),
                  pl.BlockSpec((1, w, w), lambda q: (q, 0, 0)),
                  pl.BlockSpec((1, 8, LANES), lambda q: (q, 0, 0))],
        out_specs=pl.BlockSpec((1, bsz, n_lat, w), lambda q: (q, 0, 0, 0)),
        out_shape=jax.ShapeDtypeStruct((q_n, bsz, n_lat, w), BF16),
        scratch_shapes=[pltpu.VMEM((n_blk, bsz * STATE_PITCH, LANES), F32),
                        pltpu.VMEM((n_blk, bsz * n_ctx, LANES), F32),
                        pltpu.VMEM((n_blk, bsz * STATE_PITCH, LANES), F32)],
        compiler_params=_params(("parallel",)),
        name="s5_scan",
    )(u_c, uc_c, t_mat, mb_pair, mc_pair, a_rows)


def _pack_bf16_pair(a, b):
    ua = lax.bitcast_convert_type(a.astype(BF16).astype(F32), U32)
    ub = lax.bitcast_convert_type(b.astype(BF16).astype(F32), U32)
    return ua | (ub >> 16)


def _unpack_bf16_pair(w):
    a = lax.bitcast_convert_type(w & jnp.uint32(0xFFFF0000), F32)
    b = lax.bitcast_convert_type(w << 16, F32)
    return a, b


def _first_max(rows):
    best = rows[0]
    for r in rows[1:]:
        best = jnp.maximum(best, r)
    idx = jnp.full(best.shape, float(len(rows) - 1), F32)
    for k in range(len(rows) - 2, -1, -1):
        idx = jnp.where(rows[k] == best, float(k), idx)
    return best, idx


def _route_rows(lg):
    g_rows = [lg[k:k + 1] for k in range(N_GROUPS)]
    g_max, g_idx = _first_max(g_rows)
    g_sum = sum(jnp.exp(r - g_max) for r in g_rows)
    g_p = 1.0 / g_sum
    e_rows = []
    for j in range(EXPERTS_PER_GROUP):
        r = lg[N_GROUPS + (N_GROUPS - 1) * EXPERTS_PER_GROUP + j:][:1]
        for g in range(N_GROUPS - 2, -1, -1):
            k = N_GROUPS + g * EXPERTS_PER_GROUP + j
            r = jnp.where(g_idx == float(g), lg[k:k + 1], r)
        e_rows.append(r)
    v1, i1 = _first_max(e_rows)
    rest = [jnp.where(i1 == float(j), -jnp.inf, e_rows[j]) for j in range(EXPERTS_PER_GROUP)]
    v2, i2 = _first_max(rest)
    e21 = jnp.exp(v2 - v1)
    w1 = g_p / (1.0 + e21)
    w2 = w1 * e21
    lo = jnp.minimum(i1, i2)
    hi = jnp.maximum(i1, i2)
    base = jnp.where(lo == 0.0, 0.0, jnp.where(lo == 1.0, 3.0, 5.0))
    bucket = g_idx * float(PAIRS_PER_GROUP) + base + hi - lo - 1.0
    first_is_lo = i1 < i2
    return bucket, jnp.where(first_is_lo, w1, w2), jnp.where(first_is_lo, w2, w1)


def _mix_kernel(x_ref, xup_ref, xdn_ref, y_ref, mod_ref, g1_ref, g2_ref, cw_ref, win_ref, wglu_ref, bglu_ref,
                wout_ref, wr_ref, br_ref, x1_ref, h2p_ref, route_ref, counts_ref):
    i = pl.program_id(1)
    tm = x_ref.shape[1]
    d = x_ref.shape[2]
    sub = min(MIX_SUB, tm)
    n_sub = tm // sub
    cw, rw = CONV_WIDTH, CONV_ROW_WIDTH

    def hidden(xv):
        return _modulated_norm(xv, g1_ref[...], mod_ref[0, 0:1, :], mod_ref[0, 1:2, :]).astype(BF16)

    def halo(xv):
        zh = jnp.dot(hidden(xv), win_ref[:, 2 * cw:3 * cw], preferred_element_type=F32)
        return zh[:, 0:rw] * zh[:, rw:cw]

    @pl.when(jnp.logical_and(pl.program_id(0) == 0, i == 0))
    def _():
        counts_ref[...] = jnp.zeros_like(counts_ref)

    for s in range(n_sub):
        r0 = s * sub
        xv = x_ref[0, r0:r0 + sub, :]
        g = jax.nn.gelu(jnp.concatenate([y_ref[0, j, r0:r0 + sub, :] for j in range(y_ref.shape[1])], axis=1))
        glu = g * jax.nn.sigmoid(jnp.dot(g.astype(BF16), wglu_ref[...], preferred_element_type=F32) + bglu_ref[...])
        hx = hidden(xv)
        z_r = jnp.dot(hx, win_ref[:, cw:2 * cw], preferred_element_type=F32)
        cr = z_r[:, 0:rw] * z_r[:, rw:cw]
        z_c = jnp.dot(hx, win_ref[:, 2 * cw:3 * cw], preferred_element_type=F32)
        cc = z_c[:, 0:rw] * z_c[:, rw:cw]
        bg = jnp.dot(hx, win_ref[:, 0:cw], preferred_element_type=F32)
        row = lax.broadcasted_iota(jnp.int32, (sub, 1), 0)
        col_in_row = row % GRID_W
        left = jnp.where(col_in_row == 0, 0.0, pltpu.roll(cr, 1, axis=0))
        right = jnp.where(col_in_row == GRID_W - 1, 0.0, pltpu.roll(cr, sub - 1, axis=0))
        w_r = cw_ref[:, :rw]
        row_part = left * w_r[0:1] + cr * w_r[1:2] + right * w_r[2:3]
        if s == 0:
            up_halo = jnp.where(i == 0, 0.0, halo(xup_ref[0]))
        else:
            up_halo = halo(x_ref[0, r0 - GRID_W:r0, :])
        if s == n_sub - 1:
            dn_halo = jnp.where(i == pl.num_programs(1) - 1, 0.0, halo(xdn_ref[0]))
        else:
            dn_halo = halo(x_ref[0, r0 + sub:r0 + sub + GRID_W, :])
        up = jnp.concatenate([up_halo, cc[:sub - GRID_W]], axis=0)
        dn = jnp.concatenate([cc[GRID_W:], dn_halo], axis=0)
        w_c = cw_ref[:, rw:]
        col_part = up * w_c[0:1] + cc * w_c[1:2] + dn * w_c[2:3]
        y_row = (bg[:, 0:rw] * row_part).astype(BF16)
        y_col = (bg[:, rw:cw] * col_part).astype(BF16)
        mixed = jnp.concatenate([glu.astype(BF16), y_row, y_col], axis=1)
        yx = jnp.dot(mixed, wout_ref[...], preferred_element_type=F32)
        x1 = xv + mod_ref[0, 2:3, :] * yx
        x1_ref[0, r0:r0 + sub, :] = x1
        h2 = _modulated_norm(x1, g2_ref[...], mod_ref[0, 3:4, :], mod_ref[0, 4:5, :])
        h2b = h2.astype(BF16)
        lg2 = lax.dot_general(wr_ref[...], h2b, (((1,), (1,)), ((), ())), preferred_element_type=F32)
        lg = lg2[:ROUTER_ROWS] + lg2[ROUTER_ROWS:] + br_ref[...]
        bucket, w_a, w_b = _route_rows(lg)
        r8 = lax.broadcasted_iota(jnp.int32, (8, sub), 0)
        route_ref[:, r0:r0 + sub] = jnp.where(r8 == 0, bucket, jnp.where(r8 == 1, w_a, jnp.where(r8 == 2, w_b, 0.0)))
        rl = lax.broadcasted_iota(jnp.int32, (ROW_EXTRA, sub), 0)
        gates_t = jnp.where(rl == 0, w_a, jnp.where(rl == 1, w_b, 0.0))
        h2p_ref[0, r0:r0 + sub, 0:d // 2] = _pack_bf16_pair(h2[:, :d // 2], h2[:, d // 2:])
        h2p_ref[0, r0:r0 + sub, d // 2:] = lax.bitcast_convert_type(gates_t.T, U32)
        rb = lax.broadcasted_iota(jnp.int32, (BUCKET_ROWS, sub), 0).astype(F32)
        counts_ref[...] += jnp.sum(jnp.where(rb == bucket, 1.0, 0.0), axis=-1, keepdims=True)


def _mix(x, y_s5, mods, norm1_g, norm2_g, conv_w, w_conv, w_glu, b_glu, w_out, w_router_t, b_router_t, tm, b0, nb):
    _, n, d = x.shape
    nt = n // tm
    halo_blocks = n // GRID_W
    per_tile = tm // GRID_W
    tok_out = lambda w: pl.BlockSpec((1, tm, w), lambda b, i: (b, i, 0))
    full = lambda a: pl.BlockSpec(a.shape, lambda b, i: (0,) * a.ndim)
    args = (x, x, x, y_s5, mods, norm1_g.reshape(1, d), norm2_g.reshape(1, d), conv_w, w_conv, w_glu,
            b_glu.reshape(1, -1), w_out, w_router_t, b_router_t)
    in_specs = [pl.BlockSpec((1, tm, d), lambda b, i: (b + b0, i, 0)),
                pl.BlockSpec((1, GRID_W, d), lambda b, i: (b + b0, jnp.maximum(i * per_tile - 1, 0), 0)),
                pl.BlockSpec((1, GRID_W, d),
                             lambda b, i: (b + b0, jnp.minimum((i + 1) * per_tile, halo_blocks - 1), 0)),
                pl.BlockSpec((1, y_s5.shape[1], tm, LANES), lambda b, i: (b + b0, 0, i, 0)),
                pl.BlockSpec((1, N_MOD, d), lambda b, i: (b + b0, 0, 0))] + [full(a) for a in args[5:]]
    return pl.pallas_call(
        _mix_kernel,
        grid=(nb, nt),
        in_specs=in_specs,
        out_specs=[tok_out(d), tok_out(d // 2 + ROW_EXTRA),
                   pl.BlockSpec((8, tm), lambda b, i: (0, b * nt + i)),
                   pl.BlockSpec((BUCKET_ROWS, 128), lambda b, i: (0, 0))],
        out_shape=[jax.ShapeDtypeStruct((nb, n, d), F32),
                   jax.ShapeDtypeStruct((nb, n, d // 2 + ROW_EXTRA), U32),
                   jax.ShapeDtypeStruct((8, nb * n), F32),
                   jax.ShapeDtypeStruct((BUCKET_ROWS, 128), F32)],
        compiler_params=_params(("arbitrary", "arbitrary")),
        name="mix",
    )(*args)


def _rank_kernel(route_ref, offs_ref, dest_ref, run_ref):
    tr = route_ref.shape[1]

    @pl.when(pl.program_id(0) == 0)
    def _():
        run_ref[...] = jnp.zeros_like(run_ref)

    bucket = route_ref[0:1, :]
    rb = lax.broadcasted_iota(jnp.int32, (BUCKET_ROWS, tr), 0).astype(F32)
    onehot = jnp.where(rb == bucket, 1.0, 0.0)
    s_idx = lax.broadcasted_iota(jnp.int32, (tr, tr), 0)
    t_idx = lax.broadcasted_iota(jnp.int32, (tr, tr), 1)
    tri = jnp.where(s_idx <= t_idx, 1.0, 0.0).astype(BF16)
    prefix = jnp.dot(onehot.astype(BF16), tri, preferred_element_type=F32)
    before = run_ref[:, 0:1] + offs_ref[:, 0:1]
    dest = jnp.sum(onehot * (prefix - 1.0 + before), axis=0, keepdims=True)
    dest_ref[...] = dest.astype(jnp.int32)
    run_ref[...] += prefix[:, tr - 1:tr]


def _rank(route_t, offs_rows, tr):
    n = route_t.shape[1]
    return pl.pallas_call(
        _rank_kernel,
        grid=(n // tr,),
        in_specs=[pl.BlockSpec((8, tr), lambda i: (0, i)),
                  pl.BlockSpec((BUCKET_ROWS, 128), lambda i: (0, 0))],
        out_specs=pl.BlockSpec((1, tr), lambda i: (0, i)),
        out_shape=jax.ShapeDtypeStruct((1, n), jnp.int32),
        scratch_shapes=[pltpu.VMEM((BUCKET_ROWS, 128), F32)],
        compiler_params=_params(("arbitrary",)),
        name="rank",
    )(route_t, offs_rows)


def _sc_mesh():
    return plsc.VectorSubcoreMesh(core_axis_name="core", subcore_axis_name="subcore")


def _scatter_rows(src, dest, n_out):
    n, w = src.shape
    sub = SC_INDEX_TILE // SC_ROWS

    @functools.partial(pl.kernel, out_type=jax.ShapeDtypeStruct((n_out, w), src.dtype), mesh=_sc_mesh(),
                       scratch_types=[], name="scatter_rows")
    def scatter(x_hbm, i_hbm, o_hbm):
        def body(x_vmem, i_vmem):
            j = pl.program_id(1)
            pltpu.sync_copy(x_vmem, o_hbm.at[i_vmem.at[0, pl.ds(j * SC_ROWS, SC_ROWS)]])

        pltpu.emit_pipeline(
            body,
            grid=(n // SC_INDEX_TILE, sub),
            in_specs=[pl.BlockSpec((SC_ROWS, w), lambda i, j: (i * sub + j, 0)),
                      pl.BlockSpec((1, SC_INDEX_TILE), lambda i, j: (0, i))],
            out_specs=[],
            core_axis_name=("core", "subcore"),
            dimension_semantics=(pltpu.PARALLEL, pltpu.ARBITRARY),
        )(x_hbm, i_hbm)

    return scatter(src, dest)


def _gather_rows(src, idx):
    n = idx.shape[1]
    w = src.shape[1]
    sub = SC_INDEX_TILE // SC_ROWS

    @functools.partial(pl.kernel, out_type=jax.ShapeDtypeStruct((n, w), src.dtype), mesh=_sc_mesh(),
                       scratch_types=[], name="gather_rows")
    def gather(x_hbm, i_hbm, o_hbm):
        def body(i_vmem, o_vmem):
            j = pl.program_id(1)
            pltpu.sync_copy(x_hbm.at[i_vmem.at[0, pl.ds(j * SC_ROWS, SC_ROWS)]], o_vmem)

        pltpu.emit_pipeline(
            body,
            grid=(n // SC_INDEX_TILE, sub),
            in_specs=[pl.BlockSpec((1, SC_INDEX_TILE), lambda i, j: (0, i))],
            out_specs=[pl.BlockSpec((SC_ROWS, w), lambda i, j: (i * sub + j, 0))],
            core_axis_name=("core", "subcore"),
            dimension_semantics=(pltpu.PARALLEL, pltpu.ARBITRARY),
        )(i_hbm, o_hbm)

    return gather(src, idx)


def _moe_kernel(ea_ref, eb_ref, valid_ref, new_ref, rows_ref, w1a_ref, w3a_ref, w2a_ref, w1b_ref, w3b_ref, w2b_ref,
                o_ref, *w_scr):
    j = pl.program_id(0)
    half = rows_ref.shape[1] - ROW_EXTRA

    @pl.when(new_ref[j] != 0)
    def _():
        for src, dst in zip((w1a_ref, w3a_ref, w2a_ref, w1b_ref, w3b_ref, w2b_ref), w_scr):
            dst[...] = src[0].astype(BF16)

    @pl.when(valid_ref[j] != 0)
    def _():
        ha, hb = _unpack_bf16_pair(rows_ref[:, 0:half])
        ha = ha.astype(BF16)
        hb = hb.astype(BF16)
        gates = lax.bitcast_convert_type(rows_ref[:, half:], F32)

        def expert(w1_s, w3_s, w2_s, gate):
            def up(w_s):
                return (jnp.dot(ha, w_s[0:half, :], preferred_element_type=F32)
                        + jnp.dot(hb, w_s[half:, :], preferred_element_type=F32))

            a1 = up(w1_s)
            he = (a1 * jax.nn.sigmoid(a1)) * up(w3_s) * gate
            return jnp.dot(he.astype(BF16), w2_s[...], preferred_element_type=F32)

        y = expert(*w_scr[0:3], gates[:, 0:1]) + expert(*w_scr[3:6], gates[:, 1:2])
        o_ref[...] = _pack_bf16_pair(y[:, :half], y[:, half:])


def _moe_grouped(rows, tile_ea, tile_eb, tile_valid, tile_new, w1, w3, w2, tmm):
    r, w = rows.shape
    n_e, de, d = w2.shape
    up_a = pl.BlockSpec((1, d, de), lambda j, ea, eb, va, nw: (ea[j], 0, 0))
    up_b = pl.BlockSpec((1, d, de), lambda j, ea, eb, va, nw: (eb[j], 0, 0))
    grid_spec = pltpu.PrefetchScalarGridSpec(
        num_scalar_prefetch=4,
        grid=(r // tmm,),
        in_specs=[pl.BlockSpec((tmm, w), lambda j, ea, eb, va, nw: (j, 0)),
                  up_a, up_a, pl.BlockSpec((1, de, d), lambda j, ea, eb, va, nw: (ea[j], 0, 0)),
                  up_b, up_b, pl.BlockSpec((1, de, d), lambda j, ea, eb, va, nw: (eb[j], 0, 0))],
        out_specs=pl.BlockSpec((tmm, d // 2), lambda j, ea, eb, va, nw: (j, 0)),
        scratch_shapes=[pltpu.VMEM((d, de), BF16), pltpu.VMEM((d, de), BF16), pltpu.VMEM((de, d), BF16)] * 2,
    )
    return pl.pallas_call(
        _moe_kernel,
        grid_spec=grid_spec,
        out_shape=jax.ShapeDtypeStruct((r, d // 2), U32),
        compiler_params=_params(("arbitrary",)),
        name="moe",
    )(tile_ea, tile_eb, tile_valid, tile_new, rows, w1, w3, w2, w1, w3, w2)


def _final_kernel(x1_ref, moe_ref, mod_ref, fg_ref, *rest):
    o_ref = rest[-1]
    ya, yb = _unpack_bf16_pair(moe_ref[0])
    half = ya.shape[1]
    gate = mod_ref[0, 5:6, :]
    xa = x1_ref[0, :, 0:half] + gate[:, 0:half] * ya
    xb = x1_ref[0, :, half:] + gate[:, half:] * yb
    ms = (jnp.sum(xa * xa, axis=-1, keepdims=True) + jnp.sum(xb * xb, axis=-1, keepdims=True)) / (2 * half)
    inv = lax.rsqrt(ms + RMS_EPS)
    o_ref[0, :, 0:half] = xa * inv * fg_ref[:, 0:half]
    o_ref[0, :, half:] = xb * inv * fg_ref[:, half:]


def _final(x1, moe_tok, mods, final_g, tm, b0, bsz, out_prev):
    nb, n, d = x1.shape
    tok = lambda w: pl.BlockSpec((1, tm, w), lambda b, i: (b, i, 0))
    args = [x1, moe_tok, mods, final_g.reshape(1, d)]
    in_specs = [tok(d), tok(d // 2), pl.BlockSpec((1, N_MOD, d), lambda b, i: (b + b0, 0, 0)),
                pl.BlockSpec((1, d), lambda b, i: (0, 0))]
    aliases = {}
    if out_prev is not None:
        args.append(out_prev)
        in_specs.append(pl.BlockSpec(memory_space=pl.ANY))
        aliases = {len(args) - 1: 0}
    return pl.pallas_call(
        _final_kernel,
        grid=(nb, n // tm),
        in_specs=in_specs,
        out_specs=pl.BlockSpec((1, tm, d), lambda b, i: (b + b0, i, 0)),
        out_shape=jax.ShapeDtypeStruct((bsz, n, d), F32),
        input_output_aliases=aliases,
        compiler_params=_params(("parallel", "parallel")),
        name="final",
    )(*args)


def _tile_plan(counts, tmm, n_tiles):
    tiles = (counts + (tmm - 1)) // tmm
    tile_end = jnp.cumsum(tiles)
    offs = (tile_end - tiles) * tmm
    n_valid = tile_end[-1]
    j = jnp.arange(n_tiles, dtype=jnp.int32)
    bucket = jnp.sum((tile_end[None, :] <= jnp.minimum(j, n_valid - 1)[:, None]).astype(jnp.int32), axis=1)
    pair_lo = jnp.array([0, 0, 0, 1, 1, 2], jnp.int32)
    pair_hi = jnp.array([1, 2, 3, 2, 3, 3], jnp.int32)
    group = bucket // PAIRS_PER_GROUP
    pair = bucket % PAIRS_PER_GROUP
    tile_ea = group * EXPERTS_PER_GROUP + pair_lo[pair]
    tile_eb = group * EXPERTS_PER_GROUP + pair_hi[pair]
    tile_new = jnp.concatenate([jnp.ones((1,), jnp.int32), (bucket[1:] != bucket[:-1]).astype(jnp.int32)])
    return offs, tile_ea, tile_eb, (j < n_valid).astype(jnp.int32), tile_new


def kernel(x, c, ctx, c_ctx, w_mod, b_mod, norm1_g, norm2_g, w_in, s5_lambda_re, s5_lambda_im, s5_log_dt,
           s5_b_re, s5_b_im, s5_c_re, s5_c_im, s5_d, w_glu, b_glu, conv_w, w_out, router_group_w,
           router_group_b, router_expert_w, router_expert_b, expert_w1, expert_w3, expert_w2, final_g):
    assert w_mod.shape[0] == 1, "single-layer kernel"
    bsz, n_tok, d = x.shape
    n_ctx = ctx.shape[1]
    n_all = bsz * n_tok
    l = 0
    tm = min(TOKEN_TILE, n_tok)

    n_cond = bsz + 1
    pad = (-n_cond) % 8
    cond = jnp.concatenate([c, c_ctx[None, :], jnp.zeros((pad, d), F32)], axis=0)
    m = _mod_rows(cond, w_mod[l], b_mod[l])
    mx = m[:bsz].reshape(bsz, N_MOD, d)
    mc = m[bsz:bsz + 1].reshape(1, N_MOD, d)

    w_in_b = w_in[l].astype(BF16)
    w_s5 = w_in_b[:, :S5_WIDTH]
    o_c = S5_WIDTH + CONV_WIDTH
    o_v = S5_WIDTH + 2 * CONV_WIDTH
    w_conv = jnp.concatenate(
        [w_in_b[:, S5_WIDTH:o_c], w_in_b[:, o_c:o_c + CONV_ROW_WIDTH], w_in_b[:, o_v:o_v + CONV_ROW_WIDTH],
         w_in_b[:, o_c + CONV_ROW_WIDTH:o_v], w_in_b[:, o_v + CONV_ROW_WIDTH:]], axis=1)
    u = _inproj(x, mx, True, norm1_g[l], w_s5, tm, "inproj")
    uc = _inproj(ctx, mc, False, norm1_g[l], w_s5, min(TOKEN_TILE, n_ctx), "inproj_ctx")

    t_mat, mb_pair, mc_pair, a_rows = _s5_matrices(
        s5_lambda_re[l], s5_lambda_im[l], s5_log_dt[l], s5_b_re[l], s5_b_im[l], s5_c_re[l], s5_c_im[l], s5_d[l])
    y_c = _s5_scan(_chunkify(u, "chunkify"), _chunkify(uc, "chunkify_ctx"), t_mat, mb_pair, mc_pair, a_rows)
    y_s5 = _unchunkify(y_c, u.shape[1])

    n_logits = N_GROUPS + N_EXPERTS
    w_router = jnp.concatenate(
        [router_group_w[l], router_expert_w[l], jnp.zeros((d, ROUTER_ROWS - n_logits), F32)], axis=1).T
    w_router_hi = w_router.astype(BF16)
    w_router_lo = (w_router - w_router_hi.astype(F32)).astype(BF16)
    w_router_t = jnp.concatenate([w_router_hi, w_router_lo], axis=0)
    b_router = jnp.concatenate([router_group_b[l], router_expert_b[l], jnp.zeros((ROUTER_ROWS - n_logits,), F32)])
    tm_mix = min(MIX_TILE, n_tok)
    b_router_t = jnp.broadcast_to(b_router[:, None], (ROUTER_ROWS, min(MIX_SUB, tm_mix)))

    n_parts = MOE_PARTS if bsz % MOE_PARTS == 0 else 1
    nb = bsz // n_parts
    n_part = nb * n_tok
    n_buckets = N_GROUPS * PAIRS_PER_GROUP
    n_rows = n_part + n_buckets * MOE_TILE
    w_glu_b = w_glu[l].astype(BF16)
    w_out_b = w_out[l].astype(BF16)
    staged = []
    for p in range(n_parts):
        x1, h2p, route_t, counts = _mix(x, y_s5, mx, norm1_g[l], norm2_g[l], conv_w[l], w_conv, w_glu_b, b_glu[l],
                                        w_out_b, w_router_t, b_router_t, tm_mix, p * nb, nb)
        offs, *tiles = _tile_plan(counts[:n_buckets, 0].astype(jnp.int32), MOE_TILE, n_rows // MOE_TILE)
        offs_rows = jnp.zeros((BUCKET_ROWS,), F32).at[:n_buckets].set(offs.astype(F32))
        dest = _rank(route_t, jnp.broadcast_to(offs_rows[:, None], (BUCKET_ROWS, 128)), min(RANK_TILE, n_part))
        rows = _scatter_rows(h2p.reshape(n_part, d // 2 + ROW_EXTRA), dest, n_rows)
        staged.append((x1, rows, dest, tiles))
    out = None
    for p, (x1, rows, dest, tiles) in enumerate(staged):
        y_rows = _moe_grouped(rows, *tiles, expert_w1[l], expert_w3[l], expert_w2[l], MOE_TILE)
        moe_tok = _gather_rows(y_rows, dest).reshape(nb, n_tok, d // 2)
        out = _final(x1, moe_tok, mx, final_g, tm, p * nb, bsz, out)
    return out
```

```python
import functools

import jax
import jax.numpy as jnp
from jax import lax
from jax.experimental import pallas as pl
from jax.experimental.pallas import tpu as pltpu
from jax.experimental.pallas import tpu_sc as plsc

F32 = jnp.float32
BF16 = jnp.bfloat16
U32 = jnp.uint32

RMS_EPS = 1e-6
N_MOD = 6
GRID_W = 64
S5_WIDTH = 256
S5_H = 16
S5_P = 64
S5_GROUPS = S5_WIDTH // S5_H
S5_PAIRS = S5_GROUPS // 2
S5_CHUNK = 16
LANES = 128
STATE_PITCH = 136
CONV_WIDTH = 768
CONV_ROW_WIDTH = CONV_WIDTH // 2
N_GROUPS = 4
EXPERTS_PER_GROUP = 4
N_EXPERTS = N_GROUPS * EXPERTS_PER_GROUP
PAIRS_PER_GROUP = 6
ROUTER_ROWS = 32
BUCKET_ROWS = 32
ROW_EXTRA = 128
TOKEN_TILE = 1024
MOE_TILE = 512
RANK_TILE = 2048
INPROJ_TILE = 2048
MIX_TILE = 1024
MIX_SUB = 512
MOE_PARTS = 2
SC_ROWS = 32
SC_INDEX_TILE = 128
VMEM_LIMIT = 52 * 1024 * 1024


def _params(sem, vmem=VMEM_LIMIT):
    return pltpu.CompilerParams(dimension_semantics=sem, vmem_limit_bytes=vmem)


def _mod_kernel(c_ref, w_ref, b_ref, o_ref):
    c = c_ref[...]
    o_ref[...] = jnp.dot(c * jax.nn.sigmoid(c), w_ref[...], preferred_element_type=F32) + b_ref[...]


def _mod_rows(cond, w_mod, b_mod):
    n, d = cond.shape
    nout = w_mod.shape[1]
    bn = d
    return pl.pallas_call(
        _mod_kernel,
        grid=(nout // bn,),
        in_specs=[pl.BlockSpec((n, d), lambda j: (0, 0)),
                  pl.BlockSpec((d, bn), lambda j: (0, j)),
                  pl.BlockSpec((1, bn), lambda j: (0, j))],
        out_specs=pl.BlockSpec((n, bn), lambda j: (0, j)),
        out_shape=jax.ShapeDtypeStruct((n, nout), F32),
        compiler_params=_params(("arbitrary",)),
        name="mod",
    )(cond, w_mod, b_mod.reshape(1, nout))


def _modulated_norm(x, g, shift, scale):
    ms = jnp.mean(x * x, axis=-1, keepdims=True)
    return (x * lax.rsqrt(ms + RMS_EPS) * g) * (1.0 + scale) + shift


def _inproj_kernel(x_ref, mod_ref, g_ref, w_ref, u_ref):
    h = _modulated_norm(x_ref[0], g_ref[...], mod_ref[0, 0:1, :], mod_ref[0, 1:2, :]).astype(BF16)
    u = jnp.dot(h, w_ref[...], preferred_element_type=F32)
    for j in range(S5_WIDTH // LANES):
        u_ref[0, j] = u[:, j * LANES:(j + 1) * LANES]


def _inproj(x, mods, per_batch_mod, norm_g, w_s5, tm, name):
    bsz, n, d = x.shape
    mod_map = (lambda b, i: (b, 0, 0)) if per_batch_mod else (lambda b, i: (0, 0, 0))
    n_slab = S5_WIDTH // LANES
    return pl.pallas_call(
        _inproj_kernel,
        grid=(bsz, n // tm),
        in_specs=[pl.BlockSpec((1, tm, d), lambda b, i: (b, i, 0)),
                  pl.BlockSpec((1, N_MOD, d), mod_map),
                  pl.BlockSpec((1, d), lambda b, i: (0, 0)),
                  pl.BlockSpec((d, S5_WIDTH), lambda b, i: (0, 0))],
        out_specs=pl.BlockSpec((1, n_slab, tm, LANES), lambda b, i: (b, 0, i, 0)),
        out_shape=jax.ShapeDtypeStruct((bsz, n_slab, n, LANES), F32),
        compiler_params=_params(("parallel", "parallel")),
        name=name,
    )(x, mods, norm_g.reshape(1, d), w_s5)


def _toeplitz_kernel(strip_ref, t_ref):
    lc = t_ref.shape[1] // S5_H
    for s in range(lc):
        off = (lc - 1 - s) * S5_H
        t_ref[0, s * S5_H:(s + 1) * S5_H, :] = strip_ref[0, :, off:off + lc * S5_H].astype(BF16)


def _toeplitz(strip):
    g_n, h_n, w = strip.shape
    n = S5_CHUNK * h_n
    return pl.pallas_call(
        _toeplitz_kernel,
        grid=(g_n,),
        in_specs=[pl.BlockSpec((1, h_n, w), lambda g: (g, 0, 0))],
        out_specs=pl.BlockSpec((1, n, n), lambda g: (g, 0, 0)),
        out_shape=jax.ShapeDtypeStruct((g_n, n, n), BF16),
        compiler_params=_params(("parallel",)),
        name="toeplitz",
    )(strip)


def _s5_matrices(lam_re, lam_im, log_dt, b_re, b_im, c_re, c_im, d_skip):
    lc, g_n, p_n, h_n = S5_CHUNK, S5_GROUPS, S5_P, S5_H
    lam = lax.complex(lam_re.astype(F32), lam_im.astype(F32))
    dt = jnp.exp(log_dt.astype(F32))[..., None]
    a_bar = jnp.exp(lam * dt)
    b_bar = ((a_bar - 1.0) / lam)[..., None] * lax.complex(b_re.astype(F32), b_im.astype(F32))
    cm = lax.complex(c_re.astype(F32), c_im.astype(F32))
    steps = jnp.arange(lc + 1, dtype=F32)
    apow = jnp.exp((lam * dt)[:, :, None, :] * steps[None, None, :, None])
    kern = jnp.einsum('dgop,dgjp,dgpi->dgjio', cm, apow[:, :, :lc], b_bar).real
    skip = jnp.eye(h_n, dtype=F32) * d_skip.astype(F32).reshape(g_n, 1, h_n)
    centre = kern[0, :, 0] + kern[1, :, 0] + skip
    lags = jnp.concatenate([kern[1, :, :0:-1], centre[:, None], kern[0, :, 1:]], axis=1)
    strip = lags.transpose(0, 2, 1, 3).reshape(g_n, h_n, (2 * lc - 1) * h_n)
    strip = jnp.pad(strip, ((0, 0), (0, 0), (0, h_n)))
    t_mat = _toeplitz(strip)

    def in_mat(pw, bb):
        return (pw[:, :, None, :] * bb.transpose(0, 2, 1)[:, None, :, :]).reshape(g_n, lc * h_n, p_n)

    mb_f = in_mat(apow[0, :, lc - 1::-1][:, :lc], b_bar[0])
    mb_b = in_mat(apow[1, :, :lc], b_bar[1])

    def out_mat(pw, cc):
        return (pw.transpose(0, 2, 1)[:, :, :, None] * cc.transpose(0, 2, 1)[:, :, None, :]).reshape(
            g_n, p_n, lc * h_n)

    mc_f = out_mat(apow[0, :, 1:lc + 1], cm[0])
    mc_b = out_mat(apow[1, :, lc:0:-1], cm[1])
    a_chunk = apow[:, :, lc]

    q_n = S5_PAIRS
    zeros_in = jnp.zeros((g_n, lc * h_n, p_n), F32)

    def pair_cols(m):
        m = m.reshape(q_n, 2, lc * h_n, p_n)
        z = zeros_in.reshape(q_n, 2, lc * h_n, p_n)[:, 0]
        top = jnp.concatenate([m[:, 0], z], axis=-1)
        bot = jnp.concatenate([z, m[:, 1]], axis=-1)
        return jnp.concatenate([top, bot], axis=1)

    mb_pair = jnp.concatenate([pair_cols(mb_f.real), pair_cols(mb_f.imag),
                               pair_cols(mb_b.real), pair_cols(mb_b.imag)], axis=-1)

    def pair_rows(m):
        m = m.reshape(q_n, 2, p_n, lc * h_n)
        z = jnp.zeros_like(m[:, 0])
        top = jnp.concatenate([m[:, 0], z], axis=-1)
        bot = jnp.concatenate([z, m[:, 1]], axis=-1)
        return jnp.concatenate([top, bot], axis=1)

    mc_pair = jnp.concatenate([pair_rows(mc_f.real), pair_rows(-mc_f.imag),
                               pair_rows(mc_b.real), pair_rows(-mc_b.imag)], axis=1)
    a_rows = jnp.stack([a_chunk[0].real, a_chunk[0].imag, a_chunk[1].real, a_chunk[1].imag], axis=0)
    a_rows = a_rows.reshape(4, q_n, 2 * p_n).transpose(1, 0, 2)
    a_rows = jnp.concatenate([a_rows, jnp.zeros_like(a_rows)], axis=1)
    return t_mat, mb_pair.astype(BF16), mc_pair.astype(BF16), a_rows


def _chunkify_kernel(u_ref, o_ref):
    _, nb, nc, _ = o_ref.shape
    half = S5_CHUNK * S5_H
    per_slab = LANES // S5_H
    for j in range(u_ref.shape[1]):
        cols = []
        for t in range(S5_CHUNK):
            rows = [u_ref[b, j, pl.ds(t, nc, stride=S5_CHUNK), :] for b in range(nb)]
            cols.append((rows[0] if nb == 1 else jnp.concatenate(rows, axis=0)).T)
        for gl in range(per_slab):
            g = j * per_slab + gl
            m = jnp.concatenate([c[gl * S5_H:(gl + 1) * S5_H, :] for c in cols], axis=0)
            o_ref[g // 2, :, :, (g % 2) * half:(g % 2 + 1) * half] = m.T.astype(BF16).reshape(nb, nc, half)


def _chunkify(u_slab, name):
    bsz, n_slab, n, _ = u_slab.shape
    nc = n // S5_CHUNK
    nb = min(bsz, max(1, LANES // nc))
    w = 2 * S5_CHUNK * S5_H
    return pl.pallas_call(
        _chunkify_kernel,
        grid=(bsz // nb,),
        in_specs=[pl.BlockSpec((nb, n_slab, n, LANES), lambda b: (b, 0, 0, 0))],
        out_specs=pl.BlockSpec((S5_PAIRS, nb, nc, w), lambda b: (0, b, 0, 0)),
        out_shape=jax.ShapeDtypeStruct((S5_PAIRS, bsz, nc, w), BF16),
        compiler_params=_params(("parallel",)),
        name=name,
    )(u_slab)


def _unchunkify_kernel(y_ref, o_ref):
    nc = y_ref.shape[2]
    half = S5_CHUNK * S5_H
    per_slab = LANES // S5_H
    for j in range(o_ref.shape[1]):
        rows = []
        for gl in range(per_slab):
            g = j * per_slab + gl
            rows.append(y_ref[g // 2, 0, :, (g % 2) * half:(g % 2 + 1) * half].astype(F32).T)
        for t in range(S5_CHUNK):
            bt = jnp.concatenate([r[t * S5_H:(t + 1) * S5_H, :] for r in rows], axis=0)
            o_ref[0, j, pl.ds(t, nc, stride=S5_CHUNK), :] = bt.T


def _unchunkify(y_c, n_slab):
    q_n, bsz, nc, w = y_c.shape
    n = nc * S5_CHUNK
    return pl.pallas_call(
        _unchunkify_kernel,
        grid=(bsz,),
        in_specs=[pl.BlockSpec((q_n, 1, nc, w), lambda b: (0, b, 0, 0))],
        out_specs=pl.BlockSpec((1, n_slab, n, LANES), lambda b: (b, 0, 0, 0)),
        out_shape=jax.ShapeDtypeStruct((bsz, n_slab, n, LANES), F32),
        compiler_params=_params(("parallel",)),
        name="unchunkify",
    )(y_c)


def _s5_kernel(u_ref, uc_ref, t_ref, mb_ref, mc_ref, a_ref, y_ref, s_lat, s_ctx, h_scr, *, bb):
    _, bsz, n_lat, w = u_ref.shape
    n_ctx = uc_ref.shape[2]
    n_blk = w // LANES
    rb = bb * n_lat

    mb = mb_ref[0]

    def in_lat(i, carry):
        s = jnp.dot(u_ref[0, pl.ds(i * bb, bb)].reshape(rb, w), mb, preferred_element_type=F32)
        for k in range(bb):
            r = pl.multiple_of((i * bb + k) * STATE_PITCH, 8)
            for blk in range(n_blk):
                s_lat[blk, pl.ds(r, n_lat), :] = s[k * n_lat:(k + 1) * n_lat, blk * LANES:(blk + 1) * LANES]
        return carry

    lax.fori_loop(0, bsz // bb, in_lat, 0)
    sc = jnp.dot(uc_ref[0].reshape(bsz * n_ctx, w), mb, preferred_element_type=F32)
    for blk in range(n_blk):
        s_ctx[blk] = sc[:, blk * LANES:(blk + 1) * LANES]

    a_fr, a_fi, a_br, a_bi = (a_ref[0, k:k + 1, :] for k in range(4))

    def step(h, a_r, a_i, s_r, s_i):
        h_r, h_i = h
        return a_r * h_r - a_i * h_i + s_r, a_r * h_i + a_i * h_r + s_i

    def ctx_rows(blk, c):
        return s_ctx[blk, pl.ds(c, bsz, stride=n_ctx), :]

    def lat_rows(ref, blk, c):
        return ref.at[blk, pl.ds(c, bsz, stride=STATE_PITCH), :]

    def ctx_step(k, carry):
        hf, hb = carry
        kb = n_ctx - 1 - k
        hf = step(hf, a_fr, a_fi, ctx_rows(0, k), ctx_rows(1, k))
        hb = step(hb, a_br, a_bi, ctx_rows(2, kb), ctx_rows(3, kb))
        return hf, hb

    zero = jnp.zeros((bsz, LANES), F32)
    carry = lax.fori_loop(0, n_ctx, ctx_step, ((zero, zero), (zero, zero)))

    def lat_step(k, carry):
        hf, hb = carry
        kb = n_lat - 1 - k
        lat_rows(h_scr, 0, k)[...] = hf[0]
        lat_rows(h_scr, 1, k)[...] = hf[1]
        lat_rows(h_scr, 2, kb)[...] = hb[0]
        lat_rows(h_scr, 3, kb)[...] = hb[1]
        hf = step(hf, a_fr, a_fi, lat_rows(s_lat, 0, k)[...], lat_rows(s_lat, 1, k)[...])
        hb = step(hb, a_br, a_bi, lat_rows(s_lat, 2, kb)[...], lat_rows(s_lat, 3, kb)[...])
        return hf, hb

    lax.fori_loop(0, n_lat, lat_step, carry)

    t0 = t_ref[0]
    t1 = t_ref[1]
    mc = mc_ref[0]
    half = S5_CHUNK * S5_H

    def out_lat(i, carry):
        u = u_ref[0, pl.ds(i * bb, bb)].reshape(rb, w)
        h_rows = []
        for k in range(bb):
            r = pl.multiple_of((i * bb + k) * STATE_PITCH, 8)
            h_rows.append(jnp.concatenate([h_scr[blk, pl.ds(r, n_lat), :] for blk in range(n_blk)], axis=1))
        h = jnp.concatenate(h_rows, axis=0).astype(BF16)
        inter = jnp.dot(h, mc, preferred_element_type=F32)
        y0 = jnp.dot(u[:, :half], t0, preferred_element_type=F32) + inter[:, :half]
        y1 = jnp.dot(u[:, half:], t1, preferred_element_type=F32) + inter[:, half:]
        y = jnp.concatenate([y0, y1], axis=1).astype(BF16)
        y_ref[0, pl.ds(i * bb, bb)] = y.reshape(bb, n_lat, w)
        return carry

    lax.fori_loop(0, bsz // bb, out_lat, 0)


def _s5_scan(u_c, uc_c, t_mat, mb_pair, mc_pair, a_rows):
    q_n, bsz, n_lat, w = u_c.shape
    n_ctx = uc_c.shape[2]
    assert n_lat + 8 == STATE_PITCH
    bb = min(4, bsz)
    n_blk = w // LANES
    return pl.pallas_call(
        functools.partial(_s5_kernel, bb=bb),
        grid=(q_n,),
        in_specs=[pl.BlockSpec((1, bsz, n_lat, w), lambda q: (q, 0, 0, 0)),
                  pl.BlockSpec((1, bsz, n_ctx, w), lambda q: (q, 0, 0, 0)),
                  pl.BlockSpec((2, w // 2, w // 2), lambda q: (q, 0, 0)),
                  pl.BlockSpec((1, w, w), lambda q: (q, 0, 0)),
                  pl.BlockSpec((1, w, w), lambda q: (q, 0, 0)),
                  pl.BlockSpec((1, 8, LANES), lambda q: (q, 0, 0))],
        out_specs=pl.BlockSpec((1, bsz, n_lat, w), lambda q: (q, 0, 0, 0)),
        out_shape=jax.ShapeDtypeStruct((q_n, bsz, n_lat, w), BF16),
        scratch_shapes=[pltpu.VMEM((n_blk, bsz * STATE_PITCH, LANES), F32),
                        pltpu.VMEM((n_blk, bsz * n_ctx, LANES), F32),
                        pltpu.VMEM((n_blk, bsz * STATE_PITCH, LANES), F32)],
        compiler_params=_params(("parallel",)),
        name="s5_scan",
    )(u_c, uc_c, t_mat, mb_pair, mc_pair, a_rows)


def _pack_bf16_pair(a, b):
    ua = lax.bitcast_convert_type(a.astype(BF16).astype(F32), U32)
    ub = lax.bitcast_convert_type(b.astype(BF16).astype(F32), U32)
    return ua | (ub >> 16)


def _unpack_bf16_pair(w):
    a = lax.bitcast_convert_type(w & jnp.uint32(0xFFFF0000), F32)
    b = lax.bitcast_convert_type(w << 16, F32)
    return a, b


def _first_max(rows):
    best = rows[0]
    for r in rows[1:]:
        best = jnp.maximum(best, r)
    idx = jnp.full(best.shape, float(len(rows) - 1), F32)
    for k in range(len(rows) - 2, -1, -1):
        idx = jnp.where(rows[k] == best, float(k), idx)
    return best, idx


def _route_rows(lg):
    g_rows = [lg[k:k + 1] for k in range(N_GROUPS)]
    g_max, g_idx = _first_max(g_rows)
    g_sum = sum(jnp.exp(r - g_max) for r in g_rows)
    g_p = 1.0 / g_sum
    e_rows = []
    for j in range(EXPERTS_PER_GROUP):
        r = lg[N_GROUPS + (N_GROUPS - 1) * EXPERTS_PER_GROUP + j:][:1]
        for g in range(N_GROUPS - 2, -1, -1):
            k = N_GROUPS + g * EXPERTS_PER_GROUP + j
            r = jnp.where(g_idx == float(g), lg[k:k + 1], r)
        e_rows.append(r)
    v1, i1 = _first_max(e_rows)
    rest = [jnp.where(i1 == float(j), -jnp.inf, e_rows[j]) for j in range(EXPERTS_PER_GROUP)]
    v2, i2 = _first_max(rest)
    e21 = jnp.exp(v2 - v1)
    w1 = g_p / (1.0 + e21)
    w2 = w1 * e21
    lo = jnp.minimum(i1, i2)
    hi = jnp.maximum(i1, i2)
    base = jnp.where(lo == 0.0, 0.0, jnp.where(lo == 1.0, 3.0, 5.0))
    bucket = g_idx * float(PAIRS_PER_GROUP) + base + hi - lo - 1.0
    first_is_lo = i1 < i2
    return bucket, jnp.where(first_is_lo, w1, w2), jnp.where(first_is_lo, w2, w1)


def _mix_kernel(x_ref, xup_ref, xdn_ref, y_ref, mod_ref, g1_ref, g2_ref, cw_ref, win_ref, wglu_ref, bglu_ref,
                wout_ref, wr_ref, br_ref, x1_ref, h2p_ref, route_ref, counts_ref):
    i = pl.program_id(1)
    tm = x_ref.shape[1]
    d = x_ref.shape[2]
    sub = min(MIX_SUB, tm)
    n_sub = tm // sub
    cw, rw = CONV_WIDTH, CONV_ROW_WIDTH

    def hidden(xv):
        return _modulated_norm(xv, g1_ref[...], mod_ref[0, 0:1, :], mod_ref[0, 1:2, :]).astype(BF16)

    def halo(xv):
        zh = jnp.dot(hidden(xv), win_ref[:, 2 * cw:3 * cw], preferred_element_type=F32)
        return zh[:, 0:rw] * zh[:, rw:cw]

    @pl.when(jnp.logical_and(pl.program_id(0) == 0, i == 0))
    def _():
        counts_ref[...] = jnp.zeros_like(counts_ref)

    for s in range(n_sub):
        r0 = s * sub
        xv = x_ref[0, r0:r0 + sub, :]
        g = jax.nn.gelu(jnp.concatenate([y_ref[0, j, r0:r0 + sub, :] for j in range(y_ref.shape[1])], axis=1))
        glu = g * jax.nn.sigmoid(jnp.dot(g.astype(BF16), wglu_ref[...], preferred_element_type=F32) + bglu_ref[...])
        hx = hidden(xv)
        z_r = jnp.dot(hx, win_ref[:, cw:2 * cw], preferred_element_type=F32)
        cr = z_r[:, 0:rw] * z_r[:, rw:cw]
        z_c = jnp.dot(hx, win_ref[:, 2 * cw:3 * cw], preferred_element_type=F32)
        cc = z_c[:, 0:rw] * z_c[:, rw:cw]
        bg = jnp.dot(hx, win_ref[:, 0:cw], preferred_element_type=F32)
        row = lax.broadcasted_iota(jnp.int32, (sub, 1), 0)
        col_in_row = row % GRID_W
        left = jnp.where(col_in_row == 0, 0.0, pltpu.roll(cr, 1, axis=0))
        right = jnp.where(col_in_row == GRID_W - 1, 0.0, pltpu.roll(cr, sub - 1, axis=0))
        w_r = cw_ref[:, :rw]
        row_part = left * w_r[0:1] + cr * w_r[1:2] + right * w_r[2:3]
        if s == 0:
            up_halo = jnp.where(i == 0, 0.0, halo(xup_ref[0]))
        else:
            up_halo = halo(x_ref[0, r0 - GRID_W:r0, :])
        if s == n_sub - 1:
            dn_halo = jnp.where(i == pl.num_programs(1) - 1, 0.0, halo(xdn_ref[0]))
        else:
            dn_halo = halo(x_ref[0, r0 + sub:r0 + sub + GRID_W, :])
        up = jnp.concatenate([up_halo, cc[:sub - GRID_W]], axis=0)
        dn = jnp.concatenate([cc[GRID_W:], dn_halo], axis=0)
        w_c = cw_ref[:, rw:]
        col_part = up * w_c[0:1] + cc * w_c[1:2] + dn * w_c[2:3]
        y_row = (bg[:, 0:rw] * row_part).astype(BF16)
        y_col = (bg[:, rw:cw] * col_part).astype(BF16)
        mixed = jnp.concatenate([glu.astype(BF16), y_row, y_col], axis=1)
        yx = jnp.dot(mixed, wout_ref[...], preferred_element_type=F32)
        x1 = xv + mod_ref[0, 2:3, :] * yx
        x1_ref[0, r0:r0 + sub, :] = x1
        h2 = _modulated_norm(x1, g2_ref[...], mod_ref[0, 3:4, :], mod_ref[0, 4:5, :])
        h2b = h2.astype(BF16)
        lg2 = lax.dot_general(wr_ref[...], h2b, (((1,), (1,)), ((), ())), preferred_element_type=F32)
        lg = lg2[:ROUTER_ROWS] + lg2[ROUTER_ROWS:] + br_ref[...]
        bucket, w_a, w_b = _route_rows(lg)
        r8 = lax.broadcasted_iota(jnp.int32, (8, sub), 0)
        route_ref[:, r0:r0 + sub] = jnp.where(r8 == 0, bucket, jnp.where(r8 == 1, w_a, jnp.where(r8 == 2, w_b, 0.0)))
        rl = lax.broadcasted_iota(jnp.int32, (ROW_EXTRA, sub), 0)
        gates_t = jnp.where(rl == 0, w_a, jnp.where(rl == 1, w_b, 0.0))
        h2p_ref[0, r0:r0 + sub, 0:d // 2] = _pack_bf16_pair(h2[:, :d // 2], h2[:, d // 2:])
        h2p_ref[0, r0:r0 + sub, d // 2:] = lax.bitcast_convert_type(gates_t.T, U32)
        rb = lax.broadcasted_iota(jnp.int32, (BUCKET_ROWS, sub), 0).astype(F32)
        counts_ref[...] += jnp.sum(jnp.where(rb == bucket, 1.0, 0.0), axis=-1, keepdims=True)


def _mix(x, y_s5, mods, norm1_g, norm2_g, conv_w, w_conv, w_glu, b_glu, w_out, w_router_t, b_router_t, tm, b0, nb):
    _, n, d = x.shape
    nt = n // tm
    halo_blocks = n // GRID_W
    per_tile = tm // GRID_W
    tok_out = lambda w: pl.BlockSpec((1, tm, w), lambda b, i: (b, i, 0))
    full = lambda a: pl.BlockSpec(a.shape, lambda b, i: (0,) * a.ndim)
    args = (x, x, x, y_s5, mods, norm1_g.reshape(1, d), norm2_g.reshape(1, d), conv_w, w_conv, w_glu,
            b_glu.reshape(1, -1), w_out, w_router_t, b_router_t)
    in_specs = [pl.BlockSpec((1, tm, d), lambda b, i: (b + b0, i, 0)),
                pl.BlockSpec((1, GRID_W, d), lambda b, i: (b + b0, jnp.maximum(i * per_tile - 1, 0), 0)),
                pl.BlockSpec((1, GRID_W, d),
                             lambda b, i: (b + b0, jnp.minimum((i + 1) * per_tile, halo_blocks - 1), 0)),
                pl.BlockSpec((1, y_s5.shape[1], tm, LANES), lambda b, i: (b + b0, 0, i, 0)),
                pl.BlockSpec((1, N_MOD, d), lambda b, i: (b + b0, 0, 0))] + [full(a) for a in args[5:]]
    return pl.pallas_call(
        _mix_kernel,
        grid=(nb, nt),
        in_specs=in_specs,
        out_specs=[tok_out(d), tok_out(d // 2 + ROW_EXTRA),
                   pl.BlockSpec((8, tm), lambda b, i: (0, b * nt + i)),
                   pl.BlockSpec((BUCKET_ROWS, 128), lambda b, i: (0, 0))],
        out_shape=[jax.ShapeDtypeStruct((nb, n, d), F32),
                   jax.ShapeDtypeStruct((nb, n, d // 2 + ROW_EXTRA), U32),
                   jax.ShapeDtypeStruct((8, nb * n), F32),
                   jax.ShapeDtypeStruct((BUCKET_ROWS, 128), F32)],
        compiler_params=_params(("arbitrary", "arbitrary")),
        name="mix",
    )(*args)


def _rank_kernel(route_ref, offs_ref, dest_ref, run_ref):
    tr = route_ref.shape[1]

    @pl.when(pl.program_id(0) == 0)
    def _():
        run_ref[...] = jnp.zeros_like(run_ref)

    bucket = route_ref[0:1, :]
    rb = lax.broadcasted_iota(jnp.int32, (BUCKET_ROWS, tr), 0).astype(F32)
    onehot = jnp.where(rb == bucket, 1.0, 0.0)
    s_idx = lax.broadcasted_iota(jnp.int32, (tr, tr), 0)
    t_idx = lax.broadcasted_iota(jnp.int32, (tr, tr), 1)
    tri = jnp.where(s_idx <= t_idx, 1.0, 0.0).astype(BF16)
    prefix = jnp.dot(onehot.astype(BF16), tri, preferred_element_type=F32)
    before = run_ref[:, 0:1] + offs_ref[:, 0:1]
    dest = jnp.sum(onehot * (prefix - 1.0 + before), axis=0, keepdims=True)
    dest_ref[...] = dest.astype(jnp.int32)
    run_ref[...] += prefix[:, tr - 1:tr]


def _rank(route_t, offs_rows, tr):
    n = route_t.shape[1]
    return pl.pallas_call(
        _rank_kernel,
        grid=(n // tr,),
        in_specs=[pl.BlockSpec((8, tr), lambda i: (0, i)),
                  pl.BlockSpec((BUCKET_ROWS, 128), lambda i: (0, 0))],
        out_specs=pl.BlockSpec((1, tr), lambda i: (0, i)),
        out_shape=jax.ShapeDtypeStruct((1, n), jnp.int32),
        scratch_shapes=[pltpu.VMEM((BUCKET_ROWS, 128), F32)],
        compiler_params=_params(("arbitrary",)),
        name="rank",
    )(route_t, offs_rows)


def _sc_mesh():
    return plsc.VectorSubcoreMesh(core_axis_name="core", subcore_axis_name="subcore")


def _scatter_rows(src, dest, n_out):
    n, w = src.shape
    sub = SC_INDEX_TILE // SC_ROWS

    @functools.partial(pl.kernel, out_type=jax.ShapeDtypeStruct((n_out, w), src.dtype), mesh=_sc_mesh(),
                       scratch_types=[], name="scatter_rows")
    def scatter(x_hbm, i_hbm, o_hbm):
        def body(x_vmem, i_vmem):
            j = pl.program_id(1)
            pltpu.sync_copy(x_vmem, o_hbm.at[i_vmem.at[0, pl.ds(j * SC_ROWS, SC_ROWS)]])

        pltpu.emit_pipeline(
            body,
            grid=(n // SC_INDEX_TILE, sub),
            in_specs=[pl.BlockSpec((SC_ROWS, w), lambda i, j: (i * sub + j, 0)),
                      pl.BlockSpec((1, SC_INDEX_TILE), lambda i, j: (0, i))],
            out_specs=[],
            core_axis_name=("core", "subcore"),
            dimension_semantics=(pltpu.PARALLEL, pltpu.ARBITRARY),
        )(x_hbm, i_hbm)

    return scatter(src, dest)


def _gather_rows(src, idx):
    n = idx.shape[1]
    w = src.shape[1]
    sub = SC_INDEX_TILE // SC_ROWS

    @functools.partial(pl.kernel, out_type=jax.ShapeDtypeStruct((n, w), src.dtype), mesh=_sc_mesh(),
                       scratch_types=[], name="gather_rows")
    def gather(x_hbm, i_hbm, o_hbm):
        def body(i_vmem, o_vmem):
            j = pl.program_id(1)
            pltpu.sync_copy(x_hbm.at[i_vmem.at[0, pl.ds(j * SC_ROWS, SC_ROWS)]], o_vmem)

        pltpu.emit_pipeline(
            body,
            grid=(n // SC_INDEX_TILE, sub),
            in_specs=[pl.BlockSpec((1, SC_INDEX_TILE), lambda i, j: (0, i))],
            out_specs=[pl.BlockSpec((SC_ROWS, w), lambda i, j: (i * sub + j, 0))],
            core_axis_name=("core", "subcore"),
            dimension_semantics=(pltpu.PARALLEL, pltpu.ARBITRARY),
        )(i_hbm, o_hbm)

    return gather(src, idx)


def _cast_kernel(*refs):
    n = len(refs) // 2
    for src, dst in zip(refs[:n], refs[n:]):
        dst[...] = src[...].astype(BF16)


def _cast_experts(w1, w3, w2):
    specs = [pl.BlockSpec((1,) + w.shape[1:], lambda e: (e, 0, 0)) for w in (w1, w3, w2)]
    return pl.pallas_call(
        _cast_kernel,
        grid=(w1.shape[0],),
        in_specs=specs,
        out_specs=specs,
        out_shape=[jax.ShapeDtypeStruct(w.shape, BF16) for w in (w1, w3, w2)],
        compiler_params=_params(("parallel",)),
        name="cast_experts",
    )(w1, w3, w2)


def _moe_kernel(ea_ref, eb_ref, valid_ref, rows_ref, w1a_ref, w3a_ref, w2a_ref, w1b_ref, w3b_ref, w2b_ref, o_ref):
    j = pl.program_id(0)
    half = rows_ref.shape[1] - ROW_EXTRA

    @pl.when(valid_ref[j] != 0)
    def _():
        ha, hb = _unpack_bf16_pair(rows_ref[:, 0:half])
        ha = ha.astype(BF16)
        hb = hb.astype(BF16)
        gates = lax.bitcast_convert_type(rows_ref[:, half:], F32)

        def expert(w1_ref, w3_ref, w2_ref, gate):
            def up(w_ref):
                return (jnp.dot(ha, w_ref[0, 0:half, :], preferred_element_type=F32)
                        + jnp.dot(hb, w_ref[0, half:, :], preferred_element_type=F32))

            a1 = up(w1_ref)
            he = (a1 * jax.nn.sigmoid(a1)) * up(w3_ref) * gate
            return jnp.dot(he.astype(BF16), w2_ref[0], preferred_element_type=F32)

        y = (expert(w1a_ref, w3a_ref, w2a_ref, gates[:, 0:1]) + expert(w1b_ref, w3b_ref, w2b_ref, gates[:, 1:2]))
        o_ref[...] = _pack_bf16_pair(y[:, :half], y[:, half:])


def _moe_grouped(rows, tile_ea, tile_eb, tile_valid, w1, w3, w2, tmm):
    r, w = rows.shape
    n_e, de, d = w2.shape
    up_a = pl.BlockSpec((1, d, de), lambda j, ea, eb, va: (ea[j], 0, 0))
    up_b = pl.BlockSpec((1, d, de), lambda j, ea, eb, va: (eb[j], 0, 0))
    grid_spec = pltpu.PrefetchScalarGridSpec(
        num_scalar_prefetch=3,
        grid=(r // tmm,),
        in_specs=[pl.BlockSpec((tmm, w), lambda j, ea, eb, va: (j, 0)),
                  up_a, up_a, pl.BlockSpec((1, de, d), lambda j, ea, eb, va: (ea[j], 0, 0)),
                  up_b, up_b, pl.BlockSpec((1, de, d), lambda j, ea, eb, va: (eb[j], 0, 0))],
        out_specs=pl.BlockSpec((tmm, d // 2), lambda j, ea, eb, va: (j, 0)),
    )
    return pl.pallas_call(
        _moe_kernel,
        grid_spec=grid_spec,
        out_shape=jax.ShapeDtypeStruct((r, d // 2), U32),
        compiler_params=_params(("arbitrary",)),
        name="moe",
    )(tile_ea, tile_eb, tile_valid, rows, w1, w3, w2, w1, w3, w2)


def _final_kernel(x1_ref, moe_ref, mod_ref, fg_ref, *rest):
    o_ref = rest[-1]
    ya, yb = _unpack_bf16_pair(moe_ref[0])
    half = ya.shape[1]
    gate = mod_ref[0, 5:6, :]
    xa = x1_ref[0, :, 0:half] + gate[:, 0:half] * ya
    xb = x1_ref[0, :, half:] + gate[:, half:] * yb
    ms = (jnp.sum(xa * xa, axis=-1, keepdims=True) + jnp.sum(xb * xb, axis=-1, keepdims=True)) / (2 * half)
    inv = lax.rsqrt(ms + RMS_EPS)
    o_ref[0, :, 0:half] = xa * inv * fg_ref[:, 0:half]
    o_ref[0, :, half:] = xb * inv * fg_ref[:, half:]


def _final(x1, moe_tok, mods, final_g, tm, b0, bsz, out_prev):
    nb, n, d = x1.shape
    tok = lambda w: pl.BlockSpec((1, tm, w), lambda b, i: (b, i, 0))
    args = [x1, moe_tok, mods, final_g.reshape(1, d)]
    in_specs = [tok(d), tok(d // 2), pl.BlockSpec((1, N_MOD, d), lambda b, i: (b + b0, 0, 0)),
                pl.BlockSpec((1, d), lambda b, i: (0, 0))]
    aliases = {}
    if out_prev is not None:
        args.append(out_prev)
        in_specs.append(pl.BlockSpec(memory_space=pl.ANY))
        aliases = {len(args) - 1: 0}
    return pl.pallas_call(
        _final_kernel,
        grid=(nb, n // tm),
        in_specs=in_specs,
        out_specs=pl.BlockSpec((1, tm, d), lambda b, i: (b + b0, i, 0)),
        out_shape=jax.ShapeDtypeStruct((bsz, n, d), F32),
        input_output_aliases=aliases,
        compiler_params=_params(("parallel", "parallel")),
        name="final",
    )(*args)


def _tile_plan(counts, tmm, n_tiles):
    tiles = (counts + (tmm - 1)) // tmm
    tile_end = jnp.cumsum(tiles)
    offs = (tile_end - tiles) * tmm
    n_valid = tile_end[-1]
    j = jnp.arange(n_tiles, dtype=jnp.int32)
    bucket = jnp.sum((tile_end[None, :] <= jnp.minimum(j, n_valid - 1)[:, None]).astype(jnp.int32), axis=1)
    pair_lo = jnp.array([0, 0, 0, 1, 1, 2], jnp.int32)
    pair_hi = jnp.array([1, 2, 3, 2, 3, 3], jnp.int32)
    group = bucket // PAIRS_PER_GROUP
    pair = bucket % PAIRS_PER_GROUP
    tile_ea = group * EXPERTS_PER_GROUP + pair_lo[pair]
    tile_eb = group * EXPERTS_PER_GROUP + pair_hi[pair]
    return offs, tile_ea, tile_eb, (j < n_valid).astype(jnp.int32)


def kernel(x, c, ctx, c_ctx, w_mod, b_mod, norm1_g, norm2_g, w_in, s5_lambda_re, s5_lambda_im, s5_log_dt,
           s5_b_re, s5_b_im, s5_c_re, s5_c_im, s5_d, w_glu, b_glu, conv_w, w_out, router_group_w,
           router_group_b, router_expert_w, router_expert_b, expert_w1, expert_w3, expert_w2, final_g):
    assert w_mod.shape[0] == 1, "single-layer kernel"
    bsz, n_tok, d = x.shape
    n_ctx = ctx.shape[1]
    n_all = bsz * n_tok
    l = 0
    tm = min(TOKEN_TILE, n_tok)

    n_cond = bsz + 1
    pad = (-n_cond) % 8
    cond = jnp.concatenate([c, c_ctx[None, :], jnp.zeros((pad, d), F32)], axis=0)
    m = _mod_rows(cond, w_mod[l], b_mod[l])
    mx = m[:bsz].reshape(bsz, N_MOD, d)
    mc = m[bsz:bsz + 1].reshape(1, N_MOD, d)

    w_in_b = w_in[l].astype(BF16)
    w_s5 = w_in_b[:, :S5_WIDTH]
    o_c = S5_WIDTH + CONV_WIDTH
    o_v = S5_WIDTH + 2 * CONV_WIDTH
    w_conv = jnp.concatenate(
        [w_in_b[:, S5_WIDTH:o_c], w_in_b[:, o_c:o_c + CONV_ROW_WIDTH], w_in_b[:, o_v:o_v + CONV_ROW_WIDTH],
         w_in_b[:, o_c + CONV_ROW_WIDTH:o_v], w_in_b[:, o_v + CONV_ROW_WIDTH:]], axis=1)
    w_experts = _cast_experts(expert_w1[l], expert_w3[l], expert_w2[l])
    u = _inproj(x, mx, True, norm1_g[l], w_s5, min(INPROJ_TILE, n_tok), "inproj")
    uc = _inproj(ctx, mc, False, norm1_g[l], w_s5, min(INPROJ_TILE, n_ctx), "inproj_ctx")

    t_mat, mb_pair, mc_pair, a_rows = _s5_matrices(
        s5_lambda_re[l], s5_lambda_im[l], s5_log_dt[l], s5_b_re[l], s5_b_im[l], s5_c_re[l], s5_c_im[l], s5_d[l])
    y_c = _s5_scan(_chunkify(u, "chunkify"), _chunkify(uc, "chunkify_ctx"), t_mat, mb_pair, mc_pair, a_rows)
    y_s5 = _unchunkify(y_c, u.shape[1])

    n_logits = N_GROUPS + N_EXPERTS
    w_router = jnp.concatenate(
        [router_group_w[l], router_expert_w[l], jnp.zeros((d, ROUTER_ROWS - n_logits), F32)], axis=1).T
    w_router_hi = w_router.astype(BF16)
    w_router_lo = (w_router - w_router_hi.astype(F32)).astype(BF16)
    w_router_t = jnp.concatenate([w_router_hi, w_router_lo], axis=0)
    b_router = jnp.concatenate([router_group_b[l], router_expert_b[l], jnp.zeros((ROUTER_ROWS - n_logits,), F32)])
    tm_mix = min(MIX_TILE, n_tok)
    b_router_t = jnp.broadcast_to(b_router[:, None], (ROUTER_ROWS, min(MIX_SUB, tm_mix)))

    n_parts = MOE_PARTS if bsz % MOE_PARTS == 0 else 1
    nb = bsz // n_parts
    n_part = nb * n_tok
    n_buckets = N_GROUPS * PAIRS_PER_GROUP
    n_rows = n_part + n_buckets * MOE_TILE
    w_glu_b = w_glu[l].astype(BF16)
    w_out_b = w_out[l].astype(BF16)
    staged = []
    for p in range(n_parts):
        x1, h2p, route_t, counts = _mix(x, y_s5, mx, norm1_g[l], norm2_g[l], conv_w[l], w_conv, w_glu_b, b_glu[l],
                                        w_out_b, w_router_t, b_router_t, tm_mix, p * nb, nb)
        offs, *tiles = _tile_plan(counts[:n_buckets, 0].astype(jnp.int32), MOE_TILE, n_rows // MOE_TILE)
        offs_rows = jnp.zeros((BUCKET_ROWS,), F32).at[:n_buckets].set(offs.astype(F32))
        dest = _rank(route_t, jnp.broadcast_to(offs_rows[:, None], (BUCKET_ROWS, 128)), min(RANK_TILE, n_part))
        rows = _scatter_rows(h2p.reshape(n_part, d // 2 + ROW_EXTRA), dest, n_rows)
        staged.append((x1, rows, dest, tiles))
    out = None
    for p, (x1, rows, dest, tiles) in enumerate(staged):
        y_rows = _moe_grouped(rows, *tiles, *w_experts, MOE_TILE)
        moe_tok = _gather_rows(y_rows, dest).reshape(nb, n_tok, d // 2)
        out = _final(x1, moe_tok, mx, final_g, tm, p * nb, bsz, out)
    return out
```

```python
import functools

import jax
import jax.numpy as jnp
from jax import lax
from jax.experimental import pallas as pl
from jax.experimental.pallas import tpu as pltpu
from jax.experimental.pallas import tpu_sc as plsc

F32 = jnp.float32
BF16 = jnp.bfloat16
U32 = jnp.uint32

RMS_EPS = 1e-6
N_MOD = 6
GRID_W = 64
S5_WIDTH = 256
S5_H = 16
S5_P = 64
S5_GROUPS = S5_WIDTH // S5_H
S5_PAIRS = S5_GROUPS // 2
S5_CHUNK = 16
LANES = 128
STATE_PITCH = 136
CONV_WIDTH = 768
CONV_ROW_WIDTH = CONV_WIDTH // 2
N_GROUPS = 4
EXPERTS_PER_GROUP = 4
N_EXPERTS = N_GROUPS * EXPERTS_PER_GROUP
PAIRS_PER_GROUP = 6
ROUTER_ROWS = 32
BUCKET_ROWS = 32
ROW_EXTRA = 128
TOKEN_TILE = 1024
MOE_TILE = 512
RANK_TILE = 2048
INPROJ_TILE = 2048
MIX_TILE = 1024
MIX_SUB = 512
MOE_PARTS = 2
CAST_EXPERTS = 2
LAYOUT_GROUPS = 2
SC_ROWS = 32
SC_INDEX_TILE = 128
VMEM_LIMIT = 52 * 1024 * 1024


def _params(sem, vmem=VMEM_LIMIT):
    return pltpu.CompilerParams(dimension_semantics=sem, vmem_limit_bytes=vmem)


def _mod_kernel(c_ref, w_ref, b_ref, o_ref):
    c = c_ref[...]
    o_ref[...] = jnp.dot(c * jax.nn.sigmoid(c), w_ref[...], preferred_element_type=F32) + b_ref[...]


def _mod_rows(cond, w_mod, b_mod):
    n, d = cond.shape
    nout = w_mod.shape[1]
    bn = d
    return pl.pallas_call(
        _mod_kernel,
        grid=(nout // bn,),
        in_specs=[pl.BlockSpec((n, d), lambda j: (0, 0)),
                  pl.BlockSpec((d, bn), lambda j: (0, j)),
                  pl.BlockSpec((1, bn), lambda j: (0, j))],
        out_specs=pl.BlockSpec((n, bn), lambda j: (0, j)),
        out_shape=jax.ShapeDtypeStruct((n, nout), F32),
        compiler_params=_params(("arbitrary",)),
        name="mod",
    )(cond, w_mod, b_mod.reshape(1, nout))


def _modulated_norm(x, g, shift, scale):
    ms = jnp.mean(x * x, axis=-1, keepdims=True)
    return (x * lax.rsqrt(ms + RMS_EPS) * g) * (1.0 + scale) + shift


def _inproj_kernel(x_ref, mod_ref, g_ref, w_ref, u_ref):
    h = _modulated_norm(x_ref[0], g_ref[...], mod_ref[0, 0:1, :], mod_ref[0, 1:2, :]).astype(BF16)
    u = jnp.dot(h, w_ref[...], preferred_element_type=F32)
    for j in range(S5_WIDTH // LANES):
        u_ref[0, j] = u[:, j * LANES:(j + 1) * LANES]


def _inproj(x, mods, per_batch_mod, norm_g, w_s5, tm, name):
    bsz, n, d = x.shape
    mod_map = (lambda b, i: (b, 0, 0)) if per_batch_mod else (lambda b, i: (0, 0, 0))
    n_slab = S5_WIDTH // LANES
    return pl.pallas_call(
        _inproj_kernel,
        grid=(bsz, n // tm),
        in_specs=[pl.BlockSpec((1, tm, d), lambda b, i: (b, i, 0)),
                  pl.BlockSpec((1, N_MOD, d), mod_map),
                  pl.BlockSpec((1, d), lambda b, i: (0, 0)),
                  pl.BlockSpec((d, S5_WIDTH), lambda b, i: (0, 0))],
        out_specs=pl.BlockSpec((1, n_slab, tm, LANES), lambda b, i: (b, 0, i, 0)),
        out_shape=jax.ShapeDtypeStruct((bsz, n_slab, n, LANES), F32),
        compiler_params=_params(("parallel", "parallel")),
        name=name,
    )(x, mods, norm_g.reshape(1, d), w_s5)


def _toeplitz_kernel(strip_ref, t_ref):
    lc = t_ref.shape[1] // S5_H
    for s in range(lc):
        off = (lc - 1 - s) * S5_H
        t_ref[0, s * S5_H:(s + 1) * S5_H, :] = strip_ref[0, :, off:off + lc * S5_H].astype(BF16)


def _toeplitz(strip):
    g_n, h_n, w = strip.shape
    n = S5_CHUNK * h_n
    return pl.pallas_call(
        _toeplitz_kernel,
        grid=(g_n,),
        in_specs=[pl.BlockSpec((1, h_n, w), lambda g: (g, 0, 0))],
        out_specs=pl.BlockSpec((1, n, n), lambda g: (g, 0, 0)),
        out_shape=jax.ShapeDtypeStruct((g_n, n, n), BF16),
        compiler_params=_params(("parallel",)),
        name="toeplitz",
    )(strip)


def _s5_matrices(lam_re, lam_im, log_dt, b_re, b_im, c_re, c_im, d_skip):
    lc, g_n, p_n, h_n = S5_CHUNK, S5_GROUPS, S5_P, S5_H
    lam = lax.complex(lam_re.astype(F32), lam_im.astype(F32))
    dt = jnp.exp(log_dt.astype(F32))[..., None]
    a_bar = jnp.exp(lam * dt)
    b_bar = ((a_bar - 1.0) / lam)[..., None] * lax.complex(b_re.astype(F32), b_im.astype(F32))
    cm = lax.complex(c_re.astype(F32), c_im.astype(F32))
    steps = jnp.arange(lc + 1, dtype=F32)
    apow = jnp.exp((lam * dt)[:, :, None, :] * steps[None, None, :, None])
    kern = jnp.einsum('dgop,dgjp,dgpi->dgjio', cm, apow[:, :, :lc], b_bar).real
    skip = jnp.eye(h_n, dtype=F32) * d_skip.astype(F32).reshape(g_n, 1, h_n)
    centre = kern[0, :, 0] + kern[1, :, 0] + skip
    lags = jnp.concatenate([kern[1, :, :0:-1], centre[:, None], kern[0, :, 1:]], axis=1)
    strip = lags.transpose(0, 2, 1, 3).reshape(g_n, h_n, (2 * lc - 1) * h_n)
    strip = jnp.pad(strip, ((0, 0), (0, 0), (0, h_n)))
    t_mat = _toeplitz(strip)

    def in_mat(pw, bb):
        return (pw[:, :, None, :] * bb.transpose(0, 2, 1)[:, None, :, :]).reshape(g_n, lc * h_n, p_n)

    mb_f = in_mat(apow[0, :, lc - 1::-1][:, :lc], b_bar[0])
    mb_b = in_mat(apow[1, :, :lc], b_bar[1])

    def out_mat(pw, cc):
        return (pw.transpose(0, 2, 1)[:, :, :, None] * cc.transpose(0, 2, 1)[:, :, None, :]).reshape(
            g_n, p_n, lc * h_n)

    mc_f = out_mat(apow[0, :, 1:lc + 1], cm[0])
    mc_b = out_mat(apow[1, :, lc:0:-1], cm[1])
    a_chunk = apow[:, :, lc]

    q_n = S5_PAIRS
    zeros_in = jnp.zeros((g_n, lc * h_n, p_n), F32)

    def pair_cols(m):
        m = m.reshape(q_n, 2, lc * h_n, p_n)
        z = zeros_in.reshape(q_n, 2, lc * h_n, p_n)[:, 0]
        top = jnp.concatenate([m[:, 0], z], axis=-1)
        bot = jnp.concatenate([z, m[:, 1]], axis=-1)
        return jnp.concatenate([top, bot], axis=1)

    mb_pair = jnp.concatenate([pair_cols(mb_f.real), pair_cols(mb_f.imag),
                               pair_cols(mb_b.real), pair_cols(mb_b.imag)], axis=-1)

    def pair_rows(m):
        m = m.reshape(q_n, 2, p_n, lc * h_n)
        z = jnp.zeros_like(m[:, 0])
        top = jnp.concatenate([m[:, 0], z], axis=-1)
        bot = jnp.concatenate([z, m[:, 1]], axis=-1)
        return jnp.concatenate([top, bot], axis=1)

    mc_pair = jnp.concatenate([pair_rows(mc_f.real), pair_rows(-mc_f.imag),
                               pair_rows(mc_b.real), pair_rows(-mc_b.imag)], axis=1)
    a_rows = jnp.stack([a_chunk[0].real, a_chunk[0].imag, a_chunk[1].real, a_chunk[1].imag], axis=0)
    a_rows = a_rows.reshape(4, q_n, 2 * p_n).transpose(1, 0, 2)
    a_rows = jnp.concatenate([a_rows, jnp.zeros_like(a_rows)], axis=1)
    return t_mat, mb_pair.astype(BF16), mc_pair.astype(BF16), a_rows


def _chunkify_kernel(u_ref, o_ref, *, gb):
    _, nb, nc, _ = o_ref.shape
    half = S5_CHUNK * S5_H
    per_slab = LANES // S5_H
    for b0 in range(0, nb, gb):
        for j in range(u_ref.shape[1]):
            cols = []
            for t in range(S5_CHUNK):
                rows = [u_ref[b0 + b, j, pl.ds(t, nc, stride=S5_CHUNK), :] for b in range(gb)]
                cols.append((rows[0] if gb == 1 else jnp.concatenate(rows, axis=0)).T)
            for gl in range(per_slab):
                g = j * per_slab + gl
                m = jnp.concatenate([c[gl * S5_H:(gl + 1) * S5_H, :] for c in cols], axis=0)
                o_ref[g // 2, b0:b0 + gb, :, (g % 2) * half:(g % 2 + 1) * half] = (
                    m.T.astype(BF16).reshape(gb, nc, half))


def _layout_step(bsz, gb):
    return gb * LAYOUT_GROUPS if bsz % (gb * LAYOUT_GROUPS) == 0 else gb


def _chunkify(u_slab, name):
    bsz, n_slab, n, _ = u_slab.shape
    nc = n // S5_CHUNK
    gb = min(bsz, max(1, LANES // nc))
    nb = _layout_step(bsz, gb)
    w = 2 * S5_CHUNK * S5_H
    return pl.pallas_call(
        functools.partial(_chunkify_kernel, gb=gb),
        grid=(bsz // nb,),
        in_specs=[pl.BlockSpec((nb, n_slab, n, LANES), lambda b: (b, 0, 0, 0))],
        out_specs=pl.BlockSpec((S5_PAIRS, nb, nc, w), lambda b: (0, b, 0, 0)),
        out_shape=jax.ShapeDtypeStruct((S5_PAIRS, bsz, nc, w), BF16),
        compiler_params=_params(("parallel",)),
        name=name,
    )(u_slab)


def _unchunkify_kernel(y_ref, o_ref):
    _, nb, nc, _ = y_ref.shape
    half = S5_CHUNK * S5_H
    per_slab = LANES // S5_H
    for b in range(nb):
        for j in range(o_ref.shape[1]):
            rows = []
            for gl in range(per_slab):
                g = j * per_slab + gl
                rows.append(y_ref[g // 2, b, :, (g % 2) * half:(g % 2 + 1) * half].astype(F32).T)
            for t in range(S5_CHUNK):
                bt = jnp.concatenate([r[t * S5_H:(t + 1) * S5_H, :] for r in rows], axis=0)
                o_ref[b, j, pl.ds(t, nc, stride=S5_CHUNK), :] = bt.T


def _unchunkify(y_c, n_slab):
    q_n, bsz, nc, w = y_c.shape
    n = nc * S5_CHUNK
    nb = _layout_step(bsz, 1)
    return pl.pallas_call(
        _unchunkify_kernel,
        grid=(bsz // nb,),
        in_specs=[pl.BlockSpec((q_n, nb, nc, w), lambda b: (0, b, 0, 0))],
        out_specs=pl.BlockSpec((nb, n_slab, n, LANES), lambda b: (b, 0, 0, 0)),
        out_shape=jax.ShapeDtypeStruct((bsz, n_slab, n, LANES), F32),
        compiler_params=_params(("parallel",)),
        name="unchunkify",
    )(y_c)


def _s5_kernel(u_ref, uc_ref, t_ref, mb_ref, mc_ref, a_ref, y_ref, s_lat, s_ctx, h_scr, *, bb):
    _, bsz, n_lat, w = u_ref.shape
    n_ctx = uc_ref.shape[2]
    n_blk = w // LANES
    rb = bb * n_lat

    mb = mb_ref[0]

    def in_lat(i, carry):
        s = jnp.dot(u_ref[0, pl.ds(i * bb, bb)].reshape(rb, w), mb, preferred_element_type=F32)
        for k in range(bb):
            r = pl.multiple_of((i * bb + k) * STATE_PITCH, 8)
            for blk in range(n_blk):
                s_lat[blk, pl.ds(r, n_lat), :] = s[k * n_lat:(k + 1) * n_lat, blk * LANES:(blk + 1) * LANES]
        return carry

    lax.fori_loop(0, bsz // bb, in_lat, 0)
    sc = jnp.dot(uc_ref[0].reshape(bsz * n_ctx, w), mb, preferred_element_type=F32)
    for blk in range(n_blk):
        s_ctx[blk] = sc[:, blk * LANES:(blk + 1) * LANES]

    a_fr, a_fi, a_br, a_bi = (a_ref[0, k:k + 1, :] for k in range(4))

    def step(h, a_r, a_i, s_r, s_i):
        h_r, h_i = h
        return a_r * h_r - a_i * h_i + s_r, a_r * h_i + a_i * h_r + s_i

    def ctx_rows(blk, c):
        return s_ctx[blk, pl.ds(c, bsz, stride=n_ctx), :]

    def lat_rows(ref, blk, c):
        return ref.at[blk, pl.ds(c, bsz, stride=STATE_PITCH), :]

    def ctx_step(k, carry):
        hf, hb = carry
        kb = n_ctx - 1 - k
        hf = step(hf, a_fr, a_fi, ctx_rows(0, k), ctx_rows(1, k))
        hb = step(hb, a_br, a_bi, ctx_rows(2, kb), ctx_rows(3, kb))
        return hf, hb

    zero = jnp.zeros((bsz, LANES), F32)
    carry = lax.fori_loop(0, n_ctx, ctx_step, ((zero, zero), (zero, zero)))

    def lat_step(k, carry):
        hf, hb = carry
        kb = n_lat - 1 - k
        lat_rows(h_scr, 0, k)[...] = hf[0]
        lat_rows(h_scr, 1, k)[...] = hf[1]
        lat_rows(h_scr, 2, kb)[...] = hb[0]
        lat_rows(h_scr, 3, kb)[...] = hb[1]
        hf = step(hf, a_fr, a_fi, lat_rows(s_lat, 0, k)[...], lat_rows(s_lat, 1, k)[...])
        hb = step(hb, a_br, a_bi, lat_rows(s_lat, 2, kb)[...], lat_rows(s_lat, 3, kb)[...])
        return hf, hb

    lax.fori_loop(0, n_lat, lat_step, carry)

    t0 = t_ref[0]
    t1 = t_ref[1]
    mc = mc_ref[0]
    half = S5_CHUNK * S5_H

    def out_lat(i, carry):
        u = u_ref[0, pl.ds(i * bb, bb)].reshape(rb, w)
        h_rows = []
        for k in range(bb):
            r = pl.multiple_of((i * bb + k) * STATE_PITCH, 8)
            h_rows.append(jnp.concatenate([h_scr[blk, pl.ds(r, n_lat), :] for blk in range(n_blk)], axis=1))
        h = jnp.concatenate(h_rows, axis=0).astype(BF16)
        inter = jnp.dot(h, mc, preferred_element_type=F32)
        y0 = jnp.dot(u[:, :half], t0, preferred_element_type=F32) + inter[:, :half]
        y1 = jnp.dot(u[:, half:], t1, preferred_element_type=F32) + inter[:, half:]
        y = jnp.concatenate([y0, y1], axis=1).astype(BF16)
        y_ref[0, pl.ds(i * bb, bb)] = y.reshape(bb, n_lat, w)
        return carry

    lax.fori_loop(0, bsz // bb, out_lat, 0)


def _s5_scan(u_c, uc_c, t_mat, mb_pair, mc_pair, a_rows):
    q_n, bsz, n_lat, w = u_c.shape
    n_ctx = uc_c.shape[2]
    assert n_lat + 8 == STATE_PITCH
    bb = min(4, bsz)
    n_blk = w // LANES
    return pl.pallas_call(
        functools.partial(_s5_kernel, bb=bb),
        grid=(q_n,),
        in_specs=[pl.BlockSpec((1, bsz, n_lat, w), lambda q: (q, 0, 0, 0)),
                  pl.BlockSpec((1, bsz, n_ctx, w), lambda q: (q, 0, 0, 0)),
                  pl.BlockSpec((2, w // 2, w // 2), lambda q: (q, 0, 0)),
                  pl.BlockSpec((1, w, w), lambda q: (q, 0, 0)),
                  pl.BlockSpec((1, w, w), lambda q: (q, 0, 0)),
                  pl.BlockSpec((1, 8, LANES), lambda q: (q, 0, 0))],
        out_specs=pl.BlockSpec((1, bsz, n_lat, w), lambda q: (q, 0, 0, 0)),
        out_shape=jax.ShapeDtypeStruct((q_n, bsz, n_lat, w), BF16),
        scratch_shapes=[pltpu.VMEM((n_blk, bsz * STATE_PITCH, LANES), F32),
                        pltpu.VMEM((n_blk, bsz * n_ctx, LANES), F32),
                        pltpu.VMEM((n_blk, bsz * STATE_PITCH, LANES), F32)],
        compiler_params=_params(("parallel",)),
        name="s5_scan",
    )(u_c, uc_c, t_mat, mb_pair, mc_pair, a_rows)


def _pack_bf16_pair(a, b):
    ua = lax.bitcast_convert_type(a.astype(BF16).astype(F32), U32)
    ub = lax.bitcast_convert_type(b.astype(BF16).astype(F32), U32)
    return ua | (ub >> 16)


def _unpack_bf16_pair(w):
    a = lax.bitcast_convert_type(w & jnp.uint32(0xFFFF0000), F32)
    b = lax.bitcast_convert_type(w << 16, F32)
    return a, b


def _first_max(rows):
    best = rows[0]
    for r in rows[1:]:
        best = jnp.maximum(best, r)
    idx = jnp.full(best.shape, float(len(rows) - 1), F32)
    for k in range(len(rows) - 2, -1, -1):
        idx = jnp.where(rows[k] == best, float(k), idx)
    return best, idx


def _route_rows(lg):
    g_rows = [lg[k:k + 1] for k in range(N_GROUPS)]
    g_max, g_idx = _first_max(g_rows)
    g_sum = sum(jnp.exp(r - g_max) for r in g_rows)
    g_p = 1.0 / g_sum
    e_rows = []
    for j in range(EXPERTS_PER_GROUP):
        r = lg[N_GROUPS + (N_GROUPS - 1) * EXPERTS_PER_GROUP + j:][:1]
        for g in range(N_GROUPS - 2, -1, -1):
            k = N_GROUPS + g * EXPERTS_PER_GROUP + j
            r = jnp.where(g_idx == float(g), lg[k:k + 1], r)
        e_rows.append(r)
    v1, i1 = _first_max(e_rows)
    rest = [jnp.where(i1 == float(j), -jnp.inf, e_rows[j]) for j in range(EXPERTS_PER_GROUP)]
    v2, i2 = _first_max(rest)
    e21 = jnp.exp(v2 - v1)
    w1 = g_p / (1.0 + e21)
    w2 = w1 * e21
    lo = jnp.minimum(i1, i2)
    hi = jnp.maximum(i1, i2)
    base = jnp.where(lo == 0.0, 0.0, jnp.where(lo == 1.0, 3.0, 5.0))
    bucket = g_idx * float(PAIRS_PER_GROUP) + base + hi - lo - 1.0
    first_is_lo = i1 < i2
    return bucket, jnp.where(first_is_lo, w1, w2), jnp.where(first_is_lo, w2, w1)


def _mix_kernel(x_ref, xup_ref, xdn_ref, y_ref, mod_ref, g1_ref, g2_ref, cw_ref, win_ref, wglu_ref, bglu_ref,
                wout_ref, wr_ref, br_ref, x1_ref, h2p_ref, route_ref, counts_ref):
    i = pl.program_id(1)
    tm = x_ref.shape[1]
    d = x_ref.shape[2]
    sub = min(MIX_SUB, tm)
    n_sub = tm // sub
    cw, rw = CONV_WIDTH, CONV_ROW_WIDTH

    def hidden(xv):
        return _modulated_norm(xv, g1_ref[...], mod_ref[0, 0:1, :], mod_ref[0, 1:2, :]).astype(BF16)

    def halo(h):
        zh = jnp.dot(h, win_ref[:, 2 * cw:3 * cw], preferred_element_type=F32)
        return zh[:, 0:rw] * zh[:, rw:cw]

    @pl.when(jnp.logical_and(pl.program_id(0) == 0, i == 0))
    def _():
        counts_ref[...] = jnp.zeros_like(counts_ref)

    def sub_tile(s):
        r0 = s * sub
        xv = x_ref[0, r0:r0 + sub, :]
        hx = hidden(xv)
        g = jax.nn.gelu(jnp.concatenate([y_ref[0, j, r0:r0 + sub, :] for j in range(y_ref.shape[1])], axis=1))
        if s == 0:
            hx_up = hidden(xup_ref[0])
        else:
            hx_up = hidden(x_ref[0, r0 - GRID_W:r0, :])
        if s == n_sub - 1:
            hx_dn = hidden(xdn_ref[0])
        else:
            hx_dn = hidden(x_ref[0, r0 + sub:r0 + sub + GRID_W, :])
        yield
        z_r = jnp.dot(hx, win_ref[:, cw:2 * cw], preferred_element_type=F32)
        cr = z_r[:, 0:rw] * z_r[:, rw:cw]
        z_c = jnp.dot(hx, win_ref[:, 2 * cw:3 * cw], preferred_element_type=F32)
        cc = z_c[:, 0:rw] * z_c[:, rw:cw]
        bg = jnp.dot(hx, win_ref[:, 0:cw], preferred_element_type=F32)
        up_halo = halo(hx_up)
        dn_halo = halo(hx_dn)
        glu = g * jax.nn.sigmoid(jnp.dot(g.astype(BF16), wglu_ref[...], preferred_element_type=F32) + bglu_ref[...])
        yield
        row = lax.broadcasted_iota(jnp.int32, (sub, 1), 0)
        col_in_row = row % GRID_W
        left = jnp.where(col_in_row == 0, 0.0, pltpu.roll(cr, 1, axis=0))
        right = jnp.where(col_in_row == GRID_W - 1, 0.0, pltpu.roll(cr, sub - 1, axis=0))
        w_r = cw_ref[:, :rw]
        row_part = left * w_r[0:1] + cr * w_r[1:2] + right * w_r[2:3]
        if s == 0:
            up_halo = jnp.where(i == 0, 0.0, up_halo)
        if s == n_sub - 1:
            dn_halo = jnp.where(i == pl.num_programs(1) - 1, 0.0, dn_halo)
        up = jnp.concatenate([up_halo, cc[:sub - GRID_W]], axis=0)
        dn = jnp.concatenate([cc[GRID_W:], dn_halo], axis=0)
        w_c = cw_ref[:, rw:]
        col_part = up * w_c[0:1] + cc * w_c[1:2] + dn * w_c[2:3]
        y_row = (bg[:, 0:rw] * row_part).astype(BF16)
        y_col = (bg[:, rw:cw] * col_part).astype(BF16)
        mixed = jnp.concatenate([glu.astype(BF16), y_row, y_col], axis=1)
        yx = jnp.dot(mixed, wout_ref[...], preferred_element_type=F32)
        yield
        x1 = xv + mod_ref[0, 2:3, :] * yx
        x1_ref[0, r0:r0 + sub, :] = x1
        h2 = _modulated_norm(x1, g2_ref[...], mod_ref[0, 3:4, :], mod_ref[0, 4:5, :])
        h2b = h2.astype(BF16)
        lg2 = lax.dot_general(wr_ref[...], h2b, (((1,), (1,)), ((), ())), preferred_element_type=F32)
        lg = lg2[:ROUTER_ROWS] + lg2[ROUTER_ROWS:] + br_ref[...]
        bucket, w_a, w_b = _route_rows(lg)
        r8 = lax.broadcasted_iota(jnp.int32, (8, sub), 0)
        route_ref[:, r0:r0 + sub] = jnp.where(r8 == 0, bucket, jnp.where(r8 == 1, w_a, jnp.where(r8 == 2, w_b, 0.0)))
        rl = lax.broadcasted_iota(jnp.int32, (ROW_EXTRA, sub), 0)
        gates_t = jnp.where(rl == 0, w_a, jnp.where(rl == 1, w_b, 0.0))
        h2p_ref[0, r0:r0 + sub, 0:d // 2] = _pack_bf16_pair(h2[:, :d // 2], h2[:, d // 2:])
        h2p_ref[0, r0:r0 + sub, d // 2:] = lax.bitcast_convert_type(gates_t.T, U32)
        rb = lax.broadcasted_iota(jnp.int32, (BUCKET_ROWS, sub), 0).astype(F32)
        counts_ref[...] += jnp.sum(jnp.where(rb == bucket, 1.0, 0.0), axis=-1, keepdims=True)
        yield

    n_stage = 4
    tiles = [sub_tile(s) for s in range(n_sub)]
    for step in range(n_sub + n_stage - 1):
        for s in range(n_sub):
            if 0 <= step - s < n_stage:
                next(tiles[s])


def _mix(x, y_s5, mods, norm1_g, norm2_g, conv_w, w_conv, w_glu, b_glu, w_out, w_router_t, b_router_t, tm, b0, nb):
    _, n, d = x.shape
    nt = n // tm
    halo_blocks = n // GRID_W
    per_tile = tm // GRID_W
    tok_out = lambda w: pl.BlockSpec((1, tm, w), lambda b, i: (b, i, 0))
    full = lambda a: pl.BlockSpec(a.shape, lambda b, i: (0,) * a.ndim)
    args = (x, x, x, y_s5, mods, norm1_g.reshape(1, d), norm2_g.reshape(1, d), conv_w, w_conv, w_glu,
            b_glu.reshape(1, -1), w_out, w_router_t, b_router_t)
    in_specs = [pl.BlockSpec((1, tm, d), lambda b, i: (b + b0, i, 0)),
                pl.BlockSpec((1, GRID_W, d), lambda b, i: (b + b0, jnp.maximum(i * per_tile - 1, 0), 0)),
                pl.BlockSpec((1, GRID_W, d),
                             lambda b, i: (b + b0, jnp.minimum((i + 1) * per_tile, halo_blocks - 1), 0)),
                pl.BlockSpec((1, y_s5.shape[1], tm, LANES), lambda b, i: (b + b0, 0, i, 0)),
                pl.BlockSpec((1, N_MOD, d), lambda b, i: (b + b0, 0, 0))] + [full(a) for a in args[5:]]
    return pl.pallas_call(
        _mix_kernel,
        grid=(nb, nt),
        in_specs=in_specs,
        out_specs=[tok_out(d), tok_out(d // 2 + ROW_EXTRA),
                   pl.BlockSpec((8, tm), lambda b, i: (0, b * nt + i)),
                   pl.BlockSpec((BUCKET_ROWS, 128), lambda b, i: (0, 0))],
        out_shape=[jax.ShapeDtypeStruct((nb, n, d), F32),
                   jax.ShapeDtypeStruct((nb, n, d // 2 + ROW_EXTRA), U32),
                   jax.ShapeDtypeStruct((8, nb * n), F32),
                   jax.ShapeDtypeStruct((BUCKET_ROWS, 128), F32)],
        compiler_params=_params(("arbitrary", "arbitrary")),
        name="mix",
    )(*args)


def _rank_kernel(route_ref, offs_ref, dest_ref, run_ref):
    tr = route_ref.shape[1]

    @pl.when(pl.program_id(0) == 0)
    def _():
        run_ref[...] = jnp.zeros_like(run_ref)

    bucket = route_ref[0:1, :]
    rb = lax.broadcasted_iota(jnp.int32, (BUCKET_ROWS, tr), 0).astype(F32)
    onehot = jnp.where(rb == bucket, 1.0, 0.0)
    s_idx = lax.broadcasted_iota(jnp.int32, (tr, tr), 0)
    t_idx = lax.broadcasted_iota(jnp.int32, (tr, tr), 1)
    tri = jnp.where(s_idx <= t_idx, 1.0, 0.0).astype(BF16)
    prefix = jnp.dot(onehot.astype(BF16), tri, preferred_element_type=F32)
    before = run_ref[:, 0:1] + offs_ref[:, 0:1]
    dest = jnp.sum(onehot * (prefix - 1.0 + before), axis=0, keepdims=True)
    dest_ref[...] = dest.astype(jnp.int32)
    run_ref[...] += prefix[:, tr - 1:tr]


def _rank(route_t, offs_rows, tr):
    n = route_t.shape[1]
    return pl.pallas_call(
        _rank_kernel,
        grid=(n // tr,),
        in_specs=[pl.BlockSpec((8, tr), lambda i: (0, i)),
                  pl.BlockSpec((BUCKET_ROWS, 128), lambda i: (0, 0))],
        out_specs=pl.BlockSpec((1, tr), lambda i: (0, i)),
        out_shape=jax.ShapeDtypeStruct((1, n), jnp.int32),
        scratch_shapes=[pltpu.VMEM((BUCKET_ROWS, 128), F32)],
        compiler_params=_params(("arbitrary",)),
        name="rank",
    )(route_t, offs_rows)


def _sc_mesh():
    return plsc.VectorSubcoreMesh(core_axis_name="core", subcore_axis_name="subcore")


def _scatter_rows(src, dest, n_out):
    n, w = src.shape
    sub = SC_INDEX_TILE // SC_ROWS

    @functools.partial(pl.kernel, out_type=jax.ShapeDtypeStruct((n_out, w), src.dtype), mesh=_sc_mesh(),
                       scratch_types=[], name="scatter_rows")
    def scatter(x_hbm, i_hbm, o_hbm):
        def body(x_vmem, i_vmem):
            j = pl.program_id(1)
            pltpu.sync_copy(x_vmem, o_hbm.at[i_vmem.at[0, pl.ds(j * SC_ROWS, SC_ROWS)]])

        pltpu.emit_pipeline(
            body,
            grid=(n // SC_INDEX_TILE, sub),
            in_specs=[pl.BlockSpec((SC_ROWS, w), lambda i, j: (i * sub + j, 0)),
                      pl.BlockSpec((1, SC_INDEX_TILE), lambda i, j: (0, i))],
            out_specs=[],
            core_axis_name=("core", "subcore"),
            dimension_semantics=(pltpu.PARALLEL, pltpu.ARBITRARY),
        )(x_hbm, i_hbm)

    return scatter(src, dest)


def _gather_rows(src, idx):
    n = idx.shape[1]
    w = src.shape[1]
    sub = SC_INDEX_TILE // SC_ROWS

    @functools.partial(pl.kernel, out_type=jax.ShapeDtypeStruct((n, w), src.dtype), mesh=_sc_mesh(),
                       scratch_types=[], name="gather_rows")
    def gather(x_hbm, i_hbm, o_hbm):
        def body(i_vmem, o_vmem):
            j = pl.program_id(1)
            pltpu.sync_copy(x_hbm.at[i_vmem.at[0, pl.ds(j * SC_ROWS, SC_ROWS)]], o_vmem)

        pltpu.emit_pipeline(
            body,
            grid=(n // SC_INDEX_TILE, sub),
            in_specs=[pl.BlockSpec((1, SC_INDEX_TILE), lambda i, j: (0, i))],
            out_specs=[pl.BlockSpec((SC_ROWS, w), lambda i, j: (i * sub + j, 0))],
            core_axis_name=("core", "subcore"),
            dimension_semantics=(pltpu.PARALLEL, pltpu.ARBITRARY),
        )(i_hbm, o_hbm)

    return gather(src, idx)


def _cast_kernel(*refs):
    n = len(refs) // 2
    for src, dst in zip(refs[:n], refs[n:]):
        dst[...] = src[...].astype(BF16)


def _cast_experts(w1, w3, w2):
    ne = CAST_EXPERTS if w1.shape[0] % CAST_EXPERTS == 0 else 1
    specs = [pl.BlockSpec((ne,) + w.shape[1:], lambda e: (e, 0, 0)) for w in (w1, w3, w2)]
    return pl.pallas_call(
        _cast_kernel,
        grid=(w1.shape[0] // ne,),
        in_specs=specs,
        out_specs=specs,
        out_shape=[jax.ShapeDtypeStruct(w.shape, BF16) for w in (w1, w3, w2)],
        compiler_params=_params(("parallel",)),
        name="cast_experts",
    )(w1, w3, w2)


def _moe_kernel(ea_ref, eb_ref, valid_ref, rows_ref, w1a_ref, w3a_ref, w2a_ref, w1b_ref, w3b_ref, w2b_ref, o_ref):
    j = pl.program_id(0)
    half = rows_ref.shape[1] - ROW_EXTRA

    @pl.when(valid_ref[j] != 0)
    def _():
        ha, hb = _unpack_bf16_pair(rows_ref[:, 0:half])
        ha = ha.astype(BF16)
        hb = hb.astype(BF16)
        gates = lax.bitcast_convert_type(rows_ref[:, half:], F32)

        def expert(w1_ref, w3_ref, w2_ref, gate):
            def up(w_ref):
                return (jnp.dot(ha, w_ref[0, 0:half, :], preferred_element_type=F32)
                        + jnp.dot(hb, w_ref[0, half:, :], preferred_element_type=F32))

            a1 = up(w1_ref)
            he = (a1 * jax.nn.sigmoid(a1)) * up(w3_ref) * gate
            return jnp.dot(he.astype(BF16), w2_ref[0], preferred_element_type=F32)

        y = (expert(w1a_ref, w3a_ref, w2a_ref, gates[:, 0:1]) + expert(w1b_ref, w3b_ref, w2b_ref, gates[:, 1:2]))
        o_ref[...] = _pack_bf16_pair(y[:, :half], y[:, half:])


def _moe_grouped(rows, tile_ea, tile_eb, tile_valid, w1, w3, w2, tmm):
    r, w = rows.shape
    n_e, de, d = w2.shape
    up_a = pl.BlockSpec((1, d, de), lambda j, ea, eb, va: (ea[j], 0, 0))
    up_b = pl.BlockSpec((1, d, de), lambda j, ea, eb, va: (eb[j], 0, 0))
    grid_spec = pltpu.PrefetchScalarGridSpec(
        num_scalar_prefetch=3,
        grid=(r // tmm,),
        in_specs=[pl.BlockSpec((tmm, w), lambda j, ea, eb, va: (j, 0)),
                  up_a, up_a, pl.BlockSpec((1, de, d), lambda j, ea, eb, va: (ea[j], 0, 0)),
                  up_b, up_b, pl.BlockSpec((1, de, d), lambda j, ea, eb, va: (eb[j], 0, 0))],
        out_specs=pl.BlockSpec((tmm, d // 2), lambda j, ea, eb, va: (j, 0)),
    )
    return pl.pallas_call(
        _moe_kernel,
        grid_spec=grid_spec,
        out_shape=jax.ShapeDtypeStruct((r, d // 2), U32),
        compiler_params=_params(("arbitrary",)),
        name="moe",
    )(tile_ea, tile_eb, tile_valid, rows, w1, w3, w2, w1, w3, w2)


def _final_kernel(x1_ref, moe_ref, mod_ref, fg_ref, *rest):
    o_ref = rest[-1]
    ya, yb = _unpack_bf16_pair(moe_ref[0])
    half = ya.shape[1]
    gate = mod_ref[0, 5:6, :]
    xa = x1_ref[0, :, 0:half] + gate[:, 0:half] * ya
    xb = x1_ref[0, :, half:] + gate[:, half:] * yb
    ms = (jnp.sum(xa * xa, axis=-1, keepdims=True) + jnp.sum(xb * xb, axis=-1, keepdims=True)) / (2 * half)
    inv = lax.rsqrt(ms + RMS_EPS)
    o_ref[0, :, 0:half] = xa * inv * fg_ref[:, 0:half]
    o_ref[0, :, half:] = xb * inv * fg_ref[:, half:]


def _final(x1, moe_tok, mods, final_g, tm, b0, bsz, out_prev):
    nb, n, d = x1.shape
    tok = lambda w: pl.BlockSpec((1, tm, w), lambda b, i: (b, i, 0))
    args = [x1, moe_tok, mods, final_g.reshape(1, d)]
    in_specs = [tok(d), tok(d // 2), pl.BlockSpec((1, N_MOD, d), lambda b, i: (b + b0, 0, 0)),
                pl.BlockSpec((1, d), lambda b, i: (0, 0))]
    aliases = {}
    if out_prev is not None:
        args.append(out_prev)
        in_specs.append(pl.BlockSpec(memory_space=pl.ANY))
        aliases = {len(args) - 1: 0}
    return pl.pallas_call(
        _final_kernel,
        grid=(nb, n // tm),
        in_specs=in_specs,
        out_specs=pl.BlockSpec((1, tm, d), lambda b, i: (b + b0, i, 0)),
        out_shape=jax.ShapeDtypeStruct((bsz, n, d), F32),
        input_output_aliases=aliases,
        compiler_params=_params(("parallel", "parallel")),
        name="final",
    )(*args)


def _tile_plan(counts, tmm, n_tiles):
    tiles = (counts + (tmm - 1)) // tmm
    tile_end = jnp.cumsum(tiles)
    offs = (tile_end - tiles) * tmm
    n_valid = tile_end[-1]
    j = jnp.arange(n_tiles, dtype=jnp.int32)
    bucket = jnp.sum((tile_end[None, :] <= jnp.minimum(j, n_valid - 1)[:, None]).astype(jnp.int32), axis=1)
    pair_lo = jnp.array([0, 0, 0, 1, 1, 2], jnp.int32)
    pair_hi = jnp.array([1, 2, 3, 2, 3, 3], jnp.int32)
    group = bucket // PAIRS_PER_GROUP
    pair = bucket % PAIRS_PER_GROUP
    tile_ea = group * EXPERTS_PER_GROUP + pair_lo[pair]
    tile_eb = group * EXPERTS_PER_GROUP + pair_hi[pair]
    return offs, tile_ea, tile_eb, (j < n_valid).astype(jnp.int32)


def kernel(x, c, ctx, c_ctx, w_mod, b_mod, norm1_g, norm2_g, w_in, s5_lambda_re, s5_lambda_im, s5_log_dt,
           s5_b_re, s5_b_im, s5_c_re, s5_c_im, s5_d, w_glu, b_glu, conv_w, w_out, router_group_w,
           router_group_b, router_expert_w, router_expert_b, expert_w1, expert_w3, expert_w2, final_g):
    assert w_mod.shape[0] == 1, "single-layer kernel"
    bsz, n_tok, d = x.shape
    n_ctx = ctx.shape[1]
    n_all = bsz * n_tok
    l = 0
    tm = min(TOKEN_TILE, n_tok)

    n_cond = bsz + 1
    pad = (-n_cond) % 8
    cond = jnp.concatenate([c, c_ctx[None, :], jnp.zeros((pad, d), F32)], axis=0)
    m = _mod_rows(cond, w_mod[l], b_mod[l])
    mx = m[:bsz].reshape(bsz, N_MOD, d)
    mc = m[bsz:bsz + 1].reshape(1, N_MOD, d)

    w_in_b = w_in[l].astype(BF16)
    w_s5 = w_in_b[:, :S5_WIDTH]
    o_c = S5_WIDTH + CONV_WIDTH
    o_v = S5_WIDTH + 2 * CONV_WIDTH
    w_conv = jnp.concatenate(
        [w_in_b[:, S5_WIDTH:o_c], w_in_b[:, o_c:o_c + CONV_ROW_WIDTH], w_in_b[:, o_v:o_v + CONV_ROW_WIDTH],
         w_in_b[:, o_c + CONV_ROW_WIDTH:o_v], w_in_b[:, o_v + CONV_ROW_WIDTH:]], axis=1)
    w_experts = _cast_experts(expert_w1[l], expert_w3[l], expert_w2[l])
    u = _inproj(x, mx, True, norm1_g[l], w_s5, min(INPROJ_TILE, n_tok), "inproj")
    uc = _inproj(ctx, mc, False, norm1_g[l], w_s5, min(INPROJ_TILE, n_ctx), "inproj_ctx")

    t_mat, mb_pair, mc_pair, a_rows = _s5_matrices(
        s5_lambda_re[l], s5_lambda_im[l], s5_log_dt[l], s5_b_re[l], s5_b_im[l], s5_c_re[l], s5_c_im[l], s5_d[l])
    y_c = _s5_scan(_chunkify(u, "chunkify"), _chunkify(uc, "chunkify_ctx"), t_mat, mb_pair, mc_pair, a_rows)
    y_s5 = _unchunkify(y_c, u.shape[1])

    n_logits = N_GROUPS + N_EXPERTS
    w_router = jnp.concatenate(
        [router_group_w[l], router_expert_w[l], jnp.zeros((d, ROUTER_ROWS - n_logits), F32)], axis=1).T
    w_router_hi = w_router.astype(BF16)
    w_router_lo = (w_router - w_router_hi.astype(F32)).astype(BF16)
    w_router_t = jnp.concatenate([w_router_hi, w_router_lo], axis=0)
    b_router = jnp.concatenate([router_group_b[l], router_expert_b[l], jnp.zeros((ROUTER_ROWS - n_logits,), F32)])
    tm_mix = min(MIX_TILE, n_tok)
    b_router_t = jnp.broadcast_to(b_router[:, None], (ROUTER_ROWS, min(MIX_SUB, tm_mix)))

    n_parts = MOE_PARTS if bsz % MOE_PARTS == 0 else 1
    nb = bsz // n_parts
    n_part = nb * n_tok
    n_buckets = N_GROUPS * PAIRS_PER_GROUP
    n_rows = n_part + n_buckets * MOE_TILE
    w_glu_b = w_glu[l].astype(BF16)
    w_out_b = w_out[l].astype(BF16)
    staged = []
    for p in range(n_parts):
        x1, h2p, route_t, counts = _mix(x, y_s5, mx, norm1_g[l], norm2_g[l], conv_w[l], w_conv, w_glu_b, b_glu[l],
                                        w_out_b, w_router_t, b_router_t, tm_mix, p * nb, nb)
        offs, *tiles = _tile_plan(counts[:n_buckets, 0].astype(jnp.int32), MOE_TILE, n_rows // MOE_TILE)
        offs_rows = jnp.zeros((BUCKET_ROWS,), F32).at[:n_buckets].set(offs.astype(F32))
        dest = _rank(route_t, jnp.broadcast_to(offs_rows[:, None], (BUCKET_ROWS, 128)), min(RANK_TILE, n_part))
        rows = _scatter_rows(h2p.reshape(n_part, d // 2 + ROW_EXTRA), dest, n_rows)
        staged.append((x1, rows, dest, tiles))
    out = None
    for p, (x1, rows, dest, tiles) in enumerate(staged):
        y_rows = _moe_grouped(rows, *tiles, *w_experts, MOE_TILE)
        moe_tok = _gather_rows(y_rows, dest).reshape(nb, n_tok, d // 2)
        out = _final(x1, moe_tok, mx, final_g, tm, p * nb, bsz, out)
    return out
```

```python
import functools

import jax
import jax.numpy as jnp
from jax import lax
from jax.experimental import pallas as pl
from jax.experimental.pallas import tpu as pltpu
from jax.experimental.pallas import tpu_sc as plsc

F32 = jnp.float32
BF16 = jnp.bfloat16
U32 = jnp.uint32

RMS_EPS = 1e-6
N_MOD = 6
GRID_W = 64
S5_WIDTH = 256
S5_H = 16
S5_P = 64
S5_GROUPS = S5_WIDTH // S5_H
S5_PAIRS = S5_GROUPS // 2
S5_CHUNK = 16
LANES = 128
STATE_PITCH = 136
CONV_WIDTH = 768
CONV_ROW_WIDTH = CONV_WIDTH // 2
N_GROUPS = 4
EXPERTS_PER_GROUP = 4
N_EXPERTS = N_GROUPS * EXPERTS_PER_GROUP
PAIRS_PER_GROUP = 6
ROUTER_ROWS = 32
BUCKET_ROWS = 32
ROW_EXTRA = 128
TOKEN_TILE = 1024
MOE_TILE = 512
RANK_TILE = 2048
INPROJ_TILE = 2048
MIX_TILE = 1024
MIX_SUB = 512
MOE_PARTS = 2
LAYOUT_GROUPS = 2
SC_ROWS = 32
SC_INDEX_TILE = 128
VMEM_LIMIT = 52 * 1024 * 1024


def _params(sem, vmem=VMEM_LIMIT):
    return pltpu.CompilerParams(dimension_semantics=sem, vmem_limit_bytes=vmem)


def _mod_kernel(c_ref, w_ref, b_ref, o_ref):
    c = c_ref[...]
    o_ref[...] = jnp.dot(c * jax.nn.sigmoid(c), w_ref[...], preferred_element_type=F32) + b_ref[...]


def _mod_rows(cond, w_mod, b_mod):
    n, d = cond.shape
    nout = w_mod.shape[1]
    bn = d
    return pl.pallas_call(
        _mod_kernel,
        grid=(nout // bn,),
        in_specs=[pl.BlockSpec((n, d), lambda j: (0, 0)),
                  pl.BlockSpec((d, bn), lambda j: (0, j)),
                  pl.BlockSpec((1, bn), lambda j: (0, j))],
        out_specs=pl.BlockSpec((n, bn), lambda j: (0, j)),
        out_shape=jax.ShapeDtypeStruct((n, nout), F32),
        compiler_params=_params(("arbitrary",)),
        name="mod",
    )(cond, w_mod, b_mod.reshape(1, nout))


def _modulated_norm(x, g, shift, scale):
    ms = jnp.mean(x * x, axis=-1, keepdims=True)
    return (x * lax.rsqrt(ms + RMS_EPS)) * (g * (1.0 + scale)) + shift


def _inproj_kernel(x_ref, mod_ref, g_ref, w_ref, u_ref):
    h = _modulated_norm(x_ref[0], g_ref[...], mod_ref[0, 0:1, :], mod_ref[0, 1:2, :]).astype(BF16)
    u = jnp.dot(h, w_ref[...], preferred_element_type=F32)
    for j in range(S5_WIDTH // LANES):
        u_ref[0, j] = u[:, j * LANES:(j + 1) * LANES]


def _inproj(x, mods, per_batch_mod, norm_g, w_s5, tm, name):
    bsz, n, d = x.shape
    mod_map = (lambda b, i: (b, 0, 0)) if per_batch_mod else (lambda b, i: (0, 0, 0))
    n_slab = S5_WIDTH // LANES
    return pl.pallas_call(
        _inproj_kernel,
        grid=(bsz, n // tm),
        in_specs=[pl.BlockSpec((1, tm, d), lambda b, i: (b, i, 0)),
                  pl.BlockSpec((1, N_MOD, d), mod_map),
                  pl.BlockSpec((1, d), lambda b, i: (0, 0)),
                  pl.BlockSpec((d, S5_WIDTH), lambda b, i: (0, 0))],
        out_specs=pl.BlockSpec((1, n_slab, tm, LANES), lambda b, i: (b, 0, i, 0)),
        out_shape=jax.ShapeDtypeStruct((bsz, n_slab, n, LANES), F32),
        compiler_params=_params(("parallel", "parallel")),
        name=name,
    )(x, mods, norm_g.reshape(1, d), w_s5)


def _toeplitz_kernel(strip_ref, t_ref):
    lc = t_ref.shape[1] // S5_H
    for s in range(lc):
        off = (lc - 1 - s) * S5_H
        t_ref[0, s * S5_H:(s + 1) * S5_H, :] = strip_ref[0, :, off:off + lc * S5_H].astype(BF16)


def _toeplitz(strip):
    g_n, h_n, w = strip.shape
    n = S5_CHUNK * h_n
    return pl.pallas_call(
        _toeplitz_kernel,
        grid=(g_n,),
        in_specs=[pl.BlockSpec((1, h_n, w), lambda g: (g, 0, 0))],
        out_specs=pl.BlockSpec((1, n, n), lambda g: (g, 0, 0)),
        out_shape=jax.ShapeDtypeStruct((g_n, n, n), BF16),
        compiler_params=_params(("parallel",)),
        name="toeplitz",
    )(strip)


def _s5_matrices(lam_re, lam_im, log_dt, b_re, b_im, c_re, c_im, d_skip):
    lc, g_n, p_n, h_n = S5_CHUNK, S5_GROUPS, S5_P, S5_H
    lam = lax.complex(lam_re.astype(F32), lam_im.astype(F32))
    dt = jnp.exp(log_dt.astype(F32))[..., None]
    a_bar = jnp.exp(lam * dt)
    b_bar = ((a_bar - 1.0) / lam)[..., None] * lax.complex(b_re.astype(F32), b_im.astype(F32))
    cm = lax.complex(c_re.astype(F32), c_im.astype(F32))
    steps = jnp.arange(lc + 1, dtype=F32)
    apow = jnp.exp((lam * dt)[:, :, None, :] * steps[None, None, :, None])
    kern = jnp.einsum('dgop,dgjp,dgpi->dgjio', cm, apow[:, :, :lc], b_bar).real
    skip = jnp.eye(h_n, dtype=F32) * d_skip.astype(F32).reshape(g_n, 1, h_n)
    centre = kern[0, :, 0] + kern[1, :, 0] + skip
    lags = jnp.concatenate([kern[1, :, :0:-1], centre[:, None], kern[0, :, 1:]], axis=1)
    strip = lags.transpose(0, 2, 1, 3).reshape(g_n, h_n, (2 * lc - 1) * h_n)
    strip = jnp.pad(strip, ((0, 0), (0, 0), (0, h_n)))
    t_mat = _toeplitz(strip)

    def in_mat(pw, bb):
        return (pw[:, :, None, :] * bb.transpose(0, 2, 1)[:, None, :, :]).reshape(g_n, lc * h_n, p_n)

    mb_f = in_mat(apow[0, :, lc - 1::-1][:, :lc], b_bar[0])
    mb_b = in_mat(apow[1, :, :lc], b_bar[1])

    def out_mat(pw, cc):
        return (pw.transpose(0, 2, 1)[:, :, :, None] * cc.transpose(0, 2, 1)[:, :, None, :]).reshape(
            g_n, p_n, lc * h_n)

    mc_f = out_mat(apow[0, :, 1:lc + 1], cm[0])
    mc_b = out_mat(apow[1, :, lc:0:-1], cm[1])
    a_chunk = apow[:, :, lc]

    q_n = S5_PAIRS
    zeros_in = jnp.zeros((g_n, lc * h_n, p_n), F32)

    def pair_cols(m):
        m = m.reshape(q_n, 2, lc * h_n, p_n)
        z = zeros_in.reshape(q_n, 2, lc * h_n, p_n)[:, 0]
        top = jnp.concatenate([m[:, 0], z], axis=-1)
        bot = jnp.concatenate([z, m[:, 1]], axis=-1)
        return jnp.concatenate([top, bot], axis=1)

    mb_pair = jnp.concatenate([pair_cols(mb_f.real), pair_cols(mb_f.imag),
                               pair_cols(mb_b.real), pair_cols(mb_b.imag)], axis=-1)

    def pair_rows(m):
        m = m.reshape(q_n, 2, p_n, lc * h_n)
        z = jnp.zeros_like(m[:, 0])
        top = jnp.concatenate([m[:, 0], z], axis=-1)
        bot = jnp.concatenate([z, m[:, 1]], axis=-1)
        return jnp.concatenate([top, bot], axis=1)

    mc_pair = jnp.concatenate([pair_rows(mc_f.real), pair_rows(-mc_f.imag),
                               pair_rows(mc_b.real), pair_rows(-mc_b.imag)], axis=1)
    a_rows = jnp.stack([a_chunk[0].real, a_chunk[0].imag, a_chunk[1].real, a_chunk[1].imag], axis=0)
    a_rows = a_rows.reshape(4, q_n, 2 * p_n).transpose(1, 0, 2)
    a_rows = jnp.concatenate([a_rows, jnp.zeros_like(a_rows)], axis=1)
    return t_mat, mb_pair.astype(BF16), mc_pair.astype(BF16), a_rows


def _chunkify_kernel(u_ref, o_ref, *, gb):
    _, nb, nc, _ = o_ref.shape
    half = S5_CHUNK * S5_H
    per_slab = LANES // S5_H
    for b0 in range(0, nb, gb):
        for j in range(u_ref.shape[1]):
            cols = []
            for t in range(S5_CHUNK):
                rows = [u_ref[b0 + b, j, pl.ds(t, nc, stride=S5_CHUNK), :] for b in range(gb)]
                cols.append((rows[0] if gb == 1 else jnp.concatenate(rows, axis=0)).T)
            for gl in range(per_slab):
                g = j * per_slab + gl
                m = jnp.concatenate([c[gl * S5_H:(gl + 1) * S5_H, :] for c in cols], axis=0)
                o_ref[g // 2, b0:b0 + gb, :, (g % 2) * half:(g % 2 + 1) * half] = (
                    m.T.astype(BF16).reshape(gb, nc, half))


def _layout_step(bsz, gb):
    return gb * LAYOUT_GROUPS if bsz % (gb * LAYOUT_GROUPS) == 0 else gb


def _chunkify(u_slab, name):
    bsz, n_slab, n, _ = u_slab.shape
    nc = n // S5_CHUNK
    gb = min(bsz, max(1, LANES // nc))
    nb = _layout_step(bsz, gb)
    w = 2 * S5_CHUNK * S5_H
    return pl.pallas_call(
        functools.partial(_chunkify_kernel, gb=gb),
        grid=(bsz // nb,),
        in_specs=[pl.BlockSpec((nb, n_slab, n, LANES), lambda b: (b, 0, 0, 0))],
        out_specs=pl.BlockSpec((S5_PAIRS, nb, nc, w), lambda b: (0, b, 0, 0)),
        out_shape=jax.ShapeDtypeStruct((S5_PAIRS, bsz, nc, w), BF16),
        compiler_params=_params(("parallel",)),
        name=name,
    )(u_slab)


def _unchunkify_kernel(y_ref, o_ref):
    _, nb, nc, _ = y_ref.shape
    half = S5_CHUNK * S5_H
    per_slab = LANES // S5_H
    for b in range(nb):
        for j in range(o_ref.shape[1]):
            rows = []
            for gl in range(per_slab):
                g = j * per_slab + gl
                rows.append(y_ref[g // 2, b, :, (g % 2) * half:(g % 2 + 1) * half].astype(F32).T)
            for t in range(S5_CHUNK):
                bt = jnp.concatenate([r[t * S5_H:(t + 1) * S5_H, :] for r in rows], axis=0)
                o_ref[b, j, pl.ds(t, nc, stride=S5_CHUNK), :] = bt.T


def _unchunkify(y_c, n_slab):
    q_n, bsz, nc, w = y_c.shape
    n = nc * S5_CHUNK
    nb = _layout_step(bsz, 1)
    return pl.pallas_call(
        _unchunkify_kernel,
        grid=(bsz // nb,),
        in_specs=[pl.BlockSpec((q_n, nb, nc, w), lambda b: (0, b, 0, 0))],
        out_specs=pl.BlockSpec((nb, n_slab, n, LANES), lambda b: (b, 0, 0, 0)),
        out_shape=jax.ShapeDtypeStruct((bsz, n_slab, n, LANES), F32),
        compiler_params=_params(("parallel",)),
        name="unchunkify",
    )(y_c)


def _s5_kernel(u_ref, uc_ref, t_ref, mb_ref, mc_ref, a_ref, y_ref, s_lat, s_ctx, h_scr, *, bb):
    _, bsz, n_lat, w = u_ref.shape
    n_ctx = uc_ref.shape[2]
    n_blk = w // LANES
    rb = bb * n_lat

    mb = mb_ref[0]

    def in_lat(i, carry):
        s = jnp.dot(u_ref[0, pl.ds(i * bb, bb)].reshape(rb, w), mb, preferred_element_type=F32)
        for k in range(bb):
            r = pl.multiple_of((i * bb + k) * STATE_PITCH, 8)
            for blk in range(n_blk):
                s_lat[blk, pl.ds(r, n_lat), :] = s[k * n_lat:(k + 1) * n_lat, blk * LANES:(blk + 1) * LANES]
        return carry

    lax.fori_loop(0, bsz // bb, in_lat, 0)
    sc = jnp.dot(uc_ref[0].reshape(bsz * n_ctx, w), mb, preferred_element_type=F32)
    for blk in range(n_blk):
        s_ctx[blk] = sc[:, blk * LANES:(blk + 1) * LANES]

    a_fr, a_fi, a_br, a_bi = (a_ref[0, k:k + 1, :] for k in range(4))

    def step(h, a_r, a_i, s_r, s_i):
        h_r, h_i = h
        return a_r * h_r - a_i * h_i + s_r, a_r * h_i + a_i * h_r + s_i

    def ctx_rows(blk, c):
        return s_ctx[blk, pl.ds(c, bsz, stride=n_ctx), :]

    def lat_rows(ref, blk, c):
        return ref.at[blk, pl.ds(c, bsz, stride=STATE_PITCH), :]

    def ctx_step(k, carry):
        hf, hb = carry
        kb = n_ctx - 1 - k
        hf = step(hf, a_fr, a_fi, ctx_rows(0, k), ctx_rows(1, k))
        hb = step(hb, a_br, a_bi, ctx_rows(2, kb), ctx_rows(3, kb))
        return hf, hb

    zero = jnp.zeros((bsz, LANES), F32)
    carry = lax.fori_loop(0, n_ctx, ctx_step, ((zero, zero), (zero, zero)))

    def lat_step(k, carry):
        hf, hb = carry
        kb = n_lat - 1 - k
        lat_rows(h_scr, 0, k)[...] = hf[0]
        lat_rows(h_scr, 1, k)[...] = hf[1]
        lat_rows(h_scr, 2, kb)[...] = hb[0]
        lat_rows(h_scr, 3, kb)[...] = hb[1]
        hf = step(hf, a_fr, a_fi, lat_rows(s_lat, 0, k)[...], lat_rows(s_lat, 1, k)[...])
        hb = step(hb, a_br, a_bi, lat_rows(s_lat, 2, kb)[...], lat_rows(s_lat, 3, kb)[...])
        return hf, hb

    lax.fori_loop(0, n_lat, lat_step, carry)

    t0 = t_ref[0]
    t1 = t_ref[1]
    mc = mc_ref[0]
    half = S5_CHUNK * S5_H

    def out_lat(i, carry):
        u = u_ref[0, pl.ds(i * bb, bb)].reshape(rb, w)
        h_rows = []
        for k in range(bb):
            r = pl.multiple_of((i * bb + k) * STATE_PITCH, 8)
            h_rows.append(jnp.concatenate([h_scr[blk, pl.ds(r, n_lat), :] for blk in range(n_blk)], axis=1))
        h = jnp.concatenate(h_rows, axis=0).astype(BF16)
        inter = jnp.dot(h, mc, preferred_element_type=F32)
        y0 = jnp.dot(u[:, :half], t0, preferred_element_type=F32) + inter[:, :half]
        y1 = jnp.dot(u[:, half:], t1, preferred_element_type=F32) + inter[:, half:]
        y = jnp.concatenate([y0, y1], axis=1).astype(BF16)
        y_ref[0, pl.ds(i * bb, bb)] = y.reshape(bb, n_lat, w)
        return carry

    lax.fori_loop(0, bsz // bb, out_lat, 0)


def _s5_scan(u_c, uc_c, t_mat, mb_pair, mc_pair, a_rows):
    q_n, bsz, n_lat, w = u_c.shape
    n_ctx = uc_c.shape[2]
    assert n_lat + 8 == STATE_PITCH
    bb = min(4, bsz)
    n_blk = w // LANES
    return pl.pallas_call(
        functools.partial(_s5_kernel, bb=bb),
        grid=(q_n,),
        in_specs=[pl.BlockSpec((1, bsz, n_lat, w), lambda q: (q, 0, 0, 0)),
                  pl.BlockSpec((1, bsz, n_ctx, w), lambda q: (q, 0, 0, 0)),
                  pl.BlockSpec((2, w // 2, w // 2), lambda q: (q, 0, 0)),
                  pl.BlockSpec((1, w, w), lambda q: (q, 0, 0)),
                  pl.BlockSpec((1, w, w), lambda q: (q, 0, 0)),
                  pl.BlockSpec((1, 8, LANES), lambda q: (q, 0, 0))],
        out_specs=pl.BlockSpec((1, bsz, n_lat, w), lambda q: (q, 0, 0, 0)),
        out_shape=jax.ShapeDtypeStruct((q_n, bsz, n_lat, w), BF16),
        scratch_shapes=[pltpu.VMEM((n_blk, bsz * STATE_PITCH, LANES), F32),
                        pltpu.VMEM((n_blk, bsz * n_ctx, LANES), F32),
                        pltpu.VMEM((n_blk, bsz * STATE_PITCH, LANES), F32)],
        compiler_params=_params(("parallel",)),
        name="s5_scan",
    )(u_c, uc_c, t_mat, mb_pair, mc_pair, a_rows)


def _pack_bf16_pair(a, b):
    ua = lax.bitcast_convert_type(a.astype(BF16).astype(F32), U32)
    ub = lax.bitcast_convert_type(b.astype(BF16).astype(F32), U32)
    return ua | (ub >> 16)


def _unpack_bf16_pair(w):
    a = lax.bitcast_convert_type(w & jnp.uint32(0xFFFF0000), F32)
    b = lax.bitcast_convert_type(w << 16, F32)
    return a, b


def _first_max(rows):
    best = rows[0]
    for r in rows[1:]:
        best = jnp.maximum(best, r)
    idx = jnp.full(best.shape, float(len(rows) - 1), F32)
    for k in range(len(rows) - 2, -1, -1):
        idx = jnp.where(rows[k] == best, float(k), idx)
    return best, idx


def _route_rows(lg):
    g_rows = [lg[k:k + 1] for k in range(N_GROUPS)]
    g_max, g_idx = _first_max(g_rows)
    g_sum = sum(jnp.exp(r - g_max) for r in g_rows)
    g_p = 1.0 / g_sum
    e_rows = []
    for j in range(EXPERTS_PER_GROUP):
        r = lg[N_GROUPS + (N_GROUPS - 1) * EXPERTS_PER_GROUP + j:][:1]
        for g in range(N_GROUPS - 2, -1, -1):
            k = N_GROUPS + g * EXPERTS_PER_GROUP + j
            r = jnp.where(g_idx == float(g), lg[k:k + 1], r)
        e_rows.append(r)
    v1, i1 = _first_max(e_rows)
    rest = [jnp.where(i1 == float(j), -jnp.inf, e_rows[j]) for j in range(EXPERTS_PER_GROUP)]
    v2, i2 = _first_max(rest)
    e21 = jnp.exp(v2 - v1)
    w1 = g_p / (1.0 + e21)
    w2 = w1 * e21
    lo = jnp.minimum(i1, i2)
    hi = jnp.maximum(i1, i2)
    base = jnp.where(lo == 0.0, 0.0, jnp.where(lo == 1.0, 3.0, 5.0))
    bucket = g_idx * float(PAIRS_PER_GROUP) + base + hi - lo - 1.0
    first_is_lo = i1 < i2
    return bucket, jnp.where(first_is_lo, w1, w2), jnp.where(first_is_lo, w2, w1)


def _mix_kernel(x_ref, xup_ref, xdn_ref, y_ref, mod_ref, g1_ref, g2_ref, cw_ref, win_ref, wglu_ref, bglu_ref,
                wout_ref, wr_ref, br_ref, *rest):
    n_cast = (len(rest) - 4) // 2
    x1_ref, h2p_ref, route_ref, counts_ref = rest[n_cast:n_cast + 4]
    for src, dst in zip(rest[:n_cast], rest[n_cast + 4:]):
        dst[...] = src[...].astype(BF16)
    i = pl.program_id(1)
    tm = x_ref.shape[1]
    d = x_ref.shape[2]
    sub = min(MIX_SUB, tm)
    n_sub = tm // sub
    cw, rw = CONV_WIDTH, CONV_ROW_WIDTH

    def hidden(xv):
        return _modulated_norm(xv, g1_ref[...], mod_ref[0, 0:1, :], mod_ref[0, 1:2, :]).astype(BF16)

    def halo(h):
        zh = jnp.dot(h, win_ref[:, 2 * cw:3 * cw], preferred_element_type=F32)
        return zh[:, 0:rw] * zh[:, rw:cw]

    @pl.when(jnp.logical_and(pl.program_id(0) == 0, i == 0))
    def _():
        counts_ref[...] = jnp.zeros_like(counts_ref)

    def sub_tile(s):
        r0 = s * sub
        xv = x_ref[0, r0:r0 + sub, :]
        hx = hidden(xv)
        g = jax.nn.gelu(jnp.concatenate([y_ref[0, j, r0:r0 + sub, :] for j in range(y_ref.shape[1])], axis=1))
        if s == 0:
            hx_up = hidden(xup_ref[0])
        else:
            hx_up = hidden(x_ref[0, r0 - GRID_W:r0, :])
        if s == n_sub - 1:
            hx_dn = hidden(xdn_ref[0])
        else:
            hx_dn = hidden(x_ref[0, r0 + sub:r0 + sub + GRID_W, :])
        yield
        z_r = jnp.dot(hx, win_ref[:, cw:2 * cw], preferred_element_type=F32)
        cr = z_r[:, 0:rw] * z_r[:, rw:cw]
        z_c = jnp.dot(hx, win_ref[:, 2 * cw:3 * cw], preferred_element_type=F32)
        cc = z_c[:, 0:rw] * z_c[:, rw:cw]
        bg = jnp.dot(hx, win_ref[:, 0:cw], preferred_element_type=F32)
        up_halo = halo(hx_up)
        dn_halo = halo(hx_dn)
        glu = g * jax.nn.sigmoid(jnp.dot(g.astype(BF16), wglu_ref[...], preferred_element_type=F32) + bglu_ref[...])
        yield
        row = lax.broadcasted_iota(jnp.int32, (sub, 1), 0)
        col_in_row = row % GRID_W
        left = jnp.where(col_in_row == 0, 0.0, pltpu.roll(cr, 1, axis=0))
        right = jnp.where(col_in_row == GRID_W - 1, 0.0, pltpu.roll(cr, sub - 1, axis=0))
        w_r = cw_ref[:, :rw]
        row_part = left * w_r[0:1] + cr * w_r[1:2] + right * w_r[2:3]
        if s == 0:
            up_halo = jnp.where(i == 0, 0.0, up_halo)
        if s == n_sub - 1:
            dn_halo = jnp.where(i == pl.num_programs(1) - 1, 0.0, dn_halo)
        up = jnp.concatenate([up_halo, cc[:sub - GRID_W]], axis=0)
        dn = jnp.concatenate([cc[GRID_W:], dn_halo], axis=0)
        w_c = cw_ref[:, rw:]
        col_part = up * w_c[0:1] + cc * w_c[1:2] + dn * w_c[2:3]
        y_row = (bg[:, 0:rw] * row_part).astype(BF16)
        y_col = (bg[:, rw:cw] * col_part).astype(BF16)
        mixed = jnp.concatenate([glu.astype(BF16), y_row, y_col], axis=1)
        yx = jnp.dot(mixed, wout_ref[...], preferred_element_type=F32)
        yield
        x1 = xv + mod_ref[0, 2:3, :] * yx
        x1_ref[0, r0:r0 + sub, :] = x1
        h2 = _modulated_norm(x1, g2_ref[...], mod_ref[0, 3:4, :], mod_ref[0, 4:5, :])
        h2b = h2.astype(BF16)
        lg2 = lax.dot_general(wr_ref[...], h2b, (((1,), (1,)), ((), ())), preferred_element_type=F32)
        lg = lg2[:ROUTER_ROWS] + lg2[ROUTER_ROWS:] + br_ref[...]
        bucket, w_a, w_b = _route_rows(lg)
        r8 = lax.broadcasted_iota(jnp.int32, (8, sub), 0)
        route_ref[:, r0:r0 + sub] = jnp.where(r8 == 0, bucket, jnp.where(r8 == 1, w_a, jnp.where(r8 == 2, w_b, 0.0)))
        rl = lax.broadcasted_iota(jnp.int32, (ROW_EXTRA, sub), 0)
        gates_t = jnp.where(rl == 0, w_a, jnp.where(rl == 1, w_b, 0.0))
        h2p_ref[0, r0:r0 + sub, 0:d // 2] = _pack_bf16_pair(h2[:, :d // 2], h2[:, d // 2:])
        h2p_ref[0, r0:r0 + sub, d // 2:] = lax.bitcast_convert_type(gates_t.T, U32)
        rb = lax.broadcasted_iota(jnp.int32, (BUCKET_ROWS, sub), 0).astype(F32)
        counts_ref[...] += jnp.sum(jnp.where(rb == bucket, 1.0, 0.0), axis=-1, keepdims=True)
        yield

    n_stage = 4
    tiles = [sub_tile(s) for s in range(n_sub)]
    for step in range(n_sub + n_stage - 1):
        for s in range(n_sub):
            if 0 <= step - s < n_stage:
                next(tiles[s])


def _mix(x, y_s5, mods, norm1_g, norm2_g, conv_w, w_conv, w_glu, b_glu, w_out, w_router_t, b_router_t, tm, b0, nb,
         to_cast):
    _, n, d = x.shape
    nt = n // tm
    n_steps = nb * nt
    cast_specs = []
    for w in to_cast:
        per = -(-w.shape[0] // n_steps)
        n_blk = w.shape[0] // per
        assert n_blk * per == w.shape[0]
        cast_specs.append(pl.BlockSpec((per,) + w.shape[1:],
                                       lambda b, i, n_blk=n_blk: ((b * nt + i) * n_blk // n_steps, 0, 0)))
    halo_blocks = n // GRID_W
    per_tile = tm // GRID_W
    tok_out = lambda w: pl.BlockSpec((1, tm, w), lambda b, i: (b, i, 0))
    full = lambda a: pl.BlockSpec(a.shape, lambda b, i: (0,) * a.ndim)
    args = (x, x, x, y_s5, mods, norm1_g.reshape(1, d), norm2_g.reshape(1, d), conv_w, w_conv, w_glu,
            b_glu.reshape(1, -1), w_out, w_router_t, b_router_t)
    in_specs = [pl.BlockSpec((1, tm, d), lambda b, i: (b + b0, i, 0)),
                pl.BlockSpec((1, GRID_W, d), lambda b, i: (b + b0, jnp.maximum(i * per_tile - 1, 0), 0)),
                pl.BlockSpec((1, GRID_W, d),
                             lambda b, i: (b + b0, jnp.minimum((i + 1) * per_tile, halo_blocks - 1), 0)),
                pl.BlockSpec((1, y_s5.shape[1], tm, LANES), lambda b, i: (b + b0, 0, i, 0)),
                pl.BlockSpec((1, N_MOD, d), lambda b, i: (b + b0, 0, 0))] + [full(a) for a in args[5:]]
    return pl.pallas_call(
        _mix_kernel,
        grid=(nb, nt),
        in_specs=in_specs + cast_specs,
        out_specs=[tok_out(d), tok_out(d // 2 + ROW_EXTRA),
                   pl.BlockSpec((8, tm), lambda b, i: (0, b * nt + i)),
                   pl.BlockSpec((BUCKET_ROWS, 128), lambda b, i: (0, 0))] + cast_specs,
        out_shape=[jax.ShapeDtypeStruct((nb, n, d), F32),
                   jax.ShapeDtypeStruct((nb, n, d // 2 + ROW_EXTRA), U32),
                   jax.ShapeDtypeStruct((8, nb * n), F32),
                   jax.ShapeDtypeStruct((BUCKET_ROWS, 128), F32)]
        + [jax.ShapeDtypeStruct(w.shape, BF16) for w in to_cast],
        compiler_params=_params(("arbitrary", "arbitrary")),
        name="mix",
    )(*args, *to_cast)


def _rank_kernel(route_ref, offs_ref, dest_ref, run_ref):
    tr = route_ref.shape[1]

    @pl.when(pl.program_id(0) == 0)
    def _():
        run_ref[...] = jnp.zeros_like(run_ref)

    bucket = route_ref[0:1, :]
    rb = lax.broadcasted_iota(jnp.int32, (BUCKET_ROWS, tr), 0).astype(F32)
    onehot = jnp.where(rb == bucket, 1.0, 0.0)
    s_idx = lax.broadcasted_iota(jnp.int32, (tr, tr), 0)
    t_idx = lax.broadcasted_iota(jnp.int32, (tr, tr), 1)
    tri = jnp.where(s_idx <= t_idx, 1.0, 0.0).astype(BF16)
    prefix = jnp.dot(onehot.astype(BF16), tri, preferred_element_type=F32)
    before = run_ref[:, 0:1] + offs_ref[:, 0:1]
    dest = jnp.sum(onehot * (prefix - 1.0 + before), axis=0, keepdims=True)
    dest_ref[...] = dest.astype(jnp.int32)
    run_ref[...] += prefix[:, tr - 1:tr]


def _rank(route_t, offs_rows, tr):
    n = route_t.shape[1]
    return pl.pallas_call(
        _rank_kernel,
        grid=(n // tr,),
        in_specs=[pl.BlockSpec((8, tr), lambda i: (0, i)),
                  pl.BlockSpec((BUCKET_ROWS, 128), lambda i: (0, 0))],
        out_specs=pl.BlockSpec((1, tr), lambda i: (0, i)),
        out_shape=jax.ShapeDtypeStruct((1, n), jnp.int32),
        scratch_shapes=[pltpu.VMEM((BUCKET_ROWS, 128), F32)],
        compiler_params=_params(("arbitrary",)),
        name="rank",
    )(route_t, offs_rows)


def _sc_mesh():
    return plsc.VectorSubcoreMesh(core_axis_name="core", subcore_axis_name="subcore")


def _scatter_rows(src, dest, n_out):
    n, w = src.shape
    sub = SC_INDEX_TILE // SC_ROWS

    @functools.partial(pl.kernel, out_type=jax.ShapeDtypeStruct((n_out, w), src.dtype), mesh=_sc_mesh(),
                       scratch_types=[], name="scatter_rows")
    def scatter(x_hbm, i_hbm, o_hbm):
        def body(x_vmem, i_vmem):
            j = pl.program_id(1)
            pltpu.sync_copy(x_vmem, o_hbm.at[i_vmem.at[0, pl.ds(j * SC_ROWS, SC_ROWS)]])

        pltpu.emit_pipeline(
            body,
            grid=(n // SC_INDEX_TILE, sub),
            in_specs=[pl.BlockSpec((SC_ROWS, w), lambda i, j: (i * sub + j, 0)),
                      pl.BlockSpec((1, SC_INDEX_TILE), lambda i, j: (0, i))],
            out_specs=[],
            core_axis_name=("core", "subcore"),
            dimension_semantics=(pltpu.PARALLEL, pltpu.ARBITRARY),
        )(x_hbm, i_hbm)

    return scatter(src, dest)


def _gather_rows(src, idx):
    n = idx.shape[1]
    w = src.shape[1]
    sub = SC_INDEX_TILE // SC_ROWS

    @functools.partial(pl.kernel, out_type=jax.ShapeDtypeStruct((n, w), src.dtype), mesh=_sc_mesh(),
                       scratch_types=[], name="gather_rows")
    def gather(x_hbm, i_hbm, o_hbm):
        def body(i_vmem, o_vmem):
            j = pl.program_id(1)
            pltpu.sync_copy(x_hbm.at[i_vmem.at[0, pl.ds(j * SC_ROWS, SC_ROWS)]], o_vmem)

        pltpu.emit_pipeline(
            body,
            grid=(n // SC_INDEX_TILE, sub),
            in_specs=[pl.BlockSpec((1, SC_INDEX_TILE), lambda i, j: (0, i))],
            out_specs=[pl.BlockSpec((SC_ROWS, w), lambda i, j: (i * sub + j, 0))],
            core_axis_name=("core", "subcore"),
            dimension_semantics=(pltpu.PARALLEL, pltpu.ARBITRARY),
        )(i_hbm, o_hbm)

    return gather(src, idx)


def _moe_kernel(ea_ref, eb_ref, valid_ref, rows_ref, w1a_ref, w3a_ref, w2a_ref, w1b_ref, w3b_ref, w2b_ref, o_ref):
    j = pl.program_id(0)
    half = rows_ref.shape[1] - ROW_EXTRA

    @pl.when(valid_ref[j] != 0)
    def _():
        ha, hb = _unpack_bf16_pair(rows_ref[:, 0:half])
        ha = ha.astype(BF16)
        hb = hb.astype(BF16)
        gates = lax.bitcast_convert_type(rows_ref[:, half:], F32)

        def expert(w1_ref, w3_ref, w2_ref, gate):
            def up(w_ref):
                return (jnp.dot(ha, w_ref[0, 0:half, :], preferred_element_type=F32)
                        + jnp.dot(hb, w_ref[0, half:, :], preferred_element_type=F32))

            a1 = up(w1_ref)
            he = (a1 * jax.nn.sigmoid(a1)) * up(w3_ref) * gate
            return jnp.dot(he.astype(BF16), w2_ref[0], preferred_element_type=F32)

        y = (expert(w1a_ref, w3a_ref, w2a_ref, gates[:, 0:1]) + expert(w1b_ref, w3b_ref, w2b_ref, gates[:, 1:2]))
        o_ref[...] = _pack_bf16_pair(y[:, :half], y[:, half:])


def _moe_grouped(rows, tile_ea, tile_eb, tile_valid, w1, w3, w2, tmm):
    r, w = rows.shape
    n_e, de, d = w2.shape
    up_a = pl.BlockSpec((1, d, de), lambda j, ea, eb, va: (ea[j], 0, 0))
    up_b = pl.BlockSpec((1, d, de), lambda j, ea, eb, va: (eb[j], 0, 0))
    grid_spec = pltpu.PrefetchScalarGridSpec(
        num_scalar_prefetch=3,
        grid=(r // tmm,),
        in_specs=[pl.BlockSpec((tmm, w), lambda j, ea, eb, va: (j, 0)),
                  up_a, up_a, pl.BlockSpec((1, de, d), lambda j, ea, eb, va: (ea[j], 0, 0)),
                  up_b, up_b, pl.BlockSpec((1, de, d), lambda j, ea, eb, va: (eb[j], 0, 0))],
        out_specs=pl.BlockSpec((tmm, d // 2), lambda j, ea, eb, va: (j, 0)),
    )
    return pl.pallas_call(
        _moe_kernel,
        grid_spec=grid_spec,
        out_shape=jax.ShapeDtypeStruct((r, d // 2), U32),
        compiler_params=_params(("arbitrary",)),
        name="moe",
    )(tile_ea, tile_eb, tile_valid, rows, w1, w3, w2, w1, w3, w2)


def _final_kernel(x1_ref, moe_ref, mod_ref, fg_ref, *rest):
    o_ref = rest[-1]
    ya, yb = _unpack_bf16_pair(moe_ref[0])
    half = ya.shape[1]
    gate = mod_ref[0, 5:6, :]
    xa = x1_ref[0, :, 0:half] + gate[:, 0:half] * ya
    xb = x1_ref[0, :, half:] + gate[:, half:] * yb
    ms = (jnp.sum(xa * xa, axis=-1, keepdims=True) + jnp.sum(xb * xb, axis=-1, keepdims=True)) / (2 * half)
    inv = lax.rsqrt(ms + RMS_EPS)
    o_ref[0, :, 0:half] = xa * inv * fg_ref[:, 0:half]
    o_ref[0, :, half:] = xb * inv * fg_ref[:, half:]


def _final(x1, moe_tok, mods, final_g, tm, b0, bsz, out_prev):
    nb, n, d = x1.shape
    tok = lambda w: pl.BlockSpec((1, tm, w), lambda b, i: (b, i, 0))
    args = [x1, moe_tok, mods, final_g.reshape(1, d)]
    in_specs = [tok(d), tok(d // 2), pl.BlockSpec((1, N_MOD, d), lambda b, i: (b + b0, 0, 0)),
                pl.BlockSpec((1, d), lambda b, i: (0, 0))]
    aliases = {}
    if out_prev is not None:
        args.append(out_prev)
        in_specs.append(pl.BlockSpec(memory_space=pl.ANY))
        aliases = {len(args) - 1: 0}
    return pl.pallas_call(
        _final_kernel,
        grid=(nb, n // tm),
        in_specs=in_specs,
        out_specs=pl.BlockSpec((1, tm, d), lambda b, i: (b + b0, i, 0)),
        out_shape=jax.ShapeDtypeStruct((bsz, n, d), F32),
        input_output_aliases=aliases,
        compiler_params=_params(("parallel", "parallel")),
        name="final",
    )(*args)


def _tile_plan(counts, tmm, n_tiles):
    tiles = (counts + (tmm - 1)) // tmm
    tile_end = jnp.cumsum(tiles)
    offs = (tile_end - tiles) * tmm
    n_valid = tile_end[-1]
    j = jnp.arange(n_tiles, dtype=jnp.int32)
    bucket = jnp.sum((tile_end[None, :] <= jnp.minimum(j, n_valid - 1)[:, None]).astype(jnp.int32), axis=1)
    pair_lo = jnp.array([0, 0, 0, 1, 1, 2], jnp.int32)
    pair_hi = jnp.array([1, 2, 3, 2, 3, 3], jnp.int32)
    group = bucket // PAIRS_PER_GROUP
    pair = bucket % PAIRS_PER_GROUP
    tile_ea = group * EXPERTS_PER_GROUP + pair_lo[pair]
    tile_eb = group * EXPERTS_PER_GROUP + pair_hi[pair]
    return offs, tile_ea, tile_eb, (j < n_valid).astype(jnp.int32)


def kernel(x, c, ctx, c_ctx, w_mod, b_mod, norm1_g, norm2_g, w_in, s5_lambda_re, s5_lambda_im, s5_log_dt,
           s5_b_re, s5_b_im, s5_c_re, s5_c_im, s5_d, w_glu, b_glu, conv_w, w_out, router_group_w,
           router_group_b, router_expert_w, router_expert_b, expert_w1, expert_w3, expert_w2, final_g):
    assert w_mod.shape[0] == 1, "single-layer kernel"
    bsz, n_tok, d = x.shape
    n_ctx = ctx.shape[1]
    n_all = bsz * n_tok
    l = 0
    tm = min(TOKEN_TILE, n_tok)

    n_cond = bsz + 1
    pad = (-n_cond) % 8
    cond = jnp.concatenate([c, c_ctx[None, :], jnp.zeros((pad, d), F32)], axis=0)
    m = _mod_rows(cond, w_mod[l], b_mod[l])
    mx = m[:bsz].reshape(bsz, N_MOD, d)
    mc = m[bsz:bsz + 1].reshape(1, N_MOD, d)

    w_in_b = w_in[l].astype(BF16)
    w_s5 = w_in_b[:, :S5_WIDTH]
    o_c = S5_WIDTH + CONV_WIDTH
    o_v = S5_WIDTH + 2 * CONV_WIDTH
    w_conv = jnp.concatenate(
        [w_in_b[:, S5_WIDTH:o_c], w_in_b[:, o_c:o_c + CONV_ROW_WIDTH], w_in_b[:, o_v:o_v + CONV_ROW_WIDTH],
         w_in_b[:, o_c + CONV_ROW_WIDTH:o_v], w_in_b[:, o_v + CONV_ROW_WIDTH:]], axis=1)
    u = _inproj(x, mx, True, norm1_g[l], w_s5, min(INPROJ_TILE, n_tok), "inproj")
    uc = _inproj(ctx, mc, False, norm1_g[l], w_s5, min(INPROJ_TILE, n_ctx), "inproj_ctx")

    t_mat, mb_pair, mc_pair, a_rows = _s5_matrices(
        s5_lambda_re[l], s5_lambda_im[l], s5_log_dt[l], s5_b_re[l], s5_b_im[l], s5_c_re[l], s5_c_im[l], s5_d[l])
    y_c = _s5_scan(_chunkify(u, "chunkify"), _chunkify(uc, "chunkify_ctx"), t_mat, mb_pair, mc_pair, a_rows)
    y_s5 = _unchunkify(y_c, u.shape[1])

    n_logits = N_GROUPS + N_EXPERTS
    w_router = jnp.concatenate(
        [router_group_w[l], router_expert_w[l], jnp.zeros((d, ROUTER_ROWS - n_logits), F32)], axis=1).T
    w_router_hi = w_router.astype(BF16)
    w_router_lo = (w_router - w_router_hi.astype(F32)).astype(BF16)
    w_router_t = jnp.concatenate([w_router_hi, w_router_lo], axis=0)
    b_router = jnp.concatenate([router_group_b[l], router_expert_b[l], jnp.zeros((ROUTER_ROWS - n_logits,), F32)])
    tm_mix = min(MIX_TILE, n_tok)
    b_router_t = jnp.broadcast_to(b_router[:, None], (ROUTER_ROWS, min(MIX_SUB, tm_mix)))

    n_parts = MOE_PARTS if bsz % MOE_PARTS == 0 else 1
    nb = bsz // n_parts
    n_part = nb * n_tok
    n_buckets = N_GROUPS * PAIRS_PER_GROUP
    n_rows = n_part + n_buckets * MOE_TILE
    w_glu_b = w_glu[l].astype(BF16)
    w_out_b = w_out[l].astype(BF16)
    experts_f32 = (expert_w1[l], expert_w3[l], expert_w2[l])
    cast_plan = [experts_f32] if n_parts == 1 else [experts_f32[:2], experts_f32[2:]] + [()] * (n_parts - 2)
    w_experts = []
    staged = []
    for p in range(n_parts):
        x1, h2p, route_t, counts, *w_cast = _mix(x, y_s5, mx, norm1_g[l], norm2_g[l], conv_w[l], w_conv, w_glu_b,
                                                 b_glu[l], w_out_b, w_router_t, b_router_t, tm_mix, p * nb, nb,
                                                 cast_plan[p])
        w_experts += w_cast
        offs, *tiles = _tile_plan(counts[:n_buckets, 0].astype(jnp.int32), MOE_TILE, n_rows // MOE_TILE)
        offs_rows = jnp.zeros((BUCKET_ROWS,), F32).at[:n_buckets].set(offs.astype(F32))
        dest = _rank(route_t, jnp.broadcast_to(offs_rows[:, None], (BUCKET_ROWS, 128)), min(RANK_TILE, n_part))
        rows = _scatter_rows(h2p.reshape(n_part, d // 2 + ROW_EXTRA), dest, n_rows)
        staged.append((x1, rows, dest, tiles))
    out = None
    for p, (x1, rows, dest, tiles) in enumerate(staged):
        y_rows = _moe_grouped(rows, *tiles, *w_experts, MOE_TILE)
        moe_tok = _gather_rows(y_rows, dest).reshape(nb, n_tok, d // 2)
        out = _final(x1, moe_tok, mx, final_g, tm, p * nb, bsz, out)
    return out
```

```python
import functools

import jax
import jax.numpy as jnp
from jax import lax
from jax.experimental import pallas as pl
from jax.experimental.pallas import tpu as pltpu
from jax.experimental.pallas import tpu_sc as plsc

F32 = jnp.float32
BF16 = jnp.bfloat16
U32 = jnp.uint32

RMS_EPS = 1e-6
N_MOD = 6
GRID_W = 64
S5_WIDTH = 256
S5_H = 16
S5_P = 64
S5_GROUPS = S5_WIDTH // S5_H
S5_PAIRS = S5_GROUPS // 2
S5_CHUNK = 16
LANES = 128
STATE_PITCH = 136
CONV_WIDTH = 768
CONV_ROW_WIDTH = CONV_WIDTH // 2
N_GROUPS = 4
EXPERTS_PER_GROUP = 4
N_EXPERTS = N_GROUPS * EXPERTS_PER_GROUP
PAIRS_PER_GROUP = 6
ROUTER_ROWS = 32
BUCKET_ROWS = 32
ROW_EXTRA = 128
TOKEN_TILE = 1024
MOE_TILE = 512
RANK_TILE = 2048
INPROJ_TILE = 2048
MIX_TILE = 1024
MIX_SUB = 512
MOE_PARTS = 2
LAYOUT_GROUPS = 2
SC_ROWS = 32
SC_INDEX_TILE = 128
VMEM_LIMIT = 52 * 1024 * 1024


def _params(sem, vmem=VMEM_LIMIT):
    return pltpu.CompilerParams(dimension_semantics=sem, vmem_limit_bytes=vmem)


def _mod_kernel(c_ref, w_ref, b_ref, o_ref):
    c = c_ref[...]
    o_ref[...] = jnp.dot(c * jax.nn.sigmoid(c), w_ref[...], preferred_element_type=F32) + b_ref[...]


def _mod_rows(cond, w_mod, b_mod):
    n, d = cond.shape
    nout = w_mod.shape[1]
    bn = d
    return pl.pallas_call(
        _mod_kernel,
        grid=(nout // bn,),
        in_specs=[pl.BlockSpec((n, d), lambda j: (0, 0)),
                  pl.BlockSpec((d, bn), lambda j: (0, j)),
                  pl.BlockSpec((1, bn), lambda j: (0, j))],
        out_specs=pl.BlockSpec((n, bn), lambda j: (0, j)),
        out_shape=jax.ShapeDtypeStruct((n, nout), F32),
        compiler_params=_params(("arbitrary",)),
        name="mod",
    )(cond, w_mod, b_mod.reshape(1, nout))


def _modulated_norm(x, g, shift, scale):
    ms = jnp.mean(x * x, axis=-1, keepdims=True)
    return (x * lax.rsqrt(ms + RMS_EPS)) * (g * (1.0 + scale)) + shift


def _inproj_kernel(x_ref, mod_ref, g_ref, w_ref, u_ref):
    nbk, tm, d = x_ref.shape
    h = _modulated_norm(x_ref[...].reshape(nbk * tm, d), g_ref[...], mod_ref[0, 0:1, :], mod_ref[0, 1:2, :])
    u = jnp.dot(h.astype(BF16), w_ref[...], preferred_element_type=F32)
    for b in range(nbk):
        for j in range(S5_WIDTH // LANES):
            u_ref[b, j] = u[b * tm:(b + 1) * tm, j * LANES:(j + 1) * LANES]


def _inproj(x, mods, per_batch_mod, norm_g, w_s5, tile, name):
    bsz, n, d = x.shape
    tm = min(tile, n)
    nbk = 1 if per_batch_mod else max(1, min(bsz, tile // n))
    assert bsz % nbk == 0
    mod_map = (lambda b, i: (b, 0, 0)) if per_batch_mod else (lambda b, i: (0, 0, 0))
    n_slab = S5_WIDTH // LANES
    return pl.pallas_call(
        _inproj_kernel,
        grid=(bsz // nbk, n // tm),
        in_specs=[pl.BlockSpec((nbk, tm, d), lambda b, i: (b, i, 0)),
                  pl.BlockSpec((1, N_MOD, d), mod_map),
                  pl.BlockSpec((1, d), lambda b, i: (0, 0)),
                  pl.BlockSpec((d, S5_WIDTH), lambda b, i: (0, 0))],
        out_specs=pl.BlockSpec((nbk, n_slab, tm, LANES), lambda b, i: (b, 0, i, 0)),
        out_shape=jax.ShapeDtypeStruct((bsz, n_slab, n, LANES), F32),
        compiler_params=_params(("parallel", "parallel")),
        name=name,
    )(x, mods, norm_g.reshape(1, d), w_s5)


def _toeplitz_kernel(strip_ref, t_ref):
    lc = t_ref.shape[1] // S5_H
    for s in range(lc):
        off = (lc - 1 - s) * S5_H
        t_ref[0, s * S5_H:(s + 1) * S5_H, :] = strip_ref[0, :, off:off + lc * S5_H].astype(BF16)


def _toeplitz(strip):
    g_n, h_n, w = strip.shape
    n = S5_CHUNK * h_n
    return pl.pallas_call(
        _toeplitz_kernel,
        grid=(g_n,),
        in_specs=[pl.BlockSpec((1, h_n, w), lambda g: (g, 0, 0))],
        out_specs=pl.BlockSpec((1, n, n), lambda g: (g, 0, 0)),
        out_shape=jax.ShapeDtypeStruct((g_n, n, n), BF16),
        compiler_params=_params(("parallel",)),
        name="toeplitz",
    )(strip)


def _s5_matrices(lam_re, lam_im, log_dt, b_re, b_im, c_re, c_im, d_skip):
    lc, g_n, p_n, h_n = S5_CHUNK, S5_GROUPS, S5_P, S5_H
    lam = lax.complex(lam_re.astype(F32), lam_im.astype(F32))
    dt = jnp.exp(log_dt.astype(F32))[..., None]
    a_bar = jnp.exp(lam * dt)
    b_bar = ((a_bar - 1.0) / lam)[..., None] * lax.complex(b_re.astype(F32), b_im.astype(F32))
    cm = lax.complex(c_re.astype(F32), c_im.astype(F32))
    steps = jnp.arange(lc + 1, dtype=F32)
    apow = jnp.exp((lam * dt)[:, :, None, :] * steps[None, None, :, None])
    kern = jnp.einsum('dgop,dgjp,dgpi->dgjio', cm, apow[:, :, :lc], b_bar).real
    skip = jnp.eye(h_n, dtype=F32) * d_skip.astype(F32).reshape(g_n, 1, h_n)
    centre = kern[0, :, 0] + kern[1, :, 0] + skip
    lags = jnp.concatenate([kern[1, :, :0:-1], centre[:, None], kern[0, :, 1:]], axis=1)
    strip = lags.transpose(0, 2, 1, 3).reshape(g_n, h_n, (2 * lc - 1) * h_n)
    strip = jnp.pad(strip, ((0, 0), (0, 0), (0, h_n)))
    t_mat = _toeplitz(strip)

    def in_mat(pw, bb):
        return (pw[:, :, None, :] * bb.transpose(0, 2, 1)[:, None, :, :]).reshape(g_n, lc * h_n, p_n)

    mb_f = in_mat(apow[0, :, lc - 1::-1][:, :lc], b_bar[0])
    mb_b = in_mat(apow[1, :, :lc], b_bar[1])

    def out_mat(pw, cc):
        return (pw.transpose(0, 2, 1)[:, :, :, None] * cc.transpose(0, 2, 1)[:, :, None, :]).reshape(
            g_n, p_n, lc * h_n)

    mc_f = out_mat(apow[0, :, 1:lc + 1], cm[0])
    mc_b = out_mat(apow[1, :, lc:0:-1], cm[1])
    a_chunk = apow[:, :, lc]

    q_n = S5_PAIRS
    zeros_in = jnp.zeros((g_n, lc * h_n, p_n), F32)

    def pair_cols(m):
        m = m.reshape(q_n, 2, lc * h_n, p_n)
        z = zeros_in.reshape(q_n, 2, lc * h_n, p_n)[:, 0]
        top = jnp.concatenate([m[:, 0], z], axis=-1)
        bot = jnp.concatenate([z, m[:, 1]], axis=-1)
        return jnp.concatenate([top, bot], axis=1)

    mb_pair = jnp.concatenate([pair_cols(mb_f.real), pair_cols(mb_f.imag),
                               pair_cols(mb_b.real), pair_cols(mb_b.imag)], axis=-1)

    def pair_rows(m):
        m = m.reshape(q_n, 2, p_n, lc * h_n)
        z = jnp.zeros_like(m[:, 0])
        top = jnp.concatenate([m[:, 0], z], axis=-1)
        bot = jnp.concatenate([z, m[:, 1]], axis=-1)
        return jnp.concatenate([top, bot], axis=1)

    mc_pair = jnp.concatenate([pair_rows(mc_f.real), pair_rows(-mc_f.imag),
                               pair_rows(mc_b.real), pair_rows(-mc_b.imag)], axis=1)
    a_rows = jnp.stack([a_chunk[0].real, a_chunk[0].imag, a_chunk[1].real, a_chunk[1].imag], axis=0)
    a_rows = a_rows.reshape(4, q_n, 2 * p_n).transpose(1, 0, 2)
    a_rows = jnp.concatenate([a_rows, jnp.zeros_like(a_rows)], axis=1)
    return t_mat, mb_pair.astype(BF16), mc_pair.astype(BF16), a_rows


def _chunkify_kernel(u_ref, o_ref, *, gb):
    _, nb, nc, _ = o_ref.shape
    half = S5_CHUNK * S5_H
    per_slab = LANES // S5_H
    for b0 in range(0, nb, gb):
        for j in range(u_ref.shape[1]):
            cols = []
            for t in range(S5_CHUNK):
                rows = [u_ref[b0 + b, j, pl.ds(t, nc, stride=S5_CHUNK), :] for b in range(gb)]
                cols.append((rows[0] if gb == 1 else jnp.concatenate(rows, axis=0)).T)
            for gl in range(per_slab):
                g = j * per_slab + gl
                m = jnp.concatenate([c[gl * S5_H:(gl + 1) * S5_H, :] for c in cols], axis=0)
                o_ref[g // 2, b0:b0 + gb, :, (g % 2) * half:(g % 2 + 1) * half] = (
                    m.T.astype(BF16).reshape(gb, nc, half))


def _layout_step(bsz, gb):
    return gb * LAYOUT_GROUPS if bsz % (gb * LAYOUT_GROUPS) == 0 else gb


def _chunkify(u_slab, name):
    bsz, n_slab, n, _ = u_slab.shape
    nc = n // S5_CHUNK
    gb = min(bsz, max(1, LANES // nc))
    nb = _layout_step(bsz, gb)
    w = 2 * S5_CHUNK * S5_H
    return pl.pallas_call(
        functools.partial(_chunkify_kernel, gb=gb),
        grid=(bsz // nb,),
        in_specs=[pl.BlockSpec((nb, n_slab, n, LANES), lambda b: (b, 0, 0, 0))],
        out_specs=pl.BlockSpec((S5_PAIRS, nb, nc, w), lambda b: (0, b, 0, 0)),
        out_shape=jax.ShapeDtypeStruct((S5_PAIRS, bsz, nc, w), BF16),
        compiler_params=_params(("parallel",)),
        name=name,
    )(u_slab)


def _unchunkify_kernel(y_ref, o_ref):
    _, nb, nc, _ = y_ref.shape
    half = S5_CHUNK * S5_H
    per_slab = LANES // S5_H
    for b in range(nb):
        for j in range(o_ref.shape[1]):
            rows = []
            for gl in range(per_slab):
                g = j * per_slab + gl
                rows.append(y_ref[g // 2, b, :, (g % 2) * half:(g % 2 + 1) * half].astype(F32).T)
            for t in range(S5_CHUNK):
                bt = jnp.concatenate([r[t * S5_H:(t + 1) * S5_H, :] for r in rows], axis=0)
                o_ref[b, j, pl.ds(t, nc, stride=S5_CHUNK), :] = bt.T


def _unchunkify(y_c, n_slab):
    q_n, bsz, nc, w = y_c.shape
    n = nc * S5_CHUNK
    nb = _layout_step(bsz, 1)
    return pl.pallas_call(
        _unchunkify_kernel,
        grid=(bsz // nb,),
        in_specs=[pl.BlockSpec((q_n, nb, nc, w), lambda b: (0, b, 0, 0))],
        out_specs=pl.BlockSpec((nb, n_slab, n, LANES), lambda b: (b, 0, 0, 0)),
        out_shape=jax.ShapeDtypeStruct((bsz, n_slab, n, LANES), F32),
        compiler_params=_params(("parallel",)),
        name="unchunkify",
    )(y_c)


def _s5_kernel(u_ref, uc_ref, t_ref, mb_ref, mc_ref, a_ref, y_ref, s_lat, s_ctx, h_scr, *, bb):
    _, bsz, n_lat, w = u_ref.shape
    n_ctx = uc_ref.shape[2]
    n_blk = w // LANES
    rb = bb * n_lat

    mb = mb_ref[0]

    def in_lat(i, carry):
        s = jnp.dot(u_ref[0, pl.ds(i * bb, bb)].reshape(rb, w), mb, preferred_element_type=F32)
        for k in range(bb):
            r = pl.multiple_of((i * bb + k) * STATE_PITCH, 8)
            for blk in range(n_blk):
                s_lat[blk, pl.ds(r, n_lat), :] = s[k * n_lat:(k + 1) * n_lat, blk * LANES:(blk + 1) * LANES]
        return carry

    lax.fori_loop(0, bsz // bb, in_lat, 0)
    sc = jnp.dot(uc_ref[0].reshape(bsz * n_ctx, w), mb, preferred_element_type=F32)
    for blk in range(n_blk):
        s_ctx[blk] = sc[:, blk * LANES:(blk + 1) * LANES]

    a_fr, a_fi, a_br, a_bi = (a_ref[0, k:k + 1, :] for k in range(4))

    def step(h, a_r, a_i, s_r, s_i):
        h_r, h_i = h
        return a_r * h_r - a_i * h_i + s_r, a_r * h_i + a_i * h_r + s_i

    def ctx_rows(blk, c):
        return s_ctx[blk, pl.ds(c, bsz, stride=n_ctx), :]

    def lat_rows(ref, blk, c):
        return ref.at[blk, pl.ds(c, bsz, stride=STATE_PITCH), :]

    def ctx_step(k, carry):
        hf, hb = carry
        kb = n_ctx - 1 - k
        hf = step(hf, a_fr, a_fi, ctx_rows(0, k), ctx_rows(1, k))
        hb = step(hb, a_br, a_bi, ctx_rows(2, kb), ctx_rows(3, kb))
        return hf, hb

    zero = jnp.zeros((bsz, LANES), F32)
    carry = lax.fori_loop(0, n_ctx, ctx_step, ((zero, zero), (zero, zero)))

    def lat_step(k, carry):
        hf, hb = carry
        kb = n_lat - 1 - k
        lat_rows(h_scr, 0, k)[...] = hf[0]
        lat_rows(h_scr, 1, k)[...] = hf[1]
        lat_rows(h_scr, 2, kb)[...] = hb[0]
        lat_rows(h_scr, 3, kb)[...] = hb[1]
        hf = step(hf, a_fr, a_fi, lat_rows(s_lat, 0, k)[...], lat_rows(s_lat, 1, k)[...])
        hb = step(hb, a_br, a_bi, lat_rows(s_lat, 2, kb)[...], lat_rows(s_lat, 3, kb)[...])
        return hf, hb

    lax.fori_loop(0, n_lat, lat_step, carry)

    t0 = t_ref[0]
    t1 = t_ref[1]
    mc = mc_ref[0]
    half = S5_CHUNK * S5_H

    def out_lat(i, carry):
        u = u_ref[0, pl.ds(i * bb, bb)].reshape(rb, w)
        h_rows = []
        for k in range(bb):
            r = pl.multiple_of((i * bb + k) * STATE_PITCH, 8)
            h_rows.append(jnp.concatenate([h_scr[blk, pl.ds(r, n_lat), :] for blk in range(n_blk)], axis=1))
        h = jnp.concatenate(h_rows, axis=0).astype(BF16)
        inter = jnp.dot(h, mc, preferred_element_type=F32)
        y0 = jnp.dot(u[:, :half], t0, preferred_element_type=F32) + inter[:, :half]
        y1 = jnp.dot(u[:, half:], t1, preferred_element_type=F32) + inter[:, half:]
        y = jnp.concatenate([y0, y1], axis=1).astype(BF16)
        y_ref[0, pl.ds(i * bb, bb)] = y.reshape(bb, n_lat, w)
        return carry

    lax.fori_loop(0, bsz // bb, out_lat, 0)


def _s5_scan(u_c, uc_c, t_mat, mb_pair, mc_pair, a_rows):
    q_n, bsz, n_lat, w = u_c.shape
    n_ctx = uc_c.shape[2]
    assert n_lat + 8 == STATE_PITCH
    bb = min(4, bsz)
    n_blk = w // LANES
    return pl.pallas_call(
        functools.partial(_s5_kernel, bb=bb),
        grid=(q_n,),
        in_specs=[pl.BlockSpec((1, bsz, n_lat, w), lambda q: (q, 0, 0, 0)),
                  pl.BlockSpec((1, bsz, n_ctx, w), lambda q: (q, 0, 0, 0)),
                  pl.BlockSpec((2, w // 2, w // 2), lambda q: (q, 0, 0)),
                  pl.BlockSpec((1, w, w), lambda q: (q, 0, 0)),
                  pl.BlockSpec((1, w, w), lambda q: (q, 0, 0)),
                  pl.BlockSpec((1, 8, LANES), lambda q: (q, 0, 0))],
        out_specs=pl.BlockSpec((1, bsz, n_lat, w), lambda q: (q, 0, 0, 0)),
        out_shape=jax.ShapeDtypeStruct((q_n, bsz, n_lat, w), BF16),
        scratch_shapes=[pltpu.VMEM((n_blk, bsz * STATE_PITCH, LANES), F32),
                        pltpu.VMEM((n_blk, bsz * n_ctx, LANES), F32),
                        pltpu.VMEM((n_blk, bsz * STATE_PITCH, LANES), F32)],
        compiler_params=_params(("parallel",)),
        name="s5_scan",
    )(u_c, uc_c, t_mat, mb_pair, mc_pair, a_rows)


def _pack_bf16_pair(a, b):
    ua = lax.bitcast_convert_type(a.astype(BF16).astype(F32), U32)
    ub = lax.bitcast_convert_type(b.astype(BF16).astype(F32), U32)
    return ua | (ub >> 16)


def _unpack_bf16_pair(w):
    a = lax.bitcast_convert_type(w & jnp.uint32(0xFFFF0000), F32)
    b = lax.bitcast_convert_type(w << 16, F32)
    return a, b


def _first_max(rows):
    best = rows[0]
    for r in rows[1:]:
        best = jnp.maximum(best, r)
    idx = jnp.full(best.shape, float(len(rows) - 1), F32)
    for k in range(len(rows) - 2, -1, -1):
        idx = jnp.where(rows[k] == best, float(k), idx)
    return best, idx


def _route_rows(lg):
    g_rows = [lg[k:k + 1] for k in range(N_GROUPS)]
    g_max, g_idx = _first_max(g_rows)
    g_sum = sum(jnp.exp(r - g_max) for r in g_rows)
    g_p = 1.0 / g_sum
    e_rows = []
    for j in range(EXPERTS_PER_GROUP):
        r = lg[N_GROUPS + (N_GROUPS - 1) * EXPERTS_PER_GROUP + j:][:1]
        for g in range(N_GROUPS - 2, -1, -1):
            k = N_GROUPS + g * EXPERTS_PER_GROUP + j
            r = jnp.where(g_idx == float(g), lg[k:k + 1], r)
        e_rows.append(r)
    v1, i1 = _first_max(e_rows)
    rest = [jnp.where(i1 == float(j), -jnp.inf, e_rows[j]) for j in range(EXPERTS_PER_GROUP)]
    v2, i2 = _first_max(rest)
    e21 = jnp.exp(v2 - v1)
    w1 = g_p / (1.0 + e21)
    w2 = w1 * e21
    lo = jnp.minimum(i1, i2)
    hi = jnp.maximum(i1, i2)
    base = jnp.where(lo == 0.0, 0.0, jnp.where(lo == 1.0, 3.0, 5.0))
    bucket = g_idx * float(PAIRS_PER_GROUP) + base + hi - lo - 1.0
    first_is_lo = i1 < i2
    return bucket, jnp.where(first_is_lo, w1, w2), jnp.where(first_is_lo, w2, w1)


def _mix_kernel(x_ref, xup_ref, xdn_ref, y_ref, mod_ref, g1_ref, g2_ref, cw_ref, win_ref, wglu_ref, bglu_ref,
                wout_ref, wr_ref, br_ref, *rest):
    n_cast = (len(rest) - 4) // 2
    x1_ref, h2p_ref, route_ref, counts_ref = rest[n_cast:n_cast + 4]
    for src, dst in zip(rest[:n_cast], rest[n_cast + 4:]):
        dst[...] = src[...].astype(BF16)
    i = pl.program_id(1)
    tm = x_ref.shape[1]
    d = x_ref.shape[2]
    sub = min(MIX_SUB, tm)
    n_sub = tm // sub
    cw, rw = CONV_WIDTH, CONV_ROW_WIDTH

    def hidden(xv):
        return _modulated_norm(xv, g1_ref[...], mod_ref[0, 0:1, :], mod_ref[0, 1:2, :]).astype(BF16)

    def halo(h):
        zh = jnp.dot(h, win_ref[:, 2 * cw:3 * cw], preferred_element_type=F32)
        return zh[:, 0:rw] * zh[:, rw:cw]

    @pl.when(jnp.logical_and(pl.program_id(0) == 0, i == 0))
    def _():
        counts_ref[...] = jnp.zeros_like(counts_ref)

    def sub_tile(s):
        r0 = s * sub
        xv = x_ref[0, r0:r0 + sub, :]
        hx = hidden(xv)
        g = jax.nn.gelu(jnp.concatenate([y_ref[0, j, r0:r0 + sub, :] for j in range(y_ref.shape[1])], axis=1))
        hx_up = hidden(xup_ref[0]) if s == 0 else None
        hx_dn = hidden(xdn_ref[0]) if s == n_sub - 1 else None
        yield
        z_r = jnp.dot(hx, win_ref[:, cw:2 * cw], preferred_element_type=F32)
        cr = z_r[:, 0:rw] * z_r[:, rw:cw]
        z_c = jnp.dot(hx, win_ref[:, 2 * cw:3 * cw], preferred_element_type=F32)
        cc = z_c[:, 0:rw] * z_c[:, rw:cw]
        col_products[s] = cc
        bg = jnp.dot(hx, win_ref[:, 0:cw], preferred_element_type=F32)
        if s == 0:
            up_halo = jnp.where(i == 0, 0.0, halo(hx_up))
        if s == n_sub - 1:
            dn_halo = jnp.where(i == pl.num_programs(1) - 1, 0.0, halo(hx_dn))
        glu = g * jax.nn.sigmoid(jnp.dot(g.astype(BF16), wglu_ref[...], preferred_element_type=F32) + bglu_ref[...])
        yield
        row = lax.broadcasted_iota(jnp.int32, (sub, 1), 0)
        col_in_row = row % GRID_W
        left = jnp.where(col_in_row == 0, 0.0, pltpu.roll(cr, 1, axis=0))
        right = jnp.where(col_in_row == GRID_W - 1, 0.0, pltpu.roll(cr, sub - 1, axis=0))
        w_r = cw_ref[:, :rw]
        row_part = left * w_r[0:1] + cr * w_r[1:2] + right * w_r[2:3]
        if s > 0:
            up_halo = col_products[s - 1][sub - GRID_W:]
        if s < n_sub - 1:
            dn_halo = col_products[s + 1][:GRID_W]
        up = jnp.concatenate([up_halo, cc[:sub - GRID_W]], axis=0)
        dn = jnp.concatenate([cc[GRID_W:], dn_halo], axis=0)
        w_c = cw_ref[:, rw:]
        col_part = up * w_c[0:1] + cc * w_c[1:2] + dn * w_c[2:3]
        y_row = (bg[:, 0:rw] * row_part).astype(BF16)
        y_col = (bg[:, rw:cw] * col_part).astype(BF16)
        mixed = jnp.concatenate([glu.astype(BF16), y_row, y_col], axis=1)
        yx = jnp.dot(mixed, wout_ref[...], preferred_element_type=F32)
        yield
        x1 = xv + mod_ref[0, 2:3, :] * yx
        x1_ref[0, r0:r0 + sub, :] = x1
        h2 = _modulated_norm(x1, g2_ref[...], mod_ref[0, 3:4, :], mod_ref[0, 4:5, :])
        h2b = h2.astype(BF16)
        lg2 = lax.dot_general(wr_ref[...], h2b, (((1,), (1,)), ((), ())), preferred_element_type=F32)
        lg = lg2[:ROUTER_ROWS] + lg2[ROUTER_ROWS:] + br_ref[...]
        bucket, w_a, w_b = _route_rows(lg)
        r8 = lax.broadcasted_iota(jnp.int32, (8, sub), 0)
        route_ref[:, r0:r0 + sub] = jnp.where(r8 == 0, bucket, jnp.where(r8 == 1, w_a, jnp.where(r8 == 2, w_b, 0.0)))
        rl = lax.broadcasted_iota(jnp.int32, (ROW_EXTRA, sub), 0)
        gates_t = jnp.where(rl == 0, w_a, jnp.where(rl == 1, w_b, 0.0))
        h2p_ref[0, r0:r0 + sub, 0:d // 2] = _pack_bf16_pair(h2[:, :d // 2], h2[:, d // 2:])
        h2p_ref[0, r0:r0 + sub, d // 2:] = lax.bitcast_convert_type(gates_t.T, U32)
        rb = lax.broadcasted_iota(jnp.int32, (BUCKET_ROWS, sub), 0).astype(F32)
        counts_ref[...] += jnp.sum(jnp.where(rb == bucket, 1.0, 0.0), axis=-1, keepdims=True)
        yield

    n_stage = 4
    col_products = {}
    tiles = [sub_tile(s) for s in range(n_sub)]
    for step in range(n_sub + n_stage - 1):
        for s in reversed(range(n_sub)):
            if 0 <= step - s < n_stage:
                next(tiles[s])


def _mix(x, y_s5, mods, norm1_g, norm2_g, conv_w, w_conv, w_glu, b_glu, w_out, w_router_t, b_router_t, tm, b0, nb,
         to_cast):
    _, n, d = x.shape
    nt = n // tm
    n_steps = nb * nt
    cast_specs = []
    for w in to_cast:
        per = -(-w.shape[0] // n_steps)
        n_blk = w.shape[0] // per
        assert n_blk * per == w.shape[0]
        cast_specs.append(pl.BlockSpec((per,) + w.shape[1:],
                                       lambda b, i, n_blk=n_blk: ((b * nt + i) * n_blk // n_steps, 0, 0)))
    halo_blocks = n // GRID_W
    per_tile = tm // GRID_W
    tok_out = lambda w: pl.BlockSpec((1, tm, w), lambda b, i: (b, i, 0))
    full = lambda a: pl.BlockSpec(a.shape, lambda b, i: (0,) * a.ndim)
    args = (x, x, x, y_s5, mods, norm1_g.reshape(1, d), norm2_g.reshape(1, d), conv_w, w_conv, w_glu,
            b_glu.reshape(1, -1), w_out, w_router_t, b_router_t)
    in_specs = [pl.BlockSpec((1, tm, d), lambda b, i: (b + b0, i, 0)),
                pl.BlockSpec((1, GRID_W, d), lambda b, i: (b + b0, jnp.maximum(i * per_tile - 1, 0), 0)),
                pl.BlockSpec((1, GRID_W, d),
                             lambda b, i: (b + b0, jnp.minimum((i + 1) * per_tile, halo_blocks - 1), 0)),
                pl.BlockSpec((1, y_s5.shape[1], tm, LANES), lambda b, i: (b + b0, 0, i, 0)),
                pl.BlockSpec((1, N_MOD, d), lambda b, i: (b + b0, 0, 0))] + [full(a) for a in args[5:]]
    return pl.pallas_call(
        _mix_kernel,
        grid=(nb, nt),
        in_specs=in_specs + cast_specs,
        out_specs=[tok_out(d), tok_out(d // 2 + ROW_EXTRA),
                   pl.BlockSpec((8, tm), lambda b, i: (0, b * nt + i)),
                   pl.BlockSpec((BUCKET_ROWS, 128), lambda b, i: (0, 0))] + cast_specs,
        out_shape=[jax.ShapeDtypeStruct((nb, n, d), F32),
                   jax.ShapeDtypeStruct((nb, n, d // 2 + ROW_EXTRA), U32),
                   jax.ShapeDtypeStruct((8, nb * n), F32),
                   jax.ShapeDtypeStruct((BUCKET_ROWS, 128), F32)]
        + [jax.ShapeDtypeStruct(w.shape, BF16) for w in to_cast],
        compiler_params=_params(("arbitrary", "arbitrary")),
        name="mix",
    )(*args, *to_cast)


def _rank_kernel(route_ref, offs_ref, dest_ref, run_ref, tri_ref):
    tr = route_ref.shape[1]

    @pl.when(pl.program_id(0) == 0)
    def _():
        run_ref[...] = jnp.zeros_like(run_ref)
        s_idx = lax.broadcasted_iota(jnp.int32, (tr, tr), 0)
        t_idx = lax.broadcasted_iota(jnp.int32, (tr, tr), 1)
        tri_ref[...] = jnp.where(s_idx <= t_idx, 1.0, 0.0).astype(BF16)

    bucket = route_ref[0:1, :]
    rb = lax.broadcasted_iota(jnp.int32, (BUCKET_ROWS, tr), 0).astype(F32)
    onehot = jnp.where(rb == bucket, 1.0, 0.0)
    prefix = jnp.dot(onehot.astype(BF16), tri_ref[...], preferred_element_type=F32)
    before = run_ref[:, 0:1] + offs_ref[:, 0:1]
    dest = jnp.sum(onehot * (prefix - 1.0 + before), axis=0, keepdims=True)
    dest_ref[...] = dest.astype(jnp.int32)
    run_ref[...] += prefix[:, tr - 1:tr]


def _rank(route_t, offs_rows, tr):
    n = route_t.shape[1]
    return pl.pallas_call(
        _rank_kernel,
        grid=(n // tr,),
        in_specs=[pl.BlockSpec((8, tr), lambda i: (0, i)),
                  pl.BlockSpec((BUCKET_ROWS, 128), lambda i: (0, 0))],
        out_specs=pl.BlockSpec((1, tr), lambda i: (0, i)),
        out_shape=jax.ShapeDtypeStruct((1, n), jnp.int32),
        scratch_shapes=[pltpu.VMEM((BUCKET_ROWS, 128), F32), pltpu.VMEM((tr, tr), BF16)],
        compiler_params=_params(("arbitrary",)),
        name="rank",
    )(route_t, offs_rows)


def _sc_mesh():
    return plsc.VectorSubcoreMesh(core_axis_name="core", subcore_axis_name="subcore")


def _scatter_rows(src, dest, n_out):
    n, w = src.shape
    sub = SC_INDEX_TILE // SC_ROWS

    @functools.partial(pl.kernel, out_type=jax.ShapeDtypeStruct((n_out, w), src.dtype), mesh=_sc_mesh(),
                       scratch_types=[], name="scatter_rows")
    def scatter(x_hbm, i_hbm, o_hbm):
        def body(x_vmem, i_vmem):
            j = pl.program_id(1)
            pltpu.sync_copy(x_vmem, o_hbm.at[i_vmem.at[0, pl.ds(j * SC_ROWS, SC_ROWS)]])

        pltpu.emit_pipeline(
            body,
            grid=(n // SC_INDEX_TILE, sub),
            in_specs=[pl.BlockSpec((SC_ROWS, w), lambda i, j: (i * sub + j, 0)),
                      pl.BlockSpec((1, SC_INDEX_TILE), lambda i, j: (0, i))],
            out_specs=[],
            core_axis_name=("core", "subcore"),
            dimension_semantics=(pltpu.PARALLEL, pltpu.ARBITRARY),
        )(x_hbm, i_hbm)

    return scatter(src, dest)


def _gather_rows(src, idx):
    n = idx.shape[1]
    w = src.shape[1]
    sub = SC_INDEX_TILE // SC_ROWS

    @functools.partial(pl.kernel, out_type=jax.ShapeDtypeStruct((n, w), src.dtype), mesh=_sc_mesh(),
                       scratch_types=[], name="gather_rows")
    def gather(x_hbm, i_hbm, o_hbm):
        def body(i_vmem, o_vmem):
            j = pl.program_id(1)
            pltpu.sync_copy(x_hbm.at[i_vmem.at[0, pl.ds(j * SC_ROWS, SC_ROWS)]], o_vmem)

        pltpu.emit_pipeline(
            body,
            grid=(n // SC_INDEX_TILE, sub),
            in_specs=[pl.BlockSpec((1, SC_INDEX_TILE), lambda i, j: (0, i))],
            out_specs=[pl.BlockSpec((SC_ROWS, w), lambda i, j: (i * sub + j, 0))],
            core_axis_name=("core", "subcore"),
            dimension_semantics=(pltpu.PARALLEL, pltpu.ARBITRARY),
        )(i_hbm, o_hbm)

    return gather(src, idx)


def _moe_kernel(ea_ref, eb_ref, valid_ref, rows_ref, w1a_ref, w3a_ref, w2a_ref, w1b_ref, w3b_ref, w2b_ref, o_ref):
    j = pl.program_id(0)
    half = rows_ref.shape[1] - ROW_EXTRA

    @pl.when(valid_ref[j] != 0)
    def _():
        ha, hb = _unpack_bf16_pair(rows_ref[:, 0:half])
        ha = ha.astype(BF16)
        hb = hb.astype(BF16)
        gates = lax.bitcast_convert_type(rows_ref[:, half:], F32)

        def expert(w1_ref, w3_ref, w2_ref, gate):
            def up(w_ref):
                return (jnp.dot(ha, w_ref[0, 0:half, :], preferred_element_type=F32)
                        + jnp.dot(hb, w_ref[0, half:, :], preferred_element_type=F32))

            a1 = up(w1_ref)
            he = (a1 * jax.nn.sigmoid(a1)) * up(w3_ref) * gate
            return jnp.dot(he.astype(BF16), w2_ref[0], preferred_element_type=F32)

        y = (expert(w1a_ref, w3a_ref, w2a_ref, gates[:, 0:1]) + expert(w1b_ref, w3b_ref, w2b_ref, gates[:, 1:2]))
        o_ref[...] = _pack_bf16_pair(y[:, :half], y[:, half:])


def _moe_grouped(rows, tile_ea, tile_eb, tile_valid, w1, w3, w2, tmm):
    r, w = rows.shape
    n_e, de, d = w2.shape
    up_a = pl.BlockSpec((1, d, de), lambda j, ea, eb, va: (ea[j], 0, 0))
    up_b = pl.BlockSpec((1, d, de), lambda j, ea, eb, va: (eb[j], 0, 0))
    grid_spec = pltpu.PrefetchScalarGridSpec(
        num_scalar_prefetch=3,
        grid=(r // tmm,),
        in_specs=[pl.BlockSpec((tmm, w), lambda j, ea, eb, va: (j, 0)),
                  up_a, up_a, pl.BlockSpec((1, de, d), lambda j, ea, eb, va: (ea[j], 0, 0)),
                  up_b, up_b, pl.BlockSpec((1, de, d), lambda j, ea, eb, va: (eb[j], 0, 0))],
        out_specs=pl.BlockSpec((tmm, d // 2), lambda j, ea, eb, va: (j, 0)),
    )
    return pl.pallas_call(
        _moe_kernel,
        grid_spec=grid_spec,
        out_shape=jax.ShapeDtypeStruct((r, d // 2), U32),
        compiler_params=_params(("arbitrary",)),
        name="moe",
    )(tile_ea, tile_eb, tile_valid, rows, w1, w3, w2, w1, w3, w2)


def _final_kernel(x1_ref, moe_ref, mod_ref, fg_ref, *rest):
    o_ref = rest[-1]
    ya, yb = _unpack_bf16_pair(moe_ref[0])
    half = ya.shape[1]
    gate = mod_ref[0, 5:6, :]
    xa = x1_ref[0, :, 0:half] + gate[:, 0:half] * ya
    xb = x1_ref[0, :, half:] + gate[:, half:] * yb
    ms = (jnp.sum(xa * xa, axis=-1, keepdims=True) + jnp.sum(xb * xb, axis=-1, keepdims=True)) / (2 * half)
    inv = lax.rsqrt(ms + RMS_EPS)
    o_ref[0, :, 0:half] = xa * inv * fg_ref[:, 0:half]
    o_ref[0, :, half:] = xb * inv * fg_ref[:, half:]


def _final(x1, moe_tok, mods, final_g, tm, b0, bsz, out_prev):
    nb, n, d = x1.shape
    tok = lambda w: pl.BlockSpec((1, tm, w), lambda b, i: (b, i, 0))
    args = [x1, moe_tok, mods, final_g.reshape(1, d)]
    in_specs = [tok(d), tok(d // 2), pl.BlockSpec((1, N_MOD, d), lambda b, i: (b + b0, 0, 0)),
                pl.BlockSpec((1, d), lambda b, i: (0, 0))]
    aliases = {}
    if out_prev is not None:
        args.append(out_prev)
        in_specs.append(pl.BlockSpec(memory_space=pl.ANY))
        aliases = {len(args) - 1: 0}
    return pl.pallas_call(
        _final_kernel,
        grid=(nb, n // tm),
        in_specs=in_specs,
        out_specs=pl.BlockSpec((1, tm, d), lambda b, i: (b + b0, i, 0)),
        out_shape=jax.ShapeDtypeStruct((bsz, n, d), F32),
        input_output_aliases=aliases,
        compiler_params=_params(("parallel", "parallel")),
        name="final",
    )(*args)


def _tile_plan(counts, tmm, n_tiles):
    tiles = (counts + (tmm - 1)) // tmm
    tile_end = jnp.cumsum(tiles)
    offs = (tile_end - tiles) * tmm
    n_valid = tile_end[-1]
    j = jnp.arange(n_tiles, dtype=jnp.int32)
    bucket = jnp.sum((tile_end[None, :] <= jnp.minimum(j, n_valid - 1)[:, None]).astype(jnp.int32), axis=1)
    pair_lo = jnp.array([0, 0, 0, 1, 1, 2], jnp.int32)
    pair_hi = jnp.array([1, 2, 3, 2, 3, 3], jnp.int32)
    group = bucket // PAIRS_PER_GROUP
    pair = bucket % PAIRS_PER_GROUP
    tile_ea = group * EXPERTS_PER_GROUP + pair_lo[pair]
    tile_eb = group * EXPERTS_PER_GROUP + pair_hi[pair]
    return offs, tile_ea, tile_eb, (j < n_valid).astype(jnp.int32)


def kernel(x, c, ctx, c_ctx, w_mod, b_mod, norm1_g, norm2_g, w_in, s5_lambda_re, s5_lambda_im, s5_log_dt,
           s5_b_re, s5_b_im, s5_c_re, s5_c_im, s5_d, w_glu, b_glu, conv_w, w_out, router_group_w,
           router_group_b, router_expert_w, router_expert_b, expert_w1, expert_w3, expert_w2, final_g):
    assert w_mod.shape[0] == 1, "single-layer kernel"
    bsz, n_tok, d = x.shape
    n_ctx = ctx.shape[1]
    n_all = bsz * n_tok
    l = 0
    tm = min(TOKEN_TILE, n_tok)

    n_cond = bsz + 1
    pad = (-n_cond) % 8
    cond = jnp.concatenate([c, c_ctx[None, :], jnp.zeros((pad, d), F32)], axis=0)
    m = _mod_rows(cond, w_mod[l], b_mod[l])
    mx = m[:bsz].reshape(bsz, N_MOD, d)
    mc = m[bsz:bsz + 1].reshape(1, N_MOD, d)

    w_in_b = w_in[l].astype(BF16)
    w_s5 = w_in_b[:, :S5_WIDTH]
    o_c = S5_WIDTH + CONV_WIDTH
    o_v = S5_WIDTH + 2 * CONV_WIDTH
    w_conv = jnp.concatenate(
        [w_in_b[:, S5_WIDTH:o_c], w_in_b[:, o_c:o_c + CONV_ROW_WIDTH], w_in_b[:, o_v:o_v + CONV_ROW_WIDTH],
         w_in_b[:, o_c + CONV_ROW_WIDTH:o_v], w_in_b[:, o_v + CONV_ROW_WIDTH:]], axis=1)
    u = _inproj(x, mx, True, norm1_g[l], w_s5, INPROJ_TILE, "inproj")
    uc = _inproj(ctx, mc, False, norm1_g[l], w_s5, INPROJ_TILE, "inproj_ctx")

    t_mat, mb_pair, mc_pair, a_rows = _s5_matrices(
        s5_lambda_re[l], s5_lambda_im[l], s5_log_dt[l], s5_b_re[l], s5_b_im[l], s5_c_re[l], s5_c_im[l], s5_d[l])
    y_c = _s5_scan(_chunkify(u, "chunkify"), _chunkify(uc, "chunkify_ctx"), t_mat, mb_pair, mc_pair, a_rows)
    y_s5 = _unchunkify(y_c, u.shape[1])

    n_logits = N_GROUPS + N_EXPERTS
    w_router = jnp.concatenate(
        [router_group_w[l], router_expert_w[l], jnp.zeros((d, ROUTER_ROWS - n_logits), F32)], axis=1).T
    w_router_hi = w_router.astype(BF16)
    w_router_lo = (w_router - w_router_hi.astype(F32)).astype(BF16)
    w_router_t = jnp.concatenate([w_router_hi, w_router_lo], axis=0)
    b_router = jnp.concatenate([router_group_b[l], router_expert_b[l], jnp.zeros((ROUTER_ROWS - n_logits,), F32)])
    tm_mix = min(MIX_TILE, n_tok)
    b_router_t = jnp.broadcast_to(b_router[:, None], (ROUTER_ROWS, min(MIX_SUB, tm_mix)))

    n_parts = MOE_PARTS if bsz % MOE_PARTS == 0 else 1
    nb = bsz // n_parts
    n_part = nb * n_tok
    n_buckets = N_GROUPS * PAIRS_PER_GROUP
    n_rows = n_part + n_buckets * MOE_TILE
    w_glu_b = w_glu[l].astype(BF16)
    w_out_b = w_out[l].astype(BF16)
    experts_f32 = (expert_w1[l], expert_w3[l], expert_w2[l])
    cast_plan = [experts_f32] if n_parts == 1 else [experts_f32[:2], experts_f32[2:]] + [()] * (n_parts - 2)
    w_experts = []
    staged = []
    for p in range(n_parts):
        x1, h2p, route_t, counts, *w_cast = _mix(x, y_s5, mx, norm1_g[l], norm2_g[l], conv_w[l], w_conv, w_glu_b,
                                                 b_glu[l], w_out_b, w_router_t, b_router_t, tm_mix, p * nb, nb,
                                                 cast_plan[p])
        w_experts += w_cast
        offs, *tiles = _tile_plan(counts[:n_buckets, 0].astype(jnp.int32), MOE_TILE, n_rows // MOE_TILE)
        offs_rows = jnp.zeros((BUCKET_ROWS,), F32).at[:n_buckets].set(offs.astype(F32))
        dest = _rank(route_t, jnp.broadcast_to(offs_rows[:, None], (BUCKET_ROWS, 128)), min(RANK_TILE, n_part))
        rows = _scatter_rows(h2p.reshape(n_part, d // 2 + ROW_EXTRA), dest, n_rows)
        staged.append((x1, rows, dest, tiles))
    out = None
    for p, (x1, rows, dest, tiles) in enumerate(staged):
        y_rows = _moe_grouped(rows, *tiles, *w_experts, MOE_TILE)
        moe_tok = _gather_rows(y_rows, dest).reshape(nb, n_tok, d // 2)
        out = _final(x1, moe_tok, mx, final_g, tm, p * nb, bsz, out)
    return out
```

```python
import functools

import jax
import jax.numpy as jnp
from jax import lax
from jax.experimental import pallas as pl
from jax.experimental.pallas import tpu as pltpu
from jax.experimental.pallas import tpu_sc as plsc

F32 = jnp.float32
BF16 = jnp.bfloat16
U32 = jnp.uint32

RMS_EPS = 1e-6
N_MOD = 6
GRID_W = 64
S5_WIDTH = 256
S5_H = 16
S5_P = 64
S5_GROUPS = S5_WIDTH // S5_H
S5_PAIRS = S5_GROUPS // 2
S5_CHUNK = 16
LANES = 128
STATE_PITCH = 136
CONV_WIDTH = 768
CONV_ROW_WIDTH = CONV_WIDTH // 2
N_GROUPS = 4
EXPERTS_PER_GROUP = 4
N_EXPERTS = N_GROUPS * EXPERTS_PER_GROUP
PAIRS_PER_GROUP = 6
ROUTER_ROWS = 32
BUCKET_ROWS = 32
ROW_EXTRA = 128
TOKEN_TILE = 1024
MOE_TILE = 512
RANK_TILE = 2048
INPROJ_TILE = 2048
MIX_TILE = 1024
MIX_SUB = 512
MOE_PARTS = 2
LAYOUT_GROUPS = 2
SC_ROWS = 32
SC_INDEX_TILE = 128
VMEM_LIMIT = 52 * 1024 * 1024


def _params(sem, vmem=VMEM_LIMIT):
    return pltpu.CompilerParams(dimension_semantics=sem, vmem_limit_bytes=vmem)


def _mod_kernel(c_ref, w_ref, b_ref, o_ref):
    c = c_ref[...]
    o_ref[...] = jnp.dot(c * jax.nn.sigmoid(c), w_ref[...], preferred_element_type=F32) + b_ref[...]


def _mod_rows(cond, w_mod, b_mod):
    n, d = cond.shape
    nout = w_mod.shape[1]
    bn = d
    return pl.pallas_call(
        _mod_kernel,
        grid=(nout // bn,),
        in_specs=[pl.BlockSpec((n, d), lambda j: (0, 0)),
                  pl.BlockSpec((d, bn), lambda j: (0, j)),
                  pl.BlockSpec((1, bn), lambda j: (0, j))],
        out_specs=pl.BlockSpec((n, bn), lambda j: (0, j)),
        out_shape=jax.ShapeDtypeStruct((n, nout), F32),
        compiler_params=_params(("arbitrary",)),
        name="mod",
    )(cond, w_mod, b_mod.reshape(1, nout))


def _modulated_norm(x, g, shift, scale):
    ms = jnp.mean(x * x, axis=-1, keepdims=True)
    return (x * lax.rsqrt(ms + RMS_EPS)) * (g * (1.0 + scale)) + shift


def _inproj_kernel(x_ref, mod_ref, g_ref, w_ref, u_ref):
    nbk, tm, d = x_ref.shape
    h = _modulated_norm(x_ref[...].reshape(nbk * tm, d), g_ref[...], mod_ref[0, 0:1, :], mod_ref[0, 1:2, :])
    u = jnp.dot(h.astype(BF16), w_ref[...], preferred_element_type=F32)
    for b in range(nbk):
        for j in range(S5_WIDTH // LANES):
            u_ref[b, j] = u[b * tm:(b + 1) * tm, j * LANES:(j + 1) * LANES]


def _inproj(x, mods, per_batch_mod, norm_g, w_s5, tile, name):
    bsz, n, d = x.shape
    tm = min(tile, n)
    nbk = 1 if per_batch_mod else max(1, min(bsz, tile // n))
    assert bsz % nbk == 0
    mod_map = (lambda b, i: (b, 0, 0)) if per_batch_mod else (lambda b, i: (0, 0, 0))
    n_slab = S5_WIDTH // LANES
    return pl.pallas_call(
        _inproj_kernel,
        grid=(bsz // nbk, n // tm),
        in_specs=[pl.BlockSpec((nbk, tm, d), lambda b, i: (b, i, 0)),
                  pl.BlockSpec((1, N_MOD, d), mod_map),
                  pl.BlockSpec((1, d), lambda b, i: (0, 0)),
                  pl.BlockSpec((d, S5_WIDTH), lambda b, i: (0, 0))],
        out_specs=pl.BlockSpec((nbk, n_slab, tm, LANES), lambda b, i: (b, 0, i, 0)),
        out_shape=jax.ShapeDtypeStruct((bsz, n_slab, n, LANES), F32),
        compiler_params=_params(("parallel", "parallel")),
        name=name,
    )(x, mods, norm_g.reshape(1, d), w_s5)


def _toeplitz_kernel(strip_ref, t_ref):
    lc = t_ref.shape[1] // S5_H
    for s in range(lc):
        off = (lc - 1 - s) * S5_H
        t_ref[0, s * S5_H:(s + 1) * S5_H, :] = strip_ref[0, :, off:off + lc * S5_H].astype(BF16)


def _toeplitz(strip):
    g_n, h_n, w = strip.shape
    n = S5_CHUNK * h_n
    return pl.pallas_call(
        _toeplitz_kernel,
        grid=(g_n,),
        in_specs=[pl.BlockSpec((1, h_n, w), lambda g: (g, 0, 0))],
        out_specs=pl.BlockSpec((1, n, n), lambda g: (g, 0, 0)),
        out_shape=jax.ShapeDtypeStruct((g_n, n, n), BF16),
        compiler_params=_params(("parallel",)),
        name="toeplitz",
    )(strip)


def _s5_matrices(lam_re, lam_im, log_dt, b_re, b_im, c_re, c_im, d_skip):
    lc, g_n, p_n, h_n = S5_CHUNK, S5_GROUPS, S5_P, S5_H
    lam = lax.complex(lam_re.astype(F32), lam_im.astype(F32))
    dt = jnp.exp(log_dt.astype(F32))[..., None]
    a_bar = jnp.exp(lam * dt)
    b_bar = ((a_bar - 1.0) / lam)[..., None] * lax.complex(b_re.astype(F32), b_im.astype(F32))
    cm = lax.complex(c_re.astype(F32), c_im.astype(F32))
    steps = jnp.arange(lc + 1, dtype=F32)
    apow = jnp.exp((lam * dt)[:, :, None, :] * steps[None, None, :, None])
    kern = jnp.einsum('dgop,dgjp,dgpi->dgjio', cm, apow[:, :, :lc], b_bar).real
    skip = jnp.eye(h_n, dtype=F32) * d_skip.astype(F32).reshape(g_n, 1, h_n)
    centre = kern[0, :, 0] + kern[1, :, 0] + skip
    lags = jnp.concatenate([kern[1, :, :0:-1], centre[:, None], kern[0, :, 1:]], axis=1)
    strip = lags.transpose(0, 2, 1, 3).reshape(g_n, h_n, (2 * lc - 1) * h_n)
    strip = jnp.pad(strip, ((0, 0), (0, 0), (0, h_n)))
    t_mat = _toeplitz(strip)

    q_n = S5_PAIRS
    same_group = jnp.eye(2, dtype=F32)
    pw_in = jnp.stack([apow[0, :, lc - 1::-1], apow[1, :, :lc]])
    mb = jnp.einsum('dgsp,dgpi->dgsip', pw_in, b_bar)
    mb = jnp.stack([mb.real, mb.imag], axis=1).reshape(2, 2, q_n, 2, lc, h_n, p_n)
    mb_pair = jnp.einsum('dcqgsip,gh->qgsidchp', mb, same_group).reshape(q_n, 2 * lc * h_n, 8 * p_n)
    pw_out = jnp.stack([apow[0, :, 1:lc + 1], apow[1, :, lc:0:-1]])
    mc = jnp.einsum('dgtp,dgop->dgpto', pw_out, cm)
    mc = jnp.stack([mc.real, -mc.imag], axis=1).reshape(2, 2, q_n, 2, p_n, lc, h_n)
    mc_pair = jnp.einsum('dcqhpto,hg->qdchpgto', mc, same_group).reshape(q_n, 8 * p_n, 2 * lc * h_n)
    a_chunk = apow[:, :, lc]
    a_rows = jnp.stack([a_chunk.real, a_chunk.imag], axis=1).reshape(4, q_n, 2 * p_n).transpose(1, 0, 2)
    a_rows = jnp.concatenate([a_rows, jnp.zeros_like(a_rows)], axis=1)
    return t_mat, mb_pair.astype(BF16), mc_pair.astype(BF16), a_rows


def _chunkify_kernel(u_ref, o_ref, *, gb):
    _, nb, nc, _ = o_ref.shape
    half = S5_CHUNK * S5_H
    per_slab = LANES // S5_H
    for b0 in range(0, nb, gb):
        for j in range(u_ref.shape[1]):
            cols = []
            for t in range(S5_CHUNK):
                rows = [u_ref[b0 + b, j, pl.ds(t, nc, stride=S5_CHUNK), :] for b in range(gb)]
                cols.append((rows[0] if gb == 1 else jnp.concatenate(rows, axis=0)).T)
            for gl in range(per_slab):
                g = j * per_slab + gl
                m = jnp.concatenate([c[gl * S5_H:(gl + 1) * S5_H, :] for c in cols], axis=0)
                o_ref[g // 2, b0:b0 + gb, :, (g % 2) * half:(g % 2 + 1) * half] = (
                    m.T.astype(BF16).reshape(gb, nc, half))


def _layout_step(bsz, gb):
    return gb * LAYOUT_GROUPS if bsz % (gb * LAYOUT_GROUPS) == 0 else gb


def _chunkify(u_slab, name):
    bsz, n_slab, n, _ = u_slab.shape
    nc = n // S5_CHUNK
    gb = min(bsz, max(1, LANES // nc))
    nb = _layout_step(bsz, gb)
    w = 2 * S5_CHUNK * S5_H
    return pl.pallas_call(
        functools.partial(_chunkify_kernel, gb=gb),
        grid=(bsz // nb,),
        in_specs=[pl.BlockSpec((nb, n_slab, n, LANES), lambda b: (b, 0, 0, 0))],
        out_specs=pl.BlockSpec((S5_PAIRS, nb, nc, w), lambda b: (0, b, 0, 0)),
        out_shape=jax.ShapeDtypeStruct((S5_PAIRS, bsz, nc, w), BF16),
        compiler_params=_params(("parallel",)),
        name=name,
    )(u_slab)


def _unchunkify_kernel(y_ref, o_ref):
    _, nb, nc, _ = y_ref.shape
    half = S5_CHUNK * S5_H
    per_slab = LANES // S5_H
    for b in range(nb):
        for j in range(o_ref.shape[1]):
            rows = []
            for gl in range(per_slab):
                g = j * per_slab + gl
                rows.append(y_ref[g // 2, b, :, (g % 2) * half:(g % 2 + 1) * half].astype(F32).T)
            for t in range(S5_CHUNK):
                bt = jnp.concatenate([r[t * S5_H:(t + 1) * S5_H, :] for r in rows], axis=0)
                o_ref[b, j, pl.ds(t, nc, stride=S5_CHUNK), :] = bt.T


def _unchunkify(y_c, n_slab):
    q_n, bsz, nc, w = y_c.shape
    n = nc * S5_CHUNK
    nb = _layout_step(bsz, 1)
    return pl.pallas_call(
        _unchunkify_kernel,
        grid=(bsz // nb,),
        in_specs=[pl.BlockSpec((q_n, nb, nc, w), lambda b: (0, b, 0, 0))],
        out_specs=pl.BlockSpec((nb, n_slab, n, LANES), lambda b: (b, 0, 0, 0)),
        out_shape=jax.ShapeDtypeStruct((bsz, n_slab, n, LANES), F32),
        compiler_params=_params(("parallel",)),
        name="unchunkify",
    )(y_c)


def _s5_kernel(u_ref, uc_ref, t_ref, mb_ref, mc_ref, a_ref, y_ref, s_lat, s_ctx, h_scr, *, bb):
    _, bsz, n_lat, w = u_ref.shape
    n_ctx = uc_ref.shape[2]
    n_blk = w // LANES
    rb = bb * n_lat

    mb = mb_ref[0]

    def in_lat(i, carry):
        s = jnp.dot(u_ref[0, pl.ds(i * bb, bb)].reshape(rb, w), mb, preferred_element_type=F32)
        for k in range(bb):
            r = pl.multiple_of((i * bb + k) * STATE_PITCH, 8)
            for blk in range(n_blk):
                s_lat[blk, pl.ds(r, n_lat), :] = s[k * n_lat:(k + 1) * n_lat, blk * LANES:(blk + 1) * LANES]
        return carry

    lax.fori_loop(0, bsz // bb, in_lat, 0)
    sc = jnp.dot(uc_ref[0].reshape(bsz * n_ctx, w), mb, preferred_element_type=F32)
    for blk in range(n_blk):
        s_ctx[blk] = sc[:, blk * LANES:(blk + 1) * LANES]

    a_fr, a_fi, a_br, a_bi = (a_ref[0, k:k + 1, :] for k in range(4))

    def step(h, a_r, a_i, s_r, s_i):
        h_r, h_i = h
        return a_r * h_r - a_i * h_i + s_r, a_r * h_i + a_i * h_r + s_i

    def ctx_rows(blk, c):
        return s_ctx[blk, pl.ds(c, bsz, stride=n_ctx), :]

    def lat_rows(ref, blk, c):
        return ref.at[blk, pl.ds(c, bsz, stride=STATE_PITCH), :]

    def ctx_step(k, carry):
        hf, hb = carry
        kb = n_ctx - 1 - k
        hf = step(hf, a_fr, a_fi, ctx_rows(0, k), ctx_rows(1, k))
        hb = step(hb, a_br, a_bi, ctx_rows(2, kb), ctx_rows(3, kb))
        return hf, hb

    zero = jnp.zeros((bsz, LANES), F32)
    carry = lax.fori_loop(0, n_ctx, ctx_step, ((zero, zero), (zero, zero)))

    def lat_step(k, carry):
        hf, hb = carry
        kb = n_lat - 1 - k
        lat_rows(h_scr, 0, k)[...] = hf[0]
        lat_rows(h_scr, 1, k)[...] = hf[1]
        lat_rows(h_scr, 2, kb)[...] = hb[0]
        lat_rows(h_scr, 3, kb)[...] = hb[1]
        hf = step(hf, a_fr, a_fi, lat_rows(s_lat, 0, k)[...], lat_rows(s_lat, 1, k)[...])
        hb = step(hb, a_br, a_bi, lat_rows(s_lat, 2, kb)[...], lat_rows(s_lat, 3, kb)[...])
        return hf, hb

    lax.fori_loop(0, n_lat, lat_step, carry)

    t0 = t_ref[0]
    t1 = t_ref[1]
    mc = mc_ref[0]
    half = S5_CHUNK * S5_H

    def out_lat(i, carry):
        u = u_ref[0, pl.ds(i * bb, bb)].reshape(rb, w)
        h_rows = []
        for k in range(bb):
            r = pl.multiple_of((i * bb + k) * STATE_PITCH, 8)
            h_rows.append(jnp.concatenate([h_scr[blk, pl.ds(r, n_lat), :] for blk in range(n_blk)], axis=1))
        h = jnp.concatenate(h_rows, axis=0).astype(BF16)
        inter = jnp.dot(h, mc, preferred_element_type=F32)
        y0 = jnp.dot(u[:, :half], t0, preferred_element_type=F32) + inter[:, :half]
        y1 = jnp.dot(u[:, half:], t1, preferred_element_type=F32) + inter[:, half:]
        y = jnp.concatenate([y0, y1], axis=1).astype(BF16)
        y_ref[0, pl.ds(i * bb, bb)] = y.reshape(bb, n_lat, w)
        return carry

    lax.fori_loop(0, bsz // bb, out_lat, 0)


def _s5_scan(u_c, uc_c, t_mat, mb_pair, mc_pair, a_rows):
    q_n, bsz, n_lat, w = u_c.shape
    n_ctx = uc_c.shape[2]
    assert n_lat + 8 == STATE_PITCH
    bb = min(4, bsz)
    n_blk = w // LANES
    return pl.pallas_call(
        functools.partial(_s5_kernel, bb=bb),
        grid=(q_n,),
        in_specs=[pl.BlockSpec((1, bsz, n_lat, w), lambda q: (q, 0, 0, 0)),
                  pl.BlockSpec((1, bsz, n_ctx, w), lambda q: (q, 0, 0, 0)),
                  pl.BlockSpec((2, w // 2, w // 2), lambda q: (q, 0, 0)),
                  pl.BlockSpec((1, w, w), lambda q: (q, 0, 0)),
                  pl.BlockSpec((1, w, w), lambda q: (q, 0, 0)),
                  pl.BlockSpec((1, 8, LANES), lambda q: (q, 0, 0))],
        out_specs=pl.BlockSpec((1, bsz, n_lat, w), lambda q: (q, 0, 0, 0)),
        out_shape=jax.ShapeDtypeStruct((q_n, bsz, n_lat, w), BF16),
        scratch_shapes=[pltpu.VMEM((n_blk, bsz * STATE_PITCH, LANES), F32),
                        pltpu.VMEM((n_blk, bsz * n_ctx, LANES), F32),
                        pltpu.VMEM((n_blk, bsz * STATE_PITCH, LANES), F32)],
        compiler_params=_params(("parallel",)),
        name="s5_scan",
    )(u_c, uc_c, t_mat, mb_pair, mc_pair, a_rows)


def _pack_bf16_pair(a, b):
    ua = lax.bitcast_convert_type(a.astype(BF16).astype(F32), U32)
    ub = lax.bitcast_convert_type(b.astype(BF16).astype(F32), U32)
    return ua | (ub >> 16)


def _unpack_bf16_pair(w):
    a = lax.bitcast_convert_type(w & jnp.uint32(0xFFFF0000), F32)
    b = lax.bitcast_convert_type(w << 16, F32)
    return a, b


def _first_max(rows):
    best = rows[0]
    for r in rows[1:]:
        best = jnp.maximum(best, r)
    idx = jnp.full(best.shape, float(len(rows) - 1), F32)
    for k in range(len(rows) - 2, -1, -1):
        idx = jnp.where(rows[k] == best, float(k), idx)
    return best, idx


def _route_rows(lg):
    g_rows = [lg[k:k + 1] for k in range(N_GROUPS)]
    g_max, g_idx = _first_max(g_rows)
    g_sum = sum(jnp.exp(r - g_max) for r in g_rows)
    g_p = 1.0 / g_sum
    e_rows = []
    for j in range(EXPERTS_PER_GROUP):
        r = lg[N_GROUPS + (N_GROUPS - 1) * EXPERTS_PER_GROUP + j:][:1]
        for g in range(N_GROUPS - 2, -1, -1):
            k = N_GROUPS + g * EXPERTS_PER_GROUP + j
            r = jnp.where(g_idx == float(g), lg[k:k + 1], r)
        e_rows.append(r)
    v1, i1 = _first_max(e_rows)
    rest = [jnp.where(i1 == float(j), -jnp.inf, e_rows[j]) for j in range(EXPERTS_PER_GROUP)]
    v2, i2 = _first_max(rest)
    e21 = jnp.exp(v2 - v1)
    w1 = g_p / (1.0 + e21)
    w2 = w1 * e21
    lo = jnp.minimum(i1, i2)
    hi = jnp.maximum(i1, i2)
    base = jnp.where(lo == 0.0, 0.0, jnp.where(lo == 1.0, 3.0, 5.0))
    bucket = g_idx * float(PAIRS_PER_GROUP) + base + hi - lo - 1.0
    first_is_lo = i1 < i2
    return bucket, jnp.where(first_is_lo, w1, w2), jnp.where(first_is_lo, w2, w1)


def _mix_kernel(x_ref, xup_ref, xdn_ref, y_ref, mod_ref, g1_ref, g2_ref, cw_ref, win_ref, wglu_ref, bglu_ref,
                wout_ref, wr_ref, br_ref, *rest):
    n_cast = (len(rest) - 4) // 2
    x1_ref, h2p_ref, route_ref, counts_ref = rest[n_cast:n_cast + 4]
    for src, dst in zip(rest[:n_cast], rest[n_cast + 4:]):
        dst[...] = src[...].astype(BF16)
    i = pl.program_id(1)
    tm = x_ref.shape[1]
    d = x_ref.shape[2]
    sub = min(MIX_SUB, tm)
    n_sub = tm // sub
    cw, rw = CONV_WIDTH, CONV_ROW_WIDTH

    def hidden(xv):
        return _modulated_norm(xv, g1_ref[...], mod_ref[0, 0:1, :], mod_ref[0, 1:2, :]).astype(BF16)

    def halo(h):
        zh = jnp.dot(h, win_ref[:, 2 * cw:3 * cw], preferred_element_type=F32)
        return zh[:, 0:rw] * zh[:, rw:cw]

    @pl.when(jnp.logical_and(pl.program_id(0) == 0, i == 0))
    def _():
        counts_ref[...] = jnp.zeros_like(counts_ref)

    def sub_tile(s):
        r0 = s * sub
        xv = x_ref[0, r0:r0 + sub, :]
        hx = hidden(xv)
        g = jax.nn.gelu(jnp.concatenate([y_ref[0, j, r0:r0 + sub, :] for j in range(y_ref.shape[1])], axis=1))
        hx_up = hidden(xup_ref[0]) if s == 0 else None
        hx_dn = hidden(xdn_ref[0]) if s == n_sub - 1 else None
        yield
        z_r = jnp.dot(hx, win_ref[:, cw:2 * cw], preferred_element_type=F32)
        cr = z_r[:, 0:rw] * z_r[:, rw:cw]
        z_c = jnp.dot(hx, win_ref[:, 2 * cw:3 * cw], preferred_element_type=F32)
        cc = z_c[:, 0:rw] * z_c[:, rw:cw]
        col_products[s] = cc
        bg = jnp.dot(hx, win_ref[:, 0:cw], preferred_element_type=F32)
        if s == 0:
            up_halo = jnp.where(i == 0, 0.0, halo(hx_up))
        if s == n_sub - 1:
            dn_halo = jnp.where(i == pl.num_programs(1) - 1, 0.0, halo(hx_dn))
        glu = g * jax.nn.sigmoid(jnp.dot(g.astype(BF16), wglu_ref[...], preferred_element_type=F32) + bglu_ref[...])
        yield
        row = lax.broadcasted_iota(jnp.int32, (sub, 1), 0)
        col_in_row = row % GRID_W
        left = jnp.where(col_in_row == 0, 0.0, pltpu.roll(cr, 1, axis=0))
        right = jnp.where(col_in_row == GRID_W - 1, 0.0, pltpu.roll(cr, sub - 1, axis=0))
        w_r = cw_ref[:, :rw]
        row_part = left * w_r[0:1] + cr * w_r[1:2] + right * w_r[2:3]
        if s > 0:
            up_halo = col_products[s - 1][sub - GRID_W:]
        if s < n_sub - 1:
            dn_halo = col_products[s + 1][:GRID_W]
        up = jnp.concatenate([up_halo, cc[:sub - GRID_W]], axis=0)
        dn = jnp.concatenate([cc[GRID_W:], dn_halo], axis=0)
        w_c = cw_ref[:, rw:]
        col_part = up * w_c[0:1] + cc * w_c[1:2] + dn * w_c[2:3]
        y_row = (bg[:, 0:rw] * row_part).astype(BF16)
        y_col = (bg[:, rw:cw] * col_part).astype(BF16)
        mixed = jnp.concatenate([glu.astype(BF16), y_row, y_col], axis=1)
        yx = jnp.dot(mixed, wout_ref[...], preferred_element_type=F32)
        yield
        x1 = xv + mod_ref[0, 2:3, :] * yx
        x1_ref[0, r0:r0 + sub, :] = x1
        h2 = _modulated_norm(x1, g2_ref[...], mod_ref[0, 3:4, :], mod_ref[0, 4:5, :])
        h2b = h2.astype(BF16)
        lg2 = lax.dot_general(wr_ref[...], h2b, (((1,), (1,)), ((), ())), preferred_element_type=F32)
        lg = lg2[:ROUTER_ROWS] + lg2[ROUTER_ROWS:] + br_ref[...]
        bucket, w_a, w_b = _route_rows(lg)
        r8 = lax.broadcasted_iota(jnp.int32, (8, sub), 0)
        route_ref[:, r0:r0 + sub] = jnp.where(r8 == 0, bucket, jnp.where(r8 == 1, w_a, jnp.where(r8 == 2, w_b, 0.0)))
        rl = lax.broadcasted_iota(jnp.int32, (ROW_EXTRA, sub), 0)
        gates_t = jnp.where(rl == 0, w_a, jnp.where(rl == 1, w_b, 0.0))
        h2p_ref[0, r0:r0 + sub, 0:d // 2] = _pack_bf16_pair(h2[:, :d // 2], h2[:, d // 2:])
        h2p_ref[0, r0:r0 + sub, d // 2:] = lax.bitcast_convert_type(gates_t.T, U32)
        rb = lax.broadcasted_iota(jnp.int32, (BUCKET_ROWS, sub), 0).astype(F32)
        counts_ref[...] += jnp.sum(jnp.where(rb == bucket, 1.0, 0.0), axis=-1, keepdims=True)
        yield

    n_stage = 4
    col_products = {}
    tiles = [sub_tile(s) for s in range(n_sub)]
    for step in range(n_sub + n_stage - 1):
        for s in reversed(range(n_sub)):
            if 0 <= step - s < n_stage:
                next(tiles[s])


def _mix(x, y_s5, mods, norm1_g, norm2_g, conv_w, w_conv, w_glu, b_glu, w_out, w_router_t, b_router_t, tm, b0, nb,
         to_cast):
    _, n, d = x.shape
    nt = n // tm
    n_steps = nb * nt
    cast_specs = []
    for w in to_cast:
        per = -(-w.shape[0] // n_steps)
        n_blk = w.shape[0] // per
        assert n_blk * per == w.shape[0]
        cast_specs.append(pl.BlockSpec((per,) + w.shape[1:],
                                       lambda b, i, n_blk=n_blk: ((b * nt + i) * n_blk // n_steps, 0, 0)))
    halo_blocks = n // GRID_W
    per_tile = tm // GRID_W
    tok_out = lambda w: pl.BlockSpec((1, tm, w), lambda b, i: (b, i, 0))
    full = lambda a: pl.BlockSpec(a.shape, lambda b, i: (0,) * a.ndim)
    args = (x, x, x, y_s5, mods, norm1_g.reshape(1, d), norm2_g.reshape(1, d), conv_w, w_conv, w_glu,
            b_glu.reshape(1, -1), w_out, w_router_t, b_router_t)
    in_specs = [pl.BlockSpec((1, tm, d), lambda b, i: (b + b0, i, 0)),
                pl.BlockSpec((1, GRID_W, d), lambda b, i: (b + b0, jnp.maximum(i * per_tile - 1, 0), 0)),
                pl.BlockSpec((1, GRID_W, d),
                             lambda b, i: (b + b0, jnp.minimum((i + 1) * per_tile, halo_blocks - 1), 0)),
                pl.BlockSpec((1, y_s5.shape[1], tm, LANES), lambda b, i: (b + b0, 0, i, 0)),
                pl.BlockSpec((1, N_MOD, d), lambda b, i: (b + b0, 0, 0))] + [full(a) for a in args[5:]]
    return pl.pallas_call(
        _mix_kernel,
        grid=(nb, nt),
        in_specs=in_specs + cast_specs,
        out_specs=[tok_out(d), tok_out(d // 2 + ROW_EXTRA),
                   pl.BlockSpec((8, tm), lambda b, i: (0, b * nt + i)),
                   pl.BlockSpec((BUCKET_ROWS, 128), lambda b, i: (0, 0))] + cast_specs,
        out_shape=[jax.ShapeDtypeStruct((nb, n, d), F32),
                   jax.ShapeDtypeStruct((nb, n, d // 2 + ROW_EXTRA), U32),
                   jax.ShapeDtypeStruct((8, nb * n), F32),
                   jax.ShapeDtypeStruct((BUCKET_ROWS, 128), F32)]
        + [jax.ShapeDtypeStruct(w.shape, BF16) for w in to_cast],
        compiler_params=_params(("arbitrary", "arbitrary")),
        name="mix",
    )(*args, *to_cast)


def _rank_kernel(route_ref, offs_ref, dest_ref, run_ref):
    tr = route_ref.shape[1]
    n_blk = tr // LANES

    @pl.when(pl.program_id(0) == 0)
    def _():
        run_ref[...] = jnp.zeros_like(run_ref)

    bucket = route_ref[0:1, :]
    rb = lax.broadcasted_iota(jnp.int32, (BUCKET_ROWS, tr), 0).astype(F32)
    onehot = jnp.where(rb == bucket, 1.0, 0.0)
    blocks = [onehot[:, k * LANES:(k + 1) * LANES] for k in range(n_blk)]
    s_idx = lax.broadcasted_iota(jnp.int32, (LANES, LANES), 0)
    t_idx = lax.broadcasted_iota(jnp.int32, (LANES, LANES), 1)
    tri = jnp.where(s_idx <= t_idx, 1.0, 0.0).astype(BF16)
    prefix = jnp.dot(jnp.concatenate(blocks, axis=0).astype(BF16), tri, preferred_element_type=F32)
    start = run_ref[:, 0:1] + offs_ref[:, 0:1]
    carry = start
    for k in range(n_blk):
        pk = prefix[k * BUCKET_ROWS:(k + 1) * BUCKET_ROWS]
        dest = jnp.sum(blocks[k] * (pk - 1.0 + carry), axis=0, keepdims=True)
        dest_ref[:, k * LANES:(k + 1) * LANES] = dest.astype(jnp.int32)
        carry = carry + pk[:, LANES - 1:LANES]
    run_ref[...] += carry - start


def _rank(route_t, offs_rows, tr):
    n = route_t.shape[1]
    return pl.pallas_call(
        _rank_kernel,
        grid=(n // tr,),
        in_specs=[pl.BlockSpec((8, tr), lambda i: (0, i)),
                  pl.BlockSpec((BUCKET_ROWS, 128), lambda i: (0, 0))],
        out_specs=pl.BlockSpec((1, tr), lambda i: (0, i)),
        out_shape=jax.ShapeDtypeStruct((1, n), jnp.int32),
        scratch_shapes=[pltpu.VMEM((BUCKET_ROWS, 128), F32)],
        compiler_params=_params(("arbitrary",)),
        name="rank",
    )(route_t, offs_rows)


def _sc_mesh():
    return plsc.VectorSubcoreMesh(core_axis_name="core", subcore_axis_name="subcore")


def _scatter_rows(src, dest, n_out):
    n, w = src.shape
    sub = SC_INDEX_TILE // SC_ROWS

    @functools.partial(pl.kernel, out_type=jax.ShapeDtypeStruct((n_out, w), src.dtype), mesh=_sc_mesh(),
                       scratch_types=[], name="scatter_rows")
    def scatter(x_hbm, i_hbm, o_hbm):
        def body(x_vmem, i_vmem):
            j = pl.program_id(1)
            pltpu.sync_copy(x_vmem, o_hbm.at[i_vmem.at[0, pl.ds(j * SC_ROWS, SC_ROWS)]])

        pltpu.emit_pipeline(
            body,
            grid=(n // SC_INDEX_TILE, sub),
            in_specs=[pl.BlockSpec((SC_ROWS, w), lambda i, j: (i * sub + j, 0)),
                      pl.BlockSpec((1, SC_INDEX_TILE), lambda i, j: (0, i))],
            out_specs=[],
            core_axis_name=("core", "subcore"),
            dimension_semantics=(pltpu.PARALLEL, pltpu.ARBITRARY),
        )(x_hbm, i_hbm)

    return scatter(src, dest)


def _gather_rows(src, idx):
    n = idx.shape[1]
    w = src.shape[1]
    sub = SC_INDEX_TILE // SC_ROWS

    @functools.partial(pl.kernel, out_type=jax.ShapeDtypeStruct((n, w), src.dtype), mesh=_sc_mesh(),
                       scratch_types=[], name="gather_rows")
    def gather(x_hbm, i_hbm, o_hbm):
        def body(i_vmem, o_vmem):
            j = pl.program_id(1)
            pltpu.sync_copy(x_hbm.at[i_vmem.at[0, pl.ds(j * SC_ROWS, SC_ROWS)]], o_vmem)

        pltpu.emit_pipeline(
            body,
            grid=(n // SC_INDEX_TILE, sub),
            in_specs=[pl.BlockSpec((1, SC_INDEX_TILE), lambda i, j: (0, i))],
            out_specs=[pl.BlockSpec((SC_ROWS, w), lambda i, j: (i * sub + j, 0))],
            core_axis_name=("core", "subcore"),
            dimension_semantics=(pltpu.PARALLEL, pltpu.ARBITRARY),
        )(i_hbm, o_hbm)

    return gather(src, idx)


def _moe_kernel(ea_ref, eb_ref, valid_ref, rows_ref, w1a_ref, w3a_ref, w2a_ref, w1b_ref, w3b_ref, w2b_ref, o_ref):
    j = pl.program_id(0)
    half = rows_ref.shape[1] - ROW_EXTRA

    @pl.when(valid_ref[j] != 0)
    def _():
        ha, hb = _unpack_bf16_pair(rows_ref[:, 0:half])
        ha = ha.astype(BF16)
        hb = hb.astype(BF16)
        gates = lax.bitcast_convert_type(rows_ref[:, half:], F32)

        def expert(w1_ref, w3_ref, w2_ref, gate):
            def up(w_ref):
                return (jnp.dot(ha, w_ref[0, 0:half, :], preferred_element_type=F32)
                        + jnp.dot(hb, w_ref[0, half:, :], preferred_element_type=F32))

            a1 = up(w1_ref)
            he = (a1 * jax.nn.sigmoid(a1)) * up(w3_ref) * gate
            return jnp.dot(he.astype(BF16), w2_ref[0], preferred_element_type=F32)

        y = (expert(w1a_ref, w3a_ref, w2a_ref, gates[:, 0:1]) + expert(w1b_ref, w3b_ref, w2b_ref, gates[:, 1:2]))
        o_ref[...] = _pack_bf16_pair(y[:, :half], y[:, half:])


def _moe_grouped(rows, tile_ea, tile_eb, tile_valid, w1, w3, w2, tmm):
    r, w = rows.shape
    n_e, de, d = w2.shape
    up_a = pl.BlockSpec((1, d, de), lambda j, ea, eb, va: (ea[j], 0, 0))
    up_b = pl.BlockSpec((1, d, de), lambda j, ea, eb, va: (eb[j], 0, 0))
    grid_spec = pltpu.PrefetchScalarGridSpec(
        num_scalar_prefetch=3,
        grid=(r // tmm,),
        in_specs=[pl.BlockSpec((tmm, w), lambda j, ea, eb, va: (j, 0)),
                  up_a, up_a, pl.BlockSpec((1, de, d), lambda j, ea, eb, va: (ea[j], 0, 0)),
                  up_b, up_b, pl.BlockSpec((1, de, d), lambda j, ea, eb, va: (eb[j], 0, 0))],
        out_specs=pl.BlockSpec((tmm, d // 2), lambda j, ea, eb, va: (j, 0)),
    )
    return pl.pallas_call(
        _moe_kernel,
        grid_spec=grid_spec,
        out_shape=jax.ShapeDtypeStruct((r, d // 2), U32),
        compiler_params=_params(("arbitrary",)),
        name="moe",
    )(tile_ea, tile_eb, tile_valid, rows, w1, w3, w2, w1, w3, w2)


def _final_kernel(x1_ref, moe_ref, mod_ref, fg_ref, *rest):
    o_ref = rest[-1]
    ya, yb = _unpack_bf16_pair(moe_ref[0])
    half = ya.shape[1]
    gate = mod_ref[0, 5:6, :]
    xa = x1_ref[0, :, 0:half] + gate[:, 0:half] * ya
    xb = x1_ref[0, :, half:] + gate[:, half:] * yb
    ms = (jnp.sum(xa * xa, axis=-1, keepdims=True) + jnp.sum(xb * xb, axis=-1, keepdims=True)) / (2 * half)
    inv = lax.rsqrt(ms + RMS_EPS)
    o_ref[0, :, 0:half] = xa * inv * fg_ref[:, 0:half]
    o_ref[0, :, half:] = xb * inv * fg_ref[:, half:]


def _final(x1, moe_tok, mods, final_g, tm, b0, bsz, out_prev):
    nb, n, d = x1.shape
    tok = lambda w: pl.BlockSpec((1, tm, w), lambda b, i: (b, i, 0))
    args = [x1, moe_tok, mods, final_g.reshape(1, d)]
    in_specs = [tok(d), tok(d // 2), pl.BlockSpec((1, N_MOD, d), lambda b, i: (b + b0, 0, 0)),
                pl.BlockSpec((1, d), lambda b, i: (0, 0))]
    aliases = {}
    if out_prev is not None:
        args.append(out_prev)
        in_specs.append(pl.BlockSpec(memory_space=pl.ANY))
        aliases = {len(args) - 1: 0}
    return pl.pallas_call(
        _final_kernel,
        grid=(nb, n // tm),
        in_specs=in_specs,
        out_specs=pl.BlockSpec((1, tm, d), lambda b, i: (b + b0, i, 0)),
        out_shape=jax.ShapeDtypeStruct((bsz, n, d), F32),
        input_output_aliases=aliases,
        compiler_params=_params(("parallel", "parallel")),
        name="final",
    )(*args)


def _tile_plan(counts, tmm, n_tiles):
    tiles = (counts + (tmm - 1)) // tmm
    tile_end = jnp.cumsum(tiles)
    offs = (tile_end - tiles) * tmm
    n_valid = tile_end[-1]
    j = jnp.arange(n_tiles, dtype=jnp.int32)
    bucket = jnp.sum((tile_end[None, :] <= jnp.minimum(j, n_valid - 1)[:, None]).astype(jnp.int32), axis=1)
    pair_lo = jnp.array([0, 0, 0, 1, 1, 2], jnp.int32)
    pair_hi = jnp.array([1, 2, 3, 2, 3, 3], jnp.int32)
    group = bucket // PAIRS_PER_GROUP
    pair = bucket % PAIRS_PER_GROUP
    tile_ea = group * EXPERTS_PER_GROUP + pair_lo[pair]
    tile_eb = group * EXPERTS_PER_GROUP + pair_hi[pair]
    return offs, tile_ea, tile_eb, (j < n_valid).astype(jnp.int32)


def kernel(x, c, ctx, c_ctx, w_mod, b_mod, norm1_g, norm2_g, w_in, s5_lambda_re, s5_lambda_im, s5_log_dt,
           s5_b_re, s5_b_im, s5_c_re, s5_c_im, s5_d, w_glu, b_glu, conv_w, w_out, router_group_w,
           router_group_b, router_expert_w, router_expert_b, expert_w1, expert_w3, expert_w2, final_g):
    assert w_mod.shape[0] == 1, "single-layer kernel"
    bsz, n_tok, d = x.shape
    n_ctx = ctx.shape[1]
    n_all = bsz * n_tok
    l = 0
    tm = min(TOKEN_TILE, n_tok)

    n_cond = bsz + 1
    pad = (-n_cond) % 8
    cond = jnp.concatenate([c, c_ctx[None, :], jnp.zeros((pad, d), F32)], axis=0)
    m = _mod_rows(cond, w_mod[l], b_mod[l])
    mx = m[:bsz].reshape(bsz, N_MOD, d)
    mc = m[bsz:bsz + 1].reshape(1, N_MOD, d)

    w_in_b = w_in[l].astype(BF16)
    w_s5 = w_in_b[:, :S5_WIDTH]
    o_c = S5_WIDTH + CONV_WIDTH
    o_v = S5_WIDTH + 2 * CONV_WIDTH
    w_conv = jnp.concatenate(
        [w_in_b[:, S5_WIDTH:o_c], w_in_b[:, o_c:o_c + CONV_ROW_WIDTH], w_in_b[:, o_v:o_v + CONV_ROW_WIDTH],
         w_in_b[:, o_c + CONV_ROW_WIDTH:o_v], w_in_b[:, o_v + CONV_ROW_WIDTH:]], axis=1)
    u = _inproj(x, mx, True, norm1_g[l], w_s5, INPROJ_TILE, "inproj")
    uc = _inproj(ctx, mc, False, norm1_g[l], w_s5, INPROJ_TILE, "inproj_ctx")

    t_mat, mb_pair, mc_pair, a_rows = _s5_matrices(
        s5_lambda_re[l], s5_lambda_im[l], s5_log_dt[l], s5_b_re[l], s5_b_im[l], s5_c_re[l], s5_c_im[l], s5_d[l])
    y_c = _s5_scan(_chunkify(u, "chunkify"), _chunkify(uc, "chunkify_ctx"), t_mat, mb_pair, mc_pair, a_rows)
    y_s5 = _unchunkify(y_c, u.shape[1])

    n_logits = N_GROUPS + N_EXPERTS
    w_router = jnp.concatenate(
        [router_group_w[l], router_expert_w[l], jnp.zeros((d, ROUTER_ROWS - n_logits), F32)], axis=1).T
    w_router_hi = w_router.astype(BF16)
    w_router_lo = (w_router - w_router_hi.astype(F32)).astype(BF16)
    w_router_t = jnp.concatenate([w_router_hi, w_router_lo], axis=0)
    b_router = jnp.concatenate([router_group_b[l], router_expert_b[l], jnp.zeros((ROUTER_ROWS - n_logits,), F32)])
    tm_mix = min(MIX_TILE, n_tok)
    b_router_t = jnp.broadcast_to(b_router[:, None], (ROUTER_ROWS, min(MIX_SUB, tm_mix)))

    n_parts = MOE_PARTS if bsz % MOE_PARTS == 0 else 1
    nb = bsz // n_parts
    n_part = nb * n_tok
    n_buckets = N_GROUPS * PAIRS_PER_GROUP
    n_rows = n_part + n_buckets * MOE_TILE
    w_glu_b = w_glu[l].astype(BF16)
    w_out_b = w_out[l].astype(BF16)
    experts_f32 = (expert_w1[l], expert_w3[l], expert_w2[l])
    cast_plan = [experts_f32] if n_parts == 1 else [experts_f32[:2], experts_f32[2:]] + [()] * (n_parts - 2)
    w_experts = []
    staged = []
    for p in range(n_parts):
        x1, h2p, route_t, counts, *w_cast = _mix(x, y_s5, mx, norm1_g[l], norm2_g[l], conv_w[l], w_conv, w_glu_b,
                                                 b_glu[l], w_out_b, w_router_t, b_router_t, tm_mix, p * nb, nb,
                                                 cast_plan[p])
        w_experts += w_cast
        offs, *tiles = _tile_plan(counts[:n_buckets, 0].astype(jnp.int32), MOE_TILE, n_rows // MOE_TILE)
        offs_rows = jnp.zeros((BUCKET_ROWS,), F32).at[:n_buckets].set(offs.astype(F32))
        dest = _rank(route_t, jnp.broadcast_to(offs_rows[:, None], (BUCKET_ROWS, 128)), min(RANK_TILE, n_part))
        rows = _scatter_rows(h2p.reshape(n_part, d // 2 + ROW_EXTRA), dest, n_rows)
        staged.append((x1, rows, dest, tiles))
    out = None
    for p, (x1, rows, dest, tiles) in enumerate(staged):
        y_rows = _moe_grouped(rows, *tiles, *w_experts, MOE_TILE)
        moe_tok = _gather_rows(y_rows, dest).reshape(nb, n_tok, d // 2)
        out = _final(x1, moe_tok, mx, final_g, tm, p * nb, bsz, out)
    return out
```

```python
import functools

import jax
import jax.numpy as jnp
from jax import lax
from jax.experimental import pallas as pl
from jax.experimental.pallas import tpu as pltpu
from jax.experimental.pallas import tpu_sc as plsc

F32 = jnp.float32
BF16 = jnp.bfloat16
U32 = jnp.uint32

RMS_EPS = 1e-6
N_MOD = 6
GRID_W = 64
S5_WIDTH = 256
S5_H = 16
S5_P = 64
S5_GROUPS = S5_WIDTH // S5_H
S5_PAIRS = S5_GROUPS // 2
S5_CHUNK = 16
LANES = 128
STATE_PITCH = 136
CONV_WIDTH = 768
CONV_ROW_WIDTH = CONV_WIDTH // 2
N_GROUPS = 4
EXPERTS_PER_GROUP = 4
N_EXPERTS = N_GROUPS * EXPERTS_PER_GROUP
PAIRS_PER_GROUP = 6
ROUTER_ROWS = 32
BUCKET_ROWS = 32
ROW_EXTRA = 128
TOKEN_TILE = 1024
MOE_TILE = 512
RANK_TILE = 2048
INPROJ_TILE = 2048
MIX_TILE = 1024
MIX_SUB = 512
MOE_PARTS = 2
LAYOUT_GROUPS = 2
SC_ROWS = 32
SC_INDEX_TILE = 128
VMEM_LIMIT = 52 * 1024 * 1024


def _params(sem, vmem=VMEM_LIMIT):
    return pltpu.CompilerParams(dimension_semantics=sem, vmem_limit_bytes=vmem)


def _mod_kernel(c_ref, w_ref, b_ref, o_ref):
    c = c_ref[...]
    o_ref[...] = jnp.dot(c * jax.nn.sigmoid(c), w_ref[...], preferred_element_type=F32) + b_ref[...]


def _mod_rows(cond, w_mod, b_mod):
    n, d = cond.shape
    nout = w_mod.shape[1]
    bn = d
    return pl.pallas_call(
        _mod_kernel,
        grid=(nout // bn,),
        in_specs=[pl.BlockSpec((n, d), lambda j: (0, 0)),
                  pl.BlockSpec((d, bn), lambda j: (0, j)),
                  pl.BlockSpec((1, bn), lambda j: (0, j))],
        out_specs=pl.BlockSpec((n, bn), lambda j: (0, j)),
        out_shape=jax.ShapeDtypeStruct((n, nout), F32),
        compiler_params=_params(("arbitrary",)),
        name="mod",
    )(cond, w_mod, b_mod.reshape(1, nout))


def _modulated_norm(x, g, shift, scale):
    ms = jnp.mean(x * x, axis=-1, keepdims=True)
    return (x * lax.rsqrt(ms + RMS_EPS)) * (g * (1.0 + scale)) + shift


def _inproj_kernel(x_ref, mod_ref, g_ref, w_ref, u_ref):
    nbk, tm, d = x_ref.shape
    h = _modulated_norm(x_ref[...].reshape(nbk * tm, d), g_ref[...], mod_ref[0, 0:1, :], mod_ref[0, 1:2, :])
    u = jnp.dot(h.astype(BF16), w_ref[...], preferred_element_type=F32)
    for b in range(nbk):
        for j in range(S5_WIDTH // LANES):
            u_ref[b, j] = u[b * tm:(b + 1) * tm, j * LANES:(j + 1) * LANES]


def _inproj(x, mods, per_batch_mod, norm_g, w_s5, tile, name):
    bsz, n, d = x.shape
    tm = min(tile, n)
    nbk = 1 if per_batch_mod else max(1, min(bsz, tile // n))
    assert bsz % nbk == 0
    mod_map = (lambda b, i: (b, 0, 0)) if per_batch_mod else (lambda b, i: (0, 0, 0))
    n_slab = S5_WIDTH // LANES
    return pl.pallas_call(
        _inproj_kernel,
        grid=(bsz // nbk, n // tm),
        in_specs=[pl.BlockSpec((nbk, tm, d), lambda b, i: (b, i, 0)),
                  pl.BlockSpec((1, N_MOD, d), mod_map),
                  pl.BlockSpec((1, d), lambda b, i: (0, 0)),
                  pl.BlockSpec((d, S5_WIDTH), lambda b, i: (0, 0))],
        out_specs=pl.BlockSpec((nbk, n_slab, tm, LANES), lambda b, i: (b, 0, i, 0)),
        out_shape=jax.ShapeDtypeStruct((bsz, n_slab, n, LANES), F32),
        compiler_params=_params(("parallel", "parallel")),
        name=name,
    )(x, mods, norm_g.reshape(1, d), w_s5)


def _toeplitz_kernel(strip_ref, t_ref):
    lc = t_ref.shape[1] // S5_H
    for s in range(lc):
        off = (lc - 1 - s) * S5_H
        t_ref[0, s * S5_H:(s + 1) * S5_H, :] = strip_ref[0, :, off:off + lc * S5_H].astype(BF16)


def _toeplitz(strip):
    g_n, h_n, w = strip.shape
    n = S5_CHUNK * h_n
    return pl.pallas_call(
        _toeplitz_kernel,
        grid=(g_n,),
        in_specs=[pl.BlockSpec((1, h_n, w), lambda g: (g, 0, 0))],
        out_specs=pl.BlockSpec((1, n, n), lambda g: (g, 0, 0)),
        out_shape=jax.ShapeDtypeStruct((g_n, n, n), BF16),
        compiler_params=_params(("parallel",)),
        name="toeplitz",
    )(strip)


def _s5_matrices(lam_re, lam_im, log_dt, b_re, b_im, c_re, c_im, d_skip):
    lc, g_n, p_n, h_n = S5_CHUNK, S5_GROUPS, S5_P, S5_H
    lam = lax.complex(lam_re.astype(F32), lam_im.astype(F32))
    dt = jnp.exp(log_dt.astype(F32))[..., None]
    a_bar = jnp.exp(lam * dt)
    b_bar = ((a_bar - 1.0) / lam)[..., None] * lax.complex(b_re.astype(F32), b_im.astype(F32))
    cm = lax.complex(c_re.astype(F32), c_im.astype(F32))
    steps = jnp.arange(lc + 1, dtype=F32)
    apow = jnp.exp((lam * dt)[:, :, None, :] * steps[None, None, :, None])
    kern = jnp.einsum('dgop,dgjp,dgpi->dgjio', cm, apow[:, :, :lc], b_bar).real
    skip = jnp.eye(h_n, dtype=F32) * d_skip.astype(F32).reshape(g_n, 1, h_n)
    centre = kern[0, :, 0] + kern[1, :, 0] + skip
    lags = jnp.concatenate([kern[1, :, :0:-1], centre[:, None], kern[0, :, 1:]], axis=1)
    strip = lags.transpose(0, 2, 1, 3).reshape(g_n, h_n, (2 * lc - 1) * h_n)
    strip = jnp.pad(strip, ((0, 0), (0, 0), (0, h_n)))
    t_mat = _toeplitz(strip)

    def in_mat(pw, bb):
        return (pw[:, :, None, :] * bb.transpose(0, 2, 1)[:, None, :, :]).reshape(g_n, lc * h_n, p_n)

    mb_f = in_mat(apow[0, :, lc - 1::-1][:, :lc], b_bar[0])
    mb_b = in_mat(apow[1, :, :lc], b_bar[1])

    def out_mat(pw, cc):
        return (pw.transpose(0, 2, 1)[:, :, :, None] * cc.transpose(0, 2, 1)[:, :, None, :]).reshape(
            g_n, p_n, lc * h_n)

    mc_f = out_mat(apow[0, :, 1:lc + 1], cm[0])
    mc_b = out_mat(apow[1, :, lc:0:-1], cm[1])
    a_chunk = apow[:, :, lc]

    q_n = S5_PAIRS
    zeros_in = jnp.zeros((g_n, lc * h_n, p_n), F32)

    def pair_cols(m):
        m = m.reshape(q_n, 2, lc * h_n, p_n)
        z = zeros_in.reshape(q_n, 2, lc * h_n, p_n)[:, 0]
        top = jnp.concatenate([m[:, 0], z], axis=-1)
        bot = jnp.concatenate([z, m[:, 1]], axis=-1)
        return jnp.concatenate([top, bot], axis=1)

    mb_pair = jnp.concatenate([pair_cols(mb_f.real), pair_cols(mb_f.imag),
                               pair_cols(mb_b.real), pair_cols(mb_b.imag)], axis=-1)

    def pair_rows(m):
        m = m.reshape(q_n, 2, p_n, lc * h_n)
        z = jnp.zeros_like(m[:, 0])
        top = jnp.concatenate([m[:, 0], z], axis=-1)
        bot = jnp.concatenate([z, m[:, 1]], axis=-1)
        return jnp.concatenate([top, bot], axis=1)

    mc_pair = jnp.concatenate([pair_rows(mc_f.real), pair_rows(-mc_f.imag),
                               pair_rows(mc_b.real), pair_rows(-mc_b.imag)], axis=1)
    a_rows = jnp.stack([a_chunk[0].real, a_chunk[0].imag, a_chunk[1].real, a_chunk[1].imag], axis=0)
    a_rows = a_rows.reshape(4, q_n, 2 * p_n).transpose(1, 0, 2)
    a_rows = jnp.concatenate([a_rows, jnp.zeros_like(a_rows)], axis=1)
    return t_mat, mb_pair.astype(BF16), mc_pair.astype(BF16), a_rows


def _chunkify_kernel(u_ref, o_ref, *, gb):
    _, nb, nc, _ = o_ref.shape
    half = S5_CHUNK * S5_H
    per_slab = LANES // S5_H
    for b0 in range(0, nb, gb):
        for j in range(u_ref.shape[1]):
            cols = []
            for t in range(S5_CHUNK):
                rows = [u_ref[b0 + b, j, pl.ds(t, nc, stride=S5_CHUNK), :] for b in range(gb)]
                cols.append((rows[0] if gb == 1 else jnp.concatenate(rows, axis=0)).T)
            for gl in range(per_slab):
                g = j * per_slab + gl
                m = jnp.concatenate([c[gl * S5_H:(gl + 1) * S5_H, :] for c in cols], axis=0)
                o_ref[g // 2, b0:b0 + gb, :, (g % 2) * half:(g % 2 + 1) * half] = (
                    m.T.astype(BF16).reshape(gb, nc, half))


def _layout_step(bsz, gb):
    return gb * LAYOUT_GROUPS if bsz % (gb * LAYOUT_GROUPS) == 0 else gb


def _chunkify(u_slab, name):
    bsz, n_slab, n, _ = u_slab.shape
    nc = n // S5_CHUNK
    gb = min(bsz, max(1, LANES // nc))
    nb = _layout_step(bsz, gb)
    w = 2 * S5_CHUNK * S5_H
    return pl.pallas_call(
        functools.partial(_chunkify_kernel, gb=gb),
        grid=(bsz // nb,),
        in_specs=[pl.BlockSpec((nb, n_slab, n, LANES), lambda b: (b, 0, 0, 0))],
        out_specs=pl.BlockSpec((S5_PAIRS, nb, nc, w), lambda b: (0, b, 0, 0)),
        out_shape=jax.ShapeDtypeStruct((S5_PAIRS, bsz, nc, w), BF16),
        compiler_params=_params(("parallel",)),
        name=name,
    )(u_slab)


def _unchunkify_kernel(y_ref, o_ref):
    _, nb, nc, _ = y_ref.shape
    half = S5_CHUNK * S5_H
    per_slab = LANES // S5_H
    for b in range(nb):
        for j in range(o_ref.shape[1]):
            rows = []
            for gl in range(per_slab):
                g = j * per_slab + gl
                rows.append(y_ref[g // 2, b, :, (g % 2) * half:(g % 2 + 1) * half].astype(F32).T)
            for t in range(S5_CHUNK):
                bt = jnp.concatenate([r[t * S5_H:(t + 1) * S5_H, :] for r in rows], axis=0)
                o_ref[b, j, pl.ds(t, nc, stride=S5_CHUNK), :] = bt.T


def _unchunkify(y_c, n_slab):
    q_n, bsz, nc, w = y_c.shape
    n = nc * S5_CHUNK
    nb = _layout_step(bsz, 1)
    return pl.pallas_call(
        _unchunkify_kernel,
        grid=(bsz // nb,),
        in_specs=[pl.BlockSpec((q_n, nb, nc, w), lambda b: (0, b, 0, 0))],
        out_specs=pl.BlockSpec((nb, n_slab, n, LANES), lambda b: (b, 0, 0, 0)),
        out_shape=jax.ShapeDtypeStruct((bsz, n_slab, n, LANES), F32),
        compiler_params=_params(("parallel",)),
        name="unchunkify",
    )(y_c)


def _s5_kernel(u_ref, uc_ref, t_ref, mb_ref, mc_ref, a_ref, y_ref, s_lat, s_ctx, h_scr, *, bb):
    _, bsz, n_lat, w = u_ref.shape
    n_ctx = uc_ref.shape[2]
    n_blk = w // LANES
    rb = bb * n_lat

    mb = mb_ref[0]

    def in_lat(i, carry):
        s = jnp.dot(u_ref[0, pl.ds(i * bb, bb)].reshape(rb, w), mb, preferred_element_type=F32)
        for k in range(bb):
            r = pl.multiple_of((i * bb + k) * STATE_PITCH, 8)
            for blk in range(n_blk):
                s_lat[blk, pl.ds(r, n_lat), :] = s[k * n_lat:(k + 1) * n_lat, blk * LANES:(blk + 1) * LANES]
        return carry

    lax.fori_loop(0, bsz // bb, in_lat, 0)
    sc = jnp.dot(uc_ref[0].reshape(bsz * n_ctx, w), mb, preferred_element_type=F32)
    for blk in range(n_blk):
        s_ctx[blk] = sc[:, blk * LANES:(blk + 1) * LANES]

    a_fr, a_fi, a_br, a_bi = (a_ref[0, k:k + 1, :] for k in range(4))

    def step(h, a_r, a_i, s_r, s_i):
        h_r, h_i = h
        return a_r * h_r - a_i * h_i + s_r, a_r * h_i + a_i * h_r + s_i

    def ctx_rows(blk, c):
        return s_ctx[blk, pl.ds(c, bsz, stride=n_ctx), :]

    def lat_rows(ref, blk, c):
        return ref.at[blk, pl.ds(c, bsz, stride=STATE_PITCH), :]

    def ctx_step(k, carry):
        hf, hb = carry
        kb = n_ctx - 1 - k
        hf = step(hf, a_fr, a_fi, ctx_rows(0, k), ctx_rows(1, k))
        hb = step(hb, a_br, a_bi, ctx_rows(2, kb), ctx_rows(3, kb))
        return hf, hb

    zero = jnp.zeros((bsz, LANES), F32)
    carry = lax.fori_loop(0, n_ctx, ctx_step, ((zero, zero), (zero, zero)))

    def lat_step(k, carry):
        hf, hb = carry
        kb = n_lat - 1 - k
        lat_rows(h_scr, 0, k)[...] = hf[0]
        lat_rows(h_scr, 1, k)[...] = hf[1]
        lat_rows(h_scr, 2, kb)[...] = hb[0]
        lat_rows(h_scr, 3, kb)[...] = hb[1]
        hf = step(hf, a_fr, a_fi, lat_rows(s_lat, 0, k)[...], lat_rows(s_lat, 1, k)[...])
        hb = step(hb, a_br, a_bi, lat_rows(s_lat, 2, kb)[...], lat_rows(s_lat, 3, kb)[...])
        return hf, hb

    lax.fori_loop(0, n_lat, lat_step, carry)

    t0 = t_ref[0]
    t1 = t_ref[1]
    mc = mc_ref[0]
    half = S5_CHUNK * S5_H

    def out_lat(i, carry):
        u = u_ref[0, pl.ds(i * bb, bb)].reshape(rb, w)
        h_rows = []
        for k in range(bb):
            r = pl.multiple_of((i * bb + k) * STATE_PITCH, 8)
            h_rows.append(jnp.concatenate([h_scr[blk, pl.ds(r, n_lat), :] for blk in range(n_blk)], axis=1))
        h = jnp.concatenate(h_rows, axis=0).astype(BF16)
        inter = jnp.dot(h, mc, preferred_element_type=F32)
        y0 = jnp.dot(u[:, :half], t0, preferred_element_type=F32) + inter[:, :half]
        y1 = jnp.dot(u[:, half:], t1, preferred_element_type=F32) + inter[:, half:]
        y = jnp.concatenate([y0, y1], axis=1).astype(BF16)
        y_ref[0, pl.ds(i * bb, bb)] = y.reshape(bb, n_lat, w)
        return carry

    lax.fori_loop(0, bsz // bb, out_lat, 0)


def _s5_scan(u_c, uc_c, t_mat, mb_pair, mc_pair, a_rows):
    q_n, bsz, n_lat, w = u_c.shape
    n_ctx = uc_c.shape[2]
    assert n_lat + 8 == STATE_PITCH
    bb = min(4, bsz)
    n_blk = w // LANES
    return pl.pallas_call(
        functools.partial(_s5_kernel, bb=bb),
        grid=(q_n,),
        in_specs=[pl.BlockSpec((1, bsz, n_lat, w), lambda q: (q, 0, 0, 0)),
                  pl.BlockSpec((1, bsz, n_ctx, w), lambda q: (q, 0, 0, 0)),
                  pl.BlockSpec((2, w // 2, w // 2), lambda q: (q, 0, 0)),
                  pl.BlockSpec((1, w, w), lambda q: (q, 0, 0)),
                  pl.BlockSpec((1, w, w), lambda q: (q, 0, 0)),
                  pl.BlockSpec((1, 8, LANES), lambda q: (q, 0, 0))],
        out_specs=pl.BlockSpec((1, bsz, n_lat, w), lambda q: (q, 0, 0, 0)),
        out_shape=jax.ShapeDtypeStruct((q_n, bsz, n_lat, w), BF16),
        scratch_shapes=[pltpu.VMEM((n_blk, bsz * STATE_PITCH, LANES), F32),
                        pltpu.VMEM((n_blk, bsz * n_ctx, LANES), F32),
                        pltpu.VMEM((n_blk, bsz * STATE_PITCH, LANES), F32)],
        compiler_params=_params(("parallel",)),
        name="s5_scan",
    )(u_c, uc_c, t_mat, mb_pair, mc_pair, a_rows)


def _pack_bf16_pair(a, b):
    ua = lax.bitcast_convert_type(a.astype(BF16).astype(F32), U32)
    ub = lax.bitcast_convert_type(b.astype(BF16).astype(F32), U32)
    return ua | (ub >> 16)


def _unpack_bf16_pair(w):
    a = lax.bitcast_convert_type(w & jnp.uint32(0xFFFF0000), F32)
    b = lax.bitcast_convert_type(w << 16, F32)
    return a, b


def _first_max(rows):
    best = rows[0]
    for r in rows[1:]:
        best = jnp.maximum(best, r)
    idx = jnp.full(best.shape, float(len(rows) - 1), F32)
    for k in range(len(rows) - 2, -1, -1):
        idx = jnp.where(rows[k] == best, float(k), idx)
    return best, idx


def _route_rows(lg):
    g_rows = [lg[k:k + 1] for k in range(N_GROUPS)]
    g_max, g_idx = _first_max(g_rows)
    g_sum = sum(jnp.exp(r - g_max) for r in g_rows)
    g_p = 1.0 / g_sum
    e_rows = []
    for j in range(EXPERTS_PER_GROUP):
        r = lg[N_GROUPS + (N_GROUPS - 1) * EXPERTS_PER_GROUP + j:][:1]
        for g in range(N_GROUPS - 2, -1, -1):
            k = N_GROUPS + g * EXPERTS_PER_GROUP + j
            r = jnp.where(g_idx == float(g), lg[k:k + 1], r)
        e_rows.append(r)
    v1, i1 = _first_max(e_rows)
    rest = [jnp.where(i1 == float(j), -jnp.inf, e_rows[j]) for j in range(EXPERTS_PER_GROUP)]
    v2, i2 = _first_max(rest)
    e21 = jnp.exp(v2 - v1)
    w1 = g_p / (1.0 + e21)
    w2 = w1 * e21
    lo = jnp.minimum(i1, i2)
    hi = jnp.maximum(i1, i2)
    base = jnp.where(lo == 0.0, 0.0, jnp.where(lo == 1.0, 3.0, 5.0))
    bucket = g_idx * float(PAIRS_PER_GROUP) + base + hi - lo - 1.0
    first_is_lo = i1 < i2
    return bucket, jnp.where(first_is_lo, w1, w2), jnp.where(first_is_lo, w2, w1)


def _mix_kernel(x_ref, xup_ref, xdn_ref, y_ref, mod_ref, g1_ref, g2_ref, cw_ref, win_ref, wglu_ref, bglu_ref,
                wout_ref, wr_ref, br_ref, *rest):
    n_cast = (len(rest) - 4) // 2
    x1_ref, h2p_ref, route_ref, counts_ref = rest[n_cast:n_cast + 4]
    for src, dst in zip(rest[:n_cast], rest[n_cast + 4:]):
        dst[...] = src[...].astype(BF16)
    i = pl.program_id(1)
    tm = x_ref.shape[1]
    d = x_ref.shape[2]
    sub = min(MIX_SUB, tm)
    n_sub = tm // sub
    cw, rw = CONV_WIDTH, CONV_ROW_WIDTH

    def hidden(xv):
        return _modulated_norm(xv, g1_ref[...], mod_ref[0, 0:1, :], mod_ref[0, 1:2, :]).astype(BF16)

    def halo(h):
        zh = jnp.dot(h, win_ref[:, 2 * cw:3 * cw], preferred_element_type=F32)
        return zh[:, 0:rw] * zh[:, rw:cw]

    @pl.when(jnp.logical_and(pl.program_id(0) == 0, i == 0))
    def _():
        counts_ref[...] = jnp.zeros_like(counts_ref)

    def sub_tile(s):
        r0 = s * sub
        xv = x_ref[0, r0:r0 + sub, :]
        hx = hidden(xv)
        g = jax.nn.gelu(jnp.concatenate([y_ref[0, j, r0:r0 + sub, :] for j in range(y_ref.shape[1])], axis=1))
        hx_up = hidden(xup_ref[0]) if s == 0 else None
        hx_dn = hidden(xdn_ref[0]) if s == n_sub - 1 else None
        yield
        z_r = jnp.dot(hx, win_ref[:, cw:2 * cw], preferred_element_type=F32)
        cr = z_r[:, 0:rw] * z_r[:, rw:cw]
        z_c = jnp.dot(hx, win_ref[:, 2 * cw:3 * cw], preferred_element_type=F32)
        cc = z_c[:, 0:rw] * z_c[:, rw:cw]
        col_products[s] = cc
        bg = jnp.dot(hx, win_ref[:, 0:cw], preferred_element_type=F32)
        if s == 0:
            up_halo = jnp.where(i == 0, 0.0, halo(hx_up))
        if s == n_sub - 1:
            dn_halo = jnp.where(i == pl.num_programs(1) - 1, 0.0, halo(hx_dn))
        glu = g * jax.nn.sigmoid(jnp.dot(g.astype(BF16), wglu_ref[...], preferred_element_type=F32) + bglu_ref[...])
        yield
        row = lax.broadcasted_iota(jnp.int32, (sub, 1), 0)
        col_in_row = row % GRID_W
        left = jnp.where(col_in_row == 0, 0.0, pltpu.roll(cr, 1, axis=0))
        right = jnp.where(col_in_row == GRID_W - 1, 0.0, pltpu.roll(cr, sub - 1, axis=0))
        w_r = cw_ref[:, :rw]
        row_part = left * w_r[0:1] + cr * w_r[1:2] + right * w_r[2:3]
        if s > 0:
            up_halo = col_products[s - 1][sub - GRID_W:]
        if s < n_sub - 1:
            dn_halo = col_products[s + 1][:GRID_W]
        up = jnp.concatenate([up_halo, cc[:sub - GRID_W]], axis=0)
        dn = jnp.concatenate([cc[GRID_W:], dn_halo], axis=0)
        w_c = cw_ref[:, rw:]
        col_part = up * w_c[0:1] + cc * w_c[1:2] + dn * w_c[2:3]
        y_row = (bg[:, 0:rw] * row_part).astype(BF16)
        y_col = (bg[:, rw:cw] * col_part).astype(BF16)
        mixed = jnp.concatenate([glu.astype(BF16), y_row, y_col], axis=1)
        yx = jnp.dot(mixed, wout_ref[...], preferred_element_type=F32)
        yield
        x1 = xv + mod_ref[0, 2:3, :] * yx
        x1_ref[0, r0:r0 + sub, :] = x1
        h2 = _modulated_norm(x1, g2_ref[...], mod_ref[0, 3:4, :], mod_ref[0, 4:5, :])
        h2b = h2.astype(BF16)
        lg2 = lax.dot_general(wr_ref[...], h2b, (((1,), (1,)), ((), ())), preferred_element_type=F32)
        lg = lg2[:ROUTER_ROWS] + lg2[ROUTER_ROWS:] + br_ref[...]
        bucket, w_a, w_b = _route_rows(lg)
        r8 = lax.broadcasted_iota(jnp.int32, (8, sub), 0)
        route_ref[:, r0:r0 + sub] = jnp.where(r8 == 0, bucket, jnp.where(r8 == 1, w_a, jnp.where(r8 == 2, w_b, 0.0)))
        rl = lax.broadcasted_iota(jnp.int32, (ROW_EXTRA, sub), 0)
        gates_t = jnp.where(rl == 0, w_a, jnp.where(rl == 1, w_b, 0.0))
        h2p_ref[0, r0:r0 + sub, 0:d // 2] = _pack_bf16_pair(h2[:, :d // 2], h2[:, d // 2:])
        h2p_ref[0, r0:r0 + sub, d // 2:] = lax.bitcast_convert_type(gates_t.T, U32)
        rb = lax.broadcasted_iota(jnp.int32, (BUCKET_ROWS, sub), 0).astype(F32)
        counts_ref[...] += jnp.sum(jnp.where(rb == bucket, 1.0, 0.0), axis=-1, keepdims=True)
        yield

    n_stage = 4
    col_products = {}
    tiles = [sub_tile(s) for s in range(n_sub)]
    for step in range(n_sub + n_stage - 1):
        for s in reversed(range(n_sub)):
            if 0 <= step - s < n_stage:
                next(tiles[s])


def _mix(x, y_s5, mods, norm1_g, norm2_g, conv_w, w_conv, w_glu, b_glu, w_out, w_router_t, b_router_t, tm, b0, nb,
         to_cast):
    _, n, d = x.shape
    nt = n // tm
    n_steps = nb * nt
    cast_specs = []
    for w in to_cast:
        per = -(-w.shape[0] // n_steps)
        n_blk = w.shape[0] // per
        assert n_blk * per == w.shape[0]
        cast_specs.append(pl.BlockSpec((per,) + w.shape[1:],
                                       lambda b, i, n_blk=n_blk: ((b * nt + i) * n_blk // n_steps, 0, 0)))
    halo_blocks = n // GRID_W
    per_tile = tm // GRID_W
    tok_out = lambda w: pl.BlockSpec((1, tm, w), lambda b, i: (b, i, 0))
    full = lambda a: pl.BlockSpec(a.shape, lambda b, i: (0,) * a.ndim)
    args = (x, x, x, y_s5, mods, norm1_g.reshape(1, d), norm2_g.reshape(1, d), conv_w, w_conv, w_glu,
            b_glu.reshape(1, -1), w_out, w_router_t, b_router_t)
    in_specs = [pl.BlockSpec((1, tm, d), lambda b, i: (b + b0, i, 0)),
                pl.BlockSpec((1, GRID_W, d), lambda b, i: (b + b0, jnp.maximum(i * per_tile - 1, 0), 0)),
                pl.BlockSpec((1, GRID_W, d),
                             lambda b, i: (b + b0, jnp.minimum((i + 1) * per_tile, halo_blocks - 1), 0)),
                pl.BlockSpec((1, y_s5.shape[1], tm, LANES), lambda b, i: (b + b0, 0, i, 0)),
                pl.BlockSpec((1, N_MOD, d), lambda b, i: (b + b0, 0, 0))] + [full(a) for a in args[5:]]
    return pl.pallas_call(
        _mix_kernel,
        grid=(nb, nt),
        in_specs=in_specs + cast_specs,
        out_specs=[tok_out(d), tok_out(d // 2 + ROW_EXTRA),
                   pl.BlockSpec((8, tm), lambda b, i: (0, b * nt + i)),
                   pl.BlockSpec((BUCKET_ROWS, 128), lambda b, i: (0, 0))] + cast_specs,
        out_shape=[jax.ShapeDtypeStruct((nb, n, d), F32),
                   jax.ShapeDtypeStruct((nb, n, d // 2 + ROW_EXTRA), U32),
                   jax.ShapeDtypeStruct((8, nb * n), F32),
                   jax.ShapeDtypeStruct((BUCKET_ROWS, 128), F32)]
        + [jax.ShapeDtypeStruct(w.shape, BF16) for w in to_cast],
        compiler_params=_params(("arbitrary", "arbitrary")),
        name="mix",
    )(*args, *to_cast)


def _rank_kernel(route_ref, offs_ref, dest_ref, run_ref):
    tr = route_ref.shape[1]
    n_blk = tr // LANES

    @pl.when(pl.program_id(0) == 0)
    def _():
        run_ref[...] = jnp.zeros_like(run_ref)

    bucket = route_ref[0:1, :]
    rb = lax.broadcasted_iota(jnp.int32, (BUCKET_ROWS, tr), 0).astype(F32)
    onehot = jnp.where(rb == bucket, 1.0, 0.0)
    blocks = [onehot[:, k * LANES:(k + 1) * LANES] for k in range(n_blk)]
    s_idx = lax.broadcasted_iota(jnp.int32, (LANES, LANES), 0)
    t_idx = lax.broadcasted_iota(jnp.int32, (LANES, LANES), 1)
    tri = jnp.where(s_idx <= t_idx, 1.0, 0.0).astype(BF16)
    prefix = jnp.dot(jnp.concatenate(blocks, axis=0).astype(BF16), tri, preferred_element_type=F32)
    start = run_ref[:, 0:1] + offs_ref[:, 0:1]
    carry = start
    for k in range(n_blk):
        pk = prefix[k * BUCKET_ROWS:(k + 1) * BUCKET_ROWS]
        dest = jnp.sum(blocks[k] * (pk - 1.0 + carry), axis=0, keepdims=True)
        dest_ref[:, k * LANES:(k + 1) * LANES] = dest.astype(jnp.int32)
        carry = carry + pk[:, LANES - 1:LANES]
    run_ref[...] += carry - start


def _rank(route_t, offs_rows, tr):
    n = route_t.shape[1]
    return pl.pallas_call(
        _rank_kernel,
        grid=(n // tr,),
        in_specs=[pl.BlockSpec((8, tr), lambda i: (0, i)),
                  pl.BlockSpec((BUCKET_ROWS, 128), lambda i: (0, 0))],
        out_specs=pl.BlockSpec((1, tr), lambda i: (0, i)),
        out_shape=jax.ShapeDtypeStruct((1, n), jnp.int32),
        scratch_shapes=[pltpu.VMEM((BUCKET_ROWS, 128), F32)],
        compiler_params=_params(("arbitrary",)),
        name="rank",
    )(route_t, offs_rows)


def _sc_mesh():
    return plsc.VectorSubcoreMesh(core_axis_name="core", subcore_axis_name="subcore")


def _scatter_rows(src, dest, n_out):
    n, w = src.shape
    sub = SC_INDEX_TILE // SC_ROWS

    @functools.partial(pl.kernel, out_type=jax.ShapeDtypeStruct((n_out, w), src.dtype), mesh=_sc_mesh(),
                       scratch_types=[], name="scatter_rows")
    def scatter(x_hbm, i_hbm, o_hbm):
        def body(x_vmem, i_vmem):
            j = pl.program_id(1)
            pltpu.sync_copy(x_vmem, o_hbm.at[i_vmem.at[0, pl.ds(j * SC_ROWS, SC_ROWS)]])

        pltpu.emit_pipeline(
            body,
            grid=(n // SC_INDEX_TILE, sub),
            in_specs=[pl.BlockSpec((SC_ROWS, w), lambda i, j: (i * sub + j, 0)),
                      pl.BlockSpec((1, SC_INDEX_TILE), lambda i, j: (0, i))],
            out_specs=[],
            core_axis_name=("core", "subcore"),
            dimension_semantics=(pltpu.PARALLEL, pltpu.ARBITRARY),
        )(x_hbm, i_hbm)

    return scatter(src, dest)


def _gather_rows(src, idx):
    n = idx.shape[1]
    w = src.shape[1]
    sub = SC_INDEX_TILE // SC_ROWS

    @functools.partial(pl.kernel, out_type=jax.ShapeDtypeStruct((n, w), src.dtype), mesh=_sc_mesh(),
                       scratch_types=[], name="gather_rows")
    def gather(x_hbm, i_hbm, o_hbm):
        def body(i_vmem, o_vmem):
            j = pl.program_id(1)
            pltpu.sync_copy(x_hbm.at[i_vmem.at[0, pl.ds(j * SC_ROWS, SC_ROWS)]], o_vmem)

        pltpu.emit_pipeline(
            body,
            grid=(n // SC_INDEX_TILE, sub),
            in_specs=[pl.BlockSpec((1, SC_INDEX_TILE), lambda i, j: (0, i))],
            out_specs=[pl.BlockSpec((SC_ROWS, w), lambda i, j: (i * sub + j, 0))],
            core_axis_name=("core", "subcore"),
            dimension_semantics=(pltpu.PARALLEL, pltpu.ARBITRARY),
        )(i_hbm, o_hbm)

    return gather(src, idx)


def _moe_kernel(ea_ref, eb_ref, valid_ref, rows_ref, w1a_ref, w3a_ref, w2a_ref, w1b_ref, w3b_ref, w2b_ref, o_ref):
    j = pl.program_id(0)
    half = rows_ref.shape[1] - ROW_EXTRA

    @pl.when(valid_ref[j] != 0)
    def _():
        ha, hb = _unpack_bf16_pair(rows_ref[:, 0:half])
        ha = ha.astype(BF16)
        hb = hb.astype(BF16)
        gates = lax.bitcast_convert_type(rows_ref[:, half:], F32)

        def expert(w1_ref, w3_ref, w2_ref, gate):
            def up(w_ref):
                return (jnp.dot(ha, w_ref[0, 0:half, :], preferred_element_type=F32)
                        + jnp.dot(hb, w_ref[0, half:, :], preferred_element_type=F32))

            a1 = up(w1_ref)
            he = (a1 * jax.nn.sigmoid(a1)) * up(w3_ref) * gate
            return jnp.dot(he.astype(BF16), w2_ref[0], preferred_element_type=F32)

        y = (expert(w1a_ref, w3a_ref, w2a_ref, gates[:, 0:1]) + expert(w1b_ref, w3b_ref, w2b_ref, gates[:, 1:2]))
        o_ref[...] = _pack_bf16_pair(y[:, :half], y[:, half:])


def _moe_grouped(rows, tile_ea, tile_eb, tile_valid, w1, w3, w2, tmm):
    r, w = rows.shape
    n_e, de, d = w2.shape
    up_a = pl.BlockSpec((1, d, de), lambda j, ea, eb, va: (ea[j], 0, 0))
    up_b = pl.BlockSpec((1, d, de), lambda j, ea, eb, va: (eb[j], 0, 0))
    grid_spec = pltpu.PrefetchScalarGridSpec(
        num_scalar_prefetch=3,
        grid=(r // tmm,),
        in_specs=[pl.BlockSpec((tmm, w), lambda j, ea, eb, va: (j, 0)),
                  up_a, up_a, pl.BlockSpec((1, de, d), lambda j, ea, eb, va: (ea[j], 0, 0)),
                  up_b, up_b, pl.BlockSpec((1, de, d), lambda j, ea, eb, va: (eb[j], 0, 0))],
        out_specs=pl.BlockSpec((tmm, d // 2), lambda j, ea, eb, va: (j, 0)),
    )
    return pl.pallas_call(
        _moe_kernel,
        grid_spec=grid_spec,
        out_shape=jax.ShapeDtypeStruct((r, d // 2), U32),
        compiler_params=_params(("arbitrary",)),
        name="moe",
    )(tile_ea, tile_eb, tile_valid, rows, w1, w3, w2, w1, w3, w2)


def _final_kernel(x1_ref, moe_ref, mod_ref, fg_ref, *rest):
    o_ref = rest[-1]
    ya, yb = _unpack_bf16_pair(moe_ref[0])
    half = ya.shape[1]
    gate = mod_ref[0, 5:6, :]
    xa = x1_ref[0, :, 0:half] + gate[:, 0:half] * ya
    xb = x1_ref[0, :, half:] + gate[:, half:] * yb
    ms = (jnp.sum(xa * xa, axis=-1, keepdims=True) + jnp.sum(xb * xb, axis=-1, keepdims=True)) / (2 * half)
    inv = lax.rsqrt(ms + RMS_EPS)
    o_ref[0, :, 0:half] = xa * inv * fg_ref[:, 0:half]
    o_ref[0, :, half:] = xb * inv * fg_ref[:, half:]


def _final(x1, moe_tok, mods, final_g, tm, b0, bsz, out_prev):
    nb, n, d = x1.shape
    tok = lambda w: pl.BlockSpec((1, tm, w), lambda b, i: (b, i, 0))
    args = [x1, moe_tok, mods, final_g.reshape(1, d)]
    in_specs = [tok(d), tok(d // 2), pl.BlockSpec((1, N_MOD, d), lambda b, i: (b + b0, 0, 0)),
                pl.BlockSpec((1, d), lambda b, i: (0, 0))]
    aliases = {}
    if out_prev is not None:
        args.append(out_prev)
        in_specs.append(pl.BlockSpec(memory_space=pl.ANY))
        aliases = {len(args) - 1: 0}
    return pl.pallas_call(
        _final_kernel,
        grid=(nb, n // tm),
        in_specs=in_specs,
        out_specs=pl.BlockSpec((1, tm, d), lambda b, i: (b + b0, i, 0)),
        out_shape=jax.ShapeDtypeStruct((bsz, n, d), F32),
        input_output_aliases=aliases,
        compiler_params=_params(("parallel", "parallel")),
        name="final",
    )(*args)


def _tile_plan(counts, tmm, n_tiles):
    tiles = (counts + (tmm - 1)) // tmm
    tile_end = jnp.cumsum(tiles)
    offs = (tile_end - tiles) * tmm
    n_valid = tile_end[-1]
    j = jnp.arange(n_tiles, dtype=jnp.int32)
    bucket = jnp.sum((tile_end[None, :] <= jnp.minimum(j, n_valid - 1)[:, None]).astype(jnp.int32), axis=1)
    pair_lo = jnp.array([0, 0, 0, 1, 1, 2], jnp.int32)
    pair_hi = jnp.array([1, 2, 3, 2, 3, 3], jnp.int32)
    group = bucket // PAIRS_PER_GROUP
    pair = bucket % PAIRS_PER_GROUP
    tile_ea = group * EXPERTS_PER_GROUP + pair_lo[pair]
    tile_eb = group * EXPERTS_PER_GROUP + pair_hi[pair]
    return offs, tile_ea, tile_eb, (j < n_valid).astype(jnp.int32)


def kernel(x, c, ctx, c_ctx, w_mod, b_mod, norm1_g, norm2_g, w_in, s5_lambda_re, s5_lambda_im, s5_log_dt,
           s5_b_re, s5_b_im, s5_c_re, s5_c_im, s5_d, w_glu, b_glu, conv_w, w_out, router_group_w,
           router_group_b, router_expert_w, router_expert_b, expert_w1, expert_w3, expert_w2, final_g):
    assert w_mod.shape[0] == 1, "single-layer kernel"
    bsz, n_tok, d = x.shape
    n_ctx = ctx.shape[1]
    n_all = bsz * n_tok
    l = 0
    tm = min(TOKEN_TILE, n_tok)

    n_cond = bsz + 1
    pad = (-n_cond) % 8
    cond = jnp.concatenate([c, c_ctx[None, :], jnp.zeros((pad, d), F32)], axis=0)
    m = _mod_rows(cond, w_mod[l], b_mod[l])
    mx = m[:bsz].reshape(bsz, N_MOD, d)
    mc = m[bsz:bsz + 1].reshape(1, N_MOD, d)

    w_in_b = w_in[l].astype(BF16)
    w_s5 = w_in_b[:, :S5_WIDTH]
    o_c = S5_WIDTH + CONV_WIDTH
    o_v = S5_WIDTH + 2 * CONV_WIDTH
    w_conv = jnp.concatenate(
        [w_in_b[:, S5_WIDTH:o_c], w_in_b[:, o_c:o_c + CONV_ROW_WIDTH], w_in_b[:, o_v:o_v + CONV_ROW_WIDTH],
         w_in_b[:, o_c + CONV_ROW_WIDTH:o_v], w_in_b[:, o_v + CONV_ROW_WIDTH:]], axis=1)
    u = _inproj(x, mx, True, norm1_g[l], w_s5, INPROJ_TILE, "inproj")
    uc = _inproj(ctx, mc, False, norm1_g[l], w_s5, INPROJ_TILE, "inproj_ctx")

    t_mat, mb_pair, mc_pair, a_rows = _s5_matrices(
        s5_lambda_re[l], s5_lambda_im[l], s5_log_dt[l], s5_b_re[l], s5_b_im[l], s5_c_re[l], s5_c_im[l], s5_d[l])
    y_c = _s5_scan(_chunkify(u, "chunkify"), _chunkify(uc, "chunkify_ctx"), t_mat, mb_pair, mc_pair, a_rows)
    y_s5 = _unchunkify(y_c, u.shape[1])

    n_logits = N_GROUPS + N_EXPERTS
    w_router = jnp.concatenate(
        [router_group_w[l], router_expert_w[l], jnp.zeros((d, ROUTER_ROWS - n_logits), F32)], axis=1).T
    w_router_hi = w_router.astype(BF16)
    w_router_lo = (w_router - w_router_hi.astype(F32)).astype(BF16)
    w_router_t = jnp.concatenate([w_router_hi, w_router_lo], axis=0)
    b_router = jnp.concatenate([router_group_b[l], router_expert_b[l], jnp.zeros((ROUTER_ROWS - n_logits,), F32)])
    tm_mix = min(MIX_TILE, n_tok)
    b_router_t = jnp.broadcast_to(b_router[:, None], (ROUTER_ROWS, min(MIX_SUB, tm_mix)))

    n_parts = MOE_PARTS if bsz % MOE_PARTS == 0 else 1
    nb = bsz // n_parts
    n_part = nb * n_tok
    n_buckets = N_GROUPS * PAIRS_PER_GROUP
    n_rows = n_part + n_buckets * MOE_TILE
    w_glu_b = w_glu[l].astype(BF16)
    w_out_b = w_out[l].astype(BF16)
    experts_f32 = (expert_w1[l], expert_w3[l], expert_w2[l])
    cast_plan = [experts_f32] if n_parts == 1 else [experts_f32[:2], experts_f32[2:]] + [()] * (n_parts - 2)
    w_experts = []
    staged = []
    for p in range(n_parts):
        x1, h2p, route_t, counts, *w_cast = _mix(x, y_s5, mx, norm1_g[l], norm2_g[l], conv_w[l], w_conv, w_glu_b,
                                                 b_glu[l], w_out_b, w_router_t, b_router_t, tm_mix, p * nb, nb,
                                                 cast_plan[p])
        w_experts += w_cast
        offs, *tiles = _tile_plan(counts[:n_buckets, 0].astype(jnp.int32), MOE_TILE, n_rows // MOE_TILE)
        offs_rows = jnp.zeros((BUCKET_ROWS,), F32).at[:n_buckets].set(offs.astype(F32))
        dest = _rank(route_t, jnp.broadcast_to(offs_rows[:, None], (BUCKET_ROWS, 128)), min(RANK_TILE, n_part))
        rows = _scatter_rows(h2p.reshape(n_part, d // 2 + ROW_EXTRA), dest, n_rows)
        staged.append((x1, rows, dest, tiles))
    out = None
    for p, (x1, rows, dest, tiles) in enumerate(staged):
        y_rows = _moe_grouped(rows, *tiles, *w_experts, MOE_TILE)
        moe_tok = _gather_rows(y_rows, dest).reshape(nb, n_tok, d // 2)
        out = _final(x1, moe_tok, mx, final_g, tm, p * nb, bsz, out)
    return out
```

```python
import functools

import jax
import jax.numpy as jnp
from jax import lax
from jax.experimental import pallas as pl
from jax.experimental.pallas import tpu as pltpu
from jax.experimental.pallas import tpu_sc as plsc

F32 = jnp.float32
BF16 = jnp.bfloat16
U32 = jnp.uint32

RMS_EPS = 1e-6
N_MOD = 6
GRID_W = 64
S5_WIDTH = 256
S5_H = 16
S5_P = 64
S5_GROUPS = S5_WIDTH // S5_H
S5_PAIRS = S5_GROUPS // 2
S5_CHUNK = 16
LANES = 128
STATE_PITCH = 136
CONV_WIDTH = 768
CONV_ROW_WIDTH = CONV_WIDTH // 2
N_GROUPS = 4
EXPERTS_PER_GROUP = 4
N_EXPERTS = N_GROUPS * EXPERTS_PER_GROUP
PAIRS_PER_GROUP = 6
ROUTER_ROWS = 32
BUCKET_ROWS = 32
ROW_EXTRA = 128
TOKEN_TILE = 1024
MOE_TILE = 512
RANK_TILE = 2048
INPROJ_TILE = 2048
MIX_TILE = 1024
MIX_SUB = 512
MOE_PARTS = 2
LAYOUT_GROUPS = 2
SC_ROWS = 32
SC_INDEX_TILE = 128
VMEM_LIMIT = 52 * 1024 * 1024


def _params(sem, vmem=VMEM_LIMIT):
    return pltpu.CompilerParams(dimension_semantics=sem, vmem_limit_bytes=vmem)


def _pack_bf16_pair(a, b):
    ua = lax.bitcast_convert_type(a.astype(BF16).astype(F32), U32)
    ub = lax.bitcast_convert_type(b.astype(BF16).astype(F32), U32)
    return ua | (ub >> 16)


def _unpack_bf16_pair(w):
    a = lax.bitcast_convert_type(w & jnp.uint32(0xFFFF0000), F32)
    b = lax.bitcast_convert_type(w << 16, F32)
    return a, b


def _mod_kernel(c_ref, w_ref, b_ref, o_ref):
    c = c_ref[...]
    o_ref[...] = jnp.dot(c * jax.nn.sigmoid(c), w_ref[...], preferred_element_type=F32) + b_ref[...]


def _mod_rows(cond, w_mod, b_mod):
    n, d = cond.shape
    nout = w_mod.shape[1]
    bn = d
    return pl.pallas_call(
        _mod_kernel,
        grid=(nout // bn,),
        in_specs=[pl.BlockSpec((n, d), lambda j: (0, 0)),
                  pl.BlockSpec((d, bn), lambda j: (0, j)),
                  pl.BlockSpec((1, bn), lambda j: (0, j))],
        out_specs=pl.BlockSpec((n, bn), lambda j: (0, j)),
        out_shape=jax.ShapeDtypeStruct((n, nout), F32),
        compiler_params=_params(("arbitrary",)),
        name="mod",
    )(cond, w_mod, b_mod.reshape(1, nout))


def _modulated_norm(x, g, shift, scale):
    ms = jnp.mean(x * x, axis=-1, keepdims=True)
    return (x * lax.rsqrt(ms + RMS_EPS)) * (g * (1.0 + scale)) + shift


def _inproj_kernel(x_ref, mod_ref, g_ref, w_ref, u_ref):
    nbk, tm, d = x_ref.shape
    h = _modulated_norm(x_ref[...].reshape(nbk * tm, d), g_ref[...], mod_ref[0, 0:1, :], mod_ref[0, 1:2, :])
    u = jnp.dot(h.astype(BF16), w_ref[...], preferred_element_type=F32)
    packed = _pack_bf16_pair(u[:, 0:LANES], u[:, LANES:2 * LANES])
    for b in range(nbk):
        u_ref[b] = packed[b * tm:(b + 1) * tm]


def _inproj(x, mods, per_batch_mod, norm_g, w_s5, tile, name):
    bsz, n, d = x.shape
    tm = min(tile, n)
    nbk = 1 if per_batch_mod else max(1, min(bsz, tile // n))
    assert bsz % nbk == 0
    mod_map = (lambda b, i: (b, 0, 0)) if per_batch_mod else (lambda b, i: (0, 0, 0))
    assert S5_WIDTH == 2 * LANES
    return pl.pallas_call(
        _inproj_kernel,
        grid=(bsz // nbk, n // tm),
        in_specs=[pl.BlockSpec((nbk, tm, d), lambda b, i: (b, i, 0)),
                  pl.BlockSpec((1, N_MOD, d), mod_map),
                  pl.BlockSpec((1, d), lambda b, i: (0, 0)),
                  pl.BlockSpec((d, S5_WIDTH), lambda b, i: (0, 0))],
        out_specs=pl.BlockSpec((nbk, tm, LANES), lambda b, i: (b, i, 0)),
        out_shape=jax.ShapeDtypeStruct((bsz, n, LANES), U32),
        compiler_params=_params(("parallel", "parallel")),
        name=name,
    )(x, mods, norm_g.reshape(1, d), w_s5)


def _toeplitz_kernel(strip_ref, t_ref):
    lc = t_ref.shape[1] // S5_H
    for s in range(lc):
        off = (lc - 1 - s) * S5_H
        t_ref[0, s * S5_H:(s + 1) * S5_H, :] = strip_ref[0, :, off:off + lc * S5_H].astype(BF16)


def _toeplitz(strip):
    g_n, h_n, w = strip.shape
    n = S5_CHUNK * h_n
    return pl.pallas_call(
        _toeplitz_kernel,
        grid=(g_n,),
        in_specs=[pl.BlockSpec((1, h_n, w), lambda g: (g, 0, 0))],
        out_specs=pl.BlockSpec((1, n, n), lambda g: (g, 0, 0)),
        out_shape=jax.ShapeDtypeStruct((g_n, n, n), BF16),
        compiler_params=_params(("parallel",)),
        name="toeplitz",
    )(strip)


def _s5_matrices(lam_re, lam_im, log_dt, b_re, b_im, c_re, c_im, d_skip):
    lc, g_n, p_n, h_n = S5_CHUNK, S5_GROUPS, S5_P, S5_H
    lam = lax.complex(lam_re.astype(F32), lam_im.astype(F32))
    dt = jnp.exp(log_dt.astype(F32))[..., None]
    a_bar = jnp.exp(lam * dt)
    b_bar = ((a_bar - 1.0) / lam)[..., None] * lax.complex(b_re.astype(F32), b_im.astype(F32))
    cm = lax.complex(c_re.astype(F32), c_im.astype(F32))
    steps = jnp.arange(lc + 1, dtype=F32)
    apow = jnp.exp((lam * dt)[:, :, None, :] * steps[None, None, :, None])
    kern = jnp.einsum('dgop,dgjp,dgpi->dgjio', cm, apow[:, :, :lc], b_bar).real
    skip = jnp.eye(h_n, dtype=F32) * d_skip.astype(F32).reshape(g_n, 1, h_n)
    centre = kern[0, :, 0] + kern[1, :, 0] + skip
    lags = jnp.concatenate([kern[1, :, :0:-1], centre[:, None], kern[0, :, 1:]], axis=1)
    strip = lags.transpose(0, 2, 1, 3).reshape(g_n, h_n, (2 * lc - 1) * h_n)
    strip = jnp.pad(strip, ((0, 0), (0, 0), (0, h_n)))
    t_mat = _toeplitz(strip)

    def in_mat(pw, bb):
        return (pw[:, :, None, :] * bb.transpose(0, 2, 1)[:, None, :, :]).reshape(g_n, lc * h_n, p_n)

    mb_f = in_mat(apow[0, :, lc - 1::-1][:, :lc], b_bar[0])
    mb_b = in_mat(apow[1, :, :lc], b_bar[1])

    def out_mat(pw, cc):
        return (pw.transpose(0, 2, 1)[:, :, :, None] * cc.transpose(0, 2, 1)[:, :, None, :]).reshape(
            g_n, p_n, lc * h_n)

    mc_f = out_mat(apow[0, :, 1:lc + 1], cm[0])
    mc_b = out_mat(apow[1, :, lc:0:-1], cm[1])
    a_chunk = apow[:, :, lc]

    q_n = S5_PAIRS
    zeros_in = jnp.zeros((g_n, lc * h_n, p_n), F32)

    def pair_cols(m):
        m = m.reshape(q_n, 2, lc * h_n, p_n)
        z = zeros_in.reshape(q_n, 2, lc * h_n, p_n)[:, 0]
        top = jnp.concatenate([m[:, 0], z], axis=-1)
        bot = jnp.concatenate([z, m[:, 1]], axis=-1)
        return jnp.concatenate([top, bot], axis=1)

    mb_pair = jnp.concatenate([pair_cols(mb_f.real), pair_cols(mb_f.imag),
                               pair_cols(mb_b.real), pair_cols(mb_b.imag)], axis=-1)

    def pair_rows(m):
        m = m.reshape(q_n, 2, p_n, lc * h_n)
        z = jnp.zeros_like(m[:, 0])
        top = jnp.concatenate([m[:, 0], z], axis=-1)
        bot = jnp.concatenate([z, m[:, 1]], axis=-1)
        return jnp.concatenate([top, bot], axis=1)

    mc_pair = jnp.concatenate([pair_rows(mc_f.real), pair_rows(-mc_f.imag),
                               pair_rows(mc_b.real), pair_rows(-mc_b.imag)], axis=1)
    a_rows = jnp.stack([a_chunk[0].real, a_chunk[0].imag, a_chunk[1].real, a_chunk[1].imag], axis=0)
    a_rows = a_rows.reshape(4, q_n, 2 * p_n).transpose(1, 0, 2)
    a_rows = jnp.concatenate([a_rows, jnp.zeros_like(a_rows)], axis=1)
    return t_mat, mb_pair.astype(BF16), mc_pair.astype(BF16), a_rows


def _chunkify_kernel(u_ref, o_ref, *, gb):
    _, nb, nc, _ = o_ref.shape
    half = S5_CHUNK * S5_H
    per_slab = LANES // S5_H
    for b0 in range(0, nb, gb):
        slabs = ([], [])
        for t in range(S5_CHUNK):
            rows = [u_ref[b0 + b, pl.ds(t, nc, stride=S5_CHUNK), :] for b in range(gb)]
            for j, part in enumerate(_unpack_bf16_pair(rows[0] if gb == 1 else jnp.concatenate(rows, axis=0))):
                slabs[j].append(part.T)
        for j, cols in enumerate(slabs):
            for gl in range(per_slab):
                g = j * per_slab + gl
                m = jnp.concatenate([c[gl * S5_H:(gl + 1) * S5_H, :] for c in cols], axis=0)
                o_ref[g // 2, b0:b0 + gb, :, (g % 2) * half:(g % 2 + 1) * half] = (
                    m.T.astype(BF16).reshape(gb, nc, half))


def _layout_step(bsz, gb):
    return gb * LAYOUT_GROUPS if bsz % (gb * LAYOUT_GROUPS) == 0 else gb


def _chunkify(u_rows, name):
    bsz, n, _ = u_rows.shape
    nc = n // S5_CHUNK
    gb = min(bsz, max(1, LANES // nc))
    nb = _layout_step(bsz, gb)
    w = 2 * S5_CHUNK * S5_H
    return pl.pallas_call(
        functools.partial(_chunkify_kernel, gb=gb),
        grid=(bsz // nb,),
        in_specs=[pl.BlockSpec((nb, n, LANES), lambda b: (b, 0, 0))],
        out_specs=pl.BlockSpec((S5_PAIRS, nb, nc, w), lambda b: (0, b, 0, 0)),
        out_shape=jax.ShapeDtypeStruct((S5_PAIRS, bsz, nc, w), BF16),
        compiler_params=_params(("parallel",)),
        name=name,
    )(u_rows)


def _unchunkify_kernel(y_ref, o_ref):
    _, nb, nc, _ = y_ref.shape
    half = S5_CHUNK * S5_H
    per_slab = LANES // S5_H
    for b in range(nb):
        rows = []
        for g in range(S5_GROUPS):
            rows.append(y_ref[g // 2, b, :, (g % 2) * half:(g % 2 + 1) * half].astype(F32).T)
        for t in range(S5_CHUNK):
            tiles = [jnp.concatenate([r[t * S5_H:(t + 1) * S5_H, :] for r in rows[j * per_slab:(j + 1) * per_slab]],
                                     axis=0).T for j in range(2)]
            o_ref[b, pl.ds(t, nc, stride=S5_CHUNK), :] = _pack_bf16_pair(*tiles)


def _unchunkify(y_c):
    q_n, bsz, nc, w = y_c.shape
    n = nc * S5_CHUNK
    nb = _layout_step(bsz, 1)
    return pl.pallas_call(
        _unchunkify_kernel,
        grid=(bsz // nb,),
        in_specs=[pl.BlockSpec((q_n, nb, nc, w), lambda b: (0, b, 0, 0))],
        out_specs=pl.BlockSpec((nb, n, LANES), lambda b: (b, 0, 0)),
        out_shape=jax.ShapeDtypeStruct((bsz, n, LANES), U32),
        compiler_params=_params(("parallel",)),
        name="unchunkify",
    )(y_c)


def _s5_kernel(u_ref, uc_ref, t_ref, mb_ref, mc_ref, a_ref, y_ref, s_lat, s_ctx, h_scr, *, bb):
    _, bsz, n_lat, w = u_ref.shape
    n_ctx = uc_ref.shape[2]
    n_blk = w // LANES
    rb = bb * n_lat

    mb = mb_ref[0]

    def in_lat(i, carry):
        s = jnp.dot(u_ref[0, pl.ds(i * bb, bb)].reshape(rb, w), mb, preferred_element_type=F32)
        for k in range(bb):
            r = pl.multiple_of((i * bb + k) * STATE_PITCH, 8)
            for blk in range(n_blk):
                s_lat[blk, pl.ds(r, n_lat), :] = s[k * n_lat:(k + 1) * n_lat, blk * LANES:(blk + 1) * LANES]
        return carry

    lax.fori_loop(0, bsz // bb, in_lat, 0)
    sc = jnp.dot(uc_ref[0].reshape(bsz * n_ctx, w), mb, preferred_element_type=F32)
    for blk in range(n_blk):
        s_ctx[blk] = sc[:, blk * LANES:(blk + 1) * LANES]

    a_fr, a_fi, a_br, a_bi = (a_ref[0, k:k + 1, :] for k in range(4))

    def step(h, a_r, a_i, s_r, s_i):
        h_r, h_i = h
        return a_r * h_r - a_i * h_i + s_r, a_r * h_i + a_i * h_r + s_i

    def ctx_rows(blk, c):
        return s_ctx[blk, pl.ds(c, bsz, stride=n_ctx), :]

    def lat_rows(ref, blk, c):
        return ref.at[blk, pl.ds(c, bsz, stride=STATE_PITCH), :]

    def ctx_step(k, carry):
        hf, hb = carry
        kb = n_ctx - 1 - k
        hf = step(hf, a_fr, a_fi, ctx_rows(0, k), ctx_rows(1, k))
        hb = step(hb, a_br, a_bi, ctx_rows(2, kb), ctx_rows(3, kb))
        return hf, hb

    zero = jnp.zeros((bsz, LANES), F32)
    carry = lax.fori_loop(0, n_ctx, ctx_step, ((zero, zero), (zero, zero)))

    def lat_step(k, carry):
        hf, hb = carry
        kb = n_lat - 1 - k
        lat_rows(h_scr, 0, k)[...] = hf[0]
        lat_rows(h_scr, 1, k)[...] = hf[1]
        lat_rows(h_scr, 2, kb)[...] = hb[0]
        lat_rows(h_scr, 3, kb)[...] = hb[1]
        hf = step(hf, a_fr, a_fi, lat_rows(s_lat, 0, k)[...], lat_rows(s_lat, 1, k)[...])
        hb = step(hb, a_br, a_bi, lat_rows(s_lat, 2, kb)[...], lat_rows(s_lat, 3, kb)[...])
        return hf, hb

    lax.fori_loop(0, n_lat, lat_step, carry)

    t0 = t_ref[0]
    t1 = t_ref[1]
    mc = mc_ref[0]
    half = S5_CHUNK * S5_H

    def out_lat(i, carry):
        u = u_ref[0, pl.ds(i * bb, bb)].reshape(rb, w)
        h_rows = []
        for k in range(bb):
            r = pl.multiple_of((i * bb + k) * STATE_PITCH, 8)
            h_rows.append(jnp.concatenate([h_scr[blk, pl.ds(r, n_lat), :] for blk in range(n_blk)], axis=1))
        h = jnp.concatenate(h_rows, axis=0).astype(BF16)
        inter = jnp.dot(h, mc, preferred_element_type=F32)
        y0 = jnp.dot(u[:, :half], t0, preferred_element_type=F32) + inter[:, :half]
        y1 = jnp.dot(u[:, half:], t1, preferred_element_type=F32) + inter[:, half:]
        y = jnp.concatenate([y0, y1], axis=1).astype(BF16)
        y_ref[0, pl.ds(i * bb, bb)] = y.reshape(bb, n_lat, w)
        return carry

    lax.fori_loop(0, bsz // bb, out_lat, 0)


def _s5_scan(u_c, uc_c, t_mat, mb_pair, mc_pair, a_rows):
    q_n, bsz, n_lat, w = u_c.shape
    n_ctx = uc_c.shape[2]
    assert n_lat + 8 == STATE_PITCH
    bb = min(4, bsz)
    n_blk = w // LANES
    return pl.pallas_call(
        functools.partial(_s5_kernel, bb=bb),
        grid=(q_n,),
        in_specs=[pl.BlockSpec((1, bsz, n_lat, w), lambda q: (q, 0, 0, 0)),
                  pl.BlockSpec((1, bsz, n_ctx, w), lambda q: (q, 0, 0, 0)),
                  pl.BlockSpec((2, w // 2, w // 2), lambda q: (q, 0, 0)),
                  pl.BlockSpec((1, w, w), lambda q: (q, 0, 0)),
                  pl.BlockSpec((1, w, w), lambda q: (q, 0, 0)),
                  pl.BlockSpec((1, 8, LANES), lambda q: (q, 0, 0))],
        out_specs=pl.BlockSpec((1, bsz, n_lat, w), lambda q: (q, 0, 0, 0)),
        out_shape=jax.ShapeDtypeStruct((q_n, bsz, n_lat, w), BF16),
        scratch_shapes=[pltpu.VMEM((n_blk, bsz * STATE_PITCH, LANES), F32),
                        pltpu.VMEM((n_blk, bsz * n_ctx, LANES), F32),
                        pltpu.VMEM((n_blk, bsz * STATE_PITCH, LANES), F32)],
        compiler_params=_params(("parallel",)),
        name="s5_scan",
    )(u_c, uc_c, t_mat, mb_pair, mc_pair, a_rows)


def _first_max(rows):
    best = rows[0]
    for r in rows[1:]:
        best = jnp.maximum(best, r)
    idx = jnp.full(best.shape, float(len(rows) - 1), F32)
    for k in range(len(rows) - 2, -1, -1):
        idx = jnp.where(rows[k] == best, float(k), idx)
    return best, idx


def _route_rows(lg):
    g_rows = [lg[k:k + 1] for k in range(N_GROUPS)]
    g_max, g_idx = _first_max(g_rows)
    g_sum = sum(jnp.exp(r - g_max) for r in g_rows)
    g_p = 1.0 / g_sum
    e_rows = []
    for j in range(EXPERTS_PER_GROUP):
        r = lg[N_GROUPS + (N_GROUPS - 1) * EXPERTS_PER_GROUP + j:][:1]
        for g in range(N_GROUPS - 2, -1, -1):
            k = N_GROUPS + g * EXPERTS_PER_GROUP + j
            r = jnp.where(g_idx == float(g), lg[k:k + 1], r)
        e_rows.append(r)
    v1, i1 = _first_max(e_rows)
    rest = [jnp.where(i1 == float(j), -jnp.inf, e_rows[j]) for j in range(EXPERTS_PER_GROUP)]
    v2, i2 = _first_max(rest)
    e21 = jnp.exp(v2 - v1)
    w1 = g_p / (1.0 + e21)
    w2 = w1 * e21
    lo = jnp.minimum(i1, i2)
    hi = jnp.maximum(i1, i2)
    base = jnp.where(lo == 0.0, 0.0, jnp.where(lo == 1.0, 3.0, 5.0))
    bucket = g_idx * float(PAIRS_PER_GROUP) + base + hi - lo - 1.0
    first_is_lo = i1 < i2
    return bucket, jnp.where(first_is_lo, w1, w2), jnp.where(first_is_lo, w2, w1)


def _mix_kernel(x_ref, xup_ref, xdn_ref, y_ref, mod_ref, g1_ref, g2_ref, cw_ref, win_ref, wglu_ref, bglu_ref,
                wout_ref, wr_ref, br_ref, *rest):
    n_cast = (len(rest) - 4) // 2
    x1_ref, h2p_ref, route_ref, counts_ref = rest[n_cast:n_cast + 4]
    for src, dst in zip(rest[:n_cast], rest[n_cast + 4:]):
        dst[...] = src[...].astype(BF16)
    i = pl.program_id(1)
    tm = x_ref.shape[1]
    d = x_ref.shape[2]
    sub = min(MIX_SUB, tm)
    n_sub = tm // sub
    cw, rw = CONV_WIDTH, CONV_ROW_WIDTH

    def hidden(xv):
        return _modulated_norm(xv, g1_ref[...], mod_ref[0, 0:1, :], mod_ref[0, 1:2, :]).astype(BF16)

    def halo(h):
        zh = jnp.dot(h, win_ref[:, 2 * cw:3 * cw], preferred_element_type=F32)
        return zh[:, 0:rw] * zh[:, rw:cw]

    @pl.when(jnp.logical_and(pl.program_id(0) == 0, i == 0))
    def _():
        counts_ref[...] = jnp.zeros_like(counts_ref)

    def sub_tile(s):
        r0 = s * sub
        xv = x_ref[0, r0:r0 + sub, :]
        hx = hidden(xv)
        g = jax.nn.gelu(jnp.concatenate(_unpack_bf16_pair(y_ref[0, r0:r0 + sub, :]), axis=1))
        hx_up = hidden(xup_ref[0]) if s == 0 else None
        hx_dn = hidden(xdn_ref[0]) if s == n_sub - 1 else None
        yield
        z_r = jnp.dot(hx, win_ref[:, cw:2 * cw], preferred_element_type=F32)
        cr = z_r[:, 0:rw] * z_r[:, rw:cw]
        z_c = jnp.dot(hx, win_ref[:, 2 * cw:3 * cw], preferred_element_type=F32)
        cc = z_c[:, 0:rw] * z_c[:, rw:cw]
        col_products[s] = cc
        yield
        bg = jnp.dot(hx, win_ref[:, 0:cw], preferred_element_type=F32)
        if s == 0:
            up_halo = jnp.where(i == 0, 0.0, halo(hx_up))
        if s == n_sub - 1:
            dn_halo = jnp.where(i == pl.num_programs(1) - 1, 0.0, halo(hx_dn))
        glu = g * jax.nn.sigmoid(jnp.dot(g.astype(BF16), wglu_ref[...], preferred_element_type=F32) + bglu_ref[...])
        yield
        row = lax.broadcasted_iota(jnp.int32, (sub, 1), 0)
        col_in_row = row % GRID_W
        left = jnp.where(col_in_row == 0, 0.0, pltpu.roll(cr, 1, axis=0))
        right = jnp.where(col_in_row == GRID_W - 1, 0.0, pltpu.roll(cr, sub - 1, axis=0))
        w_r = cw_ref[:, :rw]
        row_part = left * w_r[0:1] + cr * w_r[1:2] + right * w_r[2:3]
        if s > 0:
            up_halo = col_products[s - 1][sub - GRID_W:]
        if s < n_sub - 1:
            dn_halo = col_products[s + 1][:GRID_W]
        up = jnp.concatenate([up_halo, cc[:sub - GRID_W]], axis=0)
        dn = jnp.concatenate([cc[GRID_W:], dn_halo], axis=0)
        w_c = cw_ref[:, rw:]
        col_part = up * w_c[0:1] + cc * w_c[1:2] + dn * w_c[2:3]
        y_row = (bg[:, 0:rw] * row_part).astype(BF16)
        y_col = (bg[:, rw:cw] * col_part).astype(BF16)
        mixed = jnp.concatenate([glu.astype(BF16), y_row, y_col], axis=1)
        yield
        yx = jnp.dot(mixed, wout_ref[...], preferred_element_type=F32)
        yield
        x1 = xv + mod_ref[0, 2:3, :] * yx
        x1_ref[0, r0:r0 + sub, :] = x1
        h2 = _modulated_norm(x1, g2_ref[...], mod_ref[0, 3:4, :], mod_ref[0, 4:5, :])
        h2b = h2.astype(BF16)
        lg2 = lax.dot_general(wr_ref[...], h2b, (((1,), (1,)), ((), ())), preferred_element_type=F32)
        lg = lg2[:ROUTER_ROWS] + lg2[ROUTER_ROWS:] + br_ref[...]
        bucket, w_a, w_b = _route_rows(lg)
        r8 = lax.broadcasted_iota(jnp.int32, (8, sub), 0)
        route_ref[:, r0:r0 + sub] = jnp.where(r8 == 0, bucket, jnp.where(r8 == 1, w_a, jnp.where(r8 == 2, w_b, 0.0)))
        rl = lax.broadcasted_iota(jnp.int32, (ROW_EXTRA, sub), 0)
        gates_t = jnp.where(rl == 0, w_a, jnp.where(rl == 1, w_b, 0.0))
        h2p_ref[0, r0:r0 + sub, 0:d // 2] = _pack_bf16_pair(h2[:, :d // 2], h2[:, d // 2:])
        h2p_ref[0, r0:r0 + sub, d // 2:] = lax.bitcast_convert_type(gates_t.T, U32)
        rb = lax.broadcasted_iota(jnp.int32, (BUCKET_ROWS, sub), 0).astype(F32)
        counts_ref[...] += jnp.sum(jnp.where(rb == bucket, 1.0, 0.0), axis=-1, keepdims=True)
        yield

    n_stage = 6
    col_products = {}
    tiles = [sub_tile(s) for s in range(n_sub)]
    for step in range(n_sub + n_stage - 1):
        for s in reversed(range(n_sub)):
            if 0 <= step - s < n_stage:
                next(tiles[s])


def _mix(x, y_s5, mods, norm1_g, norm2_g, conv_w, w_conv, w_glu, b_glu, w_out, w_router_t, b_router_t, tm, b0, nb,
         to_cast):
    _, n, d = x.shape
    nt = n // tm
    n_steps = nb * nt
    cast_specs = []
    for w in to_cast:
        per = -(-w.shape[0] // n_steps)
        n_blk = w.shape[0] // per
        assert n_blk * per == w.shape[0]
        cast_specs.append(pl.BlockSpec((per,) + w.shape[1:],
                                       lambda b, i, n_blk=n_blk: ((b * nt + i) * n_blk // n_steps, 0, 0)))
    halo_blocks = n // GRID_W
    per_tile = tm // GRID_W
    tok_out = lambda w: pl.BlockSpec((1, tm, w), lambda b, i: (b, i, 0))
    full = lambda a: pl.BlockSpec(a.shape, lambda b, i: (0,) * a.ndim)
    args = (x, x, x, y_s5, mods, norm1_g.reshape(1, d), norm2_g.reshape(1, d), conv_w, w_conv, w_glu,
            b_glu.reshape(1, -1), w_out, w_router_t, b_router_t)
    in_specs = [pl.BlockSpec((1, tm, d), lambda b, i: (b + b0, i, 0)),
                pl.BlockSpec((1, GRID_W, d), lambda b, i: (b + b0, jnp.maximum(i * per_tile - 1, 0), 0)),
                pl.BlockSpec((1, GRID_W, d),
                             lambda b, i: (b + b0, jnp.minimum((i + 1) * per_tile, halo_blocks - 1), 0)),
                pl.BlockSpec((1, tm, LANES), lambda b, i: (b + b0, i, 0)),
                pl.BlockSpec((1, N_MOD, d), lambda b, i: (b + b0, 0, 0))] + [full(a) for a in args[5:]]
    return pl.pallas_call(
        _mix_kernel,
        grid=(nb, nt),
        in_specs=in_specs + cast_specs,
        out_specs=[tok_out(d), tok_out(d // 2 + ROW_EXTRA),
                   pl.BlockSpec((8, tm), lambda b, i: (0, b * nt + i)),
                   pl.BlockSpec((BUCKET_ROWS, LANES), lambda b, i: (0, 0))] + cast_specs,
        out_shape=[jax.ShapeDtypeStruct((nb, n, d), F32),
                   jax.ShapeDtypeStruct((nb, n, d // 2 + ROW_EXTRA), U32),
                   jax.ShapeDtypeStruct((8, nb * n), F32),
                   jax.ShapeDtypeStruct((BUCKET_ROWS, LANES), F32)]
        + [jax.ShapeDtypeStruct(w.shape, BF16) for w in to_cast],
        compiler_params=_params(("arbitrary", "arbitrary")),
        name="mix",
    )(*args, *to_cast)


def _rank_kernel(route_ref, offs_ref, dest_ref, run_ref):
    tr = route_ref.shape[1]
    n_blk = tr // LANES

    @pl.when(pl.program_id(0) == 0)
    def _():
        run_ref[...] = jnp.zeros_like(run_ref)

    bucket = route_ref[0:1, :]
    rb = lax.broadcasted_iota(jnp.int32, (BUCKET_ROWS, tr), 0).astype(F32)
    onehot = jnp.where(rb == bucket, 1.0, 0.0)
    blocks = [onehot[:, k * LANES:(k + 1) * LANES] for k in range(n_blk)]
    s_idx = lax.broadcasted_iota(jnp.int32, (LANES, LANES), 0)
    t_idx = lax.broadcasted_iota(jnp.int32, (LANES, LANES), 1)
    tri = jnp.where(s_idx <= t_idx, 1.0, 0.0).astype(BF16)
    prefix = jnp.dot(jnp.concatenate(blocks, axis=0).astype(BF16), tri, preferred_element_type=F32)
    start = run_ref[:, 0:1] + offs_ref[:, 0:1]
    carry = start
    for k in range(n_blk):
        pk = prefix[k * BUCKET_ROWS:(k + 1) * BUCKET_ROWS]
        dest = jnp.sum(blocks[k] * (pk - 1.0 + carry), axis=0, keepdims=True)
        dest_ref[:, k * LANES:(k + 1) * LANES] = dest.astype(jnp.int32)
        carry = carry + pk[:, LANES - 1:LANES]
    run_ref[...] += carry - start


def _rank(route_t, offs_rows, tr):
    n = route_t.shape[1]
    return pl.pallas_call(
        _rank_kernel,
        grid=(n // tr,),
        in_specs=[pl.BlockSpec((8, tr), lambda i: (0, i)),
                  pl.BlockSpec((BUCKET_ROWS, LANES), lambda i: (0, 0))],
        out_specs=pl.BlockSpec((1, tr), lambda i: (0, i)),
        out_shape=jax.ShapeDtypeStruct((1, n), jnp.int32),
        scratch_shapes=[pltpu.VMEM((BUCKET_ROWS, LANES), F32)],
        compiler_params=_params(("arbitrary",)),
        name="rank",
    )(route_t, offs_rows)


def _sc_mesh():
    return plsc.VectorSubcoreMesh(core_axis_name="core", subcore_axis_name="subcore")


def _scatter_rows(src, dest, n_out):
    n, w = src.shape
    sub = SC_INDEX_TILE // SC_ROWS

    @functools.partial(pl.kernel, out_type=jax.ShapeDtypeStruct((n_out, w), src.dtype), mesh=_sc_mesh(),
                       scratch_types=[], name="scatter_rows")
    def scatter(x_hbm, i_hbm, o_hbm):
        def body(x_vmem, i_vmem):
            j = pl.program_id(1)
            pltpu.sync_copy(x_vmem, o_hbm.at[i_vmem.at[0, pl.ds(j * SC_ROWS, SC_ROWS)]])

        pltpu.emit_pipeline(
            body,
            grid=(n // SC_INDEX_TILE, sub),
            in_specs=[pl.BlockSpec((SC_ROWS, w), lambda i, j: (i * sub + j, 0)),
                      pl.BlockSpec((1, SC_INDEX_TILE), lambda i, j: (0, i))],
            out_specs=[],
            core_axis_name=("core", "subcore"),
            dimension_semantics=(pltpu.PARALLEL, pltpu.ARBITRARY),
        )(x_hbm, i_hbm)

    return scatter(src, dest)


def _gather_rows(src, idx):
    n = idx.shape[1]
    w = src.shape[1]
    sub = SC_INDEX_TILE // SC_ROWS

    @functools.partial(pl.kernel, out_type=jax.ShapeDtypeStruct((n, w), src.dtype), mesh=_sc_mesh(),
                       scratch_types=[], name="gather_rows")
    def gather(x_hbm, i_hbm, o_hbm):
        def body(i_vmem, o_vmem):
            j = pl.program_id(1)
            pltpu.sync_copy(x_hbm.at[i_vmem.at[0, pl.ds(j * SC_ROWS, SC_ROWS)]], o_vmem)

        pltpu.emit_pipeline(
            body,
            grid=(n // SC_INDEX_TILE, sub),
            in_specs=[pl.BlockSpec((1, SC_INDEX_TILE), lambda i, j: (0, i))],
            out_specs=[pl.BlockSpec((SC_ROWS, w), lambda i, j: (i * sub + j, 0))],
            core_axis_name=("core", "subcore"),
            dimension_semantics=(pltpu.PARALLEL, pltpu.ARBITRARY),
        )(i_hbm, o_hbm)

    return gather(src, idx)


def _moe_kernel(ea_ref, eb_ref, valid_ref, rows_ref, w1a_ref, w3a_ref, w2a_ref, w1b_ref, w3b_ref, w2b_ref, o_ref):
    j = pl.program_id(0)
    half = rows_ref.shape[1] - ROW_EXTRA

    @pl.when(valid_ref[j] != 0)
    def _():
        ha, hb = _unpack_bf16_pair(rows_ref[:, 0:half])
        ha = ha.astype(BF16)
        hb = hb.astype(BF16)
        gates = lax.bitcast_convert_type(rows_ref[:, half:], F32)

        def expert(w1_ref, w3_ref, w2_ref, gate):
            def up(w_ref):
                return (jnp.dot(ha, w_ref[0, 0:half, :], preferred_element_type=F32)
                        + jnp.dot(hb, w_ref[0, half:, :], preferred_element_type=F32))

            a1 = up(w1_ref)
            he = (a1 * jax.nn.sigmoid(a1)) * up(w3_ref) * gate
            return jnp.dot(he.astype(BF16), w2_ref[0], preferred_element_type=F32)

        y = (expert(w1a_ref, w3a_ref, w2a_ref, gates[:, 0:1]) + expert(w1b_ref, w3b_ref, w2b_ref, gates[:, 1:2]))
        o_ref[...] = _pack_bf16_pair(y[:, :half], y[:, half:])


def _moe_grouped(rows, tile_ea, tile_eb, tile_valid, w1, w3, w2, tmm):
    r, w = rows.shape
    n_e, de, d = w2.shape
    up_a = pl.BlockSpec((1, d, de), lambda j, ea, eb, va: (ea[j], 0, 0))
    up_b = pl.BlockSpec((1, d, de), lambda j, ea, eb, va: (eb[j], 0, 0))
    grid_spec = pltpu.PrefetchScalarGridSpec(
        num_scalar_prefetch=3,
        grid=(r // tmm,),
        in_specs=[pl.BlockSpec((tmm, w), lambda j, ea, eb, va: (j, 0)),
                  up_a, up_a, pl.BlockSpec((1, de, d), lambda j, ea, eb, va: (ea[j], 0, 0)),
                  up_b, up_b, pl.BlockSpec((1, de, d), lambda j, ea, eb, va: (eb[j], 0, 0))],
        out_specs=pl.BlockSpec((tmm, d // 2), lambda j, ea, eb, va: (j, 0)),
    )
    return pl.pallas_call(
        _moe_kernel,
        grid_spec=grid_spec,
        out_shape=jax.ShapeDtypeStruct((r, d // 2), U32),
        compiler_params=_params(("arbitrary",)),
        name="moe",
    )(tile_ea, tile_eb, tile_valid, rows, w1, w3, w2, w1, w3, w2)


def _final_kernel(x1_ref, moe_ref, mod_ref, fg_ref, *rest):
    o_ref = rest[-1]
    ya, yb = _unpack_bf16_pair(moe_ref[0])
    half = ya.shape[1]
    gate = mod_ref[0, 5:6, :]
    xa = x1_ref[0, :, 0:half] + gate[:, 0:half] * ya
    xb = x1_ref[0, :, half:] + gate[:, half:] * yb
    ms = (jnp.sum(xa * xa, axis=-1, keepdims=True) + jnp.sum(xb * xb, axis=-1, keepdims=True)) / (2 * half)
    inv = lax.rsqrt(ms + RMS_EPS)
    o_ref[0, :, 0:half] = xa * inv * fg_ref[:, 0:half]
    o_ref[0, :, half:] = xb * inv * fg_ref[:, half:]


def _final(x1, moe_tok, mods, final_g, tm, b0, bsz, out_prev):
    nb, n, d = x1.shape
    tok = lambda w: pl.BlockSpec((1, tm, w), lambda b, i: (b, i, 0))
    args = [x1, moe_tok, mods, final_g.reshape(1, d)]
    in_specs = [tok(d), tok(d // 2), pl.BlockSpec((1, N_MOD, d), lambda b, i: (b + b0, 0, 0)),
                pl.BlockSpec((1, d), lambda b, i: (0, 0))]
    aliases = {}
    if out_prev is not None:
        args.append(out_prev)
        in_specs.append(pl.BlockSpec(memory_space=pl.ANY))
        aliases = {len(args) - 1: 0}
    return pl.pallas_call(
        _final_kernel,
        grid=(nb, n // tm),
        in_specs=in_specs,
        out_specs=pl.BlockSpec((1, tm, d), lambda b, i: (b + b0, i, 0)),
        out_shape=jax.ShapeDtypeStruct((bsz, n, d), F32),
        input_output_aliases=aliases,
        compiler_params=_params(("parallel", "parallel")),
        name="final",
    )(*args)


def _tile_plan(counts, tmm, n_tiles):
    tiles = (counts + (tmm - 1)) // tmm
    tile_end = jnp.cumsum(tiles)
    offs = (tile_end - tiles) * tmm
    n_valid = tile_end[-1]
    j = jnp.arange(n_tiles, dtype=jnp.int32)
    bucket = jnp.sum((tile_end[None, :] <= jnp.minimum(j, n_valid - 1)[:, None]).astype(jnp.int32), axis=1)
    pair_lo = jnp.array([0, 0, 0, 1, 1, 2], jnp.int32)
    pair_hi = jnp.array([1, 2, 3, 2, 3, 3], jnp.int32)
    group = bucket // PAIRS_PER_GROUP
    pair = bucket % PAIRS_PER_GROUP
    tile_ea = group * EXPERTS_PER_GROUP + pair_lo[pair]
    tile_eb = group * EXPERTS_PER_GROUP + pair_hi[pair]
    return offs, tile_ea, tile_eb, (j < n_valid).astype(jnp.int32)


def kernel(x, c, ctx, c_ctx, w_mod, b_mod, norm1_g, norm2_g, w_in, s5_lambda_re, s5_lambda_im, s5_log_dt,
           s5_b_re, s5_b_im, s5_c_re, s5_c_im, s5_d, w_glu, b_glu, conv_w, w_out, router_group_w,
           router_group_b, router_expert_w, router_expert_b, expert_w1, expert_w3, expert_w2, final_g):
    assert w_mod.shape[0] == 1, "single-layer kernel"
    bsz, n_tok, d = x.shape
    n_ctx = ctx.shape[1]
    l = 0
    tm = min(TOKEN_TILE, n_tok)

    n_cond = bsz + 1
    pad = (-n_cond) % 8
    cond = jnp.concatenate([c, c_ctx[None, :], jnp.zeros((pad, d), F32)], axis=0)
    m = _mod_rows(cond, w_mod[l], b_mod[l])
    mx = m[:bsz].reshape(bsz, N_MOD, d)
    mc = m[bsz:bsz + 1].reshape(1, N_MOD, d)

    w_in_b = w_in[l].astype(BF16)
    w_s5 = w_in_b[:, :S5_WIDTH]
    o_c = S5_WIDTH + CONV_WIDTH
    o_v = S5_WIDTH + 2 * CONV_WIDTH
    w_conv = jnp.concatenate(
        [w_in_b[:, S5_WIDTH:o_c], w_in_b[:, o_c:o_c + CONV_ROW_WIDTH], w_in_b[:, o_v:o_v + CONV_ROW_WIDTH],
         w_in_b[:, o_c + CONV_ROW_WIDTH:o_v], w_in_b[:, o_v + CONV_ROW_WIDTH:]], axis=1)
    u = _inproj(x, mx, True, norm1_g[l], w_s5, INPROJ_TILE, "inproj")
    uc = _inproj(ctx, mc, False, norm1_g[l], w_s5, INPROJ_TILE, "inproj_ctx")

    t_mat, mb_pair, mc_pair, a_rows = _s5_matrices(
        s5_lambda_re[l], s5_lambda_im[l], s5_log_dt[l], s5_b_re[l], s5_b_im[l], s5_c_re[l], s5_c_im[l], s5_d[l])
    y_c = _s5_scan(_chunkify(u, "chunkify"), _chunkify(uc, "chunkify_ctx"), t_mat, mb_pair, mc_pair, a_rows)
    y_s5 = _unchunkify(y_c)

    n_logits = N_GROUPS + N_EXPERTS
    w_router = jnp.concatenate(
        [router_group_w[l], router_expert_w[l], jnp.zeros((d, ROUTER_ROWS - n_logits), F32)], axis=1).T
    w_router_hi = w_router.astype(BF16)
    w_router_lo = (w_router - w_router_hi.astype(F32)).astype(BF16)
    w_router_t = jnp.concatenate([w_router_hi, w_router_lo], axis=0)
    b_router = jnp.concatenate([router_group_b[l], router_expert_b[l], jnp.zeros((ROUTER_ROWS - n_logits,), F32)])
    tm_mix = min(MIX_TILE, n_tok)
    b_router_t = jnp.broadcast_to(b_router[:, None], (ROUTER_ROWS, min(MIX_SUB, tm_mix)))

    n_parts = MOE_PARTS if bsz % MOE_PARTS == 0 else 1
    nb = bsz // n_parts
    n_part = nb * n_tok
    n_buckets = N_GROUPS * PAIRS_PER_GROUP
    n_rows = n_part + n_buckets * MOE_TILE
    w_glu_b = w_glu[l].astype(BF16)
    w_out_b = w_out[l].astype(BF16)
    experts_f32 = (expert_w1[l], expert_w3[l], expert_w2[l])
    cast_plan = [experts_f32] if n_parts == 1 else [experts_f32[:2], experts_f32[2:]] + [()] * (n_parts - 2)
    w_experts = []
    staged = []
    for p in range(n_parts):
        x1, h2p, route_t, counts, *w_cast = _mix(x, y_s5, mx, norm1_g[l], norm2_g[l], conv_w[l], w_conv, w_glu_b,
                                                 b_glu[l], w_out_b, w_router_t, b_router_t, tm_mix, p * nb, nb,
                                                 cast_plan[p])
        w_experts += w_cast
        offs, *tiles = _tile_plan(counts[:n_buckets, 0].astype(jnp.int32), MOE_TILE, n_rows // MOE_TILE)
        offs_rows = jnp.zeros((BUCKET_ROWS,), F32).at[:n_buckets].set(offs.astype(F32))
        dest = _rank(route_t, jnp.broadcast_to(offs_rows[:, None], (BUCKET_ROWS, LANES)), min(RANK_TILE, n_part))
        rows = _scatter_rows(h2p.reshape(n_part, d // 2 + ROW_EXTRA), dest, n_rows)
        staged.append((x1, rows, dest, tiles))
    out = None
    for p, (x1, rows, dest, tiles) in enumerate(staged):
        y_rows = _moe_grouped(rows, *tiles, *w_experts, MOE_TILE)
        moe_tok = _gather_rows(y_rows, dest).reshape(nb, n_tok, d // 2)
        out = _final(x1, moe_tok, mx, final_g, tm, p * nb, bsz, out)
    return out
```

```python
import functools

import jax
import jax.numpy as jnp
from jax import lax
from jax.experimental import pallas as pl
from jax.experimental.pallas import tpu as pltpu
from jax.experimental.pallas import tpu_sc as plsc

F32 = jnp.float32
BF16 = jnp.bfloat16
U32 = jnp.uint32

RMS_EPS = 1e-6
N_MOD = 6
GRID_W = 64
S5_WIDTH = 256
S5_H = 16
S5_P = 64
S5_GROUPS = S5_WIDTH // S5_H
S5_PAIRS = S5_GROUPS // 2
S5_CHUNK = 16
LANES = 128
STATE_PITCH = 136
CONV_WIDTH = 768
CONV_ROW_WIDTH = CONV_WIDTH // 2
N_GROUPS = 4
EXPERTS_PER_GROUP = 4
N_EXPERTS = N_GROUPS * EXPERTS_PER_GROUP
PAIRS_PER_GROUP = 6
ROUTER_ROWS = 32
BUCKET_ROWS = 32
ROW_EXTRA = 128
TOKEN_TILE = 2048
MOE_TILE = 512
RANK_TILE = 2048
INPROJ_TILE = 2048
MIX_TILE = 1024
MIX_SUB = 512
MOE_PARTS = 2
LAYOUT_GROUPS = 4
SC_ROWS = 32
SC_INDEX_TILE = 128
VMEM_LIMIT = 52 * 1024 * 1024


def _params(sem, vmem=VMEM_LIMIT):
    return pltpu.CompilerParams(dimension_semantics=sem, vmem_limit_bytes=vmem)


def _pack_bf16_pair(a, b):
    ua = lax.bitcast_convert_type(a.astype(BF16).astype(F32), U32)
    ub = lax.bitcast_convert_type(b.astype(BF16).astype(F32), U32)
    return ua | (ub >> 16)


def _unpack_bf16_pair(w):
    a = lax.bitcast_convert_type(w & jnp.uint32(0xFFFF0000), F32)
    b = lax.bitcast_convert_type(w << 16, F32)
    return a, b


def _mod_kernel(c_ref, w_ref, b_ref, o_ref):
    c = c_ref[...]
    o_ref[...] = jnp.dot(c * jax.nn.sigmoid(c), w_ref[...], preferred_element_type=F32) + b_ref[...]


def _mod_rows(cond, w_mod, b_mod):
    n, d = cond.shape
    nout = w_mod.shape[1]
    bn = d
    return pl.pallas_call(
        _mod_kernel,
        grid=(nout // bn,),
        in_specs=[pl.BlockSpec((n, d), lambda j: (0, 0)),
                  pl.BlockSpec((d, bn), lambda j: (0, j)),
                  pl.BlockSpec((1, bn), lambda j: (0, j))],
        out_specs=pl.BlockSpec((n, bn), lambda j: (0, j)),
        out_shape=jax.ShapeDtypeStruct((n, nout), F32),
        compiler_params=_params(("arbitrary",)),
        name="mod",
    )(cond, w_mod, b_mod.reshape(1, nout))


def _modulated_norm(x, g, shift, scale):
    ms = jnp.mean(x * x, axis=-1, keepdims=True)
    return (x * lax.rsqrt(ms + RMS_EPS)) * (g * (1.0 + scale)) + shift


def _inproj_kernel(x_ref, mod_ref, g_ref, w_ref, u_ref):
    nbk, tm, d = x_ref.shape
    h = _modulated_norm(x_ref[...].reshape(nbk * tm, d), g_ref[...], mod_ref[0, 0:1, :], mod_ref[0, 1:2, :])
    u = jnp.dot(h.astype(BF16), w_ref[...], preferred_element_type=F32)
    packed = _pack_bf16_pair(u[:, 0:LANES], u[:, LANES:2 * LANES])
    for b in range(nbk):
        u_ref[b] = packed[b * tm:(b + 1) * tm]


def _inproj(x, mods, per_batch_mod, norm_g, w_s5, tile, name):
    bsz, n, d = x.shape
    tm = min(tile, n)
    nbk = 1 if per_batch_mod else max(1, min(bsz, tile // n))
    assert bsz % nbk == 0
    mod_map = (lambda b, i: (b, 0, 0)) if per_batch_mod else (lambda b, i: (0, 0, 0))
    assert S5_WIDTH == 2 * LANES
    return pl.pallas_call(
        _inproj_kernel,
        grid=(bsz // nbk, n // tm),
        in_specs=[pl.BlockSpec((nbk, tm, d), lambda b, i: (b, i, 0)),
                  pl.BlockSpec((1, N_MOD, d), mod_map),
                  pl.BlockSpec((1, d), lambda b, i: (0, 0)),
                  pl.BlockSpec((d, S5_WIDTH), lambda b, i: (0, 0))],
        out_specs=pl.BlockSpec((nbk, tm, LANES), lambda b, i: (b, i, 0)),
        out_shape=jax.ShapeDtypeStruct((bsz, n, LANES), U32),
        compiler_params=_params(("parallel", "parallel")),
        name=name,
    )(x, mods, norm_g.reshape(1, d), w_s5)


def _toeplitz_kernel(strip_ref, t_ref):
    lc = t_ref.shape[1] // S5_H
    for s in range(lc):
        off = (lc - 1 - s) * S5_H
        t_ref[0, s * S5_H:(s + 1) * S5_H, :] = strip_ref[0, :, off:off + lc * S5_H].astype(BF16)


def _toeplitz(strip):
    g_n, h_n, w = strip.shape
    n = S5_CHUNK * h_n
    return pl.pallas_call(
        _toeplitz_kernel,
        grid=(g_n,),
        in_specs=[pl.BlockSpec((1, h_n, w), lambda g: (g, 0, 0))],
        out_specs=pl.BlockSpec((1, n, n), lambda g: (g, 0, 0)),
        out_shape=jax.ShapeDtypeStruct((g_n, n, n), BF16),
        compiler_params=_params(("parallel",)),
        name="toeplitz",
    )(strip)


def _s5_matrices(lam_re, lam_im, log_dt, b_re, b_im, c_re, c_im, d_skip):
    lc, g_n, p_n, h_n = S5_CHUNK, S5_GROUPS, S5_P, S5_H
    lam = lax.complex(lam_re.astype(F32), lam_im.astype(F32))
    dt = jnp.exp(log_dt.astype(F32))[..., None]
    a_bar = jnp.exp(lam * dt)
    b_bar = ((a_bar - 1.0) / lam)[..., None] * lax.complex(b_re.astype(F32), b_im.astype(F32))
    cm = lax.complex(c_re.astype(F32), c_im.astype(F32))
    steps = jnp.arange(lc + 1, dtype=F32)
    apow = jnp.exp((lam * dt)[:, :, None, :] * steps[None, None, :, None])
    kern = jnp.einsum('dgop,dgjp,dgpi->dgjio', cm, apow[:, :, :lc], b_bar).real
    skip = jnp.eye(h_n, dtype=F32) * d_skip.astype(F32).reshape(g_n, 1, h_n)
    centre = kern[0, :, 0] + kern[1, :, 0] + skip
    lags = jnp.concatenate([kern[1, :, :0:-1], centre[:, None], kern[0, :, 1:]], axis=1)
    strip = lags.transpose(0, 2, 1, 3).reshape(g_n, h_n, (2 * lc - 1) * h_n)
    strip = jnp.pad(strip, ((0, 0), (0, 0), (0, h_n)))
    t_mat = _toeplitz(strip)

    def in_mat(pw, bb):
        return (pw[:, :, None, :] * bb.transpose(0, 2, 1)[:, None, :, :]).reshape(g_n, lc * h_n, p_n)

    mb_f = in_mat(apow[0, :, lc - 1::-1][:, :lc], b_bar[0])
    mb_b = in_mat(apow[1, :, :lc], b_bar[1])

    def out_mat(pw, cc):
        return (pw.transpose(0, 2, 1)[:, :, :, None] * cc.transpose(0, 2, 1)[:, :, None, :]).reshape(
            g_n, p_n, lc * h_n)

    mc_f = out_mat(apow[0, :, 1:lc + 1], cm[0])
    mc_b = out_mat(apow[1, :, lc:0:-1], cm[1])
    a_chunk = apow[:, :, lc]

    q_n = S5_PAIRS
    zeros_in = jnp.zeros((g_n, lc * h_n, p_n), F32)

    def pair_cols(m):
        m = m.reshape(q_n, 2, lc * h_n, p_n)
        z = zeros_in.reshape(q_n, 2, lc * h_n, p_n)[:, 0]
        top = jnp.concatenate([m[:, 0], z], axis=-1)
        bot = jnp.concatenate([z, m[:, 1]], axis=-1)
        return jnp.concatenate([top, bot], axis=1)

    mb_pair = jnp.concatenate([pair_cols(mb_f.real), pair_cols(mb_f.imag),
                               pair_cols(mb_b.real), pair_cols(mb_b.imag)], axis=-1)

    def pair_rows(m):
        m = m.reshape(q_n, 2, p_n, lc * h_n)
        z = jnp.zeros_like(m[:, 0])
        top = jnp.concatenate([m[:, 0], z], axis=-1)
        bot = jnp.concatenate([z, m[:, 1]], axis=-1)
        return jnp.concatenate([top, bot], axis=1)

    mc_pair = jnp.concatenate([pair_rows(mc_f.real), pair_rows(-mc_f.imag),
                               pair_rows(mc_b.real), pair_rows(-mc_b.imag)], axis=1)
    a_rows = jnp.stack([a_chunk[0].real, a_chunk[0].imag, a_chunk[1].real, a_chunk[1].imag], axis=0)
    a_rows = a_rows.reshape(4, q_n, 2 * p_n).transpose(1, 0, 2)
    a_rows = jnp.concatenate([a_rows, jnp.zeros_like(a_rows)], axis=1)
    return t_mat, mb_pair.astype(BF16), mc_pair.astype(BF16), a_rows


def _chunkify_kernel(u_ref, o_ref, *, gb):
    _, nb, nc, _ = o_ref.shape
    half = S5_CHUNK * S5_H
    per_slab = LANES // S5_H
    for b0 in range(0, nb, gb):
        slabs = ([], [])
        for t in range(S5_CHUNK):
            rows = [u_ref[b0 + b, pl.ds(t, nc, stride=S5_CHUNK), :] for b in range(gb)]
            for j, part in enumerate(_unpack_bf16_pair(rows[0] if gb == 1 else jnp.concatenate(rows, axis=0))):
                slabs[j].append(part.T)
        for j, cols in enumerate(slabs):
            for gl in range(per_slab):
                g = j * per_slab + gl
                m = jnp.concatenate([c[gl * S5_H:(gl + 1) * S5_H, :] for c in cols], axis=0)
                o_ref[g // 2, b0:b0 + gb, :, (g % 2) * half:(g % 2 + 1) * half] = (
                    m.T.astype(BF16).reshape(gb, nc, half))


def _layout_step(bsz, gb):
    return gb * LAYOUT_GROUPS if bsz % (gb * LAYOUT_GROUPS) == 0 else gb


def _chunkify(u_rows, name):
    bsz, n, _ = u_rows.shape
    nc = n // S5_CHUNK
    gb = min(bsz, max(1, LANES // nc))
    nb = _layout_step(bsz, gb)
    w = 2 * S5_CHUNK * S5_H
    return pl.pallas_call(
        functools.partial(_chunkify_kernel, gb=gb),
        grid=(bsz // nb,),
        in_specs=[pl.BlockSpec((nb, n, LANES), lambda b: (b, 0, 0))],
        out_specs=pl.BlockSpec((S5_PAIRS, nb, nc, w), lambda b: (0, b, 0, 0)),
        out_shape=jax.ShapeDtypeStruct((S5_PAIRS, bsz, nc, w), BF16),
        compiler_params=_params(("parallel",)),
        name=name,
    )(u_rows)


def _unchunkify_kernel(y_ref, o_ref):
    _, nb, nc, _ = y_ref.shape
    half = S5_CHUNK * S5_H
    per_slab = LANES // S5_H
    for b in range(nb):
        rows = []
        for g in range(S5_GROUPS):
            rows.append(y_ref[g // 2, b, :, (g % 2) * half:(g % 2 + 1) * half].astype(F32).T)
        for t in range(S5_CHUNK):
            tiles = [jnp.concatenate([r[t * S5_H:(t + 1) * S5_H, :] for r in rows[j * per_slab:(j + 1) * per_slab]],
                                     axis=0).T for j in range(2)]
            o_ref[b, pl.ds(t, nc, stride=S5_CHUNK), :] = _pack_bf16_pair(*tiles)


def _unchunkify(y_c):
    q_n, bsz, nc, w = y_c.shape
    n = nc * S5_CHUNK
    nb = _layout_step(bsz, 1)
    return pl.pallas_call(
        _unchunkify_kernel,
        grid=(bsz // nb,),
        in_specs=[pl.BlockSpec((q_n, nb, nc, w), lambda b: (0, b, 0, 0))],
        out_specs=pl.BlockSpec((nb, n, LANES), lambda b: (b, 0, 0)),
        out_shape=jax.ShapeDtypeStruct((bsz, n, LANES), U32),
        compiler_params=_params(("parallel",)),
        name="unchunkify",
    )(y_c)


def _s5_kernel(u_ref, uc_ref, t_ref, mb_ref, mc_ref, a_ref, y_ref, s_lat, s_ctx, h_scr, *, bb):
    _, bsz, n_lat, w = u_ref.shape
    n_ctx = uc_ref.shape[2]
    n_blk = w // LANES
    rb = bb * n_lat

    mb = mb_ref[0]

    def in_lat(i, carry):
        s = jnp.dot(u_ref[0, pl.ds(i * bb, bb)].reshape(rb, w), mb, preferred_element_type=F32)
        for k in range(bb):
            r = pl.multiple_of((i * bb + k) * STATE_PITCH, 8)
            for blk in range(n_blk):
                s_lat[blk, pl.ds(r, n_lat), :] = s[k * n_lat:(k + 1) * n_lat, blk * LANES:(blk + 1) * LANES]
        return carry

    lax.fori_loop(0, bsz // bb, in_lat, 0)
    sc = jnp.dot(uc_ref[0].reshape(bsz * n_ctx, w), mb, preferred_element_type=F32)
    for blk in range(n_blk):
        s_ctx[blk] = sc[:, blk * LANES:(blk + 1) * LANES]

    a_fr, a_fi, a_br, a_bi = (a_ref[0, k:k + 1, :] for k in range(4))

    def step(h, a_r, a_i, s_r, s_i):
        h_r, h_i = h
        return a_r * h_r - a_i * h_i + s_r, a_r * h_i + a_i * h_r + s_i

    def ctx_rows(blk, c):
        return s_ctx[blk, pl.ds(c, bsz, stride=n_ctx), :]

    def lat_rows(ref, blk, c):
        return ref.at[blk, pl.ds(c, bsz, stride=STATE_PITCH), :]

    def ctx_step(k, carry):
        hf, hb = carry
        kb = n_ctx - 1 - k
        hf = step(hf, a_fr, a_fi, ctx_rows(0, k), ctx_rows(1, k))
        hb = step(hb, a_br, a_bi, ctx_rows(2, kb), ctx_rows(3, kb))
        return hf, hb

    zero = jnp.zeros((bsz, LANES), F32)
    carry = lax.fori_loop(0, n_ctx, ctx_step, ((zero, zero), (zero, zero)))

    def lat_step(k, carry):
        hf, hb = carry
        kb = n_lat - 1 - k
        lat_rows(h_scr, 0, k)[...] = hf[0]
        lat_rows(h_scr, 1, k)[...] = hf[1]
        lat_rows(h_scr, 2, kb)[...] = hb[0]
        lat_rows(h_scr, 3, kb)[...] = hb[1]
        hf = step(hf, a_fr, a_fi, lat_rows(s_lat, 0, k)[...], lat_rows(s_lat, 1, k)[...])
        hb = step(hb, a_br, a_bi, lat_rows(s_lat, 2, kb)[...], lat_rows(s_lat, 3, kb)[...])
        return hf, hb

    lax.fori_loop(0, n_lat, lat_step, carry)

    t0 = t_ref[0]
    t1 = t_ref[1]
    mc = mc_ref[0]
    half = S5_CHUNK * S5_H

    def out_lat(i, carry):
        u = u_ref[0, pl.ds(i * bb, bb)].reshape(rb, w)
        h_rows = []
        for k in range(bb):
            r = pl.multiple_of((i * bb + k) * STATE_PITCH, 8)
            h_rows.append(jnp.concatenate([h_scr[blk, pl.ds(r, n_lat), :] for blk in range(n_blk)], axis=1))
        h = jnp.concatenate(h_rows, axis=0).astype(BF16)
        inter = jnp.dot(h, mc, preferred_element_type=F32)
        y0 = jnp.dot(u[:, :half], t0, preferred_element_type=F32) + inter[:, :half]
        y1 = jnp.dot(u[:, half:], t1, preferred_element_type=F32) + inter[:, half:]
        y = jnp.concatenate([y0, y1], axis=1).astype(BF16)
        y_ref[0, pl.ds(i * bb, bb)] = y.reshape(bb, n_lat, w)
        return carry

    lax.fori_loop(0, bsz // bb, out_lat, 0)


def _s5_scan(u_c, uc_c, t_mat, mb_pair, mc_pair, a_rows):
    q_n, bsz, n_lat, w = u_c.shape
    n_ctx = uc_c.shape[2]
    assert n_lat + 8 == STATE_PITCH
    bb = min(8, bsz)
    n_blk = w // LANES
    return pl.pallas_call(
        functools.partial(_s5_kernel, bb=bb),
        grid=(q_n,),
        in_specs=[pl.BlockSpec((1, bsz, n_lat, w), lambda q: (q, 0, 0, 0)),
                  pl.BlockSpec((1, bsz, n_ctx, w), lambda q: (q, 0, 0, 0)),
                  pl.BlockSpec((2, w // 2, w // 2), lambda q: (q, 0, 0)),
                  pl.BlockSpec((1, w, w), lambda q: (q, 0, 0)),
                  pl.BlockSpec((1, w, w), lambda q: (q, 0, 0)),
                  pl.BlockSpec((1, 8, LANES), lambda q: (q, 0, 0))],
        out_specs=pl.BlockSpec((1, bsz, n_lat, w), lambda q: (q, 0, 0, 0)),
        out_shape=jax.ShapeDtypeStruct((q_n, bsz, n_lat, w), BF16),
        scratch_shapes=[pltpu.VMEM((n_blk, bsz * STATE_PITCH, LANES), F32),
                        pltpu.VMEM((n_blk, bsz * n_ctx, LANES), F32),
                        pltpu.VMEM((n_blk, bsz * STATE_PITCH, LANES), F32)],
        compiler_params=_params(("parallel",)),
        name="s5_scan",
    )(u_c, uc_c, t_mat, mb_pair, mc_pair, a_rows)


def _first_max(rows):
    best = rows[0]
    for r in rows[1:]:
        best = jnp.maximum(best, r)
    idx = jnp.full(best.shape, float(len(rows) - 1), F32)
    for k in range(len(rows) - 2, -1, -1):
        idx = jnp.where(rows[k] == best, float(k), idx)
    return best, idx


def _route_rows(lg):
    g_rows = [lg[k:k + 1] for k in range(N_GROUPS)]
    g_max, g_idx = _first_max(g_rows)
    g_sum = sum(jnp.exp(r - g_max) for r in g_rows)
    g_p = 1.0 / g_sum
    e_rows = []
    for j in range(EXPERTS_PER_GROUP):
        r = lg[N_GROUPS + (N_GROUPS - 1) * EXPERTS_PER_GROUP + j:][:1]
        for g in range(N_GROUPS - 2, -1, -1):
            k = N_GROUPS + g * EXPERTS_PER_GROUP + j
            r = jnp.where(g_idx == float(g), lg[k:k + 1], r)
        e_rows.append(r)
    v1, i1 = _first_max(e_rows)
    rest = [jnp.where(i1 == float(j), -jnp.inf, e_rows[j]) for j in range(EXPERTS_PER_GROUP)]
    v2, i2 = _first_max(rest)
    e21 = jnp.exp(v2 - v1)
    w1 = g_p / (1.0 + e21)
    w2 = w1 * e21
    lo = jnp.minimum(i1, i2)
    hi = jnp.maximum(i1, i2)
    base = jnp.where(lo == 0.0, 0.0, jnp.where(lo == 1.0, 3.0, 5.0))
    bucket = g_idx * float(PAIRS_PER_GROUP) + base + hi - lo - 1.0
    first_is_lo = i1 < i2
    return bucket, jnp.where(first_is_lo, w1, w2), jnp.where(first_is_lo, w2, w1)


def _mix_kernel(x_ref, xup_ref, xdn_ref, y_ref, mod_ref, g1_ref, g2_ref, cw_ref, win_ref, wglu_ref, bglu_ref,
                wout_ref, wr_ref, br_ref, *rest):
    n_cast = (len(rest) - 4) // 2
    x1_ref, h2p_ref, route_ref, counts_ref = rest[n_cast:n_cast + 4]
    for src, dst in zip(rest[:n_cast], rest[n_cast + 4:]):
        dst[...] = src[...].astype(BF16)
    i = pl.program_id(1)
    tm = x_ref.shape[1]
    d = x_ref.shape[2]
    sub = min(MIX_SUB, tm)
    n_sub = tm // sub
    cw, rw = CONV_WIDTH, CONV_ROW_WIDTH

    def hidden(xv):
        return _modulated_norm(xv, g1_ref[...], mod_ref[0, 0:1, :], mod_ref[0, 1:2, :]).astype(BF16)

    def halo(h):
        zh = jnp.dot(h, win_ref[:, 2 * cw:3 * cw], preferred_element_type=F32)
        return zh[:, 0:rw] * zh[:, rw:cw]

    @pl.when(jnp.logical_and(pl.program_id(0) == 0, i == 0))
    def _():
        counts_ref[...] = jnp.zeros_like(counts_ref)

    def sub_tile(s):
        r0 = s * sub
        xv = x_ref[0, r0:r0 + sub, :]
        hx = hidden(xv)
        g = jax.nn.gelu(jnp.concatenate(_unpack_bf16_pair(y_ref[0, r0:r0 + sub, :]), axis=1))
        hx_up = hidden(xup_ref[0]) if s == 0 else None
        hx_dn = hidden(xdn_ref[0]) if s == n_sub - 1 else None
        yield
        z_r = jnp.dot(hx, win_ref[:, cw:2 * cw], preferred_element_type=F32)
        cr = z_r[:, 0:rw] * z_r[:, rw:cw]
        z_c = jnp.dot(hx, win_ref[:, 2 * cw:3 * cw], preferred_element_type=F32)
        cc = z_c[:, 0:rw] * z_c[:, rw:cw]
        col_products[s] = cc
        yield
        bg = jnp.dot(hx, win_ref[:, 0:cw], preferred_element_type=F32)
        if s == 0:
            up_halo = jnp.where(i == 0, 0.0, halo(hx_up))
        if s == n_sub - 1:
            dn_halo = jnp.where(i == pl.num_programs(1) - 1, 0.0, halo(hx_dn))
        glu = g * jax.nn.sigmoid(jnp.dot(g.astype(BF16), wglu_ref[...], preferred_element_type=F32) + bglu_ref[...])
        yield
        row = lax.broadcasted_iota(jnp.int32, (sub, 1), 0)
        col_in_row = row % GRID_W
        left = jnp.where(col_in_row == 0, 0.0, pltpu.roll(cr, 1, axis=0))
        right = jnp.where(col_in_row == GRID_W - 1, 0.0, pltpu.roll(cr, sub - 1, axis=0))
        w_r = cw_ref[:, :rw]
        row_part = left * w_r[0:1] + cr * w_r[1:2] + right * w_r[2:3]
        if s > 0:
            up_halo = col_products[s - 1][sub - GRID_W:]
        if s < n_sub - 1:
            dn_halo = col_products[s + 1][:GRID_W]
        up = jnp.concatenate([up_halo, cc[:sub - GRID_W]], axis=0)
        dn = jnp.concatenate([cc[GRID_W:], dn_halo], axis=0)
        w_c = cw_ref[:, rw:]
        col_part = up * w_c[0:1] + cc * w_c[1:2] + dn * w_c[2:3]
        y_row = (bg[:, 0:rw] * row_part).astype(BF16)
        y_col = (bg[:, rw:cw] * col_part).astype(BF16)
        mixed = jnp.concatenate([glu.astype(BF16), y_row, y_col], axis=1)
        yield
        yx = jnp.dot(mixed, wout_ref[...], preferred_element_type=F32)
        yield
        x1 = xv + mod_ref[0, 2:3, :] * yx
        x1_ref[0, r0:r0 + sub, :] = x1
        h2 = _modulated_norm(x1, g2_ref[...], mod_ref[0, 3:4, :], mod_ref[0, 4:5, :])
        h2b = h2.astype(BF16)
        lg2 = lax.dot_general(wr_ref[...], h2b, (((1,), (1,)), ((), ())), preferred_element_type=F32)
        lg = lg2[:ROUTER_ROWS] + lg2[ROUTER_ROWS:] + br_ref[...]
        bucket, w_a, w_b = _route_rows(lg)
        r8 = lax.broadcasted_iota(jnp.int32, (8, sub), 0)
        route_ref[:, r0:r0 + sub] = jnp.where(r8 == 0, bucket, jnp.where(r8 == 1, w_a, jnp.where(r8 == 2, w_b, 0.0)))
        rl = lax.broadcasted_iota(jnp.int32, (ROW_EXTRA, sub), 0)
        gates_t = jnp.where(rl == 0, w_a, jnp.where(rl == 1, w_b, 0.0))
        h2p_ref[0, r0:r0 + sub, 0:d // 2] = _pack_bf16_pair(h2[:, :d // 2], h2[:, d // 2:])
        h2p_ref[0, r0:r0 + sub, d // 2:] = lax.bitcast_convert_type(gates_t.T, U32)
        rb = lax.broadcasted_iota(jnp.int32, (BUCKET_ROWS, sub), 0).astype(F32)
        counts_ref[...] += jnp.sum(jnp.where(rb == bucket, 1.0, 0.0), axis=-1, keepdims=True)
        yield

    n_stage = 6
    col_products = {}
    tiles = [sub_tile(s) for s in range(n_sub)]
    for step in range(n_sub + n_stage - 1):
        for s in reversed(range(n_sub)):
            if 0 <= step - s < n_stage:
                next(tiles[s])


def _mix(x, y_s5, mods, norm1_g, norm2_g, conv_w, w_conv, w_glu, b_glu, w_out, w_router_t, b_router_t, tm, b0, nb,
         to_cast):
    _, n, d = x.shape
    nt = n // tm
    n_steps = nb * nt
    cast_specs = []
    for w in to_cast:
        per = -(-w.shape[0] // n_steps)
        n_blk = w.shape[0] // per
        assert n_blk * per == w.shape[0]
        cast_specs.append(pl.BlockSpec((per,) + w.shape[1:],
                                       lambda b, i, n_blk=n_blk: ((b * nt + i) * n_blk // n_steps, 0, 0)))
    halo_blocks = n // GRID_W
    per_tile = tm // GRID_W
    tok_out = lambda w: pl.BlockSpec((1, tm, w), lambda b, i: (b, i, 0))
    full = lambda a: pl.BlockSpec(a.shape, lambda b, i: (0,) * a.ndim)
    args = (x, x, x, y_s5, mods, norm1_g.reshape(1, d), norm2_g.reshape(1, d), conv_w, w_conv, w_glu,
            b_glu.reshape(1, -1), w_out, w_router_t, b_router_t)
    in_specs = [pl.BlockSpec((1, tm, d), lambda b, i: (b + b0, i, 0)),
                pl.BlockSpec((1, GRID_W, d), lambda b, i: (b + b0, jnp.maximum(i * per_tile - 1, 0), 0)),
                pl.BlockSpec((1, GRID_W, d),
                             lambda b, i: (b + b0, jnp.minimum((i + 1) * per_tile, halo_blocks - 1), 0)),
                pl.BlockSpec((1, tm, LANES), lambda b, i: (b + b0, i, 0)),
                pl.BlockSpec((1, N_MOD, d), lambda b, i: (b + b0, 0, 0))] + [full(a) for a in args[5:]]
    return pl.pallas_call(
        _mix_kernel,
        grid=(nb, nt),
        in_specs=in_specs + cast_specs,
        out_specs=[tok_out(d), tok_out(d // 2 + ROW_EXTRA),
                   pl.BlockSpec((8, tm), lambda b, i: (0, b * nt + i)),
                   pl.BlockSpec((BUCKET_ROWS, LANES), lambda b, i: (0, 0))] + cast_specs,
        out_shape=[jax.ShapeDtypeStruct((nb, n, d), F32),
                   jax.ShapeDtypeStruct((nb, n, d // 2 + ROW_EXTRA), U32),
                   jax.ShapeDtypeStruct((8, nb * n), F32),
                   jax.ShapeDtypeStruct((BUCKET_ROWS, LANES), F32)]
        + [jax.ShapeDtypeStruct(w.shape, BF16) for w in to_cast],
        compiler_params=_params(("arbitrary", "arbitrary")),
        name="mix",
    )(*args, *to_cast)


def _rank_kernel(route_ref, offs_ref, dest_ref, run_ref):
    tr = route_ref.shape[1]
    n_blk = tr // LANES

    @pl.when(pl.program_id(0) == 0)
    def _():
        run_ref[...] = jnp.zeros_like(run_ref)

    bucket = route_ref[0:1, :]
    rb = lax.broadcasted_iota(jnp.int32, (BUCKET_ROWS, tr), 0).astype(F32)
    onehot = jnp.where(rb == bucket, 1.0, 0.0)
    blocks = [onehot[:, k * LANES:(k + 1) * LANES] for k in range(n_blk)]
    s_idx = lax.broadcasted_iota(jnp.int32, (LANES, LANES), 0)
    t_idx = lax.broadcasted_iota(jnp.int32, (LANES, LANES), 1)
    tri = jnp.where(s_idx <= t_idx, 1.0, 0.0).astype(BF16)
    prefix = jnp.dot(jnp.concatenate(blocks, axis=0).astype(BF16), tri, preferred_element_type=F32)
    start = run_ref[:, 0:1] + offs_ref[:, 0:1]
    carry = start
    for k in range(n_blk):
        pk = prefix[k * BUCKET_ROWS:(k + 1) * BUCKET_ROWS]
        dest = jnp.sum(blocks[k] * (pk - 1.0 + carry), axis=0, keepdims=True)
        dest_ref[:, k * LANES:(k + 1) * LANES] = dest.astype(jnp.int32)
        carry = carry + pk[:, LANES - 1:LANES]
    run_ref[...] += carry - start


def _rank(route_t, offs_rows, tr):
    n = route_t.shape[1]
    return pl.pallas_call(
        _rank_kernel,
        grid=(n // tr,),
        in_specs=[pl.BlockSpec((8, tr), lambda i: (0, i)),
                  pl.BlockSpec((BUCKET_ROWS, LANES), lambda i: (0, 0))],
        out_specs=pl.BlockSpec((1, tr), lambda i: (0, i)),
        out_shape=jax.ShapeDtypeStruct((1, n), jnp.int32),
        scratch_shapes=[pltpu.VMEM((BUCKET_ROWS, LANES), F32)],
        compiler_params=_params(("arbitrary",)),
        name="rank",
    )(route_t, offs_rows)


def _sc_mesh():
    return plsc.VectorSubcoreMesh(core_axis_name="core", subcore_axis_name="subcore")


def _scatter_rows(src, dest, n_out):
    n, w = src.shape
    sub = SC_INDEX_TILE // SC_ROWS

    @functools.partial(pl.kernel, out_type=jax.ShapeDtypeStruct((n_out, w), src.dtype), mesh=_sc_mesh(),
                       scratch_types=[], name="scatter_rows")
    def scatter(x_hbm, i_hbm, o_hbm):
        def body(x_vmem, i_vmem):
            j = pl.program_id(1)
            pltpu.sync_copy(x_vmem, o_hbm.at[i_vmem.at[0, pl.ds(j * SC_ROWS, SC_ROWS)]])

        pltpu.emit_pipeline(
            body,
            grid=(n // SC_INDEX_TILE, sub),
            in_specs=[pl.BlockSpec((SC_ROWS, w), lambda i, j: (i * sub + j, 0)),
                      pl.BlockSpec((1, SC_INDEX_TILE), lambda i, j: (0, i))],
            out_specs=[],
            core_axis_name=("core", "subcore"),
            dimension_semantics=(pltpu.PARALLEL, pltpu.ARBITRARY),
        )(x_hbm, i_hbm)

    return scatter(src, dest)


def _gather_rows(src, idx):
    n = idx.shape[1]
    w = src.shape[1]
    sub = SC_INDEX_TILE // SC_ROWS

    @functools.partial(pl.kernel, out_type=jax.ShapeDtypeStruct((n, w), src.dtype), mesh=_sc_mesh(),
                       scratch_types=[], name="gather_rows")
    def gather(x_hbm, i_hbm, o_hbm):
        def body(i_vmem, o_vmem):
            j = pl.program_id(1)
            pltpu.sync_copy(x_hbm.at[i_vmem.at[0, pl.ds(j * SC_ROWS, SC_ROWS)]], o_vmem)

        pltpu.emit_pipeline(
            body,
            grid=(n // SC_INDEX_TILE, sub),
            in_specs=[pl.BlockSpec((1, SC_INDEX_TILE), lambda i, j: (0, i))],
            out_specs=[pl.BlockSpec((SC_ROWS, w), lambda i, j: (i * sub + j, 0))],
            core_axis_name=("core", "subcore"),
            dimension_semantics=(pltpu.PARALLEL, pltpu.ARBITRARY),
        )(i_hbm, o_hbm)

    return gather(src, idx)


def _moe_kernel(ea_ref, eb_ref, valid_ref, rows_ref, w1a_ref, w3a_ref, w2a_ref, w1b_ref, w3b_ref, w2b_ref, o_ref):
    j = pl.program_id(0)
    half = rows_ref.shape[1] - ROW_EXTRA

    @pl.when(valid_ref[j] != 0)
    def _():
        ha, hb = _unpack_bf16_pair(rows_ref[:, 0:half])
        ha = ha.astype(BF16)
        hb = hb.astype(BF16)
        gates = lax.bitcast_convert_type(rows_ref[:, half:], F32)

        def expert(w1_ref, w3_ref, w2_ref, gate):
            def up(w_ref):
                return (jnp.dot(ha, w_ref[0, 0:half, :], preferred_element_type=F32)
                        + jnp.dot(hb, w_ref[0, half:, :], preferred_element_type=F32))

            a1 = up(w1_ref)
            he = (a1 * jax.nn.sigmoid(a1)) * up(w3_ref) * gate
            return jnp.dot(he.astype(BF16), w2_ref[0], preferred_element_type=F32)

        y = (expert(w1a_ref, w3a_ref, w2a_ref, gates[:, 0:1]) + expert(w1b_ref, w3b_ref, w2b_ref, gates[:, 1:2]))
        o_ref[...] = _pack_bf16_pair(y[:, :half], y[:, half:])


def _moe_grouped(rows, tile_ea, tile_eb, tile_valid, w1, w3, w2, tmm):
    r, w = rows.shape
    n_e, de, d = w2.shape
    up_a = pl.BlockSpec((1, d, de), lambda j, ea, eb, va: (ea[j], 0, 0))
    up_b = pl.BlockSpec((1, d, de), lambda j, ea, eb, va: (eb[j], 0, 0))
    grid_spec = pltpu.PrefetchScalarGridSpec(
        num_scalar_prefetch=3,
        grid=(r // tmm,),
        in_specs=[pl.BlockSpec((tmm, w), lambda j, ea, eb, va: (j, 0)),
                  up_a, up_a, pl.BlockSpec((1, de, d), lambda j, ea, eb, va: (ea[j], 0, 0)),
                  up_b, up_b, pl.BlockSpec((1, de, d), lambda j, ea, eb, va: (eb[j], 0, 0))],
        out_specs=pl.BlockSpec((tmm, d // 2), lambda j, ea, eb, va: (j, 0)),
    )
    return pl.pallas_call(
        _moe_kernel,
        grid_spec=grid_spec,
        out_shape=jax.ShapeDtypeStruct((r, d // 2), U32),
        compiler_params=_params(("arbitrary",)),
        name="moe",
    )(tile_ea, tile_eb, tile_valid, rows, w1, w3, w2, w1, w3, w2)


def _final_kernel(x1_ref, moe_ref, mod_ref, fg_ref, *rest):
    o_ref = rest[-1]
    ya, yb = _unpack_bf16_pair(moe_ref[0])
    half = ya.shape[1]
    gate = mod_ref[0, 5:6, :]
    xa = x1_ref[0, :, 0:half] + gate[:, 0:half] * ya
    xb = x1_ref[0, :, half:] + gate[:, half:] * yb
    ms = (jnp.sum(xa * xa, axis=-1, keepdims=True) + jnp.sum(xb * xb, axis=-1, keepdims=True)) / (2 * half)
    inv = lax.rsqrt(ms + RMS_EPS)
    o_ref[0, :, 0:half] = xa * inv * fg_ref[:, 0:half]
    o_ref[0, :, half:] = xb * inv * fg_ref[:, half:]


def _final(x1, moe_tok, mods, final_g, tm, b0, bsz, out_prev):
    nb, n, d = x1.shape
    tok = lambda w: pl.BlockSpec((1, tm, w), lambda b, i: (b, i, 0))
    args = [x1, moe_tok, mods, final_g.reshape(1, d)]
    in_specs = [tok(d), tok(d // 2), pl.BlockSpec((1, N_MOD, d), lambda b, i: (b + b0, 0, 0)),
                pl.BlockSpec((1, d), lambda b, i: (0, 0))]
    aliases = {}
    if out_prev is not None:
        args.append(out_prev)
        in_specs.append(pl.BlockSpec(memory_space=pl.ANY))
        aliases = {len(args) - 1: 0}
    return pl.pallas_call(
        _final_kernel,
        grid=(nb, n // tm),
        in_specs=in_specs,
        out_specs=pl.BlockSpec((1, tm, d), lambda b, i: (b + b0, i, 0)),
        out_shape=jax.ShapeDtypeStruct((bsz, n, d), F32),
        input_output_aliases=aliases,
        compiler_params=_params(("parallel", "parallel")),
        name="final",
    )(*args)


def _tile_plan(counts, tmm, n_tiles):
    tiles = (counts + (tmm - 1)) // tmm
    tile_end = jnp.cumsum(tiles)
    offs = (tile_end - tiles) * tmm
    n_valid = tile_end[-1]
    j = jnp.arange(n_tiles, dtype=jnp.int32)
    bucket = jnp.sum((tile_end[None, :] <= jnp.minimum(j, n_valid - 1)[:, None]).astype(jnp.int32), axis=1)
    pair_lo = jnp.array([0, 0, 0, 1, 1, 2], jnp.int32)
    pair_hi = jnp.array([1, 2, 3, 2, 3, 3], jnp.int32)
    group = bucket // PAIRS_PER_GROUP
    pair = bucket % PAIRS_PER_GROUP
    tile_ea = group * EXPERTS_PER_GROUP + pair_lo[pair]
    tile_eb = group * EXPERTS_PER_GROUP + pair_hi[pair]
    return offs, tile_ea, tile_eb, (j < n_valid).astype(jnp.int32)


def kernel(x, c, ctx, c_ctx, w_mod, b_mod, norm1_g, norm2_g, w_in, s5_lambda_re, s5_lambda_im, s5_log_dt,
           s5_b_re, s5_b_im, s5_c_re, s5_c_im, s5_d, w_glu, b_glu, conv_w, w_out, router_group_w,
           router_group_b, router_expert_w, router_expert_b, expert_w1, expert_w3, expert_w2, final_g):
    assert w_mod.shape[0] == 1, "single-layer kernel"
    bsz, n_tok, d = x.shape
    n_ctx = ctx.shape[1]
    l = 0
    tm = min(TOKEN_TILE, n_tok)

    n_cond = bsz + 1
    pad = (-n_cond) % 8
    cond = jnp.concatenate([c, c_ctx[None, :], jnp.zeros((pad, d), F32)], axis=0)
    m = _mod_rows(cond, w_mod[l], b_mod[l])
    mx = m[:bsz].reshape(bsz, N_MOD, d)
    mc = m[bsz:bsz + 1].reshape(1, N_MOD, d)

    w_in_b = w_in[l].astype(BF16)
    w_s5 = w_in_b[:, :S5_WIDTH]
    o_c = S5_WIDTH + CONV_WIDTH
    o_v = S5_WIDTH + 2 * CONV_WIDTH
    w_conv = jnp.concatenate(
        [w_in_b[:, S5_WIDTH:o_c], w_in_b[:, o_c:o_c + CONV_ROW_WIDTH], w_in_b[:, o_v:o_v + CONV_ROW_WIDTH],
         w_in_b[:, o_c + CONV_ROW_WIDTH:o_v], w_in_b[:, o_v + CONV_ROW_WIDTH:]], axis=1)
    u = _inproj(x, mx, True, norm1_g[l], w_s5, INPROJ_TILE, "inproj")
    uc = _inproj(ctx, mc, False, norm1_g[l], w_s5, INPROJ_TILE, "inproj_ctx")

    t_mat, mb_pair, mc_pair, a_rows = _s5_matrices(
        s5_lambda_re[l], s5_lambda_im[l], s5_log_dt[l], s5_b_re[l], s5_b_im[l], s5_c_re[l], s5_c_im[l], s5_d[l])
    y_c = _s5_scan(_chunkify(u, "chunkify"), _chunkify(uc, "chunkify_ctx"), t_mat, mb_pair, mc_pair, a_rows)
    y_s5 = _unchunkify(y_c)

    n_logits = N_GROUPS + N_EXPERTS
    w_router = jnp.concatenate(
        [router_group_w[l], router_expert_w[l], jnp.zeros((d, ROUTER_ROWS - n_logits), F32)], axis=1).T
    w_router_hi = w_router.astype(BF16)
    w_router_lo = (w_router - w_router_hi.astype(F32)).astype(BF16)
    w_router_t = jnp.concatenate([w_router_hi, w_router_lo], axis=0)
    b_router = jnp.concatenate([router_group_b[l], router_expert_b[l], jnp.zeros((ROUTER_ROWS - n_logits,), F32)])
    tm_mix = min(MIX_TILE, n_tok)
    b_router_t = jnp.broadcast_to(b_router[:, None], (ROUTER_ROWS, min(MIX_SUB, tm_mix)))

    n_parts = MOE_PARTS if bsz % MOE_PARTS == 0 else 1
    nb = bsz // n_parts
    n_part = nb * n_tok
    n_buckets = N_GROUPS * PAIRS_PER_GROUP
    n_rows = n_part + n_buckets * MOE_TILE
    w_glu_b = w_glu[l].astype(BF16)
    w_out_b = w_out[l].astype(BF16)
    experts_f32 = (expert_w1[l], expert_w3[l], expert_w2[l])
    cast_plan = [experts_f32] if n_parts == 1 else [experts_f32[:2], experts_f32[2:]] + [()] * (n_parts - 2)
    w_experts = []
    staged = []
    for p in range(n_parts):
        x1, h2p, route_t, counts, *w_cast = _mix(x, y_s5, mx, norm1_g[l], norm2_g[l], conv_w[l], w_conv, w_glu_b,
                                                 b_glu[l], w_out_b, w_router_t, b_router_t, tm_mix, p * nb, nb,
                                                 cast_plan[p])
        w_experts += w_cast
        offs, *tiles = _tile_plan(counts[:n_buckets, 0].astype(jnp.int32), MOE_TILE, n_rows // MOE_TILE)
        offs_rows = jnp.zeros((BUCKET_ROWS,), F32).at[:n_buckets].set(offs.astype(F32))
        dest = _rank(route_t, jnp.broadcast_to(offs_rows[:, None], (BUCKET_ROWS, LANES)), min(RANK_TILE, n_part))
        rows = _scatter_rows(h2p.reshape(n_part, d // 2 + ROW_EXTRA), dest, n_rows)
        staged.append((x1, rows, dest, tiles))
    out = None
    for p, (x1, rows, dest, tiles) in enumerate(staged):
        y_rows = _moe_grouped(rows, *tiles, *w_experts, MOE_TILE)
        moe_tok = _gather_rows(y_rows, dest).reshape(nb, n_tok, d // 2)
        out = _final(x1, moe_tok, mx, final_g, tm, p * nb, bsz, out)
    return out
```

```python
import functools

import jax
import jax.numpy as jnp
from jax import lax
from jax.experimental import pallas as pl
from jax.experimental.pallas import tpu as pltpu
from jax.experimental.pallas import tpu_sc as plsc

F32 = jnp.float32
BF16 = jnp.bfloat16
U32 = jnp.uint32

RMS_EPS = 1e-6
N_MOD = 6
GRID_W = 64
S5_WIDTH = 256
S5_H = 16
S5_P = 64
S5_GROUPS = S5_WIDTH // S5_H
S5_PAIRS = S5_GROUPS // 2
S5_CHUNK = 16
LANES = 128
STATE_PITCH = 136
CONV_WIDTH = 768
CONV_ROW_WIDTH = CONV_WIDTH // 2
N_GROUPS = 4
EXPERTS_PER_GROUP = 4
N_EXPERTS = N_GROUPS * EXPERTS_PER_GROUP
PAIRS_PER_GROUP = 6
ROUTER_ROWS = 32
BUCKET_ROWS = 32
ROW_EXTRA = 128
TOKEN_TILE = 2048
MOE_TILE = 512
RANK_TILE = 2048
INPROJ_TILE = 2048
MIX_TILE = 1024
MIX_SUB = 512
MOE_PARTS = 2
LAYOUT_GROUPS = 4
SC_ROWS = 32
SC_INDEX_TILE = 128
VMEM_LIMIT = 52 * 1024 * 1024


def _params(sem, vmem=VMEM_LIMIT):
    return pltpu.CompilerParams(dimension_semantics=sem, vmem_limit_bytes=vmem)


def _pack_bf16_pair(a, b):
    ua = lax.bitcast_convert_type(a.astype(BF16).astype(F32), U32)
    ub = lax.bitcast_convert_type(b.astype(BF16).astype(F32), U32)
    return ua | (ub >> 16)


def _unpack_bf16_pair(w):
    a = lax.bitcast_convert_type(w & jnp.uint32(0xFFFF0000), F32)
    b = lax.bitcast_convert_type(w << 16, F32)
    return a, b


def _mod_kernel(c_ref, w_ref, b_ref, o_ref):
    c = c_ref[...]
    o_ref[...] = jnp.dot(c * jax.nn.sigmoid(c), w_ref[...], preferred_element_type=F32) + b_ref[...]


def _mod_rows(cond, w_mod, b_mod):
    n, d = cond.shape
    nout = w_mod.shape[1]
    bn = d
    return pl.pallas_call(
        _mod_kernel,
        grid=(nout // bn,),
        in_specs=[pl.BlockSpec((n, d), lambda j: (0, 0)),
                  pl.BlockSpec((d, bn), lambda j: (0, j)),
                  pl.BlockSpec((1, bn), lambda j: (0, j))],
        out_specs=pl.BlockSpec((n, bn), lambda j: (0, j)),
        out_shape=jax.ShapeDtypeStruct((n, nout), F32),
        compiler_params=_params(("arbitrary",)),
        name="mod",
    )(cond, w_mod, b_mod.reshape(1, nout))


def _modulated_norm(x, g, shift, scale):
    ms = jnp.mean(x * x, axis=-1, keepdims=True)
    return (x * lax.rsqrt(ms + RMS_EPS)) * (g * (1.0 + scale)) + shift


def _inproj_kernel(x_ref, mod_ref, g_ref, w_ref, u_ref):
    nbk, tm, d = x_ref.shape
    h = _modulated_norm(x_ref[...].reshape(nbk * tm, d), g_ref[...], mod_ref[0, 0:1, :], mod_ref[0, 1:2, :])
    u = jnp.dot(h.astype(BF16), w_ref[...], preferred_element_type=F32)
    packed = _pack_bf16_pair(u[:, 0:LANES], u[:, LANES:2 * LANES])
    for b in range(nbk):
        u_ref[b] = packed[b * tm:(b + 1) * tm]


def _inproj(x, mods, per_batch_mod, norm_g, w_s5, tile, name):
    bsz, n, d = x.shape
    tm = min(tile, n)
    nbk = 1 if per_batch_mod else max(1, min(bsz, tile // n))
    assert bsz % nbk == 0
    mod_map = (lambda b, i: (b, 0, 0)) if per_batch_mod else (lambda b, i: (0, 0, 0))
    assert S5_WIDTH == 2 * LANES
    return pl.pallas_call(
        _inproj_kernel,
        grid=(bsz // nbk, n // tm),
        in_specs=[pl.BlockSpec((nbk, tm, d), lambda b, i: (b, i, 0)),
                  pl.BlockSpec((1, N_MOD, d), mod_map),
                  pl.BlockSpec((1, d), lambda b, i: (0, 0)),
                  pl.BlockSpec((d, S5_WIDTH), lambda b, i: (0, 0))],
        out_specs=pl.BlockSpec((nbk, tm, LANES), lambda b, i: (b, i, 0)),
        out_shape=jax.ShapeDtypeStruct((bsz, n, LANES), U32),
        compiler_params=_params(("parallel", "parallel")),
        name=name,
    )(x, mods, norm_g.reshape(1, d), w_s5)


def _toeplitz_kernel(strip_ref, t_ref):
    lc = t_ref.shape[1] // S5_H
    for s in range(lc):
        off = (lc - 1 - s) * S5_H
        t_ref[0, s * S5_H:(s + 1) * S5_H, :] = strip_ref[0, :, off:off + lc * S5_H].astype(BF16)


def _toeplitz(strip):
    g_n, h_n, w = strip.shape
    n = S5_CHUNK * h_n
    return pl.pallas_call(
        _toeplitz_kernel,
        grid=(g_n,),
        in_specs=[pl.BlockSpec((1, h_n, w), lambda g: (g, 0, 0))],
        out_specs=pl.BlockSpec((1, n, n), lambda g: (g, 0, 0)),
        out_shape=jax.ShapeDtypeStruct((g_n, n, n), BF16),
        compiler_params=_params(("parallel",)),
        name="toeplitz",
    )(strip)


def _s5_matrices(lam_re, lam_im, log_dt, b_re, b_im, c_re, c_im, d_skip):
    lc, g_n, p_n, h_n = S5_CHUNK, S5_GROUPS, S5_P, S5_H
    lam = lax.complex(lam_re.astype(F32), lam_im.astype(F32))
    dt = jnp.exp(log_dt.astype(F32))[..., None]
    a_bar = jnp.exp(lam * dt)
    b_bar = ((a_bar - 1.0) / lam)[..., None] * lax.complex(b_re.astype(F32), b_im.astype(F32))
    cm = lax.complex(c_re.astype(F32), c_im.astype(F32))
    steps = jnp.arange(lc + 1, dtype=F32)
    apow = jnp.exp((lam * dt)[:, :, None, :] * steps[None, None, :, None])
    kern = jnp.einsum('dgop,dgjp,dgpi->dgjio', cm, apow[:, :, :lc], b_bar).real
    skip = jnp.eye(h_n, dtype=F32) * d_skip.astype(F32).reshape(g_n, 1, h_n)
    centre = kern[0, :, 0] + kern[1, :, 0] + skip
    lags = jnp.concatenate([kern[1, :, :0:-1], centre[:, None], kern[0, :, 1:]], axis=1)
    strip = lags.transpose(0, 2, 1, 3).reshape(g_n, h_n, (2 * lc - 1) * h_n)
    strip = jnp.pad(strip, ((0, 0), (0, 0), (0, h_n)))
    t_mat = _toeplitz(strip)

    def in_mat(pw, bb):
        return (pw[:, :, None, :] * bb.transpose(0, 2, 1)[:, None, :, :]).reshape(g_n, lc * h_n, p_n)

    mb_f = in_mat(apow[0, :, lc - 1::-1][:, :lc], b_bar[0])
    mb_b = in_mat(apow[1, :, :lc], b_bar[1])

    def out_mat(pw, cc):
        return (pw.transpose(0, 2, 1)[:, :, :, None] * cc.transpose(0, 2, 1)[:, :, None, :]).reshape(
            g_n, p_n, lc * h_n)

    mc_f = out_mat(apow[0, :, 1:lc + 1], cm[0])
    mc_b = out_mat(apow[1, :, lc:0:-1], cm[1])
    a_chunk = apow[:, :, lc]

    q_n = S5_PAIRS
    zeros_in = jnp.zeros((g_n, lc * h_n, p_n), F32)

    def pair_cols(m):
        m = m.reshape(q_n, 2, lc * h_n, p_n)
        z = zeros_in.reshape(q_n, 2, lc * h_n, p_n)[:, 0]
        top = jnp.concatenate([m[:, 0], z], axis=-1)
        bot = jnp.concatenate([z, m[:, 1]], axis=-1)
        return jnp.concatenate([top, bot], axis=1)

    mb_pair = jnp.concatenate([pair_cols(mb_f.real), pair_cols(mb_f.imag),
                               pair_cols(mb_b.real), pair_cols(mb_b.imag)], axis=-1)

    def pair_rows(m):
        m = m.reshape(q_n, 2, p_n, lc * h_n)
        z = jnp.zeros_like(m[:, 0])
        top = jnp.concatenate([m[:, 0], z], axis=-1)
        bot = jnp.concatenate([z, m[:, 1]], axis=-1)
        return jnp.concatenate([top, bot], axis=1)

    mc_pair = jnp.concatenate([pair_rows(mc_f.real), pair_rows(-mc_f.imag),
                               pair_rows(mc_b.real), pair_rows(-mc_b.imag)], axis=1)
    a_rows = jnp.stack([a_chunk[0].real, a_chunk[0].imag, a_chunk[1].real, a_chunk[1].imag], axis=0)
    a_rows = a_rows.reshape(4, q_n, 2 * p_n).transpose(1, 0, 2)
    a_rows = jnp.concatenate([a_rows, jnp.zeros_like(a_rows)], axis=1)
    return t_mat, mb_pair.astype(BF16), mc_pair.astype(BF16), a_rows


def _chunkify_kernel(u_ref, o_ref, *, gb):
    _, nb, nc, _ = o_ref.shape
    half = S5_CHUNK * S5_H
    per_slab = LANES // S5_H
    for b0 in range(0, nb, gb):
        slabs = ([], [])
        for t in range(S5_CHUNK):
            rows = [u_ref[b0 + b, pl.ds(t, nc, stride=S5_CHUNK), :] for b in range(gb)]
            for j, part in enumerate(_unpack_bf16_pair(rows[0] if gb == 1 else jnp.concatenate(rows, axis=0))):
                slabs[j].append(part.T)
        for j, cols in enumerate(slabs):
            for gl in range(per_slab):
                g = j * per_slab + gl
                m = jnp.concatenate([c[gl * S5_H:(gl + 1) * S5_H, :] for c in cols], axis=0)
                o_ref[g // 2, b0:b0 + gb, :, (g % 2) * half:(g % 2 + 1) * half] = (
                    m.T.astype(BF16).reshape(gb, nc, half))


def _layout_step(bsz, gb):
    return gb * LAYOUT_GROUPS if bsz % (gb * LAYOUT_GROUPS) == 0 else gb


def _chunkify(u_rows, name):
    bsz, n, _ = u_rows.shape
    nc = n // S5_CHUNK
    gb = min(bsz, max(1, LANES // nc))
    nb = _layout_step(bsz, gb)
    w = 2 * S5_CHUNK * S5_H
    return pl.pallas_call(
        functools.partial(_chunkify_kernel, gb=gb),
        grid=(bsz // nb,),
        in_specs=[pl.BlockSpec((nb, n, LANES), lambda b: (b, 0, 0))],
        out_specs=pl.BlockSpec((S5_PAIRS, nb, nc, w), lambda b: (0, b, 0, 0)),
        out_shape=jax.ShapeDtypeStruct((S5_PAIRS, bsz, nc, w), BF16),
        compiler_params=_params(("parallel",)),
        name=name,
    )(u_rows)


def _unchunkify_kernel(y_ref, o_ref):
    _, nb, nc, _ = y_ref.shape
    half = S5_CHUNK * S5_H
    per_slab = LANES // S5_H
    for b in range(nb):
        rows = []
        for g in range(S5_GROUPS):
            rows.append(y_ref[g // 2, b, :, (g % 2) * half:(g % 2 + 1) * half].astype(F32).T)
        for t in range(S5_CHUNK):
            tiles = [jnp.concatenate([r[t * S5_H:(t + 1) * S5_H, :] for r in rows[j * per_slab:(j + 1) * per_slab]],
                                     axis=0).T for j in range(2)]
            o_ref[b, pl.ds(t, nc, stride=S5_CHUNK), :] = _pack_bf16_pair(*tiles)


def _unchunkify(y_c):
    q_n, bsz, nc, w = y_c.shape
    n = nc * S5_CHUNK
    nb = _layout_step(bsz, 1)
    return pl.pallas_call(
        _unchunkify_kernel,
        grid=(bsz // nb,),
        in_specs=[pl.BlockSpec((q_n, nb, nc, w), lambda b: (0, b, 0, 0))],
        out_specs=pl.BlockSpec((nb, n, LANES), lambda b: (b, 0, 0)),
        out_shape=jax.ShapeDtypeStruct((bsz, n, LANES), U32),
        compiler_params=_params(("parallel",)),
        name="unchunkify",
    )(y_c)


def _s5_kernel(u_ref, uc_ref, t_ref, mb_ref, mc_ref, a_ref, y_ref, s_lat, s_ctx, h_scr, *, bb):
    _, bsz, n_lat, w = u_ref.shape
    n_ctx = uc_ref.shape[2]
    n_blk = w // LANES
    rb = bb * n_lat

    mb = mb_ref[0]

    def in_lat(i, carry):
        s = jnp.dot(u_ref[0, pl.ds(i * bb, bb)].reshape(rb, w), mb, preferred_element_type=F32)
        for k in range(bb):
            r = pl.multiple_of((i * bb + k) * STATE_PITCH, 8)
            for blk in range(n_blk):
                s_lat[blk, pl.ds(r, n_lat), :] = s[k * n_lat:(k + 1) * n_lat, blk * LANES:(blk + 1) * LANES]
        return carry

    lax.fori_loop(0, bsz // bb, in_lat, 0)
    sc = jnp.dot(uc_ref[0].reshape(bsz * n_ctx, w), mb, preferred_element_type=F32)
    for blk in range(n_blk):
        s_ctx[blk] = sc[:, blk * LANES:(blk + 1) * LANES]

    a_fr, a_fi, a_br, a_bi = (a_ref[0, k:k + 1, :] for k in range(4))

    def step(h, a_r, a_i, s_r, s_i):
        h_r, h_i = h
        return a_r * h_r - a_i * h_i + s_r, a_r * h_i + a_i * h_r + s_i

    def ctx_rows(blk, c):
        return s_ctx[blk, pl.ds(c, bsz, stride=n_ctx), :]

    def lat_rows(ref, blk, c):
        return ref.at[blk, pl.ds(c, bsz, stride=STATE_PITCH), :]

    def ctx_step(k, carry):
        hf, hb = carry
        kb = n_ctx - 1 - k
        hf = step(hf, a_fr, a_fi, ctx_rows(0, k), ctx_rows(1, k))
        hb = step(hb, a_br, a_bi, ctx_rows(2, kb), ctx_rows(3, kb))
        return hf, hb

    zero = jnp.zeros((bsz, LANES), F32)
    carry = lax.fori_loop(0, n_ctx, ctx_step, ((zero, zero), (zero, zero)))

    def lat_step(k, carry):
        hf, hb = carry
        kb = n_lat - 1 - k
        lat_rows(h_scr, 0, k)[...] = hf[0]
        lat_rows(h_scr, 1, k)[...] = hf[1]
        lat_rows(h_scr, 2, kb)[...] = hb[0]
        lat_rows(h_scr, 3, kb)[...] = hb[1]
        hf = step(hf, a_fr, a_fi, lat_rows(s_lat, 0, k)[...], lat_rows(s_lat, 1, k)[...])
        hb = step(hb, a_br, a_bi, lat_rows(s_lat, 2, kb)[...], lat_rows(s_lat, 3, kb)[...])
        return hf, hb

    lax.fori_loop(0, n_lat, lat_step, carry)

    t0 = t_ref[0]
    t1 = t_ref[1]
    mc = mc_ref[0]
    half = S5_CHUNK * S5_H

    def out_lat(i, carry):
        u = u_ref[0, pl.ds(i * bb, bb)].reshape(rb, w)
        h_rows = []
        for k in range(bb):
            r = pl.multiple_of((i * bb + k) * STATE_PITCH, 8)
            h_rows.append(jnp.concatenate([h_scr[blk, pl.ds(r, n_lat), :] for blk in range(n_blk)], axis=1))
        h = jnp.concatenate(h_rows, axis=0).astype(BF16)
        inter = jnp.dot(h, mc, preferred_element_type=F32)
        y0 = jnp.dot(u[:, :half], t0, preferred_element_type=F32) + inter[:, :half]
        y1 = jnp.dot(u[:, half:], t1, preferred_element_type=F32) + inter[:, half:]
        y = jnp.concatenate([y0, y1], axis=1).astype(BF16)
        y_ref[0, pl.ds(i * bb, bb)] = y.reshape(bb, n_lat, w)
        return carry

    lax.fori_loop(0, bsz // bb, out_lat, 0)


def _s5_scan(u_c, uc_c, t_mat, mb_pair, mc_pair, a_rows):
    q_n, bsz, n_lat, w = u_c.shape
    n_ctx = uc_c.shape[2]
    assert n_lat + 8 == STATE_PITCH
    bb = min(8, bsz)
    n_blk = w // LANES
    return pl.pallas_call(
        functools.partial(_s5_kernel, bb=bb),
        grid=(q_n,),
        in_specs=[pl.BlockSpec((1, bsz, n_lat, w), lambda q: (q, 0, 0, 0)),
                  pl.BlockSpec((1, bsz, n_ctx, w), lambda q: (q, 0, 0, 0)),
                  pl.BlockSpec((2, w // 2, w // 2), lambda q: (q, 0, 0)),
                  pl.BlockSpec((1, w, w), lambda q: (q, 0, 0)),
                  pl.BlockSpec((1, w, w), lambda q: (q, 0, 0)),
                  pl.BlockSpec((1, 8, LANES), lambda q: (q, 0, 0))],
        out_specs=pl.BlockSpec((1, bsz, n_lat, w), lambda q: (q, 0, 0, 0)),
        out_shape=jax.ShapeDtypeStruct((q_n, bsz, n_lat, w), BF16),
        scratch_shapes=[pltpu.VMEM((n_blk, bsz * STATE_PITCH, LANES), F32),
                        pltpu.VMEM((n_blk, bsz * n_ctx, LANES), F32),
                        pltpu.VMEM((n_blk, bsz * STATE_PITCH, LANES), F32)],
        compiler_params=_params(("parallel",)),
        name="s5_scan",
    )(u_c, uc_c, t_mat, mb_pair, mc_pair, a_rows)


def _first_max(rows):
    best = rows[0]
    for r in rows[1:]:
        best = jnp.maximum(best, r)
    idx = jnp.full(best.shape, float(len(rows) - 1), F32)
    for k in range(len(rows) - 2, -1, -1):
        idx = jnp.where(rows[k] == best, float(k), idx)
    return best, idx


def _route_rows(lg):
    g_rows = [lg[k:k + 1] for k in range(N_GROUPS)]
    g_max, g_idx = _first_max(g_rows)
    g_sum = sum(jnp.exp(r - g_max) for r in g_rows)
    g_p = 1.0 / g_sum
    e_rows = []
    for j in range(EXPERTS_PER_GROUP):
        r = lg[N_GROUPS + (N_GROUPS - 1) * EXPERTS_PER_GROUP + j:][:1]
        for g in range(N_GROUPS - 2, -1, -1):
            k = N_GROUPS + g * EXPERTS_PER_GROUP + j
            r = jnp.where(g_idx == float(g), lg[k:k + 1], r)
        e_rows.append(r)
    v1, i1 = _first_max(e_rows)
    rest = [jnp.where(i1 == float(j), -jnp.inf, e_rows[j]) for j in range(EXPERTS_PER_GROUP)]
    v2, i2 = _first_max(rest)
    e21 = jnp.exp(v2 - v1)
    w1 = g_p / (1.0 + e21)
    w2 = w1 * e21
    lo = jnp.minimum(i1, i2)
    hi = jnp.maximum(i1, i2)
    base = jnp.where(lo == 0.0, 0.0, jnp.where(lo == 1.0, 3.0, 5.0))
    bucket = g_idx * float(PAIRS_PER_GROUP) + base + hi - lo - 1.0
    first_is_lo = i1 < i2
    return bucket, jnp.where(first_is_lo, w1, w2), jnp.where(first_is_lo, w2, w1)


def _mix_kernel(x_ref, xup_ref, xdn_ref, y_ref, mod_ref, g1_ref, g2_ref, cw_ref, win_ref, wglu_ref, bglu_ref,
                wout_ref, wr_ref, br_ref, *rest):
    n_cast = (len(rest) - 4) // 2
    x1_ref, h2p_ref, route_ref, counts_ref = rest[n_cast:n_cast + 4]
    for src, dst in zip(rest[:n_cast], rest[n_cast + 4:]):
        dst[...] = src[...].astype(BF16)
    i = pl.program_id(1)
    tm = x_ref.shape[1]
    d = x_ref.shape[2]
    sub = min(MIX_SUB, tm)
    n_sub = tm // sub
    cw, rw = CONV_WIDTH, CONV_ROW_WIDTH

    def hidden(xv):
        return _modulated_norm(xv, g1_ref[...], mod_ref[0, 0:1, :], mod_ref[0, 1:2, :]).astype(BF16)

    def halo(h):
        zh = jnp.dot(h, win_ref[:, 2 * cw:3 * cw], preferred_element_type=F32)
        return zh[:, 0:rw] * zh[:, rw:cw]

    @pl.when(jnp.logical_and(pl.program_id(0) == 0, i == 0))
    def _():
        counts_ref[...] = jnp.zeros_like(counts_ref)

    def sub_tile(s):
        r0 = s * sub
        xv = x_ref[0, r0:r0 + sub, :]
        hx = hidden(xv)
        g = jax.nn.gelu(jnp.concatenate(_unpack_bf16_pair(y_ref[0, r0:r0 + sub, :]), axis=1))
        hx_up = hidden(xup_ref[0]) if s == 0 else None
        hx_dn = hidden(xdn_ref[0]) if s == n_sub - 1 else None
        yield
        z_r = jnp.dot(hx, win_ref[:, cw:2 * cw], preferred_element_type=F32)
        cr = z_r[:, 0:rw] * z_r[:, rw:cw]
        z_c = jnp.dot(hx, win_ref[:, 2 * cw:3 * cw], preferred_element_type=F32)
        cc = z_c[:, 0:rw] * z_c[:, rw:cw]
        col_products[s] = cc
        yield
        bg = jnp.dot(hx, win_ref[:, 0:cw], preferred_element_type=F32)
        if s == 0:
            up_halo = jnp.where(i == 0, 0.0, halo(hx_up))
        if s == n_sub - 1:
            dn_halo = jnp.where(i == pl.num_programs(1) - 1, 0.0, halo(hx_dn))
        glu = g * jax.nn.sigmoid(jnp.dot(g.astype(BF16), wglu_ref[...], preferred_element_type=F32) + bglu_ref[...])
        yield
        row = lax.broadcasted_iota(jnp.int32, (sub, 1), 0)
        col_in_row = row % GRID_W
        left = jnp.where(col_in_row == 0, 0.0, pltpu.roll(cr, 1, axis=0))
        right = jnp.where(col_in_row == GRID_W - 1, 0.0, pltpu.roll(cr, sub - 1, axis=0))
        w_r = cw_ref[:, :rw]
        row_part = left * w_r[0:1] + cr * w_r[1:2] + right * w_r[2:3]
        if s > 0:
            up_halo = col_products[s - 1][sub - GRID_W:]
        if s < n_sub - 1:
            dn_halo = col_products[s + 1][:GRID_W]
        up = jnp.concatenate([up_halo, cc[:sub - GRID_W]], axis=0)
        dn = jnp.concatenate([cc[GRID_W:], dn_halo], axis=0)
        w_c = cw_ref[:, rw:]
        col_part = up * w_c[0:1] + cc * w_c[1:2] + dn * w_c[2:3]
        y_row = (bg[:, 0:rw] * row_part).astype(BF16)
        y_col = (bg[:, rw:cw] * col_part).astype(BF16)
        mixed = jnp.concatenate([glu.astype(BF16), y_row, y_col], axis=1)
        yield
        yx = jnp.dot(mixed, wout_ref[...], preferred_element_type=F32)
        yield
        x1 = xv + mod_ref[0, 2:3, :] * yx
        x1_ref[0, r0:r0 + sub, :] = x1
        h2 = _modulated_norm(x1, g2_ref[...], mod_ref[0, 3:4, :], mod_ref[0, 4:5, :])
        h2b = h2.astype(BF16)
        lg2 = lax.dot_general(wr_ref[...], h2b, (((1,), (1,)), ((), ())), preferred_element_type=F32)
        lg = lg2[:ROUTER_ROWS] + lg2[ROUTER_ROWS:] + br_ref[...]
        bucket, w_a, w_b = _route_rows(lg)
        r8 = lax.broadcasted_iota(jnp.int32, (8, sub), 0)
        route_ref[:, r0:r0 + sub] = jnp.where(r8 == 0, bucket, jnp.where(r8 == 1, w_a, jnp.where(r8 == 2, w_b, 0.0)))
        rl = lax.broadcasted_iota(jnp.int32, (ROW_EXTRA, sub), 0)
        gates_t = jnp.where(rl == 0, w_a, jnp.where(rl == 1, w_b, 0.0))
        h2p_ref[0, r0:r0 + sub, 0:d // 2] = _pack_bf16_pair(h2[:, :d // 2], h2[:, d // 2:])
        h2p_ref[0, r0:r0 + sub, d // 2:] = lax.bitcast_convert_type(gates_t.T, U32)
        rb = lax.broadcasted_iota(jnp.int32, (BUCKET_ROWS, sub), 0).astype(F32)
        counts_ref[...] += jnp.sum(jnp.where(rb == bucket, 1.0, 0.0), axis=-1, keepdims=True)
        yield

    n_stage = 6
    col_products = {}
    tiles = [sub_tile(s) for s in range(n_sub)]
    for step in range(n_sub + n_stage - 1):
        for s in reversed(range(n_sub)):
            if 0 <= step - s < n_stage:
                next(tiles[s])


def _mix(x, y_s5, mods, norm1_g, norm2_g, conv_w, w_conv, w_glu, b_glu, w_out, w_router_t, b_router_t, tm, b0, nb,
         to_cast):
    _, n, d = x.shape
    nt = n // tm
    n_steps = nb * nt
    cast_specs = []
    for w in to_cast:
        per = -(-w.shape[0] // n_steps)
        n_blk = w.shape[0] // per
        assert n_blk * per == w.shape[0]
        cast_specs.append(pl.BlockSpec((per,) + w.shape[1:],
                                       lambda b, i, n_blk=n_blk: ((b * nt + i) * n_blk // n_steps, 0, 0)))
    halo_blocks = n // GRID_W
    per_tile = tm // GRID_W
    tok_out = lambda w: pl.BlockSpec((1, tm, w), lambda b, i: (b, i, 0))
    full = lambda a: pl.BlockSpec(a.shape, lambda b, i: (0,) * a.ndim)
    args = (x, x, x, y_s5, mods, norm1_g.reshape(1, d), norm2_g.reshape(1, d), conv_w, w_conv, w_glu,
            b_glu.reshape(1, -1), w_out, w_router_t, b_router_t)
    in_specs = [pl.BlockSpec((1, tm, d), lambda b, i: (b + b0, i, 0)),
                pl.BlockSpec((1, GRID_W, d), lambda b, i: (b + b0, jnp.maximum(i * per_tile - 1, 0), 0)),
                pl.BlockSpec((1, GRID_W, d),
                             lambda b, i: (b + b0, jnp.minimum((i + 1) * per_tile, halo_blocks - 1), 0)),
                pl.BlockSpec((1, tm, LANES), lambda b, i: (b + b0, i, 0)),
                pl.BlockSpec((1, N_MOD, d), lambda b, i: (b + b0, 0, 0))] + [full(a) for a in args[5:]]
    return pl.pallas_call(
        _mix_kernel,
        grid=(nb, nt),
        in_specs=in_specs + cast_specs,
        out_specs=[tok_out(d), tok_out(d // 2 + ROW_EXTRA),
                   pl.BlockSpec((8, tm), lambda b, i: (0, b * nt + i)),
                   pl.BlockSpec((BUCKET_ROWS, LANES), lambda b, i: (0, 0))] + cast_specs,
        out_shape=[jax.ShapeDtypeStruct((nb, n, d), F32),
                   jax.ShapeDtypeStruct((nb, n, d // 2 + ROW_EXTRA), U32),
                   jax.ShapeDtypeStruct((8, nb * n), F32),
                   jax.ShapeDtypeStruct((BUCKET_ROWS, LANES), F32)]
        + [jax.ShapeDtypeStruct(w.shape, BF16) for w in to_cast],
        compiler_params=_params(("arbitrary", "arbitrary")),
        name="mix",
    )(*args, *to_cast)


def _rank_kernel(route_ref, offs_ref, dest_ref, run_ref):
    tr = route_ref.shape[1]
    n_blk = tr // LANES

    @pl.when(pl.program_id(0) == 0)
    def _():
        run_ref[...] = jnp.zeros_like(run_ref)

    bucket = route_ref[0:1, :]
    rb = lax.broadcasted_iota(jnp.int32, (BUCKET_ROWS, tr), 0).astype(F32)
    onehot = jnp.where(rb == bucket, 1.0, 0.0)
    blocks = [onehot[:, k * LANES:(k + 1) * LANES] for k in range(n_blk)]
    s_idx = lax.broadcasted_iota(jnp.int32, (LANES, LANES), 0)
    t_idx = lax.broadcasted_iota(jnp.int32, (LANES, LANES), 1)
    tri = jnp.where(s_idx <= t_idx, 1.0, 0.0).astype(BF16)
    prefix = jnp.dot(jnp.concatenate(blocks, axis=0).astype(BF16), tri, preferred_element_type=F32)
    start = run_ref[:, 0:1] + offs_ref[:, 0:1]
    carry = start
    for k in range(n_blk):
        pk = prefix[k * BUCKET_ROWS:(k + 1) * BUCKET_ROWS]
        dest = jnp.sum(blocks[k] * (pk - 1.0 + carry), axis=0, keepdims=True)
        dest_ref[:, k * LANES:(k + 1) * LANES] = dest.astype(jnp.int32)
        carry = carry + pk[:, LANES - 1:LANES]
    run_ref[...] += carry - start


def _rank(route_t, offs_rows, tr):
    n = route_t.shape[1]
    return pl.pallas_call(
        _rank_kernel,
        grid=(n // tr,),
        in_specs=[pl.BlockSpec((8, tr), lambda i: (0, i)),
                  pl.BlockSpec((BUCKET_ROWS, LANES), lambda i: (0, 0))],
        out_specs=pl.BlockSpec((1, tr), lambda i: (0, i)),
        out_shape=jax.ShapeDtypeStruct((1, n), jnp.int32),
        scratch_shapes=[pltpu.VMEM((BUCKET_ROWS, LANES), F32)],
        compiler_params=_params(("arbitrary",)),
        name="rank",
    )(route_t, offs_rows)


def _sc_mesh():
    return plsc.VectorSubcoreMesh(core_axis_name="core", subcore_axis_name="subcore")


def _scatter_rows(src, dest, n_out):
    n, w = src.shape
    sub = SC_INDEX_TILE // SC_ROWS

    @functools.partial(pl.kernel, out_type=jax.ShapeDtypeStruct((n_out, w), src.dtype), mesh=_sc_mesh(),
                       scratch_types=[], name="scatter_rows")
    def scatter(x_hbm, i_hbm, o_hbm):
        def body(x_vmem, i_vmem):
            j = pl.program_id(1)
            pltpu.sync_copy(x_vmem, o_hbm.at[i_vmem.at[0, pl.ds(j * SC_ROWS, SC_ROWS)]])

        pltpu.emit_pipeline(
            body,
            grid=(n // SC_INDEX_TILE, sub),
            in_specs=[pl.BlockSpec((SC_ROWS, w), lambda i, j: (i * sub + j, 0)),
                      pl.BlockSpec((1, SC_INDEX_TILE), lambda i, j: (0, i))],
            out_specs=[],
            core_axis_name=("core", "subcore"),
            dimension_semantics=(pltpu.PARALLEL, pltpu.ARBITRARY),
        )(x_hbm, i_hbm)

    return scatter(src, dest)


def _gather_rows(src, idx):
    n = idx.shape[1]
    w = src.shape[1]
    sub = SC_INDEX_TILE // SC_ROWS

    @functools.partial(pl.kernel, out_type=jax.ShapeDtypeStruct((n, w), src.dtype), mesh=_sc_mesh(),
                       scratch_types=[], name="gather_rows")
    def gather(x_hbm, i_hbm, o_hbm):
        def body(i_vmem, o_vmem):
            j = pl.program_id(1)
            pltpu.sync_copy(x_hbm.at[i_vmem.at[0, pl.ds(j * SC_ROWS, SC_ROWS)]], o_vmem)

        pltpu.emit_pipeline(
            body,
            grid=(n // SC_INDEX_TILE, sub),
            in_specs=[pl.BlockSpec((1, SC_INDEX_TILE), lambda i, j: (0, i))],
            out_specs=[pl.BlockSpec((SC_ROWS, w), lambda i, j: (i * sub + j, 0))],
            core_axis_name=("core", "subcore"),
            dimension_semantics=(pltpu.PARALLEL, pltpu.ARBITRARY),
        )(i_hbm, o_hbm)

    return gather(src, idx)


def _moe_kernel(xa_ref, xb_ref, ya_ref, yb_ref, valid_ref, slot_ref, rows_ref, *rest):
    o_ref = rest[-1]
    j = pl.program_id(0)
    half = rows_ref.shape[1] - ROW_EXTRA

    def run(w1a_ref, w3a_ref, w2a_ref, w1b_ref, w3b_ref, w2b_ref):
        ha, hb = _unpack_bf16_pair(rows_ref[:, 0:half])
        ha = ha.astype(BF16)
        hb = hb.astype(BF16)
        gates = lax.bitcast_convert_type(rows_ref[:, half:], F32)

        def expert(w1_ref, w3_ref, w2_ref, gate):
            def up(w_ref):
                return (jnp.dot(ha, w_ref[0, 0:half, :], preferred_element_type=F32)
                        + jnp.dot(hb, w_ref[0, half:, :], preferred_element_type=F32))

            a1 = up(w1_ref)
            he = (a1 * jax.nn.sigmoid(a1)) * up(w3_ref) * gate
            return jnp.dot(he.astype(BF16), w2_ref[0], preferred_element_type=F32)

        y = (expert(w1a_ref, w3a_ref, w2a_ref, gates[:, 0:1]) + expert(w1b_ref, w3b_ref, w2b_ref, gates[:, 1:2]))
        o_ref[...] = _pack_bf16_pair(y[:, :half], y[:, half:])

    for slot in range(2):
        pl.when(jnp.logical_and(valid_ref[j] != 0, slot_ref[j] == slot))(
            functools.partial(run, *rest[6 * slot:6 * slot + 6]))


def _moe_grouped(rows, slots, tile_valid, tile_slot, w1, w3, w2, tmm):
    r, w = rows.shape
    n_e, de, d = w2.shape

    def expert_specs(k):
        up = pl.BlockSpec((1, d, de), lambda j, *sc: (sc[k][j], 0, 0))
        return [up, up, pl.BlockSpec((1, de, d), lambda j, *sc: (sc[k][j], 0, 0))]

    grid_spec = pltpu.PrefetchScalarGridSpec(
        num_scalar_prefetch=6,
        grid=(r // tmm,),
        in_specs=[pl.BlockSpec((tmm, w), lambda j, *sc: (j, 0))] + sum([expert_specs(k) for k in range(4)], []),
        out_specs=pl.BlockSpec((tmm, d // 2), lambda j, *sc: (j, 0)),
    )
    return pl.pallas_call(
        _moe_kernel,
        grid_spec=grid_spec,
        out_shape=jax.ShapeDtypeStruct((r, d // 2), U32),
        compiler_params=_params(("arbitrary",)),
        name="moe",
    )(*slots, tile_valid, tile_slot, rows, *([w1, w3, w2] * 4))


def _final_kernel(x1_ref, moe_ref, mod_ref, fg_ref, *rest):
    o_ref = rest[-1]
    ya, yb = _unpack_bf16_pair(moe_ref[0])
    half = ya.shape[1]
    gate = mod_ref[0, 5:6, :]
    xa = x1_ref[0, :, 0:half] + gate[:, 0:half] * ya
    xb = x1_ref[0, :, half:] + gate[:, half:] * yb
    ms = (jnp.sum(xa * xa, axis=-1, keepdims=True) + jnp.sum(xb * xb, axis=-1, keepdims=True)) / (2 * half)
    inv = lax.rsqrt(ms + RMS_EPS)
    o_ref[0, :, 0:half] = xa * inv * fg_ref[:, 0:half]
    o_ref[0, :, half:] = xb * inv * fg_ref[:, half:]


def _final(x1, moe_tok, mods, final_g, tm, b0, bsz, out_prev):
    nb, n, d = x1.shape
    tok = lambda w: pl.BlockSpec((1, tm, w), lambda b, i: (b, i, 0))
    args = [x1, moe_tok, mods, final_g.reshape(1, d)]
    in_specs = [tok(d), tok(d // 2), pl.BlockSpec((1, N_MOD, d), lambda b, i: (b + b0, 0, 0)),
                pl.BlockSpec((1, d), lambda b, i: (0, 0))]
    aliases = {}
    if out_prev is not None:
        args.append(out_prev)
        in_specs.append(pl.BlockSpec(memory_space=pl.ANY))
        aliases = {len(args) - 1: 0}
    return pl.pallas_call(
        _final_kernel,
        grid=(nb, n // tm),
        in_specs=in_specs,
        out_specs=pl.BlockSpec((1, tm, d), lambda b, i: (b + b0, i, 0)),
        out_shape=jax.ShapeDtypeStruct((bsz, n, d), F32),
        input_output_aliases=aliases,
        compiler_params=_params(("parallel", "parallel")),
        name="final",
    )(*args)


def _tile_plan(counts, tmm, n_tiles):
    n_b = counts.shape[0]
    tiles = (counts + (tmm - 1)) // tmm
    tile_end = jnp.cumsum(tiles)
    offs = (tile_end - tiles) * tmm
    n_valid = tile_end[-1]
    j = jnp.arange(n_tiles, dtype=jnp.int32)
    bucket = jnp.sum((tile_end[None, :] <= jnp.minimum(j, n_valid - 1)[:, None]).astype(jnp.int32), axis=1)
    nonempty = tiles > 0
    order = jnp.cumsum(nonempty.astype(jnp.int32)) - 1
    ids = jnp.arange(n_b, dtype=jnp.int32)
    later = jnp.where(jnp.logical_and(nonempty[None, :], ids[None, :] > ids[:, None]), ids[None, :], n_b)
    nxt = jnp.min(later, axis=1)
    nxt = jnp.where(nxt == n_b, ids, nxt)
    slot = order[bucket] % 2
    x_bucket = jnp.where(slot == 0, bucket, nxt[bucket])
    y_bucket = jnp.where(slot == 1, bucket, nxt[bucket])
    pair_lo = jnp.array([0, 0, 0, 1, 1, 2], jnp.int32)
    pair_hi = jnp.array([1, 2, 3, 2, 3, 3], jnp.int32)

    def experts(bk):
        group = bk // PAIRS_PER_GROUP
        pair = bk % PAIRS_PER_GROUP
        return group * EXPERTS_PER_GROUP + pair_lo[pair], group * EXPERTS_PER_GROUP + pair_hi[pair]

    return offs, experts(x_bucket) + experts(y_bucket), (j < n_valid).astype(jnp.int32), slot


def kernel(x, c, ctx, c_ctx, w_mod, b_mod, norm1_g, norm2_g, w_in, s5_lambda_re, s5_lambda_im, s5_log_dt,
           s5_b_re, s5_b_im, s5_c_re, s5_c_im, s5_d, w_glu, b_glu, conv_w, w_out, router_group_w,
           router_group_b, router_expert_w, router_expert_b, expert_w1, expert_w3, expert_w2, final_g):
    assert w_mod.shape[0] == 1, "single-layer kernel"
    bsz, n_tok, d = x.shape
    n_ctx = ctx.shape[1]
    l = 0
    tm = min(TOKEN_TILE, n_tok)

    n_cond = bsz + 1
    pad = (-n_cond) % 8
    cond = jnp.concatenate([c, c_ctx[None, :], jnp.zeros((pad, d), F32)], axis=0)
    m = _mod_rows(cond, w_mod[l], b_mod[l])
    mx = m[:bsz].reshape(bsz, N_MOD, d)
    mc = m[bsz:bsz + 1].reshape(1, N_MOD, d)

    w_in_b = w_in[l].astype(BF16)
    w_s5 = w_in_b[:, :S5_WIDTH]
    o_c = S5_WIDTH + CONV_WIDTH
    o_v = S5_WIDTH + 2 * CONV_WIDTH
    w_conv = jnp.concatenate(
        [w_in_b[:, S5_WIDTH:o_c], w_in_b[:, o_c:o_c + CONV_ROW_WIDTH], w_in_b[:, o_v:o_v + CONV_ROW_WIDTH],
         w_in_b[:, o_c + CONV_ROW_WIDTH:o_v], w_in_b[:, o_v + CONV_ROW_WIDTH:]], axis=1)
    u = _inproj(x, mx, True, norm1_g[l], w_s5, INPROJ_TILE, "inproj")
    uc = _inproj(ctx, mc, False, norm1_g[l], w_s5, INPROJ_TILE, "inproj_ctx")

    t_mat, mb_pair, mc_pair, a_rows = _s5_matrices(
        s5_lambda_re[l], s5_lambda_im[l], s5_log_dt[l], s5_b_re[l], s5_b_im[l], s5_c_re[l], s5_c_im[l], s5_d[l])
    y_c = _s5_scan(_chunkify(u, "chunkify"), _chunkify(uc, "chunkify_ctx"), t_mat, mb_pair, mc_pair, a_rows)
    y_s5 = _unchunkify(y_c)

    n_logits = N_GROUPS + N_EXPERTS
    w_router = jnp.concatenate(
        [router_group_w[l], router_expert_w[l], jnp.zeros((d, ROUTER_ROWS - n_logits), F32)], axis=1).T
    w_router_hi = w_router.astype(BF16)
    w_router_lo = (w_router - w_router_hi.astype(F32)).astype(BF16)
    w_router_t = jnp.concatenate([w_router_hi, w_router_lo], axis=0)
    b_router = jnp.concatenate([router_group_b[l], router_expert_b[l], jnp.zeros((ROUTER_ROWS - n_logits,), F32)])
    tm_mix = min(MIX_TILE, n_tok)
    b_router_t = jnp.broadcast_to(b_router[:, None], (ROUTER_ROWS, min(MIX_SUB, tm_mix)))

    n_parts = MOE_PARTS if bsz % MOE_PARTS == 0 else 1
    nb = bsz // n_parts
    n_part = nb * n_tok
    n_buckets = N_GROUPS * PAIRS_PER_GROUP
    n_rows = n_part + n_buckets * MOE_TILE
    w_glu_b = w_glu[l].astype(BF16)
    w_out_b = w_out[l].astype(BF16)
    experts_f32 = (expert_w1[l], expert_w3[l], expert_w2[l])
    cast_plan = [experts_f32] if n_parts == 1 else [experts_f32[:2], experts_f32[2:]] + [()] * (n_parts - 2)
    w_experts = []
    staged = []
    for p in range(n_parts):
        x1, h2p, route_t, counts, *w_cast = _mix(x, y_s5, mx, norm1_g[l], norm2_g[l], conv_w[l], w_conv, w_glu_b,
                                                 b_glu[l], w_out_b, w_router_t, b_router_t, tm_mix, p * nb, nb,
                                                 cast_plan[p])
        w_experts += w_cast
        offs, *tiles = _tile_plan(counts[:n_buckets, 0].astype(jnp.int32), MOE_TILE, n_rows // MOE_TILE)
        offs_rows = jnp.zeros((BUCKET_ROWS,), F32).at[:n_buckets].set(offs.astype(F32))
        dest = _rank(route_t, jnp.broadcast_to(offs_rows[:, None], (BUCKET_ROWS, LANES)), min(RANK_TILE, n_part))
        rows = _scatter_rows(h2p.reshape(n_part, d // 2 + ROW_EXTRA), dest, n_rows)
        staged.append((x1, rows, dest, tiles))
    out = None
    for p, (x1, rows, dest, tiles) in enumerate(staged):
        y_rows = _moe_grouped(rows, *tiles, *w_experts, MOE_TILE)
        moe_tok = _gather_rows(y_rows, dest).reshape(nb, n_tok, d // 2)
        out = _final(x1, moe_tok, mx, final_g, tm, p * nb, bsz, out)
    return out
```

```python
import functools

import jax
import jax.numpy as jnp
from jax import lax
from jax.experimental import pallas as pl
from jax.experimental.pallas import tpu as pltpu
from jax.experimental.pallas import tpu_sc as plsc

F32 = jnp.float32
BF16 = jnp.bfloat16
U32 = jnp.uint32

RMS_EPS = 1e-6
N_MOD = 6
GRID_W = 64
S5_WIDTH = 256
S5_H = 16
S5_P = 64
S5_GROUPS = S5_WIDTH // S5_H
S5_PAIRS = S5_GROUPS // 2
S5_CHUNK = 16
LANES = 128
STATE_PITCH = 136
CONV_WIDTH = 768
CONV_ROW_WIDTH = CONV_WIDTH // 2
N_GROUPS = 4
EXPERTS_PER_GROUP = 4
N_EXPERTS = N_GROUPS * EXPERTS_PER_GROUP
PAIRS_PER_GROUP = 6
ROUTER_ROWS = 32
BUCKET_ROWS = 32
ROW_EXTRA = 128
TOKEN_TILE = 2048
MOE_TILE = 512
RANK_TILE = 2048
INPROJ_TILE = 2048
MIX_TILE = 1024
MIX_SUB = 512
MOE_PARTS = 2
LAYOUT_GROUPS = 4
SC_ROWS = 64
SC_INDEX_TILE = 128
VMEM_LIMIT = 52 * 1024 * 1024


def _params(sem, vmem=VMEM_LIMIT):
    return pltpu.CompilerParams(dimension_semantics=sem, vmem_limit_bytes=vmem)


def _pack_bf16_pair(a, b):
    ua = lax.bitcast_convert_type(a.astype(BF16).astype(F32), U32)
    ub = lax.bitcast_convert_type(b.astype(BF16).astype(F32), U32)
    return ua | (ub >> 16)


def _unpack_bf16_pair(w):
    a = lax.bitcast_convert_type(w & jnp.uint32(0xFFFF0000), F32)
    b = lax.bitcast_convert_type(w << 16, F32)
    return a, b


def _mod_kernel(c_ref, w_ref, b_ref, o_ref):
    c = c_ref[...]
    o_ref[...] = jnp.dot(c * jax.nn.sigmoid(c), w_ref[...], preferred_element_type=F32) + b_ref[...]


def _mod_rows(cond, w_mod, b_mod):
    n, d = cond.shape
    nout = w_mod.shape[1]
    bn = d
    return pl.pallas_call(
        _mod_kernel,
        grid=(nout // bn,),
        in_specs=[pl.BlockSpec((n, d), lambda j: (0, 0)),
                  pl.BlockSpec((d, bn), lambda j: (0, j)),
                  pl.BlockSpec((1, bn), lambda j: (0, j))],
        out_specs=pl.BlockSpec((n, bn), lambda j: (0, j)),
        out_shape=jax.ShapeDtypeStruct((n, nout), F32),
        compiler_params=_params(("arbitrary",)),
        name="mod",
    )(cond, w_mod, b_mod.reshape(1, nout))


def _modulated_norm(x, g, shift, scale):
    ms = jnp.mean(x * x, axis=-1, keepdims=True)
    return (x * lax.rsqrt(ms + RMS_EPS)) * (g * (1.0 + scale)) + shift


def _inproj_kernel(x_ref, mod_ref, g_ref, w_ref, u_ref):
    nbk, tm, d = x_ref.shape
    h = _modulated_norm(x_ref[...].reshape(nbk * tm, d), g_ref[...], mod_ref[0, 0:1, :], mod_ref[0, 1:2, :])
    u = jnp.dot(h.astype(BF16), w_ref[...], preferred_element_type=F32)
    packed = _pack_bf16_pair(u[:, 0:LANES], u[:, LANES:2 * LANES])
    for b in range(nbk):
        u_ref[b] = packed[b * tm:(b + 1) * tm]


def _inproj(x, mods, per_batch_mod, norm_g, w_s5, tile, name):
    bsz, n, d = x.shape
    tm = min(tile, n)
    nbk = 1 if per_batch_mod else max(1, min(bsz, tile // n))
    assert bsz % nbk == 0
    mod_map = (lambda b, i: (b, 0, 0)) if per_batch_mod else (lambda b, i: (0, 0, 0))
    assert S5_WIDTH == 2 * LANES
    return pl.pallas_call(
        _inproj_kernel,
        grid=(bsz // nbk, n // tm),
        in_specs=[pl.BlockSpec((nbk, tm, d), lambda b, i: (b, i, 0)),
                  pl.BlockSpec((1, N_MOD, d), mod_map),
                  pl.BlockSpec((1, d), lambda b, i: (0, 0)),
                  pl.BlockSpec((d, S5_WIDTH), lambda b, i: (0, 0))],
        out_specs=pl.BlockSpec((nbk, tm, LANES), lambda b, i: (b, i, 0)),
        out_shape=jax.ShapeDtypeStruct((bsz, n, LANES), U32),
        compiler_params=_params(("parallel", "parallel")),
        name=name,
    )(x, mods, norm_g.reshape(1, d), w_s5)


def _toeplitz_kernel(strip_ref, t_ref):
    lc = t_ref.shape[1] // S5_H
    for s in range(lc):
        off = (lc - 1 - s) * S5_H
        t_ref[0, s * S5_H:(s + 1) * S5_H, :] = strip_ref[0, :, off:off + lc * S5_H].astype(BF16)


def _toeplitz(strip):
    g_n, h_n, w = strip.shape
    n = S5_CHUNK * h_n
    return pl.pallas_call(
        _toeplitz_kernel,
        grid=(g_n,),
        in_specs=[pl.BlockSpec((1, h_n, w), lambda g: (g, 0, 0))],
        out_specs=pl.BlockSpec((1, n, n), lambda g: (g, 0, 0)),
        out_shape=jax.ShapeDtypeStruct((g_n, n, n), BF16),
        compiler_params=_params(("parallel",)),
        name="toeplitz",
    )(strip)


def _s5_matrices(lam_re, lam_im, log_dt, b_re, b_im, c_re, c_im, d_skip):
    lc, g_n, p_n, h_n = S5_CHUNK, S5_GROUPS, S5_P, S5_H
    lam = lax.complex(lam_re.astype(F32), lam_im.astype(F32))
    dt = jnp.exp(log_dt.astype(F32))[..., None]
    a_bar = jnp.exp(lam * dt)
    b_bar = ((a_bar - 1.0) / lam)[..., None] * lax.complex(b_re.astype(F32), b_im.astype(F32))
    cm = lax.complex(c_re.astype(F32), c_im.astype(F32))
    steps = jnp.arange(lc + 1, dtype=F32)
    apow = jnp.exp((lam * dt)[:, :, None, :] * steps[None, None, :, None])
    kern = jnp.einsum('dgop,dgjp,dgpi->dgjio', cm, apow[:, :, :lc], b_bar).real
    skip = jnp.eye(h_n, dtype=F32) * d_skip.astype(F32).reshape(g_n, 1, h_n)
    centre = kern[0, :, 0] + kern[1, :, 0] + skip
    lags = jnp.concatenate([kern[1, :, :0:-1], centre[:, None], kern[0, :, 1:]], axis=1)
    strip = lags.transpose(0, 2, 1, 3).reshape(g_n, h_n, (2 * lc - 1) * h_n)
    strip = jnp.pad(strip, ((0, 0), (0, 0), (0, h_n)))
    t_mat = _toeplitz(strip)

    def in_mat(pw, bb):
        return (pw[:, :, None, :] * bb.transpose(0, 2, 1)[:, None, :, :]).reshape(g_n, lc * h_n, p_n)

    mb_f = in_mat(apow[0, :, lc - 1::-1][:, :lc], b_bar[0])
    mb_b = in_mat(apow[1, :, :lc], b_bar[1])

    def out_mat(pw, cc):
        return (pw.transpose(0, 2, 1)[:, :, :, None] * cc.transpose(0, 2, 1)[:, :, None, :]).reshape(
            g_n, p_n, lc * h_n)

    mc_f = out_mat(apow[0, :, 1:lc + 1], cm[0])
    mc_b = out_mat(apow[1, :, lc:0:-1], cm[1])
    a_chunk = apow[:, :, lc]

    q_n = S5_PAIRS
    zeros_in = jnp.zeros((g_n, lc * h_n, p_n), F32)

    def pair_cols(m):
        m = m.reshape(q_n, 2, lc * h_n, p_n)
        z = zeros_in.reshape(q_n, 2, lc * h_n, p_n)[:, 0]
        top = jnp.concatenate([m[:, 0], z], axis=-1)
        bot = jnp.concatenate([z, m[:, 1]], axis=-1)
        return jnp.concatenate([top, bot], axis=1)

    mb_pair = jnp.concatenate([pair_cols(mb_f.real), pair_cols(mb_f.imag),
                               pair_cols(mb_b.real), pair_cols(mb_b.imag)], axis=-1)

    def pair_rows(m):
        m = m.reshape(q_n, 2, p_n, lc * h_n)
        z = jnp.zeros_like(m[:, 0])
        top = jnp.concatenate([m[:, 0], z], axis=-1)
        bot = jnp.concatenate([z, m[:, 1]], axis=-1)
        return jnp.concatenate([top, bot], axis=1)

    mc_pair = jnp.concatenate([pair_rows(mc_f.real), pair_rows(-mc_f.imag),
                               pair_rows(mc_b.real), pair_rows(-mc_b.imag)], axis=1)
    a_rows = jnp.stack([a_chunk[0].real, a_chunk[0].imag, a_chunk[1].real, a_chunk[1].imag], axis=0)
    a_rows = a_rows.reshape(4, q_n, 2 * p_n).transpose(1, 0, 2)
    a_rows = jnp.concatenate([a_rows, jnp.zeros_like(a_rows)], axis=1)
    return t_mat, mb_pair.astype(BF16), mc_pair.astype(BF16), a_rows


def _chunkify_kernel(u_ref, o_ref, *, gb):
    _, nb, nc, _ = o_ref.shape
    half = S5_CHUNK * S5_H
    per_slab = LANES // S5_H
    for b0 in range(0, nb, gb):
        slabs = ([], [])
        for t in range(S5_CHUNK):
            rows = [u_ref[b0 + b, pl.ds(t, nc, stride=S5_CHUNK), :] for b in range(gb)]
            for j, part in enumerate(_unpack_bf16_pair(rows[0] if gb == 1 else jnp.concatenate(rows, axis=0))):
                slabs[j].append(part.T)
        for j, cols in enumerate(slabs):
            for gl in range(per_slab):
                g = j * per_slab + gl
                m = jnp.concatenate([c[gl * S5_H:(gl + 1) * S5_H, :] for c in cols], axis=0)
                o_ref[g // 2, b0:b0 + gb, :, (g % 2) * half:(g % 2 + 1) * half] = (
                    m.T.astype(BF16).reshape(gb, nc, half))


def _layout_step(bsz, gb):
    return gb * LAYOUT_GROUPS if bsz % (gb * LAYOUT_GROUPS) == 0 else gb


def _chunkify(u_rows, name):
    bsz, n, _ = u_rows.shape
    nc = n // S5_CHUNK
    gb = min(bsz, max(1, LANES // nc))
    nb = _layout_step(bsz, gb)
    w = 2 * S5_CHUNK * S5_H
    return pl.pallas_call(
        functools.partial(_chunkify_kernel, gb=gb),
        grid=(bsz // nb,),
        in_specs=[pl.BlockSpec((nb, n, LANES), lambda b: (b, 0, 0))],
        out_specs=pl.BlockSpec((S5_PAIRS, nb, nc, w), lambda b: (0, b, 0, 0)),
        out_shape=jax.ShapeDtypeStruct((S5_PAIRS, bsz, nc, w), BF16),
        compiler_params=_params(("parallel",)),
        name=name,
    )(u_rows)


def _unchunkify_kernel(y_ref, o_ref):
    _, nb, nc, _ = y_ref.shape
    half = S5_CHUNK * S5_H
    per_slab = LANES // S5_H
    for b in range(nb):
        rows = []
        for g in range(S5_GROUPS):
            rows.append(y_ref[g // 2, b, :, (g % 2) * half:(g % 2 + 1) * half].astype(F32).T)
        for t in range(S5_CHUNK):
            tiles = [jnp.concatenate([r[t * S5_H:(t + 1) * S5_H, :] for r in rows[j * per_slab:(j + 1) * per_slab]],
                                     axis=0).T for j in range(2)]
            o_ref[b, pl.ds(t, nc, stride=S5_CHUNK), :] = _pack_bf16_pair(*tiles)


def _unchunkify(y_c):
    q_n, bsz, nc, w = y_c.shape
    n = nc * S5_CHUNK
    nb = _layout_step(bsz, 1)
    return pl.pallas_call(
        _unchunkify_kernel,
        grid=(bsz // nb,),
        in_specs=[pl.BlockSpec((q_n, nb, nc, w), lambda b: (0, b, 0, 0))],
        out_specs=pl.BlockSpec((nb, n, LANES), lambda b: (b, 0, 0)),
        out_shape=jax.ShapeDtypeStruct((bsz, n, LANES), U32),
        compiler_params=_params(("parallel",)),
        name="unchunkify",
    )(y_c)


def _s5_kernel(u_ref, uc_ref, t_ref, mb_ref, mc_ref, a_ref, y_ref, s_lat, s_ctx, h_scr, *, bb):
    _, bsz, n_lat, w = u_ref.shape
    n_ctx = uc_ref.shape[2]
    n_blk = w // LANES
    rb = bb * n_lat

    mb = mb_ref[0]

    def in_lat(i, carry):
        s = jnp.dot(u_ref[0, pl.ds(i * bb, bb)].reshape(rb, w), mb, preferred_element_type=F32)
        for k in range(bb):
            r = pl.multiple_of((i * bb + k) * STATE_PITCH, 8)
            for blk in range(n_blk):
                s_lat[blk, pl.ds(r, n_lat), :] = s[k * n_lat:(k + 1) * n_lat, blk * LANES:(blk + 1) * LANES]
        return carry

    lax.fori_loop(0, bsz // bb, in_lat, 0)
    sc = jnp.dot(uc_ref[0].reshape(bsz * n_ctx, w), mb, preferred_element_type=F32)
    for blk in range(n_blk):
        s_ctx[blk] = sc[:, blk * LANES:(blk + 1) * LANES]

    a_fr, a_fi, a_br, a_bi = (a_ref[0, k:k + 1, :] for k in range(4))

    def step(h, a_r, a_i, s_r, s_i):
        h_r, h_i = h
        return a_r * h_r - a_i * h_i + s_r, a_r * h_i + a_i * h_r + s_i

    def ctx_rows(blk, c):
        return s_ctx[blk, pl.ds(c, bsz, stride=n_ctx), :]

    def lat_rows(ref, blk, c):
        return ref.at[blk, pl.ds(c, bsz, stride=STATE_PITCH), :]

    def ctx_step(k, carry):
        hf, hb = carry
        kb = n_ctx - 1 - k
        hf = step(hf, a_fr, a_fi, ctx_rows(0, k), ctx_rows(1, k))
        hb = step(hb, a_br, a_bi, ctx_rows(2, kb), ctx_rows(3, kb))
        return hf, hb

    zero = jnp.zeros((bsz, LANES), F32)
    carry = lax.fori_loop(0, n_ctx, ctx_step, ((zero, zero), (zero, zero)))

    def lat_step(k, carry):
        hf, hb = carry
        kb = n_lat - 1 - k
        lat_rows(h_scr, 0, k)[...] = hf[0]
        lat_rows(h_scr, 1, k)[...] = hf[1]
        lat_rows(h_scr, 2, kb)[...] = hb[0]
        lat_rows(h_scr, 3, kb)[...] = hb[1]
        hf = step(hf, a_fr, a_fi, lat_rows(s_lat, 0, k)[...], lat_rows(s_lat, 1, k)[...])
        hb = step(hb, a_br, a_bi, lat_rows(s_lat, 2, kb)[...], lat_rows(s_lat, 3, kb)[...])
        return hf, hb

    lax.fori_loop(0, n_lat, lat_step, carry)

    t0 = t_ref[0]
    t1 = t_ref[1]
    mc = mc_ref[0]
    half = S5_CHUNK * S5_H

    def out_lat(i, carry):
        u = u_ref[0, pl.ds(i * bb, bb)].reshape(rb, w)
        h_rows = []
        for k in range(bb):
            r = pl.multiple_of((i * bb + k) * STATE_PITCH, 8)
            h_rows.append(jnp.concatenate([h_scr[blk, pl.ds(r, n_lat), :] for blk in range(n_blk)], axis=1))
        h = jnp.concatenate(h_rows, axis=0).astype(BF16)
        inter = jnp.dot(h, mc, preferred_element_type=F32)
        y0 = jnp.dot(u[:, :half], t0, preferred_element_type=F32) + inter[:, :half]
        y1 = jnp.dot(u[:, half:], t1, preferred_element_type=F32) + inter[:, half:]
        y = jnp.concatenate([y0, y1], axis=1).astype(BF16)
        y_ref[0, pl.ds(i * bb, bb)] = y.reshape(bb, n_lat, w)
        return carry

    lax.fori_loop(0, bsz // bb, out_lat, 0)


def _s5_scan(u_c, uc_c, t_mat, mb_pair, mc_pair, a_rows):
    q_n, bsz, n_lat, w = u_c.shape
    n_ctx = uc_c.shape[2]
    assert n_lat + 8 == STATE_PITCH
    bb = min(8, bsz)
    n_blk = w // LANES
    return pl.pallas_call(
        functools.partial(_s5_kernel, bb=bb),
        grid=(q_n,),
        in_specs=[pl.BlockSpec((1, bsz, n_lat, w), lambda q: (q, 0, 0, 0)),
                  pl.BlockSpec((1, bsz, n_ctx, w), lambda q: (q, 0, 0, 0)),
                  pl.BlockSpec((2, w // 2, w // 2), lambda q: (q, 0, 0)),
                  pl.BlockSpec((1, w, w), lambda q: (q, 0, 0)),
                  pl.BlockSpec((1, w, w), lambda q: (q, 0, 0)),
                  pl.BlockSpec((1, 8, LANES), lambda q: (q, 0, 0))],
        out_specs=pl.BlockSpec((1, bsz, n_lat, w), lambda q: (q, 0, 0, 0)),
        out_shape=jax.ShapeDtypeStruct((q_n, bsz, n_lat, w), BF16),
        scratch_shapes=[pltpu.VMEM((n_blk, bsz * STATE_PITCH, LANES), F32),
                        pltpu.VMEM((n_blk, bsz * n_ctx, LANES), F32),
                        pltpu.VMEM((n_blk, bsz * STATE_PITCH, LANES), F32)],
        compiler_params=_params(("parallel",)),
        name="s5_scan",
    )(u_c, uc_c, t_mat, mb_pair, mc_pair, a_rows)


def _first_max(rows):
    best = rows[0]
    for r in rows[1:]:
        best = jnp.maximum(best, r)
    idx = jnp.full(best.shape, float(len(rows) - 1), F32)
    for k in range(len(rows) - 2, -1, -1):
        idx = jnp.where(rows[k] == best, float(k), idx)
    return best, idx


def _route_rows(lg):
    g_rows = [lg[k:k + 1] for k in range(N_GROUPS)]
    g_max, g_idx = _first_max(g_rows)
    g_sum = sum(jnp.exp(r - g_max) for r in g_rows)
    g_p = 1.0 / g_sum
    e_rows = []
    for j in range(EXPERTS_PER_GROUP):
        r = lg[N_GROUPS + (N_GROUPS - 1) * EXPERTS_PER_GROUP + j:][:1]
        for g in range(N_GROUPS - 2, -1, -1):
            k = N_GROUPS + g * EXPERTS_PER_GROUP + j
            r = jnp.where(g_idx == float(g), lg[k:k + 1], r)
        e_rows.append(r)
    v1, i1 = _first_max(e_rows)
    rest = [jnp.where(i1 == float(j), -jnp.inf, e_rows[j]) for j in range(EXPERTS_PER_GROUP)]
    v2, i2 = _first_max(rest)
    e21 = jnp.exp(v2 - v1)
    w1 = g_p / (1.0 + e21)
    w2 = w1 * e21
    lo = jnp.minimum(i1, i2)
    hi = jnp.maximum(i1, i2)
    base = jnp.where(lo == 0.0, 0.0, jnp.where(lo == 1.0, 3.0, 5.0))
    bucket = g_idx * float(PAIRS_PER_GROUP) + base + hi - lo - 1.0
    first_is_lo = i1 < i2
    return bucket, jnp.where(first_is_lo, w1, w2), jnp.where(first_is_lo, w2, w1)


def _mix_kernel(x_ref, xup_ref, xdn_ref, y_ref, mod_ref, g1_ref, g2_ref, cw_ref, win_ref, wglu_ref, bglu_ref,
                wout_ref, wr_ref, br_ref, *rest):
    n_cast = (len(rest) - 4) // 2
    x1_ref, h2p_ref, route_ref, counts_ref = rest[n_cast:n_cast + 4]
    for src, dst in zip(rest[:n_cast], rest[n_cast + 4:]):
        dst[...] = src[...].astype(BF16)
    i = pl.program_id(1)
    tm = x_ref.shape[1]
    d = x_ref.shape[2]
    sub = min(MIX_SUB, tm)
    n_sub = tm // sub
    cw, rw = CONV_WIDTH, CONV_ROW_WIDTH

    def hidden(xv):
        return _modulated_norm(xv, g1_ref[...], mod_ref[0, 0:1, :], mod_ref[0, 1:2, :]).astype(BF16)

    def halo(h):
        zh = jnp.dot(h, win_ref[:, 2 * cw:3 * cw], preferred_element_type=F32)
        return zh[:, 0:rw] * zh[:, rw:cw]

    @pl.when(jnp.logical_and(pl.program_id(0) == 0, i == 0))
    def _():
        counts_ref[...] = jnp.zeros_like(counts_ref)

    def sub_tile(s):
        r0 = s * sub
        xv = x_ref[0, r0:r0 + sub, :]
        hx = hidden(xv)
        g = jax.nn.gelu(jnp.concatenate(_unpack_bf16_pair(y_ref[0, r0:r0 + sub, :]), axis=1))
        hx_up = hidden(xup_ref[0]) if s == 0 else None
        hx_dn = hidden(xdn_ref[0]) if s == n_sub - 1 else None
        yield
        z_r = jnp.dot(hx, win_ref[:, cw:2 * cw], preferred_element_type=F32)
        cr = z_r[:, 0:rw] * z_r[:, rw:cw]
        z_c = jnp.dot(hx, win_ref[:, 2 * cw:3 * cw], preferred_element_type=F32)
        cc = z_c[:, 0:rw] * z_c[:, rw:cw]
        col_products[s] = cc
        yield
        bg = jnp.dot(hx, win_ref[:, 0:cw], preferred_element_type=F32)
        if s == 0:
            up_halo = jnp.where(i == 0, 0.0, halo(hx_up))
        if s == n_sub - 1:
            dn_halo = jnp.where(i == pl.num_programs(1) - 1, 0.0, halo(hx_dn))
        glu = g * jax.nn.sigmoid(jnp.dot(g.astype(BF16), wglu_ref[...], preferred_element_type=F32) + bglu_ref[...])
        yield
        row = lax.broadcasted_iota(jnp.int32, (sub, 1), 0)
        col_in_row = row % GRID_W
        left = jnp.where(col_in_row == 0, 0.0, pltpu.roll(cr, 1, axis=0))
        right = jnp.where(col_in_row == GRID_W - 1, 0.0, pltpu.roll(cr, sub - 1, axis=0))
        w_r = cw_ref[:, :rw]
        row_part = left * w_r[0:1] + cr * w_r[1:2] + right * w_r[2:3]
        if s > 0:
            up_halo = col_products[s - 1][sub - GRID_W:]
        if s < n_sub - 1:
            dn_halo = col_products[s + 1][:GRID_W]
        up = jnp.concatenate([up_halo, cc[:sub - GRID_W]], axis=0)
        dn = jnp.concatenate([cc[GRID_W:], dn_halo], axis=0)
        w_c = cw_ref[:, rw:]
        col_part = up * w_c[0:1] + cc * w_c[1:2] + dn * w_c[2:3]
        y_row = (bg[:, 0:rw] * row_part).astype(BF16)
        y_col = (bg[:, rw:cw] * col_part).astype(BF16)
        mixed = jnp.concatenate([glu.astype(BF16), y_row, y_col], axis=1)
        yield
        yx = jnp.dot(mixed, wout_ref[...], preferred_element_type=F32)
        yield
        x1 = xv + mod_ref[0, 2:3, :] * yx
        x1_ref[0, r0:r0 + sub, :] = x1
        h2 = _modulated_norm(x1, g2_ref[...], mod_ref[0, 3:4, :], mod_ref[0, 4:5, :])
        h2b = h2.astype(BF16)
        lg2 = lax.dot_general(wr_ref[...], h2b, (((1,), (1,)), ((), ())), preferred_element_type=F32)
        lg = lg2[:ROUTER_ROWS] + lg2[ROUTER_ROWS:] + br_ref[...]
        bucket, w_a, w_b = _route_rows(lg)
        r8 = lax.broadcasted_iota(jnp.int32, (8, sub), 0)
        route_ref[:, r0:r0 + sub] = jnp.where(r8 == 0, bucket, jnp.where(r8 == 1, w_a, jnp.where(r8 == 2, w_b, 0.0)))
        rl = lax.broadcasted_iota(jnp.int32, (ROW_EXTRA, sub), 0)
        gates_t = jnp.where(rl == 0, w_a, jnp.where(rl == 1, w_b, 0.0))
        h2p_ref[0, r0:r0 + sub, 0:d // 2] = _pack_bf16_pair(h2[:, :d // 2], h2[:, d // 2:])
        h2p_ref[0, r0:r0 + sub, d // 2:] = lax.bitcast_convert_type(gates_t.T, U32)
        rb = lax.broadcasted_iota(jnp.int32, (BUCKET_ROWS, sub), 0).astype(F32)
        counts_ref[...] += jnp.sum(jnp.where(rb == bucket, 1.0, 0.0), axis=-1, keepdims=True)
        yield

    n_stage = 6
    col_products = {}
    tiles = [sub_tile(s) for s in range(n_sub)]
    for step in range(n_sub + n_stage - 1):
        for s in reversed(range(n_sub)):
            if 0 <= step - s < n_stage:
                next(tiles[s])


def _mix(x, y_s5, mods, norm1_g, norm2_g, conv_w, w_conv, w_glu, b_glu, w_out, w_router_t, b_router_t, tm, b0, nb,
         to_cast):
    _, n, d = x.shape
    nt = n // tm
    n_steps = nb * nt
    cast_specs = []
    for w in to_cast:
        per = -(-w.shape[0] // n_steps)
        n_blk = w.shape[0] // per
        assert n_blk * per == w.shape[0]
        cast_specs.append(pl.BlockSpec((per,) + w.shape[1:],
                                       lambda b, i, n_blk=n_blk: ((b * nt + i) * n_blk // n_steps, 0, 0)))
    halo_blocks = n // GRID_W
    per_tile = tm // GRID_W
    tok_out = lambda w: pl.BlockSpec((1, tm, w), lambda b, i: (b, i, 0))
    full = lambda a: pl.BlockSpec(a.shape, lambda b, i: (0,) * a.ndim)
    args = (x, x, x, y_s5, mods, norm1_g.reshape(1, d), norm2_g.reshape(1, d), conv_w, w_conv, w_glu,
            b_glu.reshape(1, -1), w_out, w_router_t, b_router_t)
    in_specs = [pl.BlockSpec((1, tm, d), lambda b, i: (b + b0, i, 0)),
                pl.BlockSpec((1, GRID_W, d), lambda b, i: (b + b0, jnp.maximum(i * per_tile - 1, 0), 0)),
                pl.BlockSpec((1, GRID_W, d),
                             lambda b, i: (b + b0, jnp.minimum((i + 1) * per_tile, halo_blocks - 1), 0)),
                pl.BlockSpec((1, tm, LANES), lambda b, i: (b + b0, i, 0)),
                pl.BlockSpec((1, N_MOD, d), lambda b, i: (b + b0, 0, 0))] + [full(a) for a in args[5:]]
    return pl.pallas_call(
        _mix_kernel,
        grid=(nb, nt),
        in_specs=in_specs + cast_specs,
        out_specs=[tok_out(d), tok_out(d // 2 + ROW_EXTRA),
                   pl.BlockSpec((8, tm), lambda b, i: (0, b * nt + i)),
                   pl.BlockSpec((BUCKET_ROWS, LANES), lambda b, i: (0, 0))] + cast_specs,
        out_shape=[jax.ShapeDtypeStruct((nb, n, d), F32),
                   jax.ShapeDtypeStruct((nb, n, d // 2 + ROW_EXTRA), U32),
                   jax.ShapeDtypeStruct((8, nb * n), F32),
                   jax.ShapeDtypeStruct((BUCKET_ROWS, LANES), F32)]
        + [jax.ShapeDtypeStruct(w.shape, BF16) for w in to_cast],
        compiler_params=_params(("arbitrary", "arbitrary")),
        name="mix",
    )(*args, *to_cast)


def _rank_kernel(route_ref, offs_ref, dest_ref, run_ref):
    tr = route_ref.shape[1]
    n_blk = tr // LANES

    @pl.when(pl.program_id(0) == 0)
    def _():
        run_ref[...] = jnp.zeros_like(run_ref)

    bucket = route_ref[0:1, :]
    rb = lax.broadcasted_iota(jnp.int32, (BUCKET_ROWS, tr), 0).astype(F32)
    onehot = jnp.where(rb == bucket, 1.0, 0.0)
    blocks = [onehot[:, k * LANES:(k + 1) * LANES] for k in range(n_blk)]
    s_idx = lax.broadcasted_iota(jnp.int32, (LANES, LANES), 0)
    t_idx = lax.broadcasted_iota(jnp.int32, (LANES, LANES), 1)
    tri = jnp.where(s_idx <= t_idx, 1.0, 0.0).astype(BF16)
    prefix = jnp.dot(jnp.concatenate(blocks, axis=0).astype(BF16), tri, preferred_element_type=F32)
    start = run_ref[:, 0:1] + offs_ref[:, 0:1]
    carry = start
    for k in range(n_blk):
        pk = prefix[k * BUCKET_ROWS:(k + 1) * BUCKET_ROWS]
        dest = jnp.sum(blocks[k] * (pk - 1.0 + carry), axis=0, keepdims=True)
        dest_ref[:, k * LANES:(k + 1) * LANES] = dest.astype(jnp.int32)
        carry = carry + pk[:, LANES - 1:LANES]
    run_ref[...] += carry - start


def _rank(route_t, offs_rows, tr):
    n = route_t.shape[1]
    return pl.pallas_call(
        _rank_kernel,
        grid=(n // tr,),
        in_specs=[pl.BlockSpec((8, tr), lambda i: (0, i)),
                  pl.BlockSpec((BUCKET_ROWS, LANES), lambda i: (0, 0))],
        out_specs=pl.BlockSpec((1, tr), lambda i: (0, i)),
        out_shape=jax.ShapeDtypeStruct((1, n), jnp.int32),
        scratch_shapes=[pltpu.VMEM((BUCKET_ROWS, LANES), F32)],
        compiler_params=_params(("arbitrary",)),
        name="rank",
    )(route_t, offs_rows)


def _sc_mesh():
    return plsc.VectorSubcoreMesh(core_axis_name="core", subcore_axis_name="subcore")


def _scatter_rows(src, dest, n_out):
    n, w = src.shape
    sub = SC_INDEX_TILE // SC_ROWS

    @functools.partial(pl.kernel, out_type=jax.ShapeDtypeStruct((n_out, w), src.dtype), mesh=_sc_mesh(),
                       scratch_types=[], name="scatter_rows")
    def scatter(x_hbm, i_hbm, o_hbm):
        def body(x_vmem, i_vmem):
            j = pl.program_id(1)
            pltpu.sync_copy(x_vmem, o_hbm.at[i_vmem.at[0, pl.ds(j * SC_ROWS, SC_ROWS)]])

        pltpu.emit_pipeline(
            body,
            grid=(n // SC_INDEX_TILE, sub),
            in_specs=[pl.BlockSpec((SC_ROWS, w), lambda i, j: (i * sub + j, 0)),
                      pl.BlockSpec((1, SC_INDEX_TILE), lambda i, j: (0, i))],
            out_specs=[],
            core_axis_name=("core", "subcore"),
            dimension_semantics=(pltpu.PARALLEL, pltpu.ARBITRARY),
        )(x_hbm, i_hbm)

    return scatter(src, dest)


def _gather_rows(src, idx):
    n = idx.shape[1]
    w = src.shape[1]
    sub = SC_INDEX_TILE // SC_ROWS

    @functools.partial(pl.kernel, out_type=jax.ShapeDtypeStruct((n, w), src.dtype), mesh=_sc_mesh(),
                       scratch_types=[], name="gather_rows")
    def gather(x_hbm, i_hbm, o_hbm):
        def body(i_vmem, o_vmem):
            j = pl.program_id(1)
            pltpu.sync_copy(x_hbm.at[i_vmem.at[0, pl.ds(j * SC_ROWS, SC_ROWS)]], o_vmem)

        pltpu.emit_pipeline(
            body,
            grid=(n // SC_INDEX_TILE, sub),
            in_specs=[pl.BlockSpec((1, SC_INDEX_TILE), lambda i, j: (0, i))],
            out_specs=[pl.BlockSpec((SC_ROWS, w), lambda i, j: (i * sub + j, 0))],
            core_axis_name=("core", "subcore"),
            dimension_semantics=(pltpu.PARALLEL, pltpu.ARBITRARY),
        )(i_hbm, o_hbm)

    return gather(src, idx)


def _moe_kernel(ea_ref, eb_ref, valid_ref, rows_ref, w1a_ref, w3a_ref, w2a_ref, w1b_ref, w3b_ref, w2b_ref, o_ref):
    j = pl.program_id(0)
    half = rows_ref.shape[1] - ROW_EXTRA

    @pl.when(valid_ref[j] != 0)
    def _():
        ha, hb = _unpack_bf16_pair(rows_ref[:, 0:half])
        ha = ha.astype(BF16)
        hb = hb.astype(BF16)
        gates = lax.bitcast_convert_type(rows_ref[:, half:], F32)

        def expert(w1_ref, w3_ref, w2_ref, gate):
            def up(w_ref):
                return (jnp.dot(ha, w_ref[0, 0:half, :], preferred_element_type=F32)
                        + jnp.dot(hb, w_ref[0, half:, :], preferred_element_type=F32))

            a1 = up(w1_ref)
            he = (a1 * jax.nn.sigmoid(a1)) * up(w3_ref) * gate
            return jnp.dot(he.astype(BF16), w2_ref[0], preferred_element_type=F32)

        y = (expert(w1a_ref, w3a_ref, w2a_ref, gates[:, 0:1]) + expert(w1b_ref, w3b_ref, w2b_ref, gates[:, 1:2]))
        o_ref[...] = _pack_bf16_pair(y[:, :half], y[:, half:])


def _moe_grouped(rows, tile_ea, tile_eb, tile_valid, w1, w3, w2, tmm):
    r, w = rows.shape
    _, de, d = w2.shape
    up_a = pl.BlockSpec((1, d, de), lambda j, ea, eb, va: (ea[j], 0, 0))
    up_b = pl.BlockSpec((1, d, de), lambda j, ea, eb, va: (eb[j], 0, 0))
    grid_spec = pltpu.PrefetchScalarGridSpec(
        num_scalar_prefetch=3,
        grid=(r // tmm,),
        in_specs=[pl.BlockSpec((tmm, w), lambda j, ea, eb, va: (j, 0)),
                  up_a, up_a, pl.BlockSpec((1, de, d), lambda j, ea, eb, va: (ea[j], 0, 0)),
                  up_b, up_b, pl.BlockSpec((1, de, d), lambda j, ea, eb, va: (eb[j], 0, 0))],
        out_specs=pl.BlockSpec((tmm, d // 2), lambda j, ea, eb, va: (j, 0)),
    )
    return pl.pallas_call(
        _moe_kernel,
        grid_spec=grid_spec,
        out_shape=jax.ShapeDtypeStruct((r, d // 2), U32),
        compiler_params=_params(("arbitrary",)),
        name="moe",
    )(tile_ea, tile_eb, tile_valid, rows, w1, w3, w2, w1, w3, w2)


def _final_kernel(x1_ref, moe_ref, mod_ref, fg_ref, *rest):
    o_ref = rest[-1]
    ya, yb = _unpack_bf16_pair(moe_ref[0])
    half = ya.shape[1]
    gate = mod_ref[0, 5:6, :]
    xa = x1_ref[0, :, 0:half] + gate[:, 0:half] * ya
    xb = x1_ref[0, :, half:] + gate[:, half:] * yb
    ms = (jnp.sum(xa * xa, axis=-1, keepdims=True) + jnp.sum(xb * xb, axis=-1, keepdims=True)) / (2 * half)
    inv = lax.rsqrt(ms + RMS_EPS)
    o_ref[0, :, 0:half] = xa * inv * fg_ref[:, 0:half]
    o_ref[0, :, half:] = xb * inv * fg_ref[:, half:]


def _final(x1, moe_tok, mods, final_g, tm, b0, bsz, out_prev):
    nb, n, d = x1.shape
    tok = lambda w: pl.BlockSpec((1, tm, w), lambda b, i: (b, i, 0))
    args = [x1, moe_tok, mods, final_g.reshape(1, d)]
    in_specs = [tok(d), tok(d // 2), pl.BlockSpec((1, N_MOD, d), lambda b, i: (b + b0, 0, 0)),
                pl.BlockSpec((1, d), lambda b, i: (0, 0))]
    aliases = {}
    if out_prev is not None:
        args.append(out_prev)
        in_specs.append(pl.BlockSpec(memory_space=pl.ANY))
        aliases = {len(args) - 1: 0}
    return pl.pallas_call(
        _final_kernel,
        grid=(nb, n // tm),
        in_specs=in_specs,
        out_specs=pl.BlockSpec((1, tm, d), lambda b, i: (b + b0, i, 0)),
        out_shape=jax.ShapeDtypeStruct((bsz, n, d), F32),
        input_output_aliases=aliases,
        compiler_params=_params(("parallel", "parallel")),
        name="final",
    )(*args)


def _tile_plan(counts, tmm, n_tiles):
    tiles = (counts + (tmm - 1)) // tmm
    tile_end = jnp.cumsum(tiles)
    offs = (tile_end - tiles) * tmm
    n_valid = tile_end[-1]
    j = jnp.arange(n_tiles, dtype=jnp.int32)
    bucket = jnp.sum((tile_end[None, :] <= jnp.minimum(j, n_valid - 1)[:, None]).astype(jnp.int32), axis=1)
    pair_lo = jnp.array([0, 0, 0, 1, 1, 2], jnp.int32)
    pair_hi = jnp.array([1, 2, 3, 2, 3, 3], jnp.int32)
    group = bucket // PAIRS_PER_GROUP
    pair = bucket % PAIRS_PER_GROUP
    tile_ea = group * EXPERTS_PER_GROUP + pair_lo[pair]
    tile_eb = group * EXPERTS_PER_GROUP + pair_hi[pair]
    return offs, tile_ea, tile_eb, (j < n_valid).astype(jnp.int32)


def kernel(x, c, ctx, c_ctx, w_mod, b_mod, norm1_g, norm2_g, w_in, s5_lambda_re, s5_lambda_im, s5_log_dt,
           s5_b_re, s5_b_im, s5_c_re, s5_c_im, s5_d, w_glu, b_glu, conv_w, w_out, router_group_w,
           router_group_b, router_expert_w, router_expert_b, expert_w1, expert_w3, expert_w2, final_g):
    assert w_mod.shape[0] == 1, "single-layer kernel"
    bsz, n_tok, d = x.shape
    l = 0
    tm = min(TOKEN_TILE, n_tok)

    n_cond = bsz + 1
    pad = (-n_cond) % 8
    cond = jnp.concatenate([c, c_ctx[None, :], jnp.zeros((pad, d), F32)], axis=0)
    m = _mod_rows(cond, w_mod[l], b_mod[l])
    mx = m[:bsz].reshape(bsz, N_MOD, d)
    mc = m[bsz:bsz + 1].reshape(1, N_MOD, d)

    w_in_b = w_in[l].astype(BF16)
    w_s5 = w_in_b[:, :S5_WIDTH]
    o_c = S5_WIDTH + CONV_WIDTH
    o_v = S5_WIDTH + 2 * CONV_WIDTH
    w_conv = jnp.concatenate(
        [w_in_b[:, S5_WIDTH:o_c], w_in_b[:, o_c:o_c + CONV_ROW_WIDTH], w_in_b[:, o_v:o_v + CONV_ROW_WIDTH],
         w_in_b[:, o_c + CONV_ROW_WIDTH:o_v], w_in_b[:, o_v + CONV_ROW_WIDTH:]], axis=1)
    u = _inproj(x, mx, True, norm1_g[l], w_s5, INPROJ_TILE, "inproj")
    uc = _inproj(ctx, mc, False, norm1_g[l], w_s5, INPROJ_TILE, "inproj_ctx")

    t_mat, mb_pair, mc_pair, a_rows = _s5_matrices(
        s5_lambda_re[l], s5_lambda_im[l], s5_log_dt[l], s5_b_re[l], s5_b_im[l], s5_c_re[l], s5_c_im[l], s5_d[l])
    y_c = _s5_scan(_chunkify(u, "chunkify"), _chunkify(uc, "chunkify_ctx"), t_mat, mb_pair, mc_pair, a_rows)
    y_s5 = _unchunkify(y_c)

    n_logits = N_GROUPS + N_EXPERTS
    w_router = jnp.concatenate(
        [router_group_w[l], router_expert_w[l], jnp.zeros((d, ROUTER_ROWS - n_logits), F32)], axis=1).T
    w_router_hi = w_router.astype(BF16)
    w_router_lo = (w_router - w_router_hi.astype(F32)).astype(BF16)
    w_router_t = jnp.concatenate([w_router_hi, w_router_lo], axis=0)
    b_router = jnp.concatenate([router_group_b[l], router_expert_b[l], jnp.zeros((ROUTER_ROWS - n_logits,), F32)])
    tm_mix = min(MIX_TILE, n_tok)
    b_router_t = jnp.broadcast_to(b_router[:, None], (ROUTER_ROWS, min(MIX_SUB, tm_mix)))

    n_parts = MOE_PARTS if bsz % MOE_PARTS == 0 else 1
    nb = bsz // n_parts
    n_part = nb * n_tok
    n_buckets = N_GROUPS * PAIRS_PER_GROUP
    n_rows = n_part + n_buckets * MOE_TILE
    w_glu_b = w_glu[l].astype(BF16)
    w_out_b = w_out[l].astype(BF16)
    experts_f32 = (expert_w1[l], expert_w3[l], expert_w2[l])
    cast_plan = [experts_f32] if n_parts == 1 else [experts_f32[:2], experts_f32[2:]] + [()] * (n_parts - 2)
    w_experts = []
    staged = []
    for p in range(n_parts):
        x1, h2p, route_t, counts, *w_cast = _mix(x, y_s5, mx, norm1_g[l], norm2_g[l], conv_w[l], w_conv, w_glu_b,
                                                 b_glu[l], w_out_b, w_router_t, b_router_t, tm_mix, p * nb, nb,
                                                 cast_plan[p])
        w_experts += w_cast
        offs, *tiles = _tile_plan(counts[:n_buckets, 0].astype(jnp.int32), MOE_TILE, n_rows // MOE_TILE)
        offs_rows = jnp.zeros((BUCKET_ROWS,), F32).at[:n_buckets].set(offs.astype(F32))
        dest = _rank(route_t, jnp.broadcast_to(offs_rows[:, None], (BUCKET_ROWS, LANES)), min(RANK_TILE, n_part))
        rows = _scatter_rows(h2p.reshape(n_part, d // 2 + ROW_EXTRA), dest, n_rows)
        staged.append((x1, rows, dest, tiles))
    out = None
    for p, (x1, rows, dest, tiles) in enumerate(staged):
        y_rows = _moe_grouped(rows, *tiles, *w_experts, MOE_TILE)
        moe_tok = _gather_rows(y_rows, dest).reshape(nb, n_tok, d // 2)
        out = _final(x1, moe_tok, mx, final_g, tm, p * nb, bsz, out)
    return out
```

```python
import functools

import jax
import jax.numpy as jnp
from jax import lax
from jax.experimental import pallas as pl
from jax.experimental.pallas import tpu as pltpu
from jax.experimental.pallas import tpu_sc as plsc

F32 = jnp.float32
BF16 = jnp.bfloat16
U32 = jnp.uint32

RMS_EPS = 1e-6
N_MOD = 6
GRID_W = 64
S5_WIDTH = 256
S5_H = 16
S5_P = 64
S5_GROUPS = S5_WIDTH // S5_H
S5_PAIRS = S5_GROUPS // 2
S5_CHUNK = 16
LANES = 128
STATE_PITCH = 136
CONV_WIDTH = 768
CONV_ROW_WIDTH = CONV_WIDTH // 2
N_GROUPS = 4
EXPERTS_PER_GROUP = 4
N_EXPERTS = N_GROUPS * EXPERTS_PER_GROUP
PAIRS_PER_GROUP = 6
ROUTER_ROWS = 32
BUCKET_ROWS = 32
ROW_EXTRA = 128
TOKEN_TILE = 2048
MOE_TILE = 512
RANK_TILE = 2048
INPROJ_TILE = 2048
MIX_TILE = 1024
MIX_SUB = 512
MOE_PARTS = 2
LAYOUT_GROUPS = 4
SC_ROWS = 64
SC_INDEX_TILE = 128
VMEM_LIMIT = 52 * 1024 * 1024


def _params(sem, vmem=VMEM_LIMIT):
    return pltpu.CompilerParams(dimension_semantics=sem, vmem_limit_bytes=vmem)


def _pack_bf16_pair(a, b):
    ua = lax.bitcast_convert_type(a.astype(BF16).astype(F32), U32)
    ub = lax.bitcast_convert_type(b.astype(BF16).astype(F32), U32)
    return ua | (ub >> 16)


def _unpack_bf16_pair(w):
    a = lax.bitcast_convert_type(w & jnp.uint32(0xFFFF0000), F32)
    b = lax.bitcast_convert_type(w << 16, F32)
    return a, b


def _mod_kernel(c_ref, w_ref, b_ref, o_ref):
    c = c_ref[...]
    o_ref[...] = jnp.dot(c * jax.nn.sigmoid(c), w_ref[...], preferred_element_type=F32) + b_ref[...]


def _mod_rows(cond, w_mod, b_mod):
    n, d = cond.shape
    nout = w_mod.shape[1]
    bn = d
    return pl.pallas_call(
        _mod_kernel,
        grid=(nout // bn,),
        in_specs=[pl.BlockSpec((n, d), lambda j: (0, 0)),
                  pl.BlockSpec((d, bn), lambda j: (0, j)),
                  pl.BlockSpec((1, bn), lambda j: (0, j))],
        out_specs=pl.BlockSpec((n, bn), lambda j: (0, j)),
        out_shape=jax.ShapeDtypeStruct((n, nout), F32),
        compiler_params=_params(("arbitrary",)),
        name="mod",
    )(cond, w_mod, b_mod.reshape(1, nout))


def _modulated_norm(x, g, shift, scale):
    ms = jnp.mean(x * x, axis=-1, keepdims=True)
    return (x * lax.rsqrt(ms + RMS_EPS)) * (g * (1.0 + scale)) + shift


def _inproj_kernel(x_ref, mod_ref, g_ref, w_ref, u_ref):
    nbk, tm, d = x_ref.shape
    h = _modulated_norm(x_ref[...].reshape(nbk * tm, d), g_ref[...], mod_ref[0, 0:1, :], mod_ref[0, 1:2, :])
    u = jnp.dot(h.astype(BF16), w_ref[...], preferred_element_type=F32)
    packed = _pack_bf16_pair(u[:, 0:LANES], u[:, LANES:2 * LANES])
    for b in range(nbk):
        u_ref[b] = packed[b * tm:(b + 1) * tm]


def _inproj(x, mods, per_batch_mod, norm_g, w_s5, tile, name):
    bsz, n, d = x.shape
    tm = min(tile, n)
    nbk = 1 if per_batch_mod else max(1, min(bsz, tile // n))
    assert bsz % nbk == 0
    mod_map = (lambda b, i: (b, 0, 0)) if per_batch_mod else (lambda b, i: (0, 0, 0))
    assert S5_WIDTH == 2 * LANES
    return pl.pallas_call(
        _inproj_kernel,
        grid=(bsz // nbk, n // tm),
        in_specs=[pl.BlockSpec((nbk, tm, d), lambda b, i: (b, i, 0)),
                  pl.BlockSpec((1, N_MOD, d), mod_map),
                  pl.BlockSpec((1, d), lambda b, i: (0, 0)),
                  pl.BlockSpec((d, S5_WIDTH), lambda b, i: (0, 0))],
        out_specs=pl.BlockSpec((nbk, tm, LANES), lambda b, i: (b, i, 0)),
        out_shape=jax.ShapeDtypeStruct((bsz, n, LANES), U32),
        compiler_params=_params(("parallel", "parallel")),
        name=name,
    )(x, mods, norm_g.reshape(1, d), w_s5)


def _toeplitz_kernel(strip_ref, t_ref):
    lc = t_ref.shape[1] // S5_H
    for s in range(lc):
        off = (lc - 1 - s) * S5_H
        t_ref[0, s * S5_H:(s + 1) * S5_H, :] = strip_ref[0, :, off:off + lc * S5_H].astype(BF16)


def _toeplitz(strip):
    g_n, h_n, w = strip.shape
    n = S5_CHUNK * h_n
    return pl.pallas_call(
        _toeplitz_kernel,
        grid=(g_n,),
        in_specs=[pl.BlockSpec((1, h_n, w), lambda g: (g, 0, 0))],
        out_specs=pl.BlockSpec((1, n, n), lambda g: (g, 0, 0)),
        out_shape=jax.ShapeDtypeStruct((g_n, n, n), BF16),
        compiler_params=_params(("parallel",)),
        name="toeplitz",
    )(strip)


def _s5_matrices(lam_re, lam_im, log_dt, b_re, b_im, c_re, c_im, d_skip):
    lc, g_n, p_n, h_n = S5_CHUNK, S5_GROUPS, S5_P, S5_H
    lam = lax.complex(lam_re.astype(F32), lam_im.astype(F32))
    dt = jnp.exp(log_dt.astype(F32))[..., None]
    a_bar = jnp.exp(lam * dt)
    b_bar = ((a_bar - 1.0) / lam)[..., None] * lax.complex(b_re.astype(F32), b_im.astype(F32))
    cm = lax.complex(c_re.astype(F32), c_im.astype(F32))
    steps = jnp.arange(lc + 1, dtype=F32)
    apow = jnp.exp((lam * dt)[:, :, None, :] * steps[None, None, :, None])
    kern = jnp.einsum('dgop,dgjp,dgpi->dgjio', cm, apow[:, :, :lc], b_bar).real
    skip = jnp.eye(h_n, dtype=F32) * d_skip.astype(F32).reshape(g_n, 1, h_n)
    centre = kern[0, :, 0] + kern[1, :, 0] + skip
    lags = jnp.concatenate([kern[1, :, :0:-1], centre[:, None], kern[0, :, 1:]], axis=1)
    strip = lags.transpose(0, 2, 1, 3).reshape(g_n, h_n, (2 * lc - 1) * h_n)
    strip = jnp.pad(strip, ((0, 0), (0, 0), (0, h_n)))
    t_mat = _toeplitz(strip)

    def in_mat(pw, bb):
        return (pw[:, :, None, :] * bb.transpose(0, 2, 1)[:, None, :, :]).reshape(g_n, lc * h_n, p_n)

    mb_f = in_mat(apow[0, :, lc - 1::-1][:, :lc], b_bar[0])
    mb_b = in_mat(apow[1, :, :lc], b_bar[1])

    def out_mat(pw, cc):
        return (pw.transpose(0, 2, 1)[:, :, :, None] * cc.transpose(0, 2, 1)[:, :, None, :]).reshape(
            g_n, p_n, lc * h_n)

    mc_f = out_mat(apow[0, :, 1:lc + 1], cm[0])
    mc_b = out_mat(apow[1, :, lc:0:-1], cm[1])
    a_chunk = apow[:, :, lc]

    q_n = S5_PAIRS
    zeros_in = jnp.zeros((g_n, lc * h_n, p_n), F32)

    def pair_cols(m):
        m = m.reshape(q_n, 2, lc * h_n, p_n)
        z = zeros_in.reshape(q_n, 2, lc * h_n, p_n)[:, 0]
        top = jnp.concatenate([m[:, 0], z], axis=-1)
        bot = jnp.concatenate([z, m[:, 1]], axis=-1)
        return jnp.concatenate([top, bot], axis=1)

    mb_pair = jnp.concatenate([pair_cols(mb_f.real), pair_cols(mb_f.imag),
                               pair_cols(mb_b.real), pair_cols(mb_b.imag)], axis=-1)

    def pair_rows(m):
        m = m.reshape(q_n, 2, p_n, lc * h_n)
        z = jnp.zeros_like(m[:, 0])
        top = jnp.concatenate([m[:, 0], z], axis=-1)
        bot = jnp.concatenate([z, m[:, 1]], axis=-1)
        return jnp.concatenate([top, bot], axis=1)

    mc_pair = jnp.concatenate([pair_rows(mc_f.real), pair_rows(-mc_f.imag),
                               pair_rows(mc_b.real), pair_rows(-mc_b.imag)], axis=1)
    a_rows = jnp.stack([a_chunk[0].real, a_chunk[0].imag, a_chunk[1].real, a_chunk[1].imag], axis=0)
    a_rows = a_rows.reshape(4, q_n, 2 * p_n).transpose(1, 0, 2)
    a_rows = jnp.concatenate([a_rows, jnp.zeros_like(a_rows)], axis=1)
    return t_mat, mb_pair.astype(BF16), mc_pair.astype(BF16), a_rows


def _chunkify_kernel(u_ref, o_ref, *, gb):
    _, nb, nc, _ = o_ref.shape
    half = S5_CHUNK * S5_H
    per_slab = LANES // S5_H
    for b0 in range(0, nb, gb):
        slabs = ([], [])
        for t in range(S5_CHUNK):
            rows = [u_ref[b0 + b, pl.ds(t, nc, stride=S5_CHUNK), :] for b in range(gb)]
            for j, part in enumerate(_unpack_bf16_pair(rows[0] if gb == 1 else jnp.concatenate(rows, axis=0))):
                slabs[j].append(part.T)
        for j, cols in enumerate(slabs):
            for gl in range(per_slab):
                g = j * per_slab + gl
                m = jnp.concatenate([c[gl * S5_H:(gl + 1) * S5_H, :] for c in cols], axis=0)
                o_ref[g // 2, b0:b0 + gb, :, (g % 2) * half:(g % 2 + 1) * half] = (
                    m.T.astype(BF16).reshape(gb, nc, half))


def _layout_step(bsz, gb):
    return gb * LAYOUT_GROUPS if bsz % (gb * LAYOUT_GROUPS) == 0 else gb


def _chunkify(u_rows, name):
    bsz, n, _ = u_rows.shape
    nc = n // S5_CHUNK
    gb = min(bsz, max(1, LANES // nc))
    nb = _layout_step(bsz, gb)
    w = 2 * S5_CHUNK * S5_H
    return pl.pallas_call(
        functools.partial(_chunkify_kernel, gb=gb),
        grid=(bsz // nb,),
        in_specs=[pl.BlockSpec((nb, n, LANES), lambda b: (b, 0, 0))],
        out_specs=pl.BlockSpec((S5_PAIRS, nb, nc, w), lambda b: (0, b, 0, 0)),
        out_shape=jax.ShapeDtypeStruct((S5_PAIRS, bsz, nc, w), BF16),
        compiler_params=_params(("parallel",)),
        name=name,
    )(u_rows)


def _unchunkify_kernel(y_ref, o_ref):
    _, nb, nc, _ = y_ref.shape
    half = S5_CHUNK * S5_H
    per_slab = LANES // S5_H
    for b in range(nb):
        rows = []
        for g in range(S5_GROUPS):
            rows.append(y_ref[g // 2, b, :, (g % 2) * half:(g % 2 + 1) * half].astype(F32).T)
        for t in range(S5_CHUNK):
            tiles = [jnp.concatenate([r[t * S5_H:(t + 1) * S5_H, :] for r in rows[j * per_slab:(j + 1) * per_slab]],
                                     axis=0).T for j in range(2)]
            o_ref[b, pl.ds(t, nc, stride=S5_CHUNK), :] = _pack_bf16_pair(*tiles)


def _unchunkify(y_c):
    q_n, bsz, nc, w = y_c.shape
    n = nc * S5_CHUNK
    nb = _layout_step(bsz, 1)
    return pl.pallas_call(
        _unchunkify_kernel,
        grid=(bsz // nb,),
        in_specs=[pl.BlockSpec((q_n, nb, nc, w), lambda b: (0, b, 0, 0))],
        out_specs=pl.BlockSpec((nb, n, LANES), lambda b: (b, 0, 0)),
        out_shape=jax.ShapeDtypeStruct((bsz, n, LANES), U32),
        compiler_params=_params(("parallel",)),
        name="unchunkify",
    )(y_c)


def _s5_kernel(u_ref, uc_ref, t_ref, mb_ref, mc_ref, a_ref, y_ref, s_lat, s_ctx, h_scr, *, bb):
    _, bsz, n_lat, w = u_ref.shape
    n_ctx = uc_ref.shape[2]
    n_blk = w // LANES
    rb = bb * n_lat

    mb = mb_ref[0]

    def in_lat(i, carry):
        s = jnp.dot(u_ref[0, pl.ds(i * bb, bb)].reshape(rb, w), mb, preferred_element_type=F32)
        for k in range(bb):
            r = pl.multiple_of((i * bb + k) * STATE_PITCH, 8)
            for blk in range(n_blk):
                s_lat[blk, pl.ds(r, n_lat), :] = s[k * n_lat:(k + 1) * n_lat, blk * LANES:(blk + 1) * LANES]
        return carry

    lax.fori_loop(0, bsz // bb, in_lat, 0)
    sc = jnp.dot(uc_ref[0].reshape(bsz * n_ctx, w), mb, preferred_element_type=F32)
    for blk in range(n_blk):
        s_ctx[blk] = sc[:, blk * LANES:(blk + 1) * LANES]

    a_fr, a_fi, a_br, a_bi = (a_ref[0, k:k + 1, :] for k in range(4))

    def step(h, a_r, a_i, s_r, s_i):
        h_r, h_i = h
        return a_r * h_r - a_i * h_i + s_r, a_r * h_i + a_i * h_r + s_i

    def ctx_rows(blk, c):
        return s_ctx[blk, pl.ds(c, bsz, stride=n_ctx), :]

    def lat_rows(ref, blk, c):
        return ref.at[blk, pl.ds(c, bsz, stride=STATE_PITCH), :]

    def ctx_step(k, carry):
        hf, hb = carry
        kb = n_ctx - 1 - k
        hf = step(hf, a_fr, a_fi, ctx_rows(0, k), ctx_rows(1, k))
        hb = step(hb, a_br, a_bi, ctx_rows(2, kb), ctx_rows(3, kb))
        return hf, hb

    zero = jnp.zeros((bsz, LANES), F32)
    carry = lax.fori_loop(0, n_ctx, ctx_step, ((zero, zero), (zero, zero)))

    def lat_step(k, carry):
        hf, hb = carry
        kb = n_lat - 1 - k
        lat_rows(h_scr, 0, k)[...] = hf[0]
        lat_rows(h_scr, 1, k)[...] = hf[1]
        lat_rows(h_scr, 2, kb)[...] = hb[0]
        lat_rows(h_scr, 3, kb)[...] = hb[1]
        hf = step(hf, a_fr, a_fi, lat_rows(s_lat, 0, k)[...], lat_rows(s_lat, 1, k)[...])
        hb = step(hb, a_br, a_bi, lat_rows(s_lat, 2, kb)[...], lat_rows(s_lat, 3, kb)[...])
        return hf, hb

    lax.fori_loop(0, n_lat, lat_step, carry)

    t0 = t_ref[0]
    t1 = t_ref[1]
    mc = mc_ref[0]
    half = S5_CHUNK * S5_H

    def out_lat(i, carry):
        u = u_ref[0, pl.ds(i * bb, bb)].reshape(rb, w)
        h_rows = []
        for k in range(bb):
            r = pl.multiple_of((i * bb + k) * STATE_PITCH, 8)
            h_rows.append(jnp.concatenate([h_scr[blk, pl.ds(r, n_lat), :] for blk in range(n_blk)], axis=1))
        h = jnp.concatenate(h_rows, axis=0).astype(BF16)
        inter = jnp.dot(h, mc, preferred_element_type=F32)
        y0 = jnp.dot(u[:, :half], t0, preferred_element_type=F32) + inter[:, :half]
        y1 = jnp.dot(u[:, half:], t1, preferred_element_type=F32) + inter[:, half:]
        y = jnp.concatenate([y0, y1], axis=1).astype(BF16)
        y_ref[0, pl.ds(i * bb, bb)] = y.reshape(bb, n_lat, w)
        return carry

    lax.fori_loop(0, bsz // bb, out_lat, 0)


def _s5_scan(u_c, uc_c, t_mat, mb_pair, mc_pair, a_rows):
    q_n, bsz, n_lat, w = u_c.shape
    n_ctx = uc_c.shape[2]
    assert n_lat + 8 == STATE_PITCH
    bb = min(8, bsz)
    n_blk = w // LANES
    return pl.pallas_call(
        functools.partial(_s5_kernel, bb=bb),
        grid=(q_n,),
        in_specs=[pl.BlockSpec((1, bsz, n_lat, w), lambda q: (q, 0, 0, 0)),
                  pl.BlockSpec((1, bsz, n_ctx, w), lambda q: (q, 0, 0, 0)),
                  pl.BlockSpec((2, w // 2, w // 2), lambda q: (q, 0, 0)),
                  pl.BlockSpec((1, w, w), lambda q: (q, 0, 0)),
                  pl.BlockSpec((1, w, w), lambda q: (q, 0, 0)),
                  pl.BlockSpec((1, 8, LANES), lambda q: (q, 0, 0))],
        out_specs=pl.BlockSpec((1, bsz, n_lat, w), lambda q: (q, 0, 0, 0)),
        out_shape=jax.ShapeDtypeStruct((q_n, bsz, n_lat, w), BF16),
        scratch_shapes=[pltpu.VMEM((n_blk, bsz * STATE_PITCH, LANES), F32),
                        pltpu.VMEM((n_blk, bsz * n_ctx, LANES), F32),
                        pltpu.VMEM((n_blk, bsz * STATE_PITCH, LANES), F32)],
        compiler_params=_params(("parallel",)),
        name="s5_scan",
    )(u_c, uc_c, t_mat, mb_pair, mc_pair, a_rows)


def _first_max(rows):
    best = rows[0]
    for r in rows[1:]:
        best = jnp.maximum(best, r)
    idx = jnp.full(best.shape, float(len(rows) - 1), F32)
    for k in range(len(rows) - 2, -1, -1):
        idx = jnp.where(rows[k] == best, float(k), idx)
    return best, idx


def _route_rows(lg):
    g_rows = [lg[k:k + 1] for k in range(N_GROUPS)]
    g_max, g_idx = _first_max(g_rows)
    g_sum = sum(jnp.exp(r - g_max) for r in g_rows)
    g_p = 1.0 / g_sum
    e_rows = []
    for j in range(EXPERTS_PER_GROUP):
        r = lg[N_GROUPS + (N_GROUPS - 1) * EXPERTS_PER_GROUP + j:][:1]
        for g in range(N_GROUPS - 2, -1, -1):
            k = N_GROUPS + g * EXPERTS_PER_GROUP + j
            r = jnp.where(g_idx == float(g), lg[k:k + 1], r)
        e_rows.append(r)
    v1, i1 = _first_max(e_rows)
    rest = [jnp.where(i1 == float(j), -jnp.inf, e_rows[j]) for j in range(EXPERTS_PER_GROUP)]
    v2, i2 = _first_max(rest)
    e21 = jnp.exp(v2 - v1)
    w1 = g_p / (1.0 + e21)
    w2 = w1 * e21
    lo = jnp.minimum(i1, i2)
    hi = jnp.maximum(i1, i2)
    base = jnp.where(lo == 0.0, 0.0, jnp.where(lo == 1.0, 3.0, 5.0))
    bucket = g_idx * float(PAIRS_PER_GROUP) + base + hi - lo - 1.0
    first_is_lo = i1 < i2
    return bucket, jnp.where(first_is_lo, w1, w2), jnp.where(first_is_lo, w2, w1)


def _mix_kernel(x_ref, xup_ref, xdn_ref, y_ref, mod_ref, g1_ref, g2_ref, cw_ref, win_ref, wglu_ref, bglu_ref,
                wout_ref, wr_ref, br_ref, *rest):
    n_cast = (len(rest) - 4) // 2
    x1_ref, h2p_ref, route_ref, counts_ref = rest[n_cast:n_cast + 4]
    for src, dst in zip(rest[:n_cast], rest[n_cast + 4:]):
        dst[...] = src[...].astype(BF16)
    i = pl.program_id(1)
    tm = x_ref.shape[1]
    d = x_ref.shape[2]
    sub = min(MIX_SUB, tm)
    n_sub = tm // sub
    cw, rw = CONV_WIDTH, CONV_ROW_WIDTH

    def hidden(xv):
        return _modulated_norm(xv, g1_ref[...], mod_ref[0, 0:1, :], mod_ref[0, 1:2, :]).astype(BF16)

    def halo(h):
        zh = jnp.dot(h, win_ref[:, 2 * cw:3 * cw], preferred_element_type=F32)
        return zh[:, 0:rw] * zh[:, rw:cw]

    @pl.when(jnp.logical_and(pl.program_id(0) == 0, i == 0))
    def _():
        counts_ref[...] = jnp.zeros_like(counts_ref)

    def sub_tile(s):
        r0 = s * sub
        xv = x_ref[0, r0:r0 + sub, :]
        hx = hidden(xv)
        g = jax.nn.gelu(jnp.concatenate(_unpack_bf16_pair(y_ref[0, r0:r0 + sub, :]), axis=1))
        hx_up = hidden(xup_ref[0]) if s == 0 else None
        hx_dn = hidden(xdn_ref[0]) if s == n_sub - 1 else None
        yield
        z_r = jnp.dot(hx, win_ref[:, cw:2 * cw], preferred_element_type=F32)
        cr = z_r[:, 0:rw] * z_r[:, rw:cw]
        z_c = jnp.dot(hx, win_ref[:, 2 * cw:3 * cw], preferred_element_type=F32)
        cc = z_c[:, 0:rw] * z_c[:, rw:cw]
        col_products[s] = cc
        yield
        bg = jnp.dot(hx, win_ref[:, 0:cw], preferred_element_type=F32)
        if s == 0:
            up_halo = jnp.where(i == 0, 0.0, halo(hx_up))
        if s == n_sub - 1:
            dn_halo = jnp.where(i == pl.num_programs(1) - 1, 0.0, halo(hx_dn))
        glu = g * jax.nn.sigmoid(jnp.dot(g.astype(BF16), wglu_ref[...], preferred_element_type=F32) + bglu_ref[...])
        yield
        row = lax.broadcasted_iota(jnp.int32, (sub, 1), 0)
        col_in_row = row % GRID_W
        left = jnp.where(col_in_row == 0, 0.0, pltpu.roll(cr, 1, axis=0))
        right = jnp.where(col_in_row == GRID_W - 1, 0.0, pltpu.roll(cr, sub - 1, axis=0))
        w_r = cw_ref[:, :rw]
        row_part = left * w_r[0:1] + cr * w_r[1:2] + right * w_r[2:3]
        if s > 0:
            up_halo = col_products[s - 1][sub - GRID_W:]
        if s < n_sub - 1:
            dn_halo = col_products[s + 1][:GRID_W]
        up = jnp.concatenate([up_halo, cc[:sub - GRID_W]], axis=0)
        dn = jnp.concatenate([cc[GRID_W:], dn_halo], axis=0)
        w_c = cw_ref[:, rw:]
        col_part = up * w_c[0:1] + cc * w_c[1:2] + dn * w_c[2:3]
        y_row = (bg[:, 0:rw] * row_part).astype(BF16)
        y_col = (bg[:, rw:cw] * col_part).astype(BF16)
        mixed = jnp.concatenate([glu.astype(BF16), y_row, y_col], axis=1)
        yield
        yx = jnp.dot(mixed, wout_ref[...], preferred_element_type=F32)
        yield
        x1 = xv + mod_ref[0, 2:3, :] * yx
        x1_ref[0, r0:r0 + sub, :] = x1
        h2 = _modulated_norm(x1, g2_ref[...], mod_ref[0, 3:4, :], mod_ref[0, 4:5, :])
        h2b = h2.astype(BF16)
        lg2 = lax.dot_general(wr_ref[...], h2b, (((1,), (1,)), ((), ())), preferred_element_type=F32)
        lg = lg2[:ROUTER_ROWS] + lg2[ROUTER_ROWS:] + br_ref[...]
        bucket, w_a, w_b = _route_rows(lg)
        r8 = lax.broadcasted_iota(jnp.int32, (8, sub), 0)
        route_ref[:, r0:r0 + sub] = jnp.where(r8 == 0, bucket, jnp.where(r8 == 1, w_a, jnp.where(r8 == 2, w_b, 0.0)))
        rl = lax.broadcasted_iota(jnp.int32, (ROW_EXTRA, sub), 0)
        gates_t = jnp.where(rl == 0, w_a, jnp.where(rl == 1, w_b, 0.0))
        h2p_ref[0, r0:r0 + sub, 0:d // 2] = _pack_bf16_pair(h2[:, :d // 2], h2[:, d // 2:])
        h2p_ref[0, r0:r0 + sub, d // 2:] = lax.bitcast_convert_type(gates_t.T, U32)
        rb = lax.broadcasted_iota(jnp.int32, (BUCKET_ROWS, sub), 0).astype(F32)
        counts_ref[...] += jnp.sum(jnp.where(rb == bucket, 1.0, 0.0), axis=-1, keepdims=True)
        yield

    n_stage = 6
    col_products = {}
    tiles = [sub_tile(s) for s in range(n_sub)]
    for step in range(n_sub + n_stage - 1):
        for s in reversed(range(n_sub)):
            if 0 <= step - s < n_stage:
                next(tiles[s])


def _mix(x, y_s5, mods, norm1_g, norm2_g, conv_w, w_conv, w_glu, b_glu, w_out, w_router_t, b_router_t, tm, b0, nb,
         to_cast):
    _, n, d = x.shape
    nt = n // tm
    n_steps = nb * nt
    cast_specs = []
    for w in to_cast:
        per = -(-w.shape[0] // n_steps)
        n_blk = w.shape[0] // per
        assert n_blk * per == w.shape[0]
        cast_specs.append(pl.BlockSpec((per,) + w.shape[1:],
                                       lambda b, i, n_blk=n_blk: ((b * nt + i) * n_blk // n_steps, 0, 0)))
    halo_blocks = n // GRID_W
    per_tile = tm // GRID_W
    tok_out = lambda w: pl.BlockSpec((1, tm, w), lambda b, i: (b, i, 0))
    full = lambda a: pl.BlockSpec(a.shape, lambda b, i: (0,) * a.ndim)
    args = (x, x, x, y_s5, mods, norm1_g.reshape(1, d), norm2_g.reshape(1, d), conv_w, w_conv, w_glu,
            b_glu.reshape(1, -1), w_out, w_router_t, b_router_t)
    in_specs = [pl.BlockSpec((1, tm, d), lambda b, i: (b + b0, i, 0)),
                pl.BlockSpec((1, GRID_W, d), lambda b, i: (b + b0, jnp.maximum(i * per_tile - 1, 0), 0)),
                pl.BlockSpec((1, GRID_W, d),
                             lambda b, i: (b + b0, jnp.minimum((i + 1) * per_tile, halo_blocks - 1), 0)),
                pl.BlockSpec((1, tm, LANES), lambda b, i: (b + b0, i, 0)),
                pl.BlockSpec((1, N_MOD, d), lambda b, i: (b + b0, 0, 0))] + [full(a) for a in args[5:]]
    return pl.pallas_call(
        _mix_kernel,
        grid=(nb, nt),
        in_specs=in_specs + cast_specs,
        out_specs=[tok_out(d), tok_out(d // 2 + ROW_EXTRA),
                   pl.BlockSpec((8, tm), lambda b, i: (0, b * nt + i)),
                   pl.BlockSpec((BUCKET_ROWS, LANES), lambda b, i: (0, 0))] + cast_specs,
        out_shape=[jax.ShapeDtypeStruct((nb, n, d), F32),
                   jax.ShapeDtypeStruct((nb, n, d // 2 + ROW_EXTRA), U32),
                   jax.ShapeDtypeStruct((8, nb * n), F32),
                   jax.ShapeDtypeStruct((BUCKET_ROWS, LANES), F32)]
        + [jax.ShapeDtypeStruct(w.shape, BF16) for w in to_cast],
        compiler_params=_params(("arbitrary", "arbitrary")),
        name="mix",
    )(*args, *to_cast)


def _rank_kernel(route_ref, offs_ref, dest_ref, run_ref):
    tr = route_ref.shape[1]
    n_blk = tr // LANES

    @pl.when(pl.program_id(0) == 0)
    def _():
        run_ref[...] = jnp.zeros_like(run_ref)

    bucket = route_ref[0:1, :]
    rb = lax.broadcasted_iota(jnp.int32, (BUCKET_ROWS, tr), 0).astype(F32)
    onehot = jnp.where(rb == bucket, 1.0, 0.0)
    blocks = [onehot[:, k * LANES:(k + 1) * LANES] for k in range(n_blk)]
    s_idx = lax.broadcasted_iota(jnp.int32, (LANES, LANES), 0)
    t_idx = lax.broadcasted_iota(jnp.int32, (LANES, LANES), 1)
    tri = jnp.where(s_idx <= t_idx, 1.0, 0.0).astype(BF16)
    prefix = jnp.dot(jnp.concatenate(blocks, axis=0).astype(BF16), tri, preferred_element_type=F32)
    start = run_ref[:, 0:1] + offs_ref[:, 0:1]
    carry = start
    for k in range(n_blk):
        pk = prefix[k * BUCKET_ROWS:(k + 1) * BUCKET_ROWS]
        dest = jnp.sum(blocks[k] * (pk - 1.0 + carry), axis=0, keepdims=True)
        dest_ref[:, k * LANES:(k + 1) * LANES] = dest.astype(jnp.int32)
        carry = carry + pk[:, LANES - 1:LANES]
    run_ref[...] += carry - start


def _rank(route_t, offs_rows, tr):
    n = route_t.shape[1]
    return pl.pallas_call(
        _rank_kernel,
        grid=(n // tr,),
        in_specs=[pl.BlockSpec((8, tr), lambda i: (0, i)),
                  pl.BlockSpec((BUCKET_ROWS, LANES), lambda i: (0, 0))],
        out_specs=pl.BlockSpec((1, tr), lambda i: (0, i)),
        out_shape=jax.ShapeDtypeStruct((1, n), jnp.int32),
        scratch_shapes=[pltpu.VMEM((BUCKET_ROWS, LANES), F32)],
        compiler_params=_params(("arbitrary",)),
        name="rank",
    )(route_t, offs_rows)


def _sc_mesh():
    return plsc.VectorSubcoreMesh(core_axis_name="core", subcore_axis_name="subcore")


def _scatter_rows(src, dest, n_out):
    n, w = src.shape
    sub = SC_INDEX_TILE // SC_ROWS

    @functools.partial(pl.kernel, out_type=jax.ShapeDtypeStruct((n_out, w), src.dtype), mesh=_sc_mesh(),
                       scratch_types=[], name="scatter_rows")
    def scatter(x_hbm, i_hbm, o_hbm):
        def body(x_vmem, i_vmem):
            j = pl.program_id(1)
            pltpu.sync_copy(x_vmem, o_hbm.at[i_vmem.at[0, pl.ds(j * SC_ROWS, SC_ROWS)]])

        pltpu.emit_pipeline(
            body,
            grid=(n // SC_INDEX_TILE, sub),
            in_specs=[pl.BlockSpec((SC_ROWS, w), lambda i, j: (i * sub + j, 0)),
                      pl.BlockSpec((1, SC_INDEX_TILE), lambda i, j: (0, i))],
            out_specs=[],
            core_axis_name=("core", "subcore"),
            dimension_semantics=(pltpu.PARALLEL, pltpu.ARBITRARY),
        )(x_hbm, i_hbm)

    return scatter(src, dest)


def _gather_rows(src, idx):
    n = idx.shape[1]
    w = src.shape[1]
    sub = SC_INDEX_TILE // SC_ROWS

    @functools.partial(pl.kernel, out_type=jax.ShapeDtypeStruct((n, w), src.dtype), mesh=_sc_mesh(),
                       scratch_types=[], name="gather_rows")
    def gather(x_hbm, i_hbm, o_hbm):
        def body(i_vmem, o_vmem):
            j = pl.program_id(1)
            pltpu.sync_copy(x_hbm.at[i_vmem.at[0, pl.ds(j * SC_ROWS, SC_ROWS)]], o_vmem)

        pltpu.emit_pipeline(
            body,
            grid=(n // SC_INDEX_TILE, sub),
            in_specs=[pl.BlockSpec((1, SC_INDEX_TILE), lambda i, j: (0, i))],
            out_specs=[pl.BlockSpec((SC_ROWS, w), lambda i, j: (i * sub + j, 0))],
            core_axis_name=("core", "subcore"),
            dimension_semantics=(pltpu.PARALLEL, pltpu.ARBITRARY),
        )(i_hbm, o_hbm)

    return gather(src, idx)


def _moe_kernel(ea_ref, eb_ref, valid_ref, blk_ref, rows_ref, w1a_ref, w3a_ref, w2a_ref, w1b_ref, w3b_ref, w2b_ref,
                o_ref):
    j = pl.program_id(0)
    half = rows_ref.shape[1] - ROW_EXTRA

    @pl.when(valid_ref[j] != 0)
    def _():
        ha, hb = _unpack_bf16_pair(rows_ref[:, 0:half])
        ha = ha.astype(BF16)
        hb = hb.astype(BF16)
        gates = lax.bitcast_convert_type(rows_ref[:, half:], F32)

        def expert(w1_ref, w3_ref, w2_ref, gate):
            def up(w_ref):
                return (jnp.dot(ha, w_ref[0, 0:half, :], preferred_element_type=F32)
                        + jnp.dot(hb, w_ref[0, half:, :], preferred_element_type=F32))

            a1 = up(w1_ref)
            he = (a1 * jax.nn.sigmoid(a1)) * up(w3_ref) * gate
            return jnp.dot(he.astype(BF16), w2_ref[0], preferred_element_type=F32)

        y = (expert(w1a_ref, w3a_ref, w2a_ref, gates[:, 0:1]) + expert(w1b_ref, w3b_ref, w2b_ref, gates[:, 1:2]))
        o_ref[...] = _pack_bf16_pair(y[:, :half], y[:, half:])


def _moe_grouped(rows, tile_ea, tile_eb, tile_valid, tile_blk, w1, w3, w2, tmm):
    r, w = rows.shape
    _, de, d = w2.shape
    up_a = pl.BlockSpec((1, d, de), lambda j, ea, eb, va, bk: (ea[j], 0, 0))
    up_b = pl.BlockSpec((1, d, de), lambda j, ea, eb, va, bk: (eb[j], 0, 0))
    grid_spec = pltpu.PrefetchScalarGridSpec(
        num_scalar_prefetch=4,
        grid=(r // tmm,),
        in_specs=[pl.BlockSpec((tmm, w), lambda j, ea, eb, va, bk: (bk[j], 0)),
                  up_a, up_a, pl.BlockSpec((1, de, d), lambda j, ea, eb, va, bk: (ea[j], 0, 0)),
                  up_b, up_b, pl.BlockSpec((1, de, d), lambda j, ea, eb, va, bk: (eb[j], 0, 0))],
        out_specs=pl.BlockSpec((tmm, d // 2), lambda j, ea, eb, va, bk: (bk[j], 0)),
    )
    return pl.pallas_call(
        _moe_kernel,
        grid_spec=grid_spec,
        out_shape=jax.ShapeDtypeStruct((r, d // 2), U32),
        compiler_params=_params(("arbitrary",)),
        name="moe",
    )(tile_ea, tile_eb, tile_valid, tile_blk, rows, w1, w3, w2, w1, w3, w2)


def _final_kernel(x1_ref, moe_ref, mod_ref, fg_ref, *rest):
    o_ref = rest[-1]
    ya, yb = _unpack_bf16_pair(moe_ref[0])
    half = ya.shape[1]
    gate = mod_ref[0, 5:6, :]
    xa = x1_ref[0, :, 0:half] + gate[:, 0:half] * ya
    xb = x1_ref[0, :, half:] + gate[:, half:] * yb
    ms = (jnp.sum(xa * xa, axis=-1, keepdims=True) + jnp.sum(xb * xb, axis=-1, keepdims=True)) / (2 * half)
    inv = lax.rsqrt(ms + RMS_EPS)
    o_ref[0, :, 0:half] = xa * inv * fg_ref[:, 0:half]
    o_ref[0, :, half:] = xb * inv * fg_ref[:, half:]


def _final(x1, moe_tok, mods, final_g, tm, b0, bsz, out_prev):
    nb, n, d = x1.shape
    tok = lambda w: pl.BlockSpec((1, tm, w), lambda b, i: (b, i, 0))
    args = [x1, moe_tok, mods, final_g.reshape(1, d)]
    in_specs = [tok(d), tok(d // 2), pl.BlockSpec((1, N_MOD, d), lambda b, i: (b + b0, 0, 0)),
                pl.BlockSpec((1, d), lambda b, i: (0, 0))]
    aliases = {}
    if out_prev is not None:
        args.append(out_prev)
        in_specs.append(pl.BlockSpec(memory_space=pl.ANY))
        aliases = {len(args) - 1: 0}
    return pl.pallas_call(
        _final_kernel,
        grid=(nb, n // tm),
        in_specs=in_specs,
        out_specs=pl.BlockSpec((1, tm, d), lambda b, i: (b + b0, i, 0)),
        out_shape=jax.ShapeDtypeStruct((bsz, n, d), F32),
        input_output_aliases=aliases,
        compiler_params=_params(("parallel", "parallel")),
        name="final",
    )(*args)


def _tile_plan(counts, tmm, n_tiles):
    tiles = (counts + (tmm - 1)) // tmm
    tile_end = jnp.cumsum(tiles)
    offs = (tile_end - tiles) * tmm
    n_valid = tile_end[-1]
    j = jnp.arange(n_tiles, dtype=jnp.int32)
    bucket = jnp.sum((tile_end[None, :] <= jnp.minimum(j, n_valid - 1)[:, None]).astype(jnp.int32), axis=1)
    pair_lo = jnp.array([0, 0, 0, 1, 1, 2], jnp.int32)
    pair_hi = jnp.array([1, 2, 3, 2, 3, 3], jnp.int32)
    group = bucket // PAIRS_PER_GROUP
    pair = bucket % PAIRS_PER_GROUP
    tile_ea = group * EXPERTS_PER_GROUP + pair_lo[pair]
    tile_eb = group * EXPERTS_PER_GROUP + pair_hi[pair]
    return offs, tile_ea, tile_eb, (j < n_valid).astype(jnp.int32), jnp.minimum(j, n_valid - 1)


def kernel(x, c, ctx, c_ctx, w_mod, b_mod, norm1_g, norm2_g, w_in, s5_lambda_re, s5_lambda_im, s5_log_dt,
           s5_b_re, s5_b_im, s5_c_re, s5_c_im, s5_d, w_glu, b_glu, conv_w, w_out, router_group_w,
           router_group_b, router_expert_w, router_expert_b, expert_w1, expert_w3, expert_w2, final_g):
    assert w_mod.shape[0] == 1, "single-layer kernel"
    bsz, n_tok, d = x.shape
    l = 0
    tm = min(TOKEN_TILE, n_tok)

    n_cond = bsz + 1
    pad = (-n_cond) % 8
    cond = jnp.concatenate([c, c_ctx[None, :], jnp.zeros((pad, d), F32)], axis=0)
    m = _mod_rows(cond, w_mod[l], b_mod[l])
    mx = m[:bsz].reshape(bsz, N_MOD, d)
    mc = m[bsz:bsz + 1].reshape(1, N_MOD, d)

    w_in_b = w_in[l].astype(BF16)
    w_s5 = w_in_b[:, :S5_WIDTH]
    o_c = S5_WIDTH + CONV_WIDTH
    o_v = S5_WIDTH + 2 * CONV_WIDTH
    w_conv = jnp.concatenate(
        [w_in_b[:, S5_WIDTH:o_c], w_in_b[:, o_c:o_c + CONV_ROW_WIDTH], w_in_b[:, o_v:o_v + CONV_ROW_WIDTH],
         w_in_b[:, o_c + CONV_ROW_WIDTH:o_v], w_in_b[:, o_v + CONV_ROW_WIDTH:]], axis=1)
    u = _inproj(x, mx, True, norm1_g[l], w_s5, INPROJ_TILE, "inproj")
    uc = _inproj(ctx, mc, False, norm1_g[l], w_s5, INPROJ_TILE, "inproj_ctx")

    t_mat, mb_pair, mc_pair, a_rows = _s5_matrices(
        s5_lambda_re[l], s5_lambda_im[l], s5_log_dt[l], s5_b_re[l], s5_b_im[l], s5_c_re[l], s5_c_im[l], s5_d[l])
    y_c = _s5_scan(_chunkify(u, "chunkify"), _chunkify(uc, "chunkify_ctx"), t_mat, mb_pair, mc_pair, a_rows)
    y_s5 = _unchunkify(y_c)

    n_logits = N_GROUPS + N_EXPERTS
    w_router = jnp.concatenate(
        [router_group_w[l], router_expert_w[l], jnp.zeros((d, ROUTER_ROWS - n_logits), F32)], axis=1).T
    w_router_hi = w_router.astype(BF16)
    w_router_lo = (w_router - w_router_hi.astype(F32)).astype(BF16)
    w_router_t = jnp.concatenate([w_router_hi, w_router_lo], axis=0)
    b_router = jnp.concatenate([router_group_b[l], router_expert_b[l], jnp.zeros((ROUTER_ROWS - n_logits,), F32)])
    tm_mix = min(MIX_TILE, n_tok)
    b_router_t = jnp.broadcast_to(b_router[:, None], (ROUTER_ROWS, min(MIX_SUB, tm_mix)))

    n_parts = MOE_PARTS if bsz % MOE_PARTS == 0 else 1
    nb = bsz // n_parts
    n_part = nb * n_tok
    n_buckets = N_GROUPS * PAIRS_PER_GROUP
    n_rows = n_part + n_buckets * MOE_TILE
    w_glu_b = w_glu[l].astype(BF16)
    w_out_b = w_out[l].astype(BF16)
    experts_f32 = (expert_w1[l], expert_w3[l], expert_w2[l])
    cast_plan = [experts_f32] if n_parts == 1 else [experts_f32[:2], experts_f32[2:]] + [()] * (n_parts - 2)
    w_experts = []
    staged = []
    for p in range(n_parts):
        x1, h2p, route_t, counts, *w_cast = _mix(x, y_s5, mx, norm1_g[l], norm2_g[l], conv_w[l], w_conv, w_glu_b,
                                                 b_glu[l], w_out_b, w_router_t, b_router_t, tm_mix, p * nb, nb,
                                                 cast_plan[p])
        w_experts += w_cast
        offs, *tiles = _tile_plan(counts[:n_buckets, 0].astype(jnp.int32), MOE_TILE, n_rows // MOE_TILE)
        offs_rows = jnp.zeros((BUCKET_ROWS,), F32).at[:n_buckets].set(offs.astype(F32))
        dest = _rank(route_t, jnp.broadcast_to(offs_rows[:, None], (BUCKET_ROWS, LANES)), min(RANK_TILE, n_part))
        rows = _scatter_rows(h2p.reshape(n_part, d // 2 + ROW_EXTRA), dest, n_rows)
        staged.append((x1, rows, dest, tiles))
    out = None
    for p, (x1, rows, dest, tiles) in enumerate(staged):
        y_rows = _moe_grouped(rows, *tiles, *w_experts, MOE_TILE)
        moe_tok = _gather_rows(y_rows, dest).reshape(nb, n_tok, d // 2)
        out = _final(x1, moe_tok, mx, final_g, tm, p * nb, bsz, out)
    return out
```

```python
import functools

import jax
import jax.numpy as jnp
from jax import lax
from jax.experimental import pallas as pl
from jax.experimental.pallas import tpu as pltpu
from jax.experimental.pallas import tpu_sc as plsc

F32 = jnp.float32
BF16 = jnp.bfloat16
U32 = jnp.uint32

RMS_EPS = 1e-6
N_MOD = 6
GRID_W = 64
S5_WIDTH = 256
S5_H = 16
S5_P = 64
S5_GROUPS = S5_WIDTH // S5_H
S5_PAIRS = S5_GROUPS // 2
S5_CHUNK = 16
LANES = 128
STATE_PITCH = 136
CONV_WIDTH = 768
CONV_ROW_WIDTH = CONV_WIDTH // 2
N_GROUPS = 4
EXPERTS_PER_GROUP = 4
N_EXPERTS = N_GROUPS * EXPERTS_PER_GROUP
PAIRS_PER_GROUP = 6
ROUTER_ROWS = 32
BUCKET_ROWS = 32
ROW_EXTRA = 128
TOKEN_TILE = 2048
MOE_TILE = 512
RANK_TILE = 2048
INPROJ_TILE = 2048
MIX_TILE = 1024
MIX_SUB = 512
MOE_FIRST_SHARE = (5, 8)
LAYOUT_GROUPS = 4
SC_ROWS = 64
SC_INDEX_TILE = 128
VMEM_LIMIT = 52 * 1024 * 1024


def _params(sem, vmem=VMEM_LIMIT):
    return pltpu.CompilerParams(dimension_semantics=sem, vmem_limit_bytes=vmem)


def _pack_bf16_pair(a, b):
    ua = lax.bitcast_convert_type(a.astype(BF16).astype(F32), U32)
    ub = lax.bitcast_convert_type(b.astype(BF16).astype(F32), U32)
    return ua | (ub >> 16)


def _unpack_bf16_pair(w):
    a = lax.bitcast_convert_type(w & jnp.uint32(0xFFFF0000), F32)
    b = lax.bitcast_convert_type(w << 16, F32)
    return a, b


def _mod_kernel(c_ref, w_ref, b_ref, o_ref):
    c = c_ref[...]
    o_ref[...] = jnp.dot(c * jax.nn.sigmoid(c), w_ref[...], preferred_element_type=F32) + b_ref[...]


def _mod_rows(cond, w_mod, b_mod):
    n, d = cond.shape
    nout = w_mod.shape[1]
    bn = d
    return pl.pallas_call(
        _mod_kernel,
        grid=(nout // bn,),
        in_specs=[pl.BlockSpec((n, d), lambda j: (0, 0)),
                  pl.BlockSpec((d, bn), lambda j: (0, j)),
                  pl.BlockSpec((1, bn), lambda j: (0, j))],
        out_specs=pl.BlockSpec((n, bn), lambda j: (0, j)),
        out_shape=jax.ShapeDtypeStruct((n, nout), F32),
        compiler_params=_params(("arbitrary",)),
        name="mod",
    )(cond, w_mod, b_mod.reshape(1, nout))


def _modulated_norm(x, g, shift, scale):
    ms = jnp.mean(x * x, axis=-1, keepdims=True)
    return (x * lax.rsqrt(ms + RMS_EPS)) * (g * (1.0 + scale)) + shift


def _inproj_kernel(x_ref, mod_ref, g_ref, w_ref, u_ref):
    nbk, tm, d = x_ref.shape
    h = _modulated_norm(x_ref[...].reshape(nbk * tm, d), g_ref[...], mod_ref[0, 0:1, :], mod_ref[0, 1:2, :])
    u = jnp.dot(h.astype(BF16), w_ref[...], preferred_element_type=F32)
    packed = _pack_bf16_pair(u[:, 0:LANES], u[:, LANES:2 * LANES])
    for b in range(nbk):
        u_ref[b] = packed[b * tm:(b + 1) * tm]


def _inproj(x, mods, per_batch_mod, norm_g, w_s5, tile, name):
    bsz, n, d = x.shape
    tm = min(tile, n)
    nbk = 1 if per_batch_mod else max(1, min(bsz, tile // n))
    assert bsz % nbk == 0
    mod_map = (lambda b, i: (b, 0, 0)) if per_batch_mod else (lambda b, i: (0, 0, 0))
    assert S5_WIDTH == 2 * LANES
    return pl.pallas_call(
        _inproj_kernel,
        grid=(bsz // nbk, n // tm),
        in_specs=[pl.BlockSpec((nbk, tm, d), lambda b, i: (b, i, 0)),
                  pl.BlockSpec((1, N_MOD, d), mod_map),
                  pl.BlockSpec((1, d), lambda b, i: (0, 0)),
                  pl.BlockSpec((d, S5_WIDTH), lambda b, i: (0, 0))],
        out_specs=pl.BlockSpec((nbk, tm, LANES), lambda b, i: (b, i, 0)),
        out_shape=jax.ShapeDtypeStruct((bsz, n, LANES), U32),
        compiler_params=_params(("parallel", "parallel")),
        name=name,
    )(x, mods, norm_g.reshape(1, d), w_s5)


def _toeplitz_kernel(strip_ref, t_ref):
    lc = t_ref.shape[1] // S5_H
    for s in range(lc):
        off = (lc - 1 - s) * S5_H
        t_ref[0, s * S5_H:(s + 1) * S5_H, :] = strip_ref[0, :, off:off + lc * S5_H].astype(BF16)


def _toeplitz(strip):
    g_n, h_n, w = strip.shape
    n = S5_CHUNK * h_n
    return pl.pallas_call(
        _toeplitz_kernel,
        grid=(g_n,),
        in_specs=[pl.BlockSpec((1, h_n, w), lambda g: (g, 0, 0))],
        out_specs=pl.BlockSpec((1, n, n), lambda g: (g, 0, 0)),
        out_shape=jax.ShapeDtypeStruct((g_n, n, n), BF16),
        compiler_params=_params(("parallel",)),
        name="toeplitz",
    )(strip)


def _s5_matrices(lam_re, lam_im, log_dt, b_re, b_im, c_re, c_im, d_skip):
    lc, g_n, p_n, h_n = S5_CHUNK, S5_GROUPS, S5_P, S5_H
    lam = lax.complex(lam_re.astype(F32), lam_im.astype(F32))
    dt = jnp.exp(log_dt.astype(F32))[..., None]
    a_bar = jnp.exp(lam * dt)
    b_bar = ((a_bar - 1.0) / lam)[..., None] * lax.complex(b_re.astype(F32), b_im.astype(F32))
    cm = lax.complex(c_re.astype(F32), c_im.astype(F32))
    steps = jnp.arange(lc + 1, dtype=F32)
    apow = jnp.exp((lam * dt)[:, :, None, :] * steps[None, None, :, None])
    kern = jnp.einsum('dgop,dgjp,dgpi->dgjio', cm, apow[:, :, :lc], b_bar).real
    skip = jnp.eye(h_n, dtype=F32) * d_skip.astype(F32).reshape(g_n, 1, h_n)
    centre = kern[0, :, 0] + kern[1, :, 0] + skip
    lags = jnp.concatenate([kern[1, :, :0:-1], centre[:, None], kern[0, :, 1:]], axis=1)
    strip = lags.transpose(0, 2, 1, 3).reshape(g_n, h_n, (2 * lc - 1) * h_n)
    strip = jnp.pad(strip, ((0, 0), (0, 0), (0, h_n)))
    t_mat = _toeplitz(strip)

    def in_mat(pw, bb):
        return (pw[:, :, None, :] * bb.transpose(0, 2, 1)[:, None, :, :]).reshape(g_n, lc * h_n, p_n)

    mb_f = in_mat(apow[0, :, lc - 1::-1][:, :lc], b_bar[0])
    mb_b = in_mat(apow[1, :, :lc], b_bar[1])

    def out_mat(pw, cc):
        return (pw.transpose(0, 2, 1)[:, :, :, None] * cc.transpose(0, 2, 1)[:, :, None, :]).reshape(
            g_n, p_n, lc * h_n)

    mc_f = out_mat(apow[0, :, 1:lc + 1], cm[0])
    mc_b = out_mat(apow[1, :, lc:0:-1], cm[1])
    a_chunk = apow[:, :, lc]

    q_n = S5_PAIRS
    zeros_in = jnp.zeros((g_n, lc * h_n, p_n), F32)

    def pair_cols(m):
        m = m.reshape(q_n, 2, lc * h_n, p_n)
        z = zeros_in.reshape(q_n, 2, lc * h_n, p_n)[:, 0]
        top = jnp.concatenate([m[:, 0], z], axis=-1)
        bot = jnp.concatenate([z, m[:, 1]], axis=-1)
        return jnp.concatenate([top, bot], axis=1)

    mb_pair = jnp.concatenate([pair_cols(mb_f.real), pair_cols(mb_f.imag),
                               pair_cols(mb_b.real), pair_cols(mb_b.imag)], axis=-1)

    def pair_rows(m):
        m = m.reshape(q_n, 2, p_n, lc * h_n)
        z = jnp.zeros_like(m[:, 0])
        top = jnp.concatenate([m[:, 0], z], axis=-1)
        bot = jnp.concatenate([z, m[:, 1]], axis=-1)
        return jnp.concatenate([top, bot], axis=1)

    mc_pair = jnp.concatenate([pair_rows(mc_f.real), pair_rows(-mc_f.imag),
                               pair_rows(mc_b.real), pair_rows(-mc_b.imag)], axis=1)
    a_rows = jnp.stack([a_chunk[0].real, a_chunk[0].imag, a_chunk[1].real, a_chunk[1].imag], axis=0)
    a_rows = a_rows.reshape(4, q_n, 2 * p_n).transpose(1, 0, 2)
    a_rows = jnp.concatenate([a_rows, jnp.zeros_like(a_rows)], axis=1)
    return t_mat, mb_pair.astype(BF16), mc_pair.astype(BF16), a_rows


def _chunkify_kernel(u_ref, o_ref, *, gb):
    _, nb, nc, _ = o_ref.shape
    half = S5_CHUNK * S5_H
    per_slab = LANES // S5_H
    for b0 in range(0, nb, gb):
        slabs = ([], [])
        for t in range(S5_CHUNK):
            rows = [u_ref[b0 + b, pl.ds(t, nc, stride=S5_CHUNK), :] for b in range(gb)]
            for j, part in enumerate(_unpack_bf16_pair(rows[0] if gb == 1 else jnp.concatenate(rows, axis=0))):
                slabs[j].append(part.T)
        for j, cols in enumerate(slabs):
            for gl in range(per_slab):
                g = j * per_slab + gl
                m = jnp.concatenate([c[gl * S5_H:(gl + 1) * S5_H, :] for c in cols], axis=0)
                o_ref[g // 2, b0:b0 + gb, :, (g % 2) * half:(g % 2 + 1) * half] = (
                    m.T.astype(BF16).reshape(gb, nc, half))


def _layout_step(bsz, gb):
    return gb * LAYOUT_GROUPS if bsz % (gb * LAYOUT_GROUPS) == 0 else gb


def _chunkify(u_rows, name):
    bsz, n, _ = u_rows.shape
    nc = n // S5_CHUNK
    gb = min(bsz, max(1, LANES // nc))
    nb = _layout_step(bsz, gb)
    w = 2 * S5_CHUNK * S5_H
    return pl.pallas_call(
        functools.partial(_chunkify_kernel, gb=gb),
        grid=(bsz // nb,),
        in_specs=[pl.BlockSpec((nb, n, LANES), lambda b: (b, 0, 0))],
        out_specs=pl.BlockSpec((S5_PAIRS, nb, nc, w), lambda b: (0, b, 0, 0)),
        out_shape=jax.ShapeDtypeStruct((S5_PAIRS, bsz, nc, w), BF16),
        compiler_params=_params(("parallel",)),
        name=name,
    )(u_rows)


def _unchunkify_kernel(y_ref, o_ref):
    _, nb, nc, _ = y_ref.shape
    half = S5_CHUNK * S5_H
    per_slab = LANES // S5_H
    for b in range(nb):
        rows = []
        for g in range(S5_GROUPS):
            rows.append(y_ref[g // 2, b, :, (g % 2) * half:(g % 2 + 1) * half].astype(F32).T)
        for t in range(S5_CHUNK):
            tiles = [jnp.concatenate([r[t * S5_H:(t + 1) * S5_H, :] for r in rows[j * per_slab:(j + 1) * per_slab]],
                                     axis=0).T for j in range(2)]
            o_ref[b, pl.ds(t, nc, stride=S5_CHUNK), :] = _pack_bf16_pair(*tiles)


def _unchunkify(y_c):
    q_n, bsz, nc, w = y_c.shape
    n = nc * S5_CHUNK
    nb = _layout_step(bsz, 1)
    return pl.pallas_call(
        _unchunkify_kernel,
        grid=(bsz // nb,),
        in_specs=[pl.BlockSpec((q_n, nb, nc, w), lambda b: (0, b, 0, 0))],
        out_specs=pl.BlockSpec((nb, n, LANES), lambda b: (b, 0, 0)),
        out_shape=jax.ShapeDtypeStruct((bsz, n, LANES), U32),
        compiler_params=_params(("parallel",)),
        name="unchunkify",
    )(y_c)


def _s5_kernel(u_ref, uc_ref, t_ref, mb_ref, mc_ref, a_ref, y_ref, s_lat, s_ctx, h_scr, *, bb):
    _, bsz, n_lat, w = u_ref.shape
    n_ctx = uc_ref.shape[2]
    n_blk = w // LANES
    rb = bb * n_lat

    mb = mb_ref[0]

    def in_lat(i, carry):
        s = jnp.dot(u_ref[0, pl.ds(i * bb, bb)].reshape(rb, w), mb, preferred_element_type=F32)
        for k in range(bb):
            r = pl.multiple_of((i * bb + k) * STATE_PITCH, 8)
            for blk in range(n_blk):
                s_lat[blk, pl.ds(r, n_lat), :] = s[k * n_lat:(k + 1) * n_lat, blk * LANES:(blk + 1) * LANES]
        return carry

    lax.fori_loop(0, bsz // bb, in_lat, 0)
    sc = jnp.dot(uc_ref[0].reshape(bsz * n_ctx, w), mb, preferred_element_type=F32)
    for blk in range(n_blk):
        s_ctx[blk] = sc[:, blk * LANES:(blk + 1) * LANES]

    a_fr, a_fi, a_br, a_bi = (a_ref[0, k:k + 1, :] for k in range(4))

    def step(h, a_r, a_i, s_r, s_i):
        h_r, h_i = h
        return a_r * h_r - a_i * h_i + s_r, a_r * h_i + a_i * h_r + s_i

    def ctx_rows(blk, c):
        return s_ctx[blk, pl.ds(c, bsz, stride=n_ctx), :]

    def lat_rows(ref, blk, c):
        return ref.at[blk, pl.ds(c, bsz, stride=STATE_PITCH), :]

    def ctx_step(k, carry):
        hf, hb = carry
        kb = n_ctx - 1 - k
        hf = step(hf, a_fr, a_fi, ctx_rows(0, k), ctx_rows(1, k))
        hb = step(hb, a_br, a_bi, ctx_rows(2, kb), ctx_rows(3, kb))
        return hf, hb

    zero = jnp.zeros((bsz, LANES), F32)
    carry = lax.fori_loop(0, n_ctx, ctx_step, ((zero, zero), (zero, zero)))

    def lat_step(k, carry):
        hf, hb = carry
        kb = n_lat - 1 - k
        lat_rows(h_scr, 0, k)[...] = hf[0]
        lat_rows(h_scr, 1, k)[...] = hf[1]
        lat_rows(h_scr, 2, kb)[...] = hb[0]
        lat_rows(h_scr, 3, kb)[...] = hb[1]
        hf = step(hf, a_fr, a_fi, lat_rows(s_lat, 0, k)[...], lat_rows(s_lat, 1, k)[...])
        hb = step(hb, a_br, a_bi, lat_rows(s_lat, 2, kb)[...], lat_rows(s_lat, 3, kb)[...])
        return hf, hb

    lax.fori_loop(0, n_lat, lat_step, carry)

    t0 = t_ref[0]
    t1 = t_ref[1]
    mc = mc_ref[0]
    half = S5_CHUNK * S5_H

    def out_lat(i, carry):
        u = u_ref[0, pl.ds(i * bb, bb)].reshape(rb, w)
        h_rows = []
        for k in range(bb):
            r = pl.multiple_of((i * bb + k) * STATE_PITCH, 8)
            h_rows.append(jnp.concatenate([h_scr[blk, pl.ds(r, n_lat), :] for blk in range(n_blk)], axis=1))
        h = jnp.concatenate(h_rows, axis=0).astype(BF16)
        inter = jnp.dot(h, mc, preferred_element_type=F32)
        y0 = jnp.dot(u[:, :half], t0, preferred_element_type=F32) + inter[:, :half]
        y1 = jnp.dot(u[:, half:], t1, preferred_element_type=F32) + inter[:, half:]
        y = jnp.concatenate([y0, y1], axis=1).astype(BF16)
        y_ref[0, pl.ds(i * bb, bb)] = y.reshape(bb, n_lat, w)
        return carry

    lax.fori_loop(0, bsz // bb, out_lat, 0)


def _s5_scan(u_c, uc_c, t_mat, mb_pair, mc_pair, a_rows):
    q_n, bsz, n_lat, w = u_c.shape
    n_ctx = uc_c.shape[2]
    assert n_lat + 8 == STATE_PITCH
    bb = min(8, bsz)
    n_blk = w // LANES
    return pl.pallas_call(
        functools.partial(_s5_kernel, bb=bb),
        grid=(q_n,),
        in_specs=[pl.BlockSpec((1, bsz, n_lat, w), lambda q: (q, 0, 0, 0)),
                  pl.BlockSpec((1, bsz, n_ctx, w), lambda q: (q, 0, 0, 0)),
                  pl.BlockSpec((2, w // 2, w // 2), lambda q: (q, 0, 0)),
                  pl.BlockSpec((1, w, w), lambda q: (q, 0, 0)),
                  pl.BlockSpec((1, w, w), lambda q: (q, 0, 0)),
                  pl.BlockSpec((1, 8, LANES), lambda q: (q, 0, 0))],
        out_specs=pl.BlockSpec((1, bsz, n_lat, w), lambda q: (q, 0, 0, 0)),
        out_shape=jax.ShapeDtypeStruct((q_n, bsz, n_lat, w), BF16),
        scratch_shapes=[pltpu.VMEM((n_blk, bsz * STATE_PITCH, LANES), F32),
                        pltpu.VMEM((n_blk, bsz * n_ctx, LANES), F32),
                        pltpu.VMEM((n_blk, bsz * STATE_PITCH, LANES), F32)],
        compiler_params=_params(("parallel",)),
        name="s5_scan",
    )(u_c, uc_c, t_mat, mb_pair, mc_pair, a_rows)


def _first_max(rows):
    best = rows[0]
    for r in rows[1:]:
        best = jnp.maximum(best, r)
    idx = jnp.full(best.shape, float(len(rows) - 1), F32)
    for k in range(len(rows) - 2, -1, -1):
        idx = jnp.where(rows[k] == best, float(k), idx)
    return best, idx


def _route_rows(lg):
    g_rows = [lg[k:k + 1] for k in range(N_GROUPS)]
    g_max, g_idx = _first_max(g_rows)
    g_sum = sum(jnp.exp(r - g_max) for r in g_rows)
    g_p = 1.0 / g_sum
    e_rows = []
    for j in range(EXPERTS_PER_GROUP):
        r = lg[N_GROUPS + (N_GROUPS - 1) * EXPERTS_PER_GROUP + j:][:1]
        for g in range(N_GROUPS - 2, -1, -1):
            k = N_GROUPS + g * EXPERTS_PER_GROUP + j
            r = jnp.where(g_idx == float(g), lg[k:k + 1], r)
        e_rows.append(r)
    v1, i1 = _first_max(e_rows)
    rest = [jnp.where(i1 == float(j), -jnp.inf, e_rows[j]) for j in range(EXPERTS_PER_GROUP)]
    v2, i2 = _first_max(rest)
    e21 = jnp.exp(v2 - v1)
    w1 = g_p / (1.0 + e21)
    w2 = w1 * e21
    lo = jnp.minimum(i1, i2)
    hi = jnp.maximum(i1, i2)
    base = jnp.where(lo == 0.0, 0.0, jnp.where(lo == 1.0, 3.0, 5.0))
    bucket = g_idx * float(PAIRS_PER_GROUP) + base + hi - lo - 1.0
    first_is_lo = i1 < i2
    return bucket, jnp.where(first_is_lo, w1, w2), jnp.where(first_is_lo, w2, w1)


def _mix_kernel(x_ref, xup_ref, xdn_ref, y_ref, mod_ref, g1_ref, g2_ref, cw_ref, win_ref, wglu_ref, bglu_ref,
                wout_ref, wr_ref, br_ref, *rest):
    n_cast = (len(rest) - 4) // 2
    x1_ref, h2p_ref, route_ref, counts_ref = rest[n_cast:n_cast + 4]
    for src, dst in zip(rest[:n_cast], rest[n_cast + 4:]):
        dst[...] = src[...].astype(BF16)
    i = pl.program_id(1)
    tm = x_ref.shape[1]
    d = x_ref.shape[2]
    sub = min(MIX_SUB, tm)
    n_sub = tm // sub
    cw, rw = CONV_WIDTH, CONV_ROW_WIDTH

    def hidden(xv):
        return _modulated_norm(xv, g1_ref[...], mod_ref[0, 0:1, :], mod_ref[0, 1:2, :]).astype(BF16)

    def halo(h):
        zh = jnp.dot(h, win_ref[:, 2 * cw:3 * cw], preferred_element_type=F32)
        return zh[:, 0:rw] * zh[:, rw:cw]

    @pl.when(jnp.logical_and(pl.program_id(0) == 0, i == 0))
    def _():
        counts_ref[...] = jnp.zeros_like(counts_ref)

    def sub_tile(s):
        r0 = s * sub
        xv = x_ref[0, r0:r0 + sub, :]
        hx = hidden(xv)
        g = jax.nn.gelu(jnp.concatenate(_unpack_bf16_pair(y_ref[0, r0:r0 + sub, :]), axis=1))
        hx_up = hidden(xup_ref[0]) if s == 0 else None
        hx_dn = hidden(xdn_ref[0]) if s == n_sub - 1 else None
        yield
        z_r = jnp.dot(hx, win_ref[:, cw:2 * cw], preferred_element_type=F32)
        cr = z_r[:, 0:rw] * z_r[:, rw:cw]
        z_c = jnp.dot(hx, win_ref[:, 2 * cw:3 * cw], preferred_element_type=F32)
        cc = z_c[:, 0:rw] * z_c[:, rw:cw]
        col_products[s] = cc
        yield
        bg = jnp.dot(hx, win_ref[:, 0:cw], preferred_element_type=F32)
        if s == 0:
            up_halo = jnp.where(i == 0, 0.0, halo(hx_up))
        if s == n_sub - 1:
            dn_halo = jnp.where(i == pl.num_programs(1) - 1, 0.0, halo(hx_dn))
        glu = g * jax.nn.sigmoid(jnp.dot(g.astype(BF16), wglu_ref[...], preferred_element_type=F32) + bglu_ref[...])
        yield
        row = lax.broadcasted_iota(jnp.int32, (sub, 1), 0)
        col_in_row = row % GRID_W
        left = jnp.where(col_in_row == 0, 0.0, pltpu.roll(cr, 1, axis=0))
        right = jnp.where(col_in_row == GRID_W - 1, 0.0, pltpu.roll(cr, sub - 1, axis=0))
        w_r = cw_ref[:, :rw]
        row_part = left * w_r[0:1] + cr * w_r[1:2] + right * w_r[2:3]
        if s > 0:
            up_halo = col_products[s - 1][sub - GRID_W:]
        if s < n_sub - 1:
            dn_halo = col_products[s + 1][:GRID_W]
        up = jnp.concatenate([up_halo, cc[:sub - GRID_W]], axis=0)
        dn = jnp.concatenate([cc[GRID_W:], dn_halo], axis=0)
        w_c = cw_ref[:, rw:]
        col_part = up * w_c[0:1] + cc * w_c[1:2] + dn * w_c[2:3]
        y_row = (bg[:, 0:rw] * row_part).astype(BF16)
        y_col = (bg[:, rw:cw] * col_part).astype(BF16)
        mixed = jnp.concatenate([glu.astype(BF16), y_row, y_col], axis=1)
        yield
        yx = jnp.dot(mixed, wout_ref[...], preferred_element_type=F32)
        yield
        x1 = xv + mod_ref[0, 2:3, :] * yx
        x1_ref[0, r0:r0 + sub, :] = x1
        h2 = _modulated_norm(x1, g2_ref[...], mod_ref[0, 3:4, :], mod_ref[0, 4:5, :])
        h2b = h2.astype(BF16)
        lg2 = lax.dot_general(wr_ref[...], h2b, (((1,), (1,)), ((), ())), preferred_element_type=F32)
        lg = lg2[:ROUTER_ROWS] + lg2[ROUTER_ROWS:] + br_ref[...]
        bucket, w_a, w_b = _route_rows(lg)
        r8 = lax.broadcasted_iota(jnp.int32, (8, sub), 0)
        route_ref[:, r0:r0 + sub] = jnp.where(r8 == 0, bucket, jnp.where(r8 == 1, w_a, jnp.where(r8 == 2, w_b, 0.0)))
        rl = lax.broadcasted_iota(jnp.int32, (ROW_EXTRA, sub), 0)
        gates_t = jnp.where(rl == 0, w_a, jnp.where(rl == 1, w_b, 0.0))
        h2p_ref[0, r0:r0 + sub, 0:d // 2] = _pack_bf16_pair(h2[:, :d // 2], h2[:, d // 2:])
        h2p_ref[0, r0:r0 + sub, d // 2:] = lax.bitcast_convert_type(gates_t.T, U32)
        rb = lax.broadcasted_iota(jnp.int32, (BUCKET_ROWS, sub), 0).astype(F32)
        counts_ref[...] += jnp.sum(jnp.where(rb == bucket, 1.0, 0.0), axis=-1, keepdims=True)
        yield

    n_stage = 6
    col_products = {}
    tiles = [sub_tile(s) for s in range(n_sub)]
    for step in range(n_sub + n_stage - 1):
        for s in reversed(range(n_sub)):
            if 0 <= step - s < n_stage:
                next(tiles[s])


def _mix(x, y_s5, mods, norm1_g, norm2_g, conv_w, w_conv, w_glu, b_glu, w_out, w_router_t, b_router_t, tm, b0, nb,
         to_cast):
    _, n, d = x.shape
    nt = n // tm
    n_steps = nb * nt
    cast_specs = []
    for w in to_cast:
        per = -(-w.shape[0] // n_steps)
        while w.shape[0] % per:
            per += 1
        n_blk = w.shape[0] // per
        cast_specs.append(pl.BlockSpec((per,) + w.shape[1:],
                                       lambda b, i, n_blk=n_blk: ((b * nt + i) * n_blk // n_steps, 0, 0)))
    halo_blocks = n // GRID_W
    per_tile = tm // GRID_W
    tok_out = lambda w: pl.BlockSpec((1, tm, w), lambda b, i: (b, i, 0))
    full = lambda a: pl.BlockSpec(a.shape, lambda b, i: (0,) * a.ndim)
    args = (x, x, x, y_s5, mods, norm1_g.reshape(1, d), norm2_g.reshape(1, d), conv_w, w_conv, w_glu,
            b_glu.reshape(1, -1), w_out, w_router_t, b_router_t)
    in_specs = [pl.BlockSpec((1, tm, d), lambda b, i: (b + b0, i, 0)),
                pl.BlockSpec((1, GRID_W, d), lambda b, i: (b + b0, jnp.maximum(i * per_tile - 1, 0), 0)),
                pl.BlockSpec((1, GRID_W, d),
                             lambda b, i: (b + b0, jnp.minimum((i + 1) * per_tile, halo_blocks - 1), 0)),
                pl.BlockSpec((1, tm, LANES), lambda b, i: (b + b0, i, 0)),
                pl.BlockSpec((1, N_MOD, d), lambda b, i: (b + b0, 0, 0))] + [full(a) for a in args[5:]]
    return pl.pallas_call(
        _mix_kernel,
        grid=(nb, nt),
        in_specs=in_specs + cast_specs,
        out_specs=[tok_out(d), tok_out(d // 2 + ROW_EXTRA),
                   pl.BlockSpec((8, tm), lambda b, i: (0, b * nt + i)),
                   pl.BlockSpec((BUCKET_ROWS, LANES), lambda b, i: (0, 0))] + cast_specs,
        out_shape=[jax.ShapeDtypeStruct((nb, n, d), F32),
                   jax.ShapeDtypeStruct((nb, n, d // 2 + ROW_EXTRA), U32),
                   jax.ShapeDtypeStruct((8, nb * n), F32),
                   jax.ShapeDtypeStruct((BUCKET_ROWS, LANES), F32)]
        + [jax.ShapeDtypeStruct(w.shape, BF16) for w in to_cast],
        compiler_params=_params(("arbitrary", "arbitrary")),
        name="mix",
    )(*args, *to_cast)


def _rank_kernel(route_ref, offs_ref, dest_ref, run_ref):
    tr = route_ref.shape[1]
    n_blk = tr // LANES

    @pl.when(pl.program_id(0) == 0)
    def _():
        run_ref[...] = jnp.zeros_like(run_ref)

    bucket = route_ref[0:1, :]
    rb = lax.broadcasted_iota(jnp.int32, (BUCKET_ROWS, tr), 0).astype(F32)
    onehot = jnp.where(rb == bucket, 1.0, 0.0)
    blocks = [onehot[:, k * LANES:(k + 1) * LANES] for k in range(n_blk)]
    s_idx = lax.broadcasted_iota(jnp.int32, (LANES, LANES), 0)
    t_idx = lax.broadcasted_iota(jnp.int32, (LANES, LANES), 1)
    tri = jnp.where(s_idx <= t_idx, 1.0, 0.0).astype(BF16)
    prefix = jnp.dot(jnp.concatenate(blocks, axis=0).astype(BF16), tri, preferred_element_type=F32)
    start = run_ref[:, 0:1] + offs_ref[:, 0:1]
    carry = start
    for k in range(n_blk):
        pk = prefix[k * BUCKET_ROWS:(k + 1) * BUCKET_ROWS]
        dest = jnp.sum(blocks[k] * (pk - 1.0 + carry), axis=0, keepdims=True)
        dest_ref[:, k * LANES:(k + 1) * LANES] = dest.astype(jnp.int32)
        carry = carry + pk[:, LANES - 1:LANES]
    run_ref[...] += carry - start


def _rank(route_t, offs_rows, tr):
    n = route_t.shape[1]
    return pl.pallas_call(
        _rank_kernel,
        grid=(n // tr,),
        in_specs=[pl.BlockSpec((8, tr), lambda i: (0, i)),
                  pl.BlockSpec((BUCKET_ROWS, LANES), lambda i: (0, 0))],
        out_specs=pl.BlockSpec((1, tr), lambda i: (0, i)),
        out_shape=jax.ShapeDtypeStruct((1, n), jnp.int32),
        scratch_shapes=[pltpu.VMEM((BUCKET_ROWS, LANES), F32)],
        compiler_params=_params(("arbitrary",)),
        name="rank",
    )(route_t, offs_rows)


def _sc_mesh():
    return plsc.VectorSubcoreMesh(core_axis_name="core", subcore_axis_name="subcore")


def _scatter_rows(src, dest, n_out):
    n, w = src.shape
    sub = SC_INDEX_TILE // SC_ROWS

    @functools.partial(pl.kernel, out_type=jax.ShapeDtypeStruct((n_out, w), src.dtype), mesh=_sc_mesh(),
                       scratch_types=[], name="scatter_rows")
    def scatter(x_hbm, i_hbm, o_hbm):
        def body(x_vmem, i_vmem):
            j = pl.program_id(1)
            pltpu.sync_copy(x_vmem, o_hbm.at[i_vmem.at[0, pl.ds(j * SC_ROWS, SC_ROWS)]])

        pltpu.emit_pipeline(
            body,
            grid=(n // SC_INDEX_TILE, sub),
            in_specs=[pl.BlockSpec((SC_ROWS, w), lambda i, j: (i * sub + j, 0)),
                      pl.BlockSpec((1, SC_INDEX_TILE), lambda i, j: (0, i))],
            out_specs=[],
            core_axis_name=("core", "subcore"),
            dimension_semantics=(pltpu.PARALLEL, pltpu.ARBITRARY),
        )(x_hbm, i_hbm)

    return scatter(src, dest)


def _gather_rows(src, idx):
    n = idx.shape[1]
    w = src.shape[1]
    sub = SC_INDEX_TILE // SC_ROWS

    @functools.partial(pl.kernel, out_type=jax.ShapeDtypeStruct((n, w), src.dtype), mesh=_sc_mesh(),
                       scratch_types=[], name="gather_rows")
    def gather(x_hbm, i_hbm, o_hbm):
        def body(i_vmem, o_vmem):
            j = pl.program_id(1)
            pltpu.sync_copy(x_hbm.at[i_vmem.at[0, pl.ds(j * SC_ROWS, SC_ROWS)]], o_vmem)

        pltpu.emit_pipeline(
            body,
            grid=(n // SC_INDEX_TILE, sub),
            in_specs=[pl.BlockSpec((1, SC_INDEX_TILE), lambda i, j: (0, i))],
            out_specs=[pl.BlockSpec((SC_ROWS, w), lambda i, j: (i * sub + j, 0))],
            core_axis_name=("core", "subcore"),
            dimension_semantics=(pltpu.PARALLEL, pltpu.ARBITRARY),
        )(i_hbm, o_hbm)

    return gather(src, idx)


def _moe_kernel(ea_ref, eb_ref, valid_ref, blk_ref, rows_ref, w1a_ref, w3a_ref, w2a_ref, w1b_ref, w3b_ref, w2b_ref,
                o_ref):
    j = pl.program_id(0)
    half = rows_ref.shape[1] - ROW_EXTRA

    @pl.when(valid_ref[j] != 0)
    def _():
        ha, hb = _unpack_bf16_pair(rows_ref[:, 0:half])
        ha = ha.astype(BF16)
        hb = hb.astype(BF16)
        gates = lax.bitcast_convert_type(rows_ref[:, half:], F32)

        def expert(w1_ref, w3_ref, w2_ref, gate):
            def up(w_ref):
                return (jnp.dot(ha, w_ref[0, 0:half, :], preferred_element_type=F32)
                        + jnp.dot(hb, w_ref[0, half:, :], preferred_element_type=F32))

            a1 = up(w1_ref)
            he = (a1 * jax.nn.sigmoid(a1)) * up(w3_ref) * gate
            return jnp.dot(he.astype(BF16), w2_ref[0], preferred_element_type=F32)

        y = (expert(w1a_ref, w3a_ref, w2a_ref, gates[:, 0:1]) + expert(w1b_ref, w3b_ref, w2b_ref, gates[:, 1:2]))
        o_ref[...] = _pack_bf16_pair(y[:, :half], y[:, half:])


def _moe_grouped(rows, tile_ea, tile_eb, tile_valid, tile_blk, w1, w3, w2, tmm):
    r, w = rows.shape
    _, de, d = w2.shape
    up_a = pl.BlockSpec((1, d, de), lambda j, ea, eb, va, bk: (ea[j], 0, 0))
    up_b = pl.BlockSpec((1, d, de), lambda j, ea, eb, va, bk: (eb[j], 0, 0))
    grid_spec = pltpu.PrefetchScalarGridSpec(
        num_scalar_prefetch=4,
        grid=(r // tmm,),
        in_specs=[pl.BlockSpec((tmm, w), lambda j, ea, eb, va, bk: (bk[j], 0)),
                  up_a, up_a, pl.BlockSpec((1, de, d), lambda j, ea, eb, va, bk: (ea[j], 0, 0)),
                  up_b, up_b, pl.BlockSpec((1, de, d), lambda j, ea, eb, va, bk: (eb[j], 0, 0))],
        out_specs=pl.BlockSpec((tmm, d // 2), lambda j, ea, eb, va, bk: (bk[j], 0)),
    )
    return pl.pallas_call(
        _moe_kernel,
        grid_spec=grid_spec,
        out_shape=jax.ShapeDtypeStruct((r, d // 2), U32),
        compiler_params=_params(("arbitrary",)),
        name="moe",
    )(tile_ea, tile_eb, tile_valid, tile_blk, rows, w1, w3, w2, w1, w3, w2)


def _final_kernel(x1_ref, moe_ref, mod_ref, fg_ref, *rest):
    o_ref = rest[-1]
    ya, yb = _unpack_bf16_pair(moe_ref[0])
    half = ya.shape[1]
    gate = mod_ref[0, 5:6, :]
    xa = x1_ref[0, :, 0:half] + gate[:, 0:half] * ya
    xb = x1_ref[0, :, half:] + gate[:, half:] * yb
    ms = (jnp.sum(xa * xa, axis=-1, keepdims=True) + jnp.sum(xb * xb, axis=-1, keepdims=True)) / (2 * half)
    inv = lax.rsqrt(ms + RMS_EPS)
    o_ref[0, :, 0:half] = xa * inv * fg_ref[:, 0:half]
    o_ref[0, :, half:] = xb * inv * fg_ref[:, half:]


def _final(x1, moe_tok, mods, final_g, tm, b0, bsz, out_prev):
    nb, n, d = x1.shape
    tok = lambda w: pl.BlockSpec((1, tm, w), lambda b, i: (b, i, 0))
    args = [x1, moe_tok, mods, final_g.reshape(1, d)]
    in_specs = [tok(d), tok(d // 2), pl.BlockSpec((1, N_MOD, d), lambda b, i: (b + b0, 0, 0)),
                pl.BlockSpec((1, d), lambda b, i: (0, 0))]
    aliases = {}
    if out_prev is not None:
        args.append(out_prev)
        in_specs.append(pl.BlockSpec(memory_space=pl.ANY))
        aliases = {len(args) - 1: 0}
    return pl.pallas_call(
        _final_kernel,
        grid=(nb, n // tm),
        in_specs=in_specs,
        out_specs=pl.BlockSpec((1, tm, d), lambda b, i: (b + b0, i, 0)),
        out_shape=jax.ShapeDtypeStruct((bsz, n, d), F32),
        input_output_aliases=aliases,
        compiler_params=_params(("parallel", "parallel")),
        name="final",
    )(*args)


def _tile_plan(counts, tmm, n_tiles):
    tiles = (counts + (tmm - 1)) // tmm
    tile_end = jnp.cumsum(tiles)
    offs = (tile_end - tiles) * tmm
    n_valid = tile_end[-1]
    j = jnp.arange(n_tiles, dtype=jnp.int32)
    bucket = jnp.sum((tile_end[None, :] <= jnp.minimum(j, n_valid - 1)[:, None]).astype(jnp.int32), axis=1)
    pair_lo = jnp.array([0, 0, 0, 1, 1, 2], jnp.int32)
    pair_hi = jnp.array([1, 2, 3, 2, 3, 3], jnp.int32)
    group = bucket // PAIRS_PER_GROUP
    pair = bucket % PAIRS_PER_GROUP
    tile_ea = group * EXPERTS_PER_GROUP + pair_lo[pair]
    tile_eb = group * EXPERTS_PER_GROUP + pair_hi[pair]
    return offs, tile_ea, tile_eb, (j < n_valid).astype(jnp.int32), jnp.minimum(j, n_valid - 1)


def kernel(x, c, ctx, c_ctx, w_mod, b_mod, norm1_g, norm2_g, w_in, s5_lambda_re, s5_lambda_im, s5_log_dt,
           s5_b_re, s5_b_im, s5_c_re, s5_c_im, s5_d, w_glu, b_glu, conv_w, w_out, router_group_w,
           router_group_b, router_expert_w, router_expert_b, expert_w1, expert_w3, expert_w2, final_g):
    assert w_mod.shape[0] == 1, "single-layer kernel"
    bsz, n_tok, d = x.shape
    l = 0
    tm = min(TOKEN_TILE, n_tok)

    n_cond = bsz + 1
    pad = (-n_cond) % 8
    cond = jnp.concatenate([c, c_ctx[None, :], jnp.zeros((pad, d), F32)], axis=0)
    m = _mod_rows(cond, w_mod[l], b_mod[l])
    mx = m[:bsz].reshape(bsz, N_MOD, d)
    mc = m[bsz:bsz + 1].reshape(1, N_MOD, d)

    w_in_b = w_in[l].astype(BF16)
    w_s5 = w_in_b[:, :S5_WIDTH]
    o_c = S5_WIDTH + CONV_WIDTH
    o_v = S5_WIDTH + 2 * CONV_WIDTH
    w_conv = jnp.concatenate(
        [w_in_b[:, S5_WIDTH:o_c], w_in_b[:, o_c:o_c + CONV_ROW_WIDTH], w_in_b[:, o_v:o_v + CONV_ROW_WIDTH],
         w_in_b[:, o_c + CONV_ROW_WIDTH:o_v], w_in_b[:, o_v + CONV_ROW_WIDTH:]], axis=1)
    u = _inproj(x, mx, True, norm1_g[l], w_s5, INPROJ_TILE, "inproj")
    uc = _inproj(ctx, mc, False, norm1_g[l], w_s5, INPROJ_TILE, "inproj_ctx")

    t_mat, mb_pair, mc_pair, a_rows = _s5_matrices(
        s5_lambda_re[l], s5_lambda_im[l], s5_log_dt[l], s5_b_re[l], s5_b_im[l], s5_c_re[l], s5_c_im[l], s5_d[l])
    y_c = _s5_scan(_chunkify(u, "chunkify"), _chunkify(uc, "chunkify_ctx"), t_mat, mb_pair, mc_pair, a_rows)
    y_s5 = _unchunkify(y_c)

    n_logits = N_GROUPS + N_EXPERTS
    w_router = jnp.concatenate(
        [router_group_w[l], router_expert_w[l], jnp.zeros((d, ROUTER_ROWS - n_logits), F32)], axis=1).T
    w_router_hi = w_router.astype(BF16)
    w_router_lo = (w_router - w_router_hi.astype(F32)).astype(BF16)
    w_router_t = jnp.concatenate([w_router_hi, w_router_lo], axis=0)
    b_router = jnp.concatenate([router_group_b[l], router_expert_b[l], jnp.zeros((ROUTER_ROWS - n_logits,), F32)])
    tm_mix = min(MIX_TILE, n_tok)
    b_router_t = jnp.broadcast_to(b_router[:, None], (ROUTER_ROWS, min(MIX_SUB, tm_mix)))

    first = bsz * MOE_FIRST_SHARE[0] // MOE_FIRST_SHARE[1]
    part_sizes = [first, bsz - first] if 0 < first < bsz else [bsz]
    n_buckets = N_GROUPS * PAIRS_PER_GROUP
    w_glu_b = w_glu[l].astype(BF16)
    w_out_b = w_out[l].astype(BF16)
    experts_f32 = (expert_w1[l], expert_w3[l], expert_w2[l])
    cast_plan = [experts_f32] if len(part_sizes) == 1 else [experts_f32[:2], experts_f32[2:]]
    w_experts = []
    staged = []
    b0 = 0
    for nb, to_cast in zip(part_sizes, cast_plan):
        n_part = nb * n_tok
        n_rows = n_part + n_buckets * MOE_TILE
        x1, h2p, route_t, counts, *w_cast = _mix(x, y_s5, mx, norm1_g[l], norm2_g[l], conv_w[l], w_conv, w_glu_b,
                                                 b_glu[l], w_out_b, w_router_t, b_router_t, tm_mix, b0, nb, to_cast)
        w_experts += w_cast
        offs, *tiles = _tile_plan(counts[:n_buckets, 0].astype(jnp.int32), MOE_TILE, n_rows // MOE_TILE)
        offs_rows = jnp.zeros((BUCKET_ROWS,), F32).at[:n_buckets].set(offs.astype(F32))
        dest = _rank(route_t, jnp.broadcast_to(offs_rows[:, None], (BUCKET_ROWS, LANES)), min(RANK_TILE, n_part))
        rows = _scatter_rows(h2p.reshape(n_part, d // 2 + ROW_EXTRA), dest, n_rows)
        staged.append((b0, x1, rows, dest, tiles))
        b0 += nb
    out = None
    for b0, x1, rows, dest, tiles in staged:
        y_rows = _moe_grouped(rows, *tiles, *w_experts, MOE_TILE)
        moe_tok = _gather_rows(y_rows, dest).reshape(x1.shape[0], n_tok, d // 2)
        out = _final(x1, moe_tok, mx, final_g, tm, b0, bsz, out)
    return out
```

```python
import functools

import jax
import jax.numpy as jnp
from jax import lax
from jax.experimental import pallas as pl
from jax.experimental.pallas import tpu as pltpu
from jax.experimental.pallas import tpu_sc as plsc

F32 = jnp.float32
BF16 = jnp.bfloat16
U32 = jnp.uint32

RMS_EPS = 1e-6
N_MOD = 6
GRID_W = 64
S5_WIDTH = 256
S5_H = 16
S5_P = 64
S5_GROUPS = S5_WIDTH // S5_H
S5_PAIRS = S5_GROUPS // 2
S5_CHUNK = 16
LANES = 128
STATE_PITCH = 136
CONV_WIDTH = 768
CONV_ROW_WIDTH = CONV_WIDTH // 2
N_GROUPS = 4
EXPERTS_PER_GROUP = 4
N_EXPERTS = N_GROUPS * EXPERTS_PER_GROUP
PAIRS_PER_GROUP = 6
ROUTER_ROWS = 32
BUCKET_ROWS = 32
ROW_EXTRA = 128
TOKEN_TILE = 2048
MOE_TILE = 512
RANK_TILE = 2048
INPROJ_TILE = 2048
MIX_TILE = 1024
MIX_SUB = 512
MOE_FIRST_SHARE = (3, 4)
LAYOUT_GROUPS = 4
SC_ROWS = 64
SC_INDEX_TILE = 128
VMEM_LIMIT = 52 * 1024 * 1024


def _params(sem, vmem=VMEM_LIMIT):
    return pltpu.CompilerParams(dimension_semantics=sem, vmem_limit_bytes=vmem)


def _pack_bf16_pair(a, b):
    ua = lax.bitcast_convert_type(a.astype(BF16).astype(F32), U32)
    ub = lax.bitcast_convert_type(b.astype(BF16).astype(F32), U32)
    return ua | (ub >> 16)


def _unpack_bf16_pair(w):
    a = lax.bitcast_convert_type(w & jnp.uint32(0xFFFF0000), F32)
    b = lax.bitcast_convert_type(w << 16, F32)
    return a, b


def _mod_kernel(c_ref, w_ref, b_ref, o_ref):
    c = c_ref[...]
    o_ref[...] = jnp.dot(c * jax.nn.sigmoid(c), w_ref[...], preferred_element_type=F32) + b_ref[...]


def _mod_rows(cond, w_mod, b_mod):
    n, d = cond.shape
    nout = w_mod.shape[1]
    bn = d
    return pl.pallas_call(
        _mod_kernel,
        grid=(nout // bn,),
        in_specs=[pl.BlockSpec((n, d), lambda j: (0, 0)),
                  pl.BlockSpec((d, bn), lambda j: (0, j)),
                  pl.BlockSpec((1, bn), lambda j: (0, j))],
        out_specs=pl.BlockSpec((n, bn), lambda j: (0, j)),
        out_shape=jax.ShapeDtypeStruct((n, nout), F32),
        compiler_params=_params(("arbitrary",)),
        name="mod",
    )(cond, w_mod, b_mod.reshape(1, nout))


def _modulated_norm(x, g, shift, scale):
    ms = jnp.mean(x * x, axis=-1, keepdims=True)
    return (x * lax.rsqrt(ms + RMS_EPS)) * (g * (1.0 + scale)) + shift


def _inproj_kernel(x_ref, mod_ref, g_ref, w_ref, u_ref):
    nbk, tm, d = x_ref.shape
    h = _modulated_norm(x_ref[...].reshape(nbk * tm, d), g_ref[...], mod_ref[0, 0:1, :], mod_ref[0, 1:2, :])
    u = jnp.dot(h.astype(BF16), w_ref[...], preferred_element_type=F32)
    packed = _pack_bf16_pair(u[:, 0:LANES], u[:, LANES:2 * LANES])
    for b in range(nbk):
        u_ref[b] = packed[b * tm:(b + 1) * tm]


def _inproj(x, mods, per_batch_mod, norm_g, w_s5, tile, name):
    bsz, n, d = x.shape
    tm = min(tile, n)
    nbk = 1 if per_batch_mod else max(1, min(bsz, tile // n))
    assert bsz % nbk == 0
    mod_map = (lambda b, i: (b, 0, 0)) if per_batch_mod else (lambda b, i: (0, 0, 0))
    assert S5_WIDTH == 2 * LANES
    return pl.pallas_call(
        _inproj_kernel,
        grid=(bsz // nbk, n // tm),
        in_specs=[pl.BlockSpec((nbk, tm, d), lambda b, i: (b, i, 0)),
                  pl.BlockSpec((1, N_MOD, d), mod_map),
                  pl.BlockSpec((1, d), lambda b, i: (0, 0)),
                  pl.BlockSpec((d, S5_WIDTH), lambda b, i: (0, 0))],
        out_specs=pl.BlockSpec((nbk, tm, LANES), lambda b, i: (b, i, 0)),
        out_shape=jax.ShapeDtypeStruct((bsz, n, LANES), U32),
        compiler_params=_params(("parallel", "parallel")),
        name=name,
    )(x, mods, norm_g.reshape(1, d), w_s5)


def _toeplitz_kernel(strip_ref, t_ref):
    lc = t_ref.shape[1] // S5_H
    for s in range(lc):
        off = (lc - 1 - s) * S5_H
        t_ref[0, s * S5_H:(s + 1) * S5_H, :] = strip_ref[0, :, off:off + lc * S5_H].astype(BF16)


def _toeplitz(strip):
    g_n, h_n, w = strip.shape
    n = S5_CHUNK * h_n
    return pl.pallas_call(
        _toeplitz_kernel,
        grid=(g_n,),
        in_specs=[pl.BlockSpec((1, h_n, w), lambda g: (g, 0, 0))],
        out_specs=pl.BlockSpec((1, n, n), lambda g: (g, 0, 0)),
        out_shape=jax.ShapeDtypeStruct((g_n, n, n), BF16),
        compiler_params=_params(("parallel",)),
        name="toeplitz",
    )(strip)


def _s5_matrices(lam_re, lam_im, log_dt, b_re, b_im, c_re, c_im, d_skip):
    lc, g_n, p_n, h_n = S5_CHUNK, S5_GROUPS, S5_P, S5_H
    lam = lax.complex(lam_re.astype(F32), lam_im.astype(F32))
    dt = jnp.exp(log_dt.astype(F32))[..., None]
    a_bar = jnp.exp(lam * dt)
    b_bar = ((a_bar - 1.0) / lam)[..., None] * lax.complex(b_re.astype(F32), b_im.astype(F32))
    cm = lax.complex(c_re.astype(F32), c_im.astype(F32))
    steps = jnp.arange(lc + 1, dtype=F32)
    apow = jnp.exp((lam * dt)[:, :, None, :] * steps[None, None, :, None])
    kern = jnp.einsum('dgop,dgjp,dgpi->dgjio', cm, apow[:, :, :lc], b_bar).real
    skip = jnp.eye(h_n, dtype=F32) * d_skip.astype(F32).reshape(g_n, 1, h_n)
    centre = kern[0, :, 0] + kern[1, :, 0] + skip
    lags = jnp.concatenate([kern[1, :, :0:-1], centre[:, None], kern[0, :, 1:]], axis=1)
    strip = lags.transpose(0, 2, 1, 3).reshape(g_n, h_n, (2 * lc - 1) * h_n)
    strip = jnp.pad(strip, ((0, 0), (0, 0), (0, h_n)))
    t_mat = _toeplitz(strip)

    def in_mat(pw, bb):
        return (pw[:, :, None, :] * bb.transpose(0, 2, 1)[:, None, :, :]).reshape(g_n, lc * h_n, p_n)

    mb_f = in_mat(apow[0, :, lc - 1::-1][:, :lc], b_bar[0])
    mb_b = in_mat(apow[1, :, :lc], b_bar[1])

    def out_mat(pw, cc):
        return (pw.transpose(0, 2, 1)[:, :, :, None] * cc.transpose(0, 2, 1)[:, :, None, :]).reshape(
            g_n, p_n, lc * h_n)

    mc_f = out_mat(apow[0, :, 1:lc + 1], cm[0])
    mc_b = out_mat(apow[1, :, lc:0:-1], cm[1])
    a_chunk = apow[:, :, lc]

    q_n = S5_PAIRS
    zeros_in = jnp.zeros((g_n, lc * h_n, p_n), F32)

    def pair_cols(m):
        m = m.reshape(q_n, 2, lc * h_n, p_n)
        z = zeros_in.reshape(q_n, 2, lc * h_n, p_n)[:, 0]
        top = jnp.concatenate([m[:, 0], z], axis=-1)
        bot = jnp.concatenate([z, m[:, 1]], axis=-1)
        return jnp.concatenate([top, bot], axis=1)

    mb_pair = jnp.concatenate([pair_cols(mb_f.real), pair_cols(mb_f.imag),
                               pair_cols(mb_b.real), pair_cols(mb_b.imag)], axis=-1)

    def pair_rows(m):
        m = m.reshape(q_n, 2, p_n, lc * h_n)
        z = jnp.zeros_like(m[:, 0])
        top = jnp.concatenate([m[:, 0], z], axis=-1)
        bot = jnp.concatenate([z, m[:, 1]], axis=-1)
        return jnp.concatenate([top, bot], axis=1)

    mc_pair = jnp.concatenate([pair_rows(mc_f.real), pair_rows(-mc_f.imag),
                               pair_rows(mc_b.real), pair_rows(-mc_b.imag)], axis=1)
    a_rows = jnp.stack([a_chunk[0].real, a_chunk[0].imag, a_chunk[1].real, a_chunk[1].imag], axis=0)
    a_rows = a_rows.reshape(4, q_n, 2 * p_n).transpose(1, 0, 2)
    a_rows = jnp.concatenate([a_rows, jnp.zeros_like(a_rows)], axis=1)
    return t_mat, mb_pair.astype(BF16), mc_pair.astype(BF16), a_rows


def _chunkify_kernel(u_ref, o_ref, *, gb):
    _, nb, nc, _ = o_ref.shape
    half = S5_CHUNK * S5_H
    per_slab = LANES // S5_H
    for b0 in range(0, nb, gb):
        slabs = ([], [])
        for t in range(S5_CHUNK):
            rows = [u_ref[b0 + b, pl.ds(t, nc, stride=S5_CHUNK), :] for b in range(gb)]
            for j, part in enumerate(_unpack_bf16_pair(rows[0] if gb == 1 else jnp.concatenate(rows, axis=0))):
                slabs[j].append(part.T)
        for j, cols in enumerate(slabs):
            for gl in range(per_slab):
                g = j * per_slab + gl
                m = jnp.concatenate([c[gl * S5_H:(gl + 1) * S5_H, :] for c in cols], axis=0)
                o_ref[g // 2, b0:b0 + gb, :, (g % 2) * half:(g % 2 + 1) * half] = (
                    m.T.astype(BF16).reshape(gb, nc, half))


def _layout_step(bsz, gb):
    return gb * LAYOUT_GROUPS if bsz % (gb * LAYOUT_GROUPS) == 0 else gb


def _chunkify(u_rows, name):
    bsz, n, _ = u_rows.shape
    nc = n // S5_CHUNK
    gb = min(bsz, max(1, LANES // nc))
    nb = _layout_step(bsz, gb)
    w = 2 * S5_CHUNK * S5_H
    return pl.pallas_call(
        functools.partial(_chunkify_kernel, gb=gb),
        grid=(bsz // nb,),
        in_specs=[pl.BlockSpec((nb, n, LANES), lambda b: (b, 0, 0))],
        out_specs=pl.BlockSpec((S5_PAIRS, nb, nc, w), lambda b: (0, b, 0, 0)),
        out_shape=jax.ShapeDtypeStruct((S5_PAIRS, bsz, nc, w), BF16),
        compiler_params=_params(("parallel",)),
        name=name,
    )(u_rows)


def _unchunkify_kernel(y_ref, o_ref):
    _, nb, nc, _ = y_ref.shape
    half = S5_CHUNK * S5_H
    per_slab = LANES // S5_H
    for b in range(nb):
        rows = []
        for g in range(S5_GROUPS):
            rows.append(y_ref[g // 2, b, :, (g % 2) * half:(g % 2 + 1) * half].astype(F32).T)
        for t in range(S5_CHUNK):
            tiles = [jnp.concatenate([r[t * S5_H:(t + 1) * S5_H, :] for r in rows[j * per_slab:(j + 1) * per_slab]],
                                     axis=0).T for j in range(2)]
            o_ref[b, pl.ds(t, nc, stride=S5_CHUNK), :] = _pack_bf16_pair(*tiles)


def _unchunkify(y_c):
    q_n, bsz, nc, w = y_c.shape
    n = nc * S5_CHUNK
    nb = _layout_step(bsz, 1)
    return pl.pallas_call(
        _unchunkify_kernel,
        grid=(bsz // nb,),
        in_specs=[pl.BlockSpec((q_n, nb, nc, w), lambda b: (0, b, 0, 0))],
        out_specs=pl.BlockSpec((nb, n, LANES), lambda b: (b, 0, 0)),
        out_shape=jax.ShapeDtypeStruct((bsz, n, LANES), U32),
        compiler_params=_params(("parallel",)),
        name="unchunkify",
    )(y_c)


def _s5_kernel(u_ref, uc_ref, t_ref, mb_ref, mc_ref, a_ref, y_ref, s_lat, s_ctx, h_scr, *, bb):
    _, bsz, n_lat, w = u_ref.shape
    n_ctx = uc_ref.shape[2]
    n_blk = w // LANES
    rb = bb * n_lat

    mb = mb_ref[0]

    def in_lat(i, carry):
        s = jnp.dot(u_ref[0, pl.ds(i * bb, bb)].reshape(rb, w), mb, preferred_element_type=F32)
        for k in range(bb):
            r = pl.multiple_of((i * bb + k) * STATE_PITCH, 8)
            for blk in range(n_blk):
                s_lat[blk, pl.ds(r, n_lat), :] = s[k * n_lat:(k + 1) * n_lat, blk * LANES:(blk + 1) * LANES]
        return carry

    lax.fori_loop(0, bsz // bb, in_lat, 0)
    sc = jnp.dot(uc_ref[0].reshape(bsz * n_ctx, w), mb, preferred_element_type=F32)
    for blk in range(n_blk):
        s_ctx[blk] = sc[:, blk * LANES:(blk + 1) * LANES]

    a_fr, a_fi, a_br, a_bi = (a_ref[0, k:k + 1, :] for k in range(4))

    def step(h, a_r, a_i, s_r, s_i):
        h_r, h_i = h
        return a_r * h_r - a_i * h_i + s_r, a_r * h_i + a_i * h_r + s_i

    def ctx_rows(blk, c):
        return s_ctx[blk, pl.ds(c, bsz, stride=n_ctx), :]

    def lat_rows(ref, blk, c):
        return ref.at[blk, pl.ds(c, bsz, stride=STATE_PITCH), :]

    def ctx_step(k, carry):
        hf, hb = carry
        kb = n_ctx - 1 - k
        hf = step(hf, a_fr, a_fi, ctx_rows(0, k), ctx_rows(1, k))
        hb = step(hb, a_br, a_bi, ctx_rows(2, kb), ctx_rows(3, kb))
        return hf, hb

    zero = jnp.zeros((bsz, LANES), F32)
    carry = lax.fori_loop(0, n_ctx, ctx_step, ((zero, zero), (zero, zero)))

    def lat_step(k, carry):
        hf, hb = carry
        kb = n_lat - 1 - k
        lat_rows(h_scr, 0, k)[...] = hf[0]
        lat_rows(h_scr, 1, k)[...] = hf[1]
        lat_rows(h_scr, 2, kb)[...] = hb[0]
        lat_rows(h_scr, 3, kb)[...] = hb[1]
        hf = step(hf, a_fr, a_fi, lat_rows(s_lat, 0, k)[...], lat_rows(s_lat, 1, k)[...])
        hb = step(hb, a_br, a_bi, lat_rows(s_lat, 2, kb)[...], lat_rows(s_lat, 3, kb)[...])
        return hf, hb

    lax.fori_loop(0, n_lat, lat_step, carry)

    t0 = t_ref[0]
    t1 = t_ref[1]
    mc = mc_ref[0]
    half = S5_CHUNK * S5_H

    def out_lat(i, carry):
        u = u_ref[0, pl.ds(i * bb, bb)].reshape(rb, w)
        h_rows = []
        for k in range(bb):
            r = pl.multiple_of((i * bb + k) * STATE_PITCH, 8)
            h_rows.append(jnp.concatenate([h_scr[blk, pl.ds(r, n_lat), :] for blk in range(n_blk)], axis=1))
        h = jnp.concatenate(h_rows, axis=0).astype(BF16)
        inter = jnp.dot(h, mc, preferred_element_type=F32)
        y0 = jnp.dot(u[:, :half], t0, preferred_element_type=F32) + inter[:, :half]
        y1 = jnp.dot(u[:, half:], t1, preferred_element_type=F32) + inter[:, half:]
        y = jnp.concatenate([y0, y1], axis=1).astype(BF16)
        y_ref[0, pl.ds(i * bb, bb)] = y.reshape(bb, n_lat, w)
        return carry

    lax.fori_loop(0, bsz // bb, out_lat, 0)


def _s5_scan(u_c, uc_c, t_mat, mb_pair, mc_pair, a_rows):
    q_n, bsz, n_lat, w = u_c.shape
    n_ctx = uc_c.shape[2]
    assert n_lat + 8 == STATE_PITCH
    bb = min(8, bsz)
    n_blk = w // LANES
    return pl.pallas_call(
        functools.partial(_s5_kernel, bb=bb),
        grid=(q_n,),
        in_specs=[pl.BlockSpec((1, bsz, n_lat, w), lambda q: (q, 0, 0, 0)),
                  pl.BlockSpec((1, bsz, n_ctx, w), lambda q: (q, 0, 0, 0)),
                  pl.BlockSpec((2, w // 2, w // 2), lambda q: (q, 0, 0)),
                  pl.BlockSpec((1, w, w), lambda q: (q, 0, 0)),
                  pl.BlockSpec((1, w, w), lambda q: (q, 0, 0)),
                  pl.BlockSpec((1, 8, LANES), lambda q: (q, 0, 0))],
        out_specs=pl.BlockSpec((1, bsz, n_lat, w), lambda q: (q, 0, 0, 0)),
        out_shape=jax.ShapeDtypeStruct((q_n, bsz, n_lat, w), BF16),
        scratch_shapes=[pltpu.VMEM((n_blk, bsz * STATE_PITCH, LANES), F32),
                        pltpu.VMEM((n_blk, bsz * n_ctx, LANES), F32),
                        pltpu.VMEM((n_blk, bsz * STATE_PITCH, LANES), F32)],
        compiler_params=_params(("parallel",)),
        name="s5_scan",
    )(u_c, uc_c, t_mat, mb_pair, mc_pair, a_rows)


def _first_max(rows):
    best = rows[0]
    for r in rows[1:]:
        best = jnp.maximum(best, r)
    idx = jnp.full(best.shape, float(len(rows) - 1), F32)
    for k in range(len(rows) - 2, -1, -1):
        idx = jnp.where(rows[k] == best, float(k), idx)
    return best, idx


def _route_rows(lg):
    g_rows = [lg[k:k + 1] for k in range(N_GROUPS)]
    g_max, g_idx = _first_max(g_rows)
    g_sum = sum(jnp.exp(r - g_max) for r in g_rows)
    g_p = 1.0 / g_sum
    e_rows = []
    for j in range(EXPERTS_PER_GROUP):
        r = lg[N_GROUPS + (N_GROUPS - 1) * EXPERTS_PER_GROUP + j:][:1]
        for g in range(N_GROUPS - 2, -1, -1):
            k = N_GROUPS + g * EXPERTS_PER_GROUP + j
            r = jnp.where(g_idx == float(g), lg[k:k + 1], r)
        e_rows.append(r)
    v1, i1 = _first_max(e_rows)
    rest = [jnp.where(i1 == float(j), -jnp.inf, e_rows[j]) for j in range(EXPERTS_PER_GROUP)]
    v2, i2 = _first_max(rest)
    e21 = jnp.exp(v2 - v1)
    w1 = g_p / (1.0 + e21)
    w2 = w1 * e21
    lo = jnp.minimum(i1, i2)
    hi = jnp.maximum(i1, i2)
    base = jnp.where(lo == 0.0, 0.0, jnp.where(lo == 1.0, 3.0, 5.0))
    bucket = g_idx * float(PAIRS_PER_GROUP) + base + hi - lo - 1.0
    first_is_lo = i1 < i2
    return bucket, jnp.where(first_is_lo, w1, w2), jnp.where(first_is_lo, w2, w1)


def _mix_kernel(x_ref, xup_ref, xdn_ref, y_ref, mod_ref, g1_ref, g2_ref, cw_ref, win_ref, wglu_ref, bglu_ref,
                wout_ref, wr_ref, br_ref, *rest):
    n_cast = (len(rest) - 4) // 2
    x1_ref, h2p_ref, route_ref, counts_ref = rest[n_cast:n_cast + 4]
    for src, dst in zip(rest[:n_cast], rest[n_cast + 4:]):
        dst[...] = src[...].astype(BF16)
    i = pl.program_id(1)
    tm = x_ref.shape[1]
    d = x_ref.shape[2]
    sub = min(MIX_SUB, tm)
    n_sub = tm // sub
    cw, rw = CONV_WIDTH, CONV_ROW_WIDTH

    def hidden(xv):
        return _modulated_norm(xv, g1_ref[...], mod_ref[0, 0:1, :], mod_ref[0, 1:2, :]).astype(BF16)

    def halo(h):
        zh = jnp.dot(h, win_ref[:, 2 * cw:3 * cw], preferred_element_type=F32)
        return zh[:, 0:rw] * zh[:, rw:cw]

    @pl.when(jnp.logical_and(pl.program_id(0) == 0, i == 0))
    def _():
        counts_ref[...] = jnp.zeros_like(counts_ref)

    def sub_tile(s):
        r0 = s * sub
        xv = x_ref[0, r0:r0 + sub, :]
        hx = hidden(xv)
        g = jax.nn.gelu(jnp.concatenate(_unpack_bf16_pair(y_ref[0, r0:r0 + sub, :]), axis=1))
        hx_up = hidden(xup_ref[0]) if s == 0 else None
        hx_dn = hidden(xdn_ref[0]) if s == n_sub - 1 else None
        yield
        z_r = jnp.dot(hx, win_ref[:, cw:2 * cw], preferred_element_type=F32)
        cr = z_r[:, 0:rw] * z_r[:, rw:cw]
        z_c = jnp.dot(hx, win_ref[:, 2 * cw:3 * cw], preferred_element_type=F32)
        cc = z_c[:, 0:rw] * z_c[:, rw:cw]
        col_products[s] = cc
        yield
        bg = jnp.dot(hx, win_ref[:, 0:cw], preferred_element_type=F32)
        if s == 0:
            up_halo = jnp.where(i == 0, 0.0, halo(hx_up))
        if s == n_sub - 1:
            dn_halo = jnp.where(i == pl.num_programs(1) - 1, 0.0, halo(hx_dn))
        glu = g * jax.nn.sigmoid(jnp.dot(g.astype(BF16), wglu_ref[...], preferred_element_type=F32) + bglu_ref[...])
        yield
        row = lax.broadcasted_iota(jnp.int32, (sub, 1), 0)
        col_in_row = row % GRID_W
        left = jnp.where(col_in_row == 0, 0.0, pltpu.roll(cr, 1, axis=0))
        right = jnp.where(col_in_row == GRID_W - 1, 0.0, pltpu.roll(cr, sub - 1, axis=0))
        w_r = cw_ref[:, :rw]
        row_part = left * w_r[0:1] + cr * w_r[1:2] + right * w_r[2:3]
        if s > 0:
            up_halo = col_products[s - 1][sub - GRID_W:]
        if s < n_sub - 1:
            dn_halo = col_products[s + 1][:GRID_W]
        up = jnp.concatenate([up_halo, cc[:sub - GRID_W]], axis=0)
        dn = jnp.concatenate([cc[GRID_W:], dn_halo], axis=0)
        w_c = cw_ref[:, rw:]
        col_part = up * w_c[0:1] + cc * w_c[1:2] + dn * w_c[2:3]
        y_row = (bg[:, 0:rw] * row_part).astype(BF16)
        y_col = (bg[:, rw:cw] * col_part).astype(BF16)
        mixed = jnp.concatenate([glu.astype(BF16), y_row, y_col], axis=1)
        yield
        yx = jnp.dot(mixed, wout_ref[...], preferred_element_type=F32)
        yield
        x1 = xv + mod_ref[0, 2:3, :] * yx
        x1_ref[0, r0:r0 + sub, :] = x1
        h2 = _modulated_norm(x1, g2_ref[...], mod_ref[0, 3:4, :], mod_ref[0, 4:5, :])
        h2b = h2.astype(BF16)
        lg2 = lax.dot_general(wr_ref[...], h2b, (((1,), (1,)), ((), ())), preferred_element_type=F32)
        lg = lg2[:ROUTER_ROWS] + lg2[ROUTER_ROWS:] + br_ref[...]
        bucket, w_a, w_b = _route_rows(lg)
        r8 = lax.broadcasted_iota(jnp.int32, (8, sub), 0)
        route_ref[:, r0:r0 + sub] = jnp.where(r8 == 0, bucket, jnp.where(r8 == 1, w_a, jnp.where(r8 == 2, w_b, 0.0)))
        rl = lax.broadcasted_iota(jnp.int32, (ROW_EXTRA, sub), 0)
        gates_t = jnp.where(rl == 0, w_a, jnp.where(rl == 1, w_b, 0.0))
        h2p_ref[0, r0:r0 + sub, 0:d // 2] = _pack_bf16_pair(h2[:, :d // 2], h2[:, d // 2:])
        h2p_ref[0, r0:r0 + sub, d // 2:] = lax.bitcast_convert_type(gates_t.T, U32)
        rb = lax.broadcasted_iota(jnp.int32, (BUCKET_ROWS, sub), 0).astype(F32)
        counts_ref[...] += jnp.sum(jnp.where(rb == bucket, 1.0, 0.0), axis=-1, keepdims=True)
        yield

    n_stage = 6
    col_products = {}
    tiles = [sub_tile(s) for s in range(n_sub)]
    for step in range(n_sub + n_stage - 1):
        for s in reversed(range(n_sub)):
            if 0 <= step - s < n_stage:
                next(tiles[s])


def _mix(x, y_s5, mods, norm1_g, norm2_g, conv_w, w_conv, w_glu, b_glu, w_out, w_router_t, b_router_t, tm, b0, nb,
         to_cast):
    _, n, d = x.shape
    nt = n // tm
    n_steps = nb * nt
    cast_specs = []
    for w in to_cast:
        per = -(-w.shape[0] // n_steps)
        while w.shape[0] % per:
            per += 1
        n_blk = w.shape[0] // per
        cast_specs.append(pl.BlockSpec((per,) + w.shape[1:],
                                       lambda b, i, n_blk=n_blk: ((b * nt + i) * n_blk // n_steps, 0, 0)))
    halo_blocks = n // GRID_W
    per_tile = tm // GRID_W
    tok_out = lambda w: pl.BlockSpec((1, tm, w), lambda b, i: (b, i, 0))
    full = lambda a: pl.BlockSpec(a.shape, lambda b, i: (0,) * a.ndim)
    args = (x, x, x, y_s5, mods, norm1_g.reshape(1, d), norm2_g.reshape(1, d), conv_w, w_conv, w_glu,
            b_glu.reshape(1, -1), w_out, w_router_t, b_router_t)
    in_specs = [pl.BlockSpec((1, tm, d), lambda b, i: (b + b0, i, 0)),
                pl.BlockSpec((1, GRID_W, d), lambda b, i: (b + b0, jnp.maximum(i * per_tile - 1, 0), 0)),
                pl.BlockSpec((1, GRID_W, d),
                             lambda b, i: (b + b0, jnp.minimum((i + 1) * per_tile, halo_blocks - 1), 0)),
                pl.BlockSpec((1, tm, LANES), lambda b, i: (b + b0, i, 0)),
                pl.BlockSpec((1, N_MOD, d), lambda b, i: (b + b0, 0, 0))] + [full(a) for a in args[5:]]
    return pl.pallas_call(
        _mix_kernel,
        grid=(nb, nt),
        in_specs=in_specs + cast_specs,
        out_specs=[tok_out(d), tok_out(d // 2 + ROW_EXTRA),
                   pl.BlockSpec((8, tm), lambda b, i: (0, b * nt + i)),
                   pl.BlockSpec((BUCKET_ROWS, LANES), lambda b, i: (0, 0))] + cast_specs,
        out_shape=[jax.ShapeDtypeStruct((nb, n, d), F32),
                   jax.ShapeDtypeStruct((nb, n, d // 2 + ROW_EXTRA), U32),
                   jax.ShapeDtypeStruct((8, nb * n), F32),
                   jax.ShapeDtypeStruct((BUCKET_ROWS, LANES), F32)]
        + [jax.ShapeDtypeStruct(w.shape, BF16) for w in to_cast],
        compiler_params=_params(("arbitrary", "arbitrary")),
        name="mix",
    )(*args, *to_cast)


def _rank_kernel(route_ref, offs_ref, dest_ref, run_ref):
    tr = route_ref.shape[1]
    n_blk = tr // LANES

    @pl.when(pl.program_id(0) == 0)
    def _():
        run_ref[...] = jnp.zeros_like(run_ref)

    bucket = route_ref[0:1, :]
    rb = lax.broadcasted_iota(jnp.int32, (BUCKET_ROWS, tr), 0).astype(F32)
    onehot = jnp.where(rb == bucket, 1.0, 0.0)
    blocks = [onehot[:, k * LANES:(k + 1) * LANES] for k in range(n_blk)]
    s_idx = lax.broadcasted_iota(jnp.int32, (LANES, LANES), 0)
    t_idx = lax.broadcasted_iota(jnp.int32, (LANES, LANES), 1)
    tri = jnp.where(s_idx <= t_idx, 1.0, 0.0).astype(BF16)
    prefix = jnp.dot(jnp.concatenate(blocks, axis=0).astype(BF16), tri, preferred_element_type=F32)
    start = run_ref[:, 0:1] + offs_ref[:, 0:1]
    carry = start
    for k in range(n_blk):
        pk = prefix[k * BUCKET_ROWS:(k + 1) * BUCKET_ROWS]
        dest = jnp.sum(blocks[k] * (pk - 1.0 + carry), axis=0, keepdims=True)
        dest_ref[:, k * LANES:(k + 1) * LANES] = dest.astype(jnp.int32)
        carry = carry + pk[:, LANES - 1:LANES]
    run_ref[...] += carry - start


def _rank(route_t, offs_rows, tr):
    n = route_t.shape[1]
    return pl.pallas_call(
        _rank_kernel,
        grid=(n // tr,),
        in_specs=[pl.BlockSpec((8, tr), lambda i: (0, i)),
                  pl.BlockSpec((BUCKET_ROWS, LANES), lambda i: (0, 0))],
        out_specs=pl.BlockSpec((1, tr), lambda i: (0, i)),
        out_shape=jax.ShapeDtypeStruct((1, n), jnp.int32),
        scratch_shapes=[pltpu.VMEM((BUCKET_ROWS, LANES), F32)],
        compiler_params=_params(("arbitrary",)),
        name="rank",
    )(route_t, offs_rows)


def _sc_mesh():
    return plsc.VectorSubcoreMesh(core_axis_name="core", subcore_axis_name="subcore")


def _scatter_rows(src, dest, n_out):
    n, w = src.shape
    sub = SC_INDEX_TILE // SC_ROWS

    @functools.partial(pl.kernel, out_type=jax.ShapeDtypeStruct((n_out, w), src.dtype), mesh=_sc_mesh(),
                       scratch_types=[], name="scatter_rows")
    def scatter(x_hbm, i_hbm, o_hbm):
        def body(x_vmem, i_vmem):
            j = pl.program_id(1)
            pltpu.sync_copy(x_vmem, o_hbm.at[i_vmem.at[0, pl.ds(j * SC_ROWS, SC_ROWS)]])

        pltpu.emit_pipeline(
            body,
            grid=(n // SC_INDEX_TILE, sub),
            in_specs=[pl.BlockSpec((SC_ROWS, w), lambda i, j: (i * sub + j, 0)),
                      pl.BlockSpec((1, SC_INDEX_TILE), lambda i, j: (0, i))],
            out_specs=[],
            core_axis_name=("core", "subcore"),
            dimension_semantics=(pltpu.PARALLEL, pltpu.ARBITRARY),
        )(x_hbm, i_hbm)

    return scatter(src, dest)


def _gather_rows(src, idx):
    n = idx.shape[1]
    w = src.shape[1]
    sub = SC_INDEX_TILE // SC_ROWS

    @functools.partial(pl.kernel, out_type=jax.ShapeDtypeStruct((n, w), src.dtype), mesh=_sc_mesh(),
                       scratch_types=[], name="gather_rows")
    def gather(x_hbm, i_hbm, o_hbm):
        def body(i_vmem, o_vmem):
            j = pl.program_id(1)
            pltpu.sync_copy(x_hbm.at[i_vmem.at[0, pl.ds(j * SC_ROWS, SC_ROWS)]], o_vmem)

        pltpu.emit_pipeline(
            body,
            grid=(n // SC_INDEX_TILE, sub),
            in_specs=[pl.BlockSpec((1, SC_INDEX_TILE), lambda i, j: (0, i))],
            out_specs=[pl.BlockSpec((SC_ROWS, w), lambda i, j: (i * sub + j, 0))],
            core_axis_name=("core", "subcore"),
            dimension_semantics=(pltpu.PARALLEL, pltpu.ARBITRARY),
        )(i_hbm, o_hbm)

    return gather(src, idx)


def _moe_kernel(ea_ref, eb_ref, valid_ref, blk_ref, rows_ref, w1a_ref, w3a_ref, w2a_ref, w1b_ref, w3b_ref, w2b_ref,
                o_ref):
    j = pl.program_id(0)
    half = rows_ref.shape[1] - ROW_EXTRA

    @pl.when(valid_ref[j] != 0)
    def _():
        ha, hb = _unpack_bf16_pair(rows_ref[:, 0:half])
        ha = ha.astype(BF16)
        hb = hb.astype(BF16)
        gates = lax.bitcast_convert_type(rows_ref[:, half:], F32)

        def expert(w1_ref, w3_ref, w2_ref, gate):
            def up(w_ref):
                return (jnp.dot(ha, w_ref[0, 0:half, :], preferred_element_type=F32)
                        + jnp.dot(hb, w_ref[0, half:, :], preferred_element_type=F32))

            a1 = up(w1_ref)
            he = (a1 * jax.nn.sigmoid(a1)) * up(w3_ref) * gate
            return jnp.dot(he.astype(BF16), w2_ref[0], preferred_element_type=F32)

        y = (expert(w1a_ref, w3a_ref, w2a_ref, gates[:, 0:1]) + expert(w1b_ref, w3b_ref, w2b_ref, gates[:, 1:2]))
        o_ref[...] = _pack_bf16_pair(y[:, :half], y[:, half:])


def _moe_grouped(rows, tile_ea, tile_eb, tile_valid, tile_blk, w1, w3, w2, tmm):
    r, w = rows.shape
    _, de, d = w2.shape
    up_a = pl.BlockSpec((1, d, de), lambda j, ea, eb, va, bk: (ea[j], 0, 0))
    up_b = pl.BlockSpec((1, d, de), lambda j, ea, eb, va, bk: (eb[j], 0, 0))
    grid_spec = pltpu.PrefetchScalarGridSpec(
        num_scalar_prefetch=4,
        grid=(r // tmm,),
        in_specs=[pl.BlockSpec((tmm, w), lambda j, ea, eb, va, bk: (bk[j], 0)),
                  up_a, up_a, pl.BlockSpec((1, de, d), lambda j, ea, eb, va, bk: (ea[j], 0, 0)),
                  up_b, up_b, pl.BlockSpec((1, de, d), lambda j, ea, eb, va, bk: (eb[j], 0, 0))],
        out_specs=pl.BlockSpec((tmm, d // 2), lambda j, ea, eb, va, bk: (bk[j], 0)),
    )
    return pl.pallas_call(
        _moe_kernel,
        grid_spec=grid_spec,
        out_shape=jax.ShapeDtypeStruct((r, d // 2), U32),
        compiler_params=_params(("arbitrary",)),
        name="moe",
    )(tile_ea, tile_eb, tile_valid, tile_blk, rows, w1, w3, w2, w1, w3, w2)


def _final_kernel(x1_ref, moe_ref, mod_ref, fg_ref, *rest):
    o_ref = rest[-1]
    ya, yb = _unpack_bf16_pair(moe_ref[0])
    half = ya.shape[1]
    gate = mod_ref[0, 5:6, :]
    xa = x1_ref[0, :, 0:half] + gate[:, 0:half] * ya
    xb = x1_ref[0, :, half:] + gate[:, half:] * yb
    ms = (jnp.sum(xa * xa, axis=-1, keepdims=True) + jnp.sum(xb * xb, axis=-1, keepdims=True)) / (2 * half)
    inv = lax.rsqrt(ms + RMS_EPS)
    o_ref[0, :, 0:half] = xa * inv * fg_ref[:, 0:half]
    o_ref[0, :, half:] = xb * inv * fg_ref[:, half:]


def _final(x1, moe_tok, mods, final_g, tm, b0, bsz, out_prev):
    nb, n, d = x1.shape
    tok = lambda w: pl.BlockSpec((1, tm, w), lambda b, i: (b, i, 0))
    args = [x1, moe_tok, mods, final_g.reshape(1, d)]
    in_specs = [tok(d), tok(d // 2), pl.BlockSpec((1, N_MOD, d), lambda b, i: (b + b0, 0, 0)),
                pl.BlockSpec((1, d), lambda b, i: (0, 0))]
    aliases = {}
    if out_prev is not None:
        args.append(out_prev)
        in_specs.append(pl.BlockSpec(memory_space=pl.ANY))
        aliases = {len(args) - 1: 0}
    return pl.pallas_call(
        _final_kernel,
        grid=(nb, n // tm),
        in_specs=in_specs,
        out_specs=pl.BlockSpec((1, tm, d), lambda b, i: (b + b0, i, 0)),
        out_shape=jax.ShapeDtypeStruct((bsz, n, d), F32),
        input_output_aliases=aliases,
        compiler_params=_params(("parallel", "parallel")),
        name="final",
    )(*args)


def _tile_plan(counts, tmm, n_tiles):
    tiles = (counts + (tmm - 1)) // tmm
    tile_end = jnp.cumsum(tiles)
    offs = (tile_end - tiles) * tmm
    n_valid = tile_end[-1]
    j = jnp.arange(n_tiles, dtype=jnp.int32)
    bucket = jnp.sum((tile_end[None, :] <= jnp.minimum(j, n_valid - 1)[:, None]).astype(jnp.int32), axis=1)
    pair_lo = jnp.array([0, 0, 0, 1, 1, 2], jnp.int32)
    pair_hi = jnp.array([1, 2, 3, 2, 3, 3], jnp.int32)
    group = bucket // PAIRS_PER_GROUP
    pair = bucket % PAIRS_PER_GROUP
    tile_ea = group * EXPERTS_PER_GROUP + pair_lo[pair]
    tile_eb = group * EXPERTS_PER_GROUP + pair_hi[pair]
    return offs, tile_ea, tile_eb, (j < n_valid).astype(jnp.int32), jnp.minimum(j, n_valid - 1)


def kernel(x, c, ctx, c_ctx, w_mod, b_mod, norm1_g, norm2_g, w_in, s5_lambda_re, s5_lambda_im, s5_log_dt,
           s5_b_re, s5_b_im, s5_c_re, s5_c_im, s5_d, w_glu, b_glu, conv_w, w_out, router_group_w,
           router_group_b, router_expert_w, router_expert_b, expert_w1, expert_w3, expert_w2, final_g):
    assert w_mod.shape[0] == 1, "single-layer kernel"
    bsz, n_tok, d = x.shape
    l = 0
    tm = min(TOKEN_TILE, n_tok)

    n_cond = bsz + 1
    pad = (-n_cond) % 8
    cond = jnp.concatenate([c, c_ctx[None, :], jnp.zeros((pad, d), F32)], axis=0)
    m = _mod_rows(cond, w_mod[l], b_mod[l])
    mx = m[:bsz].reshape(bsz, N_MOD, d)
    mc = m[bsz:bsz + 1].reshape(1, N_MOD, d)

    w_in_b = w_in[l].astype(BF16)
    w_s5 = w_in_b[:, :S5_WIDTH]
    o_c = S5_WIDTH + CONV_WIDTH
    o_v = S5_WIDTH + 2 * CONV_WIDTH
    w_conv = jnp.concatenate(
        [w_in_b[:, S5_WIDTH:o_c], w_in_b[:, o_c:o_c + CONV_ROW_WIDTH], w_in_b[:, o_v:o_v + CONV_ROW_WIDTH],
         w_in_b[:, o_c + CONV_ROW_WIDTH:o_v], w_in_b[:, o_v + CONV_ROW_WIDTH:]], axis=1)
    u = _inproj(x, mx, True, norm1_g[l], w_s5, INPROJ_TILE, "inproj")
    uc = _inproj(ctx, mc, False, norm1_g[l], w_s5, INPROJ_TILE, "inproj_ctx")

    t_mat, mb_pair, mc_pair, a_rows = _s5_matrices(
        s5_lambda_re[l], s5_lambda_im[l], s5_log_dt[l], s5_b_re[l], s5_b_im[l], s5_c_re[l], s5_c_im[l], s5_d[l])
    y_c = _s5_scan(_chunkify(u, "chunkify"), _chunkify(uc, "chunkify_ctx"), t_mat, mb_pair, mc_pair, a_rows)
    y_s5 = _unchunkify(y_c)

    n_logits = N_GROUPS + N_EXPERTS
    w_router = jnp.concatenate(
        [router_group_w[l], router_expert_w[l], jnp.zeros((d, ROUTER_ROWS - n_logits), F32)], axis=1).T
    w_router_hi = w_router.astype(BF16)
    w_router_lo = (w_router - w_router_hi.astype(F32)).astype(BF16)
    w_router_t = jnp.concatenate([w_router_hi, w_router_lo], axis=0)
    b_router = jnp.concatenate([router_group_b[l], router_expert_b[l], jnp.zeros((ROUTER_ROWS - n_logits,), F32)])
    tm_mix = min(MIX_TILE, n_tok)
    b_router_t = jnp.broadcast_to(b_router[:, None], (ROUTER_ROWS, min(MIX_SUB, tm_mix)))

    first = bsz * MOE_FIRST_SHARE[0] // MOE_FIRST_SHARE[1]
    part_sizes = [first, bsz - first] if 0 < first < bsz else [bsz]
    n_buckets = N_GROUPS * PAIRS_PER_GROUP
    w_glu_b = w_glu[l].astype(BF16)
    w_out_b = w_out[l].astype(BF16)
    experts_f32 = (expert_w1[l], expert_w3[l], expert_w2[l])
    cast_plan = [experts_f32] if len(part_sizes) == 1 else [experts_f32[:2], experts_f32[2:]]
    w_experts = []
    staged = []
    b0 = 0
    for nb, to_cast in zip(part_sizes, cast_plan):
        n_part = nb * n_tok
        n_rows = n_part + n_buckets * MOE_TILE
        x1, h2p, route_t, counts, *w_cast = _mix(x, y_s5, mx, norm1_g[l], norm2_g[l], conv_w[l], w_conv, w_glu_b,
                                                 b_glu[l], w_out_b, w_router_t, b_router_t, tm_mix, b0, nb, to_cast)
        w_experts += w_cast
        offs, *tiles = _tile_plan(counts[:n_buckets, 0].astype(jnp.int32), MOE_TILE, n_rows // MOE_TILE)
        offs_rows = jnp.zeros((BUCKET_ROWS,), F32).at[:n_buckets].set(offs.astype(F32))
        dest = _rank(route_t, jnp.broadcast_to(offs_rows[:, None], (BUCKET_ROWS, LANES)), min(RANK_TILE, n_part))
        rows = _scatter_rows(h2p.reshape(n_part, d // 2 + ROW_EXTRA), dest, n_rows)
        staged.append((b0, x1, rows, dest, tiles))
        b0 += nb
    out = None
    for b0, x1, rows, dest, tiles in staged:
        y_rows = _moe_grouped(rows, *tiles, *w_experts, MOE_TILE)
        moe_tok = _gather_rows(y_rows, dest).reshape(x1.shape[0], n_tok, d // 2)
        out = _final(x1, moe_tok, mx, final_g, tm, b0, bsz, out)
    return out
```

```python
import functools

import jax
import jax.numpy as jnp
from jax import lax
from jax.experimental import pallas as pl
from jax.experimental.pallas import tpu as pltpu
from jax.experimental.pallas import tpu_sc as plsc

F32 = jnp.float32
BF16 = jnp.bfloat16
U32 = jnp.uint32

RMS_EPS = 1e-6
N_MOD = 6
GRID_W = 64
S5_WIDTH = 256
S5_H = 16
S5_P = 64
S5_GROUPS = S5_WIDTH // S5_H
S5_PAIRS = S5_GROUPS // 2
S5_CHUNK = 16
LANES = 128
STATE_PITCH = 136
CONV_WIDTH = 768
CONV_ROW_WIDTH = CONV_WIDTH // 2
N_GROUPS = 4
EXPERTS_PER_GROUP = 4
N_EXPERTS = N_GROUPS * EXPERTS_PER_GROUP
PAIRS_PER_GROUP = 6
ROUTER_ROWS = 32
BUCKET_ROWS = 32
ROW_EXTRA = 128
TOKEN_TILE = 2048
MOE_TILE = 512
RANK_TILE = 2048
INPROJ_TILE = 2048
MIX_TILE = 1024
MIX_SUB = 512
TAIL_ROWS = 128
MOE_FIRST_SHARE = (5, 8)
LAYOUT_GROUPS = 4
SC_ROWS = 64
SC_INDEX_TILE = 128
VMEM_LIMIT = 52 * 1024 * 1024


def _params(sem, vmem=VMEM_LIMIT):
    return pltpu.CompilerParams(dimension_semantics=sem, vmem_limit_bytes=vmem)


def _pack_bf16_pair(a, b):
    ua = lax.bitcast_convert_type(a.astype(BF16).astype(F32), U32)
    ub = lax.bitcast_convert_type(b.astype(BF16).astype(F32), U32)
    return ua | (ub >> 16)


def _unpack_bf16_pair(w):
    a = lax.bitcast_convert_type(w & jnp.uint32(0xFFFF0000), F32)
    b = lax.bitcast_convert_type(w << 16, F32)
    return a, b


def _mod_kernel(c_ref, w_ref, b_ref, o_ref):
    c = c_ref[...]
    o_ref[...] = jnp.dot(c * jax.nn.sigmoid(c), w_ref[...], preferred_element_type=F32) + b_ref[...]


def _mod_rows(cond, w_mod, b_mod):
    n, d = cond.shape
    nout = w_mod.shape[1]
    bn = d
    return pl.pallas_call(
        _mod_kernel,
        grid=(nout // bn,),
        in_specs=[pl.BlockSpec((n, d), lambda j: (0, 0)),
                  pl.BlockSpec((d, bn), lambda j: (0, j)),
                  pl.BlockSpec((1, bn), lambda j: (0, j))],
        out_specs=pl.BlockSpec((n, bn), lambda j: (0, j)),
        out_shape=jax.ShapeDtypeStruct((n, nout), F32),
        compiler_params=_params(("arbitrary",)),
        name="mod",
    )(cond, w_mod, b_mod.reshape(1, nout))


def _modulated_norm(x, g, shift, scale):
    ms = jnp.mean(x * x, axis=-1, keepdims=True)
    return (x * lax.rsqrt(ms + RMS_EPS)) * (g * (1.0 + scale)) + shift


def _inproj_kernel(x_ref, mod_ref, g_ref, w_ref, u_ref):
    nbk, tm, d = x_ref.shape
    h = _modulated_norm(x_ref[...].reshape(nbk * tm, d), g_ref[...], mod_ref[0, 0:1, :], mod_ref[0, 1:2, :])
    u = jnp.dot(h.astype(BF16), w_ref[...], preferred_element_type=F32)
    packed = _pack_bf16_pair(u[:, 0:LANES], u[:, LANES:2 * LANES])
    for b in range(nbk):
        u_ref[b] = packed[b * tm:(b + 1) * tm]


def _inproj(x, mods, per_batch_mod, norm_g, w_s5, tile, name):
    bsz, n, d = x.shape
    tm = min(tile, n)
    nbk = 1 if per_batch_mod else max(1, min(bsz, tile // n))
    assert bsz % nbk == 0
    mod_map = (lambda b, i: (b, 0, 0)) if per_batch_mod else (lambda b, i: (0, 0, 0))
    assert S5_WIDTH == 2 * LANES
    return pl.pallas_call(
        _inproj_kernel,
        grid=(bsz // nbk, n // tm),
        in_specs=[pl.BlockSpec((nbk, tm, d), lambda b, i: (b, i, 0)),
                  pl.BlockSpec((1, N_MOD, d), mod_map),
                  pl.BlockSpec((1, d), lambda b, i: (0, 0)),
                  pl.BlockSpec((d, S5_WIDTH), lambda b, i: (0, 0))],
        out_specs=pl.BlockSpec((nbk, tm, LANES), lambda b, i: (b, i, 0)),
        out_shape=jax.ShapeDtypeStruct((bsz, n, LANES), U32),
        compiler_params=_params(("parallel", "parallel")),
        name=name,
    )(x, mods, norm_g.reshape(1, d), w_s5)


def _toeplitz_kernel(strip_ref, t_ref):
    lc = t_ref.shape[1] // S5_H
    for s in range(lc):
        off = (lc - 1 - s) * S5_H
        t_ref[0, s * S5_H:(s + 1) * S5_H, :] = strip_ref[0, :, off:off + lc * S5_H].astype(BF16)


def _toeplitz(strip):
    g_n, h_n, w = strip.shape
    n = S5_CHUNK * h_n
    return pl.pallas_call(
        _toeplitz_kernel,
        grid=(g_n,),
        in_specs=[pl.BlockSpec((1, h_n, w), lambda g: (g, 0, 0))],
        out_specs=pl.BlockSpec((1, n, n), lambda g: (g, 0, 0)),
        out_shape=jax.ShapeDtypeStruct((g_n, n, n), BF16),
        compiler_params=_params(("parallel",)),
        name="toeplitz",
    )(strip)


def _s5_matrices(lam_re, lam_im, log_dt, b_re, b_im, c_re, c_im, d_skip):
    lc, g_n, p_n, h_n = S5_CHUNK, S5_GROUPS, S5_P, S5_H
    lam = lax.complex(lam_re.astype(F32), lam_im.astype(F32))
    dt = jnp.exp(log_dt.astype(F32))[..., None]
    a_bar = jnp.exp(lam * dt)
    b_bar = ((a_bar - 1.0) / lam)[..., None] * lax.complex(b_re.astype(F32), b_im.astype(F32))
    cm = lax.complex(c_re.astype(F32), c_im.astype(F32))
    steps = jnp.arange(lc + 1, dtype=F32)
    apow = jnp.exp((lam * dt)[:, :, None, :] * steps[None, None, :, None])
    kern = jnp.einsum('dgop,dgjp,dgpi->dgjio', cm, apow[:, :, :lc], b_bar).real
    skip = jnp.eye(h_n, dtype=F32) * d_skip.astype(F32).reshape(g_n, 1, h_n)
    centre = kern[0, :, 0] + kern[1, :, 0] + skip
    lags = jnp.concatenate([kern[1, :, :0:-1], centre[:, None], kern[0, :, 1:]], axis=1)
    strip = lags.transpose(0, 2, 1, 3).reshape(g_n, h_n, (2 * lc - 1) * h_n)
    strip = jnp.pad(strip, ((0, 0), (0, 0), (0, h_n)))
    t_mat = _toeplitz(strip)

    def in_mat(pw, bb):
        return (pw[:, :, None, :] * bb.transpose(0, 2, 1)[:, None, :, :]).reshape(g_n, lc * h_n, p_n)

    mb_f = in_mat(apow[0, :, lc - 1::-1][:, :lc], b_bar[0])
    mb_b = in_mat(apow[1, :, :lc], b_bar[1])

    def out_mat(pw, cc):
        return (pw.transpose(0, 2, 1)[:, :, :, None] * cc.transpose(0, 2, 1)[:, :, None, :]).reshape(
            g_n, p_n, lc * h_n)

    mc_f = out_mat(apow[0, :, 1:lc + 1], cm[0])
    mc_b = out_mat(apow[1, :, lc:0:-1], cm[1])
    a_chunk = apow[:, :, lc]

    q_n = S5_PAIRS
    zeros_in = jnp.zeros((g_n, lc * h_n, p_n), F32)

    def pair_cols(m):
        m = m.reshape(q_n, 2, lc * h_n, p_n)
        z = zeros_in.reshape(q_n, 2, lc * h_n, p_n)[:, 0]
        top = jnp.concatenate([m[:, 0], z], axis=-1)
        bot = jnp.concatenate([z, m[:, 1]], axis=-1)
        return jnp.concatenate([top, bot], axis=1)

    mb_pair = jnp.concatenate([pair_cols(mb_f.real), pair_cols(mb_f.imag),
                               pair_cols(mb_b.real), pair_cols(mb_b.imag)], axis=-1)

    def pair_rows(m):
        m = m.reshape(q_n, 2, p_n, lc * h_n)
        z = jnp.zeros_like(m[:, 0])
        top = jnp.concatenate([m[:, 0], z], axis=-1)
        bot = jnp.concatenate([z, m[:, 1]], axis=-1)
        return jnp.concatenate([top, bot], axis=1)

    mc_pair = jnp.concatenate([pair_rows(mc_f.real), pair_rows(-mc_f.imag),
                               pair_rows(mc_b.real), pair_rows(-mc_b.imag)], axis=1)
    a_rows = jnp.stack([a_chunk[0].real, a_chunk[0].imag, a_chunk[1].real, a_chunk[1].imag], axis=0)
    a_rows = a_rows.reshape(4, q_n, 2 * p_n).transpose(1, 0, 2)
    a_rows = jnp.concatenate([a_rows, jnp.zeros_like(a_rows)], axis=1)
    return t_mat, mb_pair.astype(BF16), mc_pair.astype(BF16), a_rows


def _chunkify_kernel(u_ref, o_ref, *, gb):
    _, nb, nc, _ = o_ref.shape
    half = S5_CHUNK * S5_H
    per_slab = LANES // S5_H
    for b0 in range(0, nb, gb):
        slabs = ([], [])
        for t in range(S5_CHUNK):
            rows = [u_ref[b0 + b, pl.ds(t, nc, stride=S5_CHUNK), :] for b in range(gb)]
            for j, part in enumerate(_unpack_bf16_pair(rows[0] if gb == 1 else jnp.concatenate(rows, axis=0))):
                slabs[j].append(part.T)
        for j, cols in enumerate(slabs):
            for gl in range(per_slab):
                g = j * per_slab + gl
                m = jnp.concatenate([c[gl * S5_H:(gl + 1) * S5_H, :] for c in cols], axis=0)
                o_ref[g // 2, b0:b0 + gb, :, (g % 2) * half:(g % 2 + 1) * half] = (
                    m.T.astype(BF16).reshape(gb, nc, half))


def _layout_step(bsz, gb):
    return gb * LAYOUT_GROUPS if bsz % (gb * LAYOUT_GROUPS) == 0 else gb


def _chunkify(u_rows, name):
    bsz, n, _ = u_rows.shape
    nc = n // S5_CHUNK
    gb = min(bsz, max(1, LANES // nc))
    nb = _layout_step(bsz, gb)
    w = 2 * S5_CHUNK * S5_H
    return pl.pallas_call(
        functools.partial(_chunkify_kernel, gb=gb),
        grid=(bsz // nb,),
        in_specs=[pl.BlockSpec((nb, n, LANES), lambda b: (b, 0, 0))],
        out_specs=pl.BlockSpec((S5_PAIRS, nb, nc, w), lambda b: (0, b, 0, 0)),
        out_shape=jax.ShapeDtypeStruct((S5_PAIRS, bsz, nc, w), BF16),
        compiler_params=_params(("parallel",)),
        name=name,
    )(u_rows)


def _unchunkify_kernel(y_ref, o_ref):
    _, nb, nc, _ = y_ref.shape
    half = S5_CHUNK * S5_H
    per_slab = LANES // S5_H
    for b in range(nb):
        rows = []
        for g in range(S5_GROUPS):
            rows.append(y_ref[g // 2, b, :, (g % 2) * half:(g % 2 + 1) * half].astype(F32).T)
        for t in range(S5_CHUNK):
            tiles = [jnp.concatenate([r[t * S5_H:(t + 1) * S5_H, :] for r in rows[j * per_slab:(j + 1) * per_slab]],
                                     axis=0).T for j in range(2)]
            o_ref[b, pl.ds(t, nc, stride=S5_CHUNK), :] = _pack_bf16_pair(*tiles)


def _unchunkify(y_c):
    q_n, bsz, nc, w = y_c.shape
    n = nc * S5_CHUNK
    nb = _layout_step(bsz, 1)
    return pl.pallas_call(
        _unchunkify_kernel,
        grid=(bsz // nb,),
        in_specs=[pl.BlockSpec((q_n, nb, nc, w), lambda b: (0, b, 0, 0))],
        out_specs=pl.BlockSpec((nb, n, LANES), lambda b: (b, 0, 0)),
        out_shape=jax.ShapeDtypeStruct((bsz, n, LANES), U32),
        compiler_params=_params(("parallel",)),
        name="unchunkify",
    )(y_c)


def _s5_kernel(u_ref, uc_ref, t_ref, mb_ref, mc_ref, a_ref, y_ref, s_lat, s_ctx, h_scr, *, bb):
    _, bsz, n_lat, w = u_ref.shape
    n_ctx = uc_ref.shape[2]
    n_blk = w // LANES
    rb = bb * n_lat

    mb = mb_ref[0]

    def in_lat(i, carry):
        s = jnp.dot(u_ref[0, pl.ds(i * bb, bb)].reshape(rb, w), mb, preferred_element_type=F32)
        for k in range(bb):
            r = pl.multiple_of((i * bb + k) * STATE_PITCH, 8)
            for blk in range(n_blk):
                s_lat[blk, pl.ds(r, n_lat), :] = s[k * n_lat:(k + 1) * n_lat, blk * LANES:(blk + 1) * LANES]
        return carry

    lax.fori_loop(0, bsz // bb, in_lat, 0)
    sc = jnp.dot(uc_ref[0].reshape(bsz * n_ctx, w), mb, preferred_element_type=F32)
    for blk in range(n_blk):
        s_ctx[blk] = sc[:, blk * LANES:(blk + 1) * LANES]

    a_fr, a_fi, a_br, a_bi = (a_ref[0, k:k + 1, :] for k in range(4))

    def step(h, a_r, a_i, s_r, s_i):
        h_r, h_i = h
        return a_r * h_r - a_i * h_i + s_r, a_r * h_i + a_i * h_r + s_i

    def ctx_rows(blk, c):
        return s_ctx[blk, pl.ds(c, bsz, stride=n_ctx), :]

    def lat_rows(ref, blk, c):
        return ref.at[blk, pl.ds(c, bsz, stride=STATE_PITCH), :]

    def ctx_step(k, carry):
        hf, hb = carry
        kb = n_ctx - 1 - k
        hf = step(hf, a_fr, a_fi, ctx_rows(0, k), ctx_rows(1, k))
        hb = step(hb, a_br, a_bi, ctx_rows(2, kb), ctx_rows(3, kb))
        return hf, hb

    zero = jnp.zeros((bsz, LANES), F32)
    carry = lax.fori_loop(0, n_ctx, ctx_step, ((zero, zero), (zero, zero)))

    def lat_step(k, carry):
        hf, hb = carry
        kb = n_lat - 1 - k
        lat_rows(h_scr, 0, k)[...] = hf[0]
        lat_rows(h_scr, 1, k)[...] = hf[1]
        lat_rows(h_scr, 2, kb)[...] = hb[0]
        lat_rows(h_scr, 3, kb)[...] = hb[1]
        hf = step(hf, a_fr, a_fi, lat_rows(s_lat, 0, k)[...], lat_rows(s_lat, 1, k)[...])
        hb = step(hb, a_br, a_bi, lat_rows(s_lat, 2, kb)[...], lat_rows(s_lat, 3, kb)[...])
        return hf, hb

    lax.fori_loop(0, n_lat, lat_step, carry)

    t0 = t_ref[0]
    t1 = t_ref[1]
    mc = mc_ref[0]
    half = S5_CHUNK * S5_H

    def out_lat(i, carry):
        u = u_ref[0, pl.ds(i * bb, bb)].reshape(rb, w)
        h_rows = []
        for k in range(bb):
            r = pl.multiple_of((i * bb + k) * STATE_PITCH, 8)
            h_rows.append(jnp.concatenate([h_scr[blk, pl.ds(r, n_lat), :] for blk in range(n_blk)], axis=1))
        h = jnp.concatenate(h_rows, axis=0).astype(BF16)
        inter = jnp.dot(h, mc, preferred_element_type=F32)
        y0 = jnp.dot(u[:, :half], t0, preferred_element_type=F32) + inter[:, :half]
        y1 = jnp.dot(u[:, half:], t1, preferred_element_type=F32) + inter[:, half:]
        y = jnp.concatenate([y0, y1], axis=1).astype(BF16)
        y_ref[0, pl.ds(i * bb, bb)] = y.reshape(bb, n_lat, w)
        return carry

    lax.fori_loop(0, bsz // bb, out_lat, 0)


def _s5_scan(u_c, uc_c, t_mat, mb_pair, mc_pair, a_rows):
    q_n, bsz, n_lat, w = u_c.shape
    n_ctx = uc_c.shape[2]
    assert n_lat + 8 == STATE_PITCH
    bb = min(8, bsz)
    n_blk = w // LANES
    return pl.pallas_call(
        functools.partial(_s5_kernel, bb=bb),
        grid=(q_n,),
        in_specs=[pl.BlockSpec((1, bsz, n_lat, w), lambda q: (q, 0, 0, 0)),
                  pl.BlockSpec((1, bsz, n_ctx, w), lambda q: (q, 0, 0, 0)),
                  pl.BlockSpec((2, w // 2, w // 2), lambda q: (q, 0, 0)),
                  pl.BlockSpec((1, w, w), lambda q: (q, 0, 0)),
                  pl.BlockSpec((1, w, w), lambda q: (q, 0, 0)),
                  pl.BlockSpec((1, 8, LANES), lambda q: (q, 0, 0))],
        out_specs=pl.BlockSpec((1, bsz, n_lat, w), lambda q: (q, 0, 0, 0)),
        out_shape=jax.ShapeDtypeStruct((q_n, bsz, n_lat, w), BF16),
        scratch_shapes=[pltpu.VMEM((n_blk, bsz * STATE_PITCH, LANES), F32),
                        pltpu.VMEM((n_blk, bsz * n_ctx, LANES), F32),
                        pltpu.VMEM((n_blk, bsz * STATE_PITCH, LANES), F32)],
        compiler_params=_params(("parallel",)),
        name="s5_scan",
    )(u_c, uc_c, t_mat, mb_pair, mc_pair, a_rows)


def _first_max(rows):
    best = rows[0]
    for r in rows[1:]:
        best = jnp.maximum(best, r)
    idx = jnp.full(best.shape, float(len(rows) - 1), F32)
    for k in range(len(rows) - 2, -1, -1):
        idx = jnp.where(rows[k] == best, float(k), idx)
    return best, idx


def _route_rows(lg):
    g_rows = [lg[k:k + 1] for k in range(N_GROUPS)]
    g_max, g_idx = _first_max(g_rows)
    g_sum = sum(jnp.exp(r - g_max) for r in g_rows)
    g_p = 1.0 / g_sum
    e_rows = []
    for j in range(EXPERTS_PER_GROUP):
        r = lg[N_GROUPS + (N_GROUPS - 1) * EXPERTS_PER_GROUP + j:][:1]
        for g in range(N_GROUPS - 2, -1, -1):
            k = N_GROUPS + g * EXPERTS_PER_GROUP + j
            r = jnp.where(g_idx == float(g), lg[k:k + 1], r)
        e_rows.append(r)
    v1, i1 = _first_max(e_rows)
    rest = [jnp.where(i1 == float(j), -jnp.inf, e_rows[j]) for j in range(EXPERTS_PER_GROUP)]
    v2, i2 = _first_max(rest)
    e21 = jnp.exp(v2 - v1)
    w1 = g_p / (1.0 + e21)
    w2 = w1 * e21
    lo = jnp.minimum(i1, i2)
    hi = jnp.maximum(i1, i2)
    base = jnp.where(lo == 0.0, 0.0, jnp.where(lo == 1.0, 3.0, 5.0))
    bucket = g_idx * float(PAIRS_PER_GROUP) + base + hi - lo - 1.0
    first_is_lo = i1 < i2
    return bucket, jnp.where(first_is_lo, w1, w2), jnp.where(first_is_lo, w2, w1)


def _mix_kernel(x_ref, xup_ref, xdn_ref, y_ref, mod_ref, g1_ref, g2_ref, cw_ref, win_ref, wglu_ref, bglu_ref,
                wout_ref, wr_ref, br_ref, *rest):
    n_cast = (len(rest) - 4) // 2
    x1_ref, h2p_ref, route_ref, counts_ref = rest[n_cast:n_cast + 4]
    for src, dst in zip(rest[:n_cast], rest[n_cast + 4:]):
        dst[...] = src[...].astype(BF16)
    i = pl.program_id(1)
    tm = x_ref.shape[1]
    d = x_ref.shape[2]
    sub = min(MIX_SUB, tm)
    n_sub = tm // sub
    cw, rw = CONV_WIDTH, CONV_ROW_WIDTH

    def hidden(xv):
        return _modulated_norm(xv, g1_ref[...], mod_ref[0, 0:1, :], mod_ref[0, 1:2, :]).astype(BF16)

    def halo(h):
        zh = jnp.dot(h, win_ref[:, 2 * cw:3 * cw], preferred_element_type=F32)
        return zh[:, 0:rw] * zh[:, rw:cw]

    @pl.when(jnp.logical_and(pl.program_id(0) == 0, i == 0))
    def _():
        counts_ref[...] = jnp.zeros_like(counts_ref)

    def sub_tile(s):
        r0 = s * sub
        xv = x_ref[0, r0:r0 + sub, :]
        hx = hidden(xv)
        g = jax.nn.gelu(jnp.concatenate(_unpack_bf16_pair(y_ref[0, r0:r0 + sub, :]), axis=1))
        hx_up = hidden(xup_ref[0]) if s == 0 else None
        hx_dn = hidden(xdn_ref[0]) if s == n_sub - 1 else None
        yield
        z_r = jnp.dot(hx, win_ref[:, cw:2 * cw], preferred_element_type=F32)
        cr = z_r[:, 0:rw] * z_r[:, rw:cw]
        z_c = jnp.dot(hx, win_ref[:, 2 * cw:3 * cw], preferred_element_type=F32)
        cc = z_c[:, 0:rw] * z_c[:, rw:cw]
        col_products[s] = cc
        yield
        bg = jnp.dot(hx, win_ref[:, 0:cw], preferred_element_type=F32)
        if s == 0:
            up_halo = jnp.where(i == 0, 0.0, halo(hx_up))
        if s == n_sub - 1:
            dn_halo = jnp.where(i == pl.num_programs(1) - 1, 0.0, halo(hx_dn))
        glu = g * jax.nn.sigmoid(jnp.dot(g.astype(BF16), wglu_ref[...], preferred_element_type=F32) + bglu_ref[...])
        yield
        row = lax.broadcasted_iota(jnp.int32, (sub, 1), 0)
        col_in_row = row % GRID_W
        left = jnp.where(col_in_row == 0, 0.0, pltpu.roll(cr, 1, axis=0))
        right = jnp.where(col_in_row == GRID_W - 1, 0.0, pltpu.roll(cr, sub - 1, axis=0))
        w_r = cw_ref[:, :rw]
        row_part = left * w_r[0:1] + cr * w_r[1:2] + right * w_r[2:3]
        if s > 0:
            up_halo = col_products[s - 1][sub - GRID_W:]
        if s < n_sub - 1:
            dn_halo = col_products[s + 1][:GRID_W]
        up = jnp.concatenate([up_halo, cc[:sub - GRID_W]], axis=0)
        dn = jnp.concatenate([cc[GRID_W:], dn_halo], axis=0)
        w_c = cw_ref[:, rw:]
        col_part = up * w_c[0:1] + cc * w_c[1:2] + dn * w_c[2:3]
        y_row = (bg[:, 0:rw] * row_part).astype(BF16)
        y_col = (bg[:, rw:cw] * col_part).astype(BF16)
        mixed = jnp.concatenate([glu.astype(BF16), y_row, y_col], axis=1)
        yield
        yx = jnp.dot(mixed, wout_ref[...], preferred_element_type=F32)
        yield
        for c0 in range(0, sub, TAIL_ROWS):
            rs = r0 + c0
            x1 = xv[c0:c0 + TAIL_ROWS] + mod_ref[0, 2:3, :] * yx[c0:c0 + TAIL_ROWS]
            x1_ref[0, rs:rs + TAIL_ROWS, :] = x1
            h2 = _modulated_norm(x1, g2_ref[...], mod_ref[0, 3:4, :], mod_ref[0, 4:5, :])
            h2b = h2.astype(BF16)
            lg2 = lax.dot_general(wr_ref[...], h2b, (((1,), (1,)), ((), ())), preferred_element_type=F32)
            lg = lg2[:ROUTER_ROWS] + lg2[ROUTER_ROWS:] + br_ref[:, 0:TAIL_ROWS]
            bucket, w_a, w_b = _route_rows(lg)
            r8 = lax.broadcasted_iota(jnp.int32, (8, TAIL_ROWS), 0)
            route_ref[:, rs:rs + TAIL_ROWS] = jnp.where(
                r8 == 0, bucket, jnp.where(r8 == 1, w_a, jnp.where(r8 == 2, w_b, 0.0)))
            rl = lax.broadcasted_iota(jnp.int32, (ROW_EXTRA, TAIL_ROWS), 0)
            gates_t = jnp.where(rl == 0, w_a, jnp.where(rl == 1, w_b, 0.0))
            h2p_ref[0, rs:rs + TAIL_ROWS, 0:d // 2] = _pack_bf16_pair(h2[:, :d // 2], h2[:, d // 2:])
            h2p_ref[0, rs:rs + TAIL_ROWS, d // 2:] = lax.bitcast_convert_type(gates_t.T, U32)
            rb = lax.broadcasted_iota(jnp.int32, (BUCKET_ROWS, TAIL_ROWS), 0).astype(F32)
            counts_ref[...] += jnp.sum(jnp.where(rb == bucket, 1.0, 0.0), axis=-1, keepdims=True)
        yield

    n_stage = 6
    col_products = {}
    tiles = [sub_tile(s) for s in range(n_sub)]
    for step in range(n_sub + n_stage - 1):
        for s in reversed(range(n_sub)):
            if 0 <= step - s < n_stage:
                next(tiles[s])


def _mix(x, y_s5, mods, norm1_g, norm2_g, conv_w, w_conv, w_glu, b_glu, w_out, w_router_t, b_router_t, tm, b0, nb,
         to_cast):
    _, n, d = x.shape
    nt = n // tm
    n_steps = nb * nt
    cast_specs = []
    for w in to_cast:
        per = -(-w.shape[0] // n_steps)
        while w.shape[0] % per:
            per += 1
        n_blk = w.shape[0] // per
        cast_specs.append(pl.BlockSpec((per,) + w.shape[1:],
                                       lambda b, i, n_blk=n_blk: ((b * nt + i) * n_blk // n_steps, 0, 0)))
    halo_blocks = n // GRID_W
    per_tile = tm // GRID_W
    tok_out = lambda w: pl.BlockSpec((1, tm, w), lambda b, i: (b, i, 0))
    full = lambda a: pl.BlockSpec(a.shape, lambda b, i: (0,) * a.ndim)
    args = (x, x, x, y_s5, mods, norm1_g.reshape(1, d), norm2_g.reshape(1, d), conv_w, w_conv, w_glu,
            b_glu.reshape(1, -1), w_out, w_router_t, b_router_t)
    in_specs = [pl.BlockSpec((1, tm, d), lambda b, i: (b + b0, i, 0)),
                pl.BlockSpec((1, GRID_W, d), lambda b, i: (b + b0, jnp.maximum(i * per_tile - 1, 0), 0)),
                pl.BlockSpec((1, GRID_W, d),
                             lambda b, i: (b + b0, jnp.minimum((i + 1) * per_tile, halo_blocks - 1), 0)),
                pl.BlockSpec((1, tm, LANES), lambda b, i: (b + b0, i, 0)),
                pl.BlockSpec((1, N_MOD, d), lambda b, i: (b + b0, 0, 0))] + [full(a) for a in args[5:]]
    return pl.pallas_call(
        _mix_kernel,
        grid=(nb, nt),
        in_specs=in_specs + cast_specs,
        out_specs=[tok_out(d), tok_out(d // 2 + ROW_EXTRA),
                   pl.BlockSpec((8, tm), lambda b, i: (0, b * nt + i)),
                   pl.BlockSpec((BUCKET_ROWS, LANES), lambda b, i: (0, 0))] + cast_specs,
        out_shape=[jax.ShapeDtypeStruct((nb, n, d), F32),
                   jax.ShapeDtypeStruct((nb, n, d // 2 + ROW_EXTRA), U32),
                   jax.ShapeDtypeStruct((8, nb * n), F32),
                   jax.ShapeDtypeStruct((BUCKET_ROWS, LANES), F32)]
        + [jax.ShapeDtypeStruct(w.shape, BF16) for w in to_cast],
        compiler_params=_params(("arbitrary", "arbitrary")),
        name="mix",
    )(*args, *to_cast)


def _rank_kernel(route_ref, offs_ref, dest_ref, run_ref):
    tr = route_ref.shape[1]
    n_blk = tr // LANES

    @pl.when(pl.program_id(0) == 0)
    def _():
        run_ref[...] = jnp.zeros_like(run_ref)

    bucket = route_ref[0:1, :]
    rb = lax.broadcasted_iota(jnp.int32, (BUCKET_ROWS, tr), 0).astype(F32)
    onehot = jnp.where(rb == bucket, 1.0, 0.0)
    blocks = [onehot[:, k * LANES:(k + 1) * LANES] for k in range(n_blk)]
    s_idx = lax.broadcasted_iota(jnp.int32, (LANES, LANES), 0)
    t_idx = lax.broadcasted_iota(jnp.int32, (LANES, LANES), 1)
    tri = jnp.where(s_idx <= t_idx, 1.0, 0.0).astype(BF16)
    prefix = jnp.dot(jnp.concatenate(blocks, axis=0).astype(BF16), tri, preferred_element_type=F32)
    start = run_ref[:, 0:1] + offs_ref[:, 0:1]
    carry = start
    for k in range(n_blk):
        pk = prefix[k * BUCKET_ROWS:(k + 1) * BUCKET_ROWS]
        dest = jnp.sum(blocks[k] * (pk - 1.0 + carry), axis=0, keepdims=True)
        dest_ref[:, k * LANES:(k + 1) * LANES] = dest.astype(jnp.int32)
        carry = carry + pk[:, LANES - 1:LANES]
    run_ref[...] += carry - start


def _rank(route_t, offs_rows, tr):
    n = route_t.shape[1]
    return pl.pallas_call(
        _rank_kernel,
        grid=(n // tr,),
        in_specs=[pl.BlockSpec((8, tr), lambda i: (0, i)),
                  pl.BlockSpec((BUCKET_ROWS, LANES), lambda i: (0, 0))],
        out_specs=pl.BlockSpec((1, tr), lambda i: (0, i)),
        out_shape=jax.ShapeDtypeStruct((1, n), jnp.int32),
        scratch_shapes=[pltpu.VMEM((BUCKET_ROWS, LANES), F32)],
        compiler_params=_params(("arbitrary",)),
        name="rank",
    )(route_t, offs_rows)


def _sc_mesh():
    return plsc.VectorSubcoreMesh(core_axis_name="core", subcore_axis_name="subcore")


def _scatter_rows(src, dest, n_out):
    n, w = src.shape
    sub = SC_INDEX_TILE // SC_ROWS

    @functools.partial(pl.kernel, out_type=jax.ShapeDtypeStruct((n_out, w), src.dtype), mesh=_sc_mesh(),
                       scratch_types=[], name="scatter_rows")
    def scatter(x_hbm, i_hbm, o_hbm):
        def body(x_vmem, i_vmem):
            j = pl.program_id(1)
            pltpu.sync_copy(x_vmem, o_hbm.at[i_vmem.at[0, pl.ds(j * SC_ROWS, SC_ROWS)]])

        pltpu.emit_pipeline(
            body,
            grid=(n // SC_INDEX_TILE, sub),
            in_specs=[pl.BlockSpec((SC_ROWS, w), lambda i, j: (i * sub + j, 0)),
                      pl.BlockSpec((1, SC_INDEX_TILE), lambda i, j: (0, i))],
            out_specs=[],
            core_axis_name=("core", "subcore"),
            dimension_semantics=(pltpu.PARALLEL, pltpu.ARBITRARY),
        )(x_hbm, i_hbm)

    return scatter(src, dest)


def _gather_rows(src, idx):
    n = idx.shape[1]
    w = src.shape[1]
    sub = SC_INDEX_TILE // SC_ROWS

    @functools.partial(pl.kernel, out_type=jax.ShapeDtypeStruct((n, w), src.dtype), mesh=_sc_mesh(),
                       scratch_types=[], name="gather_rows")
    def gather(x_hbm, i_hbm, o_hbm):
        def body(i_vmem, o_vmem):
            j = pl.program_id(1)
            pltpu.sync_copy(x_hbm.at[i_vmem.at[0, pl.ds(j * SC_ROWS, SC_ROWS)]], o_vmem)

        pltpu.emit_pipeline(
            body,
            grid=(n // SC_INDEX_TILE, sub),
            in_specs=[pl.BlockSpec((1, SC_INDEX_TILE), lambda i, j: (0, i))],
            out_specs=[pl.BlockSpec((SC_ROWS, w), lambda i, j: (i * sub + j, 0))],
            core_axis_name=("core", "subcore"),
            dimension_semantics=(pltpu.PARALLEL, pltpu.ARBITRARY),
        )(i_hbm, o_hbm)

    return gather(src, idx)


def _moe_kernel(ea_ref, eb_ref, valid_ref, blk_ref, rows_ref, w1a_ref, w3a_ref, w2a_ref, w1b_ref, w3b_ref, w2b_ref,
                o_ref):
    j = pl.program_id(0)
    half = rows_ref.shape[1] - ROW_EXTRA

    @pl.when(valid_ref[j] != 0)
    def _():
        ha, hb = _unpack_bf16_pair(rows_ref[:, 0:half])
        ha = ha.astype(BF16)
        hb = hb.astype(BF16)
        gates = lax.bitcast_convert_type(rows_ref[:, half:], F32)

        def expert(w1_ref, w3_ref, w2_ref, gate):
            def up(w_ref):
                return (jnp.dot(ha, w_ref[0, 0:half, :], preferred_element_type=F32)
                        + jnp.dot(hb, w_ref[0, half:, :], preferred_element_type=F32))

            a1 = up(w1_ref)
            he = (a1 * jax.nn.sigmoid(a1)) * up(w3_ref) * gate
            return jnp.dot(he.astype(BF16), w2_ref[0], preferred_element_type=F32)

        y = (expert(w1a_ref, w3a_ref, w2a_ref, gates[:, 0:1]) + expert(w1b_ref, w3b_ref, w2b_ref, gates[:, 1:2]))
        o_ref[...] = _pack_bf16_pair(y[:, :half], y[:, half:])


def _moe_grouped(rows, tile_ea, tile_eb, tile_valid, tile_blk, w1, w3, w2, tmm):
    r, w = rows.shape
    _, de, d = w2.shape
    up_a = pl.BlockSpec((1, d, de), lambda j, ea, eb, va, bk: (ea[j], 0, 0))
    up_b = pl.BlockSpec((1, d, de), lambda j, ea, eb, va, bk: (eb[j], 0, 0))
    grid_spec = pltpu.PrefetchScalarGridSpec(
        num_scalar_prefetch=4,
        grid=(r // tmm,),
        in_specs=[pl.BlockSpec((tmm, w), lambda j, ea, eb, va, bk: (bk[j], 0)),
                  up_a, up_a, pl.BlockSpec((1, de, d), lambda j, ea, eb, va, bk: (ea[j], 0, 0)),
                  up_b, up_b, pl.BlockSpec((1, de, d), lambda j, ea, eb, va, bk: (eb[j], 0, 0))],
        out_specs=pl.BlockSpec((tmm, d // 2), lambda j, ea, eb, va, bk: (bk[j], 0)),
    )
    return pl.pallas_call(
        _moe_kernel,
        grid_spec=grid_spec,
        out_shape=jax.ShapeDtypeStruct((r, d // 2), U32),
        compiler_params=_params(("arbitrary",)),
        name="moe",
    )(tile_ea, tile_eb, tile_valid, tile_blk, rows, w1, w3, w2, w1, w3, w2)


def _final_kernel(x1_ref, moe_ref, mod_ref, fg_ref, *rest):
    o_ref = rest[-1]
    ya, yb = _unpack_bf16_pair(moe_ref[0])
    half = ya.shape[1]
    gate = mod_ref[0, 5:6, :]
    xa = x1_ref[0, :, 0:half] + gate[:, 0:half] * ya
    xb = x1_ref[0, :, half:] + gate[:, half:] * yb
    ms = (jnp.sum(xa * xa, axis=-1, keepdims=True) + jnp.sum(xb * xb, axis=-1, keepdims=True)) / (2 * half)
    inv = lax.rsqrt(ms + RMS_EPS)
    o_ref[0, :, 0:half] = xa * inv * fg_ref[:, 0:half]
    o_ref[0, :, half:] = xb * inv * fg_ref[:, half:]


def _final(x1, moe_tok, mods, final_g, tm, b0, bsz, out_prev):
    nb, n, d = x1.shape
    tok = lambda w: pl.BlockSpec((1, tm, w), lambda b, i: (b, i, 0))
    args = [x1, moe_tok, mods, final_g.reshape(1, d)]
    in_specs = [tok(d), tok(d // 2), pl.BlockSpec((1, N_MOD, d), lambda b, i: (b + b0, 0, 0)),
                pl.BlockSpec((1, d), lambda b, i: (0, 0))]
    aliases = {}
    if out_prev is not None:
        args.append(out_prev)
        in_specs.append(pl.BlockSpec(memory_space=pl.ANY))
        aliases = {len(args) - 1: 0}
    return pl.pallas_call(
        _final_kernel,
        grid=(nb, n // tm),
        in_specs=in_specs,
        out_specs=pl.BlockSpec((1, tm, d), lambda b, i: (b + b0, i, 0)),
        out_shape=jax.ShapeDtypeStruct((bsz, n, d), F32),
        input_output_aliases=aliases,
        compiler_params=_params(("parallel", "parallel")),
        name="final",
    )(*args)


def _tile_plan(counts, tmm, n_tiles):
    tiles = (counts + (tmm - 1)) // tmm
    tile_end = jnp.cumsum(tiles)
    offs = (tile_end - tiles) * tmm
    n_valid = tile_end[-1]
    j = jnp.arange(n_tiles, dtype=jnp.int32)
    bucket = jnp.sum((tile_end[None, :] <= jnp.minimum(j, n_valid - 1)[:, None]).astype(jnp.int32), axis=1)
    pair_lo = jnp.array([0, 0, 0, 1, 1, 2], jnp.int32)
    pair_hi = jnp.array([1, 2, 3, 2, 3, 3], jnp.int32)
    group = bucket // PAIRS_PER_GROUP
    pair = bucket % PAIRS_PER_GROUP
    tile_ea = group * EXPERTS_PER_GROUP + pair_lo[pair]
    tile_eb = group * EXPERTS_PER_GROUP + pair_hi[pair]
    return offs, tile_ea, tile_eb, (j < n_valid).astype(jnp.int32), jnp.minimum(j, n_valid - 1)


def kernel(x, c, ctx, c_ctx, w_mod, b_mod, norm1_g, norm2_g, w_in, s5_lambda_re, s5_lambda_im, s5_log_dt,
           s5_b_re, s5_b_im, s5_c_re, s5_c_im, s5_d, w_glu, b_glu, conv_w, w_out, router_group_w,
           router_group_b, router_expert_w, router_expert_b, expert_w1, expert_w3, expert_w2, final_g):
    assert w_mod.shape[0] == 1, "single-layer kernel"
    bsz, n_tok, d = x.shape
    l = 0
    tm = min(TOKEN_TILE, n_tok)

    n_cond = bsz + 1
    pad = (-n_cond) % 8
    cond = jnp.concatenate([c, c_ctx[None, :], jnp.zeros((pad, d), F32)], axis=0)
    m = _mod_rows(cond, w_mod[l], b_mod[l])
    mx = m[:bsz].reshape(bsz, N_MOD, d)
    mc = m[bsz:bsz + 1].reshape(1, N_MOD, d)

    w_in_b = w_in[l].astype(BF16)
    w_s5 = w_in_b[:, :S5_WIDTH]
    o_c = S5_WIDTH + CONV_WIDTH
    o_v = S5_WIDTH + 2 * CONV_WIDTH
    w_conv = jnp.concatenate(
        [w_in_b[:, S5_WIDTH:o_c], w_in_b[:, o_c:o_c + CONV_ROW_WIDTH], w_in_b[:, o_v:o_v + CONV_ROW_WIDTH],
         w_in_b[:, o_c + CONV_ROW_WIDTH:o_v], w_in_b[:, o_v + CONV_ROW_WIDTH:]], axis=1)
    u = _inproj(x, mx, True, norm1_g[l], w_s5, INPROJ_TILE, "inproj")
    uc = _inproj(ctx, mc, False, norm1_g[l], w_s5, INPROJ_TILE, "inproj_ctx")

    t_mat, mb_pair, mc_pair, a_rows = _s5_matrices(
        s5_lambda_re[l], s5_lambda_im[l], s5_log_dt[l], s5_b_re[l], s5_b_im[l], s5_c_re[l], s5_c_im[l], s5_d[l])
    y_c = _s5_scan(_chunkify(u, "chunkify"), _chunkify(uc, "chunkify_ctx"), t_mat, mb_pair, mc_pair, a_rows)
    y_s5 = _unchunkify(y_c)

    n_logits = N_GROUPS + N_EXPERTS
    w_router = jnp.concatenate(
        [router_group_w[l], router_expert_w[l], jnp.zeros((d, ROUTER_ROWS - n_logits), F32)], axis=1).T
    w_router_hi = w_router.astype(BF16)
    w_router_lo = (w_router - w_router_hi.astype(F32)).astype(BF16)
    w_router_t = jnp.concatenate([w_router_hi, w_router_lo], axis=0)
    b_router = jnp.concatenate([router_group_b[l], router_expert_b[l], jnp.zeros((ROUTER_ROWS - n_logits,), F32)])
    tm_mix = min(MIX_TILE, n_tok)
    b_router_t = jnp.broadcast_to(b_router[:, None], (ROUTER_ROWS, min(MIX_SUB, tm_mix)))

    first = bsz * MOE_FIRST_SHARE[0] // MOE_FIRST_SHARE[1]
    part_sizes = [first, bsz - first] if 0 < first < bsz else [bsz]
    n_buckets = N_GROUPS * PAIRS_PER_GROUP
    w_glu_b = w_glu[l].astype(BF16)
    w_out_b = w_out[l].astype(BF16)
    experts_f32 = (expert_w1[l], expert_w3[l], expert_w2[l])
    cast_plan = [experts_f32] if len(part_sizes) == 1 else [experts_f32[:2], experts_f32[2:]]
    w_experts = []
    staged = []
    b0 = 0
    for nb, to_cast in zip(part_sizes, cast_plan):
        n_part = nb * n_tok
        n_rows = n_part + n_buckets * MOE_TILE
        x1, h2p, route_t, counts, *w_cast = _mix(x, y_s5, mx, norm1_g[l], norm2_g[l], conv_w[l], w_conv, w_glu_b,
                                                 b_glu[l], w_out_b, w_router_t, b_router_t, tm_mix, b0, nb, to_cast)
        w_experts += w_cast
        offs, *tiles = _tile_plan(counts[:n_buckets, 0].astype(jnp.int32), MOE_TILE, n_rows // MOE_TILE)
        offs_rows = jnp.zeros((BUCKET_ROWS,), F32).at[:n_buckets].set(offs.astype(F32))
        dest = _rank(route_t, jnp.broadcast_to(offs_rows[:, None], (BUCKET_ROWS, LANES)), min(RANK_TILE, n_part))
        rows = _scatter_rows(h2p.reshape(n_part, d // 2 + ROW_EXTRA), dest, n_rows)
        staged.append((b0, x1, rows, dest, tiles))
        b0 += nb
    out = None
    for b0, x1, rows, dest, tiles in staged:
        y_rows = _moe_grouped(rows, *tiles, *w_experts, MOE_TILE)
        moe_tok = _gather_rows(y_rows, dest).reshape(x1.shape[0], n_tok, d // 2)
        out = _final(x1, moe_tok, mx, final_g, tm, b0, bsz, out)
    return out
```

```python
import functools

import jax
import jax.numpy as jnp
from jax import lax
from jax.experimental import pallas as pl
from jax.experimental.pallas import tpu as pltpu
from jax.experimental.pallas import tpu_sc as plsc

F32 = jnp.float32
BF16 = jnp.bfloat16
U32 = jnp.uint32

RMS_EPS = 1e-6
N_MOD = 6
GRID_W = 64
S5_WIDTH = 256
S5_H = 16
S5_P = 64
S5_GROUPS = S5_WIDTH // S5_H
S5_PAIRS = S5_GROUPS // 2
S5_CHUNK = 16
LANES = 128
STATE_PITCH = 136
CONV_WIDTH = 768
CONV_ROW_WIDTH = CONV_WIDTH // 2
N_GROUPS = 4
EXPERTS_PER_GROUP = 4
N_EXPERTS = N_GROUPS * EXPERTS_PER_GROUP
PAIRS_PER_GROUP = 6
ROUTER_ROWS = 32
BUCKET_ROWS = 32
ROW_EXTRA = 128
TOKEN_TILE = 2048
MOE_TILE = 512
RANK_TILE = 2048
INPROJ_TILE = 2048
MIX_TILE = 1024
MIX_SUB = 512
TAIL_ROWS = 128
MOE_FIRST_SHARE = (5, 8)
LAYOUT_GROUPS = 4
SC_ROWS = 64
SC_INDEX_TILE = 128
VMEM_LIMIT = 52 * 1024 * 1024


def _params(sem, vmem=VMEM_LIMIT):
    return pltpu.CompilerParams(dimension_semantics=sem, vmem_limit_bytes=vmem)


def _pack_bf16_pair(a, b):
    ua = lax.bitcast_convert_type(a.astype(BF16).astype(F32), U32)
    ub = lax.bitcast_convert_type(b.astype(BF16).astype(F32), U32)
    return ua | (ub >> 16)


def _unpack_bf16_pair(w):
    a = lax.bitcast_convert_type(w & jnp.uint32(0xFFFF0000), F32)
    b = lax.bitcast_convert_type(w << 16, F32)
    return a, b


def _mod_kernel(c_ref, w_ref, b_ref, o_ref):
    c = c_ref[...]
    o_ref[...] = jnp.dot(c * jax.nn.sigmoid(c), w_ref[...], preferred_element_type=F32) + b_ref[...]


def _mod_rows(cond, w_mod, b_mod):
    n, d = cond.shape
    nout = w_mod.shape[1]
    bn = d
    return pl.pallas_call(
        _mod_kernel,
        grid=(nout // bn,),
        in_specs=[pl.BlockSpec((n, d), lambda j: (0, 0)),
                  pl.BlockSpec((d, bn), lambda j: (0, j)),
                  pl.BlockSpec((1, bn), lambda j: (0, j))],
        out_specs=pl.BlockSpec((n, bn), lambda j: (0, j)),
        out_shape=jax.ShapeDtypeStruct((n, nout), F32),
        compiler_params=_params(("arbitrary",)),
        name="mod",
    )(cond, w_mod, b_mod.reshape(1, nout))


def _modulated_norm(x, g, shift, scale):
    ms = jnp.mean(x * x, axis=-1, keepdims=True)
    return (x * lax.rsqrt(ms + RMS_EPS)) * (g * (1.0 + scale)) + shift


def _inproj_kernel(x_ref, mod_ref, g_ref, w_ref, u_ref):
    nbk, tm, d = x_ref.shape
    h = _modulated_norm(x_ref[...].reshape(nbk * tm, d), g_ref[...], mod_ref[0, 0:1, :], mod_ref[0, 1:2, :])
    u = jnp.dot(h.astype(BF16), w_ref[...], preferred_element_type=F32)
    packed = _pack_bf16_pair(u[:, 0:LANES], u[:, LANES:2 * LANES])
    for b in range(nbk):
        u_ref[b] = packed[b * tm:(b + 1) * tm]


def _inproj(x, mods, per_batch_mod, norm_g, w_s5, tile, name):
    bsz, n, d = x.shape
    tm = min(tile, n)
    nbk = 1 if per_batch_mod else max(1, min(bsz, tile // n))
    assert bsz % nbk == 0
    mod_map = (lambda b, i: (b, 0, 0)) if per_batch_mod else (lambda b, i: (0, 0, 0))
    assert S5_WIDTH == 2 * LANES
    return pl.pallas_call(
        _inproj_kernel,
        grid=(bsz // nbk, n // tm),
        in_specs=[pl.BlockSpec((nbk, tm, d), lambda b, i: (b, i, 0)),
                  pl.BlockSpec((1, N_MOD, d), mod_map),
                  pl.BlockSpec((1, d), lambda b, i: (0, 0)),
                  pl.BlockSpec((d, S5_WIDTH), lambda b, i: (0, 0))],
        out_specs=pl.BlockSpec((nbk, tm, LANES), lambda b, i: (b, i, 0)),
        out_shape=jax.ShapeDtypeStruct((bsz, n, LANES), U32),
        compiler_params=_params(("parallel", "parallel")),
        name=name,
    )(x, mods, norm_g.reshape(1, d), w_s5)


def _toeplitz_kernel(strip_ref, t_ref):
    lc = t_ref.shape[1] // S5_H
    for s in range(lc):
        off = (lc - 1 - s) * S5_H
        t_ref[0, s * S5_H:(s + 1) * S5_H, :] = strip_ref[0, :, off:off + lc * S5_H].astype(BF16)


def _toeplitz(strip):
    g_n, h_n, w = strip.shape
    n = S5_CHUNK * h_n
    return pl.pallas_call(
        _toeplitz_kernel,
        grid=(g_n,),
        in_specs=[pl.BlockSpec((1, h_n, w), lambda g: (g, 0, 0))],
        out_specs=pl.BlockSpec((1, n, n), lambda g: (g, 0, 0)),
        out_shape=jax.ShapeDtypeStruct((g_n, n, n), BF16),
        compiler_params=_params(("parallel",)),
        name="toeplitz",
    )(strip)


def _s5_matrices(lam_re, lam_im, log_dt, b_re, b_im, c_re, c_im, d_skip):
    lc, g_n, p_n, h_n = S5_CHUNK, S5_GROUPS, S5_P, S5_H
    lam = lax.complex(lam_re.astype(F32), lam_im.astype(F32))
    dt = jnp.exp(log_dt.astype(F32))[..., None]
    a_bar = jnp.exp(lam * dt)
    b_bar = ((a_bar - 1.0) / lam)[..., None] * lax.complex(b_re.astype(F32), b_im.astype(F32))
    cm = lax.complex(c_re.astype(F32), c_im.astype(F32))
    steps = jnp.arange(lc + 1, dtype=F32)
    apow = jnp.exp((lam * dt)[:, :, None, :] * steps[None, None, :, None])
    kern = jnp.einsum('dgop,dgjp,dgpi->dgjio', cm, apow[:, :, :lc], b_bar).real
    skip = jnp.eye(h_n, dtype=F32) * d_skip.astype(F32).reshape(g_n, 1, h_n)
    centre = kern[0, :, 0] + kern[1, :, 0] + skip
    lags = jnp.concatenate([kern[1, :, :0:-1], centre[:, None], kern[0, :, 1:]], axis=1)
    strip = lags.transpose(0, 2, 1, 3).reshape(g_n, h_n, (2 * lc - 1) * h_n)
    strip = jnp.pad(strip, ((0, 0), (0, 0), (0, h_n)))
    t_mat = _toeplitz(strip)

    def in_mat(pw, bb):
        return (pw[:, :, None, :] * bb.transpose(0, 2, 1)[:, None, :, :]).reshape(g_n, lc * h_n, p_n)

    mb_f = in_mat(apow[0, :, lc - 1::-1][:, :lc], b_bar[0])
    mb_b = in_mat(apow[1, :, :lc], b_bar[1])

    def out_mat(pw, cc):
        return (pw.transpose(0, 2, 1)[:, :, :, None] * cc.transpose(0, 2, 1)[:, :, None, :]).reshape(
            g_n, p_n, lc * h_n)

    mc_f = out_mat(apow[0, :, 1:lc + 1], cm[0])
    mc_b = out_mat(apow[1, :, lc:0:-1], cm[1])
    a_chunk = apow[:, :, lc]

    q_n = S5_PAIRS
    zeros_in = jnp.zeros((g_n, lc * h_n, p_n), F32)

    def pair_cols(m):
        m = m.reshape(q_n, 2, lc * h_n, p_n)
        z = zeros_in.reshape(q_n, 2, lc * h_n, p_n)[:, 0]
        top = jnp.concatenate([m[:, 0], z], axis=-1)
        bot = jnp.concatenate([z, m[:, 1]], axis=-1)
        return jnp.concatenate([top, bot], axis=1)

    mb_pair = jnp.concatenate([pair_cols(mb_f.real), pair_cols(mb_f.imag),
                               pair_cols(mb_b.real), pair_cols(mb_b.imag)], axis=-1)

    def pair_rows(m):
        m = m.reshape(q_n, 2, p_n, lc * h_n)
        z = jnp.zeros_like(m[:, 0])
        top = jnp.concatenate([m[:, 0], z], axis=-1)
        bot = jnp.concatenate([z, m[:, 1]], axis=-1)
        return jnp.concatenate([top, bot], axis=1)

    mc_pair = jnp.concatenate([pair_rows(mc_f.real), pair_rows(-mc_f.imag),
                               pair_rows(mc_b.real), pair_rows(-mc_b.imag)], axis=1)
    a_rows = jnp.stack([a_chunk[0].real, a_chunk[0].imag, a_chunk[1].real, a_chunk[1].imag], axis=0)
    a_rows = a_rows.reshape(4, q_n, 2 * p_n).transpose(1, 0, 2)
    a_rows = jnp.concatenate([a_rows, jnp.zeros_like(a_rows)], axis=1)
    return t_mat, mb_pair.astype(BF16), mc_pair.astype(BF16), a_rows


def _chunkify_kernel(u_ref, o_ref, *, gb):
    _, nb, nc, _ = o_ref.shape
    half = S5_CHUNK * S5_H
    per_slab = LANES // S5_H
    for b0 in range(0, nb, gb):
        slabs = ([], [])
        for t in range(S5_CHUNK):
            rows = [u_ref[b0 + b, pl.ds(t, nc, stride=S5_CHUNK), :] for b in range(gb)]
            for j, part in enumerate(_unpack_bf16_pair(rows[0] if gb == 1 else jnp.concatenate(rows, axis=0))):
                slabs[j].append(part.T)
        for j, cols in enumerate(slabs):
            for gl in range(per_slab):
                g = j * per_slab + gl
                m = jnp.concatenate([c[gl * S5_H:(gl + 1) * S5_H, :] for c in cols], axis=0)
                o_ref[g // 2, b0:b0 + gb, :, (g % 2) * half:(g % 2 + 1) * half] = (
                    m.T.astype(BF16).reshape(gb, nc, half))


def _layout_step(bsz, gb):
    return gb * LAYOUT_GROUPS if bsz % (gb * LAYOUT_GROUPS) == 0 else gb


def _chunkify(u_rows, name):
    bsz, n, _ = u_rows.shape
    nc = n // S5_CHUNK
    gb = min(bsz, max(1, LANES // nc))
    nb = _layout_step(bsz, gb)
    w = 2 * S5_CHUNK * S5_H
    return pl.pallas_call(
        functools.partial(_chunkify_kernel, gb=gb),
        grid=(bsz // nb,),
        in_specs=[pl.BlockSpec((nb, n, LANES), lambda b: (b, 0, 0))],
        out_specs=pl.BlockSpec((S5_PAIRS, nb, nc, w), lambda b: (0, b, 0, 0)),
        out_shape=jax.ShapeDtypeStruct((S5_PAIRS, bsz, nc, w), BF16),
        compiler_params=_params(("parallel",)),
        name=name,
    )(u_rows)


def _unchunkify_kernel(y_ref, o_ref):
    _, nb, nc, _ = y_ref.shape
    half = S5_CHUNK * S5_H
    per_slab = LANES // S5_H
    for b in range(nb):
        rows = []
        for g in range(S5_GROUPS):
            rows.append(y_ref[g // 2, b, :, (g % 2) * half:(g % 2 + 1) * half].astype(F32).T)
        for t in range(S5_CHUNK):
            tiles = [jnp.concatenate([r[t * S5_H:(t + 1) * S5_H, :] for r in rows[j * per_slab:(j + 1) * per_slab]],
                                     axis=0).T for j in range(2)]
            o_ref[b, pl.ds(t, nc, stride=S5_CHUNK), :] = _pack_bf16_pair(*tiles)


def _unchunkify(y_c):
    q_n, bsz, nc, w = y_c.shape
    n = nc * S5_CHUNK
    nb = _layout_step(bsz, 1)
    return pl.pallas_call(
        _unchunkify_kernel,
        grid=(bsz // nb,),
        in_specs=[pl.BlockSpec((q_n, nb, nc, w), lambda b: (0, b, 0, 0))],
        out_specs=pl.BlockSpec((nb, n, LANES), lambda b: (b, 0, 0)),
        out_shape=jax.ShapeDtypeStruct((bsz, n, LANES), U32),
        compiler_params=_params(("parallel",)),
        name="unchunkify",
    )(y_c)


def _s5_kernel(u_ref, uc_ref, t_ref, mb_ref, mc_ref, a_ref, y_ref, s_lat, s_ctx, h_scr, *, bb):
    _, bsz, n_lat, w = u_ref.shape
    n_ctx = uc_ref.shape[2]
    n_blk = w // LANES
    rb = bb * n_lat

    mb = mb_ref[0]

    def in_lat(i, carry):
        s = jnp.dot(u_ref[0, pl.ds(i * bb, bb)].reshape(rb, w), mb, preferred_element_type=F32)
        for k in range(bb):
            r = pl.multiple_of((i * bb + k) * STATE_PITCH, 8)
            for blk in range(n_blk):
                s_lat[blk, pl.ds(r, n_lat), :] = s[k * n_lat:(k + 1) * n_lat, blk * LANES:(blk + 1) * LANES]
        return carry

    lax.fori_loop(0, bsz // bb, in_lat, 0)
    sc = jnp.dot(uc_ref[0].reshape(bsz * n_ctx, w), mb, preferred_element_type=F32)
    for blk in range(n_blk):
        s_ctx[blk] = sc[:, blk * LANES:(blk + 1) * LANES]

    a_fr, a_fi, a_br, a_bi = (a_ref[0, k:k + 1, :] for k in range(4))

    def step(h, a_r, a_i, s_r, s_i):
        h_r, h_i = h
        return a_r * h_r - a_i * h_i + s_r, a_r * h_i + a_i * h_r + s_i

    def ctx_rows(blk, c):
        return s_ctx[blk, pl.ds(c, bsz, stride=n_ctx), :]

    def lat_rows(ref, blk, c):
        return ref.at[blk, pl.ds(c, bsz, stride=STATE_PITCH), :]

    def ctx_step(k, carry):
        hf, hb = carry
        kb = n_ctx - 1 - k
        hf = step(hf, a_fr, a_fi, ctx_rows(0, k), ctx_rows(1, k))
        hb = step(hb, a_br, a_bi, ctx_rows(2, kb), ctx_rows(3, kb))
        return hf, hb

    zero = jnp.zeros((bsz, LANES), F32)
    carry = lax.fori_loop(0, n_ctx, ctx_step, ((zero, zero), (zero, zero)))

    def lat_step(k, carry):
        hf, hb = carry
        kb = n_lat - 1 - k
        lat_rows(h_scr, 0, k)[...] = hf[0]
        lat_rows(h_scr, 1, k)[...] = hf[1]
        lat_rows(h_scr, 2, kb)[...] = hb[0]
        lat_rows(h_scr, 3, kb)[...] = hb[1]
        hf = step(hf, a_fr, a_fi, lat_rows(s_lat, 0, k)[...], lat_rows(s_lat, 1, k)[...])
        hb = step(hb, a_br, a_bi, lat_rows(s_lat, 2, kb)[...], lat_rows(s_lat, 3, kb)[...])
        return hf, hb

    lax.fori_loop(0, n_lat, lat_step, carry)

    t0 = t_ref[0]
    t1 = t_ref[1]
    mc = mc_ref[0]
    half = S5_CHUNK * S5_H

    def out_lat(i, carry):
        u = u_ref[0, pl.ds(i * bb, bb)].reshape(rb, w)
        h_rows = []
        for k in range(bb):
            r = pl.multiple_of((i * bb + k) * STATE_PITCH, 8)
            h_rows.append(jnp.concatenate([h_scr[blk, pl.ds(r, n_lat), :] for blk in range(n_blk)], axis=1))
        h = jnp.concatenate(h_rows, axis=0).astype(BF16)
        inter = jnp.dot(h, mc, preferred_element_type=F32)
        y0 = jnp.dot(u[:, :half], t0, preferred_element_type=F32) + inter[:, :half]
        y1 = jnp.dot(u[:, half:], t1, preferred_element_type=F32) + inter[:, half:]
        y = jnp.concatenate([y0, y1], axis=1).astype(BF16)
        y_ref[0, pl.ds(i * bb, bb)] = y.reshape(bb, n_lat, w)
        return carry

    lax.fori_loop(0, bsz // bb, out_lat, 0)


def _s5_scan(u_c, uc_c, t_mat, mb_pair, mc_pair, a_rows):
    q_n, bsz, n_lat, w = u_c.shape
    n_ctx = uc_c.shape[2]
    assert n_lat + 8 == STATE_PITCH
    bb = min(8, bsz)
    n_blk = w // LANES
    return pl.pallas_call(
        functools.partial(_s5_kernel, bb=bb),
        grid=(q_n,),
        in_specs=[pl.BlockSpec((1, bsz, n_lat, w), lambda q: (q, 0, 0, 0)),
                  pl.BlockSpec((1, bsz, n_ctx, w), lambda q: (q, 0, 0, 0)),
                  pl.BlockSpec((2, w // 2, w // 2), lambda q: (q, 0, 0)),
                  pl.BlockSpec((1, w, w), lambda q: (q, 0, 0)),
                  pl.BlockSpec((1, w, w), lambda q: (q, 0, 0)),
                  pl.BlockSpec((1, 8, LANES), lambda q: (q, 0, 0))],
        out_specs=pl.BlockSpec((1, bsz, n_lat, w), lambda q: (q, 0, 0, 0)),
        out_shape=jax.ShapeDtypeStruct((q_n, bsz, n_lat, w), BF16),
        scratch_shapes=[pltpu.VMEM((n_blk, bsz * STATE_PITCH, LANES), F32),
                        pltpu.VMEM((n_blk, bsz * n_ctx, LANES), F32),
                        pltpu.VMEM((n_blk, bsz * STATE_PITCH, LANES), F32)],
        compiler_params=_params(("parallel",)),
        name="s5_scan",
    )(u_c, uc_c, t_mat, mb_pair, mc_pair, a_rows)


def _first_max(rows):
    best = rows[0]
    for r in rows[1:]:
        best = jnp.maximum(best, r)
    idx = jnp.full(best.shape, float(len(rows) - 1), F32)
    for k in range(len(rows) - 2, -1, -1):
        idx = jnp.where(rows[k] == best, float(k), idx)
    return best, idx


def _route_rows(lg):
    g_rows = [lg[k:k + 1] for k in range(N_GROUPS)]
    g_max, g_idx = _first_max(g_rows)
    g_sum = sum(jnp.exp(r - g_max) for r in g_rows)
    g_p = 1.0 / g_sum
    e_rows = []
    for j in range(EXPERTS_PER_GROUP):
        r = lg[N_GROUPS + (N_GROUPS - 1) * EXPERTS_PER_GROUP + j:][:1]
        for g in range(N_GROUPS - 2, -1, -1):
            k = N_GROUPS + g * EXPERTS_PER_GROUP + j
            r = jnp.where(g_idx == float(g), lg[k:k + 1], r)
        e_rows.append(r)
    v1, i1 = _first_max(e_rows)
    rest = [jnp.where(i1 == float(j), -jnp.inf, e_rows[j]) for j in range(EXPERTS_PER_GROUP)]
    v2, i2 = _first_max(rest)
    e21 = jnp.exp(v2 - v1)
    w1 = g_p / (1.0 + e21)
    w2 = w1 * e21
    lo = jnp.minimum(i1, i2)
    hi = jnp.maximum(i1, i2)
    base = jnp.where(lo == 0.0, 0.0, jnp.where(lo == 1.0, 3.0, 5.0))
    bucket = g_idx * float(PAIRS_PER_GROUP) + base + hi - lo - 1.0
    first_is_lo = i1 < i2
    return bucket, jnp.where(first_is_lo, w1, w2), jnp.where(first_is_lo, w2, w1)


def _mix_kernel(x_ref, xup_ref, xdn_ref, y_ref, mod_ref, g1_ref, g2_ref, cw_ref, win_ref, wglu_ref, bglu_ref,
                wout_ref, wr_ref, br_ref, *rest):
    n_cast = (len(rest) - 4) // 2
    x1_ref, h2p_ref, route_ref, counts_ref = rest[n_cast:n_cast + 4]
    for src, dst in zip(rest[:n_cast], rest[n_cast + 4:]):
        dst[...] = src[...].astype(BF16)
    i = pl.program_id(1)
    tm = x_ref.shape[1]
    d = x_ref.shape[2]
    sub = min(MIX_SUB, tm)
    n_sub = tm // sub
    cw, rw = CONV_WIDTH, CONV_ROW_WIDTH

    def hidden(xv):
        return _modulated_norm(xv, g1_ref[...], mod_ref[0, 0:1, :], mod_ref[0, 1:2, :]).astype(BF16)

    def halo(h):
        zh = jnp.dot(h, win_ref[:, 2 * cw:3 * cw], preferred_element_type=F32)
        return zh[:, 0:rw] * zh[:, rw:cw]

    @pl.when(jnp.logical_and(pl.program_id(0) == 0, i == 0))
    def _():
        counts_ref[...] = jnp.zeros_like(counts_ref)

    def sub_tile(s):
        r0 = s * sub
        xv = x_ref[0, r0:r0 + sub, :]
        hx = jnp.concatenate([hidden(x_ref[0, r0 + c0:r0 + c0 + TAIL_ROWS, :]) for c0 in range(0, sub, TAIL_ROWS)],
                             axis=0)
        g = jax.nn.gelu(jnp.concatenate(_unpack_bf16_pair(y_ref[0, r0:r0 + sub, :]), axis=1))
        hx_up = hidden(xup_ref[0]) if s == 0 else None
        hx_dn = hidden(xdn_ref[0]) if s == n_sub - 1 else None
        yield
        z_r = jnp.dot(hx, win_ref[:, cw:2 * cw], preferred_element_type=F32)
        cr = z_r[:, 0:rw] * z_r[:, rw:cw]
        z_c = jnp.dot(hx, win_ref[:, 2 * cw:3 * cw], preferred_element_type=F32)
        cc = z_c[:, 0:rw] * z_c[:, rw:cw]
        col_products[s] = cc
        yield
        bg = jnp.dot(hx, win_ref[:, 0:cw], preferred_element_type=F32)
        if s == 0:
            up_halo = jnp.where(i == 0, 0.0, halo(hx_up))
        if s == n_sub - 1:
            dn_halo = jnp.where(i == pl.num_programs(1) - 1, 0.0, halo(hx_dn))
        glu = g * jax.nn.sigmoid(jnp.dot(g.astype(BF16), wglu_ref[...], preferred_element_type=F32) + bglu_ref[...])
        yield
        row = lax.broadcasted_iota(jnp.int32, (sub, 1), 0)
        col_in_row = row % GRID_W
        if s > 0:
            up_halo = col_products[s - 1][sub - GRID_W:]
        if s < n_sub - 1:
            dn_halo = col_products[s + 1][:GRID_W]
        y_rows, y_cols = [], []
        for k0 in range(0, rw, LANES):
            ks = slice(k0, k0 + LANES)
            cr_k = cr[:, ks]
            left = jnp.where(col_in_row == 0, 0.0, pltpu.roll(cr_k, 1, axis=0))
            right = jnp.where(col_in_row == GRID_W - 1, 0.0, pltpu.roll(cr_k, sub - 1, axis=0))
            w_r = cw_ref[:, k0:k0 + LANES]
            y_rows.append((bg[:, ks] * (left * w_r[0:1] + cr_k * w_r[1:2] + right * w_r[2:3])).astype(BF16))
            cc_k = cc[:, ks]
            up = jnp.concatenate([up_halo[:, ks], cc_k[:sub - GRID_W]], axis=0)
            dn = jnp.concatenate([cc_k[GRID_W:], dn_halo[:, ks]], axis=0)
            w_c = cw_ref[:, rw + k0:rw + k0 + LANES]
            y_cols.append((bg[:, rw + k0:rw + k0 + LANES]
                           * (up * w_c[0:1] + cc_k * w_c[1:2] + dn * w_c[2:3])).astype(BF16))
        mixed = jnp.concatenate([glu.astype(BF16)] + y_rows + y_cols, axis=1)
        yield
        yx = jnp.dot(mixed, wout_ref[...], preferred_element_type=F32)
        yield
        for c0 in range(0, sub, TAIL_ROWS):
            rs = r0 + c0
            x1 = xv[c0:c0 + TAIL_ROWS] + mod_ref[0, 2:3, :] * yx[c0:c0 + TAIL_ROWS]
            x1_ref[0, rs:rs + TAIL_ROWS, :] = x1
            h2 = _modulated_norm(x1, g2_ref[...], mod_ref[0, 3:4, :], mod_ref[0, 4:5, :])
            h2b = h2.astype(BF16)
            lg2 = lax.dot_general(wr_ref[...], h2b, (((1,), (1,)), ((), ())), preferred_element_type=F32)
            lg = lg2[:ROUTER_ROWS] + lg2[ROUTER_ROWS:] + br_ref[:, 0:TAIL_ROWS]
            bucket, w_a, w_b = _route_rows(lg)
            r8 = lax.broadcasted_iota(jnp.int32, (8, TAIL_ROWS), 0)
            route_ref[:, rs:rs + TAIL_ROWS] = jnp.where(
                r8 == 0, bucket, jnp.where(r8 == 1, w_a, jnp.where(r8 == 2, w_b, 0.0)))
            rl = lax.broadcasted_iota(jnp.int32, (ROW_EXTRA, TAIL_ROWS), 0)
            gates_t = jnp.where(rl == 0, w_a, jnp.where(rl == 1, w_b, 0.0))
            h2p_ref[0, rs:rs + TAIL_ROWS, 0:d // 2] = _pack_bf16_pair(h2[:, :d // 2], h2[:, d // 2:])
            h2p_ref[0, rs:rs + TAIL_ROWS, d // 2:] = lax.bitcast_convert_type(gates_t.T, U32)
            rb = lax.broadcasted_iota(jnp.int32, (BUCKET_ROWS, TAIL_ROWS), 0).astype(F32)
            counts_ref[...] += jnp.sum(jnp.where(rb == bucket, 1.0, 0.0), axis=-1, keepdims=True)
        yield

    n_stage = 6
    col_products = {}
    tiles = [sub_tile(s) for s in range(n_sub)]
    for step in range(n_sub + n_stage - 1):
        for s in reversed(range(n_sub)):
            if 0 <= step - s < n_stage:
                next(tiles[s])


def _mix(x, y_s5, mods, norm1_g, norm2_g, conv_w, w_conv, w_glu, b_glu, w_out, w_router_t, b_router_t, tm, b0, nb,
         to_cast):
    _, n, d = x.shape
    nt = n // tm
    n_steps = nb * nt
    cast_specs = []
    for w in to_cast:
        per = -(-w.shape[0] // n_steps)
        while w.shape[0] % per:
            per += 1
        n_blk = w.shape[0] // per
        cast_specs.append(pl.BlockSpec((per,) + w.shape[1:],
                                       lambda b, i, n_blk=n_blk: ((b * nt + i) * n_blk // n_steps, 0, 0)))
    halo_blocks = n // GRID_W
    per_tile = tm // GRID_W
    tok_out = lambda w: pl.BlockSpec((1, tm, w), lambda b, i: (b, i, 0))
    full = lambda a: pl.BlockSpec(a.shape, lambda b, i: (0,) * a.ndim)
    args = (x, x, x, y_s5, mods, norm1_g.reshape(1, d), norm2_g.reshape(1, d), conv_w, w_conv, w_glu,
            b_glu.reshape(1, -1), w_out, w_router_t, b_router_t)
    in_specs = [pl.BlockSpec((1, tm, d), lambda b, i: (b + b0, i, 0)),
                pl.BlockSpec((1, GRID_W, d), lambda b, i: (b + b0, jnp.maximum(i * per_tile - 1, 0), 0)),
                pl.BlockSpec((1, GRID_W, d),
                             lambda b, i: (b + b0, jnp.minimum((i + 1) * per_tile, halo_blocks - 1), 0)),
                pl.BlockSpec((1, tm, LANES), lambda b, i: (b + b0, i, 0)),
                pl.BlockSpec((1, N_MOD, d), lambda b, i: (b + b0, 0, 0))] + [full(a) for a in args[5:]]
    return pl.pallas_call(
        _mix_kernel,
        grid=(nb, nt),
        in_specs=in_specs + cast_specs,
        out_specs=[tok_out(d), tok_out(d // 2 + ROW_EXTRA),
                   pl.BlockSpec((8, tm), lambda b, i: (0, b * nt + i)),
                   pl.BlockSpec((BUCKET_ROWS, LANES), lambda b, i: (0, 0))] + cast_specs,
        out_shape=[jax.ShapeDtypeStruct((nb, n, d), F32),
                   jax.ShapeDtypeStruct((nb, n, d // 2 + ROW_EXTRA), U32),
                   jax.ShapeDtypeStruct((8, nb * n), F32),
                   jax.ShapeDtypeStruct((BUCKET_ROWS, LANES), F32)]
        + [jax.ShapeDtypeStruct(w.shape, BF16) for w in to_cast],
        compiler_params=_params(("arbitrary", "arbitrary")),
        name="mix",
    )(*args, *to_cast)


def _rank_kernel(route_ref, offs_ref, dest_ref, run_ref):
    tr = route_ref.shape[1]
    n_blk = tr // LANES

    @pl.when(pl.program_id(0) == 0)
    def _():
        run_ref[...] = jnp.zeros_like(run_ref)

    bucket = route_ref[0:1, :]
    rb = lax.broadcasted_iota(jnp.int32, (BUCKET_ROWS, tr), 0).astype(F32)
    onehot = jnp.where(rb == bucket, 1.0, 0.0)
    blocks = [onehot[:, k * LANES:(k + 1) * LANES] for k in range(n_blk)]
    s_idx = lax.broadcasted_iota(jnp.int32, (LANES, LANES), 0)
    t_idx = lax.broadcasted_iota(jnp.int32, (LANES, LANES), 1)
    tri = jnp.where(s_idx <= t_idx, 1.0, 0.0).astype(BF16)
    prefix = jnp.dot(jnp.concatenate(blocks, axis=0).astype(BF16), tri, preferred_element_type=F32)
    start = run_ref[:, 0:1] + offs_ref[:, 0:1]
    carry = start
    for k in range(n_blk):
        pk = prefix[k * BUCKET_ROWS:(k + 1) * BUCKET_ROWS]
        dest = jnp.sum(blocks[k] * (pk - 1.0 + carry), axis=0, keepdims=True)
        dest_ref[:, k * LANES:(k + 1) * LANES] = dest.astype(jnp.int32)
        carry = carry + pk[:, LANES - 1:LANES]
    run_ref[...] += carry - start


def _rank(route_t, offs_rows, tr):
    n = route_t.shape[1]
    return pl.pallas_call(
        _rank_kernel,
        grid=(n // tr,),
        in_specs=[pl.BlockSpec((8, tr), lambda i: (0, i)),
                  pl.BlockSpec((BUCKET_ROWS, LANES), lambda i: (0, 0))],
        out_specs=pl.BlockSpec((1, tr), lambda i: (0, i)),
        out_shape=jax.ShapeDtypeStruct((1, n), jnp.int32),
        scratch_shapes=[pltpu.VMEM((BUCKET_ROWS, LANES), F32)],
        compiler_params=_params(("arbitrary",)),
        name="rank",
    )(route_t, offs_rows)


def _sc_mesh():
    return plsc.VectorSubcoreMesh(core_axis_name="core", subcore_axis_name="subcore")


def _scatter_rows(src, dest, n_out):
    n, w = src.shape
    sub = SC_INDEX_TILE // SC_ROWS

    @functools.partial(pl.kernel, out_type=jax.ShapeDtypeStruct((n_out, w), src.dtype), mesh=_sc_mesh(),
                       scratch_types=[], name="scatter_rows")
    def scatter(x_hbm, i_hbm, o_hbm):
        def body(x_vmem, i_vmem):
            j = pl.program_id(1)
            pltpu.sync_copy(x_vmem, o_hbm.at[i_vmem.at[0, pl.ds(j * SC_ROWS, SC_ROWS)]])

        pltpu.emit_pipeline(
            body,
            grid=(n // SC_INDEX_TILE, sub),
            in_specs=[pl.BlockSpec((SC_ROWS, w), lambda i, j: (i * sub + j, 0)),
                      pl.BlockSpec((1, SC_INDEX_TILE), lambda i, j: (0, i))],
            out_specs=[],
            core_axis_name=("core", "subcore"),
            dimension_semantics=(pltpu.PARALLEL, pltpu.ARBITRARY),
        )(x_hbm, i_hbm)

    return scatter(src, dest)


def _gather_rows(src, idx):
    n = idx.shape[1]
    w = src.shape[1]
    sub = SC_INDEX_TILE // SC_ROWS

    @functools.partial(pl.kernel, out_type=jax.ShapeDtypeStruct((n, w), src.dtype), mesh=_sc_mesh(),
                       scratch_types=[], name="gather_rows")
    def gather(x_hbm, i_hbm, o_hbm):
        def body(i_vmem, o_vmem):
            j = pl.program_id(1)
            pltpu.sync_copy(x_hbm.at[i_vmem.at[0, pl.ds(j * SC_ROWS, SC_ROWS)]], o_vmem)

        pltpu.emit_pipeline(
            body,
            grid=(n // SC_INDEX_TILE, sub),
            in_specs=[pl.BlockSpec((1, SC_INDEX_TILE), lambda i, j: (0, i))],
            out_specs=[pl.BlockSpec((SC_ROWS, w), lambda i, j: (i * sub + j, 0))],
            core_axis_name=("core", "subcore"),
            dimension_semantics=(pltpu.PARALLEL, pltpu.ARBITRARY),
        )(i_hbm, o_hbm)

    return gather(src, idx)


def _moe_kernel(ea_ref, eb_ref, valid_ref, blk_ref, rows_ref, w1a_ref, w3a_ref, w2a_ref, w1b_ref, w3b_ref, w2b_ref,
                o_ref):
    j = pl.program_id(0)
    half = rows_ref.shape[1] - ROW_EXTRA

    @pl.when(valid_ref[j] != 0)
    def _():
        ha, hb = _unpack_bf16_pair(rows_ref[:, 0:half])
        ha = ha.astype(BF16)
        hb = hb.astype(BF16)
        gates = lax.bitcast_convert_type(rows_ref[:, half:], F32)

        def expert(w1_ref, w3_ref, w2_ref, gate):
            def up(w_ref):
                return (jnp.dot(ha, w_ref[0, 0:half, :], preferred_element_type=F32)
                        + jnp.dot(hb, w_ref[0, half:, :], preferred_element_type=F32))

            a1 = up(w1_ref)
            he = (a1 * jax.nn.sigmoid(a1)) * up(w3_ref) * gate
            return jnp.dot(he.astype(BF16), w2_ref[0], preferred_element_type=F32)

        y = (expert(w1a_ref, w3a_ref, w2a_ref, gates[:, 0:1]) + expert(w1b_ref, w3b_ref, w2b_ref, gates[:, 1:2]))
        o_ref[...] = _pack_bf16_pair(y[:, :half], y[:, half:])


def _moe_grouped(rows, tile_ea, tile_eb, tile_valid, tile_blk, w1, w3, w2, tmm):
    r, w = rows.shape
    _, de, d = w2.shape
    up_a = pl.BlockSpec((1, d, de), lambda j, ea, eb, va, bk: (ea[j], 0, 0))
    up_b = pl.BlockSpec((1, d, de), lambda j, ea, eb, va, bk: (eb[j], 0, 0))
    grid_spec = pltpu.PrefetchScalarGridSpec(
        num_scalar_prefetch=4,
        grid=(r // tmm,),
        in_specs=[pl.BlockSpec((tmm, w), lambda j, ea, eb, va, bk: (bk[j], 0)),
                  up_a, up_a, pl.BlockSpec((1, de, d), lambda j, ea, eb, va, bk: (ea[j], 0, 0)),
                  up_b, up_b, pl.BlockSpec((1, de, d), lambda j, ea, eb, va, bk: (eb[j], 0, 0))],
        out_specs=pl.BlockSpec((tmm, d // 2), lambda j, ea, eb, va, bk: (bk[j], 0)),
    )
    return pl.pallas_call(
        _moe_kernel,
        grid_spec=grid_spec,
        out_shape=jax.ShapeDtypeStruct((r, d // 2), U32),
        compiler_params=_params(("arbitrary",)),
        name="moe",
    )(tile_ea, tile_eb, tile_valid, tile_blk, rows, w1, w3, w2, w1, w3, w2)


def _final_kernel(x1_ref, moe_ref, mod_ref, fg_ref, *rest):
    o_ref = rest[-1]
    ya, yb = _unpack_bf16_pair(moe_ref[0])
    half = ya.shape[1]
    gate = mod_ref[0, 5:6, :]
    xa = x1_ref[0, :, 0:half] + gate[:, 0:half] * ya
    xb = x1_ref[0, :, half:] + gate[:, half:] * yb
    ms = (jnp.sum(xa * xa, axis=-1, keepdims=True) + jnp.sum(xb * xb, axis=-1, keepdims=True)) / (2 * half)
    inv = lax.rsqrt(ms + RMS_EPS)
    o_ref[0, :, 0:half] = xa * inv * fg_ref[:, 0:half]
    o_ref[0, :, half:] = xb * inv * fg_ref[:, half:]


def _final(x1, moe_tok, mods, final_g, tm, b0, bsz, out_prev):
    nb, n, d = x1.shape
    tok = lambda w: pl.BlockSpec((1, tm, w), lambda b, i: (b, i, 0))
    args = [x1, moe_tok, mods, final_g.reshape(1, d)]
    in_specs = [tok(d), tok(d // 2), pl.BlockSpec((1, N_MOD, d), lambda b, i: (b + b0, 0, 0)),
                pl.BlockSpec((1, d), lambda b, i: (0, 0))]
    aliases = {}
    if out_prev is not None:
        args.append(out_prev)
        in_specs.append(pl.BlockSpec(memory_space=pl.ANY))
        aliases = {len(args) - 1: 0}
    return pl.pallas_call(
        _final_kernel,
        grid=(nb, n // tm),
        in_specs=in_specs,
        out_specs=pl.BlockSpec((1, tm, d), lambda b, i: (b + b0, i, 0)),
        out_shape=jax.ShapeDtypeStruct((bsz, n, d), F32),
        input_output_aliases=aliases,
        compiler_params=_params(("parallel", "parallel")),
        name="final",
    )(*args)


def _tile_plan(counts, tmm, n_tiles):
    tiles = (counts + (tmm - 1)) // tmm
    tile_end = jnp.cumsum(tiles)
    offs = (tile_end - tiles) * tmm
    n_valid = tile_end[-1]
    j = jnp.arange(n_tiles, dtype=jnp.int32)
    bucket = jnp.sum((tile_end[None, :] <= jnp.minimum(j, n_valid - 1)[:, None]).astype(jnp.int32), axis=1)
    pair_lo = jnp.array([0, 0, 0, 1, 1, 2], jnp.int32)
    pair_hi = jnp.array([1, 2, 3, 2, 3, 3], jnp.int32)
    group = bucket // PAIRS_PER_GROUP
    pair = bucket % PAIRS_PER_GROUP
    tile_ea = group * EXPERTS_PER_GROUP + pair_lo[pair]
    tile_eb = group * EXPERTS_PER_GROUP + pair_hi[pair]
    return offs, tile_ea, tile_eb, (j < n_valid).astype(jnp.int32), jnp.minimum(j, n_valid - 1)


def kernel(x, c, ctx, c_ctx, w_mod, b_mod, norm1_g, norm2_g, w_in, s5_lambda_re, s5_lambda_im, s5_log_dt,
           s5_b_re, s5_b_im, s5_c_re, s5_c_im, s5_d, w_glu, b_glu, conv_w, w_out, router_group_w,
           router_group_b, router_expert_w, router_expert_b, expert_w1, expert_w3, expert_w2, final_g):
    assert w_mod.shape[0] == 1, "single-layer kernel"
    bsz, n_tok, d = x.shape
    l = 0
    tm = min(TOKEN_TILE, n_tok)

    n_cond = bsz + 1
    pad = (-n_cond) % 8
    cond = jnp.concatenate([c, c_ctx[None, :], jnp.zeros((pad, d), F32)], axis=0)
    m = _mod_rows(cond, w_mod[l], b_mod[l])
    mx = m[:bsz].reshape(bsz, N_MOD, d)
    mc = m[bsz:bsz + 1].reshape(1, N_MOD, d)

    w_in_b = w_in[l].astype(BF16)
    w_s5 = w_in_b[:, :S5_WIDTH]
    o_c = S5_WIDTH + CONV_WIDTH
    o_v = S5_WIDTH + 2 * CONV_WIDTH
    w_conv = jnp.concatenate(
        [w_in_b[:, S5_WIDTH:o_c], w_in_b[:, o_c:o_c + CONV_ROW_WIDTH], w_in_b[:, o_v:o_v + CONV_ROW_WIDTH],
         w_in_b[:, o_c + CONV_ROW_WIDTH:o_v], w_in_b[:, o_v + CONV_ROW_WIDTH:]], axis=1)
    u = _inproj(x, mx, True, norm1_g[l], w_s5, INPROJ_TILE, "inproj")
    uc = _inproj(ctx, mc, False, norm1_g[l], w_s5, INPROJ_TILE, "inproj_ctx")

    t_mat, mb_pair, mc_pair, a_rows = _s5_matrices(
        s5_lambda_re[l], s5_lambda_im[l], s5_log_dt[l], s5_b_re[l], s5_b_im[l], s5_c_re[l], s5_c_im[l], s5_d[l])
    y_c = _s5_scan(_chunkify(u, "chunkify"), _chunkify(uc, "chunkify_ctx"), t_mat, mb_pair, mc_pair, a_rows)
    y_s5 = _unchunkify(y_c)

    n_logits = N_GROUPS + N_EXPERTS
    w_router = jnp.concatenate(
        [router_group_w[l], router_expert_w[l], jnp.zeros((d, ROUTER_ROWS - n_logits), F32)], axis=1).T
    w_router_hi = w_router.astype(BF16)
    w_router_lo = (w_router - w_router_hi.astype(F32)).astype(BF16)
    w_router_t = jnp.concatenate([w_router_hi, w_router_lo], axis=0)
    b_router = jnp.concatenate([router_group_b[l], router_expert_b[l], jnp.zeros((ROUTER_ROWS - n_logits,), F32)])
    tm_mix = min(MIX_TILE, n_tok)
    b_router_t = jnp.broadcast_to(b_router[:, None], (ROUTER_ROWS, min(MIX_SUB, tm_mix)))

    first = bsz * MOE_FIRST_SHARE[0] // MOE_FIRST_SHARE[1]
    part_sizes = [first, bsz - first] if 0 < first < bsz else [bsz]
    n_buckets = N_GROUPS * PAIRS_PER_GROUP
    w_glu_b = w_glu[l].astype(BF16)
    w_out_b = w_out[l].astype(BF16)
    experts_f32 = (expert_w1[l], expert_w3[l], expert_w2[l])
    cast_plan = [experts_f32] if len(part_sizes) == 1 else [experts_f32[:2], experts_f32[2:]]
    w_experts = []
    staged = []
    b0 = 0
    for nb, to_cast in zip(part_sizes, cast_plan):
        n_part = nb * n_tok
        n_rows = n_part + n_buckets * MOE_TILE
        x1, h2p, route_t, counts, *w_cast = _mix(x, y_s5, mx, norm1_g[l], norm2_g[l], conv_w[l], w_conv, w_glu_b,
                                                 b_glu[l], w_out_b, w_router_t, b_router_t, tm_mix, b0, nb, to_cast)
        w_experts += w_cast
        offs, *tiles = _tile_plan(counts[:n_buckets, 0].astype(jnp.int32), MOE_TILE, n_rows // MOE_TILE)
        offs_rows = jnp.zeros((BUCKET_ROWS,), F32).at[:n_buckets].set(offs.astype(F32))
        dest = _rank(route_t, jnp.broadcast_to(offs_rows[:, None], (BUCKET_ROWS, LANES)), min(RANK_TILE, n_part))
        rows = _scatter_rows(h2p.reshape(n_part, d // 2 + ROW_EXTRA), dest, n_rows)
        staged.append((b0, x1, rows, dest, tiles))
        b0 += nb
    out = None
    for b0, x1, rows, dest, tiles in staged:
        y_rows = _moe_grouped(rows, *tiles, *w_experts, MOE_TILE)
        moe_tok = _gather_rows(y_rows, dest).reshape(x1.shape[0], n_tok, d // 2)
        out = _final(x1, moe_tok, mx, final_g, tm, b0, bsz, out)
    return out
```
